```python
import jax, jax.numpy as jnp
from jax import lax
import numpy as np

D_MODEL = 2048
BATCH = 4
SEQ = 4096
DEPTH = 1

MIX_WIDTH = D_MODEL
POOL_WINDOWS = (2, 4, 8, 16)
N_POOL_GROUPS = len(POOL_WINDOWS)
POOL_WIDTH = MIX_WIDTH // 2
POOL_GROUP_DIM = POOL_WIDTH // N_POOL_GROUPS
FOURIER_WIDTH = MIX_WIDTH - POOL_WIDTH
N_FOURIER_HEADS = 4
FOURIER_HEAD_DIM = FOURIER_WIDTH // N_FOURIER_HEADS
N_EXPERT_GROUPS = 4
EXPERTS_PER_GROUP = 4
N_EXPERTS = N_EXPERT_GROUPS * EXPERTS_PER_GROUP
TOP_K_IN_GROUP = 2
EXPERT_FF = D_MODEL // 4
RMS_EPS = 1e-6

kernel_name = "hybrid_pool_fourier_hmoe_block"


def _rmsnorm(x, g):
    xf = x.astype(jnp.float32)
    y = xf * lax.rsqrt(jnp.mean(xf * xf, axis=-1, keepdims=True) + RMS_EPS)
    return (y * g.astype(jnp.float32)).astype(x.dtype)


def _multiscale_pool_minus_token(u):
    s_len = u.shape[1]
    uf = u.astype(jnp.float32)
    c = jnp.cumsum(uf, axis=1)
    c = jnp.pad(c, ((0, 0), (1, 0), (0, 0), (0, 0)))
    t = np.arange(s_len)[:, None]
    k = np.array(POOL_WINDOWS)[None, :]
    lo = np.clip(t - (k - 1) // 2, 0, s_len)
    hi = np.clip(t + k // 2 + 1, 0, s_len)
    gid = np.arange(len(POOL_WINDOWS))[None, :]
    window_sum = c[:, hi, gid, :] - c[:, lo, gid, :]
    count = jnp.asarray((hi - lo).astype(np.float32))[None, :, :, None]
    return window_sum / count - uf


def _fourier_real(u):
    f = jnp.fft.fft2(u.astype(jnp.float32), axes=(1, 3), norm="ortho")
    return jnp.real(f)


def _hier_moe(h, w_group_router, b_group_router, w_expert_router, b_expert_router,
              w_gate, w_up, w_down):
    b, s, d = h.shape
    hf = h.reshape(b * s, d)
    g_logits = jnp.dot(hf, w_group_router).astype(jnp.float32) + b_group_router.astype(jnp.float32)
    g_probs = jax.nn.softmax(g_logits, axis=-1)
    p_g, g_idx = lax.top_k(g_probs, 1)
    e_logits = jnp.dot(hf, w_expert_router).astype(jnp.float32) + b_expert_router.astype(jnp.float32)
    e_logits = e_logits.reshape(-1, N_EXPERT_GROUPS, EXPERTS_PER_GROUP)
    e_sel = jnp.take_along_axis(e_logits, g_idx[:, :, None], axis=1)[:, 0]
    top_v, top_i = lax.top_k(e_sel, TOP_K_IN_GROUP)
    w_pair = p_g * jax.nn.softmax(top_v, axis=-1)
    expert_ids = g_idx * EXPERTS_PER_GROUP + top_i
    combine = jnp.sum(jax.nn.one_hot(expert_ids, N_EXPERTS, dtype=jnp.float32)
                      * w_pair[..., None], axis=1).astype(h.dtype)
    a = jnp.einsum('nd,edf->nef', hf, w_gate)
    v = jnp.einsum('nd,edf->nef', hf, w_up)
    act = jax.nn.silu(a) * v * combine[:, :, None]
    out = jnp.einsum('nef,efd->nd', act, w_down)
    return out.reshape(b, s, d)


def setup_inputs(seed: int = 0) -> dict:
    key = jax.random.key(seed)
    ks = jax.random.split(key, 20)
    f32 = jnp.float32
    nrm = lambda k, shp, fan: jax.random.normal(k, shp, f32) * (fan ** -0.5)
    return {
        "x": jax.random.normal(ks[0], (BATCH, SEQ, D_MODEL), f32),
        "g_mix": 1.0 + 0.02 * jax.random.normal(ks[1], (D_MODEL,), f32),
        "w_in": nrm(ks[2], (D_MODEL, MIX_WIDTH), D_MODEL),
        "w_pool": nrm(ks[3], (N_POOL_GROUPS, POOL_GROUP_DIM, POOL_GROUP_DIM), POOL_GROUP_DIM),
        "pool_scale": 1.0 + 0.02 * jax.random.normal(ks[4], (POOL_WIDTH,), f32),
        "w_fourier": nrm(ks[5], (N_FOURIER_HEADS, FOURIER_HEAD_DIM, FOURIER_HEAD_DIM), FOURIER_HEAD_DIM),
        "w_out": nrm(ks[6], (MIX_WIDTH, D_MODEL), MIX_WIDTH),
        "g_ffn": 1.0 + 0.02 * jax.random.normal(ks[7], (D_MODEL,), f32),
        "w_group_router": nrm(ks[8], (D_MODEL, N_EXPERT_GROUPS), D_MODEL),
        "b_group_router": 0.01 * jax.random.normal(ks[9], (N_EXPERT_GROUPS,), f32),
        "w_expert_router": nrm(ks[10], (D_MODEL, N_EXPERTS), D_MODEL),
        "b_expert_router": 0.01 * jax.random.normal(ks[11], (N_EXPERTS,), f32),
        "w_gate": nrm(ks[12], (N_EXPERTS, D_MODEL, EXPERT_FF), D_MODEL),
        "w_up": nrm(ks[13], (N_EXPERTS, D_MODEL, EXPERT_FF), D_MODEL),
        "w_down": nrm(ks[14], (N_EXPERTS, EXPERT_FF, D_MODEL), EXPERT_FF),
        "g_final": 1.0 + 0.02 * jax.random.normal(ks[15], (D_MODEL,), f32),
    }


def reference(x, g_mix, w_in, w_pool, pool_scale, w_fourier, w_out, g_ffn,
              w_group_router, b_group_router, w_expert_router, b_expert_router,
              w_gate, w_up, w_down, g_final):
    b, s, _ = x.shape
    for _layer in range(DEPTH):
        h = _rmsnorm(x, g_mix)
        u = jnp.dot(h, w_in)
        u_pool = u[..., :POOL_WIDTH].reshape(b, s, N_POOL_GROUPS, POOL_GROUP_DIM)
        u_four = u[..., POOL_WIDTH:].reshape(b, s, N_FOURIER_HEADS, FOURIER_HEAD_DIM)
        p = _multiscale_pool_minus_token(u_pool).astype(x.dtype)
        p = jnp.einsum('bsgc,gce->bsge', p, w_pool).reshape(b, s, POOL_WIDTH) * pool_scale
        f = _fourier_real(u_four).astype(x.dtype)
        f = jnp.einsum('bshd,hde->bshe', f, w_fourier).reshape(b, s, FOURIER_WIDTH)
        mixed = jnp.concatenate([p, f], axis=-1)
        x = x + jnp.dot(mixed, w_out)
        h2 = _rmsnorm(x, g_ffn)
        x = x + _hier_moe(h2, w_group_router, b_group_router, w_expert_router,
                          b_expert_router, w_gate, w_up, w_down)
    return _rmsnorm(x, g_final)
```

```python
import functools

import numpy as np
import jax
import jax.numpy as jnp
from jax import lax
from jax.experimental import pallas as pl
from jax.experimental.pallas import tpu as pltpu

D_MODEL = 2048
POOL_WINDOWS = (2, 4, 8, 16)
N_POOL_GROUPS = len(POOL_WINDOWS)
POOL_WIDTH = D_MODEL // 2
POOL_GROUP_DIM = POOL_WIDTH // N_POOL_GROUPS
FOURIER_WIDTH = D_MODEL - POOL_WIDTH
N_FOURIER_HEADS = 4
FOURIER_HEAD_DIM = FOURIER_WIDTH // N_FOURIER_HEADS
N_EXPERT_GROUPS = 4
EXPERTS_PER_GROUP = 4
N_EXPERTS = N_EXPERT_GROUPS * EXPERTS_PER_GROUP
EXPERT_FF = D_MODEL // 4
RMS_EPS = 1e-6

LANES = 128
BF16_SUBLANES = 16
ROUTER_LANES = LANES
MIB = 1024 * 1024

TOKEN_TILE = 512
SEQ_TILE = 256
POOL_HALO = BF16_SUBLANES
TWIDDLE_ROWS = 32

BF16 = jnp.bfloat16
F32 = jnp.float32


def _rmsnorm(x, g):
    ms = jnp.mean(x * x, axis=-1, keepdims=True)
    return x * lax.rsqrt(ms + RMS_EPS) * g


def _dot(a, b):
    return jnp.dot(a, b, preferred_element_type=F32)


def _twiddle(rows, cols, period):
    m = (np.asarray(rows, np.int64)[:, None] * np.asarray(cols, np.int64)[None, :]) % period
    ang = (2.0 * np.pi / period) * m.astype(np.float64)
    return np.cos(ang).astype(np.float32), np.sin(ang).astype(np.float32)


def _pool_band(seq_len, tile, halo):
    n_tiles = seq_len // tile
    out = np.zeros((3, N_POOL_GROUPS, tile, tile + 2 * halo), np.float64)
    for v, m in enumerate((0, 1, n_tiles - 1)):
        t0 = m * tile
        for g, k in enumerate(POOL_WINDOWS):
            for r in range(tile):
                t = t0 + r
                lo = max(t - (k - 1) // 2, 0)
                hi = min(t + k // 2 + 1, seq_len)
                out[v, g, r, lo - t0 + halo:hi - t0 + halo] = 1.0 / (hi - lo)
                out[v, g, r, r + halo] -= 1.0
    return out.astype(np.float32)


def _fourier_weight_kernel(cd_ref, sd_ref, w_ref, cw_ref, sw_ref, *, scale):
    w = w_ref[0]
    cw = jnp.dot(cd_ref[...], w, preferred_element_type=F32, precision=lax.Precision.HIGHEST)
    sw = jnp.dot(sd_ref[...], w, preferred_element_type=F32, precision=lax.Precision.HIGHEST)
    cw_ref[0] = (cw * scale).astype(BF16)
    sw_ref[0] = (sw * (-scale)).astype(BF16)


def _fourier_weights(w_fourier, seq_len):
    dh = FOURIER_HEAD_DIM
    cd, sd = _twiddle(np.arange(dh), np.arange(dh), dh)
    scale = 1.0 / np.sqrt(float(seq_len * dh))
    mat = pl.BlockSpec((dh, dh), lambda h: (0, 0))
    per_head = pl.BlockSpec((1, dh, dh), lambda h: (h, 0, 0))
    return pl.pallas_call(
        functools.partial(_fourier_weight_kernel, scale=scale),
        grid=(N_FOURIER_HEADS,),
        in_specs=[mat, mat, per_head],
        out_specs=[per_head, per_head],
        out_shape=[jax.ShapeDtypeStruct((N_FOURIER_HEADS, dh, dh), BF16)] * 2,
        name="fourier_weights",
    )(jnp.asarray(cd), jnp.asarray(sd), w_fourier)


def _norm_proj_kernel(x_ref, g_ref, w_ref, u_ref):
    h = _rmsnorm(x_ref[...], g_ref[...])
    u_ref[...] = _dot(h.astype(BF16), w_ref[...]).astype(BF16)


def _norm_proj(x2, g_mix, w_in_bf16):
    n, d = x2.shape
    tm = TOKEN_TILE
    vmem = 2 * tm * d * 4 + 2 * d * d * 2 + 2 * tm * d * 2 + 3 * tm * d * 4
    return pl.pallas_call(
        _norm_proj_kernel,
        grid=(n // tm,),
        in_specs=[
            pl.BlockSpec((tm, d), lambda i: (i, 0)),
            pl.BlockSpec((1, d), lambda i: (0, 0)),
            pl.BlockSpec((d, d), lambda i: (0, 0)),
        ],
        out_specs=pl.BlockSpec((tm, d), lambda i: (i, 0)),
        out_shape=jax.ShapeDtypeStruct((n, d), BF16),
        compiler_params=pltpu.CompilerParams(
            dimension_semantics=("arbitrary",), vmem_limit_bytes=vmem + 4 * MIB),
        name="norm_proj",
    )(x2, g_mix.reshape(1, d), w_in_bf16)


def _mix_kernel(uf_ref, up_ref, prev_ref, next_ref, c0_ref, s0_ref, cph_ref, sph_ref,
                band_ref, wpool_ref, pscale_ref, cw_ref, sw_ref, out_ref, lhs_ref):
    t = SEQ_TILE
    gd = POOL_GROUP_DIM
    hd = FOURIER_HEAD_DIM

    @pl.when(pl.program_id(1) == 0)
    def _():
        cph = cph_ref[0]
        sph = sph_ref[0]
        for r0 in range(0, t, TWIDDLE_ROWS):
            c0 = c0_ref[r0:r0 + TWIDDLE_ROWS, :]
            s0 = s0_ref[r0:r0 + TWIDDLE_ROWS, :]
            lhs_ref[r0:r0 + TWIDDLE_ROWS, :] = (cph * c0 - sph * s0).astype(BF16)
            lhs_ref[t + r0:t + r0 + TWIDDLE_ROWS, :] = (sph * c0 + cph * s0).astype(BF16)

    pq = _dot(lhs_ref[...], uf_ref[...])
    for h in range(N_FOURIER_HEADS):
        cols = slice(h * hd, (h + 1) * hd)
        p = pq[:t, cols].astype(BF16)
        q = pq[t:, cols].astype(BF16)
        f = _dot(p, cw_ref[h]) + _dot(q, sw_ref[h])
        out_ref[:, POOL_WIDTH + h * hd:POOL_WIDTH + (h + 1) * hd] = f.astype(BF16)

    win = jnp.concatenate([prev_ref[...], up_ref[...], next_ref[...]], axis=0)
    for g in range(N_POOL_GROUPS):
        cols = slice(g * gd, (g + 1) * gd)
        pooled = _dot(band_ref[0, g], win[:, cols]).astype(BF16)
        y = _dot(pooled, wpool_ref[g]) * pscale_ref[:, cols]
        out_ref[:, cols] = y.astype(BF16)


def _mix(u3, band, c0, s0, cph, sph, w_pool_bf16, pool_scale, cw, sw):
    b, s, d = u3.shape
    t = SEQ_TILE
    halo = POOL_HALO
    n_tiles = s // t
    halo_blocks_per_tile = t // halo
    last_halo_block = s // halo - 1

    def band_variant(m):
        return jnp.where(m == 0, 0, jnp.where(m == n_tiles - 1, 2, 1))

    in_specs = [
        pl.BlockSpec((None, s, FOURIER_WIDTH), lambda m, bi: (bi, 0, 1)),
        pl.BlockSpec((None, t, POOL_WIDTH), lambda m, bi: (bi, m, 0)),
        pl.BlockSpec((None, halo, POOL_WIDTH),
                     lambda m, bi: (bi, jnp.maximum(m * halo_blocks_per_tile - 1, 0), 0)),
        pl.BlockSpec((None, halo, POOL_WIDTH),
                     lambda m, bi: (bi, jnp.minimum((m + 1) * halo_blocks_per_tile, last_halo_block), 0)),
        pl.BlockSpec((t, s), lambda m, bi: (0, 0)),
        pl.BlockSpec((t, s), lambda m, bi: (0, 0)),
        pl.BlockSpec((1, 1, s), lambda m, bi: (m, 0, 0)),
        pl.BlockSpec((1, 1, s), lambda m, bi: (m, 0, 0)),
        pl.BlockSpec((1, N_POOL_GROUPS, t, t + 2 * halo), lambda m, bi: (band_variant(m), 0, 0, 0)),
        pl.BlockSpec((N_POOL_GROUPS, POOL_GROUP_DIM, POOL_GROUP_DIM), lambda m, bi: (0, 0, 0)),
        pl.BlockSpec((1, POOL_WIDTH), lambda m, bi: (0, 0)),
        pl.BlockSpec((N_FOURIER_HEADS, FOURIER_HEAD_DIM, FOURIER_HEAD_DIM), lambda m, bi: (0, 0, 0)),
        pl.BlockSpec((N_FOURIER_HEADS, FOURIER_HEAD_DIM, FOURIER_HEAD_DIM), lambda m, bi: (0, 0, 0)),
    ]
    vmem = (2 * s * FOURIER_WIDTH * 2
            + 2 * 2 * t * s * 4
            + 2 * t * s * 2
            + 2 * t * d * 2 * 2
            + 4 * 2 * t * FOURIER_WIDTH * 4)
    return pl.pallas_call(
        _mix_kernel,
        grid=(n_tiles, b),
        in_specs=in_specs,
        out_specs=pl.BlockSpec((None, t, d), lambda m, bi: (bi, m, 0)),
        out_shape=jax.ShapeDtypeStruct((b, s, d), BF16),
        scratch_shapes=[pltpu.VMEM((2 * t, s), BF16)],
        compiler_params=pltpu.CompilerParams(
            dimension_semantics=("arbitrary", "arbitrary"), vmem_limit_bytes=vmem + 4 * MIB),
        name="seq_mix",
    )(u3, u3, u3, u3, c0, s0, cph, sph, band, w_pool_bf16, pool_scale.reshape(1, POOL_WIDTH), cw, sw)


def _out_proj_route_kernel(x_ref, mixed_ref, wout_ref, g_ref, wr_hi_ref, wr_lo_ref, br_ref,
                           x1_ref, h2_ref, comb_ref):
    x1 = x_ref[...] + _dot(mixed_ref[...], wout_ref[...])
    x1_ref[...] = x1
    h2 = _rmsnorm(x1, g_ref[...])
    h_hi = h2.astype(BF16)
    h2_ref[...] = h_hi
    h_lo = (h2 - h_hi.astype(F32)).astype(BF16)
    logits = (_dot(h_hi, wr_hi_ref[...]) + _dot(h_hi, wr_lo_ref[...])
              + _dot(h_lo, wr_hi_ref[...]) + br_ref[...])

    lane = lax.broadcasted_iota(jnp.int32, logits.shape, 1)
    neg = jnp.float32(-jnp.inf)
    big = jnp.int32(ROUTER_LANES)

    is_group = lane < N_EXPERT_GROUPS
    gl = jnp.where(is_group, logits, neg)
    gmax = jnp.max(gl, axis=-1, keepdims=True)
    gidx = jnp.min(jnp.where(gl == gmax, lane, big), axis=-1, keepdims=True)
    p_g = 1.0 / jnp.sum(jnp.exp(gl - gmax), axis=-1, keepdims=True)

    e_lane = lane - N_EXPERT_GROUPS
    in_group = (e_lane >= gidx * EXPERTS_PER_GROUP) & (e_lane < (gidx + 1) * EXPERTS_PER_GROUP)
    el = jnp.where(in_group, logits, neg)
    v1 = jnp.max(el, axis=-1, keepdims=True)
    i1 = jnp.min(jnp.where(el == v1, lane, big), axis=-1, keepdims=True)
    el2 = jnp.where(lane == i1, neg, el)
    v2 = jnp.max(el2, axis=-1, keepdims=True)
    i2 = jnp.min(jnp.where(el2 == v2, lane, big), axis=-1, keepdims=True)
    r = jnp.exp(v2 - v1)
    w1 = p_g / (1.0 + r)
    w2 = p_g * r / (1.0 + r)
    comb_ref[...] = (jnp.where(lane == i1 - N_EXPERT_GROUPS, w1, 0.0)
                     + jnp.where(lane == i2 - N_EXPERT_GROUPS, w2, 0.0))


def _out_proj_route(x2, mixed2, w_out_bf16, g_ffn, wr_hi, wr_lo, br):
    n, d = x2.shape
    tm = TOKEN_TILE
    row = lambda width: pl.BlockSpec((tm, width), lambda i: (i, 0))
    const = lambda shape: pl.BlockSpec(shape, lambda i: (0, 0))
    vmem = (2 * tm * d * 4 * 2 + 2 * tm * d * 2 * 2 + 2 * d * d * 2
            + 4 * d * ROUTER_LANES * 2 + 4 * tm * d * 4)
    return pl.pallas_call(
        _out_proj_route_kernel,
        grid=(n // tm,),
        in_specs=[row(d), row(d), const((d, d)), const((1, d)),
                  const((d, ROUTER_LANES)), const((d, ROUTER_LANES)), const((1, ROUTER_LANES))],
        out_specs=[row(d), row(d), row(ROUTER_LANES)],
        out_shape=[jax.ShapeDtypeStruct((n, d), F32),
                   jax.ShapeDtypeStruct((n, d), BF16),
                   jax.ShapeDtypeStruct((n, ROUTER_LANES), F32)],
        compiler_params=pltpu.CompilerParams(
            dimension_semantics=("arbitrary",), vmem_limit_bytes=vmem + 4 * MIB),
        name="out_proj_route",
    )(x2, mixed2, w_out_bf16, g_ffn.reshape(1, d), wr_hi, wr_lo, br)


def _experts_kernel(h2_ref, comb_ref, x1_ref, wg_ref, wu_ref, wd_ref, g_ref, out_ref, acc_ref):
    e = pl.program_id(1)

    @pl.when(e == 0)
    def _():
        acc_ref[...] = jnp.zeros_like(acc_ref)

    h = h2_ref[...]
    a = _dot(h, wg_ref[0])
    v = _dot(h, wu_ref[0])
    comb = comb_ref[...]
    c_hi = comb.astype(BF16)
    c_lo = (comb - c_hi.astype(F32)).astype(BF16)
    pick = jnp.where(lax.broadcasted_iota(jnp.int32, (ROUTER_LANES, EXPERT_FF), 0) == e,
                     1.0, 0.0).astype(BF16)
    c = _dot(c_hi, pick) + _dot(c_lo, pick)
    act = a * (1.0 / (1.0 + jnp.exp(-a))) * v * c
    acc_ref[...] += _dot(act.astype(BF16), wd_ref[0])

    @pl.when(e == N_EXPERTS - 1)
    def _():
        out_ref[...] = _rmsnorm(x1_ref[...] + acc_ref[...], g_ref[...])


def _experts(h2, comb, x1, wg, wu, wd, g_final):
    n, d = x1.shape
    tm = TOKEN_TILE
    f = EXPERT_FF
    vmem = (2 * tm * d * 2 + 2 * tm * d * 4 * 2 + tm * d * 4 + 2 * 3 * d * f * 2
            + 2 * tm * ROUTER_LANES * 4 + 6 * tm * f * 4 + 2 * tm * d * 4)
    return pl.pallas_call(
        _experts_kernel,
        grid=(n // tm, N_EXPERTS),
        in_specs=[
            pl.BlockSpec((tm, d), lambda i, e: (i, 0)),
            pl.BlockSpec((tm, ROUTER_LANES), lambda i, e: (i, 0)),
            pl.BlockSpec((tm, d), lambda i, e: (i, 0)),
            pl.BlockSpec((1, d, f), lambda i, e: (e, 0, 0)),
            pl.BlockSpec((1, d, f), lambda i, e: (e, 0, 0)),
            pl.BlockSpec((1, f, d), lambda i, e: (e, 0, 0)),
            pl.BlockSpec((1, d), lambda i, e: (0, 0)),
        ],
        out_specs=pl.BlockSpec((tm, d), lambda i, e: (i, 0)),
        out_shape=jax.ShapeDtypeStruct((n, d), F32),
        scratch_shapes=[pltpu.VMEM((tm, d), F32)],
        compiler_params=pltpu.CompilerParams(
            dimension_semantics=("arbitrary", "arbitrary"), vmem_limit_bytes=vmem + 4 * MIB),
        name="experts",
    )(h2, comb, x1, wg, wu, wd, g_final.reshape(1, d))


def kernel(x, g_mix, w_in, w_pool, pool_scale, w_fourier, w_out, g_ffn, w_group_router,
           b_group_router, w_expert_router, b_expert_router, w_gate, w_up, w_down, g_final):
    b, s, d = x.shape
    assert d == D_MODEL and s % SEQ_TILE == 0 and (b * s) % TOKEN_TILE == 0
    n = b * s
    x2 = x.reshape(n, d)

    c0, s0 = _twiddle(np.arange(SEQ_TILE), np.arange(s), s)
    cph, sph = _twiddle(np.arange(0, s, SEQ_TILE), np.arange(s), s)
    band = _pool_band(s, SEQ_TILE, POOL_HALO)
    n_tiles = s // SEQ_TILE

    cw, sw = _fourier_weights(w_fourier, s)
    u = _norm_proj(x2, g_mix, w_in.astype(BF16))
    mixed = _mix(u.reshape(b, s, d), jnp.asarray(band).astype(BF16), jnp.asarray(c0), jnp.asarray(s0),
                 jnp.asarray(cph).reshape(n_tiles, 1, s), jnp.asarray(sph).reshape(n_tiles, 1, s),
                 w_pool.astype(BF16), pool_scale, cw, sw)

    wr = jnp.concatenate([w_group_router, w_expert_router], axis=1)
    wr = jnp.pad(wr, ((0, 0), (0, ROUTER_LANES - wr.shape[1])))
    wr_hi = wr.astype(BF16)
    wr_lo = (wr - wr_hi.astype(F32)).astype(BF16)
    br = jnp.concatenate([b_group_router, b_expert_router])
    br = jnp.pad(br, (0, ROUTER_LANES - br.shape[0])).reshape(1, ROUTER_LANES)

    x1, h2, comb = _out_proj_route(x2, mixed.reshape(n, d), w_out.astype(BF16), g_ffn, wr_hi, wr_lo, br)
    out = _experts(h2, comb, x1, w_gate.astype(BF16), w_up.astype(BF16), w_down.astype(BF16), g_final)
    return out.reshape(b, s, d)
```

```python
import functools

import numpy as np
import jax
import jax.numpy as jnp
from jax import lax
from jax.experimental import pallas as pl
from jax.experimental.pallas import tpu as pltpu

D_MODEL = 2048
POOL_WINDOWS = (2, 4, 8, 16)
N_POOL_GROUPS = len(POOL_WINDOWS)
POOL_WIDTH = D_MODEL // 2
POOL_GROUP_DIM = POOL_WIDTH // N_POOL_GROUPS
FOURIER_WIDTH = D_MODEL - POOL_WIDTH
N_FOURIER_HEADS = 4
FOURIER_HEAD_DIM = FOURIER_WIDTH // N_FOURIER_HEADS
N_EXPERT_GROUPS = 4
EXPERTS_PER_GROUP = 4
N_EXPERTS = N_EXPERT_GROUPS * EXPERTS_PER_GROUP
EXPERT_FF = D_MODEL // 4
RMS_EPS = 1e-6

LANES = 128
SUBLANES = 8
BF16_SUBLANES = 16
ROUTER_LANES = LANES
MIB = 1024 * 1024

TOKEN_TILE = 512
SEQ_TILE = 256
POOL_HALO = BF16_SUBLANES
TWIDDLE_ROWS = 32

PAIR_SLOT_A = (0, 0, 0, 1, 1, 3)
PAIR_SLOT_B = (1, 2, 3, 3, 2, 2)
PAIRS_PER_GROUP = len(PAIR_SLOT_A)
N_CLASSES = N_EXPERT_GROUPS * PAIRS_PER_GROUP
ROW_WORDS = D_MODEL + LANES
RANK_RADIX = 128
EXPERT_TILE = 256
PERM_TILE = 1024
UNPERM_TILE = 2048

BF16 = jnp.bfloat16
F32 = jnp.float32


def _rmsnorm(x, g):
    ms = jnp.mean(x * x, axis=-1, keepdims=True)
    return x * lax.rsqrt(ms + RMS_EPS) * g


def _dot(a, b):
    return jnp.dot(a, b, preferred_element_type=F32)


def _twiddle(rows, cols, period):
    m = (np.asarray(rows, np.int64)[:, None] * np.asarray(cols, np.int64)[None, :]) % period
    ang = (2.0 * np.pi / period) * m.astype(np.float64)
    return np.cos(ang).astype(np.float32), np.sin(ang).astype(np.float32)


def _pool_band(seq_len, tile, halo):
    n_tiles = seq_len // tile
    out = np.zeros((3, N_POOL_GROUPS, tile, tile + 2 * halo), np.float64)
    for v, m in enumerate((0, 1, n_tiles - 1)):
        t0 = m * tile
        for g, k in enumerate(POOL_WINDOWS):
            for r in range(tile):
                t = t0 + r
                lo = max(t - (k - 1) // 2, 0)
                hi = min(t + k // 2 + 1, seq_len)
                out[v, g, r, lo - t0 + halo:hi - t0 + halo] = 1.0 / (hi - lo)
                out[v, g, r, r + halo] -= 1.0
    return out.astype(np.float32)


def _fourier_weight_kernel(cd_ref, sd_ref, w_ref, cw_ref, sw_ref, *, scale):
    w = w_ref[0]
    cw = jnp.dot(cd_ref[...], w, preferred_element_type=F32, precision=lax.Precision.HIGHEST)
    sw = jnp.dot(sd_ref[...], w, preferred_element_type=F32, precision=lax.Precision.HIGHEST)
    cw_ref[0] = (cw * scale).astype(BF16)
    sw_ref[0] = (sw * (-scale)).astype(BF16)


def _fourier_weights(w_fourier, seq_len):
    dh = FOURIER_HEAD_DIM
    cd, sd = _twiddle(np.arange(dh), np.arange(dh), dh)
    scale = 1.0 / np.sqrt(float(seq_len * dh))
    mat = pl.BlockSpec((dh, dh), lambda h: (0, 0))
    per_head = pl.BlockSpec((1, dh, dh), lambda h: (h, 0, 0))
    return pl.pallas_call(
        functools.partial(_fourier_weight_kernel, scale=scale),
        grid=(N_FOURIER_HEADS,),
        in_specs=[mat, mat, per_head],
        out_specs=[per_head, per_head],
        out_shape=[jax.ShapeDtypeStruct((N_FOURIER_HEADS, dh, dh), BF16)] * 2,
        name="fourier_weights",
    )(jnp.asarray(cd), jnp.asarray(sd), w_fourier)


def _norm_proj_kernel(x_ref, g_ref, w_ref, u_ref):
    h = _rmsnorm(x_ref[...], g_ref[...])
    u_ref[...] = _dot(h.astype(BF16), w_ref[...]).astype(BF16)


def _norm_proj(x2, g_mix, w_in_bf16):
    n, d = x2.shape
    tm = TOKEN_TILE
    vmem = 2 * tm * d * 4 + 2 * d * d * 2 + 2 * tm * d * 2 + 3 * tm * d * 4
    return pl.pallas_call(
        _norm_proj_kernel,
        grid=(n // tm,),
        in_specs=[
            pl.BlockSpec((tm, d), lambda i: (i, 0)),
            pl.BlockSpec((1, d), lambda i: (0, 0)),
            pl.BlockSpec((d, d), lambda i: (0, 0)),
        ],
        out_specs=pl.BlockSpec((tm, d), lambda i: (i, 0)),
        out_shape=jax.ShapeDtypeStruct((n, d), BF16),
        compiler_params=pltpu.CompilerParams(
            dimension_semantics=("arbitrary",), vmem_limit_bytes=vmem + 4 * MIB),
        name="norm_proj",
    )(x2, g_mix.reshape(1, d), w_in_bf16)


def _mix_kernel(uf_ref, up_ref, prev_ref, next_ref, c0_ref, s0_ref, cph_ref, sph_ref,
                band_ref, wpool_ref, pscale_ref, cw_ref, sw_ref, out_ref, lhs_ref):
    t = SEQ_TILE
    gd = POOL_GROUP_DIM
    hd = FOURIER_HEAD_DIM

    @pl.when(pl.program_id(1) == 0)
    def _():
        cph = cph_ref[0]
        sph = sph_ref[0]
        for r0 in range(0, t, TWIDDLE_ROWS):
            c0 = c0_ref[r0:r0 + TWIDDLE_ROWS, :]
            s0 = s0_ref[r0:r0 + TWIDDLE_ROWS, :]
            lhs_ref[r0:r0 + TWIDDLE_ROWS, :] = (cph * c0 - sph * s0).astype(BF16)
            lhs_ref[t + r0:t + r0 + TWIDDLE_ROWS, :] = (sph * c0 + cph * s0).astype(BF16)

    pq = _dot(lhs_ref[...], uf_ref[...])
    for h in range(N_FOURIER_HEADS):
        cols = slice(h * hd, (h + 1) * hd)
        p = pq[:t, cols].astype(BF16)
        q = pq[t:, cols].astype(BF16)
        f = _dot(p, cw_ref[h]) + _dot(q, sw_ref[h])
        out_ref[:, POOL_WIDTH + h * hd:POOL_WIDTH + (h + 1) * hd] = f.astype(BF16)

    win = jnp.concatenate([prev_ref[...], up_ref[...], next_ref[...]], axis=0)
    for g in range(N_POOL_GROUPS):
        cols = slice(g * gd, (g + 1) * gd)
        pooled = _dot(band_ref[0, g], win[:, cols]).astype(BF16)
        y = _dot(pooled, wpool_ref[g]) * pscale_ref[:, cols]
        out_ref[:, cols] = y.astype(BF16)


def _mix(u3, band, c0, s0, cph, sph, w_pool_bf16, pool_scale, cw, sw):
    b, s, d = u3.shape
    t = SEQ_TILE
    halo = POOL_HALO
    n_tiles = s // t
    halo_blocks_per_tile = t // halo
    last_halo_block = s // halo - 1

    def band_variant(m):
        return jnp.where(m == 0, 0, jnp.where(m == n_tiles - 1, 2, 1))

    in_specs = [
        pl.BlockSpec((None, s, FOURIER_WIDTH), lambda m, bi: (bi, 0, 1)),
        pl.BlockSpec((None, t, POOL_WIDTH), lambda m, bi: (bi, m, 0)),
        pl.BlockSpec((None, halo, POOL_WIDTH),
                     lambda m, bi: (bi, jnp.maximum(m * halo_blocks_per_tile - 1, 0), 0)),
        pl.BlockSpec((None, halo, POOL_WIDTH),
                     lambda m, bi: (bi, jnp.minimum((m + 1) * halo_blocks_per_tile, last_halo_block), 0)),
        pl.BlockSpec((t, s), lambda m, bi: (0, 0)),
        pl.BlockSpec((t, s), lambda m, bi: (0, 0)),
        pl.BlockSpec((1, 1, s), lambda m, bi: (m, 0, 0)),
        pl.BlockSpec((1, 1, s), lambda m, bi: (m, 0, 0)),
        pl.BlockSpec((1, N_POOL_GROUPS, t, t + 2 * halo), lambda m, bi: (band_variant(m), 0, 0, 0)),
        pl.BlockSpec((N_POOL_GROUPS, POOL_GROUP_DIM, POOL_GROUP_DIM), lambda m, bi: (0, 0, 0)),
        pl.BlockSpec((1, POOL_WIDTH), lambda m, bi: (0, 0)),
        pl.BlockSpec((N_FOURIER_HEADS, FOURIER_HEAD_DIM, FOURIER_HEAD_DIM), lambda m, bi: (0, 0, 0)),
        pl.BlockSpec((N_FOURIER_HEADS, FOURIER_HEAD_DIM, FOURIER_HEAD_DIM), lambda m, bi: (0, 0, 0)),
    ]
    vmem = (2 * s * FOURIER_WIDTH * 2
            + 2 * 2 * t * s * 4
            + 2 * t * s * 2
            + 2 * t * d * 2 * 2
            + 4 * 2 * t * FOURIER_WIDTH * 4)
    return pl.pallas_call(
        _mix_kernel,
        grid=(n_tiles, b),
        in_specs=in_specs,
        out_specs=pl.BlockSpec((None, t, d), lambda m, bi: (bi, m, 0)),
        out_shape=jax.ShapeDtypeStruct((b, s, d), BF16),
        scratch_shapes=[pltpu.VMEM((2 * t, s), BF16)],
        compiler_params=pltpu.CompilerParams(
            dimension_semantics=("arbitrary", "arbitrary"), vmem_limit_bytes=vmem + 4 * MIB),
        name="seq_mix",
    )(u3, u3, u3, u3, c0, s0, cph, sph, band, w_pool_bf16, pool_scale.reshape(1, POOL_WIDTH), cw, sw)


def _out_proj_route_kernel(x_ref, mixed_ref, wout_ref, g_ref, wr_hi_ref, wr_lo_ref, br_ref, tri_ref,
                           pick_ref, rows_ref, route_ref, counts_ref, carry_ref):
    @pl.when(pl.program_id(0) == 0)
    def _():
        carry_ref[...] = jnp.zeros_like(carry_ref)

    x1 = x_ref[...] + _dot(mixed_ref[...], wout_ref[...])
    rows_ref[:, :D_MODEL] = x1
    h2 = _rmsnorm(x1, g_ref[...])
    h_hi = h2.astype(BF16)
    h_lo = (h2 - h_hi.astype(F32)).astype(BF16)
    logits = (_dot(h_hi, wr_hi_ref[...]) + _dot(h_hi, wr_lo_ref[...])
              + _dot(h_lo, wr_hi_ref[...]) + br_ref[...])

    lane = lax.broadcasted_iota(jnp.int32, logits.shape, 1)
    neg = jnp.float32(-jnp.inf)
    big = jnp.int32(ROUTER_LANES)

    is_group = lane < N_EXPERT_GROUPS
    gl = jnp.where(is_group, logits, neg)
    gmax = jnp.max(gl, axis=-1, keepdims=True)
    gidx = jnp.min(jnp.where(gl == gmax, lane, big), axis=-1, keepdims=True)
    p_g = 1.0 / jnp.sum(jnp.exp(gl - gmax), axis=-1, keepdims=True)

    e_lane = lane - N_EXPERT_GROUPS
    in_group = (e_lane >= gidx * EXPERTS_PER_GROUP) & (e_lane < (gidx + 1) * EXPERTS_PER_GROUP)
    el = jnp.where(in_group, logits, neg)
    v1 = jnp.max(el, axis=-1, keepdims=True)
    i1 = jnp.min(jnp.where(el == v1, lane, big), axis=-1, keepdims=True)
    el2 = jnp.where(lane == i1, neg, el)
    v2 = jnp.max(el2, axis=-1, keepdims=True)
    i2 = jnp.min(jnp.where(el2 == v2, lane, big), axis=-1, keepdims=True)
    r = jnp.exp(v2 - v1)
    w1 = p_g / (1.0 + r)
    w2 = p_g * r / (1.0 + r)

    first_is_low = i1 < i2
    first_lane = N_EXPERT_GROUPS + gidx * EXPERTS_PER_GROUP
    la = jnp.where(first_is_low, i1, i2) - first_lane
    lb = jnp.where(first_is_low, i2, i1) - first_lane
    w_low = jnp.where(first_is_low, w1, w2)
    w_high = jnp.where(first_is_low, w2, w1)
    pair = jnp.where(la == 0, lb - 1, jnp.where(la == 1, 6 - lb, 5))
    slot_a_is_high = la == 2
    w_a = jnp.where(slot_a_is_high, w_high, w_low)
    w_b = jnp.where(slot_a_is_high, w_low, w_high)
    cls = gidx * PAIRS_PER_GROUP + pair
    rows_ref[:, D_MODEL:] = jnp.where(lane == 0, w_a, jnp.where(lane == 1, w_b, 0.0))

    onehot = jnp.where(lane == cls, 1.0, 0.0)
    before = _dot(tri_ref[...], onehot.astype(BF16)) + carry_ref[...]
    rank = jnp.sum(jnp.where(lane == cls, before, 0.0), axis=-1, keepdims=True)
    carry_ref[...] += jnp.sum(onehot, axis=0, keepdims=True)
    counts_ref[...] = carry_ref[...]

    rank_hi = jnp.floor(rank * (1.0 / RANK_RADIX))
    rank_lo = rank - rank_hi * RANK_RADIX
    digits = jnp.where(lane == 0, cls.astype(F32),
                       jnp.where(lane == 1, rank_hi, jnp.where(lane == 2, rank_lo, 0.0)))
    route_ref[...] = lax.dot_general(pick_ref[...], digits.astype(BF16), (((1,), (1,)), ((), ())),
                                     preferred_element_type=F32)


def _out_proj_route(x2, mixed2, w_out_bf16, g_ffn, wr_hi, wr_lo, br):
    n, d = x2.shape
    tm = TOKEN_TILE
    row = lambda width: pl.BlockSpec((tm, width), lambda i: (i, 0))
    const = lambda shape: pl.BlockSpec(shape, lambda i: (0, 0))
    tri = np.tril(np.ones((tm, tm), np.float32), -1)
    pick = np.eye(SUBLANES, ROUTER_LANES, dtype=np.float32)
    vmem = (2 * tm * d * 4 + 2 * tm * d * 2 + 2 * tm * ROW_WORDS * 4 + 2 * d * d * 2
            + 4 * d * ROUTER_LANES * 2 + 2 * tm * tm * 2 + 4 * tm * d * 4)
    return pl.pallas_call(
        _out_proj_route_kernel,
        grid=(n // tm,),
        in_specs=[row(d), row(d), const((d, d)), const((1, d)),
                  const((d, ROUTER_LANES)), const((d, ROUTER_LANES)), const((1, ROUTER_LANES)),
                  const((tm, tm)), const((SUBLANES, ROUTER_LANES))],
        out_specs=[row(ROW_WORDS), pl.BlockSpec((SUBLANES, tm), lambda i: (0, i)),
                   const((1, ROUTER_LANES))],
        out_shape=[jax.ShapeDtypeStruct((n, ROW_WORDS), F32),
                   jax.ShapeDtypeStruct((SUBLANES, n), F32),
                   jax.ShapeDtypeStruct((1, ROUTER_LANES), F32)],
        scratch_shapes=[pltpu.VMEM((1, ROUTER_LANES), F32)],
        compiler_params=pltpu.CompilerParams(
            dimension_semantics=("arbitrary",), vmem_limit_bytes=vmem + 4 * MIB),
        name="out_proj_route",
    )(x2, mixed2, w_out_bf16, g_ffn.reshape(1, d), wr_hi, wr_lo, br,
      jnp.asarray(tri).astype(BF16), jnp.asarray(pick).astype(BF16))


def _routing_plan(route, counts, n):
    tm = EXPERT_TILE
    cnt = counts[0, :N_CLASSES].astype(jnp.int32)
    class_end = jnp.cumsum(cnt)
    class_start = class_end - cnt
    cls = route[0].astype(jnp.int32)
    rank = (route[1] * RANK_RADIX + route[2]).astype(jnp.int32)
    pos = class_start[cls] + rank

    first_tile = class_start // tm
    items_per_class = jnp.where(cnt > 0, (class_end - 1) // tm - first_tile + 1, 0)
    item_end = jnp.cumsum(items_per_class)
    item_start = item_end - items_per_class
    n_items = n // tm + N_CLASSES
    item = jnp.arange(n_items, dtype=jnp.int32)
    used = item < item_end[-1]
    item_c = jnp.minimum(item, item_end[-1] - 1)
    item_cls = jnp.sum((item_c[:, None] >= item_end[None, :]).astype(jnp.int32), axis=1)
    item_tile = first_tile[item_cls] + item_c - item_start[item_cls]
    lo = jnp.clip(class_start[item_cls] - item_tile * tm, 0, tm)
    hi = jnp.clip(class_end[item_cls] - item_tile * tm, 0, tm)
    hi = jnp.where(used, hi, lo)
    group_base = np.repeat(np.arange(N_EXPERT_GROUPS) * EXPERTS_PER_GROUP, PAIRS_PER_GROUP)
    class_ea = jnp.asarray(group_base + np.tile(PAIR_SLOT_A, N_EXPERT_GROUPS), jnp.int32)
    class_eb = jnp.asarray(group_base + np.tile(PAIR_SLOT_B, N_EXPERT_GROUPS), jnp.int32)
    return pos, item_tile, class_ea[item_cls], class_eb[item_cls], lo, hi


def _row_copy(src_ref, src_row, dst_ref, dst_row, sem):
    return pltpu.make_async_copy(src_ref.at[pl.ds(src_row, 1), :], dst_ref.at[pl.ds(dst_row, 1), :], sem)


def _permute_kernel(pos_ref, rows_ref, dst_ref, sem):
    tm = PERM_TILE
    base = pl.program_id(0) * tm

    def issue(r, carry):
        _row_copy(rows_ref, r, dst_ref, pos_ref[base + r], sem).start()
        return carry

    lax.fori_loop(0, tm, issue, 0, unroll=8)
    pltpu.make_async_copy(rows_ref, dst_ref.at[pl.ds(0, tm), :], sem).wait()


def _permute_rows(pos, rows):
    n, w = rows.shape
    tm = PERM_TILE
    grid_spec = pltpu.PrefetchScalarGridSpec(
        num_scalar_prefetch=1,
        grid=(n // tm,),
        in_specs=[pl.BlockSpec((tm, w), lambda i, pos_ref: (i, 0))],
        out_specs=pl.BlockSpec(memory_space=pl.ANY),
        scratch_shapes=[pltpu.SemaphoreType.DMA(())],
    )
    return pl.pallas_call(
        _permute_kernel,
        grid_spec=grid_spec,
        out_shape=jax.ShapeDtypeStruct((n, w), rows.dtype),
        compiler_params=pltpu.CompilerParams(
            dimension_semantics=("arbitrary",), vmem_limit_bytes=2 * tm * w * 4 + 4 * MIB),
        name="permute_rows",
    )(pos, rows)


def _expert_pair_kernel(tile_ref, ea_ref, eb_ref, lo_ref, hi_ref, rows_ref, gffn_ref, gfin_ref,
                        wga_ref, wua_ref, wda_ref, wgb_ref, wub_ref, wdb_ref, out_ref):
    del tile_ref, ea_ref, eb_ref
    lo = lo_ref[pl.program_id(0)]
    hi = hi_ref[pl.program_id(0)]

    @pl.when(hi > lo)
    def _():
        x1 = rows_ref[:, :D_MODEL]
        wts = rows_ref[:, D_MODEL:]
        lane = lax.broadcasted_iota(jnp.int32, wts.shape, 1)
        w_a = jnp.sum(jnp.where(lane == 0, wts, 0.0), axis=-1, keepdims=True)
        w_b = jnp.sum(jnp.where(lane == 1, wts, 0.0), axis=-1, keepdims=True)
        h = _rmsnorm(x1, gffn_ref[...]).astype(BF16)

        def gated(wg_ref, wu_ref, w):
            a = _dot(h, wg_ref[0])
            v = _dot(h, wu_ref[0])
            return (a * (1.0 / (1.0 + jnp.exp(-a))) * v * w).astype(BF16)

        y = (_dot(gated(wga_ref, wua_ref, w_a), wda_ref[0])
             + _dot(gated(wgb_ref, wub_ref, w_b), wdb_ref[0]))
        res = _rmsnorm(x1 + y, gfin_ref[...])

        row = lax.broadcasted_iota(jnp.int32, (EXPERT_TILE, 1), 0)
        mine = (row >= lo) & (row < hi)

        @pl.when(lo == 0)
        def _():
            out_ref[...] = jnp.where(mine, res, 0.0)

        @pl.when(lo > 0)
        def _():
            out_ref[...] = jnp.where(mine, res, out_ref[...])


def _expert_pairs(item_tile, item_ea, item_eb, item_lo, item_hi, rows_sorted, g_ffn, g_final, wg, wu, wd):
    n, w = rows_sorted.shape
    d, f = D_MODEL, EXPERT_FF
    tm = EXPERT_TILE
    gate_a = pl.BlockSpec((1, d, f), lambda j, t, ea, eb, lo, hi: (ea[j], 0, 0))
    gate_b = pl.BlockSpec((1, d, f), lambda j, t, ea, eb, lo, hi: (eb[j], 0, 0))
    down_a = pl.BlockSpec((1, f, d), lambda j, t, ea, eb, lo, hi: (ea[j], 0, 0))
    down_b = pl.BlockSpec((1, f, d), lambda j, t, ea, eb, lo, hi: (eb[j], 0, 0))
    gain = pl.BlockSpec((1, d), lambda j, t, ea, eb, lo, hi: (0, 0))
    grid_spec = pltpu.PrefetchScalarGridSpec(
        num_scalar_prefetch=5,
        grid=(item_tile.shape[0],),
        in_specs=[pl.BlockSpec((tm, w), lambda j, t, ea, eb, lo, hi: (t[j], 0)), gain, gain,
                  gate_a, gate_a, down_a, gate_b, gate_b, down_b],
        out_specs=pl.BlockSpec((tm, d), lambda j, t, ea, eb, lo, hi: (t[j], 0)),
    )
    vmem = 2 * 6 * d * f * 2 + 2 * tm * w * 4 + 2 * tm * d * 4 + 8 * tm * d * 4
    return pl.pallas_call(
        _expert_pair_kernel,
        grid_spec=grid_spec,
        out_shape=jax.ShapeDtypeStruct((n, d), F32),
        compiler_params=pltpu.CompilerParams(
            dimension_semantics=("arbitrary",), vmem_limit_bytes=vmem + 4 * MIB),
        name="expert_pairs",
    )(item_tile, item_ea, item_eb, item_lo, item_hi, rows_sorted, g_ffn.reshape(1, d),
      g_final.reshape(1, d), wg, wu, wd, wg, wu, wd)


def _unpermute_kernel(pos_ref, src_ref, dst_ref, sem):
    tm = UNPERM_TILE
    base = pl.program_id(0) * tm

    def issue(r, carry):
        _row_copy(src_ref, pos_ref[base + r], dst_ref, base + r, sem).start()
        return carry

    lax.fori_loop(0, tm, issue, 0, unroll=8)
    pltpu.make_async_copy(src_ref.at[pl.ds(0, tm), :], dst_ref.at[pl.ds(0, tm), :], sem).wait()


def _unpermute_rows(pos, sorted_rows, n):
    _, d = sorted_rows.shape
    grid_spec = pltpu.PrefetchScalarGridSpec(
        num_scalar_prefetch=1,
        grid=(n // UNPERM_TILE,),
        in_specs=[pl.BlockSpec(memory_space=pl.ANY)],
        out_specs=pl.BlockSpec(memory_space=pl.ANY),
        scratch_shapes=[pltpu.SemaphoreType.DMA(())],
    )
    return pl.pallas_call(
        _unpermute_kernel,
        grid_spec=grid_spec,
        out_shape=jax.ShapeDtypeStruct((n, d), sorted_rows.dtype),
        compiler_params=pltpu.CompilerParams(dimension_semantics=("arbitrary",)),
        name="unpermute_rows",
    )(pos, sorted_rows)


def kernel(x, g_mix, w_in, w_pool, pool_scale, w_fourier, w_out, g_ffn, w_group_router,
           b_group_router, w_expert_router, b_expert_router, w_gate, w_up, w_down, g_final):
    b, s, d = x.shape
    assert d == D_MODEL and s % SEQ_TILE == 0
    assert (b * s) % max(TOKEN_TILE, PERM_TILE, UNPERM_TILE) == 0
    n = b * s
    x2 = x.reshape(n, d)

    c0, s0 = _twiddle(np.arange(SEQ_TILE), np.arange(s), s)
    cph, sph = _twiddle(np.arange(0, s, SEQ_TILE), np.arange(s), s)
    band = _pool_band(s, SEQ_TILE, POOL_HALO)
    n_tiles = s // SEQ_TILE

    cw, sw = _fourier_weights(w_fourier, s)
    u = _norm_proj(x2, g_mix, w_in.astype(BF16))
    mixed = _mix(u.reshape(b, s, d), jnp.asarray(band).astype(BF16), jnp.asarray(c0), jnp.asarray(s0),
                 jnp.asarray(cph).reshape(n_tiles, 1, s), jnp.asarray(sph).reshape(n_tiles, 1, s),
                 w_pool.astype(BF16), pool_scale, cw, sw)

    wr = jnp.concatenate([w_group_router, w_expert_router], axis=1)
    wr = jnp.pad(wr, ((0, 0), (0, ROUTER_LANES - wr.shape[1])))
    wr_hi = wr.astype(BF16)
    wr_lo = (wr - wr_hi.astype(F32)).astype(BF16)
    br = jnp.concatenate([b_group_router, b_expert_router])
    br = jnp.pad(br, (0, ROUTER_LANES - br.shape[0])).reshape(1, ROUTER_LANES)

    rows, route, counts = _out_proj_route(x2, mixed.reshape(n, d), w_out.astype(BF16), g_ffn,
                                          wr_hi, wr_lo, br)
    pos, item_tile, item_ea, item_eb, item_lo, item_hi = _routing_plan(route, counts, n)
    rows_sorted = _permute_rows(pos, rows)
    out_sorted = _expert_pairs(item_tile, item_ea, item_eb, item_lo, item_hi, rows_sorted, g_ffn, g_final,
                               w_gate.astype(BF16), w_up.astype(BF16), w_down.astype(BF16))
    out = _unpermute_rows(pos, out_sorted, n)
    return out.reshape(b, s, d)
```

```python
import functools

import numpy as np
import jax
import jax.numpy as jnp
from jax import lax
from jax.experimental import pallas as pl
from jax.experimental.pallas import tpu as pltpu

D_MODEL = 2048
POOL_WINDOWS = (2, 4, 8, 16)
N_POOL_GROUPS = len(POOL_WINDOWS)
POOL_WIDTH = D_MODEL // 2
POOL_GROUP_DIM = POOL_WIDTH // N_POOL_GROUPS
FOURIER_WIDTH = D_MODEL - POOL_WIDTH
N_FOURIER_HEADS = 4
FOURIER_HEAD_DIM = FOURIER_WIDTH // N_FOURIER_HEADS
N_EXPERT_GROUPS = 4
EXPERTS_PER_GROUP = 4
N_EXPERTS = N_EXPERT_GROUPS * EXPERTS_PER_GROUP
EXPERT_FF = D_MODEL // 4
RMS_EPS = 1e-6

LANES = 128
SUBLANES = 8
BF16_SUBLANES = 16
ROUTER_LANES = LANES
MIB = 1024 * 1024

TOKEN_TILE = 512
SEQ_TILE = 256
POOL_HALO = BF16_SUBLANES
TWIDDLE_ROWS = 32

PAIR_SLOT_A = (0, 0, 0, 1, 1, 3)
PAIR_SLOT_B = (1, 2, 3, 3, 2, 2)
PAIRS_PER_GROUP = len(PAIR_SLOT_A)
N_CLASSES = N_EXPERT_GROUPS * PAIRS_PER_GROUP
ROW_WORDS = D_MODEL + LANES
RANK_RADIX = 128
EXPERT_TILE = 256
PERM_TILE = 1024

BF16 = jnp.bfloat16
F32 = jnp.float32


def _rmsnorm(x, g):
    ms = jnp.mean(x * x, axis=-1, keepdims=True)
    return x * lax.rsqrt(ms + RMS_EPS) * g


def _dot(a, b):
    return jnp.dot(a, b, preferred_element_type=F32)


def _twiddle(rows, cols, period):
    m = (np.asarray(rows, np.int64)[:, None] * np.asarray(cols, np.int64)[None, :]) % period
    ang = (2.0 * np.pi / period) * m.astype(np.float64)
    return np.cos(ang).astype(np.float32), np.sin(ang).astype(np.float32)


def _pool_band(seq_len, tile, halo):
    n_tiles = seq_len // tile
    out = np.zeros((3, N_POOL_GROUPS, tile, tile + 2 * halo), np.float64)
    for v, m in enumerate((0, 1, n_tiles - 1)):
        t0 = m * tile
        for g, k in enumerate(POOL_WINDOWS):
            for r in range(tile):
                t = t0 + r
                lo = max(t - (k - 1) // 2, 0)
                hi = min(t + k // 2 + 1, seq_len)
                out[v, g, r, lo - t0 + halo:hi - t0 + halo] = 1.0 / (hi - lo)
                out[v, g, r, r + halo] -= 1.0
    return out.astype(np.float32)


def _fourier_weight_kernel(cd_ref, sd_ref, w_ref, cw_ref, sw_ref, *, scale):
    w = w_ref[0]
    cw = jnp.dot(cd_ref[...], w, preferred_element_type=F32, precision=lax.Precision.HIGHEST)
    sw = jnp.dot(sd_ref[...], w, preferred_element_type=F32, precision=lax.Precision.HIGHEST)
    cw_ref[0] = (cw * scale).astype(BF16)
    sw_ref[0] = (sw * (-scale)).astype(BF16)


def _fourier_weights(w_fourier, seq_len):
    dh = FOURIER_HEAD_DIM
    cd, sd = _twiddle(np.arange(dh), np.arange(dh), dh)
    scale = 1.0 / np.sqrt(float(seq_len * dh))
    mat = pl.BlockSpec((dh, dh), lambda h: (0, 0))
    per_head = pl.BlockSpec((1, dh, dh), lambda h: (h, 0, 0))
    return pl.pallas_call(
        functools.partial(_fourier_weight_kernel, scale=scale),
        grid=(N_FOURIER_HEADS,),
        in_specs=[mat, mat, per_head],
        out_specs=[per_head, per_head],
        out_shape=[jax.ShapeDtypeStruct((N_FOURIER_HEADS, dh, dh), BF16)] * 2,
        name="fourier_weights",
    )(jnp.asarray(cd), jnp.asarray(sd), w_fourier)


def _norm_proj_kernel(x_ref, g_ref, w_ref, u_ref):
    h = _rmsnorm(x_ref[...], g_ref[...])
    u_ref[...] = _dot(h.astype(BF16), w_ref[...]).astype(BF16)


def _norm_proj(x2, g_mix, w_in_bf16):
    n, d = x2.shape
    tm = TOKEN_TILE
    vmem = 2 * tm * d * 4 + 2 * d * d * 2 + 2 * tm * d * 2 + 3 * tm * d * 4
    return pl.pallas_call(
        _norm_proj_kernel,
        grid=(n // tm,),
        in_specs=[
            pl.BlockSpec((tm, d), lambda i: (i, 0)),
            pl.BlockSpec((1, d), lambda i: (0, 0)),
            pl.BlockSpec((d, d), lambda i: (0, 0)),
        ],
        out_specs=pl.BlockSpec((tm, d), lambda i: (i, 0)),
        out_shape=jax.ShapeDtypeStruct((n, d), BF16),
        compiler_params=pltpu.CompilerParams(
            dimension_semantics=("arbitrary",), vmem_limit_bytes=vmem + 4 * MIB),
        name="norm_proj",
    )(x2, g_mix.reshape(1, d), w_in_bf16)


def _mix_kernel(uf_ref, up_ref, prev_ref, next_ref, c0_ref, s0_ref, cph_ref, sph_ref,
                band_ref, wpool_ref, pscale_ref, cw_ref, sw_ref, out_ref, lhs_ref):
    t = SEQ_TILE
    gd = POOL_GROUP_DIM
    hd = FOURIER_HEAD_DIM

    @pl.when(pl.program_id(1) == 0)
    def _():
        cph = cph_ref[0]
        sph = sph_ref[0]
        for r0 in range(0, t, TWIDDLE_ROWS):
            c0 = c0_ref[r0:r0 + TWIDDLE_ROWS, :]
            s0 = s0_ref[r0:r0 + TWIDDLE_ROWS, :]
            lhs_ref[r0:r0 + TWIDDLE_ROWS, :] = (cph * c0 - sph * s0).astype(BF16)
            lhs_ref[t + r0:t + r0 + TWIDDLE_ROWS, :] = (sph * c0 + cph * s0).astype(BF16)

    pq = _dot(lhs_ref[...], uf_ref[...])
    for h in range(N_FOURIER_HEADS):
        cols = slice(h * hd, (h + 1) * hd)
        p = pq[:t, cols].astype(BF16)
        q = pq[t:, cols].astype(BF16)
        f = _dot(p, cw_ref[h]) + _dot(q, sw_ref[h])
        out_ref[:, POOL_WIDTH + h * hd:POOL_WIDTH + (h + 1) * hd] = f.astype(BF16)

    win = jnp.concatenate([prev_ref[...], up_ref[...], next_ref[...]], axis=0)
    for g in range(N_POOL_GROUPS):
        cols = slice(g * gd, (g + 1) * gd)
        pooled = _dot(band_ref[0, g], win[:, cols]).astype(BF16)
        y = _dot(pooled, wpool_ref[g]) * pscale_ref[:, cols]
        out_ref[:, cols] = y.astype(BF16)


def _mix(u3, band, c0, s0, cph, sph, w_pool_bf16, pool_scale, cw, sw):
    b, s, d = u3.shape
    t = SEQ_TILE
    halo = POOL_HALO
    n_tiles = s // t
    halo_blocks_per_tile = t // halo
    last_halo_block = s // halo - 1

    def band_variant(m):
        return jnp.where(m == 0, 0, jnp.where(m == n_tiles - 1, 2, 1))

    in_specs = [
        pl.BlockSpec((None, s, FOURIER_WIDTH), lambda m, bi: (bi, 0, 1)),
        pl.BlockSpec((None, t, POOL_WIDTH), lambda m, bi: (bi, m, 0)),
        pl.BlockSpec((None, halo, POOL_WIDTH),
                     lambda m, bi: (bi, jnp.maximum(m * halo_blocks_per_tile - 1, 0), 0)),
        pl.BlockSpec((None, halo, POOL_WIDTH),
                     lambda m, bi: (bi, jnp.minimum((m + 1) * halo_blocks_per_tile, last_halo_block), 0)),
        pl.BlockSpec((t, s), lambda m, bi: (0, 0)),
        pl.BlockSpec((t, s), lambda m, bi: (0, 0)),
        pl.BlockSpec((1, 1, s), lambda m, bi: (m, 0, 0)),
        pl.BlockSpec((1, 1, s), lambda m, bi: (m, 0, 0)),
        pl.BlockSpec((1, N_POOL_GROUPS, t, t + 2 * halo), lambda m, bi: (band_variant(m), 0, 0, 0)),
        pl.BlockSpec((N_POOL_GROUPS, POOL_GROUP_DIM, POOL_GROUP_DIM), lambda m, bi: (0, 0, 0)),
        pl.BlockSpec((1, POOL_WIDTH), lambda m, bi: (0, 0)),
        pl.BlockSpec((N_FOURIER_HEADS, FOURIER_HEAD_DIM, FOURIER_HEAD_DIM), lambda m, bi: (0, 0, 0)),
        pl.BlockSpec((N_FOURIER_HEADS, FOURIER_HEAD_DIM, FOURIER_HEAD_DIM), lambda m, bi: (0, 0, 0)),
    ]
    vmem = (2 * s * FOURIER_WIDTH * 2
            + 2 * 2 * t * s * 4
            + 2 * t * s * 2
            + 2 * t * d * 2 * 2
            + 4 * 2 * t * FOURIER_WIDTH * 4)
    return pl.pallas_call(
        _mix_kernel,
        grid=(n_tiles, b),
        in_specs=in_specs,
        out_specs=pl.BlockSpec((None, t, d), lambda m, bi: (bi, m, 0)),
        out_shape=jax.ShapeDtypeStruct((b, s, d), BF16),
        scratch_shapes=[pltpu.VMEM((2 * t, s), BF16)],
        compiler_params=pltpu.CompilerParams(
            dimension_semantics=("arbitrary", "arbitrary"), vmem_limit_bytes=vmem + 4 * MIB),
        name="seq_mix",
    )(u3, u3, u3, u3, c0, s0, cph, sph, band, w_pool_bf16, pool_scale.reshape(1, POOL_WIDTH), cw, sw)


def _out_proj_route_kernel(x_ref, mixed_ref, wout_ref, g_ref, wr_hi_ref, wr_lo_ref, br_ref, tri_ref,
                           pick_ref, rows_ref, route_ref, counts_ref, carry_ref):
    @pl.when(pl.program_id(0) == 0)
    def _():
        carry_ref[...] = jnp.zeros_like(carry_ref)

    x1 = x_ref[...] + _dot(mixed_ref[...], wout_ref[...])
    rows_ref[:, :D_MODEL] = x1
    h2 = _rmsnorm(x1, g_ref[...])
    h_hi = h2.astype(BF16)
    h_lo = (h2 - h_hi.astype(F32)).astype(BF16)
    logits = (_dot(h_hi, wr_hi_ref[...]) + _dot(h_hi, wr_lo_ref[...])
              + _dot(h_lo, wr_hi_ref[...]) + br_ref[...])

    lane = lax.broadcasted_iota(jnp.int32, logits.shape, 1)
    neg = jnp.float32(-jnp.inf)
    big = jnp.int32(ROUTER_LANES)

    is_group = lane < N_EXPERT_GROUPS
    gl = jnp.where(is_group, logits, neg)
    gmax = jnp.max(gl, axis=-1, keepdims=True)
    gidx = jnp.min(jnp.where(gl == gmax, lane, big), axis=-1, keepdims=True)
    p_g = 1.0 / jnp.sum(jnp.exp(gl - gmax), axis=-1, keepdims=True)

    e_lane = lane - N_EXPERT_GROUPS
    in_group = (e_lane >= gidx * EXPERTS_PER_GROUP) & (e_lane < (gidx + 1) * EXPERTS_PER_GROUP)
    el = jnp.where(in_group, logits, neg)
    v1 = jnp.max(el, axis=-1, keepdims=True)
    i1 = jnp.min(jnp.where(el == v1, lane, big), axis=-1, keepdims=True)
    el2 = jnp.where(lane == i1, neg, el)
    v2 = jnp.max(el2, axis=-1, keepdims=True)
    i2 = jnp.min(jnp.where(el2 == v2, lane, big), axis=-1, keepdims=True)
    r = jnp.exp(v2 - v1)
    w1 = p_g / (1.0 + r)
    w2 = p_g * r / (1.0 + r)

    first_is_low = i1 < i2
    first_lane = N_EXPERT_GROUPS + gidx * EXPERTS_PER_GROUP
    la = jnp.where(first_is_low, i1, i2) - first_lane
    lb = jnp.where(first_is_low, i2, i1) - first_lane
    w_low = jnp.where(first_is_low, w1, w2)
    w_high = jnp.where(first_is_low, w2, w1)
    pair = jnp.where(la == 0, lb - 1, jnp.where(la == 1, 6 - lb, 5))
    slot_a_is_high = la == 2
    w_a = jnp.where(slot_a_is_high, w_high, w_low)
    w_b = jnp.where(slot_a_is_high, w_low, w_high)
    cls = gidx * PAIRS_PER_GROUP + pair
    rows_ref[:, D_MODEL:] = jnp.where(lane == 0, w_a, jnp.where(lane == 1, w_b, 0.0))

    onehot = jnp.where(lane == cls, 1.0, 0.0)
    before = _dot(tri_ref[...], onehot.astype(BF16)) + carry_ref[...]
    rank = jnp.sum(jnp.where(lane == cls, before, 0.0), axis=-1, keepdims=True)
    carry_ref[...] += jnp.sum(onehot, axis=0, keepdims=True)
    counts_ref[...] = carry_ref[...]

    rank_hi = jnp.floor(rank * (1.0 / RANK_RADIX))
    rank_lo = rank - rank_hi * RANK_RADIX
    digits = jnp.where(lane == 0, cls.astype(F32),
                       jnp.where(lane == 1, rank_hi, jnp.where(lane == 2, rank_lo, 0.0)))
    route_ref[...] = lax.dot_general(pick_ref[...], digits.astype(BF16), (((1,), (1,)), ((), ())),
                                     preferred_element_type=F32)


def _out_proj_route(x2, mixed2, w_out_bf16, g_ffn, wr_hi, wr_lo, br):
    n, d = x2.shape
    tm = TOKEN_TILE
    row = lambda width: pl.BlockSpec((tm, width), lambda i: (i, 0))
    const = lambda shape: pl.BlockSpec(shape, lambda i: (0, 0))
    tri = np.tril(np.ones((tm, tm), np.float32), -1)
    pick = np.eye(SUBLANES, ROUTER_LANES, dtype=np.float32)
    vmem = (2 * tm * d * 4 + 2 * tm * d * 2 + 2 * tm * ROW_WORDS * 4 + 2 * d * d * 2
            + 4 * d * ROUTER_LANES * 2 + 2 * tm * tm * 2 + 4 * tm * d * 4)
    return pl.pallas_call(
        _out_proj_route_kernel,
        grid=(n // tm,),
        in_specs=[row(d), row(d), const((d, d)), const((1, d)),
                  const((d, ROUTER_LANES)), const((d, ROUTER_LANES)), const((1, ROUTER_LANES)),
                  const((tm, tm)), const((SUBLANES, ROUTER_LANES))],
        out_specs=[row(ROW_WORDS), pl.BlockSpec((SUBLANES, tm), lambda i: (0, i)),
                   const((1, ROUTER_LANES))],
        out_shape=[jax.ShapeDtypeStruct((n, ROW_WORDS), F32),
                   jax.ShapeDtypeStruct((SUBLANES, n), F32),
                   jax.ShapeDtypeStruct((1, ROUTER_LANES), F32)],
        scratch_shapes=[pltpu.VMEM((1, ROUTER_LANES), F32)],
        compiler_params=pltpu.CompilerParams(
            dimension_semantics=("arbitrary",), vmem_limit_bytes=vmem + 4 * MIB),
        name="out_proj_route",
    )(x2, mixed2, w_out_bf16, g_ffn.reshape(1, d), wr_hi, wr_lo, br,
      jnp.asarray(tri).astype(BF16), jnp.asarray(pick).astype(BF16))


def _routing_plan(route, counts, n):
    tm = EXPERT_TILE
    cnt = counts[0, :N_CLASSES].astype(jnp.int32)
    class_end = jnp.cumsum(cnt)
    class_start = class_end - cnt
    cls = route[0].astype(jnp.int32)
    rank = (route[1] * RANK_RADIX + route[2]).astype(jnp.int32)
    pos = class_start[cls] + rank

    first_tile = class_start // tm
    items_per_class = jnp.where(cnt > 0, (class_end - 1) // tm - first_tile + 1, 0)
    item_end = jnp.cumsum(items_per_class)
    item_start = item_end - items_per_class
    n_items = n // tm + N_CLASSES
    item = jnp.arange(n_items, dtype=jnp.int32)
    used = item < item_end[-1]
    item_c = jnp.minimum(item, item_end[-1] - 1)
    item_cls = jnp.sum((item_c[:, None] >= item_end[None, :]).astype(jnp.int32), axis=1)
    item_tile = first_tile[item_cls] + item_c - item_start[item_cls]
    lo = jnp.clip(class_start[item_cls] - item_tile * tm, 0, tm)
    hi = jnp.clip(class_end[item_cls] - item_tile * tm, 0, tm)
    hi = jnp.where(used, hi, lo)
    group_base = np.repeat(np.arange(N_EXPERT_GROUPS) * EXPERTS_PER_GROUP, PAIRS_PER_GROUP)
    class_ea = jnp.asarray(group_base + np.tile(PAIR_SLOT_A, N_EXPERT_GROUPS), jnp.int32)
    class_eb = jnp.asarray(group_base + np.tile(PAIR_SLOT_B, N_EXPERT_GROUPS), jnp.int32)
    return pos, item_tile, class_ea[item_cls], class_eb[item_cls], lo, hi


def _row_copy(src_ref, src_row, dst_ref, dst_row, sem):
    return pltpu.make_async_copy(src_ref.at[pl.ds(src_row, 1), :], dst_ref.at[pl.ds(dst_row, 1), :], sem)


def _scatter_rows_kernel(pos_ref, rows_ref, dst_ref, *rest, with_inverse):
    sem = rest[-1]
    tm = PERM_TILE
    base = pl.program_id(0) * tm

    def issue(r, carry):
        p = pos_ref[base + r]
        if with_inverse:
            rest[0][p] = base + r
        _row_copy(rows_ref, r, dst_ref, p, sem).start()
        return carry

    lax.fori_loop(0, tm, issue, 0, unroll=8)
    pltpu.make_async_copy(rows_ref, dst_ref.at[pl.ds(0, tm), :], sem).wait()


def _scatter_rows(pos, rows, *, with_inverse, name):
    n, w = rows.shape
    tm = PERM_TILE
    out_specs = [pl.BlockSpec(memory_space=pl.ANY)]
    out_shape = [jax.ShapeDtypeStruct((n, w), rows.dtype)]
    if with_inverse:
        out_specs.append(pl.BlockSpec(memory_space=pltpu.SMEM))
        out_shape.append(jax.ShapeDtypeStruct((n,), jnp.int32))
    grid_spec = pltpu.PrefetchScalarGridSpec(
        num_scalar_prefetch=1,
        grid=(n // tm,),
        in_specs=[pl.BlockSpec((tm, w), lambda i, pos_ref: (i, 0))],
        out_specs=out_specs,
        scratch_shapes=[pltpu.SemaphoreType.DMA(())],
    )
    return pl.pallas_call(
        functools.partial(_scatter_rows_kernel, with_inverse=with_inverse),
        grid_spec=grid_spec,
        out_shape=out_shape,
        compiler_params=pltpu.CompilerParams(
            dimension_semantics=("arbitrary",), vmem_limit_bytes=2 * tm * w * 4 + 4 * MIB),
        name=name,
    )(pos, rows)


def _expert_pair_kernel(tile_ref, ea_ref, eb_ref, lo_ref, hi_ref, rows_ref, gffn_ref, gfin_ref,
                        wga_ref, wua_ref, wda_ref, wgb_ref, wub_ref, wdb_ref, out_ref):
    del tile_ref, ea_ref, eb_ref
    lo = lo_ref[pl.program_id(0)]
    hi = hi_ref[pl.program_id(0)]

    @pl.when(hi > lo)
    def _():
        x1 = rows_ref[:, :D_MODEL]
        wts = rows_ref[:, D_MODEL:]
        lane = lax.broadcasted_iota(jnp.int32, wts.shape, 1)
        w_a = jnp.sum(jnp.where(lane == 0, wts, 0.0), axis=-1, keepdims=True)
        w_b = jnp.sum(jnp.where(lane == 1, wts, 0.0), axis=-1, keepdims=True)
        h = _rmsnorm(x1, gffn_ref[...]).astype(BF16)

        def gated(wg_ref, wu_ref, w):
            a = _dot(h, wg_ref[0])
            v = _dot(h, wu_ref[0])
            return (a * (1.0 / (1.0 + jnp.exp(-a))) * v * w).astype(BF16)

        y = (_dot(gated(wga_ref, wua_ref, w_a), wda_ref[0])
             + _dot(gated(wgb_ref, wub_ref, w_b), wdb_ref[0]))
        res = _rmsnorm(x1 + y, gfin_ref[...])

        row = lax.broadcasted_iota(jnp.int32, (EXPERT_TILE, 1), 0)
        mine = (row >= lo) & (row < hi)

        @pl.when(lo == 0)
        def _():
            out_ref[...] = jnp.where(mine, res, 0.0)

        @pl.when(lo > 0)
        def _():
            out_ref[...] = jnp.where(mine, res, out_ref[...])


def _expert_pairs(item_tile, item_ea, item_eb, item_lo, item_hi, rows_sorted, g_ffn, g_final, wg, wu, wd):
    n, w = rows_sorted.shape
    d, f = D_MODEL, EXPERT_FF
    tm = EXPERT_TILE
    gate_a = pl.BlockSpec((1, d, f), lambda j, t, ea, eb, lo, hi: (ea[j], 0, 0))
    gate_b = pl.BlockSpec((1, d, f), lambda j, t, ea, eb, lo, hi: (eb[j], 0, 0))
    down_a = pl.BlockSpec((1, f, d), lambda j, t, ea, eb, lo, hi: (ea[j], 0, 0))
    down_b = pl.BlockSpec((1, f, d), lambda j, t, ea, eb, lo, hi: (eb[j], 0, 0))
    gain = pl.BlockSpec((1, d), lambda j, t, ea, eb, lo, hi: (0, 0))
    grid_spec = pltpu.PrefetchScalarGridSpec(
        num_scalar_prefetch=5,
        grid=(item_tile.shape[0],),
        in_specs=[pl.BlockSpec((tm, w), lambda j, t, ea, eb, lo, hi: (t[j], 0)), gain, gain,
                  gate_a, gate_a, down_a, gate_b, gate_b, down_b],
        out_specs=pl.BlockSpec((tm, d), lambda j, t, ea, eb, lo, hi: (t[j], 0)),
    )
    vmem = 2 * 6 * d * f * 2 + 2 * tm * w * 4 + 2 * tm * d * 4 + 8 * tm * d * 4
    return pl.pallas_call(
        _expert_pair_kernel,
        grid_spec=grid_spec,
        out_shape=jax.ShapeDtypeStruct((n, d), F32),
        compiler_params=pltpu.CompilerParams(
            dimension_semantics=("arbitrary",), vmem_limit_bytes=vmem + 4 * MIB),
        name="expert_pairs",
    )(item_tile, item_ea, item_eb, item_lo, item_hi, rows_sorted, g_ffn.reshape(1, d),
      g_final.reshape(1, d), wg, wu, wd, wg, wu, wd)


def kernel(x, g_mix, w_in, w_pool, pool_scale, w_fourier, w_out, g_ffn, w_group_router,
           b_group_router, w_expert_router, b_expert_router, w_gate, w_up, w_down, g_final):
    b, s, d = x.shape
    assert d == D_MODEL and s % SEQ_TILE == 0
    assert (b * s) % max(TOKEN_TILE, PERM_TILE) == 0
    n = b * s
    x2 = x.reshape(n, d)

    c0, s0 = _twiddle(np.arange(SEQ_TILE), np.arange(s), s)
    cph, sph = _twiddle(np.arange(0, s, SEQ_TILE), np.arange(s), s)
    band = _pool_band(s, SEQ_TILE, POOL_HALO)
    n_tiles = s // SEQ_TILE

    cw, sw = _fourier_weights(w_fourier, s)
    u = _norm_proj(x2, g_mix, w_in.astype(BF16))
    mixed = _mix(u.reshape(b, s, d), jnp.asarray(band).astype(BF16), jnp.asarray(c0), jnp.asarray(s0),
                 jnp.asarray(cph).reshape(n_tiles, 1, s), jnp.asarray(sph).reshape(n_tiles, 1, s),
                 w_pool.astype(BF16), pool_scale, cw, sw)

    wr = jnp.concatenate([w_group_router, w_expert_router], axis=1)
    wr = jnp.pad(wr, ((0, 0), (0, ROUTER_LANES - wr.shape[1])))
    wr_hi = wr.astype(BF16)
    wr_lo = (wr - wr_hi.astype(F32)).astype(BF16)
    br = jnp.concatenate([b_group_router, b_expert_router])
    br = jnp.pad(br, (0, ROUTER_LANES - br.shape[0])).reshape(1, ROUTER_LANES)

    rows, route, counts = _out_proj_route(x2, mixed.reshape(n, d), w_out.astype(BF16), g_ffn,
                                          wr_hi, wr_lo, br)
    pos, item_tile, item_ea, item_eb, item_lo, item_hi = _routing_plan(route, counts, n)
    rows_sorted, inv_pos = _scatter_rows(pos, rows, with_inverse=True, name="permute_rows")
    out_sorted = _expert_pairs(item_tile, item_ea, item_eb, item_lo, item_hi, rows_sorted, g_ffn, g_final,
                               w_gate.astype(BF16), w_up.astype(BF16), w_down.astype(BF16))
    (out,) = _scatter_rows(inv_pos, out_sorted, with_inverse=False, name="unpermute_rows")
    return out.reshape(b, s, d)
```

```python
import functools

import numpy as np
import jax
import jax.numpy as jnp
from jax import lax
from jax.experimental import pallas as pl
from jax.experimental.pallas import tpu as pltpu

D_MODEL = 2048
POOL_WINDOWS = (2, 4, 8, 16)
N_POOL_GROUPS = len(POOL_WINDOWS)
POOL_WIDTH = D_MODEL // 2
POOL_GROUP_DIM = POOL_WIDTH // N_POOL_GROUPS
FOURIER_WIDTH = D_MODEL - POOL_WIDTH
N_FOURIER_HEADS = 4
FOURIER_HEAD_DIM = FOURIER_WIDTH // N_FOURIER_HEADS
N_EXPERT_GROUPS = 4
EXPERTS_PER_GROUP = 4
N_EXPERTS = N_EXPERT_GROUPS * EXPERTS_PER_GROUP
EXPERT_FF = D_MODEL // 4
RMS_EPS = 1e-6

LANES = 128
SUBLANES = 8
BF16_SUBLANES = 16
ROUTER_LANES = LANES
MIB = 1024 * 1024

TOKEN_TILE = 512
SEQ_TILE = 256
POOL_HALO = BF16_SUBLANES
DFT_ROWS = SEQ_TILE + BF16_SUBLANES
TWIDDLE_ROWS = 16

PAIR_SLOT_A = (0, 0, 0, 1, 1, 3)
PAIR_SLOT_B = (1, 2, 3, 3, 2, 2)
PAIRS_PER_GROUP = len(PAIR_SLOT_A)
N_CLASSES = N_EXPERT_GROUPS * PAIRS_PER_GROUP
ROW_WORDS = D_MODEL + LANES
RANK_RADIX = 128
EXPERT_TILE = 256
PERM_TILE = 1024

BF16 = jnp.bfloat16
F32 = jnp.float32


def _rmsnorm(x, g):
    ms = jnp.mean(x * x, axis=-1, keepdims=True)
    return x * lax.rsqrt(ms + RMS_EPS) * g


def _dot(a, b):
    return jnp.dot(a, b, preferred_element_type=F32)


def _twiddle(rows, cols, period):
    m = (np.asarray(rows, np.int64)[:, None] * np.asarray(cols, np.int64)[None, :]) % period
    ang = (2.0 * np.pi / period) * m.astype(np.float64)
    return np.cos(ang).astype(np.float32), np.sin(ang).astype(np.float32)


def _pool_band(seq_len, tile, halo):
    n_tiles = seq_len // tile
    out = np.zeros((3, N_POOL_GROUPS, tile, tile + 2 * halo), np.float64)
    for v, m in enumerate((0, 1, n_tiles - 1)):
        t0 = m * tile
        for g, k in enumerate(POOL_WINDOWS):
            for r in range(tile):
                t = t0 + r
                lo = max(t - (k - 1) // 2, 0)
                hi = min(t + k // 2 + 1, seq_len)
                out[v, g, r, lo - t0 + halo:hi - t0 + halo] = 1.0 / (hi - lo)
                out[v, g, r, r + halo] -= 1.0
    return out.astype(np.float32)


def _fourier_weight_kernel(cd_ref, sd_ref, w_ref, cw_ref, sw_ref, *, scale):
    w = w_ref[0]
    cw = jnp.dot(cd_ref[...], w, preferred_element_type=F32, precision=lax.Precision.HIGHEST)
    sw = jnp.dot(sd_ref[...], w, preferred_element_type=F32, precision=lax.Precision.HIGHEST)
    cw_ref[0] = (cw * scale).astype(BF16)
    sw_ref[0] = (sw * (-scale)).astype(BF16)


def _fourier_weights(w_fourier, seq_len):
    dh = FOURIER_HEAD_DIM
    cd, sd = _twiddle(np.arange(dh), np.arange(dh), dh)
    scale = 1.0 / np.sqrt(float(seq_len * dh))
    mat = pl.BlockSpec((dh, dh), lambda h: (0, 0))
    per_head = pl.BlockSpec((1, dh, dh), lambda h: (h, 0, 0))
    return pl.pallas_call(
        functools.partial(_fourier_weight_kernel, scale=scale),
        grid=(N_FOURIER_HEADS,),
        in_specs=[mat, mat, per_head],
        out_specs=[per_head, per_head],
        out_shape=[jax.ShapeDtypeStruct((N_FOURIER_HEADS, dh, dh), BF16)] * 2,
        name="fourier_weights",
    )(jnp.asarray(cd), jnp.asarray(sd), w_fourier)


def _norm_proj_kernel(x_ref, g_ref, w_ref, u_ref):
    h = _rmsnorm(x_ref[...], g_ref[...])
    u_ref[...] = _dot(h.astype(BF16), w_ref[...]).astype(BF16)


def _norm_proj(x2, g_mix, w_in_bf16):
    n, d = x2.shape
    tm = TOKEN_TILE
    vmem = 2 * tm * d * 4 + 2 * d * d * 2 + 2 * tm * d * 2 + 3 * tm * d * 4
    return pl.pallas_call(
        _norm_proj_kernel,
        grid=(n // tm,),
        in_specs=[
            pl.BlockSpec((tm, d), lambda i: (i, 0)),
            pl.BlockSpec((1, d), lambda i: (0, 0)),
            pl.BlockSpec((d, d), lambda i: (0, 0)),
        ],
        out_specs=pl.BlockSpec((tm, d), lambda i: (i, 0)),
        out_shape=jax.ShapeDtypeStruct((n, d), BF16),
        compiler_params=pltpu.CompilerParams(
            dimension_semantics=("arbitrary",), vmem_limit_bytes=vmem + 4 * MIB),
        name="norm_proj",
    )(x2, g_mix.reshape(1, d), w_in_bf16)


def _mix_kernel(uf_ref, up_lo_ref, prev_lo_ref, next_lo_ref, up_hi_ref, prev_hi_ref, next_hi_ref,
                c0_ref, s0_ref, cph_ref, sph_ref, band_lo_ref, band_hi_ref, rev_ref,
                wpool_ref, pscale_ref, cw_ref, sw_ref, lo_ref, hi_ref, lhs_ref):
    t = SEQ_TILE
    tp = DFT_ROWS
    gd = POOL_GROUP_DIM
    hd = FOURIER_HEAD_DIM

    @pl.when(pl.program_id(1) == 0)
    def _():
        cph = cph_ref[0]
        sph = sph_ref[0]
        for r0 in range(0, tp, TWIDDLE_ROWS):
            c0 = c0_ref[r0:r0 + TWIDDLE_ROWS, :]
            s0 = s0_ref[r0:r0 + TWIDDLE_ROWS, :]
            lhs_ref[r0:r0 + TWIDDLE_ROWS, :] = (cph * c0 - sph * s0).astype(BF16)
            lhs_ref[tp + r0:tp + r0 + TWIDDLE_ROWS, :] = (sph * c0 + cph * s0).astype(BF16)

    pq = _dot(lhs_ref[...], uf_ref[...])
    for h in range(N_FOURIER_HEADS):
        cols = slice(h * hd, (h + 1) * hd)
        out_cols = slice(POOL_WIDTH + h * hd, POOL_WIDTH + (h + 1) * hd)
        pc = _dot(pq[:tp, cols].astype(BF16), cw_ref[h])
        qs = _dot(pq[tp:, cols].astype(BF16), sw_ref[h])
        lo_ref[:, out_cols] = (pc + qs)[:t].astype(BF16)
        hi_ref[:, out_cols] = _dot(rev_ref[...], (pc - qs).astype(BF16)).astype(BF16)

    def pool(up_ref, prev_ref, next_ref, band_ref, out_ref):
        win = jnp.concatenate([prev_ref[...], up_ref[...], next_ref[...]], axis=0)
        for g in range(N_POOL_GROUPS):
            cols = slice(g * gd, (g + 1) * gd)
            pooled = _dot(band_ref[0, g], win[:, cols]).astype(BF16)
            y = _dot(pooled, wpool_ref[g]) * pscale_ref[:, cols]
            out_ref[:, cols] = y.astype(BF16)

    pool(up_lo_ref, prev_lo_ref, next_lo_ref, band_lo_ref, lo_ref)
    pool(up_hi_ref, prev_hi_ref, next_hi_ref, band_hi_ref, hi_ref)


def _mix(u3, w_pool_bf16, pool_scale, cw, sw):
    b, s, d = u3.shape
    t = SEQ_TILE
    tp = DFT_ROWS
    halo = POOL_HALO
    n_tiles = s // t
    n_steps = n_tiles // 2
    halo_blocks_per_tile = t // halo
    last_halo_block = s // halo - 1

    c0, s0 = _twiddle(np.arange(tp), np.arange(s), s)
    cph, sph = _twiddle(np.arange(0, s // 2, t), np.arange(s), s)
    band = jnp.asarray(_pool_band(s, t, halo)).astype(BF16)
    rev = np.zeros((t, tp), np.float32)
    rev[np.arange(t), t - np.arange(t)] = 1.0

    hi_tile = lambda m: n_tiles - 1 - m
    tile_spec = lambda tile_of: pl.BlockSpec((None, t, POOL_WIDTH), lambda m, bi: (bi, tile_of(m), 0))
    prev_spec = lambda tile_of: pl.BlockSpec(
        (None, halo, POOL_WIDTH),
        lambda m, bi: (bi, jnp.maximum(tile_of(m) * halo_blocks_per_tile - 1, 0), 0))
    next_spec = lambda tile_of: pl.BlockSpec(
        (None, halo, POOL_WIDTH),
        lambda m, bi: (bi, jnp.minimum((tile_of(m) + 1) * halo_blocks_per_tile, last_halo_block), 0))
    band_shape = (1, N_POOL_GROUPS, t, t + 2 * halo)
    const2 = lambda shape: pl.BlockSpec(shape, lambda m, bi: (0, 0))
    const3 = lambda shape: pl.BlockSpec(shape, lambda m, bi: (0, 0, 0))
    lo_tile = lambda m: m

    in_specs = [
        pl.BlockSpec((None, s, FOURIER_WIDTH), lambda m, bi: (bi, 0, 1)),
        tile_spec(lo_tile), prev_spec(lo_tile), next_spec(lo_tile),
        tile_spec(hi_tile), prev_spec(hi_tile), next_spec(hi_tile),
        const2((tp, s)), const2((tp, s)),
        pl.BlockSpec((1, 1, s), lambda m, bi: (m, 0, 0)),
        pl.BlockSpec((1, 1, s), lambda m, bi: (m, 0, 0)),
        pl.BlockSpec(band_shape, lambda m, bi: (jnp.where(m == 0, 0, 1), 0, 0, 0)),
        pl.BlockSpec(band_shape, lambda m, bi: (jnp.where(m == 0, 2, 1), 0, 0, 0)),
        const2((t, tp)),
        const3((N_POOL_GROUPS, POOL_GROUP_DIM, POOL_GROUP_DIM)),
        const2((1, POOL_WIDTH)),
        const3((N_FOURIER_HEADS, FOURIER_HEAD_DIM, FOURIER_HEAD_DIM)),
        const3((N_FOURIER_HEADS, FOURIER_HEAD_DIM, FOURIER_HEAD_DIM)),
    ]
    half_out = pl.BlockSpec((None, t, d), lambda m, bi: (bi, m, 0))
    hi_out = pl.BlockSpec((None, t, d), lambda m, bi: (bi, n_steps - 1 - m, 0))
    vmem = (2 * s * FOURIER_WIDTH * 2
            + 2 * 2 * tp * s * 4
            + 2 * tp * s * 2
            + 2 * 2 * t * d * 2 * 2
            + 4 * 2 * tp * FOURIER_WIDTH * 4)
    return pl.pallas_call(
        _mix_kernel,
        grid=(n_steps, b),
        in_specs=in_specs,
        out_specs=[half_out, hi_out],
        out_shape=[jax.ShapeDtypeStruct((b, s // 2, d), BF16)] * 2,
        scratch_shapes=[pltpu.VMEM((2 * tp, s), BF16)],
        compiler_params=pltpu.CompilerParams(
            dimension_semantics=("arbitrary", "arbitrary"), vmem_limit_bytes=vmem + 4 * MIB),
        name="seq_mix",
    )(u3, u3, u3, u3, u3, u3, u3, jnp.asarray(c0), jnp.asarray(s0),
      jnp.asarray(cph).reshape(n_steps, 1, s), jnp.asarray(sph).reshape(n_steps, 1, s),
      band, band, jnp.asarray(rev).astype(BF16),
      w_pool_bf16, pool_scale.reshape(1, POOL_WIDTH), cw, sw)


def _out_proj_route_kernel(x_ref, mixed_lo_ref, mixed_hi_ref, wout_ref, g_ref, wr_hi_ref, wr_lo_ref, br_ref,
                           tri_ref, pick_ref, rows_ref, route_ref, counts_ref, carry_ref, *, tiles_per_seq):
    @pl.when(pl.program_id(0) == 0)
    def _():
        carry_ref[...] = jnp.zeros_like(carry_ref)

    in_lo_half = (pl.program_id(0) % tiles_per_seq) < tiles_per_seq // 2
    mixed = jnp.where(in_lo_half, mixed_lo_ref[...], mixed_hi_ref[...])
    x1 = x_ref[...] + _dot(mixed, wout_ref[...])
    rows_ref[:, :D_MODEL] = x1
    h2 = _rmsnorm(x1, g_ref[...])
    h_hi = h2.astype(BF16)
    h_lo = (h2 - h_hi.astype(F32)).astype(BF16)
    logits = (_dot(h_hi, wr_hi_ref[...]) + _dot(h_hi, wr_lo_ref[...])
              + _dot(h_lo, wr_hi_ref[...]) + br_ref[...])

    lane = lax.broadcasted_iota(jnp.int32, logits.shape, 1)
    neg = jnp.float32(-jnp.inf)
    big = jnp.int32(ROUTER_LANES)

    is_group = lane < N_EXPERT_GROUPS
    gl = jnp.where(is_group, logits, neg)
    gmax = jnp.max(gl, axis=-1, keepdims=True)
    gidx = jnp.min(jnp.where(gl == gmax, lane, big), axis=-1, keepdims=True)
    p_g = 1.0 / jnp.sum(jnp.exp(gl - gmax), axis=-1, keepdims=True)

    e_lane = lane - N_EXPERT_GROUPS
    in_group = (e_lane >= gidx * EXPERTS_PER_GROUP) & (e_lane < (gidx + 1) * EXPERTS_PER_GROUP)
    el = jnp.where(in_group, logits, neg)
    v1 = jnp.max(el, axis=-1, keepdims=True)
    i1 = jnp.min(jnp.where(el == v1, lane, big), axis=-1, keepdims=True)
    el2 = jnp.where(lane == i1, neg, el)
    v2 = jnp.max(el2, axis=-1, keepdims=True)
    i2 = jnp.min(jnp.where(el2 == v2, lane, big), axis=-1, keepdims=True)
    r = jnp.exp(v2 - v1)
    w1 = p_g / (1.0 + r)
    w2 = p_g * r / (1.0 + r)

    first_is_low = i1 < i2
    first_lane = N_EXPERT_GROUPS + gidx * EXPERTS_PER_GROUP
    la = jnp.where(first_is_low, i1, i2) - first_lane
    lb = jnp.where(first_is_low, i2, i1) - first_lane
    w_low = jnp.where(first_is_low, w1, w2)
    w_high = jnp.where(first_is_low, w2, w1)
    pair = jnp.where(la == 0, lb - 1, jnp.where(la == 1, 6 - lb, 5))
    slot_a_is_high = la == 2
    w_a = jnp.where(slot_a_is_high, w_high, w_low)
    w_b = jnp.where(slot_a_is_high, w_low, w_high)
    cls = gidx * PAIRS_PER_GROUP + pair
    rows_ref[:, D_MODEL:] = jnp.where(lane == 0, w_a, jnp.where(lane == 1, w_b, 0.0))

    onehot = jnp.where(lane == cls, 1.0, 0.0)
    before = _dot(tri_ref[...], onehot.astype(BF16)) + carry_ref[...]
    rank = jnp.sum(jnp.where(lane == cls, before, 0.0), axis=-1, keepdims=True)
    carry_ref[...] += jnp.sum(onehot, axis=0, keepdims=True)
    counts_ref[...] = carry_ref[...]

    rank_hi = jnp.floor(rank * (1.0 / RANK_RADIX))
    rank_lo = rank - rank_hi * RANK_RADIX
    digits = jnp.where(lane == 0, cls.astype(F32),
                       jnp.where(lane == 1, rank_hi, jnp.where(lane == 2, rank_lo, 0.0)))
    route_ref[...] = lax.dot_general(pick_ref[...], digits.astype(BF16), (((1,), (1,)), ((), ())),
                                     preferred_element_type=F32)


def _out_proj_route(x2, mixed_lo, mixed_hi, seq_len, w_out_bf16, g_ffn, wr_hi, wr_lo, br):
    n, d = x2.shape
    tm = TOKEN_TILE
    tiles_per_seq = seq_len // tm
    half_tiles = tiles_per_seq // 2
    row = lambda width: pl.BlockSpec((tm, width), lambda i: (i, 0))
    const = lambda shape: pl.BlockSpec(shape, lambda i: (0, 0))
    lo_spec = pl.BlockSpec(
        (tm, d), lambda i: ((i // tiles_per_seq) * half_tiles + jnp.minimum(i % tiles_per_seq, half_tiles - 1), 0))
    hi_spec = pl.BlockSpec(
        (tm, d), lambda i: ((i // tiles_per_seq) * half_tiles + jnp.maximum(i % tiles_per_seq - half_tiles, 0), 0))
    tri = np.tril(np.ones((tm, tm), np.float32), -1)
    pick = np.eye(SUBLANES, ROUTER_LANES, dtype=np.float32)
    vmem = (2 * tm * d * 4 + 4 * tm * d * 2 + 2 * tm * ROW_WORDS * 4 + 2 * d * d * 2
            + 4 * d * ROUTER_LANES * 2 + 2 * tm * tm * 2 + 4 * tm * d * 4)
    return pl.pallas_call(
        functools.partial(_out_proj_route_kernel, tiles_per_seq=tiles_per_seq),
        grid=(n // tm,),
        in_specs=[row(d), lo_spec, hi_spec, const((d, d)), const((1, d)),
                  const((d, ROUTER_LANES)), const((d, ROUTER_LANES)), const((1, ROUTER_LANES)),
                  const((tm, tm)), const((SUBLANES, ROUTER_LANES))],
        out_specs=[row(ROW_WORDS), pl.BlockSpec((SUBLANES, tm), lambda i: (0, i)),
                   const((1, ROUTER_LANES))],
        out_shape=[jax.ShapeDtypeStruct((n, ROW_WORDS), F32),
                   jax.ShapeDtypeStruct((SUBLANES, n), F32),
                   jax.ShapeDtypeStruct((1, ROUTER_LANES), F32)],
        scratch_shapes=[pltpu.VMEM((1, ROUTER_LANES), F32)],
        compiler_params=pltpu.CompilerParams(
            dimension_semantics=("arbitrary",), vmem_limit_bytes=vmem + 4 * MIB),
        name="out_proj_route",
    )(x2, mixed_lo, mixed_hi, w_out_bf16, g_ffn.reshape(1, d), wr_hi, wr_lo, br,
      jnp.asarray(tri).astype(BF16), jnp.asarray(pick).astype(BF16))


def _routing_plan(route, counts, n):
    tm = EXPERT_TILE
    cnt = counts[0, :N_CLASSES].astype(jnp.int32)
    class_end = jnp.cumsum(cnt)
    class_start = class_end - cnt
    cls = route[0].astype(jnp.int32)
    rank = (route[1] * RANK_RADIX + route[2]).astype(jnp.int32)
    pos = class_start[cls] + rank

    first_tile = class_start // tm
    items_per_class = jnp.where(cnt > 0, (class_end - 1) // tm - first_tile + 1, 0)
    item_end = jnp.cumsum(items_per_class)
    item_start = item_end - items_per_class
    n_items = n // tm + N_CLASSES
    item = jnp.arange(n_items, dtype=jnp.int32)
    used = item < item_end[-1]
    item_c = jnp.minimum(item, item_end[-1] - 1)
    item_cls = jnp.sum((item_c[:, None] >= item_end[None, :]).astype(jnp.int32), axis=1)
    item_tile = first_tile[item_cls] + item_c - item_start[item_cls]
    lo = jnp.clip(class_start[item_cls] - item_tile * tm, 0, tm)
    hi = jnp.clip(class_end[item_cls] - item_tile * tm, 0, tm)
    hi = jnp.where(used, hi, lo)
    group_base = np.repeat(np.arange(N_EXPERT_GROUPS) * EXPERTS_PER_GROUP, PAIRS_PER_GROUP)
    class_ea = jnp.asarray(group_base + np.tile(PAIR_SLOT_A, N_EXPERT_GROUPS), jnp.int32)
    class_eb = jnp.asarray(group_base + np.tile(PAIR_SLOT_B, N_EXPERT_GROUPS), jnp.int32)
    return pos, item_tile, class_ea[item_cls], class_eb[item_cls], lo, hi


def _row_copy(src_ref, src_row, dst_ref, dst_row, sem):
    return pltpu.make_async_copy(src_ref.at[pl.ds(src_row, 1), :], dst_ref.at[pl.ds(dst_row, 1), :], sem)


def _scatter_rows_kernel(pos_ref, rows_ref, dst_ref, *rest, with_inverse):
    sem = rest[-1]
    tm = PERM_TILE
    base = pl.program_id(0) * tm

    def issue(r, carry):
        p = pos_ref[base + r]
        if with_inverse:
            rest[0][p] = base + r
        _row_copy(rows_ref, r, dst_ref, p, sem).start()
        return carry

    lax.fori_loop(0, tm, issue, 0, unroll=8)
    pltpu.make_async_copy(rows_ref, dst_ref.at[pl.ds(0, tm), :], sem).wait()


def _scatter_rows(pos, rows, *, with_inverse, name):
    n, w = rows.shape
    tm = PERM_TILE
    out_specs = [pl.BlockSpec(memory_space=pl.ANY)]
    out_shape = [jax.ShapeDtypeStruct((n, w), rows.dtype)]
    if with_inverse:
        out_specs.append(pl.BlockSpec(memory_space=pltpu.SMEM))
        out_shape.append(jax.ShapeDtypeStruct((n,), jnp.int32))
    grid_spec = pltpu.PrefetchScalarGridSpec(
        num_scalar_prefetch=1,
        grid=(n // tm,),
        in_specs=[pl.BlockSpec((tm, w), lambda i, pos_ref: (i, 0))],
        out_specs=out_specs,
        scratch_shapes=[pltpu.SemaphoreType.DMA(())],
    )
    return pl.pallas_call(
        functools.partial(_scatter_rows_kernel, with_inverse=with_inverse),
        grid_spec=grid_spec,
        out_shape=out_shape,
        compiler_params=pltpu.CompilerParams(
            dimension_semantics=("arbitrary",), vmem_limit_bytes=2 * tm * w * 4 + 4 * MIB),
        name=name,
    )(pos, rows)


def _expert_pair_kernel(tile_ref, ea_ref, eb_ref, lo_ref, hi_ref, rows_ref, gffn_ref, gfin_ref,
                        wga_ref, wua_ref, wda_ref, wgb_ref, wub_ref, wdb_ref, out_ref):
    del tile_ref, ea_ref, eb_ref
    lo = lo_ref[pl.program_id(0)]
    hi = hi_ref[pl.program_id(0)]

    @pl.when(hi > lo)
    def _():
        x1 = rows_ref[:, :D_MODEL]
        wts = rows_ref[:, D_MODEL:]
        lane = lax.broadcasted_iota(jnp.int32, wts.shape, 1)
        w_a = jnp.sum(jnp.where(lane == 0, wts, 0.0), axis=-1, keepdims=True)
        w_b = jnp.sum(jnp.where(lane == 1, wts, 0.0), axis=-1, keepdims=True)
        h = _rmsnorm(x1, gffn_ref[...]).astype(BF16)

        def gated(wg_ref, wu_ref, w):
            a = _dot(h, wg_ref[0])
            v = _dot(h, wu_ref[0])
            return (a * (1.0 / (1.0 + jnp.exp(-a))) * v * w).astype(BF16)

        y = (_dot(gated(wga_ref, wua_ref, w_a), wda_ref[0])
             + _dot(gated(wgb_ref, wub_ref, w_b), wdb_ref[0]))
        res = _rmsnorm(x1 + y, gfin_ref[...])

        row = lax.broadcasted_iota(jnp.int32, (EXPERT_TILE, 1), 0)
        mine = (row >= lo) & (row < hi)

        @pl.when(lo == 0)
        def _():
            out_ref[...] = jnp.where(mine, res, 0.0)

        @pl.when(lo > 0)
        def _():
            out_ref[...] = jnp.where(mine, res, out_ref[...])


def _expert_pairs(item_tile, item_ea, item_eb, item_lo, item_hi, rows_sorted, g_ffn, g_final, wg, wu, wd):
    n, w = rows_sorted.shape
    d, f = D_MODEL, EXPERT_FF
    tm = EXPERT_TILE
    gate_a = pl.BlockSpec((1, d, f), lambda j, t, ea, eb, lo, hi: (ea[j], 0, 0))
    gate_b = pl.BlockSpec((1, d, f), lambda j, t, ea, eb, lo, hi: (eb[j], 0, 0))
    down_a = pl.BlockSpec((1, f, d), lambda j, t, ea, eb, lo, hi: (ea[j], 0, 0))
    down_b = pl.BlockSpec((1, f, d), lambda j, t, ea, eb, lo, hi: (eb[j], 0, 0))
    gain = pl.BlockSpec((1, d), lambda j, t, ea, eb, lo, hi: (0, 0))
    grid_spec = pltpu.PrefetchScalarGridSpec(
        num_scalar_prefetch=5,
        grid=(item_tile.shape[0],),
        in_specs=[pl.BlockSpec((tm, w), lambda j, t, ea, eb, lo, hi: (t[j], 0)), gain, gain,
                  gate_a, gate_a, down_a, gate_b, gate_b, down_b],
        out_specs=pl.BlockSpec((tm, d), lambda j, t, ea, eb, lo, hi: (t[j], 0)),
    )
    vmem = 2 * 6 * d * f * 2 + 2 * tm * w * 4 + 2 * tm * d * 4 + 8 * tm * d * 4
    return pl.pallas_call(
        _expert_pair_kernel,
        grid_spec=grid_spec,
        out_shape=jax.ShapeDtypeStruct((n, d), F32),
        compiler_params=pltpu.CompilerParams(
            dimension_semantics=("arbitrary",), vmem_limit_bytes=vmem + 4 * MIB),
        name="expert_pairs",
    )(item_tile, item_ea, item_eb, item_lo, item_hi, rows_sorted, g_ffn.reshape(1, d),
      g_final.reshape(1, d), wg, wu, wd, wg, wu, wd)


def kernel(x, g_mix, w_in, w_pool, pool_scale, w_fourier, w_out, g_ffn, w_group_router,
           b_group_router, w_expert_router, b_expert_router, w_gate, w_up, w_down, g_final):
    b, s, d = x.shape
    assert d == D_MODEL and s % (2 * SEQ_TILE) == 0 and s % (2 * TOKEN_TILE) == 0
    assert (b * s) % max(TOKEN_TILE, PERM_TILE) == 0
    n = b * s
    x2 = x.reshape(n, d)

    cw, sw = _fourier_weights(w_fourier, s)
    u = _norm_proj(x2, g_mix, w_in.astype(BF16))
    mixed_lo, mixed_hi = _mix(u.reshape(b, s, d), w_pool.astype(BF16), pool_scale, cw, sw)

    wr = jnp.concatenate([w_group_router, w_expert_router], axis=1)
    wr = jnp.pad(wr, ((0, 0), (0, ROUTER_LANES - wr.shape[1])))
    wr_hi = wr.astype(BF16)
    wr_lo = (wr - wr_hi.astype(F32)).astype(BF16)
    br = jnp.concatenate([b_group_router, b_expert_router])
    br = jnp.pad(br, (0, ROUTER_LANES - br.shape[0])).reshape(1, ROUTER_LANES)

    rows, route, counts = _out_proj_route(x2, mixed_lo.reshape(n // 2, d), mixed_hi.reshape(n // 2, d), s,
                                          w_out.astype(BF16), g_ffn, wr_hi, wr_lo, br)
    pos, item_tile, item_ea, item_eb, item_lo, item_hi = _routing_plan(route, counts, n)
    rows_sorted, inv_pos = _scatter_rows(pos, rows, with_inverse=True, name="permute_rows")
    out_sorted = _expert_pairs(item_tile, item_ea, item_eb, item_lo, item_hi, rows_sorted, g_ffn, g_final,
                               w_gate.astype(BF16), w_up.astype(BF16), w_down.astype(BF16))
    (out,) = _scatter_rows(inv_pos, out_sorted, with_inverse=False, name="unpermute_rows")
    return out.reshape(b, s, d)
```

```python
import functools

import numpy as np
import jax
import jax.numpy as jnp
from jax import lax
from jax.experimental import pallas as pl
from jax.experimental.pallas import tpu as pltpu

D_MODEL = 2048
POOL_WINDOWS = (2, 4, 8, 16)
N_POOL_GROUPS = len(POOL_WINDOWS)
POOL_WIDTH = D_MODEL // 2
POOL_GROUP_DIM = POOL_WIDTH // N_POOL_GROUPS
FOURIER_WIDTH = D_MODEL - POOL_WIDTH
N_FOURIER_HEADS = 4
FOURIER_HEAD_DIM = FOURIER_WIDTH // N_FOURIER_HEADS
N_EXPERT_GROUPS = 4
EXPERTS_PER_GROUP = 4
N_EXPERTS = N_EXPERT_GROUPS * EXPERTS_PER_GROUP
EXPERT_FF = D_MODEL // 4
RMS_EPS = 1e-6

LANES = 128
SUBLANES = 8
BF16_SUBLANES = 16
ROUTER_LANES = LANES
MIB = 1024 * 1024

TOKEN_TILE = 512
SEQ_TILE = 256
POOL_HALO = BF16_SUBLANES
DFT_ROWS = SEQ_TILE + BF16_SUBLANES
TWIDDLE_ROWS = 16

PAIR_SLOT_A = (0, 0, 0, 1, 1, 3)
PAIR_SLOT_B = (1, 2, 3, 3, 2, 2)
PAIRS_PER_GROUP = len(PAIR_SLOT_A)
N_CLASSES = N_EXPERT_GROUPS * PAIRS_PER_GROUP
ROW_WORDS = D_MODEL + LANES
RANK_RADIX = 128
EXPERT_TILE = 256
PERM_TILE = 1024

BF16 = jnp.bfloat16
F32 = jnp.float32


def _rmsnorm(x, g):
    ms = jnp.mean(x * x, axis=-1, keepdims=True)
    return x * lax.rsqrt(ms + RMS_EPS) * g


def _dot(a, b):
    return jnp.dot(a, b, preferred_element_type=F32)


def _twiddle(rows, cols, period):
    m = (np.asarray(rows, np.int64)[:, None] * np.asarray(cols, np.int64)[None, :]) % period
    ang = (2.0 * np.pi / period) * m.astype(np.float64)
    return np.cos(ang).astype(np.float32), np.sin(ang).astype(np.float32)


def _pool_band(seq_len, tile, halo):
    n_tiles = seq_len // tile
    out = np.zeros((3, N_POOL_GROUPS, tile, tile + 2 * halo), np.float64)
    for v, m in enumerate((0, 1, n_tiles - 1)):
        t0 = m * tile
        for g, k in enumerate(POOL_WINDOWS):
            for r in range(tile):
                t = t0 + r
                lo = max(t - (k - 1) // 2, 0)
                hi = min(t + k // 2 + 1, seq_len)
                out[v, g, r, lo - t0 + halo:hi - t0 + halo] = 1.0 / (hi - lo)
                out[v, g, r, r + halo] -= 1.0
    return out.astype(np.float32)


def _fourier_weight_kernel(cd_ref, sd_ref, w_ref, cw_ref, sw_ref, *, scale):
    w = w_ref[0]
    cw = jnp.dot(cd_ref[...], w, preferred_element_type=F32, precision=lax.Precision.HIGHEST)
    sw = jnp.dot(sd_ref[...], w, preferred_element_type=F32, precision=lax.Precision.HIGHEST)
    cw_ref[0] = (cw * scale).astype(BF16)
    sw_ref[0] = (sw * (-scale)).astype(BF16)


def _fourier_weights(w_fourier, seq_len):
    dh = FOURIER_HEAD_DIM
    cd, sd = _twiddle(np.arange(dh), np.arange(dh), dh)
    scale = 1.0 / np.sqrt(float(seq_len * dh))
    mat = pl.BlockSpec((dh, dh), lambda h: (0, 0))
    per_head = pl.BlockSpec((1, dh, dh), lambda h: (h, 0, 0))
    return pl.pallas_call(
        functools.partial(_fourier_weight_kernel, scale=scale),
        grid=(N_FOURIER_HEADS,),
        in_specs=[mat, mat, per_head],
        out_specs=[per_head, per_head],
        out_shape=[jax.ShapeDtypeStruct((N_FOURIER_HEADS, dh, dh), BF16)] * 2,
        name="fourier_weights",
    )(jnp.asarray(cd), jnp.asarray(sd), w_fourier)


def _norm_proj_kernel(x_ref, g_ref, w_ref, u_ref):
    h = _rmsnorm(x_ref[...], g_ref[...])
    u_ref[...] = _dot(h.astype(BF16), w_ref[...]).astype(BF16)


def _norm_proj(x2, g_mix, w_in_bf16):
    n, d = x2.shape
    tm = TOKEN_TILE
    vmem = 2 * tm * d * 4 + 2 * d * d * 2 + 2 * tm * d * 2 + 3 * tm * d * 4
    return pl.pallas_call(
        _norm_proj_kernel,
        grid=(n // tm,),
        in_specs=[
            pl.BlockSpec((tm, d), lambda i: (i, 0)),
            pl.BlockSpec((1, d), lambda i: (0, 0)),
            pl.BlockSpec((d, d), lambda i: (0, 0)),
        ],
        out_specs=pl.BlockSpec((tm, d), lambda i: (i, 0)),
        out_shape=jax.ShapeDtypeStruct((n, d), BF16),
        compiler_params=pltpu.CompilerParams(
            dimension_semantics=("arbitrary",), vmem_limit_bytes=vmem + 4 * MIB),
        name="norm_proj",
    )(x2, g_mix.reshape(1, d), w_in_bf16)


def _mix_kernel(uf_ref, up_lo_ref, prev_lo_ref, next_lo_ref, up_hi_ref, prev_hi_ref, next_hi_ref,
                c0_ref, s0_ref, cph_ref, sph_ref, band_lo_ref, band_hi_ref, rev_ref,
                wpool_ref, pscale_ref, cw_ref, sw_ref, lo_ref, hi_ref, lhs_ref):
    t = SEQ_TILE
    tp = DFT_ROWS
    gd = POOL_GROUP_DIM
    hd = FOURIER_HEAD_DIM

    @pl.when(pl.program_id(1) == 0)
    def _():
        cph = cph_ref[0]
        sph = sph_ref[0]
        for r0 in range(0, tp, TWIDDLE_ROWS):
            c0 = c0_ref[r0:r0 + TWIDDLE_ROWS, :]
            s0 = s0_ref[r0:r0 + TWIDDLE_ROWS, :]
            lhs_ref[r0:r0 + TWIDDLE_ROWS, :] = (cph * c0 - sph * s0).astype(BF16)
            lhs_ref[tp + r0:tp + r0 + TWIDDLE_ROWS, :] = (sph * c0 + cph * s0).astype(BF16)

    pq = _dot(lhs_ref[...], uf_ref[...])
    for h in range(N_FOURIER_HEADS):
        cols = slice(h * hd, (h + 1) * hd)
        out_cols = slice(POOL_WIDTH + h * hd, POOL_WIDTH + (h + 1) * hd)
        pc = _dot(pq[:tp, cols].astype(BF16), cw_ref[h])
        qs = _dot(pq[tp:, cols].astype(BF16), sw_ref[h])
        lo_ref[:, out_cols] = (pc + qs)[:t].astype(BF16)
        hi_ref[:, out_cols] = _dot(rev_ref[...], (pc - qs).astype(BF16)).astype(BF16)

    def pool(up_ref, prev_ref, next_ref, band_ref, out_ref):
        win = jnp.concatenate([prev_ref[...], up_ref[...], next_ref[...]], axis=0)
        for g in range(N_POOL_GROUPS):
            cols = slice(g * gd, (g + 1) * gd)
            pooled = _dot(band_ref[0, g], win[:, cols]).astype(BF16)
            y = _dot(pooled, wpool_ref[g]) * pscale_ref[:, cols]
            out_ref[:, cols] = y.astype(BF16)

    pool(up_lo_ref, prev_lo_ref, next_lo_ref, band_lo_ref, lo_ref)
    pool(up_hi_ref, prev_hi_ref, next_hi_ref, band_hi_ref, hi_ref)


def _mix(u3, w_pool_bf16, pool_scale, cw, sw):
    b, s, d = u3.shape
    t = SEQ_TILE
    tp = DFT_ROWS
    halo = POOL_HALO
    n_tiles = s // t
    n_steps = n_tiles // 2
    halo_blocks_per_tile = t // halo
    last_halo_block = s // halo - 1

    c0, s0 = _twiddle(np.arange(tp), np.arange(s), s)
    cph, sph = _twiddle(np.arange(0, s // 2, t), np.arange(s), s)
    band = jnp.asarray(_pool_band(s, t, halo)).astype(BF16)
    rev = np.zeros((t, tp), np.float32)
    rev[np.arange(t), t - np.arange(t)] = 1.0

    hi_tile = lambda m: n_tiles - 1 - m
    tile_spec = lambda tile_of: pl.BlockSpec((None, t, POOL_WIDTH), lambda m, bi: (bi, tile_of(m), 0))
    prev_spec = lambda tile_of: pl.BlockSpec(
        (None, halo, POOL_WIDTH),
        lambda m, bi: (bi, jnp.maximum(tile_of(m) * halo_blocks_per_tile - 1, 0), 0))
    next_spec = lambda tile_of: pl.BlockSpec(
        (None, halo, POOL_WIDTH),
        lambda m, bi: (bi, jnp.minimum((tile_of(m) + 1) * halo_blocks_per_tile, last_halo_block), 0))
    band_shape = (1, N_POOL_GROUPS, t, t + 2 * halo)
    const2 = lambda shape: pl.BlockSpec(shape, lambda m, bi: (0, 0))
    const3 = lambda shape: pl.BlockSpec(shape, lambda m, bi: (0, 0, 0))
    lo_tile = lambda m: m

    in_specs = [
        pl.BlockSpec((None, s, FOURIER_WIDTH), lambda m, bi: (bi, 0, 1)),
        tile_spec(lo_tile), prev_spec(lo_tile), next_spec(lo_tile),
        tile_spec(hi_tile), prev_spec(hi_tile), next_spec(hi_tile),
        const2((tp, s)), const2((tp, s)),
        pl.BlockSpec((1, 1, s), lambda m, bi: (m, 0, 0)),
        pl.BlockSpec((1, 1, s), lambda m, bi: (m, 0, 0)),
        pl.BlockSpec(band_shape, lambda m, bi: (jnp.where(m == 0, 0, 1), 0, 0, 0)),
        pl.BlockSpec(band_shape, lambda m, bi: (jnp.where(m == 0, 2, 1), 0, 0, 0)),
        const2((t, tp)),
        const3((N_POOL_GROUPS, POOL_GROUP_DIM, POOL_GROUP_DIM)),
        const2((1, POOL_WIDTH)),
        const3((N_FOURIER_HEADS, FOURIER_HEAD_DIM, FOURIER_HEAD_DIM)),
        const3((N_FOURIER_HEADS, FOURIER_HEAD_DIM, FOURIER_HEAD_DIM)),
    ]
    half_out = pl.BlockSpec((None, t, d), lambda m, bi: (bi, m, 0))
    hi_out = pl.BlockSpec((None, t, d), lambda m, bi: (bi, n_steps - 1 - m, 0))
    vmem = (2 * s * FOURIER_WIDTH * 2
            + 2 * 2 * tp * s * 4
            + 2 * tp * s * 2
            + 2 * 2 * t * d * 2 * 2
            + 4 * 2 * tp * FOURIER_WIDTH * 4)
    return pl.pallas_call(
        _mix_kernel,
        grid=(n_steps, b),
        in_specs=in_specs,
        out_specs=[half_out, hi_out],
        out_shape=[jax.ShapeDtypeStruct((b, s // 2, d), BF16)] * 2,
        scratch_shapes=[pltpu.VMEM((2 * tp, s), BF16)],
        compiler_params=pltpu.CompilerParams(
            dimension_semantics=("arbitrary", "arbitrary"), vmem_limit_bytes=vmem + 4 * MIB),
        name="seq_mix",
    )(u3, u3, u3, u3, u3, u3, u3, jnp.asarray(c0), jnp.asarray(s0),
      jnp.asarray(cph).reshape(n_steps, 1, s), jnp.asarray(sph).reshape(n_steps, 1, s),
      band, band, jnp.asarray(rev).astype(BF16),
      w_pool_bf16, pool_scale.reshape(1, POOL_WIDTH), cw, sw)


def _out_proj_route_kernel(x_ref, mixed_lo_ref, mixed_hi_ref, wout_ref, g_ref, wr_ref, br_ref,
                           tri_ref, pick_ref, rows_ref, route_ref, counts_ref, carry_ref, x1_ref,
                           *, tiles_per_seq, n_tiles):
    step = pl.program_id(0)

    @pl.when(step == 0)
    def _():
        carry_ref[...] = jnp.zeros_like(carry_ref)
        x1_ref[...] = jnp.zeros_like(x1_ref)

    x1 = x1_ref[...]
    rows_ref[:, :D_MODEL] = x1
    h2 = _rmsnorm(x1, g_ref[...])
    logits = _dot(h2.astype(BF16), wr_ref[...]) + br_ref[...]

    tile = jnp.minimum(step, n_tiles - 1)
    in_lo_half = (tile % tiles_per_seq) < tiles_per_seq // 2
    mixed = jnp.where(in_lo_half, mixed_lo_ref[...], mixed_hi_ref[...])
    x1_ref[...] = x_ref[...] + _dot(mixed, wout_ref[...])

    lane = lax.broadcasted_iota(jnp.int32, logits.shape, 1)
    neg = jnp.float32(-jnp.inf)
    big = jnp.int32(ROUTER_LANES)

    is_group = lane < N_EXPERT_GROUPS
    gl = jnp.where(is_group, logits, neg)
    gmax = jnp.max(gl, axis=-1, keepdims=True)
    gidx = jnp.min(jnp.where(gl == gmax, lane, big), axis=-1, keepdims=True)
    p_g = 1.0 / jnp.sum(jnp.exp(gl - gmax), axis=-1, keepdims=True)

    e_lane = lane - N_EXPERT_GROUPS
    in_group = (e_lane >= gidx * EXPERTS_PER_GROUP) & (e_lane < (gidx + 1) * EXPERTS_PER_GROUP)
    el = jnp.where(in_group, logits, neg)
    v1 = jnp.max(el, axis=-1, keepdims=True)
    i1 = jnp.min(jnp.where(el == v1, lane, big), axis=-1, keepdims=True)
    el2 = jnp.where(lane == i1, neg, el)
    v2 = jnp.max(el2, axis=-1, keepdims=True)
    i2 = jnp.min(jnp.where(el2 == v2, lane, big), axis=-1, keepdims=True)
    r = jnp.exp(v2 - v1)
    w1 = p_g / (1.0 + r)
    w2 = p_g * r / (1.0 + r)

    first_is_low = i1 < i2
    first_lane = N_EXPERT_GROUPS + gidx * EXPERTS_PER_GROUP
    la = jnp.where(first_is_low, i1, i2) - first_lane
    lb = jnp.where(first_is_low, i2, i1) - first_lane
    w_low = jnp.where(first_is_low, w1, w2)
    w_high = jnp.where(first_is_low, w2, w1)
    pair = jnp.where(la == 0, lb - 1, jnp.where(la == 1, 6 - lb, 5))
    slot_a_is_high = la == 2
    w_a = jnp.where(slot_a_is_high, w_high, w_low)
    w_b = jnp.where(slot_a_is_high, w_low, w_high)
    cls = gidx * PAIRS_PER_GROUP + pair
    rows_ref[:, D_MODEL:] = jnp.where(lane == 0, w_a, jnp.where(lane == 1, w_b, 0.0))

    onehot = jnp.where(lane == cls, 1.0, 0.0)
    before = _dot(tri_ref[...], onehot.astype(BF16)) + carry_ref[...]
    rank = jnp.sum(jnp.where(lane == cls, before, 0.0), axis=-1, keepdims=True)
    carry_ref[...] += jnp.sum(onehot, axis=0, keepdims=True) * jnp.where(step > 0, 1.0, 0.0)
    counts_ref[...] = carry_ref[...]

    rank_hi = jnp.floor(rank * (1.0 / RANK_RADIX))
    rank_lo = rank - rank_hi * RANK_RADIX
    digits = jnp.where(lane == 0, cls.astype(F32),
                       jnp.where(lane == 1, rank_hi, jnp.where(lane == 2, rank_lo, 0.0)))
    route_ref[...] = lax.dot_general(pick_ref[...], digits.astype(BF16), (((1,), (1,)), ((), ())),
                                     preferred_element_type=F32)


def _out_proj_route(x2, mixed_lo, mixed_hi, seq_len, w_out_bf16, g_ffn, wr, br):
    n, d = x2.shape
    tm = TOKEN_TILE
    n_tiles = n // tm
    tiles_per_seq = seq_len // tm
    half_tiles = tiles_per_seq // 2
    const = lambda shape: pl.BlockSpec(shape, lambda i: (0, 0))
    in_tile = lambda i: jnp.minimum(i, n_tiles - 1)
    out_tile = lambda i: jnp.maximum(i - 1, 0)
    lo_spec = pl.BlockSpec((tm, d), lambda i: (
        (in_tile(i) // tiles_per_seq) * half_tiles + jnp.minimum(in_tile(i) % tiles_per_seq, half_tiles - 1), 0))
    hi_spec = pl.BlockSpec((tm, d), lambda i: (
        (in_tile(i) // tiles_per_seq) * half_tiles + jnp.maximum(in_tile(i) % tiles_per_seq - half_tiles, 0), 0))
    tri = np.tril(np.ones((tm, tm), np.float32), -1)
    pick = np.eye(SUBLANES, ROUTER_LANES, dtype=np.float32)
    vmem = (2 * tm * d * 4 + 4 * tm * d * 2 + 2 * tm * ROW_WORDS * 4 + 2 * d * d * 2 + tm * d * 4
            + 2 * d * ROUTER_LANES * 2 + 2 * tm * tm * 2 + 4 * tm * d * 4)
    return pl.pallas_call(
        functools.partial(_out_proj_route_kernel, tiles_per_seq=tiles_per_seq, n_tiles=n_tiles),
        grid=(n_tiles + 1,),
        in_specs=[pl.BlockSpec((tm, d), lambda i: (in_tile(i), 0)), lo_spec, hi_spec,
                  const((d, d)), const((1, d)), const((d, ROUTER_LANES)), const((1, ROUTER_LANES)),
                  const((tm, tm)), const((SUBLANES, ROUTER_LANES))],
        out_specs=[pl.BlockSpec((tm, ROW_WORDS), lambda i: (out_tile(i), 0)),
                   pl.BlockSpec((SUBLANES, tm), lambda i: (0, out_tile(i))),
                   const((1, ROUTER_LANES))],
        out_shape=[jax.ShapeDtypeStruct((n, ROW_WORDS), F32),
                   jax.ShapeDtypeStruct((SUBLANES, n), F32),
                   jax.ShapeDtypeStruct((1, ROUTER_LANES), F32)],
        scratch_shapes=[pltpu.VMEM((1, ROUTER_LANES), F32), pltpu.VMEM((tm, d), F32)],
        compiler_params=pltpu.CompilerParams(
            dimension_semantics=("arbitrary",), vmem_limit_bytes=vmem + 4 * MIB),
        name="out_proj_route",
    )(x2, mixed_lo, mixed_hi, w_out_bf16, g_ffn.reshape(1, d), wr, br,
      jnp.asarray(tri).astype(BF16), jnp.asarray(pick).astype(BF16))


def _routing_plan(route, counts, n):
    tm = EXPERT_TILE
    cnt = counts[0, :N_CLASSES].astype(jnp.int32)
    class_end = jnp.cumsum(cnt)
    class_start = class_end - cnt
    cls = route[0].astype(jnp.int32)
    rank = (route[1] * RANK_RADIX + route[2]).astype(jnp.int32)
    pos = class_start[cls] + rank

    first_tile = class_start // tm
    items_per_class = jnp.where(cnt > 0, (class_end - 1) // tm - first_tile + 1, 0)
    item_end = jnp.cumsum(items_per_class)
    item_start = item_end - items_per_class
    n_items = n // tm + N_CLASSES
    item = jnp.arange(n_items, dtype=jnp.int32)
    used = item < item_end[-1]
    item_c = jnp.minimum(item, item_end[-1] - 1)
    item_cls = jnp.sum((item_c[:, None] >= item_end[None, :]).astype(jnp.int32), axis=1)
    item_tile = first_tile[item_cls] + item_c - item_start[item_cls]
    lo = jnp.clip(class_start[item_cls] - item_tile * tm, 0, tm)
    hi = jnp.clip(class_end[item_cls] - item_tile * tm, 0, tm)
    hi = jnp.where(used, hi, lo)
    group_base = np.repeat(np.arange(N_EXPERT_GROUPS) * EXPERTS_PER_GROUP, PAIRS_PER_GROUP)
    class_ea = jnp.asarray(group_base + np.tile(PAIR_SLOT_A, N_EXPERT_GROUPS), jnp.int32)
    class_eb = jnp.asarray(group_base + np.tile(PAIR_SLOT_B, N_EXPERT_GROUPS), jnp.int32)
    return pos, item_tile, class_ea[item_cls], class_eb[item_cls], lo, hi


def _row_copy(src_ref, src_row, dst_ref, dst_row, sem):
    return pltpu.make_async_copy(src_ref.at[pl.ds(src_row, 1), :], dst_ref.at[pl.ds(dst_row, 1), :], sem)


def _scatter_rows_kernel(pos_ref, rows_ref, dst_ref, *rest, with_inverse):
    sem = rest[-1]
    tm = PERM_TILE
    base = pl.program_id(0) * tm

    def issue(r, carry):
        p = pos_ref[base + r]
        if with_inverse:
            rest[0][p] = base + r
        _row_copy(rows_ref, r, dst_ref, p, sem).start()
        return carry

    lax.fori_loop(0, tm, issue, 0, unroll=8)
    pltpu.make_async_copy(rows_ref, dst_ref.at[pl.ds(0, tm), :], sem).wait()


def _scatter_rows(pos, rows, *, with_inverse, name):
    n, w = rows.shape
    tm = PERM_TILE
    out_specs = [pl.BlockSpec(memory_space=pl.ANY)]
    out_shape = [jax.ShapeDtypeStruct((n, w), rows.dtype)]
    if with_inverse:
        out_specs.append(pl.BlockSpec(memory_space=pltpu.SMEM))
        out_shape.append(jax.ShapeDtypeStruct((n,), jnp.int32))
    grid_spec = pltpu.PrefetchScalarGridSpec(
        num_scalar_prefetch=1,
        grid=(n // tm,),
        in_specs=[pl.BlockSpec((tm, w), lambda i, pos_ref: (i, 0))],
        out_specs=out_specs,
        scratch_shapes=[pltpu.SemaphoreType.DMA(())],
    )
    return pl.pallas_call(
        functools.partial(_scatter_rows_kernel, with_inverse=with_inverse),
        grid_spec=grid_spec,
        out_shape=out_shape,
        compiler_params=pltpu.CompilerParams(
            dimension_semantics=("arbitrary",), vmem_limit_bytes=2 * tm * w * 4 + 4 * MIB),
        name=name,
    )(pos, rows)


def _expert_pair_kernel(tile_ref, ea_ref, eb_ref, lo_ref, hi_ref, rows_ref, gffn_ref, gfin_ref,
                        wga_ref, wua_ref, wda_ref, wgb_ref, wub_ref, wdb_ref, out_ref):
    del tile_ref, ea_ref, eb_ref
    lo = lo_ref[pl.program_id(0)]
    hi = hi_ref[pl.program_id(0)]

    @pl.when(hi > lo)
    def _():
        x1 = rows_ref[:, :D_MODEL]
        wts = rows_ref[:, D_MODEL:]
        lane = lax.broadcasted_iota(jnp.int32, wts.shape, 1)
        w_a = jnp.sum(jnp.where(lane == 0, wts, 0.0), axis=-1, keepdims=True)
        w_b = jnp.sum(jnp.where(lane == 1, wts, 0.0), axis=-1, keepdims=True)
        h = _rmsnorm(x1, gffn_ref[...]).astype(BF16)

        def gated(wg_ref, wu_ref, w):
            a = _dot(h, wg_ref[0])
            v = _dot(h, wu_ref[0])
            return (a * (1.0 / (1.0 + jnp.exp(-a))) * v * w).astype(BF16)

        y = (_dot(gated(wga_ref, wua_ref, w_a), wda_ref[0])
             + _dot(gated(wgb_ref, wub_ref, w_b), wdb_ref[0]))
        res = _rmsnorm(x1 + y, gfin_ref[...])

        row = lax.broadcasted_iota(jnp.int32, (EXPERT_TILE, 1), 0)
        mine = (row >= lo) & (row < hi)

        @pl.when(lo == 0)
        def _():
            out_ref[...] = jnp.where(mine, res, 0.0)

        @pl.when(lo > 0)
        def _():
            out_ref[...] = jnp.where(mine, res, out_ref[...])


def _expert_pairs(item_tile, item_ea, item_eb, item_lo, item_hi, rows_sorted, g_ffn, g_final, wg, wu, wd):
    n, w = rows_sorted.shape
    d, f = D_MODEL, EXPERT_FF
    tm = EXPERT_TILE
    gate_a = pl.BlockSpec((1, d, f), lambda j, t, ea, eb, lo, hi: (ea[j], 0, 0))
    gate_b = pl.BlockSpec((1, d, f), lambda j, t, ea, eb, lo, hi: (eb[j], 0, 0))
    down_a = pl.BlockSpec((1, f, d), lambda j, t, ea, eb, lo, hi: (ea[j], 0, 0))
    down_b = pl.BlockSpec((1, f, d), lambda j, t, ea, eb, lo, hi: (eb[j], 0, 0))
    gain = pl.BlockSpec((1, d), lambda j, t, ea, eb, lo, hi: (0, 0))
    grid_spec = pltpu.PrefetchScalarGridSpec(
        num_scalar_prefetch=5,
        grid=(item_tile.shape[0],),
        in_specs=[pl.BlockSpec((tm, w), lambda j, t, ea, eb, lo, hi: (t[j], 0)), gain, gain,
                  gate_a, gate_a, down_a, gate_b, gate_b, down_b],
        out_specs=pl.BlockSpec((tm, d), lambda j, t, ea, eb, lo, hi: (t[j], 0)),
    )
    vmem = 2 * 6 * d * f * 2 + 2 * tm * w * 4 + 2 * tm * d * 4 + 8 * tm * d * 4
    return pl.pallas_call(
        _expert_pair_kernel,
        grid_spec=grid_spec,
        out_shape=jax.ShapeDtypeStruct((n, d), F32),
        compiler_params=pltpu.CompilerParams(
            dimension_semantics=("arbitrary",), vmem_limit_bytes=vmem + 4 * MIB),
        name="expert_pairs",
    )(item_tile, item_ea, item_eb, item_lo, item_hi, rows_sorted, g_ffn.reshape(1, d),
      g_final.reshape(1, d), wg, wu, wd, wg, wu, wd)


def kernel(x, g_mix, w_in, w_pool, pool_scale, w_fourier, w_out, g_ffn, w_group_router,
           b_group_router, w_expert_router, b_expert_router, w_gate, w_up, w_down, g_final):
    b, s, d = x.shape
    assert d == D_MODEL and s % (2 * SEQ_TILE) == 0 and s % (2 * TOKEN_TILE) == 0
    assert (b * s) % max(TOKEN_TILE, PERM_TILE) == 0
    n = b * s
    x2 = x.reshape(n, d)

    cw, sw = _fourier_weights(w_fourier, s)
    u = _norm_proj(x2, g_mix, w_in.astype(BF16))
    mixed_lo, mixed_hi = _mix(u.reshape(b, s, d), w_pool.astype(BF16), pool_scale, cw, sw)

    wr = jnp.concatenate([w_group_router, w_expert_router], axis=1)
    wr = jnp.pad(wr, ((0, 0), (0, ROUTER_LANES - wr.shape[1]))).astype(BF16)
    br = jnp.concatenate([b_group_router, b_expert_router])
    br = jnp.pad(br, (0, ROUTER_LANES - br.shape[0])).reshape(1, ROUTER_LANES)

    rows, route, counts = _out_proj_route(x2, mixed_lo.reshape(n // 2, d), mixed_hi.reshape(n // 2, d), s,
                                          w_out.astype(BF16), g_ffn, wr, br)
    pos, item_tile, item_ea, item_eb, item_lo, item_hi = _routing_plan(route, counts, n)
    rows_sorted, inv_pos = _scatter_rows(pos, rows, with_inverse=True, name="permute_rows")
    out_sorted = _expert_pairs(item_tile, item_ea, item_eb, item_lo, item_hi, rows_sorted, g_ffn, g_final,
                               w_gate.astype(BF16), w_up.astype(BF16), w_down.astype(BF16))
    (out,) = _scatter_rows(inv_pos, out_sorted, with_inverse=False, name="unpermute_rows")
    return out.reshape(b, s, d)
```

```python
import functools

import numpy as np
import jax
import jax.numpy as jnp
from jax import lax
from jax.experimental import pallas as pl
from jax.experimental.pallas import tpu as pltpu

D_MODEL = 2048
POOL_WINDOWS = (2, 4, 8, 16)
N_POOL_GROUPS = len(POOL_WINDOWS)
POOL_WIDTH = D_MODEL // 2
POOL_GROUP_DIM = POOL_WIDTH // N_POOL_GROUPS
FOURIER_WIDTH = D_MODEL - POOL_WIDTH
N_FOURIER_HEADS = 4
FOURIER_HEAD_DIM = FOURIER_WIDTH // N_FOURIER_HEADS
N_EXPERT_GROUPS = 4
EXPERTS_PER_GROUP = 4
N_EXPERTS = N_EXPERT_GROUPS * EXPERTS_PER_GROUP
EXPERT_FF = D_MODEL // 4
RMS_EPS = 1e-6

LANES = 128
SUBLANES = 8
BF16_SUBLANES = 16
ROUTER_LANES = LANES
MIB = 1024 * 1024

TOKEN_TILE = 512
SEQ_TILE = 256
POOL_HALO = BF16_SUBLANES
DFT_ROWS = SEQ_TILE + BF16_SUBLANES
TWIDDLE_ROWS = 16

PAIR_SLOT_A = (0, 0, 0, 1, 1, 3)
PAIR_SLOT_B = (1, 2, 3, 3, 2, 2)
PAIRS_PER_GROUP = len(PAIR_SLOT_A)
N_CLASSES = N_EXPERT_GROUPS * PAIRS_PER_GROUP
ROW_WORDS = D_MODEL + LANES
RANK_RADIX = 128
EXPERT_TILE = 256
PERM_TILE = 1024

BF16 = jnp.bfloat16
F32 = jnp.float32


def _rmsnorm(x, g):
    ms = jnp.mean(x * x, axis=-1, keepdims=True)
    return x * lax.rsqrt(ms + RMS_EPS) * g


def _dot(a, b):
    return jnp.dot(a, b, preferred_element_type=F32)


def _twiddle(rows, cols, period):
    m = (np.asarray(rows, np.int64)[:, None] * np.asarray(cols, np.int64)[None, :]) % period
    ang = (2.0 * np.pi / period) * m.astype(np.float64)
    return np.cos(ang).astype(np.float32), np.sin(ang).astype(np.float32)


def _pool_band(seq_len, tile, halo):
    n_tiles = seq_len // tile
    out = np.zeros((3, N_POOL_GROUPS, tile, tile + 2 * halo), np.float64)
    for v, m in enumerate((0, 1, n_tiles - 1)):
        t0 = m * tile
        for g, k in enumerate(POOL_WINDOWS):
            for r in range(tile):
                t = t0 + r
                lo = max(t - (k - 1) // 2, 0)
                hi = min(t + k // 2 + 1, seq_len)
                out[v, g, r, lo - t0 + halo:hi - t0 + halo] = 1.0 / (hi - lo)
                out[v, g, r, r + halo] -= 1.0
    return out.astype(np.float32)


def _fourier_weight_kernel(cd_ref, sd_ref, w_ref, cw_ref, sw_ref, *, scale):
    w = w_ref[0]
    cw = jnp.dot(cd_ref[...], w, preferred_element_type=F32, precision=lax.Precision.HIGHEST)
    sw = jnp.dot(sd_ref[...], w, preferred_element_type=F32, precision=lax.Precision.HIGHEST)
    cw_ref[0] = (cw * scale).astype(BF16)
    sw_ref[0] = (sw * (-scale)).astype(BF16)


def _fourier_weights(w_fourier, seq_len):
    dh = FOURIER_HEAD_DIM
    cd, sd = _twiddle(np.arange(dh), np.arange(dh), dh)
    scale = 1.0 / np.sqrt(float(seq_len * dh))
    mat = pl.BlockSpec((dh, dh), lambda h: (0, 0))
    per_head = pl.BlockSpec((1, dh, dh), lambda h: (h, 0, 0))
    return pl.pallas_call(
        functools.partial(_fourier_weight_kernel, scale=scale),
        grid=(N_FOURIER_HEADS,),
        in_specs=[mat, mat, per_head],
        out_specs=[per_head, per_head],
        out_shape=[jax.ShapeDtypeStruct((N_FOURIER_HEADS, dh, dh), BF16)] * 2,
        name="fourier_weights",
    )(jnp.asarray(cd), jnp.asarray(sd), w_fourier)


class _CastAlong:
    def __init__(self, w, n_chunks, chunk_of):
        cols = w.shape[-1]
        rows = w.size // cols // n_chunks
        self.shape = w.shape
        self.src = w.reshape(n_chunks, rows, cols)
        self.spec = pl.BlockSpec((1, rows, cols), lambda *idx: (chunk_of(*idx), 0, 0))
        self.out_shape = jax.ShapeDtypeStruct((n_chunks, rows, cols), BF16)
        self.vmem_bytes = 2 * rows * cols * (4 + 2)


def _cast_chunks(src_refs, dst_refs):
    for src_ref, dst_ref in zip(src_refs, dst_refs):
        dst_ref[...] = src_ref[...].astype(BF16)


def _norm_proj_kernel(x_ref, g_ref, w_ref, *rest, n_cast):
    cast_src, (u_ref, *cast_dst) = rest[:n_cast], rest[n_cast:]
    h = _rmsnorm(x_ref[...], g_ref[...])
    u_ref[...] = _dot(h.astype(BF16), w_ref[...]).astype(BF16)
    _cast_chunks(cast_src, cast_dst)


def _norm_proj(x2, g_mix, w_in_bf16, cast_weights):
    n, d = x2.shape
    tm = TOKEN_TILE
    casts = [_CastAlong(w, n // tm, lambda i: i) for w in cast_weights]
    vmem = (2 * tm * d * 4 + d * d * 2 + 2 * tm * d * 2 + 3 * tm * d * 4
            + sum(c.vmem_bytes for c in casts))
    u, *cast_out = pl.pallas_call(
        functools.partial(_norm_proj_kernel, n_cast=len(casts)),
        grid=(n // tm,),
        in_specs=[
            pl.BlockSpec((tm, d), lambda i: (i, 0)),
            pl.BlockSpec((1, d), lambda i: (0, 0)),
            pl.BlockSpec((d, d), lambda i: (0, 0), pipeline_mode=pl.Buffered(1)),
        ] + [c.spec for c in casts],
        out_specs=[pl.BlockSpec((tm, d), lambda i: (i, 0))] + [c.spec for c in casts],
        out_shape=[jax.ShapeDtypeStruct((n, d), BF16)] + [c.out_shape for c in casts],
        compiler_params=pltpu.CompilerParams(
            dimension_semantics=("arbitrary",), vmem_limit_bytes=vmem + 4 * MIB),
        name="norm_proj",
    )(x2, g_mix.reshape(1, d), w_in_bf16, *[c.src for c in casts])
    return u, [o.reshape(c.shape) for o, c in zip(cast_out, casts)]


def _mix_kernel(uf_ref, up_lo_ref, prev_lo_ref, next_lo_ref, up_hi_ref, prev_hi_ref, next_hi_ref,
                c0_ref, s0_ref, cph_ref, sph_ref, band_lo_ref, band_hi_ref, rev_ref,
                wpool_ref, pscale_ref, cw_ref, sw_ref, cast_src_ref, lo_ref, hi_ref, cast_dst_ref, lhs_ref):
    t = SEQ_TILE
    tp = DFT_ROWS
    gd = POOL_GROUP_DIM
    hd = FOURIER_HEAD_DIM

    @pl.when(pl.program_id(1) == 0)
    def _():
        cph = cph_ref[0]
        sph = sph_ref[0]
        for r0 in range(0, tp, TWIDDLE_ROWS):
            c0 = c0_ref[r0:r0 + TWIDDLE_ROWS, :]
            s0 = s0_ref[r0:r0 + TWIDDLE_ROWS, :]
            lhs_ref[r0:r0 + TWIDDLE_ROWS, :] = (cph * c0 - sph * s0).astype(BF16)
            lhs_ref[tp + r0:tp + r0 + TWIDDLE_ROWS, :] = (sph * c0 + cph * s0).astype(BF16)

    _cast_chunks([cast_src_ref], [cast_dst_ref])
    pq = _dot(lhs_ref[...], uf_ref[...])
    for h in range(N_FOURIER_HEADS):
        cols = slice(h * hd, (h + 1) * hd)
        out_cols = slice(POOL_WIDTH + h * hd, POOL_WIDTH + (h + 1) * hd)
        pc = _dot(pq[:tp, cols].astype(BF16), cw_ref[h])
        qs = _dot(pq[tp:, cols].astype(BF16), sw_ref[h])
        lo_ref[:, out_cols] = (pc + qs)[:t].astype(BF16)
        hi_ref[:, out_cols] = _dot(rev_ref[...], (pc - qs).astype(BF16)).astype(BF16)

    def pool(up_ref, prev_ref, next_ref, band_ref, out_ref):
        win = jnp.concatenate([prev_ref[...], up_ref[...], next_ref[...]], axis=0)
        for g in range(N_POOL_GROUPS):
            cols = slice(g * gd, (g + 1) * gd)
            pooled = _dot(band_ref[0, g], win[:, cols]).astype(BF16)
            y = _dot(pooled, wpool_ref[g]) * pscale_ref[:, cols]
            out_ref[:, cols] = y.astype(BF16)

    pool(up_lo_ref, prev_lo_ref, next_lo_ref, band_lo_ref, lo_ref)
    pool(up_hi_ref, prev_hi_ref, next_hi_ref, band_hi_ref, hi_ref)


def _mix(u3, w_pool_bf16, pool_scale, cw, sw, cast_weight):
    b, s, d = u3.shape
    t = SEQ_TILE
    tp = DFT_ROWS
    halo = POOL_HALO
    n_tiles = s // t
    n_steps = n_tiles // 2
    halo_blocks_per_tile = t // halo
    last_halo_block = s // halo - 1

    c0, s0 = _twiddle(np.arange(tp), np.arange(s), s)
    cph, sph = _twiddle(np.arange(0, s // 2, t), np.arange(s), s)
    band = jnp.asarray(_pool_band(s, t, halo)).astype(BF16)
    rev = np.zeros((t, tp), np.float32)
    rev[np.arange(t), t - np.arange(t)] = 1.0

    hi_tile = lambda m: n_tiles - 1 - m
    tile_spec = lambda tile_of: pl.BlockSpec((None, t, POOL_WIDTH), lambda m, bi: (bi, tile_of(m), 0))
    prev_spec = lambda tile_of: pl.BlockSpec(
        (None, halo, POOL_WIDTH),
        lambda m, bi: (bi, jnp.maximum(tile_of(m) * halo_blocks_per_tile - 1, 0), 0))
    next_spec = lambda tile_of: pl.BlockSpec(
        (None, halo, POOL_WIDTH),
        lambda m, bi: (bi, jnp.minimum((tile_of(m) + 1) * halo_blocks_per_tile, last_halo_block), 0))
    band_shape = (1, N_POOL_GROUPS, t, t + 2 * halo)
    const2 = lambda shape: pl.BlockSpec(shape, lambda m, bi: (0, 0))
    const3 = lambda shape: pl.BlockSpec(shape, lambda m, bi: (0, 0, 0))
    lo_tile = lambda m: m

    in_specs = [
        pl.BlockSpec((None, s, FOURIER_WIDTH), lambda m, bi: (bi, 0, 1)),
        tile_spec(lo_tile), prev_spec(lo_tile), next_spec(lo_tile),
        tile_spec(hi_tile), prev_spec(hi_tile), next_spec(hi_tile),
        pl.BlockSpec((tp, s), lambda m, bi: (0, 0), pipeline_mode=pl.Buffered(1)),
        pl.BlockSpec((tp, s), lambda m, bi: (0, 0), pipeline_mode=pl.Buffered(1)),
        pl.BlockSpec((1, 1, s), lambda m, bi: (m, 0, 0)),
        pl.BlockSpec((1, 1, s), lambda m, bi: (m, 0, 0)),
        pl.BlockSpec(band_shape, lambda m, bi: (jnp.where(m == 0, 0, 1), 0, 0, 0)),
        pl.BlockSpec(band_shape, lambda m, bi: (jnp.where(m == 0, 2, 1), 0, 0, 0)),
        const2((t, tp)),
        const3((N_POOL_GROUPS, POOL_GROUP_DIM, POOL_GROUP_DIM)),
        const2((1, POOL_WIDTH)),
        const3((N_FOURIER_HEADS, FOURIER_HEAD_DIM, FOURIER_HEAD_DIM)),
        const3((N_FOURIER_HEADS, FOURIER_HEAD_DIM, FOURIER_HEAD_DIM)),
    ]
    half_out = pl.BlockSpec((None, t, d), lambda m, bi: (bi, m, 0))
    hi_out = pl.BlockSpec((None, t, d), lambda m, bi: (bi, n_steps - 1 - m, 0))
    cast = _CastAlong(cast_weight, n_steps * b, lambda m, bi: m * b + bi)
    vmem = (2 * s * FOURIER_WIDTH * 2
            + 2 * tp * s * 4
            + 2 * tp * s * 2
            + 2 * 2 * t * d * 2 * 2
            + 4 * 2 * tp * FOURIER_WIDTH * 4
            + cast.vmem_bytes)
    lo, hi, cast_out = pl.pallas_call(
        _mix_kernel,
        grid=(n_steps, b),
        in_specs=in_specs + [cast.spec],
        out_specs=[half_out, hi_out, cast.spec],
        out_shape=[jax.ShapeDtypeStruct((b, s // 2, d), BF16)] * 2 + [cast.out_shape],
        scratch_shapes=[pltpu.VMEM((2 * tp, s), BF16)],
        compiler_params=pltpu.CompilerParams(
            dimension_semantics=("arbitrary", "arbitrary"), vmem_limit_bytes=vmem + 4 * MIB),
        name="seq_mix",
    )(u3, u3, u3, u3, u3, u3, u3, jnp.asarray(c0), jnp.asarray(s0),
      jnp.asarray(cph).reshape(n_steps, 1, s), jnp.asarray(sph).reshape(n_steps, 1, s),
      band, band, jnp.asarray(rev).astype(BF16),
      w_pool_bf16, pool_scale.reshape(1, POOL_WIDTH), cw, sw, cast.src)
    return lo, hi, cast_out.reshape(cast.shape)


def _out_proj_route_kernel(x_ref, mixed_lo_ref, mixed_hi_ref, wout_ref, g_ref, wr_ref, br_ref,
                           tri_ref, pick_ref, cast_src_ref, rows_ref, route_ref, counts_ref, cast_dst_ref,
                           carry_ref, x1_ref, *, tiles_per_seq, n_tiles):
    step = pl.program_id(0)

    @pl.when(step == 0)
    def _():
        carry_ref[...] = jnp.zeros_like(carry_ref)
        x1_ref[...] = jnp.zeros_like(x1_ref)

    _cast_chunks([cast_src_ref], [cast_dst_ref])

    x1 = x1_ref[...]
    rows_ref[:, :D_MODEL] = x1
    h2 = _rmsnorm(x1, g_ref[...])
    logits = _dot(h2.astype(BF16), wr_ref[...]) + br_ref[...]

    tile = jnp.minimum(step, n_tiles - 1)
    in_lo_half = (tile % tiles_per_seq) < tiles_per_seq // 2
    mixed = jnp.where(in_lo_half, mixed_lo_ref[...], mixed_hi_ref[...])
    x1_ref[...] = x_ref[...] + _dot(mixed, wout_ref[...])

    lane = lax.broadcasted_iota(jnp.int32, logits.shape, 1)
    neg = jnp.float32(-jnp.inf)
    big = jnp.int32(ROUTER_LANES)

    is_group = lane < N_EXPERT_GROUPS
    gl = jnp.where(is_group, logits, neg)
    gmax = jnp.max(gl, axis=-1, keepdims=True)
    gidx = jnp.min(jnp.where(gl == gmax, lane, big), axis=-1, keepdims=True)
    p_g = 1.0 / jnp.sum(jnp.exp(gl - gmax), axis=-1, keepdims=True)

    e_lane = lane - N_EXPERT_GROUPS
    in_group = (e_lane >= gidx * EXPERTS_PER_GROUP) & (e_lane < (gidx + 1) * EXPERTS_PER_GROUP)
    el = jnp.where(in_group, logits, neg)
    v1 = jnp.max(el, axis=-1, keepdims=True)
    i1 = jnp.min(jnp.where(el == v1, lane, big), axis=-1, keepdims=True)
    el2 = jnp.where(lane == i1, neg, el)
    v2 = jnp.max(el2, axis=-1, keepdims=True)
    i2 = jnp.min(jnp.where(el2 == v2, lane, big), axis=-1, keepdims=True)
    r = jnp.exp(v2 - v1)
    w1 = p_g / (1.0 + r)
    w2 = p_g * r / (1.0 + r)

    first_is_low = i1 < i2
    first_lane = N_EXPERT_GROUPS + gidx * EXPERTS_PER_GROUP
    la = jnp.where(first_is_low, i1, i2) - first_lane
    lb = jnp.where(first_is_low, i2, i1) - first_lane
    w_low = jnp.where(first_is_low, w1, w2)
    w_high = jnp.where(first_is_low, w2, w1)
    pair = jnp.where(la == 0, lb - 1, jnp.where(la == 1, 6 - lb, 5))
    slot_a_is_high = la == 2
    w_a = jnp.where(slot_a_is_high, w_high, w_low)
    w_b = jnp.where(slot_a_is_high, w_low, w_high)
    cls = gidx * PAIRS_PER_GROUP + pair
    rows_ref[:, D_MODEL:] = jnp.where(lane == 0, w_a, jnp.where(lane == 1, w_b, 0.0))

    onehot = jnp.where(lane == cls, 1.0, 0.0)
    before = _dot(tri_ref[...], onehot.astype(BF16)) + carry_ref[...]
    rank = jnp.sum(jnp.where(lane == cls, before, 0.0), axis=-1, keepdims=True)
    carry_ref[...] += jnp.sum(onehot, axis=0, keepdims=True) * jnp.where(step > 0, 1.0, 0.0)
    counts_ref[...] = carry_ref[...]

    rank_hi = jnp.floor(rank * (1.0 / RANK_RADIX))
    rank_lo = rank - rank_hi * RANK_RADIX
    digits = jnp.where(lane == 0, cls.astype(F32),
                       jnp.where(lane == 1, rank_hi, jnp.where(lane == 2, rank_lo, 0.0)))
    route_ref[...] = lax.dot_general(pick_ref[...], digits.astype(BF16), (((1,), (1,)), ((), ())),
                                     preferred_element_type=F32)


def _out_proj_route(x2, mixed_lo, mixed_hi, seq_len, w_out_bf16, g_ffn, wr, br, cast_weight):
    n, d = x2.shape
    tm = TOKEN_TILE
    n_tiles = n // tm
    tiles_per_seq = seq_len // tm
    half_tiles = tiles_per_seq // 2
    const = lambda shape: pl.BlockSpec(shape, lambda i: (0, 0))
    in_tile = lambda i: jnp.minimum(i, n_tiles - 1)
    out_tile = lambda i: jnp.maximum(i - 1, 0)
    lo_spec = pl.BlockSpec((tm, d), lambda i: (
        (in_tile(i) // tiles_per_seq) * half_tiles + jnp.minimum(in_tile(i) % tiles_per_seq, half_tiles - 1), 0))
    hi_spec = pl.BlockSpec((tm, d), lambda i: (
        (in_tile(i) // tiles_per_seq) * half_tiles + jnp.maximum(in_tile(i) % tiles_per_seq - half_tiles, 0), 0))
    tri = np.tril(np.ones((tm, tm), np.float32), -1)
    pick = np.eye(SUBLANES, ROUTER_LANES, dtype=np.float32)
    vmem = (2 * tm * d * 4 + 4 * tm * d * 2 + 2 * tm * ROW_WORDS * 4 + d * d * 2 + tm * d * 4
            + 2 * d * ROUTER_LANES * 2 + 2 * tm * tm * 2 + 3 * tm * d * 4)
    cast = _CastAlong(cast_weight, n_tiles, in_tile)
    vmem += cast.vmem_bytes
    rows, route, counts, cast_out = pl.pallas_call(
        functools.partial(_out_proj_route_kernel, tiles_per_seq=tiles_per_seq, n_tiles=n_tiles),
        grid=(n_tiles + 1,),
        in_specs=[pl.BlockSpec((tm, d), lambda i: (in_tile(i), 0)), lo_spec, hi_spec,
                  pl.BlockSpec((d, d), lambda i: (0, 0), pipeline_mode=pl.Buffered(1)),
                  const((1, d)), const((d, ROUTER_LANES)), const((1, ROUTER_LANES)),
                  const((tm, tm)), const((SUBLANES, ROUTER_LANES)), cast.spec],
        out_specs=[pl.BlockSpec((tm, ROW_WORDS), lambda i: (out_tile(i), 0)),
                   pl.BlockSpec((SUBLANES, tm), lambda i: (0, out_tile(i))),
                   const((1, ROUTER_LANES)), cast.spec],
        out_shape=[jax.ShapeDtypeStruct((n, ROW_WORDS), F32),
                   jax.ShapeDtypeStruct((SUBLANES, n), F32),
                   jax.ShapeDtypeStruct((1, ROUTER_LANES), F32), cast.out_shape],
        scratch_shapes=[pltpu.VMEM((1, ROUTER_LANES), F32), pltpu.VMEM((tm, d), F32)],
        compiler_params=pltpu.CompilerParams(
            dimension_semantics=("arbitrary",), vmem_limit_bytes=vmem + 4 * MIB),
        name="out_proj_route",
    )(x2, mixed_lo, mixed_hi, w_out_bf16, g_ffn.reshape(1, d), wr, br,
      jnp.asarray(tri).astype(BF16), jnp.asarray(pick).astype(BF16), cast.src)
    return rows, route, counts, cast_out.reshape(cast.shape)


def _routing_plan(route, counts, n):
    tm = EXPERT_TILE
    cnt = counts[0, :N_CLASSES].astype(jnp.int32)
    class_end = jnp.cumsum(cnt)
    class_start = class_end - cnt
    cls = route[0].astype(jnp.int32)
    rank = (route[1] * RANK_RADIX + route[2]).astype(jnp.int32)
    pos = class_start[cls] + rank

    first_tile = class_start // tm
    items_per_class = jnp.where(cnt > 0, (class_end - 1) // tm - first_tile + 1, 0)
    item_end = jnp.cumsum(items_per_class)
    item_start = item_end - items_per_class
    n_items = n // tm + N_CLASSES
    item = jnp.arange(n_items, dtype=jnp.int32)
    used = item < item_end[-1]
    item_c = jnp.minimum(item, item_end[-1] - 1)
    item_cls = jnp.sum((item_c[:, None] >= item_end[None, :]).astype(jnp.int32), axis=1)
    item_tile = first_tile[item_cls] + item_c - item_start[item_cls]
    lo = jnp.clip(class_start[item_cls] - item_tile * tm, 0, tm)
    hi = jnp.clip(class_end[item_cls] - item_tile * tm, 0, tm)
    hi = jnp.where(used, hi, lo)
    group_base = np.repeat(np.arange(N_EXPERT_GROUPS) * EXPERTS_PER_GROUP, PAIRS_PER_GROUP)
    class_ea = jnp.asarray(group_base + np.tile(PAIR_SLOT_A, N_EXPERT_GROUPS), jnp.int32)
    class_eb = jnp.asarray(group_base + np.tile(PAIR_SLOT_B, N_EXPERT_GROUPS), jnp.int32)
    return pos, item_tile, class_ea[item_cls], class_eb[item_cls], lo, hi


def _row_copy(src_ref, src_row, dst_ref, dst_row, sem):
    return pltpu.make_async_copy(src_ref.at[pl.ds(src_row, 1), :], dst_ref.at[pl.ds(dst_row, 1), :], sem)


def _scatter_rows_kernel(pos_ref, rows_ref, dst_ref, *rest, with_inverse):
    sem = rest[-1]
    tm = PERM_TILE
    base = pl.program_id(0) * tm

    def issue(r, carry):
        p = pos_ref[base + r]
        if with_inverse:
            rest[0][p] = base + r
        _row_copy(rows_ref, r, dst_ref, p, sem).start()
        return carry

    lax.fori_loop(0, tm, issue, 0, unroll=8)
    pltpu.make_async_copy(rows_ref, dst_ref.at[pl.ds(0, tm), :], sem).wait()


def _scatter_rows(pos, rows, *, with_inverse, name):
    n, w = rows.shape
    tm = PERM_TILE
    out_specs = [pl.BlockSpec(memory_space=pl.ANY)]
    out_shape = [jax.ShapeDtypeStruct((n, w), rows.dtype)]
    if with_inverse:
        out_specs.append(pl.BlockSpec(memory_space=pltpu.SMEM))
        out_shape.append(jax.ShapeDtypeStruct((n,), jnp.int32))
    grid_spec = pltpu.PrefetchScalarGridSpec(
        num_scalar_prefetch=1,
        grid=(n // tm,),
        in_specs=[pl.BlockSpec((tm, w), lambda i, pos_ref: (i, 0))],
        out_specs=out_specs,
        scratch_shapes=[pltpu.SemaphoreType.DMA(())],
    )
    return pl.pallas_call(
        functools.partial(_scatter_rows_kernel, with_inverse=with_inverse),
        grid_spec=grid_spec,
        out_shape=out_shape,
        compiler_params=pltpu.CompilerParams(
            dimension_semantics=("arbitrary",), vmem_limit_bytes=2 * tm * w * 4 + 4 * MIB),
        name=name,
    )(pos, rows)


def _expert_pair_kernel(tile_ref, ea_ref, eb_ref, lo_ref, hi_ref, rows_ref, gffn_ref, gfin_ref,
                        wga_ref, wua_ref, wda_ref, wgb_ref, wub_ref, wdb_ref, out_ref):
    del tile_ref, ea_ref, eb_ref
    lo = lo_ref[pl.program_id(0)]
    hi = hi_ref[pl.program_id(0)]

    @pl.when(hi > lo)
    def _():
        x1 = rows_ref[:, :D_MODEL]
        wts = rows_ref[:, D_MODEL:]
        lane = lax.broadcasted_iota(jnp.int32, wts.shape, 1)
        w_a = jnp.sum(jnp.where(lane == 0, wts, 0.0), axis=-1, keepdims=True)
        w_b = jnp.sum(jnp.where(lane == 1, wts, 0.0), axis=-1, keepdims=True)
        h = _rmsnorm(x1, gffn_ref[...]).astype(BF16)

        def gated(wg_ref, wu_ref, w):
            a = _dot(h, wg_ref[0])
            v = _dot(h, wu_ref[0])
            return (a * (1.0 / (1.0 + jnp.exp(-a))) * v * w).astype(BF16)

        y = (_dot(gated(wga_ref, wua_ref, w_a), wda_ref[0])
             + _dot(gated(wgb_ref, wub_ref, w_b), wdb_ref[0]))
        res = _rmsnorm(x1 + y, gfin_ref[...])

        row = lax.broadcasted_iota(jnp.int32, (EXPERT_TILE, 1), 0)
        mine = (row >= lo) & (row < hi)

        @pl.when(lo == 0)
        def _():
            out_ref[...] = jnp.where(mine, res, 0.0)

        @pl.when(lo > 0)
        def _():
            out_ref[...] = jnp.where(mine, res, out_ref[...])


def _expert_pairs(item_tile, item_ea, item_eb, item_lo, item_hi, rows_sorted, g_ffn, g_final, wg, wu, wd):
    n, w = rows_sorted.shape
    d, f = D_MODEL, EXPERT_FF
    tm = EXPERT_TILE
    gate_a = pl.BlockSpec((1, d, f), lambda j, t, ea, eb, lo, hi: (ea[j], 0, 0))
    gate_b = pl.BlockSpec((1, d, f), lambda j, t, ea, eb, lo, hi: (eb[j], 0, 0))
    down_a = pl.BlockSpec((1, f, d), lambda j, t, ea, eb, lo, hi: (ea[j], 0, 0))
    down_b = pl.BlockSpec((1, f, d), lambda j, t, ea, eb, lo, hi: (eb[j], 0, 0))
    gain = pl.BlockSpec((1, d), lambda j, t, ea, eb, lo, hi: (0, 0))
    grid_spec = pltpu.PrefetchScalarGridSpec(
        num_scalar_prefetch=5,
        grid=(item_tile.shape[0],),
        in_specs=[pl.BlockSpec((tm, w), lambda j, t, ea, eb, lo, hi: (t[j], 0)), gain, gain,
                  gate_a, gate_a, down_a, gate_b, gate_b, down_b],
        out_specs=pl.BlockSpec((tm, d), lambda j, t, ea, eb, lo, hi: (t[j], 0)),
    )
    vmem = 2 * 6 * d * f * 2 + 2 * tm * w * 4 + 2 * tm * d * 4 + 8 * tm * d * 4
    return pl.pallas_call(
        _expert_pair_kernel,
        grid_spec=grid_spec,
        out_shape=jax.ShapeDtypeStruct((n, d), F32),
        compiler_params=pltpu.CompilerParams(
            dimension_semantics=("arbitrary",), vmem_limit_bytes=vmem + 4 * MIB),
        name="expert_pairs",
    )(item_tile, item_ea, item_eb, item_lo, item_hi, rows_sorted, g_ffn.reshape(1, d),
      g_final.reshape(1, d), wg, wu, wd, wg, wu, wd)


def kernel(x, g_mix, w_in, w_pool, pool_scale, w_fourier, w_out, g_ffn, w_group_router,
           b_group_router, w_expert_router, b_expert_router, w_gate, w_up, w_down, g_final):
    b, s, d = x.shape
    assert d == D_MODEL and s % (2 * SEQ_TILE) == 0 and s % (2 * TOKEN_TILE) == 0
    assert (b * s) % max(TOKEN_TILE, PERM_TILE) == 0
    n = b * s
    x2 = x.reshape(n, d)

    cw, sw = _fourier_weights(w_fourier, s)
    u, (w_out_bf16, w_gate_bf16) = _norm_proj(x2, g_mix, w_in.astype(BF16), [w_out, w_gate])
    mixed_lo, mixed_hi, w_up_bf16 = _mix(u.reshape(b, s, d), w_pool.astype(BF16), pool_scale, cw, sw, w_up)

    wr = jnp.concatenate([w_group_router, w_expert_router], axis=1)
    wr = jnp.pad(wr, ((0, 0), (0, ROUTER_LANES - wr.shape[1]))).astype(BF16)
    br = jnp.concatenate([b_group_router, b_expert_router])
    br = jnp.pad(br, (0, ROUTER_LANES - br.shape[0])).reshape(1, ROUTER_LANES)

    rows, route, counts, w_down_bf16 = _out_proj_route(
        x2, mixed_lo.reshape(n // 2, d), mixed_hi.reshape(n // 2, d), s, w_out_bf16, g_ffn, wr, br, w_down)
    pos, item_tile, item_ea, item_eb, item_lo, item_hi = _routing_plan(route, counts, n)
    rows_sorted, inv_pos = _scatter_rows(pos, rows, with_inverse=True, name="permute_rows")
    out_sorted = _expert_pairs(item_tile, item_ea, item_eb, item_lo, item_hi, rows_sorted, g_ffn, g_final,
                               w_gate_bf16, w_up_bf16, w_down_bf16)
    (out,) = _scatter_rows(inv_pos, out_sorted, with_inverse=False, name="unpermute_rows")
    return out.reshape(b, s, d)
```

```python
import functools

import numpy as np
import jax
import jax.numpy as jnp
from jax import lax
from jax.experimental import pallas as pl
from jax.experimental.pallas import tpu as pltpu

D_MODEL = 2048
POOL_WINDOWS = (2, 4, 8, 16)
N_POOL_GROUPS = len(POOL_WINDOWS)
POOL_WIDTH = D_MODEL // 2
POOL_GROUP_DIM = POOL_WIDTH // N_POOL_GROUPS
FOURIER_WIDTH = D_MODEL - POOL_WIDTH
N_FOURIER_HEADS = 4
FOURIER_HEAD_DIM = FOURIER_WIDTH // N_FOURIER_HEADS
N_EXPERT_GROUPS = 4
EXPERTS_PER_GROUP = 4
N_EXPERTS = N_EXPERT_GROUPS * EXPERTS_PER_GROUP
EXPERT_FF = D_MODEL // 4
RMS_EPS = 1e-6

LANES = 128
SUBLANES = 8
BF16_SUBLANES = 16
ROUTER_LANES = LANES
MIB = 1024 * 1024

TOKEN_TILE = 512
SEQ_TILE = 256
POOL_HALO = BF16_SUBLANES
DFT_ROWS = SEQ_TILE + BF16_SUBLANES
TWIDDLE_ROWS = 16

PAIR_SLOT_A = (0, 0, 0, 1, 1, 3)
PAIR_SLOT_B = (1, 2, 3, 3, 2, 2)
PAIRS_PER_GROUP = len(PAIR_SLOT_A)
N_CLASSES = N_EXPERT_GROUPS * PAIRS_PER_GROUP
ROW_WORDS = D_MODEL + LANES
RANK_RADIX = 128
EXPERT_TILE = 256
PERM_TILE = 1024

BF16 = jnp.bfloat16
F32 = jnp.float32


def _rmsnorm(x, g):
    ms = jnp.mean(x * x, axis=-1, keepdims=True)
    return x * lax.rsqrt(ms + RMS_EPS) * g


def _dot(a, b):
    return jnp.dot(a, b, preferred_element_type=F32)


def _twiddle(rows, cols, period):
    m = (np.asarray(rows, np.int64)[:, None] * np.asarray(cols, np.int64)[None, :]) % period
    ang = (2.0 * np.pi / period) * m.astype(np.float64)
    return np.cos(ang).astype(np.float32), np.sin(ang).astype(np.float32)


def _pool_band(seq_len, tile, halo):
    n_tiles = seq_len // tile
    out = np.zeros((3, N_POOL_GROUPS, tile, tile + 2 * halo), np.float64)
    for v, m in enumerate((0, 1, n_tiles - 1)):
        t0 = m * tile
        for g, k in enumerate(POOL_WINDOWS):
            for r in range(tile):
                t = t0 + r
                lo = max(t - (k - 1) // 2, 0)
                hi = min(t + k // 2 + 1, seq_len)
                out[v, g, r, lo - t0 + halo:hi - t0 + halo] = 1.0 / (hi - lo)
                out[v, g, r, r + halo] -= 1.0
    return out.astype(np.float32)


def _fourier_weight_kernel(cd_ref, sd_ref, w_ref, cw_ref, sw_ref, *, scale):
    w = w_ref[0]
    cw = jnp.dot(cd_ref[...], w, preferred_element_type=F32, precision=lax.Precision.HIGHEST)
    sw = jnp.dot(sd_ref[...], w, preferred_element_type=F32, precision=lax.Precision.HIGHEST)
    cw_ref[0] = (cw * scale).astype(BF16)
    sw_ref[0] = (sw * (-scale)).astype(BF16)


def _fourier_weights(w_fourier, seq_len):
    dh = FOURIER_HEAD_DIM
    cd, sd = _twiddle(np.arange(dh), np.arange(dh), dh)
    scale = 1.0 / np.sqrt(float(seq_len * dh))
    mat = pl.BlockSpec((dh, dh), lambda h: (0, 0))
    per_head = pl.BlockSpec((1, dh, dh), lambda h: (h, 0, 0))
    return pl.pallas_call(
        functools.partial(_fourier_weight_kernel, scale=scale),
        grid=(N_FOURIER_HEADS,),
        in_specs=[mat, mat, per_head],
        out_specs=[per_head, per_head],
        out_shape=[jax.ShapeDtypeStruct((N_FOURIER_HEADS, dh, dh), BF16)] * 2,
        name="fourier_weights",
    )(jnp.asarray(cd), jnp.asarray(sd), w_fourier)


class _CastAlong:
    def __init__(self, w, n_chunks, chunk_of):
        cols = w.shape[-1]
        rows = w.size // cols // n_chunks
        self.shape = w.shape
        self.src = w.reshape(n_chunks, rows, cols)
        self.spec = pl.BlockSpec((1, rows, cols), lambda *idx: (chunk_of(*idx), 0, 0))
        self.out_shape = jax.ShapeDtypeStruct((n_chunks, rows, cols), BF16)
        self.vmem_bytes = 2 * rows * cols * (4 + 2)


def _cast_chunks(src_refs, dst_refs):
    for src_ref, dst_ref in zip(src_refs, dst_refs):
        dst_ref[...] = src_ref[...].astype(BF16)


def _norm_proj_kernel(x_ref, g_ref, w_ref, *rest, n_cast):
    cast_src, (u_ref, *cast_dst) = rest[:n_cast], rest[n_cast:]
    h = _rmsnorm(x_ref[...], g_ref[...])
    u_ref[...] = _dot(h.astype(BF16), w_ref[...]).astype(BF16)
    _cast_chunks(cast_src, cast_dst)


def _norm_proj(x2, g_mix, w_in_bf16, cast_weights):
    n, d = x2.shape
    tm = TOKEN_TILE
    casts = [_CastAlong(w, n // tm, lambda i: i) for w in cast_weights]
    vmem = (2 * tm * d * 4 + d * d * 2 + 2 * tm * d * 2 + 3 * tm * d * 4
            + sum(c.vmem_bytes for c in casts))
    u, *cast_out = pl.pallas_call(
        functools.partial(_norm_proj_kernel, n_cast=len(casts)),
        grid=(n // tm,),
        in_specs=[
            pl.BlockSpec((tm, d), lambda i: (i, 0)),
            pl.BlockSpec((1, d), lambda i: (0, 0)),
            pl.BlockSpec((d, d), lambda i: (0, 0), pipeline_mode=pl.Buffered(1)),
        ] + [c.spec for c in casts],
        out_specs=[pl.BlockSpec((tm, d), lambda i: (i, 0))] + [c.spec for c in casts],
        out_shape=[jax.ShapeDtypeStruct((n, d), BF16)] + [c.out_shape for c in casts],
        compiler_params=pltpu.CompilerParams(
            dimension_semantics=("arbitrary",), vmem_limit_bytes=vmem + 4 * MIB),
        name="norm_proj",
    )(x2, g_mix.reshape(1, d), w_in_bf16, *[c.src for c in casts])
    return u, [o.reshape(c.shape) for o, c in zip(cast_out, casts)]


def _mix_kernel(uf_ref, up_lo_ref, prev_lo_ref, next_lo_ref, up_hi_ref, prev_hi_ref, next_hi_ref,
                c0_ref, s0_ref, cph_ref, sph_ref, band_lo_ref, band_hi_ref, rev_ref,
                wpool_ref, pscale_ref, cw_ref, sw_ref, cast_src_ref, lo_ref, hi_ref, cast_dst_ref, lhs_ref):
    t = SEQ_TILE
    tp = DFT_ROWS
    gd = POOL_GROUP_DIM
    hd = FOURIER_HEAD_DIM

    @pl.when(pl.program_id(1) == 0)
    def _():
        cph = cph_ref[0]
        sph = sph_ref[0]
        for r0 in range(0, tp, TWIDDLE_ROWS):
            c0 = c0_ref[r0:r0 + TWIDDLE_ROWS, :]
            s0 = s0_ref[r0:r0 + TWIDDLE_ROWS, :]
            lhs_ref[r0:r0 + TWIDDLE_ROWS, :] = (cph * c0 - sph * s0).astype(BF16)
            lhs_ref[tp + r0:tp + r0 + TWIDDLE_ROWS, :] = (sph * c0 + cph * s0).astype(BF16)

    _cast_chunks([cast_src_ref], [cast_dst_ref])
    pq = _dot(lhs_ref[...], uf_ref[...])
    for h in range(N_FOURIER_HEADS):
        cols = slice(h * hd, (h + 1) * hd)
        out_cols = slice(POOL_WIDTH + h * hd, POOL_WIDTH + (h + 1) * hd)
        pc = _dot(pq[:tp, cols].astype(BF16), cw_ref[h])
        qs = _dot(pq[tp:, cols].astype(BF16), sw_ref[h])
        lo_ref[:, out_cols] = (pc + qs)[:t].astype(BF16)
        hi_ref[:, out_cols] = _dot(rev_ref[...], (pc - qs).astype(BF16)).astype(BF16)

    def pool(up_ref, prev_ref, next_ref, band_ref, out_ref):
        win = jnp.concatenate([prev_ref[...], up_ref[...], next_ref[...]], axis=0)
        for g in range(N_POOL_GROUPS):
            cols = slice(g * gd, (g + 1) * gd)
            pooled = _dot(band_ref[0, g], win[:, cols]).astype(BF16)
            y = _dot(pooled, wpool_ref[g]) * pscale_ref[:, cols]
            out_ref[:, cols] = y.astype(BF16)

    pool(up_lo_ref, prev_lo_ref, next_lo_ref, band_lo_ref, lo_ref)
    pool(up_hi_ref, prev_hi_ref, next_hi_ref, band_hi_ref, hi_ref)


def _mix(u3, w_pool_bf16, pool_scale, cw, sw, cast_weight):
    b, s, d = u3.shape
    t = SEQ_TILE
    tp = DFT_ROWS
    halo = POOL_HALO
    n_tiles = s // t
    n_steps = n_tiles // 2
    halo_blocks_per_tile = t // halo
    last_halo_block = s // halo - 1

    c0, s0 = _twiddle(np.arange(tp), np.arange(s), s)
    cph, sph = _twiddle(np.arange(0, s // 2, t), np.arange(s), s)
    band = jnp.asarray(_pool_band(s, t, halo)).astype(BF16)
    rev = np.zeros((t, tp), np.float32)
    rev[np.arange(t), t - np.arange(t)] = 1.0

    hi_tile = lambda m: n_tiles - 1 - m
    tile_spec = lambda tile_of: pl.BlockSpec((None, t, POOL_WIDTH), lambda m, bi: (bi, tile_of(m), 0))
    prev_spec = lambda tile_of: pl.BlockSpec(
        (None, halo, POOL_WIDTH),
        lambda m, bi: (bi, jnp.maximum(tile_of(m) * halo_blocks_per_tile - 1, 0), 0))
    next_spec = lambda tile_of: pl.BlockSpec(
        (None, halo, POOL_WIDTH),
        lambda m, bi: (bi, jnp.minimum((tile_of(m) + 1) * halo_blocks_per_tile, last_halo_block), 0))
    band_shape = (1, N_POOL_GROUPS, t, t + 2 * halo)
    const2 = lambda shape: pl.BlockSpec(shape, lambda m, bi: (0, 0))
    const3 = lambda shape: pl.BlockSpec(shape, lambda m, bi: (0, 0, 0))
    lo_tile = lambda m: m

    in_specs = [
        pl.BlockSpec((None, s, FOURIER_WIDTH), lambda m, bi: (bi, 0, 1)),
        tile_spec(lo_tile), prev_spec(lo_tile), next_spec(lo_tile),
        tile_spec(hi_tile), prev_spec(hi_tile), next_spec(hi_tile),
        pl.BlockSpec((tp, s), lambda m, bi: (0, 0), pipeline_mode=pl.Buffered(1)),
        pl.BlockSpec((tp, s), lambda m, bi: (0, 0), pipeline_mode=pl.Buffered(1)),
        pl.BlockSpec((1, 1, s), lambda m, bi: (m, 0, 0)),
        pl.BlockSpec((1, 1, s), lambda m, bi: (m, 0, 0)),
        pl.BlockSpec(band_shape, lambda m, bi: (jnp.where(m == 0, 0, 1), 0, 0, 0)),
        pl.BlockSpec(band_shape, lambda m, bi: (jnp.where(m == 0, 2, 1), 0, 0, 0)),
        const2((t, tp)),
        const3((N_POOL_GROUPS, POOL_GROUP_DIM, POOL_GROUP_DIM)),
        const2((1, POOL_WIDTH)),
        const3((N_FOURIER_HEADS, FOURIER_HEAD_DIM, FOURIER_HEAD_DIM)),
        const3((N_FOURIER_HEADS, FOURIER_HEAD_DIM, FOURIER_HEAD_DIM)),
    ]
    half_out = pl.BlockSpec((None, t, d), lambda m, bi: (bi, m, 0))
    hi_out = pl.BlockSpec((None, t, d), lambda m, bi: (bi, n_steps - 1 - m, 0))
    cast = _CastAlong(cast_weight, n_steps * b, lambda m, bi: m * b + bi)
    vmem = (2 * s * FOURIER_WIDTH * 2
            + 2 * tp * s * 4
            + 2 * tp * s * 2
            + 2 * 2 * t * d * 2 * 2
            + 4 * 2 * tp * FOURIER_WIDTH * 4
            + cast.vmem_bytes)
    lo, hi, cast_out = pl.pallas_call(
        _mix_kernel,
        grid=(n_steps, b),
        in_specs=in_specs + [cast.spec],
        out_specs=[half_out, hi_out, cast.spec],
        out_shape=[jax.ShapeDtypeStruct((b, s // 2, d), BF16)] * 2 + [cast.out_shape],
        scratch_shapes=[pltpu.VMEM((2 * tp, s), BF16)],
        compiler_params=pltpu.CompilerParams(
            dimension_semantics=("arbitrary", "arbitrary"), vmem_limit_bytes=vmem + 4 * MIB),
        name="seq_mix",
    )(u3, u3, u3, u3, u3, u3, u3, jnp.asarray(c0), jnp.asarray(s0),
      jnp.asarray(cph).reshape(n_steps, 1, s), jnp.asarray(sph).reshape(n_steps, 1, s),
      band, band, jnp.asarray(rev).astype(BF16),
      w_pool_bf16, pool_scale.reshape(1, POOL_WIDTH), cw, sw, cast.src)
    return lo, hi, cast_out.reshape(cast.shape)


def _out_proj_route_kernel(x_ref, mixed_lo_ref, mixed_hi_ref, wout_ref, g_ref, wr_ref, br_ref,
                           tri_ref, pick_ref, cast_src_ref, rows_ref, route_ref, counts_ref, cast_dst_ref,
                           carry_ref, x1_ref, *, tiles_per_seq, n_tiles):
    step = pl.program_id(0)

    @pl.when(step == 0)
    def _():
        carry_ref[...] = jnp.zeros_like(carry_ref)
        x1_ref[...] = jnp.zeros_like(x1_ref)

    _cast_chunks([cast_src_ref], [cast_dst_ref])

    x1 = x1_ref[...]
    rows_ref[:, :D_MODEL] = x1
    h2 = _rmsnorm(x1, g_ref[...])
    logits = _dot(h2.astype(BF16), wr_ref[...]) + br_ref[...]

    tile = jnp.minimum(step, n_tiles - 1)
    in_lo_half = (tile % tiles_per_seq) < tiles_per_seq // 2
    mixed = jnp.where(in_lo_half, mixed_lo_ref[...], mixed_hi_ref[...])
    x1_ref[...] = x_ref[...] + _dot(mixed, wout_ref[...])

    lane = lax.broadcasted_iota(jnp.int32, logits.shape, 1)
    neg = jnp.float32(-jnp.inf)
    big = jnp.int32(ROUTER_LANES)

    is_group = lane < N_EXPERT_GROUPS
    gl = jnp.where(is_group, logits, neg)
    gmax = jnp.max(gl, axis=-1, keepdims=True)
    gidx = jnp.min(jnp.where(gl == gmax, lane, big), axis=-1, keepdims=True)
    p_g = 1.0 / jnp.sum(jnp.exp(gl - gmax), axis=-1, keepdims=True)

    e_lane = lane - N_EXPERT_GROUPS
    in_group = (e_lane >= gidx * EXPERTS_PER_GROUP) & (e_lane < (gidx + 1) * EXPERTS_PER_GROUP)
    el = jnp.where(in_group, logits, neg)
    v1 = jnp.max(el, axis=-1, keepdims=True)
    i1 = jnp.min(jnp.where(el == v1, lane, big), axis=-1, keepdims=True)
    el2 = jnp.where(lane == i1, neg, el)
    v2 = jnp.max(el2, axis=-1, keepdims=True)
    i2 = jnp.min(jnp.where(el2 == v2, lane, big), axis=-1, keepdims=True)
    r = jnp.exp(v2 - v1)
    w1 = p_g / (1.0 + r)
    w2 = p_g * r / (1.0 + r)

    first_is_low = i1 < i2
    first_lane = N_EXPERT_GROUPS + gidx * EXPERTS_PER_GROUP
    la = jnp.where(first_is_low, i1, i2) - first_lane
    lb = jnp.where(first_is_low, i2, i1) - first_lane
    w_low = jnp.where(first_is_low, w1, w2)
    w_high = jnp.where(first_is_low, w2, w1)
    pair = jnp.where(la == 0, lb - 1, jnp.where(la == 1, 6 - lb, 5))
    slot_a_is_high = la == 2
    w_a = jnp.where(slot_a_is_high, w_high, w_low)
    w_b = jnp.where(slot_a_is_high, w_low, w_high)
    cls = gidx * PAIRS_PER_GROUP + pair
    rows_ref[:, D_MODEL:] = jnp.where(lane == 0, w_a, jnp.where(lane == 1, w_b, 0.0))

    onehot = jnp.where(lane == cls, 1.0, 0.0)
    before = _dot(tri_ref[...], onehot.astype(BF16)) + carry_ref[...]
    rank = jnp.sum(jnp.where(lane == cls, before, 0.0), axis=-1, keepdims=True)
    carry_ref[...] += jnp.sum(onehot, axis=0, keepdims=True) * jnp.where(step > 0, 1.0, 0.0)
    counts_ref[...] = carry_ref[...]

    rank_hi = jnp.floor(rank * (1.0 / RANK_RADIX))
    rank_lo = rank - rank_hi * RANK_RADIX
    digits = jnp.where(lane == 0, cls.astype(F32),
                       jnp.where(lane == 1, rank_hi, jnp.where(lane == 2, rank_lo, 0.0)))
    route_ref[...] = lax.dot_general(pick_ref[...], digits.astype(BF16), (((1,), (1,)), ((), ())),
                                     preferred_element_type=F32)


def _out_proj_route(x2, mixed_lo, mixed_hi, seq_len, w_out_bf16, g_ffn, wr, br, cast_weight):
    n, d = x2.shape
    tm = TOKEN_TILE
    n_tiles = n // tm
    tiles_per_seq = seq_len // tm
    half_tiles = tiles_per_seq // 2
    const = lambda shape: pl.BlockSpec(shape, lambda i: (0, 0))
    in_tile = lambda i: jnp.minimum(i, n_tiles - 1)
    out_tile = lambda i: jnp.maximum(i - 1, 0)
    lo_spec = pl.BlockSpec((tm, d), lambda i: (
        (in_tile(i) // tiles_per_seq) * half_tiles + jnp.minimum(in_tile(i) % tiles_per_seq, half_tiles - 1), 0))
    hi_spec = pl.BlockSpec((tm, d), lambda i: (
        (in_tile(i) // tiles_per_seq) * half_tiles + jnp.maximum(in_tile(i) % tiles_per_seq - half_tiles, 0), 0))
    tri = np.tril(np.ones((tm, tm), np.float32), -1)
    pick = np.eye(SUBLANES, ROUTER_LANES, dtype=np.float32)
    vmem = (2 * tm * d * 4 + 4 * tm * d * 2 + 2 * tm * ROW_WORDS * 4 + d * d * 2 + tm * d * 4
            + 2 * d * ROUTER_LANES * 2 + 2 * tm * tm * 2 + 3 * tm * d * 4)
    cast = _CastAlong(cast_weight, n_tiles, in_tile)
    vmem += cast.vmem_bytes
    rows, route, counts, cast_out = pl.pallas_call(
        functools.partial(_out_proj_route_kernel, tiles_per_seq=tiles_per_seq, n_tiles=n_tiles),
        grid=(n_tiles + 1,),
        in_specs=[pl.BlockSpec((tm, d), lambda i: (in_tile(i), 0)), lo_spec, hi_spec,
                  pl.BlockSpec((d, d), lambda i: (0, 0), pipeline_mode=pl.Buffered(1)),
                  const((1, d)), const((d, ROUTER_LANES)), const((1, ROUTER_LANES)),
                  const((tm, tm)), const((SUBLANES, ROUTER_LANES)), cast.spec],
        out_specs=[pl.BlockSpec((tm, ROW_WORDS), lambda i: (out_tile(i), 0)),
                   pl.BlockSpec((SUBLANES, tm), lambda i: (0, out_tile(i))),
                   const((1, ROUTER_LANES)), cast.spec],
        out_shape=[jax.ShapeDtypeStruct((n, ROW_WORDS), F32),
                   jax.ShapeDtypeStruct((SUBLANES, n), F32),
                   jax.ShapeDtypeStruct((1, ROUTER_LANES), F32), cast.out_shape],
        scratch_shapes=[pltpu.VMEM((1, ROUTER_LANES), F32), pltpu.VMEM((tm, d), F32)],
        compiler_params=pltpu.CompilerParams(
            dimension_semantics=("arbitrary",), vmem_limit_bytes=vmem + 4 * MIB),
        name="out_proj_route",
    )(x2, mixed_lo, mixed_hi, w_out_bf16, g_ffn.reshape(1, d), wr, br,
      jnp.asarray(tri).astype(BF16), jnp.asarray(pick).astype(BF16), cast.src)
    return rows, route, counts, cast_out.reshape(cast.shape)


def _routing_plan(route, counts, n):
    tm = EXPERT_TILE
    cnt = counts[0, :N_CLASSES].astype(jnp.int32)
    class_end = jnp.cumsum(cnt)
    class_start = class_end - cnt
    cls = route[0].astype(jnp.int32)
    rank = (route[1] * RANK_RADIX + route[2]).astype(jnp.int32)
    pos = class_start[cls] + rank

    first_tile = class_start // tm
    items_per_class = jnp.where(cnt > 0, (class_end - 1) // tm - first_tile + 1, 0)
    item_end = jnp.cumsum(items_per_class)
    item_start = item_end - items_per_class
    n_items = n // tm + N_CLASSES
    item = jnp.arange(n_items, dtype=jnp.int32)
    item_c = jnp.minimum(item, item_end[-1] - 1)
    item_cls = jnp.sum((item_c[:, None] >= item_end[None, :]).astype(jnp.int32), axis=1)
    item_tile = first_tile[item_cls] + item_c - item_start[item_cls]
    lo = jnp.clip(class_start[item_cls] - item_tile * tm, 0, tm)
    hi = jnp.clip(class_end[item_cls] - item_tile * tm, 0, tm)
    group_base = np.repeat(np.arange(N_EXPERT_GROUPS) * EXPERTS_PER_GROUP, PAIRS_PER_GROUP)
    class_ea = jnp.asarray(group_base + np.tile(PAIR_SLOT_A, N_EXPERT_GROUPS), jnp.int32)
    class_eb = jnp.asarray(group_base + np.tile(PAIR_SLOT_B, N_EXPERT_GROUPS), jnp.int32)
    return pos, item_tile, class_ea[item_cls], class_eb[item_cls], lo, hi


def _row_copy(src_ref, src_row, dst_ref, dst_row, sem):
    return pltpu.make_async_copy(src_ref.at[pl.ds(src_row, 1), :], dst_ref.at[pl.ds(dst_row, 1), :], sem)


def _invert_permutation_kernel(pos_ref, inv_ref):
    def body(t, carry):
        inv_ref[pos_ref[t]] = t
        return carry

    lax.fori_loop(0, pos_ref.shape[0], body, 0, unroll=8)


def _invert_permutation(pos):
    return pl.pallas_call(
        _invert_permutation_kernel,
        in_specs=[pl.BlockSpec(memory_space=pltpu.SMEM)],
        out_specs=pl.BlockSpec(memory_space=pltpu.SMEM),
        out_shape=jax.ShapeDtypeStruct(pos.shape, jnp.int32),
        name="invert_permutation",
    )(pos)


def _expert_pair_kernel(tile_ref, ea_ref, eb_ref, lo_ref, hi_ref, tok_ref, rows_hbm_ref, gffn_ref, gfin_ref,
                        wga_ref, wua_ref, wda_ref, wgb_ref, wub_ref, wdb_ref, out_hbm_ref,
                        buf_ref, acc_ref, gather_sem, scatter_sem):
    del ea_ref, eb_ref
    tm = EXPERT_TILE
    j = pl.program_id(0)
    last = pl.num_programs(0) - 1
    slot = j % 2
    other = 1 - slot

    def start_gather(item, dst_slot):
        base = tile_ref[item] * tm
        for r in range(tm):
            _row_copy(rows_hbm_ref, tok_ref[base + r], buf_ref.at[dst_slot], r, gather_sem.at[dst_slot]).start()

    def wait_gather(dst_slot):
        pltpu.make_async_copy(rows_hbm_ref.at[pl.ds(0, tm), :], buf_ref.at[dst_slot],
                              gather_sem.at[dst_slot]).wait()

    def start_scatter(item, src_slot):
        base = tile_ref[item] * tm
        for r in range(tm):
            _row_copy(acc_ref.at[src_slot], r, out_hbm_ref, tok_ref[base + r], scatter_sem).start()

    def wait_scatter():
        pltpu.make_async_copy(acc_ref.at[0], out_hbm_ref.at[pl.ds(0, tm), :], scatter_sem).wait()

    @pl.when(j == 0)
    def _():
        acc_ref[...] = jnp.zeros_like(acc_ref)
        start_gather(0, 0)
        start_scatter(0, 1)

    wait_scatter()
    start_scatter(jnp.maximum(j - 1, 0), other)
    wait_gather(slot)
    start_gather(jnp.minimum(j + 1, last), other)

    lo = lo_ref[j]
    hi = hi_ref[j]
    x1 = buf_ref[slot, :, :D_MODEL]
    wts = buf_ref[slot, :, D_MODEL:]
    lane = lax.broadcasted_iota(jnp.int32, wts.shape, 1)
    w_a = jnp.sum(jnp.where(lane == 0, wts, 0.0), axis=-1, keepdims=True)
    w_b = jnp.sum(jnp.where(lane == 1, wts, 0.0), axis=-1, keepdims=True)
    h = _rmsnorm(x1, gffn_ref[...]).astype(BF16)

    def gated(wg_ref, wu_ref, w):
        a = _dot(h, wg_ref[0])
        v = _dot(h, wu_ref[0])
        return (a * (1.0 / (1.0 + jnp.exp(-a))) * v * w).astype(BF16)

    y = (_dot(gated(wga_ref, wua_ref, w_a), wda_ref[0])
         + _dot(gated(wgb_ref, wub_ref, w_b), wdb_ref[0]))
    res = _rmsnorm(x1 + y, gfin_ref[...])

    row = lax.broadcasted_iota(jnp.int32, (tm, 1), 0)
    mine = (row >= lo) & (row < hi)
    acc_ref[slot] = jnp.where(mine, res, jnp.where(lo > 0, acc_ref[other], 0.0))

    @pl.when(j == last)
    def _():
        wait_scatter()
        start_scatter(j, slot)
        wait_scatter()
        wait_gather(other)


def _expert_pairs(item_tile, item_ea, item_eb, item_lo, item_hi, slot_token, rows, g_ffn, g_final, wg, wu, wd):
    n, w = rows.shape
    d, f = D_MODEL, EXPERT_FF
    tm = EXPERT_TILE
    gate_a = pl.BlockSpec((1, d, f), lambda j, t, ea, eb, lo, hi, tok: (ea[j], 0, 0))
    gate_b = pl.BlockSpec((1, d, f), lambda j, t, ea, eb, lo, hi, tok: (eb[j], 0, 0))
    down_a = pl.BlockSpec((1, f, d), lambda j, t, ea, eb, lo, hi, tok: (ea[j], 0, 0))
    down_b = pl.BlockSpec((1, f, d), lambda j, t, ea, eb, lo, hi, tok: (eb[j], 0, 0))
    gain = pl.BlockSpec((1, d), lambda j, t, ea, eb, lo, hi, tok: (0, 0))
    grid_spec = pltpu.PrefetchScalarGridSpec(
        num_scalar_prefetch=6,
        grid=(item_tile.shape[0],),
        in_specs=[pl.BlockSpec(memory_space=pl.ANY), gain, gain,
                  gate_a, gate_a, down_a, gate_b, gate_b, down_b],
        out_specs=pl.BlockSpec(memory_space=pl.ANY),
        scratch_shapes=[pltpu.VMEM((2, tm, w), F32), pltpu.VMEM((2, tm, d), F32),
                        pltpu.SemaphoreType.DMA((2,)), pltpu.SemaphoreType.DMA(())],
    )
    vmem = 2 * 6 * d * f * 2 + 2 * tm * w * 4 + 2 * tm * d * 4 + 8 * tm * d * 4
    return pl.pallas_call(
        _expert_pair_kernel,
        grid_spec=grid_spec,
        out_shape=jax.ShapeDtypeStruct((n, d), F32),
        compiler_params=pltpu.CompilerParams(
            dimension_semantics=("arbitrary",), vmem_limit_bytes=vmem + 4 * MIB),
        name="expert_pairs",
    )(item_tile, item_ea, item_eb, item_lo, item_hi, slot_token, rows, g_ffn.reshape(1, d),
      g_final.reshape(1, d), wg, wu, wd, wg, wu, wd)


def kernel(x, g_mix, w_in, w_pool, pool_scale, w_fourier, w_out, g_ffn, w_group_router,
           b_group_router, w_expert_router, b_expert_router, w_gate, w_up, w_down, g_final):
    b, s, d = x.shape
    assert d == D_MODEL and s % (2 * SEQ_TILE) == 0 and s % (2 * TOKEN_TILE) == 0
    assert (b * s) % max(TOKEN_TILE, PERM_TILE) == 0
    n = b * s
    x2 = x.reshape(n, d)

    cw, sw = _fourier_weights(w_fourier, s)
    u, (w_out_bf16, w_gate_bf16) = _norm_proj(x2, g_mix, w_in.astype(BF16), [w_out, w_gate])
    mixed_lo, mixed_hi, w_up_bf16 = _mix(u.reshape(b, s, d), w_pool.astype(BF16), pool_scale, cw, sw, w_up)

    wr = jnp.concatenate([w_group_router, w_expert_router], axis=1)
    wr = jnp.pad(wr, ((0, 0), (0, ROUTER_LANES - wr.shape[1]))).astype(BF16)
    br = jnp.concatenate([b_group_router, b_expert_router])
    br = jnp.pad(br, (0, ROUTER_LANES - br.shape[0])).reshape(1, ROUTER_LANES)

    rows, route, counts, w_down_bf16 = _out_proj_route(
        x2, mixed_lo.reshape(n // 2, d), mixed_hi.reshape(n // 2, d), s, w_out_bf16, g_ffn, wr, br, w_down)
    pos, item_tile, item_ea, item_eb, item_lo, item_hi = _routing_plan(route, counts, n)
    slot_token = _invert_permutation(pos)
    out = _expert_pairs(item_tile, item_ea, item_eb, item_lo, item_hi, slot_token, rows, g_ffn, g_final,
                        w_gate_bf16, w_up_bf16, w_down_bf16)
    return out.reshape(b, s, d)
```

```python
import functools

import numpy as np
import jax
import jax.numpy as jnp
from jax import lax
from jax.experimental import pallas as pl
from jax.experimental.pallas import tpu as pltpu

D_MODEL = 2048
POOL_WINDOWS = (2, 4, 8, 16)
N_POOL_GROUPS = len(POOL_WINDOWS)
POOL_WIDTH = D_MODEL // 2
POOL_GROUP_DIM = POOL_WIDTH // N_POOL_GROUPS
FOURIER_WIDTH = D_MODEL - POOL_WIDTH
N_FOURIER_HEADS = 4
FOURIER_HEAD_DIM = FOURIER_WIDTH // N_FOURIER_HEADS
N_EXPERT_GROUPS = 4
EXPERTS_PER_GROUP = 4
N_EXPERTS = N_EXPERT_GROUPS * EXPERTS_PER_GROUP
EXPERT_FF = D_MODEL // 4
RMS_EPS = 1e-6

LANES = 128
SUBLANES = 8
BF16_SUBLANES = 16
ROUTER_LANES = LANES
MIB = 1024 * 1024

TOKEN_TILE = 512
SEQ_TILE = 256
POOL_HALO = BF16_SUBLANES
DFT_ROWS = SEQ_TILE + BF16_SUBLANES
TWIDDLE_ROWS = 16

PAIR_SLOT_A = (0, 0, 0, 1, 1, 3)
PAIR_SLOT_B = (1, 2, 3, 3, 2, 2)
PAIRS_PER_GROUP = len(PAIR_SLOT_A)
N_CLASSES = N_EXPERT_GROUPS * PAIRS_PER_GROUP
ROW_WORDS = D_MODEL + LANES
RANK_RADIX = 128
EXPERT_TILE = 256
DMA_PARTS = 8

BF16 = jnp.bfloat16
F32 = jnp.float32


def _rmsnorm(x, g):
    ms = jnp.mean(x * x, axis=-1, keepdims=True)
    return x * lax.rsqrt(ms + RMS_EPS) * g


def _dot(a, b):
    return jnp.dot(a, b, preferred_element_type=F32)


def _twiddle(rows, cols, period):
    m = (np.asarray(rows, np.int64)[:, None] * np.asarray(cols, np.int64)[None, :]) % period
    ang = (2.0 * np.pi / period) * m.astype(np.float64)
    return np.cos(ang).astype(np.float32), np.sin(ang).astype(np.float32)


def _pool_band(seq_len, tile, halo):
    n_tiles = seq_len // tile
    out = np.zeros((3, N_POOL_GROUPS, tile, tile + 2 * halo), np.float64)
    for v, m in enumerate((0, 1, n_tiles - 1)):
        t0 = m * tile
        for g, k in enumerate(POOL_WINDOWS):
            for r in range(tile):
                t = t0 + r
                lo = max(t - (k - 1) // 2, 0)
                hi = min(t + k // 2 + 1, seq_len)
                out[v, g, r, lo - t0 + halo:hi - t0 + halo] = 1.0 / (hi - lo)
                out[v, g, r, r + halo] -= 1.0
    return out.astype(np.float32)


def _fourier_weight_kernel(cd_ref, sd_ref, w_ref, cw_ref, sw_ref, *, scale):
    w = w_ref[0]
    cw = jnp.dot(cd_ref[...], w, preferred_element_type=F32, precision=lax.Precision.HIGHEST)
    sw = jnp.dot(sd_ref[...], w, preferred_element_type=F32, precision=lax.Precision.HIGHEST)
    cw_ref[0] = (cw * scale).astype(BF16)
    sw_ref[0] = (sw * (-scale)).astype(BF16)


def _fourier_weights(w_fourier, seq_len):
    dh = FOURIER_HEAD_DIM
    cd, sd = _twiddle(np.arange(dh), np.arange(dh), dh)
    scale = 1.0 / np.sqrt(float(seq_len * dh))
    mat = pl.BlockSpec((dh, dh), lambda h: (0, 0))
    per_head = pl.BlockSpec((1, dh, dh), lambda h: (h, 0, 0))
    return pl.pallas_call(
        functools.partial(_fourier_weight_kernel, scale=scale),
        grid=(N_FOURIER_HEADS,),
        in_specs=[mat, mat, per_head],
        out_specs=[per_head, per_head],
        out_shape=[jax.ShapeDtypeStruct((N_FOURIER_HEADS, dh, dh), BF16)] * 2,
        name="fourier_weights",
    )(jnp.asarray(cd), jnp.asarray(sd), w_fourier)


class _CastAlong:
    def __init__(self, w, n_chunks, chunk_of):
        cols = w.shape[-1]
        rows = w.size // cols // n_chunks
        self.shape = w.shape
        self.src = w.reshape(n_chunks, rows, cols)
        self.spec = pl.BlockSpec((1, rows, cols), lambda *idx: (chunk_of(*idx), 0, 0))
        self.out_shape = jax.ShapeDtypeStruct((n_chunks, rows, cols), BF16)
        self.vmem_bytes = 2 * rows * cols * (4 + 2)


def _cast_chunks(src_refs, dst_refs):
    for src_ref, dst_ref in zip(src_refs, dst_refs):
        dst_ref[...] = src_ref[...].astype(BF16)


def _norm_proj_kernel(x_ref, g_ref, w_ref, *rest, n_cast):
    cast_src, (u_ref, *cast_dst) = rest[:n_cast], rest[n_cast:]
    h = _rmsnorm(x_ref[...], g_ref[...])
    u_ref[...] = _dot(h.astype(BF16), w_ref[...]).astype(BF16)
    _cast_chunks(cast_src, cast_dst)


def _norm_proj(x2, g_mix, w_in_bf16, cast_weights):
    n, d = x2.shape
    tm = TOKEN_TILE
    casts = [_CastAlong(w, n // tm, lambda i: i) for w in cast_weights]
    vmem = (2 * tm * d * 4 + d * d * 2 + 2 * tm * d * 2 + 3 * tm * d * 4
            + sum(c.vmem_bytes for c in casts))
    u, *cast_out = pl.pallas_call(
        functools.partial(_norm_proj_kernel, n_cast=len(casts)),
        grid=(n // tm,),
        in_specs=[
            pl.BlockSpec((tm, d), lambda i: (i, 0)),
            pl.BlockSpec((1, d), lambda i: (0, 0)),
            pl.BlockSpec((d, d), lambda i: (0, 0), pipeline_mode=pl.Buffered(1)),
        ] + [c.spec for c in casts],
        out_specs=[pl.BlockSpec((tm, d), lambda i: (i, 0))] + [c.spec for c in casts],
        out_shape=[jax.ShapeDtypeStruct((n, d), BF16)] + [c.out_shape for c in casts],
        compiler_params=pltpu.CompilerParams(
            dimension_semantics=("arbitrary",), vmem_limit_bytes=vmem + 4 * MIB),
        name="norm_proj",
    )(x2, g_mix.reshape(1, d), w_in_bf16, *[c.src for c in casts])
    return u, [o.reshape(c.shape) for o, c in zip(cast_out, casts)]


def _mix_kernel(uf_ref, up_lo_ref, prev_lo_ref, next_lo_ref, up_hi_ref, prev_hi_ref, next_hi_ref,
                c0_ref, s0_ref, cph_ref, sph_ref, band_lo_ref, band_hi_ref, rev_ref,
                wpool_ref, pscale_ref, cw_ref, sw_ref, cast_src_ref, lo_ref, hi_ref, cast_dst_ref, lhs_ref):
    t = SEQ_TILE
    tp = DFT_ROWS
    gd = POOL_GROUP_DIM
    hd = FOURIER_HEAD_DIM

    @pl.when(pl.program_id(1) == 0)
    def _():
        cph = cph_ref[0]
        sph = sph_ref[0]
        for r0 in range(0, tp, TWIDDLE_ROWS):
            c0 = c0_ref[r0:r0 + TWIDDLE_ROWS, :]
            s0 = s0_ref[r0:r0 + TWIDDLE_ROWS, :]
            lhs_ref[r0:r0 + TWIDDLE_ROWS, :] = (cph * c0 - sph * s0).astype(BF16)
            lhs_ref[tp + r0:tp + r0 + TWIDDLE_ROWS, :] = (sph * c0 + cph * s0).astype(BF16)

    _cast_chunks([cast_src_ref], [cast_dst_ref])
    pq = _dot(lhs_ref[...], uf_ref[...])
    for h in range(N_FOURIER_HEADS):
        cols = slice(h * hd, (h + 1) * hd)
        out_cols = slice(POOL_WIDTH + h * hd, POOL_WIDTH + (h + 1) * hd)
        pc = _dot(pq[:tp, cols].astype(BF16), cw_ref[h])
        qs = _dot(pq[tp:, cols].astype(BF16), sw_ref[h])
        lo_ref[:, out_cols] = (pc + qs)[:t].astype(BF16)
        hi_ref[:, out_cols] = _dot(rev_ref[...], (pc - qs).astype(BF16)).astype(BF16)

    def pool(up_ref, prev_ref, next_ref, band_ref, out_ref):
        win = jnp.concatenate([prev_ref[...], up_ref[...], next_ref[...]], axis=0)
        for g in range(N_POOL_GROUPS):
            cols = slice(g * gd, (g + 1) * gd)
            pooled = _dot(band_ref[0, g], win[:, cols]).astype(BF16)
            y = _dot(pooled, wpool_ref[g]) * pscale_ref[:, cols]
            out_ref[:, cols] = y.astype(BF16)

    pool(up_lo_ref, prev_lo_ref, next_lo_ref, band_lo_ref, lo_ref)
    pool(up_hi_ref, prev_hi_ref, next_hi_ref, band_hi_ref, hi_ref)


def _mix(u3, w_pool_bf16, pool_scale, cw, sw, cast_weight):
    b, s, d = u3.shape
    t = SEQ_TILE
    tp = DFT_ROWS
    halo = POOL_HALO
    n_tiles = s // t
    n_steps = n_tiles // 2
    halo_blocks_per_tile = t // halo
    last_halo_block = s // halo - 1

    c0, s0 = _twiddle(np.arange(tp), np.arange(s), s)
    cph, sph = _twiddle(np.arange(0, s // 2, t), np.arange(s), s)
    band = jnp.asarray(_pool_band(s, t, halo)).astype(BF16)
    rev = np.zeros((t, tp), np.float32)
    rev[np.arange(t), t - np.arange(t)] = 1.0

    hi_tile = lambda m: n_tiles - 1 - m
    tile_spec = lambda tile_of: pl.BlockSpec((None, t, POOL_WIDTH), lambda m, bi: (bi, tile_of(m), 0))
    prev_spec = lambda tile_of: pl.BlockSpec(
        (None, halo, POOL_WIDTH),
        lambda m, bi: (bi, jnp.maximum(tile_of(m) * halo_blocks_per_tile - 1, 0), 0))
    next_spec = lambda tile_of: pl.BlockSpec(
        (None, halo, POOL_WIDTH),
        lambda m, bi: (bi, jnp.minimum((tile_of(m) + 1) * halo_blocks_per_tile, last_halo_block), 0))
    band_shape = (1, N_POOL_GROUPS, t, t + 2 * halo)
    const2 = lambda shape: pl.BlockSpec(shape, lambda m, bi: (0, 0))
    const3 = lambda shape: pl.BlockSpec(shape, lambda m, bi: (0, 0, 0))
    lo_tile = lambda m: m

    in_specs = [
        pl.BlockSpec((None, s, FOURIER_WIDTH), lambda m, bi: (bi, 0, 1)),
        tile_spec(lo_tile), prev_spec(lo_tile), next_spec(lo_tile),
        tile_spec(hi_tile), prev_spec(hi_tile), next_spec(hi_tile),
        pl.BlockSpec((tp, s), lambda m, bi: (0, 0), pipeline_mode=pl.Buffered(1)),
        pl.BlockSpec((tp, s), lambda m, bi: (0, 0), pipeline_mode=pl.Buffered(1)),
        pl.BlockSpec((1, 1, s), lambda m, bi: (m, 0, 0)),
        pl.BlockSpec((1, 1, s), lambda m, bi: (m, 0, 0)),
        pl.BlockSpec(band_shape, lambda m, bi: (jnp.where(m == 0, 0, 1), 0, 0, 0)),
        pl.BlockSpec(band_shape, lambda m, bi: (jnp.where(m == 0, 2, 1), 0, 0, 0)),
        const2((t, tp)),
        const3((N_POOL_GROUPS, POOL_GROUP_DIM, POOL_GROUP_DIM)),
        const2((1, POOL_WIDTH)),
        const3((N_FOURIER_HEADS, FOURIER_HEAD_DIM, FOURIER_HEAD_DIM)),
        const3((N_FOURIER_HEADS, FOURIER_HEAD_DIM, FOURIER_HEAD_DIM)),
    ]
    half_out = pl.BlockSpec((None, t, d), lambda m, bi: (bi, m, 0))
    hi_out = pl.BlockSpec((None, t, d), lambda m, bi: (bi, n_steps - 1 - m, 0))
    cast = _CastAlong(cast_weight, n_steps * b, lambda m, bi: m * b + bi)
    vmem = (2 * s * FOURIER_WIDTH * 2
            + 2 * tp * s * 4
            + 2 * tp * s * 2
            + 2 * 2 * t * d * 2 * 2
            + 4 * 2 * tp * FOURIER_WIDTH * 4
            + cast.vmem_bytes)
    lo, hi, cast_out = pl.pallas_call(
        _mix_kernel,
        grid=(n_steps, b),
        in_specs=in_specs + [cast.spec],
        out_specs=[half_out, hi_out, cast.spec],
        out_shape=[jax.ShapeDtypeStruct((b, s // 2, d), BF16)] * 2 + [cast.out_shape],
        scratch_shapes=[pltpu.VMEM((2 * tp, s), BF16)],
        compiler_params=pltpu.CompilerParams(
            dimension_semantics=("arbitrary", "arbitrary"), vmem_limit_bytes=vmem + 4 * MIB),
        name="seq_mix",
    )(u3, u3, u3, u3, u3, u3, u3, jnp.asarray(c0), jnp.asarray(s0),
      jnp.asarray(cph).reshape(n_steps, 1, s), jnp.asarray(sph).reshape(n_steps, 1, s),
      band, band, jnp.asarray(rev).astype(BF16),
      w_pool_bf16, pool_scale.reshape(1, POOL_WIDTH), cw, sw, cast.src)
    return lo, hi, cast_out.reshape(cast.shape)


def _out_proj_route_kernel(x_ref, mixed_lo_ref, mixed_hi_ref, wout_ref, g_ref, wr_ref, br_ref,
                           tri_ref, pick_ref, cast_src_ref, rows_ref, route_ref, counts_ref, cast_dst_ref,
                           carry_ref, x1_ref, *, tiles_per_seq, n_tiles):
    step = pl.program_id(0)

    @pl.when(step == 0)
    def _():
        carry_ref[...] = jnp.zeros_like(carry_ref)
        x1_ref[...] = jnp.zeros_like(x1_ref)

    _cast_chunks([cast_src_ref], [cast_dst_ref])

    x1 = x1_ref[...]
    rows_ref[:, :D_MODEL] = x1
    h2 = _rmsnorm(x1, g_ref[...])
    logits = _dot(h2.astype(BF16), wr_ref[...]) + br_ref[...]

    tile = jnp.minimum(step, n_tiles - 1)
    in_lo_half = (tile % tiles_per_seq) < tiles_per_seq // 2
    mixed = jnp.where(in_lo_half, mixed_lo_ref[...], mixed_hi_ref[...])
    x1_ref[...] = x_ref[...] + _dot(mixed, wout_ref[...])

    lane = lax.broadcasted_iota(jnp.int32, logits.shape, 1)
    neg = jnp.float32(-jnp.inf)
    big = jnp.int32(ROUTER_LANES)

    is_group = lane < N_EXPERT_GROUPS
    gl = jnp.where(is_group, logits, neg)
    gmax = jnp.max(gl, axis=-1, keepdims=True)
    gidx = jnp.min(jnp.where(gl == gmax, lane, big), axis=-1, keepdims=True)
    p_g = 1.0 / jnp.sum(jnp.exp(gl - gmax), axis=-1, keepdims=True)

    e_lane = lane - N_EXPERT_GROUPS
    in_group = (e_lane >= gidx * EXPERTS_PER_GROUP) & (e_lane < (gidx + 1) * EXPERTS_PER_GROUP)
    el = jnp.where(in_group, logits, neg)
    v1 = jnp.max(el, axis=-1, keepdims=True)
    i1 = jnp.min(jnp.where(el == v1, lane, big), axis=-1, keepdims=True)
    el2 = jnp.where(lane == i1, neg, el)
    v2 = jnp.max(el2, axis=-1, keepdims=True)
    i2 = jnp.min(jnp.where(el2 == v2, lane, big), axis=-1, keepdims=True)
    r = jnp.exp(v2 - v1)
    w1 = p_g / (1.0 + r)
    w2 = p_g * r / (1.0 + r)

    first_is_low = i1 < i2
    first_lane = N_EXPERT_GROUPS + gidx * EXPERTS_PER_GROUP
    la = jnp.where(first_is_low, i1, i2) - first_lane
    lb = jnp.where(first_is_low, i2, i1) - first_lane
    w_low = jnp.where(first_is_low, w1, w2)
    w_high = jnp.where(first_is_low, w2, w1)
    pair = jnp.where(la == 0, lb - 1, jnp.where(la == 1, 6 - lb, 5))
    slot_a_is_high = la == 2
    w_a = jnp.where(slot_a_is_high, w_high, w_low)
    w_b = jnp.where(slot_a_is_high, w_low, w_high)
    cls = gidx * PAIRS_PER_GROUP + pair
    rows_ref[:, D_MODEL:] = jnp.where(lane == 0, w_a, jnp.where(lane == 1, w_b, 0.0))

    onehot = jnp.where(lane == cls, 1.0, 0.0)
    before = _dot(tri_ref[...], onehot.astype(BF16)) + carry_ref[...]
    rank = jnp.sum(jnp.where(lane == cls, before, 0.0), axis=-1, keepdims=True)
    carry_ref[...] += jnp.sum(onehot, axis=0, keepdims=True) * jnp.where(step > 0, 1.0, 0.0)
    counts_ref[...] = carry_ref[...]

    rank_hi = jnp.floor(rank * (1.0 / RANK_RADIX))
    rank_lo = rank - rank_hi * RANK_RADIX
    digits = jnp.where(lane == 0, cls.astype(F32),
                       jnp.where(lane == 1, rank_hi, jnp.where(lane == 2, rank_lo, 0.0)))
    route_ref[...] = lax.dot_general(pick_ref[...], digits.astype(BF16), (((1,), (1,)), ((), ())),
                                     preferred_element_type=F32)


def _out_proj_route(x2, mixed_lo, mixed_hi, seq_len, w_out_bf16, g_ffn, wr, br, cast_weight):
    n, d = x2.shape
    tm = TOKEN_TILE
    n_tiles = n // tm
    tiles_per_seq = seq_len // tm
    half_tiles = tiles_per_seq // 2
    const = lambda shape: pl.BlockSpec(shape, lambda i: (0, 0))
    in_tile = lambda i: jnp.minimum(i, n_tiles - 1)
    out_tile = lambda i: jnp.maximum(i - 1, 0)
    lo_spec = pl.BlockSpec((tm, d), lambda i: (
        (in_tile(i) // tiles_per_seq) * half_tiles + jnp.minimum(in_tile(i) % tiles_per_seq, half_tiles - 1), 0))
    hi_spec = pl.BlockSpec((tm, d), lambda i: (
        (in_tile(i) // tiles_per_seq) * half_tiles + jnp.maximum(in_tile(i) % tiles_per_seq - half_tiles, 0), 0))
    tri = np.tril(np.ones((tm, tm), np.float32), -1)
    pick = np.eye(SUBLANES, ROUTER_LANES, dtype=np.float32)
    vmem = (2 * tm * d * 4 + 4 * tm * d * 2 + 2 * tm * ROW_WORDS * 4 + d * d * 2 + tm * d * 4
            + 2 * d * ROUTER_LANES * 2 + 2 * tm * tm * 2 + 3 * tm * d * 4)
    cast = _CastAlong(cast_weight, n_tiles, in_tile)
    vmem += cast.vmem_bytes
    rows, route, counts, cast_out = pl.pallas_call(
        functools.partial(_out_proj_route_kernel, tiles_per_seq=tiles_per_seq, n_tiles=n_tiles),
        grid=(n_tiles + 1,),
        in_specs=[pl.BlockSpec((tm, d), lambda i: (in_tile(i), 0)), lo_spec, hi_spec,
                  pl.BlockSpec((d, d), lambda i: (0, 0), pipeline_mode=pl.Buffered(1)),
                  const((1, d)), const((d, ROUTER_LANES)), const((1, ROUTER_LANES)),
                  const((tm, tm)), const((SUBLANES, ROUTER_LANES)), cast.spec],
        out_specs=[pl.BlockSpec((tm, ROW_WORDS), lambda i: (out_tile(i), 0)),
                   pl.BlockSpec((SUBLANES, tm), lambda i: (0, out_tile(i))),
                   const((1, ROUTER_LANES)), cast.spec],
        out_shape=[jax.ShapeDtypeStruct((n, ROW_WORDS), F32),
                   jax.ShapeDtypeStruct((SUBLANES, n), F32),
                   jax.ShapeDtypeStruct((1, ROUTER_LANES), F32), cast.out_shape],
        scratch_shapes=[pltpu.VMEM((1, ROUTER_LANES), F32), pltpu.VMEM((tm, d), F32)],
        compiler_params=pltpu.CompilerParams(
            dimension_semantics=("arbitrary",), vmem_limit_bytes=vmem + 4 * MIB),
        name="out_proj_route",
    )(x2, mixed_lo, mixed_hi, w_out_bf16, g_ffn.reshape(1, d), wr, br,
      jnp.asarray(tri).astype(BF16), jnp.asarray(pick).astype(BF16), cast.src)
    return rows, route, counts, cast_out.reshape(cast.shape)


def _routing_plan(route, counts, n):
    tm = EXPERT_TILE
    cnt = counts[0, :N_CLASSES].astype(jnp.int32)
    class_end = jnp.cumsum(cnt)
    class_start = class_end - cnt
    cls = route[0].astype(jnp.int32)
    rank = (route[1] * RANK_RADIX + route[2]).astype(jnp.int32)
    pos = class_start[cls] + rank

    first_tile = class_start // tm
    items_per_class = jnp.where(cnt > 0, (class_end - 1) // tm - first_tile + 1, 0)
    item_end = jnp.cumsum(items_per_class)
    item_start = item_end - items_per_class
    n_items = n // tm + N_CLASSES
    item = jnp.arange(n_items, dtype=jnp.int32)
    item_c = jnp.minimum(item, item_end[-1] - 1)
    item_cls = jnp.sum((item_c[:, None] >= item_end[None, :]).astype(jnp.int32), axis=1)
    item_tile = first_tile[item_cls] + item_c - item_start[item_cls]
    lo = jnp.clip(class_start[item_cls] - item_tile * tm, 0, tm)
    hi = jnp.clip(class_end[item_cls] - item_tile * tm, 0, tm)
    group_base = np.repeat(np.arange(N_EXPERT_GROUPS) * EXPERTS_PER_GROUP, PAIRS_PER_GROUP)
    class_ea = jnp.asarray(group_base + np.tile(PAIR_SLOT_A, N_EXPERT_GROUPS), jnp.int32)
    class_eb = jnp.asarray(group_base + np.tile(PAIR_SLOT_B, N_EXPERT_GROUPS), jnp.int32)
    return pos, item_tile, class_ea[item_cls], class_eb[item_cls], lo, hi


def _row_copy(src_ref, src_row, dst_ref, dst_row, sem):
    return pltpu.make_async_copy(src_ref.at[pl.ds(src_row, 1), :], dst_ref.at[pl.ds(dst_row, 1), :], sem)


def _invert_permutation_kernel(pos_ref, inv_ref):
    def body(t, carry):
        inv_ref[pos_ref[t]] = t
        return carry

    lax.fori_loop(0, pos_ref.shape[0], body, 0, unroll=8)


def _invert_permutation(pos):
    return pl.pallas_call(
        _invert_permutation_kernel,
        in_specs=[pl.BlockSpec(memory_space=pltpu.SMEM)],
        out_specs=pl.BlockSpec(memory_space=pltpu.SMEM),
        out_shape=jax.ShapeDtypeStruct(pos.shape, jnp.int32),
        name="invert_permutation",
    )(pos)


def _expert_pair_kernel(tile_ref, ea_ref, eb_ref, lo_ref, hi_ref, tok_ref, rows_hbm_ref, gffn_ref, gfin_ref,
                        wga_ref, wua_ref, wda_ref, wgb_ref, wub_ref, wdb_ref, out_hbm_ref,
                        buf_ref, acc_ref, gather_sem, scatter_sem):
    del ea_ref, eb_ref
    tm = EXPERT_TILE
    j = pl.program_id(0)
    last = pl.num_programs(0) - 1
    slot = j % 2
    other = 1 - slot

    def start_gather(item, dst_slot, rows=range(tm)):
        base = tile_ref[item] * tm
        for r in rows:
            _row_copy(rows_hbm_ref, tok_ref[base + r], buf_ref.at[dst_slot], r, gather_sem.at[dst_slot]).start()

    def wait_gather(dst_slot):
        pltpu.make_async_copy(rows_hbm_ref.at[pl.ds(0, tm), :], buf_ref.at[dst_slot],
                              gather_sem.at[dst_slot]).wait()

    def start_scatter(item, src_slot, rows=range(tm)):
        base = tile_ref[item] * tm
        for r in rows:
            _row_copy(acc_ref.at[src_slot], r, out_hbm_ref, tok_ref[base + r], scatter_sem).start()

    def wait_scatter():
        pltpu.make_async_copy(acc_ref.at[0], out_hbm_ref.at[pl.ds(0, tm), :], scatter_sem).wait()

    @pl.when(j == 0)
    def _():
        acc_ref[...] = jnp.zeros_like(acc_ref)
        start_gather(0, 0)
        start_scatter(0, 1)

    wait_scatter()
    wait_gather(slot)
    prev_item = jnp.maximum(j - 1, 0)
    next_item = jnp.minimum(j + 1, last)

    def issue_copies(part):
        rows = range(part * tm // DMA_PARTS, (part + 1) * tm // DMA_PARTS)
        start_scatter(prev_item, other, rows)
        start_gather(next_item, other, rows)

    lo = lo_ref[j]
    hi = hi_ref[j]
    x1 = buf_ref[slot, :, :D_MODEL]
    h = _rmsnorm(x1, gffn_ref[...]).astype(BF16)

    part = 0
    ff_chunk = EXPERT_FF * 4 // DMA_PARTS
    for slot_lane, (wg_ref, wu_ref, wd_ref) in enumerate(((wga_ref, wua_ref, wda_ref),
                                                          (wgb_ref, wub_ref, wdb_ref))):
        for c0 in range(0, EXPERT_FF, ff_chunk):
            issue_copies(part)
            issue_copies(part + 1)
            wts = buf_ref[slot, :, D_MODEL:]
            lane = lax.broadcasted_iota(jnp.int32, wts.shape, 1)
            w = jnp.sum(jnp.where(lane == slot_lane, wts, 0.0), axis=-1, keepdims=True)
            a = _dot(h, wg_ref[0, :, c0:c0 + ff_chunk])
            v = _dot(h, wu_ref[0, :, c0:c0 + ff_chunk])
            act = (a * (1.0 / (1.0 + jnp.exp(-a))) * v * w).astype(BF16)
            y_part = _dot(act, wd_ref[0, c0:c0 + ff_chunk, :])
            if part == 0:
                acc_ref[slot] = y_part
            else:
                acc_ref[slot] += y_part
            part += 2
    res = _rmsnorm(buf_ref[slot, :, :D_MODEL] + acc_ref[slot], gfin_ref[...])

    row = lax.broadcasted_iota(jnp.int32, (tm, 1), 0)
    mine = (row >= lo) & (row < hi)
    acc_ref[slot] = jnp.where(mine, res, jnp.where(lo > 0, acc_ref[other], 0.0))

    @pl.when(j == last)
    def _():
        wait_scatter()
        start_scatter(j, slot)
        wait_scatter()
        wait_gather(other)


def _expert_pairs(item_tile, item_ea, item_eb, item_lo, item_hi, slot_token, rows, g_ffn, g_final, wg, wu, wd):
    n, w = rows.shape
    d, f = D_MODEL, EXPERT_FF
    tm = EXPERT_TILE
    gate_a = pl.BlockSpec((1, d, f), lambda j, t, ea, eb, lo, hi, tok: (ea[j], 0, 0))
    gate_b = pl.BlockSpec((1, d, f), lambda j, t, ea, eb, lo, hi, tok: (eb[j], 0, 0))
    down_a = pl.BlockSpec((1, f, d), lambda j, t, ea, eb, lo, hi, tok: (ea[j], 0, 0))
    down_b = pl.BlockSpec((1, f, d), lambda j, t, ea, eb, lo, hi, tok: (eb[j], 0, 0))
    gain = pl.BlockSpec((1, d), lambda j, t, ea, eb, lo, hi, tok: (0, 0))
    grid_spec = pltpu.PrefetchScalarGridSpec(
        num_scalar_prefetch=6,
        grid=(item_tile.shape[0],),
        in_specs=[pl.BlockSpec(memory_space=pl.ANY), gain, gain,
                  gate_a, gate_a, down_a, gate_b, gate_b, down_b],
        out_specs=pl.BlockSpec(memory_space=pl.ANY),
        scratch_shapes=[pltpu.VMEM((2, tm, w), F32), pltpu.VMEM((2, tm, d), F32),
                        pltpu.SemaphoreType.DMA((2,)), pltpu.SemaphoreType.DMA(())],
    )
    vmem = 2 * 6 * d * f * 2 + 2 * tm * w * 4 + 2 * tm * d * 4 + 8 * tm * d * 4
    return pl.pallas_call(
        _expert_pair_kernel,
        grid_spec=grid_spec,
        out_shape=jax.ShapeDtypeStruct((n, d), F32),
        compiler_params=pltpu.CompilerParams(
            dimension_semantics=("arbitrary",), vmem_limit_bytes=vmem + 4 * MIB),
        name="expert_pairs",
    )(item_tile, item_ea, item_eb, item_lo, item_hi, slot_token, rows, g_ffn.reshape(1, d),
      g_final.reshape(1, d), wg, wu, wd, wg, wu, wd)


def kernel(x, g_mix, w_in, w_pool, pool_scale, w_fourier, w_out, g_ffn, w_group_router,
           b_group_router, w_expert_router, b_expert_router, w_gate, w_up, w_down, g_final):
    b, s, d = x.shape
    assert d == D_MODEL and s % (2 * SEQ_TILE) == 0 and s % (2 * TOKEN_TILE) == 0
    assert (b * s) % max(TOKEN_TILE, EXPERT_TILE) == 0
    n = b * s
    x2 = x.reshape(n, d)

    cw, sw = _fourier_weights(w_fourier, s)
    u, (w_out_bf16, w_gate_bf16) = _norm_proj(x2, g_mix, w_in.astype(BF16), [w_out, w_gate])
    mixed_lo, mixed_hi, w_up_bf16 = _mix(u.reshape(b, s, d), w_pool.astype(BF16), pool_scale, cw, sw, w_up)

    wr = jnp.concatenate([w_group_router, w_expert_router], axis=1)
    wr = jnp.pad(wr, ((0, 0), (0, ROUTER_LANES - wr.shape[1]))).astype(BF16)
    br = jnp.concatenate([b_group_router, b_expert_router])
    br = jnp.pad(br, (0, ROUTER_LANES - br.shape[0])).reshape(1, ROUTER_LANES)

    rows, route, counts, w_down_bf16 = _out_proj_route(
        x2, mixed_lo.reshape(n // 2, d), mixed_hi.reshape(n // 2, d), s, w_out_bf16, g_ffn, wr, br, w_down)
    pos, item_tile, item_ea, item_eb, item_lo, item_hi = _routing_plan(route, counts, n)
    slot_token = _invert_permutation(pos)
    out = _expert_pairs(item_tile, item_ea, item_eb, item_lo, item_hi, slot_token, rows, g_ffn, g_final,
                        w_gate_bf16, w_up_bf16, w_down_bf16)
    return out.reshape(b, s, d)
```

```python
import functools

import numpy as np
import jax
import jax.numpy as jnp
from jax import lax
from jax.experimental import pallas as pl
from jax.experimental.pallas import tpu as pltpu

D_MODEL = 2048
POOL_WINDOWS = (2, 4, 8, 16)
N_POOL_GROUPS = len(POOL_WINDOWS)
POOL_WIDTH = D_MODEL // 2
POOL_GROUP_DIM = POOL_WIDTH // N_POOL_GROUPS
FOURIER_WIDTH = D_MODEL - POOL_WIDTH
N_FOURIER_HEADS = 4
FOURIER_HEAD_DIM = FOURIER_WIDTH // N_FOURIER_HEADS
N_EXPERT_GROUPS = 4
EXPERTS_PER_GROUP = 4
N_EXPERTS = N_EXPERT_GROUPS * EXPERTS_PER_GROUP
EXPERT_FF = D_MODEL // 4
RMS_EPS = 1e-6

LANES = 128
SUBLANES = 8
BF16_SUBLANES = 16
ROUTER_LANES = LANES
MIB = 1024 * 1024

TOKEN_TILE = 512
SEQ_TILE = 256
POOL_HALO = BF16_SUBLANES
DFT_ROWS = SEQ_TILE + BF16_SUBLANES
TWIDDLE_ROWS = 16

PAIR_SLOT_A = (0, 0, 0, 1, 1, 3)
PAIR_SLOT_B = (1, 2, 3, 3, 2, 2)
PAIRS_PER_GROUP = len(PAIR_SLOT_A)
N_CLASSES = N_EXPERT_GROUPS * PAIRS_PER_GROUP
ROW_WORDS = D_MODEL + LANES
RANK_RADIX = 128
EXPERT_TILE = 256
DMA_PARTS = 8

BF16 = jnp.bfloat16
F32 = jnp.float32


def _rmsnorm(x, g):
    ms = jnp.mean(x * x, axis=-1, keepdims=True)
    return x * lax.rsqrt(ms + RMS_EPS) * g


def _dot(a, b):
    return jnp.dot(a, b, preferred_element_type=F32)


def _twiddle(rows, cols, period):
    m = (np.asarray(rows, np.int64)[:, None] * np.asarray(cols, np.int64)[None, :]) % period
    ang = (2.0 * np.pi / period) * m.astype(np.float64)
    return np.cos(ang).astype(np.float32), np.sin(ang).astype(np.float32)


def _pool_band(seq_len, tile, halo):
    n_tiles = seq_len // tile
    out = np.zeros((3, N_POOL_GROUPS, tile, tile + 2 * halo), np.float64)
    for v, m in enumerate((0, 1, n_tiles - 1)):
        t0 = m * tile
        for g, k in enumerate(POOL_WINDOWS):
            for r in range(tile):
                t = t0 + r
                lo = max(t - (k - 1) // 2, 0)
                hi = min(t + k // 2 + 1, seq_len)
                out[v, g, r, lo - t0 + halo:hi - t0 + halo] = 1.0 / (hi - lo)
                out[v, g, r, r + halo] -= 1.0
    return out.astype(np.float32)


def _fourier_weight_kernel(cd_ref, sd_ref, w_ref, cw_ref, sw_ref, *, scale):
    w = w_ref[0]
    cw = jnp.dot(cd_ref[...], w, preferred_element_type=F32, precision=lax.Precision.HIGHEST)
    sw = jnp.dot(sd_ref[...], w, preferred_element_type=F32, precision=lax.Precision.HIGHEST)
    cw_ref[0] = (cw * scale).astype(BF16)
    sw_ref[0] = (sw * (-scale)).astype(BF16)


def _fourier_weights(w_fourier, seq_len):
    dh = FOURIER_HEAD_DIM
    cd, sd = _twiddle(np.arange(dh), np.arange(dh), dh)
    scale = 1.0 / np.sqrt(float(seq_len * dh))
    mat = pl.BlockSpec((dh, dh), lambda h: (0, 0))
    per_head = pl.BlockSpec((1, dh, dh), lambda h: (h, 0, 0))
    return pl.pallas_call(
        functools.partial(_fourier_weight_kernel, scale=scale),
        grid=(N_FOURIER_HEADS,),
        in_specs=[mat, mat, per_head],
        out_specs=[per_head, per_head],
        out_shape=[jax.ShapeDtypeStruct((N_FOURIER_HEADS, dh, dh), BF16)] * 2,
        name="fourier_weights",
    )(jnp.asarray(cd), jnp.asarray(sd), w_fourier)


class _CastAlong:
    def __init__(self, w, n_chunks, chunk_of):
        cols = w.shape[-1]
        rows = w.size // cols // n_chunks
        self.shape = w.shape
        self.src = w.reshape(n_chunks, rows, cols)
        self.spec = pl.BlockSpec((1, rows, cols), lambda *idx: (chunk_of(*idx), 0, 0))
        self.out_shape = jax.ShapeDtypeStruct((n_chunks, rows, cols), BF16)
        self.vmem_bytes = 2 * rows * cols * (4 + 2)


def _cast_chunks(src_refs, dst_refs):
    for src_ref, dst_ref in zip(src_refs, dst_refs):
        dst_ref[...] = src_ref[...].astype(BF16)


def _norm_proj_kernel(x_ref, g_ref, w_ref, *rest, n_cast):
    cast_src, (u_ref, *cast_dst) = rest[:n_cast], rest[n_cast:]
    h = _rmsnorm(x_ref[...], g_ref[...])
    u_ref[...] = _dot(h.astype(BF16), w_ref[...]).astype(BF16)
    _cast_chunks(cast_src, cast_dst)


def _norm_proj(x2, g_mix, w_in_bf16, cast_weights):
    n, d = x2.shape
    tm = TOKEN_TILE
    casts = [_CastAlong(w, n // tm, lambda i: i) for w in cast_weights]
    vmem = (2 * tm * d * 4 + d * d * 2 + 2 * tm * d * 2 + 3 * tm * d * 4
            + sum(c.vmem_bytes for c in casts))
    u, *cast_out = pl.pallas_call(
        functools.partial(_norm_proj_kernel, n_cast=len(casts)),
        grid=(n // tm,),
        in_specs=[
            pl.BlockSpec((tm, d), lambda i: (i, 0)),
            pl.BlockSpec((1, d), lambda i: (0, 0)),
            pl.BlockSpec((d, d), lambda i: (0, 0), pipeline_mode=pl.Buffered(1)),
        ] + [c.spec for c in casts],
        out_specs=[pl.BlockSpec((tm, d), lambda i: (i, 0))] + [c.spec for c in casts],
        out_shape=[jax.ShapeDtypeStruct((n, d), BF16)] + [c.out_shape for c in casts],
        compiler_params=pltpu.CompilerParams(
            dimension_semantics=("arbitrary",), vmem_limit_bytes=vmem + 4 * MIB),
        name="norm_proj",
    )(x2, g_mix.reshape(1, d), w_in_bf16, *[c.src for c in casts])
    return u, [o.reshape(c.shape) for o, c in zip(cast_out, casts)]


def _mix_kernel(uf_ref, up_lo_ref, prev_lo_ref, next_lo_ref, up_hi_ref, prev_hi_ref, next_hi_ref,
                c0_ref, s0_ref, cph_ref, sph_ref, band_lo_ref, band_hi_ref, rev_ref,
                wpool_ref, pscale_ref, cw_ref, sw_ref, cast_src_ref, lo_ref, hi_ref, cast_dst_ref, lhs_ref):
    t = SEQ_TILE
    tp = DFT_ROWS
    gd = POOL_GROUP_DIM
    hd = FOURIER_HEAD_DIM

    @pl.when(pl.program_id(1) == 0)
    def _():
        cph = cph_ref[0]
        sph = sph_ref[0]
        for r0 in range(0, tp, TWIDDLE_ROWS):
            c0 = c0_ref[r0:r0 + TWIDDLE_ROWS, :]
            s0 = s0_ref[r0:r0 + TWIDDLE_ROWS, :]
            lhs_ref[r0:r0 + TWIDDLE_ROWS, :] = (cph * c0 - sph * s0).astype(BF16)
            lhs_ref[tp + r0:tp + r0 + TWIDDLE_ROWS, :] = (sph * c0 + cph * s0).astype(BF16)

    _cast_chunks([cast_src_ref], [cast_dst_ref])
    pq = _dot(lhs_ref[...], uf_ref[...])
    for h in range(N_FOURIER_HEADS):
        cols = slice(h * hd, (h + 1) * hd)
        out_cols = slice(POOL_WIDTH + h * hd, POOL_WIDTH + (h + 1) * hd)
        pc = _dot(pq[:tp, cols].astype(BF16), cw_ref[h])
        qs = _dot(pq[tp:, cols].astype(BF16), sw_ref[h])
        lo_ref[:, out_cols] = (pc + qs)[:t].astype(BF16)
        hi_ref[:, out_cols] = _dot(rev_ref[...], (pc - qs).astype(BF16)).astype(BF16)

    def pool(up_ref, prev_ref, next_ref, band_ref, out_ref):
        win = jnp.concatenate([prev_ref[...], up_ref[...], next_ref[...]], axis=0)
        for g in range(N_POOL_GROUPS):
            cols = slice(g * gd, (g + 1) * gd)
            pooled = _dot(band_ref[0, g], win[:, cols]).astype(BF16)
            y = _dot(pooled, wpool_ref[g]) * pscale_ref[:, cols]
            out_ref[:, cols] = y.astype(BF16)

    pool(up_lo_ref, prev_lo_ref, next_lo_ref, band_lo_ref, lo_ref)
    pool(up_hi_ref, prev_hi_ref, next_hi_ref, band_hi_ref, hi_ref)


def _mix(u3, w_pool_bf16, pool_scale, cw, sw, cast_weight):
    b, s, d = u3.shape
    t = SEQ_TILE
    tp = DFT_ROWS
    halo = POOL_HALO
    n_tiles = s // t
    n_steps = n_tiles // 2
    halo_blocks_per_tile = t // halo
    last_halo_block = s // halo - 1

    c0, s0 = _twiddle(np.arange(tp), np.arange(s), s)
    cph, sph = _twiddle(np.arange(0, s // 2, t), np.arange(s), s)
    band = jnp.asarray(_pool_band(s, t, halo)).astype(BF16)
    rev = np.zeros((t, tp), np.float32)
    rev[np.arange(t), t - np.arange(t)] = 1.0

    hi_tile = lambda m: n_tiles - 1 - m
    tile_spec = lambda tile_of: pl.BlockSpec((None, t, POOL_WIDTH), lambda m, bi: (bi, tile_of(m), 0))
    prev_spec = lambda tile_of: pl.BlockSpec(
        (None, halo, POOL_WIDTH),
        lambda m, bi: (bi, jnp.maximum(tile_of(m) * halo_blocks_per_tile - 1, 0), 0))
    next_spec = lambda tile_of: pl.BlockSpec(
        (None, halo, POOL_WIDTH),
        lambda m, bi: (bi, jnp.minimum((tile_of(m) + 1) * halo_blocks_per_tile, last_halo_block), 0))
    band_shape = (1, N_POOL_GROUPS, t, t + 2 * halo)
    const2 = lambda shape: pl.BlockSpec(shape, lambda m, bi: (0, 0))
    const3 = lambda shape: pl.BlockSpec(shape, lambda m, bi: (0, 0, 0))
    lo_tile = lambda m: m

    in_specs = [
        pl.BlockSpec((None, s, FOURIER_WIDTH), lambda m, bi: (bi, 0, 1)),
        tile_spec(lo_tile), prev_spec(lo_tile), next_spec(lo_tile),
        tile_spec(hi_tile), prev_spec(hi_tile), next_spec(hi_tile),
        pl.BlockSpec((tp, s), lambda m, bi: (0, 0), pipeline_mode=pl.Buffered(1)),
        pl.BlockSpec((tp, s), lambda m, bi: (0, 0), pipeline_mode=pl.Buffered(1)),
        pl.BlockSpec((1, 1, s), lambda m, bi: (m, 0, 0)),
        pl.BlockSpec((1, 1, s), lambda m, bi: (m, 0, 0)),
        pl.BlockSpec(band_shape, lambda m, bi: (jnp.where(m == 0, 0, 1), 0, 0, 0)),
        pl.BlockSpec(band_shape, lambda m, bi: (jnp.where(m == 0, 2, 1), 0, 0, 0)),
        const2((t, tp)),
        const3((N_POOL_GROUPS, POOL_GROUP_DIM, POOL_GROUP_DIM)),
        const2((1, POOL_WIDTH)),
        const3((N_FOURIER_HEADS, FOURIER_HEAD_DIM, FOURIER_HEAD_DIM)),
        const3((N_FOURIER_HEADS, FOURIER_HEAD_DIM, FOURIER_HEAD_DIM)),
    ]
    half_out = pl.BlockSpec((None, t, d), lambda m, bi: (bi, m, 0))
    hi_out = pl.BlockSpec((None, t, d), lambda m, bi: (bi, n_steps - 1 - m, 0))
    cast = _CastAlong(cast_weight, n_steps * b, lambda m, bi: m * b + bi)
    vmem = (2 * s * FOURIER_WIDTH * 2
            + 2 * tp * s * 4
            + 2 * tp * s * 2
            + 2 * 2 * t * d * 2 * 2
            + 4 * 2 * tp * FOURIER_WIDTH * 4
            + cast.vmem_bytes)
    lo, hi, cast_out = pl.pallas_call(
        _mix_kernel,
        grid=(n_steps, b),
        in_specs=in_specs + [cast.spec],
        out_specs=[half_out, hi_out, cast.spec],
        out_shape=[jax.ShapeDtypeStruct((b, s // 2, d), BF16)] * 2 + [cast.out_shape],
        scratch_shapes=[pltpu.VMEM((2 * tp, s), BF16)],
        compiler_params=pltpu.CompilerParams(
            dimension_semantics=("arbitrary", "arbitrary"), vmem_limit_bytes=vmem + 4 * MIB),
        name="seq_mix",
    )(u3, u3, u3, u3, u3, u3, u3, jnp.asarray(c0), jnp.asarray(s0),
      jnp.asarray(cph).reshape(n_steps, 1, s), jnp.asarray(sph).reshape(n_steps, 1, s),
      band, band, jnp.asarray(rev).astype(BF16),
      w_pool_bf16, pool_scale.reshape(1, POOL_WIDTH), cw, sw, cast.src)
    return lo, hi, cast_out.reshape(cast.shape)


def _out_proj_route_kernel(x_ref, mixed_lo_ref, mixed_hi_ref, wout_ref, g_ref, wr_ref, br_ref,
                           tri_ref, pick_ref, cast_src_ref, rows_ref, route_ref, counts_ref, cast_dst_ref,
                           carry_ref, x1_ref, *, tiles_per_seq, n_tiles):
    step = pl.program_id(0)

    @pl.when(step == 0)
    def _():
        carry_ref[...] = jnp.zeros_like(carry_ref)
        x1_ref[...] = jnp.zeros_like(x1_ref)

    _cast_chunks([cast_src_ref], [cast_dst_ref])

    x1 = x1_ref[...]
    rows_ref[:, :D_MODEL] = x1
    h2 = _rmsnorm(x1, g_ref[...])
    logits = _dot(h2.astype(BF16), wr_ref[...]) + br_ref[...]

    tile = jnp.minimum(step, n_tiles - 1)
    in_lo_half = (tile % tiles_per_seq) < tiles_per_seq // 2
    mixed = jnp.where(in_lo_half, mixed_lo_ref[...], mixed_hi_ref[...])
    x1_ref[...] = x_ref[...] + _dot(mixed, wout_ref[...])

    lane = lax.broadcasted_iota(jnp.int32, logits.shape, 1)
    neg = jnp.float32(-jnp.inf)
    big = jnp.int32(ROUTER_LANES)

    is_group = lane < N_EXPERT_GROUPS
    gl = jnp.where(is_group, logits, neg)
    gmax = jnp.max(gl, axis=-1, keepdims=True)
    gidx = jnp.min(jnp.where(gl == gmax, lane, big), axis=-1, keepdims=True)
    p_g = 1.0 / jnp.sum(jnp.exp(gl - gmax), axis=-1, keepdims=True)

    e_lane = lane - N_EXPERT_GROUPS
    in_group = (e_lane >= gidx * EXPERTS_PER_GROUP) & (e_lane < (gidx + 1) * EXPERTS_PER_GROUP)
    el = jnp.where(in_group, logits, neg)
    v1 = jnp.max(el, axis=-1, keepdims=True)
    i1 = jnp.min(jnp.where(el == v1, lane, big), axis=-1, keepdims=True)
    el2 = jnp.where(lane == i1, neg, el)
    v2 = jnp.max(el2, axis=-1, keepdims=True)
    i2 = jnp.min(jnp.where(el2 == v2, lane, big), axis=-1, keepdims=True)
    r = jnp.exp(v2 - v1)
    w1 = p_g / (1.0 + r)
    w2 = p_g * r / (1.0 + r)

    first_is_low = i1 < i2
    first_lane = N_EXPERT_GROUPS + gidx * EXPERTS_PER_GROUP
    la = jnp.where(first_is_low, i1, i2) - first_lane
    lb = jnp.where(first_is_low, i2, i1) - first_lane
    w_low = jnp.where(first_is_low, w1, w2)
    w_high = jnp.where(first_is_low, w2, w1)
    pair = jnp.where(la == 0, lb - 1, jnp.where(la == 1, 6 - lb, 5))
    slot_a_is_high = la == 2
    w_a = jnp.where(slot_a_is_high, w_high, w_low)
    w_b = jnp.where(slot_a_is_high, w_low, w_high)
    cls = gidx * PAIRS_PER_GROUP + pair
    rows_ref[:, D_MODEL:] = jnp.where(lane == 0, w_a, jnp.where(lane == 1, w_b, 0.0))

    onehot = jnp.where(lane == cls, 1.0, 0.0)
    before = _dot(tri_ref[...], onehot.astype(BF16)) + carry_ref[...]
    rank = jnp.sum(jnp.where(lane == cls, before, 0.0), axis=-1, keepdims=True)
    carry_ref[...] += jnp.sum(onehot, axis=0, keepdims=True) * jnp.where(step > 0, 1.0, 0.0)
    counts_ref[...] = carry_ref[...]

    rank_hi = jnp.floor(rank * (1.0 / RANK_RADIX))
    rank_lo = rank - rank_hi * RANK_RADIX
    digits = jnp.where(lane == 0, cls.astype(F32),
                       jnp.where(lane == 1, rank_hi, jnp.where(lane == 2, rank_lo, 0.0)))
    route_ref[...] = lax.dot_general(pick_ref[...], digits.astype(BF16), (((1,), (1,)), ((), ())),
                                     preferred_element_type=F32)


def _out_proj_route(x2, mixed_lo, mixed_hi, seq_len, w_out_bf16, g_ffn, wr, br, cast_weight):
    n, d = x2.shape
    tm = TOKEN_TILE
    n_tiles = n // tm
    tiles_per_seq = seq_len // tm
    half_tiles = tiles_per_seq // 2
    const = lambda shape: pl.BlockSpec(shape, lambda i: (0, 0))
    in_tile = lambda i: jnp.minimum(i, n_tiles - 1)
    out_tile = lambda i: jnp.maximum(i - 1, 0)
    lo_spec = pl.BlockSpec((tm, d), lambda i: (
        (in_tile(i) // tiles_per_seq) * half_tiles + jnp.minimum(in_tile(i) % tiles_per_seq, half_tiles - 1), 0))
    hi_spec = pl.BlockSpec((tm, d), lambda i: (
        (in_tile(i) // tiles_per_seq) * half_tiles + jnp.maximum(in_tile(i) % tiles_per_seq - half_tiles, 0), 0))
    tri = np.tril(np.ones((tm, tm), np.float32), -1)
    pick = np.eye(SUBLANES, ROUTER_LANES, dtype=np.float32)
    vmem = (2 * tm * d * 4 + 4 * tm * d * 2 + 2 * tm * ROW_WORDS * 4 + d * d * 2 + tm * d * 4
            + 2 * d * ROUTER_LANES * 2 + 2 * tm * tm * 2 + 3 * tm * d * 4)
    cast = _CastAlong(cast_weight, n_tiles, in_tile)
    vmem += cast.vmem_bytes
    rows, route, counts, cast_out = pl.pallas_call(
        functools.partial(_out_proj_route_kernel, tiles_per_seq=tiles_per_seq, n_tiles=n_tiles),
        grid=(n_tiles + 1,),
        in_specs=[pl.BlockSpec((tm, d), lambda i: (in_tile(i), 0)), lo_spec, hi_spec,
                  pl.BlockSpec((d, d), lambda i: (0, 0), pipeline_mode=pl.Buffered(1)),
                  const((1, d)), const((d, ROUTER_LANES)), const((1, ROUTER_LANES)),
                  const((tm, tm)), const((SUBLANES, ROUTER_LANES)), cast.spec],
        out_specs=[pl.BlockSpec((tm, ROW_WORDS), lambda i: (out_tile(i), 0)),
                   pl.BlockSpec((SUBLANES, tm), lambda i: (0, out_tile(i))),
                   const((1, ROUTER_LANES)), cast.spec],
        out_shape=[jax.ShapeDtypeStruct((n, ROW_WORDS), F32),
                   jax.ShapeDtypeStruct((SUBLANES, n), F32),
                   jax.ShapeDtypeStruct((1, ROUTER_LANES), F32), cast.out_shape],
        scratch_shapes=[pltpu.VMEM((1, ROUTER_LANES), F32), pltpu.VMEM((tm, d), F32)],
        compiler_params=pltpu.CompilerParams(
            dimension_semantics=("arbitrary",), vmem_limit_bytes=vmem + 4 * MIB),
        name="out_proj_route",
    )(x2, mixed_lo, mixed_hi, w_out_bf16, g_ffn.reshape(1, d), wr, br,
      jnp.asarray(tri).astype(BF16), jnp.asarray(pick).astype(BF16), cast.src)
    return rows, route, counts, cast_out.reshape(cast.shape)


def _routing_plan(route, counts, n):
    tm = EXPERT_TILE
    cnt = counts[0, :N_CLASSES].astype(jnp.int32)
    class_end = jnp.cumsum(cnt)
    class_start = class_end - cnt
    cls = route[0].astype(jnp.int32)
    rank = (route[1] * RANK_RADIX + route[2]).astype(jnp.int32)
    pos = class_start[cls] + rank

    first_tile = class_start // tm
    items_per_class = jnp.where(cnt > 0, (class_end - 1) // tm - first_tile + 1, 0)
    item_end = jnp.cumsum(items_per_class)
    item_start = item_end - items_per_class
    n_items = n // tm + N_CLASSES
    item = jnp.arange(n_items, dtype=jnp.int32)
    item_c = jnp.minimum(item, item_end[-1] - 1)
    item_cls = jnp.sum((item_c[:, None] >= item_end[None, :]).astype(jnp.int32), axis=1)
    item_tile = first_tile[item_cls] + item_c - item_start[item_cls]
    lo = jnp.clip(class_start[item_cls] - item_tile * tm, 0, tm)
    hi = jnp.clip(class_end[item_cls] - item_tile * tm, 0, tm)
    group_base = np.repeat(np.arange(N_EXPERT_GROUPS) * EXPERTS_PER_GROUP, PAIRS_PER_GROUP)
    class_ea = jnp.asarray(group_base + np.tile(PAIR_SLOT_A, N_EXPERT_GROUPS), jnp.int32)
    class_eb = jnp.asarray(group_base + np.tile(PAIR_SLOT_B, N_EXPERT_GROUPS), jnp.int32)
    return pos, item_tile, class_ea[item_cls], class_eb[item_cls], lo, hi


def _row_copy(src_ref, src_row, dst_ref, dst_row, sem):
    return pltpu.make_async_copy(src_ref.at[pl.ds(src_row, 1), :], dst_ref.at[pl.ds(dst_row, 1), :], sem)


def _invert_permutation_kernel(pos_ref, inv_ref):
    def body(t, carry):
        inv_ref[pos_ref[t]] = t
        return carry

    lax.fori_loop(0, pos_ref.shape[0], body, 0, unroll=8)


def _invert_permutation(pos):
    return pl.pallas_call(
        _invert_permutation_kernel,
        in_specs=[pl.BlockSpec(memory_space=pltpu.SMEM)],
        out_specs=pl.BlockSpec(memory_space=pltpu.SMEM),
        out_shape=jax.ShapeDtypeStruct(pos.shape, jnp.int32),
        name="invert_permutation",
    )(pos)


def _expert_pair_kernel(tile_ref, ea_ref, eb_ref, lo_ref, hi_ref, tok_ref, rows_hbm_ref, gffn_ref, gfin_ref,
                        wga_ref, wua_ref, wda_ref, wgb_ref, wub_ref, wdb_ref, out_hbm_ref,
                        buf_ref, acc_ref, gather_sem, scatter_sem):
    del ea_ref, eb_ref
    tm = EXPERT_TILE
    j = pl.program_id(0)
    last = pl.num_programs(0) - 1
    slot = j % 2
    other = 1 - slot

    def start_gather(item, dst_slot, rows=range(tm)):
        base = tile_ref[item] * tm
        for r in rows:
            _row_copy(rows_hbm_ref, tok_ref[base + r], buf_ref.at[dst_slot], r, gather_sem.at[dst_slot]).start()

    def wait_gather(dst_slot):
        pltpu.make_async_copy(rows_hbm_ref.at[pl.ds(0, tm), :], buf_ref.at[dst_slot],
                              gather_sem.at[dst_slot]).wait()

    def start_scatter(item, src_slot, rows=range(tm)):
        base = tile_ref[item] * tm
        for r in rows:
            _row_copy(acc_ref.at[src_slot], r, out_hbm_ref, tok_ref[base + r], scatter_sem).start()

    def wait_scatter():
        pltpu.make_async_copy(acc_ref.at[0], out_hbm_ref.at[pl.ds(0, tm), :], scatter_sem).wait()

    @pl.when(j == 0)
    def _():
        acc_ref[...] = jnp.zeros_like(acc_ref)
        start_gather(0, 0)
        start_scatter(0, 1)

    wait_scatter()
    wait_gather(slot)
    prev_item = jnp.maximum(j - 1, 0)
    next_item = jnp.minimum(j + 1, last)

    def issue_copies(part):
        rows = range(part * tm // DMA_PARTS, (part + 1) * tm // DMA_PARTS)
        start_scatter(prev_item, other, rows)
        start_gather(next_item, other, rows)

    lo = lo_ref[j]
    hi = hi_ref[j]
    x1 = buf_ref[slot, :, :D_MODEL]
    h = _rmsnorm(x1, gffn_ref[...]).astype(BF16)

    part = 0
    ff_chunk = EXPERT_FF * 4 // DMA_PARTS
    for slot_lane, (wg_ref, wu_ref, wd_ref) in enumerate(((wga_ref, wua_ref, wda_ref),
                                                          (wgb_ref, wub_ref, wdb_ref))):
        for c0 in range(0, EXPERT_FF, ff_chunk):
            issue_copies(part)
            wts = buf_ref[slot, :, D_MODEL:]
            lane = lax.broadcasted_iota(jnp.int32, wts.shape, 1)
            w = jnp.sum(jnp.where(lane == slot_lane, wts, 0.0), axis=-1, keepdims=True)
            a = _dot(h, wg_ref[0, :, c0:c0 + ff_chunk])
            v = _dot(h, wu_ref[0, :, c0:c0 + ff_chunk])
            act = (a * (1.0 / (1.0 + jnp.exp(-a))) * v * w).astype(BF16)
            for half, d0 in enumerate(range(0, D_MODEL, D_MODEL // 2)):
                cols = slice(d0, d0 + D_MODEL // 2)
                if half == 1:
                    issue_copies(part + 1)
                y_part = _dot(act, wd_ref[0, c0:c0 + ff_chunk, cols])
                if part == 0:
                    acc_ref[slot, :, cols] = y_part
                else:
                    acc_ref[slot, :, cols] += y_part
            part += 2
    res = _rmsnorm(buf_ref[slot, :, :D_MODEL] + acc_ref[slot], gfin_ref[...])

    row = lax.broadcasted_iota(jnp.int32, (tm, 1), 0)
    mine = (row >= lo) & (row < hi)
    acc_ref[slot] = jnp.where(mine, res, jnp.where(lo > 0, acc_ref[other], 0.0))

    @pl.when(j == last)
    def _():
        wait_scatter()
        start_scatter(j, slot)
        wait_scatter()
        wait_gather(other)


def _expert_pairs(item_tile, item_ea, item_eb, item_lo, item_hi, slot_token, rows, g_ffn, g_final, wg, wu, wd):
    n, w = rows.shape
    d, f = D_MODEL, EXPERT_FF
    tm = EXPERT_TILE
    gate_a = pl.BlockSpec((1, d, f), lambda j, t, ea, eb, lo, hi, tok: (ea[j], 0, 0))
    gate_b = pl.BlockSpec((1, d, f), lambda j, t, ea, eb, lo, hi, tok: (eb[j], 0, 0))
    down_a = pl.BlockSpec((1, f, d), lambda j, t, ea, eb, lo, hi, tok: (ea[j], 0, 0))
    down_b = pl.BlockSpec((1, f, d), lambda j, t, ea, eb, lo, hi, tok: (eb[j], 0, 0))
    gain = pl.BlockSpec((1, d), lambda j, t, ea, eb, lo, hi, tok: (0, 0))
    grid_spec = pltpu.PrefetchScalarGridSpec(
        num_scalar_prefetch=6,
        grid=(item_tile.shape[0],),
        in_specs=[pl.BlockSpec(memory_space=pl.ANY), gain, gain,
                  gate_a, gate_a, down_a, gate_b, gate_b, down_b],
        out_specs=pl.BlockSpec(memory_space=pl.ANY),
        scratch_shapes=[pltpu.VMEM((2, tm, w), F32), pltpu.VMEM((2, tm, d), F32),
                        pltpu.SemaphoreType.DMA((2,)), pltpu.SemaphoreType.DMA(())],
    )
    vmem = 2 * 6 * d * f * 2 + 2 * tm * w * 4 + 2 * tm * d * 4 + 8 * tm * d * 4
    return pl.pallas_call(
        _expert_pair_kernel,
        grid_spec=grid_spec,
        out_shape=jax.ShapeDtypeStruct((n, d), F32),
        compiler_params=pltpu.CompilerParams(
            dimension_semantics=("arbitrary",), vmem_limit_bytes=vmem + 4 * MIB),
        name="expert_pairs",
    )(item_tile, item_ea, item_eb, item_lo, item_hi, slot_token, rows, g_ffn.reshape(1, d),
      g_final.reshape(1, d), wg, wu, wd, wg, wu, wd)


def kernel(x, g_mix, w_in, w_pool, pool_scale, w_fourier, w_out, g_ffn, w_group_router,
           b_group_router, w_expert_router, b_expert_router, w_gate, w_up, w_down, g_final):
    b, s, d = x.shape
    assert d == D_MODEL and s % (2 * SEQ_TILE) == 0 and s % (2 * TOKEN_TILE) == 0
    assert (b * s) % max(TOKEN_TILE, EXPERT_TILE) == 0
    n = b * s
    x2 = x.reshape(n, d)

    cw, sw = _fourier_weights(w_fourier, s)
    u, (w_out_bf16, w_gate_bf16) = _norm_proj(x2, g_mix, w_in.astype(BF16), [w_out, w_gate])
    mixed_lo, mixed_hi, w_up_bf16 = _mix(u.reshape(b, s, d), w_pool.astype(BF16), pool_scale, cw, sw, w_up)

    wr = jnp.concatenate([w_group_router, w_expert_router], axis=1)
    wr = jnp.pad(wr, ((0, 0), (0, ROUTER_LANES - wr.shape[1]))).astype(BF16)
    br = jnp.concatenate([b_group_router, b_expert_router])
    br = jnp.pad(br, (0, ROUTER_LANES - br.shape[0])).reshape(1, ROUTER_LANES)

    rows, route, counts, w_down_bf16 = _out_proj_route(
        x2, mixed_lo.reshape(n // 2, d), mixed_hi.reshape(n // 2, d), s, w_out_bf16, g_ffn, wr, br, w_down)
    pos, item_tile, item_ea, item_eb, item_lo, item_hi = _routing_plan(route, counts, n)
    slot_token = _invert_permutation(pos)
    out = _expert_pairs(item_tile, item_ea, item_eb, item_lo, item_hi, slot_token, rows, g_ffn, g_final,
                        w_gate_bf16, w_up_bf16, w_down_bf16)
    return out.reshape(b, s, d)
```

```python
import functools

import numpy as np
import jax
import jax.numpy as jnp
from jax import lax
from jax.experimental import pallas as pl
from jax.experimental.pallas import tpu as pltpu

D_MODEL = 2048
POOL_WINDOWS = (2, 4, 8, 16)
N_POOL_GROUPS = len(POOL_WINDOWS)
POOL_WIDTH = D_MODEL // 2
POOL_GROUP_DIM = POOL_WIDTH // N_POOL_GROUPS
FOURIER_WIDTH = D_MODEL - POOL_WIDTH
N_FOURIER_HEADS = 4
FOURIER_HEAD_DIM = FOURIER_WIDTH // N_FOURIER_HEADS
N_EXPERT_GROUPS = 4
EXPERTS_PER_GROUP = 4
N_EXPERTS = N_EXPERT_GROUPS * EXPERTS_PER_GROUP
EXPERT_FF = D_MODEL // 4
RMS_EPS = 1e-6

LANES = 128
SUBLANES = 8
BF16_SUBLANES = 16
ROUTER_LANES = LANES
MIB = 1024 * 1024

TOKEN_TILE = 512
SEQ_TILE = 256
POOL_HALO = BF16_SUBLANES
DFT_ROWS = SEQ_TILE + BF16_SUBLANES
TWIDDLE_ROWS = 16

PAIR_SLOT_A = (0, 0, 0, 1, 1, 3)
PAIR_SLOT_B = (1, 2, 3, 3, 2, 2)
PAIRS_PER_GROUP = len(PAIR_SLOT_A)
N_CLASSES = N_EXPERT_GROUPS * PAIRS_PER_GROUP
ROW_WORDS = D_MODEL + LANES
RANK_RADIX = 128
EXPERT_TILE = 256
DMA_PARTS = 8

BF16 = jnp.bfloat16
F32 = jnp.float32


def _rmsnorm(x, g):
    ms = jnp.mean(x * x, axis=-1, keepdims=True)
    return x * lax.rsqrt(ms + RMS_EPS) * g


def _dot(a, b):
    return jnp.dot(a, b, preferred_element_type=F32)


def _twiddle(rows, cols, period):
    m = (np.asarray(rows, np.int64)[:, None] * np.asarray(cols, np.int64)[None, :]) % period
    ang = (2.0 * np.pi / period) * m.astype(np.float64)
    return np.cos(ang).astype(np.float32), np.sin(ang).astype(np.float32)


def _pool_band(seq_len, tile, halo):
    n_tiles = seq_len // tile
    out = np.zeros((3, N_POOL_GROUPS, tile, tile + 2 * halo), np.float64)
    for v, m in enumerate((0, 1, n_tiles - 1)):
        t0 = m * tile
        for g, k in enumerate(POOL_WINDOWS):
            for r in range(tile):
                t = t0 + r
                lo = max(t - (k - 1) // 2, 0)
                hi = min(t + k // 2 + 1, seq_len)
                out[v, g, r, lo - t0 + halo:hi - t0 + halo] = 1.0 / (hi - lo)
                out[v, g, r, r + halo] -= 1.0
    return out.astype(np.float32)


def _fourier_weight_kernel(cd_ref, sd_ref, w_ref, cw_ref, sw_ref, *, scale):
    w = w_ref[0]
    cw = jnp.dot(cd_ref[...], w, preferred_element_type=F32, precision=lax.Precision.HIGHEST)
    sw = jnp.dot(sd_ref[...], w, preferred_element_type=F32, precision=lax.Precision.HIGHEST)
    cw_ref[0] = (cw * scale).astype(BF16)
    sw_ref[0] = (sw * (-scale)).astype(BF16)


def _fourier_weights(w_fourier, seq_len):
    dh = FOURIER_HEAD_DIM
    cd, sd = _twiddle(np.arange(dh), np.arange(dh), dh)
    scale = 1.0 / np.sqrt(float(seq_len * dh))
    mat = pl.BlockSpec((dh, dh), lambda h: (0, 0))
    per_head = pl.BlockSpec((1, dh, dh), lambda h: (h, 0, 0))
    return pl.pallas_call(
        functools.partial(_fourier_weight_kernel, scale=scale),
        grid=(N_FOURIER_HEADS,),
        in_specs=[mat, mat, per_head],
        out_specs=[per_head, per_head],
        out_shape=[jax.ShapeDtypeStruct((N_FOURIER_HEADS, dh, dh), BF16)] * 2,
        name="fourier_weights",
    )(jnp.asarray(cd), jnp.asarray(sd), w_fourier)


class _CastAlong:
    def __init__(self, w, n_chunks, chunk_of):
        cols = w.shape[-1]
        rows = w.size // cols // n_chunks
        self.shape = w.shape
        self.src = w.reshape(n_chunks, rows, cols)
        self.spec = pl.BlockSpec((1, rows, cols), lambda *idx: (chunk_of(*idx), 0, 0))
        self.out_shape = jax.ShapeDtypeStruct((n_chunks, rows, cols), BF16)
        self.vmem_bytes = 2 * rows * cols * (4 + 2)


def _cast_chunks(src_refs, dst_refs):
    for src_ref, dst_ref in zip(src_refs, dst_refs):
        dst_ref[...] = src_ref[...].astype(BF16)


def _norm_proj_kernel(x_ref, g_ref, w_ref, *rest, n_cast):
    cast_src, (u_ref, *cast_dst) = rest[:n_cast], rest[n_cast:]
    h = _rmsnorm(x_ref[...], g_ref[...])
    u_ref[...] = _dot(h.astype(BF16), w_ref[...]).astype(BF16)
    _cast_chunks(cast_src, cast_dst)


def _norm_proj(x2, g_mix, w_in_bf16, cast_weights):
    n, d = x2.shape
    tm = TOKEN_TILE
    casts = [_CastAlong(w, n // tm, lambda i: i) for w in cast_weights]
    vmem = (2 * tm * d * 4 + d * d * 2 + 2 * tm * d * 2 + 3 * tm * d * 4
            + sum(c.vmem_bytes for c in casts))
    u, *cast_out = pl.pallas_call(
        functools.partial(_norm_proj_kernel, n_cast=len(casts)),
        grid=(n // tm,),
        in_specs=[
            pl.BlockSpec((tm, d), lambda i: (i, 0)),
            pl.BlockSpec((1, d), lambda i: (0, 0)),
            pl.BlockSpec((d, d), lambda i: (0, 0), pipeline_mode=pl.Buffered(1)),
        ] + [c.spec for c in casts],
        out_specs=[pl.BlockSpec((tm, d), lambda i: (i, 0))] + [c.spec for c in casts],
        out_shape=[jax.ShapeDtypeStruct((n, d), BF16)] + [c.out_shape for c in casts],
        compiler_params=pltpu.CompilerParams(
            dimension_semantics=("arbitrary",), vmem_limit_bytes=vmem + 4 * MIB),
        name="norm_proj",
    )(x2, g_mix.reshape(1, d), w_in_bf16, *[c.src for c in casts])
    return u, [o.reshape(c.shape) for o, c in zip(cast_out, casts)]


def _mix_kernel(uf_ref, up_lo_ref, prev_lo_ref, next_lo_ref, up_hi_ref, prev_hi_ref, next_hi_ref,
                c0_ref, s0_ref, cph_ref, sph_ref, band_lo_ref, band_hi_ref, rev_ref,
                wpool_ref, pscale_ref, cw_ref, sw_ref, cast_src_ref, lo_ref, hi_ref, cast_dst_ref, lhs_ref):
    t = SEQ_TILE
    tp = DFT_ROWS
    gd = POOL_GROUP_DIM
    hd = FOURIER_HEAD_DIM

    @pl.when(pl.program_id(1) == 0)
    def _():
        cph = cph_ref[0]
        sph = sph_ref[0]
        for r0 in range(0, tp, TWIDDLE_ROWS):
            c0 = c0_ref[r0:r0 + TWIDDLE_ROWS, :]
            s0 = s0_ref[r0:r0 + TWIDDLE_ROWS, :]
            lhs_ref[r0:r0 + TWIDDLE_ROWS, :] = (cph * c0 - sph * s0).astype(BF16)
            lhs_ref[tp + r0:tp + r0 + TWIDDLE_ROWS, :] = (sph * c0 + cph * s0).astype(BF16)

    _cast_chunks([cast_src_ref], [cast_dst_ref])
    pq = _dot(lhs_ref[...], uf_ref[...])
    for h in range(N_FOURIER_HEADS):
        cols = slice(h * hd, (h + 1) * hd)
        out_cols = slice(POOL_WIDTH + h * hd, POOL_WIDTH + (h + 1) * hd)
        pc = _dot(pq[:tp, cols].astype(BF16), cw_ref[h])
        qs = _dot(pq[tp:, cols].astype(BF16), sw_ref[h])
        lo_ref[:, out_cols] = (pc + qs)[:t].astype(BF16)
        hi_ref[:, out_cols] = _dot(rev_ref[...], (pc - qs).astype(BF16)).astype(BF16)

    def pool(up_ref, prev_ref, next_ref, band_ref, out_ref):
        win = jnp.concatenate([prev_ref[...], up_ref[...], next_ref[...]], axis=0)
        for g in range(N_POOL_GROUPS):
            cols = slice(g * gd, (g + 1) * gd)
            pooled = _dot(band_ref[0, g], win[:, cols]).astype(BF16)
            y = _dot(pooled, wpool_ref[g]) * pscale_ref[:, cols]
            out_ref[:, cols] = y.astype(BF16)

    pool(up_lo_ref, prev_lo_ref, next_lo_ref, band_lo_ref, lo_ref)
    pool(up_hi_ref, prev_hi_ref, next_hi_ref, band_hi_ref, hi_ref)


def _mix(u3, w_pool_bf16, pool_scale, cw, sw, cast_weight):
    b, s, d = u3.shape
    t = SEQ_TILE
    tp = DFT_ROWS
    halo = POOL_HALO
    n_tiles = s // t
    n_steps = n_tiles // 2
    halo_blocks_per_tile = t // halo
    last_halo_block = s // halo - 1

    c0, s0 = _twiddle(np.arange(tp), np.arange(s), s)
    cph, sph = _twiddle(np.arange(0, s // 2, t), np.arange(s), s)
    band = jnp.asarray(_pool_band(s, t, halo)).astype(BF16)
    rev = np.zeros((t, tp), np.float32)
    rev[np.arange(t), t - np.arange(t)] = 1.0

    hi_tile = lambda m: n_tiles - 1 - m
    tile_spec = lambda tile_of: pl.BlockSpec((None, t, POOL_WIDTH), lambda m, bi: (bi, tile_of(m), 0))
    prev_spec = lambda tile_of: pl.BlockSpec(
        (None, halo, POOL_WIDTH),
        lambda m, bi: (bi, jnp.maximum(tile_of(m) * halo_blocks_per_tile - 1, 0), 0))
    next_spec = lambda tile_of: pl.BlockSpec(
        (None, halo, POOL_WIDTH),
        lambda m, bi: (bi, jnp.minimum((tile_of(m) + 1) * halo_blocks_per_tile, last_halo_block), 0))
    band_shape = (1, N_POOL_GROUPS, t, t + 2 * halo)
    const2 = lambda shape: pl.BlockSpec(shape, lambda m, bi: (0, 0))
    const3 = lambda shape: pl.BlockSpec(shape, lambda m, bi: (0, 0, 0))
    lo_tile = lambda m: m

    in_specs = [
        pl.BlockSpec((None, s, FOURIER_WIDTH), lambda m, bi: (bi, 0, 1)),
        tile_spec(lo_tile), prev_spec(lo_tile), next_spec(lo_tile),
        tile_spec(hi_tile), prev_spec(hi_tile), next_spec(hi_tile),
        pl.BlockSpec((tp, s), lambda m, bi: (0, 0), pipeline_mode=pl.Buffered(1)),
        pl.BlockSpec((tp, s), lambda m, bi: (0, 0), pipeline_mode=pl.Buffered(1)),
        pl.BlockSpec((1, 1, s), lambda m, bi: (m, 0, 0)),
        pl.BlockSpec((1, 1, s), lambda m, bi: (m, 0, 0)),
        pl.BlockSpec(band_shape, lambda m, bi: (jnp.where(m == 0, 0, 1), 0, 0, 0)),
        pl.BlockSpec(band_shape, lambda m, bi: (jnp.where(m == 0, 2, 1), 0, 0, 0)),
        const2((t, tp)),
        const3((N_POOL_GROUPS, POOL_GROUP_DIM, POOL_GROUP_DIM)),
        const2((1, POOL_WIDTH)),
        const3((N_FOURIER_HEADS, FOURIER_HEAD_DIM, FOURIER_HEAD_DIM)),
        const3((N_FOURIER_HEADS, FOURIER_HEAD_DIM, FOURIER_HEAD_DIM)),
    ]
    half_out = pl.BlockSpec((None, t, d), lambda m, bi: (bi, m, 0))
    hi_out = pl.BlockSpec((None, t, d), lambda m, bi: (bi, n_steps - 1 - m, 0))
    cast = _CastAlong(cast_weight, n_steps * b, lambda m, bi: m * b + bi)
    vmem = (2 * s * FOURIER_WIDTH * 2
            + 2 * tp * s * 4
            + 2 * tp * s * 2
            + 2 * 2 * t * d * 2 * 2
            + 4 * 2 * tp * FOURIER_WIDTH * 4
            + cast.vmem_bytes)
    lo, hi, cast_out = pl.pallas_call(
        _mix_kernel,
        grid=(n_steps, b),
        in_specs=in_specs + [cast.spec],
        out_specs=[half_out, hi_out, cast.spec],
        out_shape=[jax.ShapeDtypeStruct((b, s // 2, d), BF16)] * 2 + [cast.out_shape],
        scratch_shapes=[pltpu.VMEM((2 * tp, s), BF16)],
        compiler_params=pltpu.CompilerParams(
            dimension_semantics=("arbitrary", "arbitrary"), vmem_limit_bytes=vmem + 4 * MIB),
        name="seq_mix",
    )(u3, u3, u3, u3, u3, u3, u3, jnp.asarray(c0), jnp.asarray(s0),
      jnp.asarray(cph).reshape(n_steps, 1, s), jnp.asarray(sph).reshape(n_steps, 1, s),
      band, band, jnp.asarray(rev).astype(BF16),
      w_pool_bf16, pool_scale.reshape(1, POOL_WIDTH), cw, sw, cast.src)
    return lo, hi, cast_out.reshape(cast.shape)


def _out_proj_route_kernel(x_ref, mixed_lo_ref, mixed_hi_ref, wout_ref, g_ref, wr_ref, br_ref,
                           tri_ref, pick_ref, cast_src_ref, rows_ref, route_ref, counts_ref, cast_dst_ref,
                           carry_ref, x1_ref, *, tiles_per_seq, n_tiles):
    step = pl.program_id(0)

    @pl.when(step == 0)
    def _():
        carry_ref[...] = jnp.zeros_like(carry_ref)
        x1_ref[...] = jnp.zeros_like(x1_ref)

    _cast_chunks([cast_src_ref], [cast_dst_ref])

    x1 = x1_ref[...]
    rows_ref[:, :D_MODEL] = x1

    tile = jnp.minimum(step, n_tiles - 1)
    in_lo_half = (tile % tiles_per_seq) < tiles_per_seq // 2
    mixed = jnp.where(in_lo_half, mixed_lo_ref[...], mixed_hi_ref[...])
    half = mixed.shape[0] // 2
    x1_ref[:half, :] = x_ref[:half, :] + _dot(mixed[:half], wout_ref[...])

    h2 = _rmsnorm(x1, g_ref[...])
    logits = _dot(h2.astype(BF16), wr_ref[...]) + br_ref[...]

    x1_ref[half:, :] = x_ref[half:, :] + _dot(mixed[half:], wout_ref[...])

    lane = lax.broadcasted_iota(jnp.int32, logits.shape, 1)
    neg = jnp.float32(-jnp.inf)
    big = jnp.int32(ROUTER_LANES)

    is_group = lane < N_EXPERT_GROUPS
    gl = jnp.where(is_group, logits, neg)
    gmax = jnp.max(gl, axis=-1, keepdims=True)
    gidx = jnp.min(jnp.where(gl == gmax, lane, big), axis=-1, keepdims=True)
    p_g = 1.0 / jnp.sum(jnp.exp(gl - gmax), axis=-1, keepdims=True)

    e_lane = lane - N_EXPERT_GROUPS
    in_group = (e_lane >= gidx * EXPERTS_PER_GROUP) & (e_lane < (gidx + 1) * EXPERTS_PER_GROUP)
    el = jnp.where(in_group, logits, neg)
    v1 = jnp.max(el, axis=-1, keepdims=True)
    i1 = jnp.min(jnp.where(el == v1, lane, big), axis=-1, keepdims=True)
    el2 = jnp.where(lane == i1, neg, el)
    v2 = jnp.max(el2, axis=-1, keepdims=True)
    i2 = jnp.min(jnp.where(el2 == v2, lane, big), axis=-1, keepdims=True)
    r = jnp.exp(v2 - v1)
    w1 = p_g / (1.0 + r)
    w2 = p_g * r / (1.0 + r)

    first_is_low = i1 < i2
    first_lane = N_EXPERT_GROUPS + gidx * EXPERTS_PER_GROUP
    la = jnp.where(first_is_low, i1, i2) - first_lane
    lb = jnp.where(first_is_low, i2, i1) - first_lane
    w_low = jnp.where(first_is_low, w1, w2)
    w_high = jnp.where(first_is_low, w2, w1)
    pair = jnp.where(la == 0, lb - 1, jnp.where(la == 1, 6 - lb, 5))
    slot_a_is_high = la == 2
    w_a = jnp.where(slot_a_is_high, w_high, w_low)
    w_b = jnp.where(slot_a_is_high, w_low, w_high)
    cls = gidx * PAIRS_PER_GROUP + pair
    rows_ref[:, D_MODEL:] = jnp.where(lane == 0, w_a, jnp.where(lane == 1, w_b, 0.0))

    onehot = jnp.where(lane == cls, 1.0, 0.0)
    before = _dot(tri_ref[...], onehot.astype(BF16)) + carry_ref[...]
    rank = jnp.sum(jnp.where(lane == cls, before, 0.0), axis=-1, keepdims=True)
    carry_ref[...] += jnp.sum(onehot, axis=0, keepdims=True) * jnp.where(step > 0, 1.0, 0.0)
    counts_ref[...] = carry_ref[...]

    rank_hi = jnp.floor(rank * (1.0 / RANK_RADIX))
    rank_lo = rank - rank_hi * RANK_RADIX
    digits = jnp.where(lane == 0, cls.astype(F32),
                       jnp.where(lane == 1, rank_hi, jnp.where(lane == 2, rank_lo, 0.0)))
    route_ref[...] = lax.dot_general(pick_ref[...], digits.astype(BF16), (((1,), (1,)), ((), ())),
                                     preferred_element_type=F32)


def _out_proj_route(x2, mixed_lo, mixed_hi, seq_len, w_out_bf16, g_ffn, wr, br, cast_weight):
    n, d = x2.shape
    tm = TOKEN_TILE
    n_tiles = n // tm
    tiles_per_seq = seq_len // tm
    half_tiles = tiles_per_seq // 2
    const = lambda shape: pl.BlockSpec(shape, lambda i: (0, 0))
    in_tile = lambda i: jnp.minimum(i, n_tiles - 1)
    out_tile = lambda i: jnp.maximum(i - 1, 0)
    lo_spec = pl.BlockSpec((tm, d), lambda i: (
        (in_tile(i) // tiles_per_seq) * half_tiles + jnp.minimum(in_tile(i) % tiles_per_seq, half_tiles - 1), 0))
    hi_spec = pl.BlockSpec((tm, d), lambda i: (
        (in_tile(i) // tiles_per_seq) * half_tiles + jnp.maximum(in_tile(i) % tiles_per_seq - half_tiles, 0), 0))
    tri = np.tril(np.ones((tm, tm), np.float32), -1)
    pick = np.eye(SUBLANES, ROUTER_LANES, dtype=np.float32)
    vmem = (2 * tm * d * 4 + 4 * tm * d * 2 + 2 * tm * ROW_WORDS * 4 + d * d * 2 + tm * d * 4
            + 2 * d * ROUTER_LANES * 2 + 2 * tm * tm * 2 + 3 * tm * d * 4)
    cast = _CastAlong(cast_weight, n_tiles, in_tile)
    vmem += cast.vmem_bytes
    rows, route, counts, cast_out = pl.pallas_call(
        functools.partial(_out_proj_route_kernel, tiles_per_seq=tiles_per_seq, n_tiles=n_tiles),
        grid=(n_tiles + 1,),
        in_specs=[pl.BlockSpec((tm, d), lambda i: (in_tile(i), 0)), lo_spec, hi_spec,
                  pl.BlockSpec((d, d), lambda i: (0, 0), pipeline_mode=pl.Buffered(1)),
                  const((1, d)), const((d, ROUTER_LANES)), const((1, ROUTER_LANES)),
                  const((tm, tm)), const((SUBLANES, ROUTER_LANES)), cast.spec],
        out_specs=[pl.BlockSpec((tm, ROW_WORDS), lambda i: (out_tile(i), 0)),
                   pl.BlockSpec((SUBLANES, tm), lambda i: (0, out_tile(i))),
                   const((1, ROUTER_LANES)), cast.spec],
        out_shape=[jax.ShapeDtypeStruct((n, ROW_WORDS), F32),
                   jax.ShapeDtypeStruct((SUBLANES, n), F32),
                   jax.ShapeDtypeStruct((1, ROUTER_LANES), F32), cast.out_shape],
        scratch_shapes=[pltpu.VMEM((1, ROUTER_LANES), F32), pltpu.VMEM((tm, d), F32)],
        compiler_params=pltpu.CompilerParams(
            dimension_semantics=("arbitrary",), vmem_limit_bytes=vmem + 4 * MIB),
        name="out_proj_route",
    )(x2, mixed_lo, mixed_hi, w_out_bf16, g_ffn.reshape(1, d), wr, br,
      jnp.asarray(tri).astype(BF16), jnp.asarray(pick).astype(BF16), cast.src)
    return rows, route, counts, cast_out.reshape(cast.shape)


def _routing_plan(route, counts, n):
    tm = EXPERT_TILE
    cnt = counts[0, :N_CLASSES].astype(jnp.int32)
    class_end = jnp.cumsum(cnt)
    class_start = class_end - cnt
    cls = route[0].astype(jnp.int32)
    rank = (route[1] * RANK_RADIX + route[2]).astype(jnp.int32)
    pos = class_start[cls] + rank

    first_tile = class_start // tm
    items_per_class = jnp.where(cnt > 0, (class_end - 1) // tm - first_tile + 1, 0)
    item_end = jnp.cumsum(items_per_class)
    item_start = item_end - items_per_class
    n_items = n // tm + N_CLASSES
    item = jnp.arange(n_items, dtype=jnp.int32)
    item_c = jnp.minimum(item, item_end[-1] - 1)
    item_cls = jnp.sum((item_c[:, None] >= item_end[None, :]).astype(jnp.int32), axis=1)
    item_tile = first_tile[item_cls] + item_c - item_start[item_cls]
    lo = jnp.clip(class_start[item_cls] - item_tile * tm, 0, tm)
    hi = jnp.clip(class_end[item_cls] - item_tile * tm, 0, tm)
    group_base = np.repeat(np.arange(N_EXPERT_GROUPS) * EXPERTS_PER_GROUP, PAIRS_PER_GROUP)
    class_ea = jnp.asarray(group_base + np.tile(PAIR_SLOT_A, N_EXPERT_GROUPS), jnp.int32)
    class_eb = jnp.asarray(group_base + np.tile(PAIR_SLOT_B, N_EXPERT_GROUPS), jnp.int32)
    return pos, item_tile, class_ea[item_cls], class_eb[item_cls], lo, hi


def _row_copy(src_ref, src_row, dst_ref, dst_row, sem):
    return pltpu.make_async_copy(src_ref.at[pl.ds(src_row, 1), :], dst_ref.at[pl.ds(dst_row, 1), :], sem)


def _invert_permutation_kernel(pos_ref, inv_ref):
    def body(t, carry):
        inv_ref[pos_ref[t]] = t
        return carry

    lax.fori_loop(0, pos_ref.shape[0], body, 0, unroll=8)


def _invert_permutation(pos):
    return pl.pallas_call(
        _invert_permutation_kernel,
        in_specs=[pl.BlockSpec(memory_space=pltpu.SMEM)],
        out_specs=pl.BlockSpec(memory_space=pltpu.SMEM),
        out_shape=jax.ShapeDtypeStruct(pos.shape, jnp.int32),
        name="invert_permutation",
    )(pos)


def _expert_pair_kernel(tile_ref, ea_ref, eb_ref, lo_ref, hi_ref, tok_ref, rows_hbm_ref, gffn_ref, gfin_ref,
                        wga_ref, wua_ref, wda_ref, wgb_ref, wub_ref, wdb_ref, out_hbm_ref,
                        buf_ref, acc_ref, gather_sem, scatter_sem):
    del ea_ref, eb_ref
    tm = EXPERT_TILE
    j = pl.program_id(0)
    last = pl.num_programs(0) - 1
    slot = j % 2
    other = 1 - slot

    def start_gather(item, dst_slot, rows=range(tm)):
        base = tile_ref[item] * tm
        for r in rows:
            _row_copy(rows_hbm_ref, tok_ref[base + r], buf_ref.at[dst_slot], r, gather_sem.at[dst_slot]).start()

    def wait_gather(dst_slot):
        pltpu.make_async_copy(rows_hbm_ref.at[pl.ds(0, tm), :], buf_ref.at[dst_slot],
                              gather_sem.at[dst_slot]).wait()

    def start_scatter(item, src_slot, rows=range(tm)):
        base = tile_ref[item] * tm
        for r in rows:
            _row_copy(acc_ref.at[src_slot], r, out_hbm_ref, tok_ref[base + r], scatter_sem).start()

    def wait_scatter():
        pltpu.make_async_copy(acc_ref.at[0], out_hbm_ref.at[pl.ds(0, tm), :], scatter_sem).wait()

    @pl.when(j == 0)
    def _():
        acc_ref[...] = jnp.zeros_like(acc_ref)
        start_gather(0, 0)
        start_scatter(0, 1)

    wait_scatter()
    wait_gather(slot)
    prev_item = jnp.maximum(j - 1, 0)
    next_item = jnp.minimum(j + 1, last)

    def issue_copies(part):
        rows = range(part * tm // DMA_PARTS, (part + 1) * tm // DMA_PARTS)
        start_scatter(prev_item, other, rows)
        start_gather(next_item, other, rows)

    lo = lo_ref[j]
    hi = hi_ref[j]
    x1 = buf_ref[slot, :, :D_MODEL]
    h = _rmsnorm(x1, gffn_ref[...]).astype(BF16)

    part = 0
    ff_chunk = EXPERT_FF * 4 // DMA_PARTS
    for slot_lane, (wg_ref, wu_ref, wd_ref) in enumerate(((wga_ref, wua_ref, wda_ref),
                                                          (wgb_ref, wub_ref, wdb_ref))):
        for c0 in range(0, EXPERT_FF, ff_chunk):
            issue_copies(part)
            wts = buf_ref[slot, :, D_MODEL:]
            lane = lax.broadcasted_iota(jnp.int32, wts.shape, 1)
            w = jnp.sum(jnp.where(lane == slot_lane, wts, 0.0), axis=-1, keepdims=True)
            a = _dot(h, wg_ref[0, :, c0:c0 + ff_chunk])
            v = _dot(h, wu_ref[0, :, c0:c0 + ff_chunk])
            act = (a * (1.0 / (1.0 + jnp.exp(-a))) * v * w).astype(BF16)
            for half, d0 in enumerate(range(0, D_MODEL, D_MODEL // 2)):
                cols = slice(d0, d0 + D_MODEL // 2)
                if half == 1:
                    issue_copies(part + 1)
                y_part = _dot(act, wd_ref[0, c0:c0 + ff_chunk, cols])
                if part == 0:
                    acc_ref[slot, :, cols] = y_part
                else:
                    acc_ref[slot, :, cols] += y_part
            part += 2
    res = _rmsnorm(buf_ref[slot, :, :D_MODEL] + acc_ref[slot], gfin_ref[...])

    row = lax.broadcasted_iota(jnp.int32, (tm, 1), 0)
    mine = (row >= lo) & (row < hi)
    acc_ref[slot] = jnp.where(mine, res, jnp.where(lo > 0, acc_ref[other], 0.0))

    @pl.when(j == last)
    def _():
        wait_scatter()
        start_scatter(j, slot)
        wait_scatter()
        wait_gather(other)


def _expert_pairs(item_tile, item_ea, item_eb, item_lo, item_hi, slot_token, rows, g_ffn, g_final, wg, wu, wd):
    n, w = rows.shape
    d, f = D_MODEL, EXPERT_FF
    tm = EXPERT_TILE
    gate_a = pl.BlockSpec((1, d, f), lambda j, t, ea, eb, lo, hi, tok: (ea[j], 0, 0))
    gate_b = pl.BlockSpec((1, d, f), lambda j, t, ea, eb, lo, hi, tok: (eb[j], 0, 0))
    down_a = pl.BlockSpec((1, f, d), lambda j, t, ea, eb, lo, hi, tok: (ea[j], 0, 0))
    down_b = pl.BlockSpec((1, f, d), lambda j, t, ea, eb, lo, hi, tok: (eb[j], 0, 0))
    gain = pl.BlockSpec((1, d), lambda j, t, ea, eb, lo, hi, tok: (0, 0))
    grid_spec = pltpu.PrefetchScalarGridSpec(
        num_scalar_prefetch=6,
        grid=(item_tile.shape[0],),
        in_specs=[pl.BlockSpec(memory_space=pl.ANY), gain, gain,
                  gate_a, gate_a, down_a, gate_b, gate_b, down_b],
        out_specs=pl.BlockSpec(memory_space=pl.ANY),
        scratch_shapes=[pltpu.VMEM((2, tm, w), F32), pltpu.VMEM((2, tm, d), F32),
                        pltpu.SemaphoreType.DMA((2,)), pltpu.SemaphoreType.DMA(())],
    )
    vmem = 2 * 6 * d * f * 2 + 2 * tm * w * 4 + 2 * tm * d * 4 + 8 * tm * d * 4
    return pl.pallas_call(
        _expert_pair_kernel,
        grid_spec=grid_spec,
        out_shape=jax.ShapeDtypeStruct((n, d), F32),
        compiler_params=pltpu.CompilerParams(
            dimension_semantics=("arbitrary",), vmem_limit_bytes=vmem + 4 * MIB),
        name="expert_pairs",
    )(item_tile, item_ea, item_eb, item_lo, item_hi, slot_token, rows, g_ffn.reshape(1, d),
      g_final.reshape(1, d), wg, wu, wd, wg, wu, wd)


def kernel(x, g_mix, w_in, w_pool, pool_scale, w_fourier, w_out, g_ffn, w_group_router,
           b_group_router, w_expert_router, b_expert_router, w_gate, w_up, w_down, g_final):
    b, s, d = x.shape
    assert d == D_MODEL and s % (2 * SEQ_TILE) == 0 and s % (2 * TOKEN_TILE) == 0
    assert (b * s) % max(TOKEN_TILE, EXPERT_TILE) == 0
    n = b * s
    x2 = x.reshape(n, d)

    cw, sw = _fourier_weights(w_fourier, s)
    u, (w_out_bf16, w_gate_bf16) = _norm_proj(x2, g_mix, w_in.astype(BF16), [w_out, w_gate])
    mixed_lo, mixed_hi, w_up_bf16 = _mix(u.reshape(b, s, d), w_pool.astype(BF16), pool_scale, cw, sw, w_up)

    wr = jnp.concatenate([w_group_router, w_expert_router], axis=1)
    wr = jnp.pad(wr, ((0, 0), (0, ROUTER_LANES - wr.shape[1]))).astype(BF16)
    br = jnp.concatenate([b_group_router, b_expert_router])
    br = jnp.pad(br, (0, ROUTER_LANES - br.shape[0])).reshape(1, ROUTER_LANES)

    rows, route, counts, w_down_bf16 = _out_proj_route(
        x2, mixed_lo.reshape(n // 2, d), mixed_hi.reshape(n // 2, d), s, w_out_bf16, g_ffn, wr, br, w_down)
    pos, item_tile, item_ea, item_eb, item_lo, item_hi = _routing_plan(route, counts, n)
    slot_token = _invert_permutation(pos)
    out = _expert_pairs(item_tile, item_ea, item_eb, item_lo, item_hi, slot_token, rows, g_ffn, g_final,
                        w_gate_bf16, w_up_bf16, w_down_bf16)
    return out.reshape(b, s, d)
```

```python
import functools

import numpy as np
import jax
import jax.numpy as jnp
from jax import lax
from jax.experimental import pallas as pl
from jax.experimental.pallas import tpu as pltpu

D_MODEL = 2048
POOL_WINDOWS = (2, 4, 8, 16)
N_POOL_GROUPS = len(POOL_WINDOWS)
POOL_WIDTH = D_MODEL // 2
POOL_GROUP_DIM = POOL_WIDTH // N_POOL_GROUPS
FOURIER_WIDTH = D_MODEL - POOL_WIDTH
N_FOURIER_HEADS = 4
FOURIER_HEAD_DIM = FOURIER_WIDTH // N_FOURIER_HEADS
N_EXPERT_GROUPS = 4
EXPERTS_PER_GROUP = 4
N_EXPERTS = N_EXPERT_GROUPS * EXPERTS_PER_GROUP
EXPERT_FF = D_MODEL // 4
RMS_EPS = 1e-6

LANES = 128
SUBLANES = 8
BF16_SUBLANES = 16
ROUTER_LANES = LANES
MIB = 1024 * 1024

TOKEN_TILE = 512
SEQ_TILE = 256
POOL_HALO = BF16_SUBLANES
DFT_ROWS = SEQ_TILE + BF16_SUBLANES
TWIDDLE_ROWS = 16

PAIR_SLOT_A = (0, 0, 0, 1, 1, 3)
PAIR_SLOT_B = (1, 2, 3, 3, 2, 2)
PAIRS_PER_GROUP = len(PAIR_SLOT_A)
N_CLASSES = N_EXPERT_GROUPS * PAIRS_PER_GROUP
ROW_WORDS = D_MODEL + LANES
RANK_RADIX = 128
EXPERT_TILE = 256
DMA_PARTS = 8

BF16 = jnp.bfloat16
F32 = jnp.float32


def _rmsnorm(x, g):
    ms = jnp.mean(x * x, axis=-1, keepdims=True)
    return x * lax.rsqrt(ms + RMS_EPS) * g


def _dot(a, b):
    return jnp.dot(a, b, preferred_element_type=F32)


def _twiddle(rows, cols, period):
    m = (np.asarray(rows, np.int64)[:, None] * np.asarray(cols, np.int64)[None, :]) % period
    ang = (2.0 * np.pi / period) * m.astype(np.float64)
    return np.cos(ang).astype(np.float32), np.sin(ang).astype(np.float32)


def _pool_band(seq_len, tile, halo):
    n_tiles = seq_len // tile
    out = np.zeros((3, N_POOL_GROUPS, tile, tile + 2 * halo), np.float64)
    for v, m in enumerate((0, 1, n_tiles - 1)):
        t0 = m * tile
        for g, k in enumerate(POOL_WINDOWS):
            for r in range(tile):
                t = t0 + r
                lo = max(t - (k - 1) // 2, 0)
                hi = min(t + k // 2 + 1, seq_len)
                out[v, g, r, lo - t0 + halo:hi - t0 + halo] = 1.0 / (hi - lo)
                out[v, g, r, r + halo] -= 1.0
    return out.astype(np.float32)


def _fourier_weight_kernel(cd_ref, sd_ref, w_ref, cw_ref, sw_ref, *, scale):
    w = w_ref[0]
    cw = jnp.dot(cd_ref[...], w, preferred_element_type=F32, precision=lax.Precision.HIGHEST)
    sw = jnp.dot(sd_ref[...], w, preferred_element_type=F32, precision=lax.Precision.HIGHEST)
    cw_ref[0] = (cw * scale).astype(BF16)
    sw_ref[0] = (sw * (-scale)).astype(BF16)


def _fourier_weights(w_fourier, seq_len):
    dh = FOURIER_HEAD_DIM
    cd, sd = _twiddle(np.arange(dh), np.arange(dh), dh)
    scale = 1.0 / np.sqrt(float(seq_len * dh))
    mat = pl.BlockSpec((dh, dh), lambda h: (0, 0))
    per_head = pl.BlockSpec((1, dh, dh), lambda h: (h, 0, 0))
    return pl.pallas_call(
        functools.partial(_fourier_weight_kernel, scale=scale),
        grid=(N_FOURIER_HEADS,),
        in_specs=[mat, mat, per_head],
        out_specs=[per_head, per_head],
        out_shape=[jax.ShapeDtypeStruct((N_FOURIER_HEADS, dh, dh), BF16)] * 2,
        name="fourier_weights",
    )(jnp.asarray(cd), jnp.asarray(sd), w_fourier)


class _CastAlong:
    def __init__(self, w, n_chunks, chunk_of):
        cols = w.shape[-1]
        rows = w.size // cols // n_chunks
        self.shape = w.shape
        self.src = w.reshape(n_chunks, rows, cols)
        self.spec = pl.BlockSpec((1, rows, cols), lambda *idx: (chunk_of(*idx), 0, 0))
        self.out_shape = jax.ShapeDtypeStruct((n_chunks, rows, cols), BF16)
        self.vmem_bytes = 2 * rows * cols * (4 + 2)


def _cast_chunks(src_refs, dst_refs):
    for src_ref, dst_ref in zip(src_refs, dst_refs):
        dst_ref[...] = src_ref[...].astype(BF16)


def _norm_proj_kernel(x_ref, g_ref, w_ref, *rest, n_cast):
    cast_src, (u_ref, *cast_dst) = rest[:n_cast], rest[n_cast:]
    h = _rmsnorm(x_ref[...], g_ref[...])
    u_ref[...] = _dot(h.astype(BF16), w_ref[...]).astype(BF16)
    _cast_chunks(cast_src, cast_dst)


def _norm_proj(x2, g_mix, w_in_bf16, cast_weights):
    n, d = x2.shape
    tm = TOKEN_TILE
    casts = [_CastAlong(w, n // tm, lambda i: i) for w in cast_weights]
    vmem = (2 * tm * d * 4 + d * d * 2 + 2 * tm * d * 2 + 3 * tm * d * 4
            + sum(c.vmem_bytes for c in casts))
    u, *cast_out = pl.pallas_call(
        functools.partial(_norm_proj_kernel, n_cast=len(casts)),
        grid=(n // tm,),
        in_specs=[
            pl.BlockSpec((tm, d), lambda i: (i, 0)),
            pl.BlockSpec((1, d), lambda i: (0, 0)),
            pl.BlockSpec((d, d), lambda i: (0, 0), pipeline_mode=pl.Buffered(1)),
        ] + [c.spec for c in casts],
        out_specs=[pl.BlockSpec((tm, d), lambda i: (i, 0))] + [c.spec for c in casts],
        out_shape=[jax.ShapeDtypeStruct((n, d), BF16)] + [c.out_shape for c in casts],
        compiler_params=pltpu.CompilerParams(
            dimension_semantics=("arbitrary",), vmem_limit_bytes=vmem + 4 * MIB),
        name="norm_proj",
    )(x2, g_mix.reshape(1, d), w_in_bf16, *[c.src for c in casts])
    return u, [o.reshape(c.shape) for o, c in zip(cast_out, casts)]


def _mix_kernel(uf_ref, up_lo_ref, prev_lo_ref, next_lo_ref, up_hi_ref, prev_hi_ref, next_hi_ref,
                c0_ref, s0_ref, cph_ref, sph_ref, band_lo_ref, band_hi_ref, rev_ref,
                wpool_ref, pscale_ref, cw_ref, sw_ref, cast_src_ref, lo_ref, hi_ref, cast_dst_ref, lhs_ref):
    t = SEQ_TILE
    tp = DFT_ROWS
    gd = POOL_GROUP_DIM
    hd = FOURIER_HEAD_DIM

    @pl.when(pl.program_id(1) == 0)
    def _():
        cph = cph_ref[0]
        sph = sph_ref[0]
        for r0 in range(0, tp, TWIDDLE_ROWS):
            c0 = c0_ref[r0:r0 + TWIDDLE_ROWS, :]
            s0 = s0_ref[r0:r0 + TWIDDLE_ROWS, :]
            lhs_ref[r0:r0 + TWIDDLE_ROWS, :] = (cph * c0 - sph * s0).astype(BF16)
            lhs_ref[tp + r0:tp + r0 + TWIDDLE_ROWS, :] = (sph * c0 + cph * s0).astype(BF16)

    _cast_chunks([cast_src_ref], [cast_dst_ref])
    pq = _dot(lhs_ref[...], uf_ref[...])
    for h in range(N_FOURIER_HEADS):
        cols = slice(h * hd, (h + 1) * hd)
        out_cols = slice(POOL_WIDTH + h * hd, POOL_WIDTH + (h + 1) * hd)
        pc = _dot(pq[:tp, cols].astype(BF16), cw_ref[h])
        qs = _dot(pq[tp:, cols].astype(BF16), sw_ref[h])
        lo_ref[:, out_cols] = (pc + qs)[:t].astype(BF16)
        hi_ref[:, out_cols] = _dot(rev_ref[...], (pc - qs).astype(BF16)).astype(BF16)

    def pool(up_ref, prev_ref, next_ref, band_ref, out_ref):
        win = jnp.concatenate([prev_ref[...], up_ref[...], next_ref[...]], axis=0)
        for g in range(N_POOL_GROUPS):
            cols = slice(g * gd, (g + 1) * gd)
            pooled = _dot(band_ref[0, g], win[:, cols]).astype(BF16)
            y = _dot(pooled, wpool_ref[g]) * pscale_ref[:, cols]
            out_ref[:, cols] = y.astype(BF16)

    pool(up_lo_ref, prev_lo_ref, next_lo_ref, band_lo_ref, lo_ref)
    pool(up_hi_ref, prev_hi_ref, next_hi_ref, band_hi_ref, hi_ref)


def _mix(u3, w_pool_bf16, pool_scale, cw, sw, cast_weight):
    b, s, d = u3.shape
    t = SEQ_TILE
    tp = DFT_ROWS
    halo = POOL_HALO
    n_tiles = s // t
    n_steps = n_tiles // 2
    halo_blocks_per_tile = t // halo
    last_halo_block = s // halo - 1

    c0, s0 = _twiddle(np.arange(tp), np.arange(s), s)
    cph, sph = _twiddle(np.arange(0, s // 2, t), np.arange(s), s)
    band = jnp.asarray(_pool_band(s, t, halo)).astype(BF16)
    rev = np.zeros((t, tp), np.float32)
    rev[np.arange(t), t - np.arange(t)] = 1.0

    hi_tile = lambda m: n_tiles - 1 - m
    tile_spec = lambda tile_of: pl.BlockSpec((None, t, POOL_WIDTH), lambda m, bi: (bi, tile_of(m), 0))
    prev_spec = lambda tile_of: pl.BlockSpec(
        (None, halo, POOL_WIDTH),
        lambda m, bi: (bi, jnp.maximum(tile_of(m) * halo_blocks_per_tile - 1, 0), 0))
    next_spec = lambda tile_of: pl.BlockSpec(
        (None, halo, POOL_WIDTH),
        lambda m, bi: (bi, jnp.minimum((tile_of(m) + 1) * halo_blocks_per_tile, last_halo_block), 0))
    band_shape = (1, N_POOL_GROUPS, t, t + 2 * halo)
    const2 = lambda shape: pl.BlockSpec(shape, lambda m, bi: (0, 0))
    const3 = lambda shape: pl.BlockSpec(shape, lambda m, bi: (0, 0, 0))
    lo_tile = lambda m: m

    in_specs = [
        pl.BlockSpec((None, s, FOURIER_WIDTH), lambda m, bi: (bi, 0, 1)),
        tile_spec(lo_tile), prev_spec(lo_tile), next_spec(lo_tile),
        tile_spec(hi_tile), prev_spec(hi_tile), next_spec(hi_tile),
        pl.BlockSpec((tp, s), lambda m, bi: (0, 0), pipeline_mode=pl.Buffered(1)),
        pl.BlockSpec((tp, s), lambda m, bi: (0, 0), pipeline_mode=pl.Buffered(1)),
        pl.BlockSpec((1, 1, s), lambda m, bi: (m, 0, 0)),
        pl.BlockSpec((1, 1, s), lambda m, bi: (m, 0, 0)),
        pl.BlockSpec(band_shape, lambda m, bi: (jnp.where(m == 0, 0, 1), 0, 0, 0)),
        pl.BlockSpec(band_shape, lambda m, bi: (jnp.where(m == 0, 2, 1), 0, 0, 0)),
        const2((t, tp)),
        const3((N_POOL_GROUPS, POOL_GROUP_DIM, POOL_GROUP_DIM)),
        const2((1, POOL_WIDTH)),
        const3((N_FOURIER_HEADS, FOURIER_HEAD_DIM, FOURIER_HEAD_DIM)),
        const3((N_FOURIER_HEADS, FOURIER_HEAD_DIM, FOURIER_HEAD_DIM)),
    ]
    half_out = pl.BlockSpec((None, t, d), lambda m, bi: (bi, m, 0))
    hi_out = pl.BlockSpec((None, t, d), lambda m, bi: (bi, n_steps - 1 - m, 0))
    cast = _CastAlong(cast_weight, n_steps * b, lambda m, bi: m * b + bi)
    vmem = (2 * s * FOURIER_WIDTH * 2
            + 2 * tp * s * 4
            + 2 * tp * s * 2
            + 2 * 2 * t * d * 2 * 2
            + 4 * 2 * tp * FOURIER_WIDTH * 4
            + cast.vmem_bytes)
    lo, hi, cast_out = pl.pallas_call(
        _mix_kernel,
        grid=(n_steps, b),
        in_specs=in_specs + [cast.spec],
        out_specs=[half_out, hi_out, cast.spec],
        out_shape=[jax.ShapeDtypeStruct((b, s // 2, d), BF16)] * 2 + [cast.out_shape],
        scratch_shapes=[pltpu.VMEM((2 * tp, s), BF16)],
        compiler_params=pltpu.CompilerParams(
            dimension_semantics=("arbitrary", "arbitrary"), vmem_limit_bytes=vmem + 4 * MIB),
        name="seq_mix",
    )(u3, u3, u3, u3, u3, u3, u3, jnp.asarray(c0), jnp.asarray(s0),
      jnp.asarray(cph).reshape(n_steps, 1, s), jnp.asarray(sph).reshape(n_steps, 1, s),
      band, band, jnp.asarray(rev).astype(BF16),
      w_pool_bf16, pool_scale.reshape(1, POOL_WIDTH), cw, sw, cast.src)
    return lo, hi, cast_out.reshape(cast.shape)


def _out_proj_route_kernel(x_ref, mixed_lo_ref, mixed_hi_ref, wout_ref, g_ref, wr_ref, br_ref,
                           tri_ref, pick_ref, cast_src_ref, rows_ref, route_ref, counts_ref, cast_dst_ref,
                           carry_ref, x1_ref, *, tiles_per_seq, n_tiles):
    step = pl.program_id(0)

    @pl.when(step == 0)
    def _():
        carry_ref[...] = jnp.zeros_like(carry_ref)
        x1_ref[...] = jnp.zeros_like(x1_ref)

    _cast_chunks([cast_src_ref], [cast_dst_ref])

    x1 = x1_ref[...]
    rows_ref[:, :D_MODEL] = x1
    h2 = _rmsnorm(x1, g_ref[...])
    logits = _dot(h2.astype(BF16), wr_ref[...]) + br_ref[...]

    tile = jnp.minimum(step, n_tiles - 1)
    in_lo_half = (tile % tiles_per_seq) < tiles_per_seq // 2
    mixed = jnp.where(in_lo_half, mixed_lo_ref[...], mixed_hi_ref[...])
    x1_ref[...] = x_ref[...] + _dot(mixed, wout_ref[...])

    lane = lax.broadcasted_iota(jnp.int32, logits.shape, 1)
    neg = jnp.float32(-jnp.inf)
    big = jnp.int32(ROUTER_LANES)

    is_group = lane < N_EXPERT_GROUPS
    gl = jnp.where(is_group, logits, neg)
    gmax = jnp.max(gl, axis=-1, keepdims=True)
    gidx = jnp.min(jnp.where(gl == gmax, lane, big), axis=-1, keepdims=True)
    p_g = 1.0 / jnp.sum(jnp.exp(gl - gmax), axis=-1, keepdims=True)

    e_lane = lane - N_EXPERT_GROUPS
    in_group = (e_lane >= gidx * EXPERTS_PER_GROUP) & (e_lane < (gidx + 1) * EXPERTS_PER_GROUP)
    el = jnp.where(in_group, logits, neg)
    v1 = jnp.max(el, axis=-1, keepdims=True)
    i1 = jnp.min(jnp.where(el == v1, lane, big), axis=-1, keepdims=True)
    el2 = jnp.where(lane == i1, neg, el)
    v2 = jnp.max(el2, axis=-1, keepdims=True)
    i2 = jnp.min(jnp.where(el2 == v2, lane, big), axis=-1, keepdims=True)
    r = jnp.exp(v2 - v1)
    w1 = p_g / (1.0 + r)
    w2 = p_g * r / (1.0 + r)

    first_is_low = i1 < i2
    first_lane = N_EXPERT_GROUPS + gidx * EXPERTS_PER_GROUP
    la = jnp.where(first_is_low, i1, i2) - first_lane
    lb = jnp.where(first_is_low, i2, i1) - first_lane
    w_low = jnp.where(first_is_low, w1, w2)
    w_high = jnp.where(first_is_low, w2, w1)
    pair = jnp.where(la == 0, lb - 1, jnp.where(la == 1, 6 - lb, 5))
    slot_a_is_high = la == 2
    w_a = jnp.where(slot_a_is_high, w_high, w_low)
    w_b = jnp.where(slot_a_is_high, w_low, w_high)
    cls = gidx * PAIRS_PER_GROUP + pair
    rows_ref[:, D_MODEL:] = jnp.where(lane == 0, w_a, jnp.where(lane == 1, w_b, 0.0))

    onehot = jnp.where(lane == cls, 1.0, 0.0)
    before = _dot(tri_ref[...], onehot.astype(BF16)) + carry_ref[...]
    rank = jnp.sum(jnp.where(lane == cls, before, 0.0), axis=-1, keepdims=True)
    carry_ref[...] += jnp.sum(onehot, axis=0, keepdims=True) * jnp.where(step > 0, 1.0, 0.0)
    counts_ref[...] = carry_ref[...]

    rank_hi = jnp.floor(rank * (1.0 / RANK_RADIX))
    rank_lo = rank - rank_hi * RANK_RADIX
    digits = jnp.where(lane == 0, cls.astype(F32),
                       jnp.where(lane == 1, rank_hi, jnp.where(lane == 2, rank_lo, 0.0)))
    route_ref[...] = lax.dot_general(pick_ref[...], digits.astype(BF16), (((1,), (1,)), ((), ())),
                                     preferred_element_type=F32)


def _out_proj_route(x2, mixed_lo, mixed_hi, seq_len, w_out_bf16, g_ffn, wr, br, cast_weight):
    n, d = x2.shape
    tm = TOKEN_TILE
    n_tiles = n // tm
    tiles_per_seq = seq_len // tm
    half_tiles = tiles_per_seq // 2
    const = lambda shape: pl.BlockSpec(shape, lambda i: (0, 0))
    in_tile = lambda i: jnp.minimum(i, n_tiles - 1)
    out_tile = lambda i: jnp.maximum(i - 1, 0)
    lo_spec = pl.BlockSpec((tm, d), lambda i: (
        (in_tile(i) // tiles_per_seq) * half_tiles + jnp.minimum(in_tile(i) % tiles_per_seq, half_tiles - 1), 0))
    hi_spec = pl.BlockSpec((tm, d), lambda i: (
        (in_tile(i) // tiles_per_seq) * half_tiles + jnp.maximum(in_tile(i) % tiles_per_seq - half_tiles, 0), 0))
    tri = np.tril(np.ones((tm, tm), np.float32), -1)
    pick = np.eye(SUBLANES, ROUTER_LANES, dtype=np.float32)
    vmem = (2 * tm * d * 4 + 4 * tm * d * 2 + 2 * tm * ROW_WORDS * 4 + d * d * 2 + tm * d * 4
            + 2 * d * ROUTER_LANES * 2 + 2 * tm * tm * 2 + 3 * tm * d * 4)
    cast = _CastAlong(cast_weight, n_tiles, in_tile)
    vmem += cast.vmem_bytes
    rows, route, counts, cast_out = pl.pallas_call(
        functools.partial(_out_proj_route_kernel, tiles_per_seq=tiles_per_seq, n_tiles=n_tiles),
        grid=(n_tiles + 1,),
        in_specs=[pl.BlockSpec((tm, d), lambda i: (in_tile(i), 0)), lo_spec, hi_spec,
                  pl.BlockSpec((d, d), lambda i: (0, 0), pipeline_mode=pl.Buffered(1)),
                  const((1, d)), const((d, ROUTER_LANES)), const((1, ROUTER_LANES)),
                  const((tm, tm)), const((SUBLANES, ROUTER_LANES)), cast.spec],
        out_specs=[pl.BlockSpec((tm, ROW_WORDS), lambda i: (out_tile(i), 0)),
                   pl.BlockSpec((SUBLANES, tm), lambda i: (0, out_tile(i))),
                   const((1, ROUTER_LANES)), cast.spec],
        out_shape=[jax.ShapeDtypeStruct((n, ROW_WORDS), F32),
                   jax.ShapeDtypeStruct((SUBLANES, n), F32),
                   jax.ShapeDtypeStruct((1, ROUTER_LANES), F32), cast.out_shape],
        scratch_shapes=[pltpu.VMEM((1, ROUTER_LANES), F32), pltpu.VMEM((tm, d), F32)],
        compiler_params=pltpu.CompilerParams(
            dimension_semantics=("arbitrary",), vmem_limit_bytes=vmem + 4 * MIB),
        name="out_proj_route",
    )(x2, mixed_lo, mixed_hi, w_out_bf16, g_ffn.reshape(1, d), wr, br,
      jnp.asarray(tri).astype(BF16), jnp.asarray(pick).astype(BF16), cast.src)
    return rows, route, counts, cast_out.reshape(cast.shape)


def _routing_plan(route, counts, n):
    tm = EXPERT_TILE
    cnt = counts[0, :N_CLASSES].astype(jnp.int32)
    class_end = jnp.cumsum(cnt)
    class_start = class_end - cnt
    cls = route[0].astype(jnp.int32)
    rank = (route[1] * RANK_RADIX + route[2]).astype(jnp.int32)
    pos = class_start[cls] + rank

    n_items = n // tm + N_CLASSES
    item = jnp.arange(n_items, dtype=jnp.int32)

    def expand(items_per_class):
        item_end = jnp.cumsum(items_per_class)
        item_c = jnp.minimum(item, item_end[-1] - 1)
        item_cls = jnp.sum((item_c[:, None] >= item_end[None, :]).astype(jnp.int32), axis=1)
        return item_cls, item_c - (item_end - items_per_class)[item_cls], item_end[-1]

    in_cls, in_k, in_used = expand((cnt + tm - 1) // tm)
    in_start = jnp.where(in_k < cnt[in_cls] // tm, class_start[in_cls] + in_k * tm, class_end[in_cls] - tm)
    first_tile = class_start // tm
    al_cls, al_k, al_used = expand(jnp.where(cnt > 0, (class_end - 1) // tm - first_tile + 1, 0))
    al_start = (first_tile[al_cls] + al_k) * tm
    al_lo = jnp.clip(class_start[al_cls] - al_start, 0, tm)
    al_hi = jnp.clip(class_end[al_cls] - al_start, 0, tm)

    inside = jnp.all((cnt == 0) | (cnt >= tm))
    item_cls = jnp.where(inside, in_cls, al_cls)
    start = jnp.where(inside, in_start, al_start)
    lo = jnp.where(inside, 0, al_lo)
    hi = jnp.where(inside, tm, al_hi)
    n_used = jnp.where(inside, in_used, al_used).reshape(1)
    group_base = np.repeat(np.arange(N_EXPERT_GROUPS) * EXPERTS_PER_GROUP, PAIRS_PER_GROUP)
    class_ea = jnp.asarray(group_base + np.tile(PAIR_SLOT_A, N_EXPERT_GROUPS), jnp.int32)
    class_eb = jnp.asarray(group_base + np.tile(PAIR_SLOT_B, N_EXPERT_GROUPS), jnp.int32)
    return pos, start, class_ea[item_cls], class_eb[item_cls], lo, hi, n_used


def _row_copy(src_ref, src_row, dst_ref, dst_row, sem):
    return pltpu.make_async_copy(src_ref.at[pl.ds(src_row, 1), :], dst_ref.at[pl.ds(dst_row, 1), :], sem)


def _invert_permutation_kernel(pos_ref, inv_ref):
    def body(t, carry):
        inv_ref[pos_ref[t]] = t
        return carry

    lax.fori_loop(0, pos_ref.shape[0], body, 0, unroll=8)


def _invert_permutation(pos):
    return pl.pallas_call(
        _invert_permutation_kernel,
        in_specs=[pl.BlockSpec(memory_space=pltpu.SMEM)],
        out_specs=pl.BlockSpec(memory_space=pltpu.SMEM),
        out_shape=jax.ShapeDtypeStruct(pos.shape, jnp.int32),
        name="invert_permutation",
    )(pos)


def _expert_pair_kernel(start_ref, ea_ref, eb_ref, lo_ref, hi_ref, nused_ref, tok_ref,
                        rows_hbm_ref, gffn_ref, gfin_ref,
                        wga_ref, wua_ref, wda_ref, wgb_ref, wub_ref, wdb_ref, out_hbm_ref,
                        buf_ref, acc_ref, gather_sem, scatter_sem):
    del ea_ref, eb_ref
    tm = EXPERT_TILE
    j = pl.program_id(0)
    last = nused_ref[0] - 1
    slot = j % 2
    other = 1 - slot

    def start_gather(item, dst_slot, rows=range(tm)):
        base = start_ref[item]
        for r in rows:
            _row_copy(rows_hbm_ref, tok_ref[base + r], buf_ref.at[dst_slot], r, gather_sem.at[dst_slot]).start()

    def wait_gather(dst_slot):
        pltpu.make_async_copy(rows_hbm_ref.at[pl.ds(0, tm), :], buf_ref.at[dst_slot],
                              gather_sem.at[dst_slot]).wait()

    def start_scatter(item, src_slot, rows=range(tm)):
        base = start_ref[item]
        for r in rows:
            _row_copy(acc_ref.at[src_slot], r, out_hbm_ref, tok_ref[base + r], scatter_sem).start()

    def wait_scatter():
        pltpu.make_async_copy(acc_ref.at[0], out_hbm_ref.at[pl.ds(0, tm), :], scatter_sem).wait()

    @pl.when(j == 0)
    def _():
        acc_ref[...] = jnp.zeros_like(acc_ref)
        start_gather(0, 0)
        start_scatter(0, 1)

    @pl.when(j <= last)
    def _():
        wait_scatter()
        wait_gather(slot)
        prev_item = jnp.maximum(j - 1, 0)
        next_item = jnp.minimum(j + 1, last)

        def issue_copies(part):
            rows = range(part * tm // DMA_PARTS, (part + 1) * tm // DMA_PARTS)
            start_scatter(prev_item, other, rows)
            start_gather(next_item, other, rows)

        lo = lo_ref[j]
        hi = hi_ref[j]
        x1 = buf_ref[slot, :, :D_MODEL]
        h = _rmsnorm(x1, gffn_ref[...]).astype(BF16)

        part = 0
        ff_chunk = EXPERT_FF * 4 // DMA_PARTS
        for slot_lane, (wg_ref, wu_ref, wd_ref) in enumerate(((wga_ref, wua_ref, wda_ref),
                                                              (wgb_ref, wub_ref, wdb_ref))):
            for c0 in range(0, EXPERT_FF, ff_chunk):
                issue_copies(part)
                wts = buf_ref[slot, :, D_MODEL:]
                lane = lax.broadcasted_iota(jnp.int32, wts.shape, 1)
                w = jnp.sum(jnp.where(lane == slot_lane, wts, 0.0), axis=-1, keepdims=True)
                a = _dot(h, wg_ref[0, :, c0:c0 + ff_chunk])
                v = _dot(h, wu_ref[0, :, c0:c0 + ff_chunk])
                act = (a * (1.0 / (1.0 + jnp.exp(-a))) * v * w).astype(BF16)
                for half, d0 in enumerate(range(0, D_MODEL, D_MODEL // 2)):
                    cols = slice(d0, d0 + D_MODEL // 2)
                    if half == 1:
                        issue_copies(part + 1)
                    y_part = _dot(act, wd_ref[0, c0:c0 + ff_chunk, cols])
                    if part == 0:
                        acc_ref[slot, :, cols] = y_part
                    else:
                        acc_ref[slot, :, cols] += y_part
                part += 2
        res = _rmsnorm(buf_ref[slot, :, :D_MODEL] + acc_ref[slot], gfin_ref[...])

        row = lax.broadcasted_iota(jnp.int32, (tm, 1), 0)
        mine = (row >= lo) & (row < hi)
        acc_ref[slot] = jnp.where(mine, res, jnp.where(lo > 0, acc_ref[other], 0.0))

    @pl.when(j == last)
    def _():
        wait_scatter()
        start_scatter(j, slot)
        wait_scatter()
        wait_gather(other)


def _expert_pairs(item_start, item_ea, item_eb, item_lo, item_hi, n_used, slot_token, rows,
                  g_ffn, g_final, wg, wu, wd):
    n, w = rows.shape
    d, f = D_MODEL, EXPERT_FF
    tm = EXPERT_TILE
    gate_a = pl.BlockSpec((1, d, f), lambda j, st, ea, eb, lo, hi, nu, tok: (ea[j], 0, 0))
    gate_b = pl.BlockSpec((1, d, f), lambda j, st, ea, eb, lo, hi, nu, tok: (eb[j], 0, 0))
    down_a = pl.BlockSpec((1, f, d), lambda j, st, ea, eb, lo, hi, nu, tok: (ea[j], 0, 0))
    down_b = pl.BlockSpec((1, f, d), lambda j, st, ea, eb, lo, hi, nu, tok: (eb[j], 0, 0))
    gain = pl.BlockSpec((1, d), lambda j, st, ea, eb, lo, hi, nu, tok: (0, 0))
    grid_spec = pltpu.PrefetchScalarGridSpec(
        num_scalar_prefetch=7,
        grid=(item_start.shape[0],),
        in_specs=[pl.BlockSpec(memory_space=pl.ANY), gain, gain,
                  gate_a, gate_a, down_a, gate_b, gate_b, down_b],
        out_specs=pl.BlockSpec(memory_space=pl.ANY),
        scratch_shapes=[pltpu.VMEM((2, tm, w), F32), pltpu.VMEM((2, tm, d), F32),
                        pltpu.SemaphoreType.DMA((2,)), pltpu.SemaphoreType.DMA(())],
    )
    vmem = 2 * 6 * d * f * 2 + 2 * tm * w * 4 + 2 * tm * d * 4 + 8 * tm * d * 4
    return pl.pallas_call(
        _expert_pair_kernel,
        grid_spec=grid_spec,
        out_shape=jax.ShapeDtypeStruct((n, d), F32),
        compiler_params=pltpu.CompilerParams(
            dimension_semantics=("arbitrary",), vmem_limit_bytes=vmem + 4 * MIB),
        name="expert_pairs",
    )(item_start, item_ea, item_eb, item_lo, item_hi, n_used, slot_token, rows, g_ffn.reshape(1, d),
      g_final.reshape(1, d), wg, wu, wd, wg, wu, wd)


def kernel(x, g_mix, w_in, w_pool, pool_scale, w_fourier, w_out, g_ffn, w_group_router,
           b_group_router, w_expert_router, b_expert_router, w_gate, w_up, w_down, g_final):
    b, s, d = x.shape
    assert d == D_MODEL and s % (2 * SEQ_TILE) == 0 and s % (2 * TOKEN_TILE) == 0
    assert (b * s) % max(TOKEN_TILE, EXPERT_TILE) == 0
    n = b * s
    x2 = x.reshape(n, d)

    cw, sw = _fourier_weights(w_fourier, s)
    u, (w_out_bf16, w_gate_bf16) = _norm_proj(x2, g_mix, w_in.astype(BF16), [w_out, w_gate])
    mixed_lo, mixed_hi, w_up_bf16 = _mix(u.reshape(b, s, d), w_pool.astype(BF16), pool_scale, cw, sw, w_up)

    wr = jnp.concatenate([w_group_router, w_expert_router], axis=1)
    wr = jnp.pad(wr, ((0, 0), (0, ROUTER_LANES - wr.shape[1]))).astype(BF16)
    br = jnp.concatenate([b_group_router, b_expert_router])
    br = jnp.pad(br, (0, ROUTER_LANES - br.shape[0])).reshape(1, ROUTER_LANES)

    rows, route, counts, w_down_bf16 = _out_proj_route(
        x2, mixed_lo.reshape(n // 2, d), mixed_hi.reshape(n // 2, d), s, w_out_bf16, g_ffn, wr, br, w_down)
    pos, item_start, item_ea, item_eb, item_lo, item_hi, n_used = _routing_plan(route, counts, n)
    slot_token = _invert_permutation(pos)
    out = _expert_pairs(item_start, item_ea, item_eb, item_lo, item_hi, n_used, slot_token, rows,
                        g_ffn, g_final, w_gate_bf16, w_up_bf16, w_down_bf16)
    return out.reshape(b, s, d)
```

```python
import functools

import numpy as np
import jax
import jax.numpy as jnp
from jax import lax
from jax.experimental import pallas as pl
from jax.experimental.pallas import tpu as pltpu

D_MODEL = 2048
POOL_WINDOWS = (2, 4, 8, 16)
N_POOL_GROUPS = len(POOL_WINDOWS)
POOL_WIDTH = D_MODEL // 2
POOL_GROUP_DIM = POOL_WIDTH // N_POOL_GROUPS
FOURIER_WIDTH = D_MODEL - POOL_WIDTH
N_FOURIER_HEADS = 4
FOURIER_HEAD_DIM = FOURIER_WIDTH // N_FOURIER_HEADS
N_EXPERT_GROUPS = 4
EXPERTS_PER_GROUP = 4
N_EXPERTS = N_EXPERT_GROUPS * EXPERTS_PER_GROUP
EXPERT_FF = D_MODEL // 4
RMS_EPS = 1e-6

LANES = 128
SUBLANES = 8
BF16_SUBLANES = 16
ROUTER_LANES = LANES
MIB = 1024 * 1024

TOKEN_TILE = 512
SEQ_TILE = 256
POOL_HALO = BF16_SUBLANES
DFT_ROWS = SEQ_TILE + BF16_SUBLANES
TWIDDLE_ROWS = 16

PAIR_SLOT_A = (0, 0, 0, 1, 1, 3)
PAIR_SLOT_B = (1, 2, 3, 3, 2, 2)
PAIRS_PER_GROUP = len(PAIR_SLOT_A)
N_CLASSES = N_EXPERT_GROUPS * PAIRS_PER_GROUP
ROW_WORDS = D_MODEL + LANES
RANK_RADIX = 128
EXPERT_TILE = 256
DMA_PARTS = 8
DMA_QUEUES = 2

BF16 = jnp.bfloat16
F32 = jnp.float32


def _rmsnorm(x, g):
    ms = jnp.mean(x * x, axis=-1, keepdims=True)
    return x * lax.rsqrt(ms + RMS_EPS) * g


def _dot(a, b):
    return jnp.dot(a, b, preferred_element_type=F32)


def _twiddle(rows, cols, period):
    m = (np.asarray(rows, np.int64)[:, None] * np.asarray(cols, np.int64)[None, :]) % period
    ang = (2.0 * np.pi / period) * m.astype(np.float64)
    return np.cos(ang).astype(np.float32), np.sin(ang).astype(np.float32)


def _pool_band(seq_len, tile, halo):
    n_tiles = seq_len // tile
    out = np.zeros((3, N_POOL_GROUPS, tile, tile + 2 * halo), np.float64)
    for v, m in enumerate((0, 1, n_tiles - 1)):
        t0 = m * tile
        for g, k in enumerate(POOL_WINDOWS):
            for r in range(tile):
                t = t0 + r
                lo = max(t - (k - 1) // 2, 0)
                hi = min(t + k // 2 + 1, seq_len)
                out[v, g, r, lo - t0 + halo:hi - t0 + halo] = 1.0 / (hi - lo)
                out[v, g, r, r + halo] -= 1.0
    return out.astype(np.float32)


def _fourier_weight_kernel(cd_ref, sd_ref, w_ref, cw_ref, sw_ref, *, scale):
    w = w_ref[0]
    cw = jnp.dot(cd_ref[...], w, preferred_element_type=F32, precision=lax.Precision.HIGHEST)
    sw = jnp.dot(sd_ref[...], w, preferred_element_type=F32, precision=lax.Precision.HIGHEST)
    cw_ref[0] = (cw * scale).astype(BF16)
    sw_ref[0] = (sw * (-scale)).astype(BF16)


def _fourier_weights(w_fourier, seq_len):
    dh = FOURIER_HEAD_DIM
    cd, sd = _twiddle(np.arange(dh), np.arange(dh), dh)
    scale = 1.0 / np.sqrt(float(seq_len * dh))
    mat = pl.BlockSpec((dh, dh), lambda h: (0, 0))
    per_head = pl.BlockSpec((1, dh, dh), lambda h: (h, 0, 0))
    return pl.pallas_call(
        functools.partial(_fourier_weight_kernel, scale=scale),
        grid=(N_FOURIER_HEADS,),
        in_specs=[mat, mat, per_head],
        out_specs=[per_head, per_head],
        out_shape=[jax.ShapeDtypeStruct((N_FOURIER_HEADS, dh, dh), BF16)] * 2,
        name="fourier_weights",
    )(jnp.asarray(cd), jnp.asarray(sd), w_fourier)


class _CastAlong:
    def __init__(self, w, n_chunks, chunk_of):
        cols = w.shape[-1]
        rows = w.size // cols // n_chunks
        self.shape = w.shape
        self.src = w.reshape(n_chunks, rows, cols)
        self.spec = pl.BlockSpec((1, rows, cols), lambda *idx: (chunk_of(*idx), 0, 0))
        self.out_shape = jax.ShapeDtypeStruct((n_chunks, rows, cols), BF16)
        self.vmem_bytes = 2 * rows * cols * (4 + 2)


def _cast_chunks(src_refs, dst_refs):
    for src_ref, dst_ref in zip(src_refs, dst_refs):
        dst_ref[...] = src_ref[...].astype(BF16)


def _norm_proj_kernel(x_ref, g_ref, w_ref, *rest, n_cast):
    cast_src, (u_ref, *cast_dst) = rest[:n_cast], rest[n_cast:]
    h = _rmsnorm(x_ref[...], g_ref[...])
    u_ref[...] = _dot(h.astype(BF16), w_ref[...]).astype(BF16)
    _cast_chunks(cast_src, cast_dst)


def _norm_proj(x2, g_mix, w_in_bf16, cast_weights):
    n, d = x2.shape
    tm = TOKEN_TILE
    casts = [_CastAlong(w, n // tm, lambda i: i) for w in cast_weights]
    vmem = (2 * tm * d * 4 + d * d * 2 + 2 * tm * d * 2 + 3 * tm * d * 4
            + sum(c.vmem_bytes for c in casts))
    u, *cast_out = pl.pallas_call(
        functools.partial(_norm_proj_kernel, n_cast=len(casts)),
        grid=(n // tm,),
        in_specs=[
            pl.BlockSpec((tm, d), lambda i: (i, 0)),
            pl.BlockSpec((1, d), lambda i: (0, 0)),
            pl.BlockSpec((d, d), lambda i: (0, 0), pipeline_mode=pl.Buffered(1)),
        ] + [c.spec for c in casts],
        out_specs=[pl.BlockSpec((tm, d), lambda i: (i, 0))] + [c.spec for c in casts],
        out_shape=[jax.ShapeDtypeStruct((n, d), BF16)] + [c.out_shape for c in casts],
        compiler_params=pltpu.CompilerParams(
            dimension_semantics=("arbitrary",), vmem_limit_bytes=vmem + 4 * MIB),
        name="norm_proj",
    )(x2, g_mix.reshape(1, d), w_in_bf16, *[c.src for c in casts])
    return u, [o.reshape(c.shape) for o, c in zip(cast_out, casts)]


def _mix_kernel(uf_ref, up_lo_ref, prev_lo_ref, next_lo_ref, up_hi_ref, prev_hi_ref, next_hi_ref,
                c0_ref, s0_ref, cph_ref, sph_ref, band_lo_ref, band_hi_ref, rev_ref,
                wpool_ref, pscale_ref, cw_ref, sw_ref, cast_src_ref, lo_ref, hi_ref, cast_dst_ref, lhs_ref):
    t = SEQ_TILE
    tp = DFT_ROWS
    gd = POOL_GROUP_DIM
    hd = FOURIER_HEAD_DIM

    @pl.when(pl.program_id(1) == 0)
    def _():
        cph = cph_ref[0]
        sph = sph_ref[0]
        for r0 in range(0, tp, TWIDDLE_ROWS):
            c0 = c0_ref[r0:r0 + TWIDDLE_ROWS, :]
            s0 = s0_ref[r0:r0 + TWIDDLE_ROWS, :]
            lhs_ref[r0:r0 + TWIDDLE_ROWS, :] = (cph * c0 - sph * s0).astype(BF16)
            lhs_ref[tp + r0:tp + r0 + TWIDDLE_ROWS, :] = (sph * c0 + cph * s0).astype(BF16)

    _cast_chunks([cast_src_ref], [cast_dst_ref])
    pq = _dot(lhs_ref[...], uf_ref[...])
    for h in range(N_FOURIER_HEADS):
        cols = slice(h * hd, (h + 1) * hd)
        out_cols = slice(POOL_WIDTH + h * hd, POOL_WIDTH + (h + 1) * hd)
        pc = _dot(pq[:tp, cols].astype(BF16), cw_ref[h])
        qs = _dot(pq[tp:, cols].astype(BF16), sw_ref[h])
        lo_ref[:, out_cols] = (pc + qs)[:t].astype(BF16)
        hi_ref[:, out_cols] = _dot(rev_ref[...], (pc - qs).astype(BF16)).astype(BF16)

    def pool(up_ref, prev_ref, next_ref, band_ref, out_ref):
        win = jnp.concatenate([prev_ref[...], up_ref[...], next_ref[...]], axis=0)
        for g in range(N_POOL_GROUPS):
            cols = slice(g * gd, (g + 1) * gd)
            pooled = _dot(band_ref[0, g], win[:, cols]).astype(BF16)
            y = _dot(pooled, wpool_ref[g]) * pscale_ref[:, cols]
            out_ref[:, cols] = y.astype(BF16)

    pool(up_lo_ref, prev_lo_ref, next_lo_ref, band_lo_ref, lo_ref)
    pool(up_hi_ref, prev_hi_ref, next_hi_ref, band_hi_ref, hi_ref)


def _mix(u3, w_pool_bf16, pool_scale, cw, sw, cast_weight):
    b, s, d = u3.shape
    t = SEQ_TILE
    tp = DFT_ROWS
    halo = POOL_HALO
    n_tiles = s // t
    n_steps = n_tiles // 2
    halo_blocks_per_tile = t // halo
    last_halo_block = s // halo - 1

    c0, s0 = _twiddle(np.arange(tp), np.arange(s), s)
    cph, sph = _twiddle(np.arange(0, s // 2, t), np.arange(s), s)
    band = jnp.asarray(_pool_band(s, t, halo)).astype(BF16)
    rev = np.zeros((t, tp), np.float32)
    rev[np.arange(t), t - np.arange(t)] = 1.0

    hi_tile = lambda m: n_tiles - 1 - m
    tile_spec = lambda tile_of: pl.BlockSpec((None, t, POOL_WIDTH), lambda m, bi: (bi, tile_of(m), 0))
    prev_spec = lambda tile_of: pl.BlockSpec(
        (None, halo, POOL_WIDTH),
        lambda m, bi: (bi, jnp.maximum(tile_of(m) * halo_blocks_per_tile - 1, 0), 0))
    next_spec = lambda tile_of: pl.BlockSpec(
        (None, halo, POOL_WIDTH),
        lambda m, bi: (bi, jnp.minimum((tile_of(m) + 1) * halo_blocks_per_tile, last_halo_block), 0))
    band_shape = (1, N_POOL_GROUPS, t, t + 2 * halo)
    const2 = lambda shape: pl.BlockSpec(shape, lambda m, bi: (0, 0))
    const3 = lambda shape: pl.BlockSpec(shape, lambda m, bi: (0, 0, 0))
    lo_tile = lambda m: m

    in_specs = [
        pl.BlockSpec((None, s, FOURIER_WIDTH), lambda m, bi: (bi, 0, 1)),
        tile_spec(lo_tile), prev_spec(lo_tile), next_spec(lo_tile),
        tile_spec(hi_tile), prev_spec(hi_tile), next_spec(hi_tile),
        pl.BlockSpec((tp, s), lambda m, bi: (0, 0), pipeline_mode=pl.Buffered(1)),
        pl.BlockSpec((tp, s), lambda m, bi: (0, 0), pipeline_mode=pl.Buffered(1)),
        pl.BlockSpec((1, 1, s), lambda m, bi: (m, 0, 0)),
        pl.BlockSpec((1, 1, s), lambda m, bi: (m, 0, 0)),
        pl.BlockSpec(band_shape, lambda m, bi: (jnp.where(m == 0, 0, 1), 0, 0, 0)),
        pl.BlockSpec(band_shape, lambda m, bi: (jnp.where(m == 0, 2, 1), 0, 0, 0)),
        const2((t, tp)),
        const3((N_POOL_GROUPS, POOL_GROUP_DIM, POOL_GROUP_DIM)),
        const2((1, POOL_WIDTH)),
        const3((N_FOURIER_HEADS, FOURIER_HEAD_DIM, FOURIER_HEAD_DIM)),
        const3((N_FOURIER_HEADS, FOURIER_HEAD_DIM, FOURIER_HEAD_DIM)),
    ]
    half_out = pl.BlockSpec((None, t, d), lambda m, bi: (bi, m, 0))
    hi_out = pl.BlockSpec((None, t, d), lambda m, bi: (bi, n_steps - 1 - m, 0))
    cast = _CastAlong(cast_weight, n_steps * b, lambda m, bi: m * b + bi)
    vmem = (2 * s * FOURIER_WIDTH * 2
            + 2 * tp * s * 4
            + 2 * tp * s * 2
            + 2 * 2 * t * d * 2 * 2
            + 4 * 2 * tp * FOURIER_WIDTH * 4
            + cast.vmem_bytes)
    lo, hi, cast_out = pl.pallas_call(
        _mix_kernel,
        grid=(n_steps, b),
        in_specs=in_specs + [cast.spec],
        out_specs=[half_out, hi_out, cast.spec],
        out_shape=[jax.ShapeDtypeStruct((b, s // 2, d), BF16)] * 2 + [cast.out_shape],
        scratch_shapes=[pltpu.VMEM((2 * tp, s), BF16)],
        compiler_params=pltpu.CompilerParams(
            dimension_semantics=("arbitrary", "arbitrary"), vmem_limit_bytes=vmem + 4 * MIB),
        name="seq_mix",
    )(u3, u3, u3, u3, u3, u3, u3, jnp.asarray(c0), jnp.asarray(s0),
      jnp.asarray(cph).reshape(n_steps, 1, s), jnp.asarray(sph).reshape(n_steps, 1, s),
      band, band, jnp.asarray(rev).astype(BF16),
      w_pool_bf16, pool_scale.reshape(1, POOL_WIDTH), cw, sw, cast.src)
    return lo, hi, cast_out.reshape(cast.shape)


def _out_proj_route_kernel(x_ref, mixed_lo_ref, mixed_hi_ref, wout_ref, g_ref, wr_ref, br_ref,
                           tri_ref, pick_ref, cast_src_ref, rows_ref, route_ref, counts_ref, cast_dst_ref,
                           carry_ref, x1_ref, *, tiles_per_seq, n_tiles):
    step = pl.program_id(0)

    @pl.when(step == 0)
    def _():
        carry_ref[...] = jnp.zeros_like(carry_ref)
        x1_ref[...] = jnp.zeros_like(x1_ref)

    _cast_chunks([cast_src_ref], [cast_dst_ref])

    x1 = x1_ref[...]
    rows_ref[:, :D_MODEL] = x1
    h2 = _rmsnorm(x1, g_ref[...])
    logits = _dot(h2.astype(BF16), wr_ref[...]) + br_ref[...]

    tile = jnp.minimum(step, n_tiles - 1)
    in_lo_half = (tile % tiles_per_seq) < tiles_per_seq // 2
    mixed = jnp.where(in_lo_half, mixed_lo_ref[...], mixed_hi_ref[...])
    x1_ref[...] = x_ref[...] + _dot(mixed, wout_ref[...])

    lane = lax.broadcasted_iota(jnp.int32, logits.shape, 1)
    neg = jnp.float32(-jnp.inf)
    big = jnp.int32(ROUTER_LANES)

    is_group = lane < N_EXPERT_GROUPS
    gl = jnp.where(is_group, logits, neg)
    gmax = jnp.max(gl, axis=-1, keepdims=True)
    gidx = jnp.min(jnp.where(gl == gmax, lane, big), axis=-1, keepdims=True)
    p_g = 1.0 / jnp.sum(jnp.exp(gl - gmax), axis=-1, keepdims=True)

    e_lane = lane - N_EXPERT_GROUPS
    in_group = (e_lane >= gidx * EXPERTS_PER_GROUP) & (e_lane < (gidx + 1) * EXPERTS_PER_GROUP)
    el = jnp.where(in_group, logits, neg)
    v1 = jnp.max(el, axis=-1, keepdims=True)
    i1 = jnp.min(jnp.where(el == v1, lane, big), axis=-1, keepdims=True)
    el2 = jnp.where(lane == i1, neg, el)
    v2 = jnp.max(el2, axis=-1, keepdims=True)
    i2 = jnp.min(jnp.where(el2 == v2, lane, big), axis=-1, keepdims=True)
    r = jnp.exp(v2 - v1)
    w1 = p_g / (1.0 + r)
    w2 = p_g * r / (1.0 + r)

    first_is_low = i1 < i2
    first_lane = N_EXPERT_GROUPS + gidx * EXPERTS_PER_GROUP
    la = jnp.where(first_is_low, i1, i2) - first_lane
    lb = jnp.where(first_is_low, i2, i1) - first_lane
    w_low = jnp.where(first_is_low, w1, w2)
    w_high = jnp.where(first_is_low, w2, w1)
    pair = jnp.where(la == 0, lb - 1, jnp.where(la == 1, 6 - lb, 5))
    slot_a_is_high = la == 2
    w_a = jnp.where(slot_a_is_high, w_high, w_low)
    w_b = jnp.where(slot_a_is_high, w_low, w_high)
    cls = gidx * PAIRS_PER_GROUP + pair
    rows_ref[:, D_MODEL:] = jnp.where(lane == 0, w_a, jnp.where(lane == 1, w_b, 0.0))

    onehot = jnp.where(lane == cls, 1.0, 0.0)
    before = _dot(tri_ref[...], onehot.astype(BF16)) + carry_ref[...]
    rank = jnp.sum(jnp.where(lane == cls, before, 0.0), axis=-1, keepdims=True)
    carry_ref[...] += jnp.sum(onehot, axis=0, keepdims=True) * jnp.where(step > 0, 1.0, 0.0)
    counts_ref[...] = carry_ref[...]

    rank_hi = jnp.floor(rank * (1.0 / RANK_RADIX))
    rank_lo = rank - rank_hi * RANK_RADIX
    digits = jnp.where(lane == 0, cls.astype(F32),
                       jnp.where(lane == 1, rank_hi, jnp.where(lane == 2, rank_lo, 0.0)))
    route_ref[...] = lax.dot_general(pick_ref[...], digits.astype(BF16), (((1,), (1,)), ((), ())),
                                     preferred_element_type=F32)


def _out_proj_route(x2, mixed_lo, mixed_hi, seq_len, w_out_bf16, g_ffn, wr, br, cast_weight):
    n, d = x2.shape
    tm = TOKEN_TILE
    n_tiles = n // tm
    tiles_per_seq = seq_len // tm
    half_tiles = tiles_per_seq // 2
    const = lambda shape: pl.BlockSpec(shape, lambda i: (0, 0))
    in_tile = lambda i: jnp.minimum(i, n_tiles - 1)
    out_tile = lambda i: jnp.maximum(i - 1, 0)
    lo_spec = pl.BlockSpec((tm, d), lambda i: (
        (in_tile(i) // tiles_per_seq) * half_tiles + jnp.minimum(in_tile(i) % tiles_per_seq, half_tiles - 1), 0))
    hi_spec = pl.BlockSpec((tm, d), lambda i: (
        (in_tile(i) // tiles_per_seq) * half_tiles + jnp.maximum(in_tile(i) % tiles_per_seq - half_tiles, 0), 0))
    tri = np.tril(np.ones((tm, tm), np.float32), -1)
    pick = np.eye(SUBLANES, ROUTER_LANES, dtype=np.float32)
    vmem = (2 * tm * d * 4 + 4 * tm * d * 2 + 2 * tm * ROW_WORDS * 4 + d * d * 2 + tm * d * 4
            + 2 * d * ROUTER_LANES * 2 + 2 * tm * tm * 2 + 3 * tm * d * 4)
    cast = _CastAlong(cast_weight, n_tiles, in_tile)
    vmem += cast.vmem_bytes
    rows, route, counts, cast_out = pl.pallas_call(
        functools.partial(_out_proj_route_kernel, tiles_per_seq=tiles_per_seq, n_tiles=n_tiles),
        grid=(n_tiles + 1,),
        in_specs=[pl.BlockSpec((tm, d), lambda i: (in_tile(i), 0)), lo_spec, hi_spec,
                  pl.BlockSpec((d, d), lambda i: (0, 0), pipeline_mode=pl.Buffered(1)),
                  const((1, d)), const((d, ROUTER_LANES)), const((1, ROUTER_LANES)),
                  const((tm, tm)), const((SUBLANES, ROUTER_LANES)), cast.spec],
        out_specs=[pl.BlockSpec((tm, ROW_WORDS), lambda i: (out_tile(i), 0)),
                   pl.BlockSpec((SUBLANES, tm), lambda i: (0, out_tile(i))),
                   const((1, ROUTER_LANES)), cast.spec],
        out_shape=[jax.ShapeDtypeStruct((n, ROW_WORDS), F32),
                   jax.ShapeDtypeStruct((SUBLANES, n), F32),
                   jax.ShapeDtypeStruct((1, ROUTER_LANES), F32), cast.out_shape],
        scratch_shapes=[pltpu.VMEM((1, ROUTER_LANES), F32), pltpu.VMEM((tm, d), F32)],
        compiler_params=pltpu.CompilerParams(
            dimension_semantics=("arbitrary",), vmem_limit_bytes=vmem + 4 * MIB),
        name="out_proj_route",
    )(x2, mixed_lo, mixed_hi, w_out_bf16, g_ffn.reshape(1, d), wr, br,
      jnp.asarray(tri).astype(BF16), jnp.asarray(pick).astype(BF16), cast.src)
    return rows, route, counts, cast_out.reshape(cast.shape)


def _routing_plan(route, counts, n):
    tm = EXPERT_TILE
    cnt = counts[0, :N_CLASSES].astype(jnp.int32)
    class_end = jnp.cumsum(cnt)
    class_start = class_end - cnt
    cls = route[0].astype(jnp.int32)
    rank = (route[1] * RANK_RADIX + route[2]).astype(jnp.int32)
    pos = class_start[cls] + rank

    n_items = n // tm + N_CLASSES
    item = jnp.arange(n_items, dtype=jnp.int32)

    def expand(items_per_class):
        item_end = jnp.cumsum(items_per_class)
        item_c = jnp.minimum(item, item_end[-1] - 1)
        item_cls = jnp.sum((item_c[:, None] >= item_end[None, :]).astype(jnp.int32), axis=1)
        return item_cls, item_c - (item_end - items_per_class)[item_cls], item_end[-1]

    in_cls, in_k, in_used = expand((cnt + tm - 1) // tm)
    in_start = jnp.where(in_k < cnt[in_cls] // tm, class_start[in_cls] + in_k * tm, class_end[in_cls] - tm)
    first_tile = class_start // tm
    al_cls, al_k, al_used = expand(jnp.where(cnt > 0, (class_end - 1) // tm - first_tile + 1, 0))
    al_start = (first_tile[al_cls] + al_k) * tm
    al_lo = jnp.clip(class_start[al_cls] - al_start, 0, tm)
    al_hi = jnp.clip(class_end[al_cls] - al_start, 0, tm)

    inside = jnp.all((cnt == 0) | (cnt >= tm))
    item_cls = jnp.where(inside, in_cls, al_cls)
    start = jnp.where(inside, in_start, al_start)
    lo = jnp.where(inside, 0, al_lo)
    hi = jnp.where(inside, tm, al_hi)
    n_used = jnp.where(inside, in_used, al_used).reshape(1)
    group_base = np.repeat(np.arange(N_EXPERT_GROUPS) * EXPERTS_PER_GROUP, PAIRS_PER_GROUP)
    class_ea = jnp.asarray(group_base + np.tile(PAIR_SLOT_A, N_EXPERT_GROUPS), jnp.int32)
    class_eb = jnp.asarray(group_base + np.tile(PAIR_SLOT_B, N_EXPERT_GROUPS), jnp.int32)
    return pos, start, class_ea[item_cls], class_eb[item_cls], lo, hi, n_used


def _row_copy(src_ref, src_row, dst_ref, dst_row, sem):
    return pltpu.make_async_copy(src_ref.at[pl.ds(src_row, 1), :], dst_ref.at[pl.ds(dst_row, 1), :], sem)


def _invert_permutation_kernel(pos_ref, inv_ref):
    def body(t, carry):
        inv_ref[pos_ref[t]] = t
        return carry

    lax.fori_loop(0, pos_ref.shape[0], body, 0, unroll=8)


def _invert_permutation(pos):
    return pl.pallas_call(
        _invert_permutation_kernel,
        in_specs=[pl.BlockSpec(memory_space=pltpu.SMEM)],
        out_specs=pl.BlockSpec(memory_space=pltpu.SMEM),
        out_shape=jax.ShapeDtypeStruct(pos.shape, jnp.int32),
        name="invert_permutation",
    )(pos)


def _expert_pair_kernel(start_ref, ea_ref, eb_ref, lo_ref, hi_ref, nused_ref, tok_ref,
                        rows_hbm_ref, gffn_ref, gfin_ref,
                        wga_ref, wua_ref, wda_ref, wgb_ref, wub_ref, wdb_ref, out_hbm_ref,
                        buf_ref, acc_ref, gather_sem, scatter_sem):
    del ea_ref, eb_ref
    tm = EXPERT_TILE
    j = pl.program_id(0)
    last = nused_ref[0] - 1
    slot = j % 2
    other = 1 - slot

    def start_gather(item, dst_slot, rows=range(tm)):
        base = start_ref[item]
        for r in rows:
            _row_copy(rows_hbm_ref, tok_ref[base + r], buf_ref.at[dst_slot], r,
                      gather_sem.at[dst_slot]).start(priority=r % DMA_QUEUES)

    def wait_gather(dst_slot):
        pltpu.make_async_copy(rows_hbm_ref.at[pl.ds(0, tm), :], buf_ref.at[dst_slot],
                              gather_sem.at[dst_slot]).wait()

    def start_scatter(item, src_slot, rows=range(tm)):
        base = start_ref[item]
        for r in rows:
            _row_copy(acc_ref.at[src_slot], r, out_hbm_ref, tok_ref[base + r],
                      scatter_sem).start(priority=r % DMA_QUEUES)

    def wait_scatter():
        pltpu.make_async_copy(acc_ref.at[0], out_hbm_ref.at[pl.ds(0, tm), :], scatter_sem).wait()

    @pl.when(j == 0)
    def _():
        acc_ref[...] = jnp.zeros_like(acc_ref)
        start_gather(0, 0)
        start_scatter(0, 1)

    @pl.when(j <= last)
    def _():
        wait_scatter()
        wait_gather(slot)
        prev_item = jnp.maximum(j - 1, 0)
        next_item = jnp.minimum(j + 1, last)

        def issue_copies(part):
            rows = range(part * tm // DMA_PARTS, (part + 1) * tm // DMA_PARTS)
            start_scatter(prev_item, other, rows)
            start_gather(next_item, other, rows)

        lo = lo_ref[j]
        hi = hi_ref[j]
        x1 = buf_ref[slot, :, :D_MODEL]
        h = _rmsnorm(x1, gffn_ref[...]).astype(BF16)

        part = 0
        ff_chunk = EXPERT_FF * 4 // DMA_PARTS
        for slot_lane, (wg_ref, wu_ref, wd_ref) in enumerate(((wga_ref, wua_ref, wda_ref),
                                                              (wgb_ref, wub_ref, wdb_ref))):
            for c0 in range(0, EXPERT_FF, ff_chunk):
                issue_copies(part)
                wts = buf_ref[slot, :, D_MODEL:]
                lane = lax.broadcasted_iota(jnp.int32, wts.shape, 1)
                w = jnp.sum(jnp.where(lane == slot_lane, wts, 0.0), axis=-1, keepdims=True)
                a = _dot(h, wg_ref[0, :, c0:c0 + ff_chunk])
                v = _dot(h, wu_ref[0, :, c0:c0 + ff_chunk])
                act = (a * (1.0 / (1.0 + jnp.exp(-a))) * v * w).astype(BF16)
                for half, d0 in enumerate(range(0, D_MODEL, D_MODEL // 2)):
                    cols = slice(d0, d0 + D_MODEL // 2)
                    if half == 1:
                        issue_copies(part + 1)
                    y_part = _dot(act, wd_ref[0, c0:c0 + ff_chunk, cols])
                    if part == 0:
                        acc_ref[slot, :, cols] = y_part
                    else:
                        acc_ref[slot, :, cols] += y_part
                part += 2
        res = _rmsnorm(buf_ref[slot, :, :D_MODEL] + acc_ref[slot], gfin_ref[...])

        row = lax.broadcasted_iota(jnp.int32, (tm, 1), 0)
        mine = (row >= lo) & (row < hi)
        acc_ref[slot] = jnp.where(mine, res, jnp.where(lo > 0, acc_ref[other], 0.0))

    @pl.when(j == last)
    def _():
        wait_scatter()
        start_scatter(j, slot)
        wait_scatter()
        wait_gather(other)


def _expert_pairs(item_start, item_ea, item_eb, item_lo, item_hi, n_used, slot_token, rows,
                  g_ffn, g_final, wg, wu, wd):
    n, w = rows.shape
    d, f = D_MODEL, EXPERT_FF
    tm = EXPERT_TILE
    gate_a = pl.BlockSpec((1, d, f), lambda j, st, ea, eb, lo, hi, nu, tok: (ea[j], 0, 0))
    gate_b = pl.BlockSpec((1, d, f), lambda j, st, ea, eb, lo, hi, nu, tok: (eb[j], 0, 0))
    down_a = pl.BlockSpec((1, f, d), lambda j, st, ea, eb, lo, hi, nu, tok: (ea[j], 0, 0))
    down_b = pl.BlockSpec((1, f, d), lambda j, st, ea, eb, lo, hi, nu, tok: (eb[j], 0, 0))
    gain = pl.BlockSpec((1, d), lambda j, st, ea, eb, lo, hi, nu, tok: (0, 0))
    grid_spec = pltpu.PrefetchScalarGridSpec(
        num_scalar_prefetch=7,
        grid=(item_start.shape[0],),
        in_specs=[pl.BlockSpec(memory_space=pl.ANY), gain, gain,
                  gate_a, gate_a, down_a, gate_b, gate_b, down_b],
        out_specs=pl.BlockSpec(memory_space=pl.ANY),
        scratch_shapes=[pltpu.VMEM((2, tm, w), F32), pltpu.VMEM((2, tm, d), F32),
                        pltpu.SemaphoreType.DMA((2,)), pltpu.SemaphoreType.DMA(())],
    )
    vmem = 2 * 6 * d * f * 2 + 2 * tm * w * 4 + 2 * tm * d * 4 + 8 * tm * d * 4
    return pl.pallas_call(
        _expert_pair_kernel,
        grid_spec=grid_spec,
        out_shape=jax.ShapeDtypeStruct((n, d), F32),
        compiler_params=pltpu.CompilerParams(
            dimension_semantics=("arbitrary",), vmem_limit_bytes=vmem + 4 * MIB),
        name="expert_pairs",
    )(item_start, item_ea, item_eb, item_lo, item_hi, n_used, slot_token, rows, g_ffn.reshape(1, d),
      g_final.reshape(1, d), wg, wu, wd, wg, wu, wd)


def kernel(x, g_mix, w_in, w_pool, pool_scale, w_fourier, w_out, g_ffn, w_group_router,
           b_group_router, w_expert_router, b_expert_router, w_gate, w_up, w_down, g_final):
    b, s, d = x.shape
    assert d == D_MODEL and s % (2 * SEQ_TILE) == 0 and s % (2 * TOKEN_TILE) == 0
    assert (b * s) % max(TOKEN_TILE, EXPERT_TILE) == 0
    n = b * s
    x2 = x.reshape(n, d)

    cw, sw = _fourier_weights(w_fourier, s)
    u, (w_out_bf16, w_gate_bf16) = _norm_proj(x2, g_mix, w_in.astype(BF16), [w_out, w_gate])
    mixed_lo, mixed_hi, w_up_bf16 = _mix(u.reshape(b, s, d), w_pool.astype(BF16), pool_scale, cw, sw, w_up)

    wr = jnp.concatenate([w_group_router, w_expert_router], axis=1)
    wr = jnp.pad(wr, ((0, 0), (0, ROUTER_LANES - wr.shape[1]))).astype(BF16)
    br = jnp.concatenate([b_group_router, b_expert_router])
    br = jnp.pad(br, (0, ROUTER_LANES - br.shape[0])).reshape(1, ROUTER_LANES)

    rows, route, counts, w_down_bf16 = _out_proj_route(
        x2, mixed_lo.reshape(n // 2, d), mixed_hi.reshape(n // 2, d), s, w_out_bf16, g_ffn, wr, br, w_down)
    pos, item_start, item_ea, item_eb, item_lo, item_hi, n_used = _routing_plan(route, counts, n)
    slot_token = _invert_permutation(pos)
    out = _expert_pairs(item_start, item_ea, item_eb, item_lo, item_hi, n_used, slot_token, rows,
                        g_ffn, g_final, w_gate_bf16, w_up_bf16, w_down_bf16)
    return out.reshape(b, s, d)
```

```python
import functools

import numpy as np
import jax
import jax.numpy as jnp
from jax import lax
from jax.experimental import pallas as pl
from jax.experimental.pallas import tpu as pltpu

D_MODEL = 2048
POOL_WINDOWS = (2, 4, 8, 16)
N_POOL_GROUPS = len(POOL_WINDOWS)
POOL_WIDTH = D_MODEL // 2
POOL_GROUP_DIM = POOL_WIDTH // N_POOL_GROUPS
FOURIER_WIDTH = D_MODEL - POOL_WIDTH
N_FOURIER_HEADS = 4
FOURIER_HEAD_DIM = FOURIER_WIDTH // N_FOURIER_HEADS
N_EXPERT_GROUPS = 4
EXPERTS_PER_GROUP = 4
N_EXPERTS = N_EXPERT_GROUPS * EXPERTS_PER_GROUP
EXPERT_FF = D_MODEL // 4
RMS_EPS = 1e-6

LANES = 128
SUBLANES = 8
BF16_SUBLANES = 16
ROUTER_LANES = LANES
MIB = 1024 * 1024

TOKEN_TILE = 512
SEQ_TILE = 256
POOL_HALO = BF16_SUBLANES
DFT_ROWS = SEQ_TILE + BF16_SUBLANES
TWIDDLE_ROWS = 16

PAIR_SLOT_A = (0, 0, 0, 1, 1, 3)
PAIR_SLOT_B = (1, 2, 3, 3, 2, 2)
PAIRS_PER_GROUP = len(PAIR_SLOT_A)
N_CLASSES = N_EXPERT_GROUPS * PAIRS_PER_GROUP
ROW_WORDS = D_MODEL + LANES
RANK_RADIX = 128
EXPERT_TILE = 256
DMA_PARTS = 8

BF16 = jnp.bfloat16
F32 = jnp.float32


def _rmsnorm(x, g):
    ms = jnp.mean(x * x, axis=-1, keepdims=True)
    return x * lax.rsqrt(ms + RMS_EPS) * g


def _dot(a, b):
    return jnp.dot(a, b, preferred_element_type=F32)


def _twiddle(rows, cols, period):
    m = (np.asarray(rows, np.int64)[:, None] * np.asarray(cols, np.int64)[None, :]) % period
    ang = (2.0 * np.pi / period) * m.astype(np.float64)
    return np.cos(ang).astype(np.float32), np.sin(ang).astype(np.float32)


def _pool_band(seq_len, tile, halo):
    n_tiles = seq_len // tile
    out = np.zeros((3, N_POOL_GROUPS, tile, tile + 2 * halo), np.float64)
    for v, m in enumerate((0, 1, n_tiles - 1)):
        t0 = m * tile
        for g, k in enumerate(POOL_WINDOWS):
            for r in range(tile):
                t = t0 + r
                lo = max(t - (k - 1) // 2, 0)
                hi = min(t + k // 2 + 1, seq_len)
                out[v, g, r, lo - t0 + halo:hi - t0 + halo] = 1.0 / (hi - lo)
                out[v, g, r, r + halo] -= 1.0
    return out.astype(np.float32)


def _mixer_weight_kernel(cd_ref, sd_ref, wf_ref, wp_ref, ps_ref, wo_ref, cw_ref, sw_ref, wtop_ref, *, scale):
    wf = wf_ref[0]
    cw = jnp.dot(cd_ref[...], wf, preferred_element_type=F32, precision=lax.Precision.HIGHEST)
    sw = jnp.dot(sd_ref[...], wf, preferred_element_type=F32, precision=lax.Precision.HIGHEST)
    cw_ref[0] = (cw * scale).astype(BF16)
    sw_ref[0] = (sw * (-scale)).astype(BF16)
    wtop_ref[...] = _dot((wp_ref[0] * ps_ref[0]).astype(BF16), wo_ref[...].astype(BF16)).astype(BF16)


def _mixer_weights(w_fourier, w_pool, pool_scale, w_out, seq_len):
    dh = FOURIER_HEAD_DIM
    assert N_FOURIER_HEADS == N_POOL_GROUPS and dh == POOL_GROUP_DIM
    d = w_out.shape[1]
    cd, sd = _twiddle(np.arange(dh), np.arange(dh), dh)
    scale = 1.0 / np.sqrt(float(seq_len * dh))
    mat = pl.BlockSpec((dh, dh), lambda h: (0, 0))
    per_head = pl.BlockSpec((1, dh, dh), lambda h: (h, 0, 0))
    out_rows = pl.BlockSpec((dh, d), lambda h: (h, 0))
    return pl.pallas_call(
        functools.partial(_mixer_weight_kernel, scale=scale),
        grid=(N_FOURIER_HEADS,),
        in_specs=[mat, mat, per_head, per_head, pl.BlockSpec((1, 1, dh), lambda h: (h, 0, 0)), out_rows],
        out_specs=[per_head, per_head, out_rows],
        out_shape=[jax.ShapeDtypeStruct((N_FOURIER_HEADS, dh, dh), BF16)] * 2
        + [jax.ShapeDtypeStruct((POOL_WIDTH, d), BF16)],
        name="mixer_weights",
    )(jnp.asarray(cd), jnp.asarray(sd), w_fourier, w_pool, pool_scale.reshape(N_POOL_GROUPS, 1, dh), w_out)


class _CastAlong:
    def __init__(self, w, n_chunks, chunk_of):
        cols = w.shape[-1]
        rows = w.size // cols // n_chunks
        self.shape = w.shape
        self.src = w.reshape(n_chunks, rows, cols)
        self.spec = pl.BlockSpec((1, rows, cols), lambda *idx: (chunk_of(*idx), 0, 0))
        self.out_shape = jax.ShapeDtypeStruct((n_chunks, rows, cols), BF16)
        self.vmem_bytes = 2 * rows * cols * (4 + 2)


def _cast_chunks(src_refs, dst_refs):
    for src_ref, dst_ref in zip(src_refs, dst_refs):
        dst_ref[...] = src_ref[...].astype(BF16)


def _norm_proj_kernel(x_ref, g_ref, w_ref, *rest, n_cast):
    cast_src, (u_ref, *cast_dst) = rest[:n_cast], rest[n_cast:]
    h = _rmsnorm(x_ref[...], g_ref[...])
    u_ref[...] = _dot(h.astype(BF16), w_ref[...]).astype(BF16)
    _cast_chunks(cast_src, cast_dst)


def _norm_proj(x2, g_mix, w_in_bf16, cast_weights):
    n, d = x2.shape
    tm = TOKEN_TILE
    casts = [_CastAlong(w, n // tm, lambda i: i) for w in cast_weights]
    vmem = (2 * tm * d * 4 + d * d * 2 + 2 * tm * d * 2 + 3 * tm * d * 4
            + sum(c.vmem_bytes for c in casts))
    u, *cast_out = pl.pallas_call(
        functools.partial(_norm_proj_kernel, n_cast=len(casts)),
        grid=(n // tm,),
        in_specs=[
            pl.BlockSpec((tm, d), lambda i: (i, 0)),
            pl.BlockSpec((1, d), lambda i: (0, 0)),
            pl.BlockSpec((d, d), lambda i: (0, 0), pipeline_mode=pl.Buffered(1)),
        ] + [c.spec for c in casts],
        out_specs=[pl.BlockSpec((tm, d), lambda i: (i, 0))] + [c.spec for c in casts],
        out_shape=[jax.ShapeDtypeStruct((n, d), BF16)] + [c.out_shape for c in casts],
        compiler_params=pltpu.CompilerParams(
            dimension_semantics=("arbitrary",), vmem_limit_bytes=vmem + 4 * MIB),
        name="norm_proj",
    )(x2, g_mix.reshape(1, d), w_in_bf16, *[c.src for c in casts])
    return u, [o.reshape(c.shape) for o, c in zip(cast_out, casts)]


def _mix_kernel(uf_ref, up_lo_ref, prev_lo_ref, next_lo_ref, up_hi_ref, prev_hi_ref, next_hi_ref,
                c0_ref, s0_ref, cph_ref, sph_ref, band_lo_ref, band_hi_ref, rev_ref,
                cw_ref, sw_ref, cast_src_ref, lo_ref, hi_ref, cast_dst_ref, lhs_ref):
    t = SEQ_TILE
    tp = DFT_ROWS
    gd = POOL_GROUP_DIM
    hd = FOURIER_HEAD_DIM

    @pl.when(pl.program_id(1) == 0)
    def _():
        cph = cph_ref[0]
        sph = sph_ref[0]
        for r0 in range(0, tp, TWIDDLE_ROWS):
            c0 = c0_ref[r0:r0 + TWIDDLE_ROWS, :]
            s0 = s0_ref[r0:r0 + TWIDDLE_ROWS, :]
            lhs_ref[r0:r0 + TWIDDLE_ROWS, :] = (cph * c0 - sph * s0).astype(BF16)
            lhs_ref[tp + r0:tp + r0 + TWIDDLE_ROWS, :] = (sph * c0 + cph * s0).astype(BF16)

    _cast_chunks([cast_src_ref], [cast_dst_ref])
    pq = _dot(lhs_ref[...], uf_ref[...])
    for h in range(N_FOURIER_HEADS):
        cols = slice(h * hd, (h + 1) * hd)
        out_cols = slice(POOL_WIDTH + h * hd, POOL_WIDTH + (h + 1) * hd)
        pc = _dot(pq[:tp, cols].astype(BF16), cw_ref[h])
        qs = _dot(pq[tp:, cols].astype(BF16), sw_ref[h])
        lo_ref[:, out_cols] = (pc + qs)[:t].astype(BF16)
        hi_ref[:, out_cols] = _dot(rev_ref[...], (pc - qs).astype(BF16)).astype(BF16)

    def pool(up_ref, prev_ref, next_ref, band_ref, out_ref):
        win = jnp.concatenate([prev_ref[...], up_ref[...], next_ref[...]], axis=0)
        for g in range(N_POOL_GROUPS):
            cols = slice(g * gd, (g + 1) * gd)
            out_ref[:, cols] = _dot(band_ref[0, g], win[:, cols]).astype(BF16)

    pool(up_lo_ref, prev_lo_ref, next_lo_ref, band_lo_ref, lo_ref)
    pool(up_hi_ref, prev_hi_ref, next_hi_ref, band_hi_ref, hi_ref)


def _mix(u3, cw, sw, cast_weight):
    b, s, d = u3.shape
    t = SEQ_TILE
    tp = DFT_ROWS
    halo = POOL_HALO
    n_tiles = s // t
    n_steps = n_tiles // 2
    halo_blocks_per_tile = t // halo
    last_halo_block = s // halo - 1

    c0, s0 = _twiddle(np.arange(tp), np.arange(s), s)
    cph, sph = _twiddle(np.arange(0, s // 2, t), np.arange(s), s)
    band = jnp.asarray(_pool_band(s, t, halo)).astype(BF16)
    rev = np.zeros((t, tp), np.float32)
    rev[np.arange(t), t - np.arange(t)] = 1.0

    hi_tile = lambda m: n_tiles - 1 - m
    tile_spec = lambda tile_of: pl.BlockSpec((None, t, POOL_WIDTH), lambda m, bi: (bi, tile_of(m), 0))
    prev_spec = lambda tile_of: pl.BlockSpec(
        (None, halo, POOL_WIDTH),
        lambda m, bi: (bi, jnp.maximum(tile_of(m) * halo_blocks_per_tile - 1, 0), 0))
    next_spec = lambda tile_of: pl.BlockSpec(
        (None, halo, POOL_WIDTH),
        lambda m, bi: (bi, jnp.minimum((tile_of(m) + 1) * halo_blocks_per_tile, last_halo_block), 0))
    band_shape = (1, N_POOL_GROUPS, t, t + 2 * halo)
    const2 = lambda shape: pl.BlockSpec(shape, lambda m, bi: (0, 0))
    const3 = lambda shape: pl.BlockSpec(shape, lambda m, bi: (0, 0, 0))
    lo_tile = lambda m: m

    in_specs = [
        pl.BlockSpec((None, s, FOURIER_WIDTH), lambda m, bi: (bi, 0, 1)),
        tile_spec(lo_tile), prev_spec(lo_tile), next_spec(lo_tile),
        tile_spec(hi_tile), prev_spec(hi_tile), next_spec(hi_tile),
        pl.BlockSpec((tp, s), lambda m, bi: (0, 0), pipeline_mode=pl.Buffered(1)),
        pl.BlockSpec((tp, s), lambda m, bi: (0, 0), pipeline_mode=pl.Buffered(1)),
        pl.BlockSpec((1, 1, s), lambda m, bi: (m, 0, 0)),
        pl.BlockSpec((1, 1, s), lambda m, bi: (m, 0, 0)),
        pl.BlockSpec(band_shape, lambda m, bi: (jnp.where(m == 0, 0, 1), 0, 0, 0)),
        pl.BlockSpec(band_shape, lambda m, bi: (jnp.where(m == 0, 2, 1), 0, 0, 0)),
        const2((t, tp)),
        const3((N_FOURIER_HEADS, FOURIER_HEAD_DIM, FOURIER_HEAD_DIM)),
        const3((N_FOURIER_HEADS, FOURIER_HEAD_DIM, FOURIER_HEAD_DIM)),
    ]
    half_out = pl.BlockSpec((None, t, d), lambda m, bi: (bi, m, 0))
    hi_out = pl.BlockSpec((None, t, d), lambda m, bi: (bi, n_steps - 1 - m, 0))
    cast = _CastAlong(cast_weight, n_steps * b, lambda m, bi: m * b + bi)
    vmem = (2 * s * FOURIER_WIDTH * 2
            + 2 * tp * s * 4
            + 2 * tp * s * 2
            + 2 * 2 * t * d * 2 * 2
            + 4 * 2 * tp * FOURIER_WIDTH * 4
            + cast.vmem_bytes)
    lo, hi, cast_out = pl.pallas_call(
        _mix_kernel,
        grid=(n_steps, b),
        in_specs=in_specs + [cast.spec],
        out_specs=[half_out, hi_out, cast.spec],
        out_shape=[jax.ShapeDtypeStruct((b, s // 2, d), BF16)] * 2 + [cast.out_shape],
        scratch_shapes=[pltpu.VMEM((2 * tp, s), BF16)],
        compiler_params=pltpu.CompilerParams(
            dimension_semantics=("arbitrary", "arbitrary"), vmem_limit_bytes=vmem + 4 * MIB),
        name="seq_mix",
    )(u3, u3, u3, u3, u3, u3, u3, jnp.asarray(c0), jnp.asarray(s0),
      jnp.asarray(cph).reshape(n_steps, 1, s), jnp.asarray(sph).reshape(n_steps, 1, s),
      band, band, jnp.asarray(rev).astype(BF16),
      cw, sw, cast.src)
    return lo, hi, cast_out.reshape(cast.shape)


def _out_proj_route_kernel(x_ref, mixed_lo_ref, mixed_hi_ref, wtop_ref, wbot_ref, g_ref, wr_ref, br_ref,
                           tri_ref, pick_ref, cast_src_ref, rows_ref, route_ref, counts_ref, cast_dst_ref,
                           carry_ref, x1_ref, *, tiles_per_seq, n_tiles):
    step = pl.program_id(0)

    @pl.when(step == 0)
    def _():
        carry_ref[...] = jnp.zeros_like(carry_ref)
        x1_ref[...] = jnp.zeros_like(x1_ref)

    _cast_chunks([cast_src_ref], [cast_dst_ref])

    x1 = x1_ref[...]
    rows_ref[:, :D_MODEL] = x1
    h2 = _rmsnorm(x1, g_ref[...])
    logits = _dot(h2.astype(BF16), wr_ref[...]) + br_ref[...]

    tile = jnp.minimum(step, n_tiles - 1)
    in_lo_half = (tile % tiles_per_seq) < tiles_per_seq // 2
    mixed = jnp.where(in_lo_half, mixed_lo_ref[...], mixed_hi_ref[...])
    x1_ref[...] = (x_ref[...] + _dot(mixed[:, :POOL_WIDTH], wtop_ref[...])
                   + _dot(mixed[:, POOL_WIDTH:], wbot_ref[...]))

    lane = lax.broadcasted_iota(jnp.int32, logits.shape, 1)
    neg = jnp.float32(-jnp.inf)
    big = jnp.int32(ROUTER_LANES)

    is_group = lane < N_EXPERT_GROUPS
    gl = jnp.where(is_group, logits, neg)
    gmax = jnp.max(gl, axis=-1, keepdims=True)
    gidx = jnp.min(jnp.where(gl == gmax, lane, big), axis=-1, keepdims=True)
    p_g = 1.0 / jnp.sum(jnp.exp(gl - gmax), axis=-1, keepdims=True)

    e_lane = lane - N_EXPERT_GROUPS
    in_group = (e_lane >= gidx * EXPERTS_PER_GROUP) & (e_lane < (gidx + 1) * EXPERTS_PER_GROUP)
    el = jnp.where(in_group, logits, neg)
    v1 = jnp.max(el, axis=-1, keepdims=True)
    i1 = jnp.min(jnp.where(el == v1, lane, big), axis=-1, keepdims=True)
    el2 = jnp.where(lane == i1, neg, el)
    v2 = jnp.max(el2, axis=-1, keepdims=True)
    i2 = jnp.min(jnp.where(el2 == v2, lane, big), axis=-1, keepdims=True)
    r = jnp.exp(v2 - v1)
    w1 = p_g / (1.0 + r)
    w2 = p_g * r / (1.0 + r)

    first_is_low = i1 < i2
    first_lane = N_EXPERT_GROUPS + gidx * EXPERTS_PER_GROUP
    la = jnp.where(first_is_low, i1, i2) - first_lane
    lb = jnp.where(first_is_low, i2, i1) - first_lane
    w_low = jnp.where(first_is_low, w1, w2)
    w_high = jnp.where(first_is_low, w2, w1)
    pair = jnp.where(la == 0, lb - 1, jnp.where(la == 1, 6 - lb, 5))
    slot_a_is_high = la == 2
    w_a = jnp.where(slot_a_is_high, w_high, w_low)
    w_b = jnp.where(slot_a_is_high, w_low, w_high)
    cls = gidx * PAIRS_PER_GROUP + pair
    rows_ref[:, D_MODEL:] = jnp.where(lane == 0, w_a, jnp.where(lane == 1, w_b, 0.0))

    onehot = jnp.where(lane == cls, 1.0, 0.0)
    before = _dot(tri_ref[...], onehot.astype(BF16)) + carry_ref[...]
    rank = jnp.sum(jnp.where(lane == cls, before, 0.0), axis=-1, keepdims=True)
    carry_ref[...] += jnp.sum(onehot, axis=0, keepdims=True) * jnp.where(step > 0, 1.0, 0.0)
    counts_ref[...] = carry_ref[...]

    rank_hi = jnp.floor(rank * (1.0 / RANK_RADIX))
    rank_lo = rank - rank_hi * RANK_RADIX
    digits = jnp.where(lane == 0, cls.astype(F32),
                       jnp.where(lane == 1, rank_hi, jnp.where(lane == 2, rank_lo, 0.0)))
    route_ref[...] = lax.dot_general(pick_ref[...], digits.astype(BF16), (((1,), (1,)), ((), ())),
                                     preferred_element_type=F32)


def _out_proj_route(x2, mixed_lo, mixed_hi, seq_len, w_top_bf16, w_out_bf16, g_ffn, wr, br, cast_weight):
    n, d = x2.shape
    tm = TOKEN_TILE
    n_tiles = n // tm
    tiles_per_seq = seq_len // tm
    half_tiles = tiles_per_seq // 2
    const = lambda shape: pl.BlockSpec(shape, lambda i: (0, 0))
    in_tile = lambda i: jnp.minimum(i, n_tiles - 1)
    out_tile = lambda i: jnp.maximum(i - 1, 0)
    lo_spec = pl.BlockSpec((tm, d), lambda i: (
        (in_tile(i) // tiles_per_seq) * half_tiles + jnp.minimum(in_tile(i) % tiles_per_seq, half_tiles - 1), 0))
    hi_spec = pl.BlockSpec((tm, d), lambda i: (
        (in_tile(i) // tiles_per_seq) * half_tiles + jnp.maximum(in_tile(i) % tiles_per_seq - half_tiles, 0), 0))
    tri = np.tril(np.ones((tm, tm), np.float32), -1)
    pick = np.eye(SUBLANES, ROUTER_LANES, dtype=np.float32)
    vmem = (2 * tm * d * 4 + 4 * tm * d * 2 + 2 * tm * ROW_WORDS * 4 + d * d * 2 + tm * d * 4
            + 2 * d * ROUTER_LANES * 2 + 2 * tm * tm * 2 + 3 * tm * d * 4)
    cast = _CastAlong(cast_weight, n_tiles, in_tile)
    vmem += cast.vmem_bytes
    rows, route, counts, cast_out = pl.pallas_call(
        functools.partial(_out_proj_route_kernel, tiles_per_seq=tiles_per_seq, n_tiles=n_tiles),
        grid=(n_tiles + 1,),
        in_specs=[pl.BlockSpec((tm, d), lambda i: (in_tile(i), 0)), lo_spec, hi_spec,
                  pl.BlockSpec((POOL_WIDTH, d), lambda i: (0, 0), pipeline_mode=pl.Buffered(1)),
                  pl.BlockSpec((FOURIER_WIDTH, d), lambda i: (1, 0), pipeline_mode=pl.Buffered(1)),
                  const((1, d)), const((d, ROUTER_LANES)), const((1, ROUTER_LANES)),
                  const((tm, tm)), const((SUBLANES, ROUTER_LANES)), cast.spec],
        out_specs=[pl.BlockSpec((tm, ROW_WORDS), lambda i: (out_tile(i), 0)),
                   pl.BlockSpec((SUBLANES, tm), lambda i: (0, out_tile(i))),
                   const((1, ROUTER_LANES)), cast.spec],
        out_shape=[jax.ShapeDtypeStruct((n, ROW_WORDS), F32),
                   jax.ShapeDtypeStruct((SUBLANES, n), F32),
                   jax.ShapeDtypeStruct((1, ROUTER_LANES), F32), cast.out_shape],
        scratch_shapes=[pltpu.VMEM((1, ROUTER_LANES), F32), pltpu.VMEM((tm, d), F32)],
        compiler_params=pltpu.CompilerParams(
            dimension_semantics=("arbitrary",), vmem_limit_bytes=vmem + 4 * MIB),
        name="out_proj_route",
    )(x2, mixed_lo, mixed_hi, w_top_bf16, w_out_bf16, g_ffn.reshape(1, d), wr, br,
      jnp.asarray(tri).astype(BF16), jnp.asarray(pick).astype(BF16), cast.src)
    return rows, route, counts, cast_out.reshape(cast.shape)


def _routing_plan(route, counts, n):
    tm = EXPERT_TILE
    cnt = counts[0, :N_CLASSES].astype(jnp.int32)
    class_end = jnp.cumsum(cnt)
    class_start = class_end - cnt
    cls = route[0].astype(jnp.int32)
    rank = (route[1] * RANK_RADIX + route[2]).astype(jnp.int32)
    pos = class_start[cls] + rank

    n_items = n // tm + N_CLASSES
    item = jnp.arange(n_items, dtype=jnp.int32)

    def expand(items_per_class):
        item_end = jnp.cumsum(items_per_class)
        item_c = jnp.minimum(item, item_end[-1] - 1)
        item_cls = jnp.sum((item_c[:, None] >= item_end[None, :]).astype(jnp.int32), axis=1)
        return item_cls, item_c - (item_end - items_per_class)[item_cls], item_end[-1]

    in_cls, in_k, in_used = expand((cnt + tm - 1) // tm)
    in_start = jnp.where(in_k < cnt[in_cls] // tm, class_start[in_cls] + in_k * tm, class_end[in_cls] - tm)
    first_tile = class_start // tm
    al_cls, al_k, al_used = expand(jnp.where(cnt > 0, (class_end - 1) // tm - first_tile + 1, 0))
    al_start = (first_tile[al_cls] + al_k) * tm
    al_lo = jnp.clip(class_start[al_cls] - al_start, 0, tm)
    al_hi = jnp.clip(class_end[al_cls] - al_start, 0, tm)

    inside = jnp.all((cnt == 0) | (cnt >= tm))
    item_cls = jnp.where(inside, in_cls, al_cls)
    start = jnp.where(inside, in_start, al_start)
    lo = jnp.where(inside, 0, al_lo)
    hi = jnp.where(inside, tm, al_hi)
    n_used = jnp.where(inside, in_used, al_used).reshape(1)
    group_base = np.repeat(np.arange(N_EXPERT_GROUPS) * EXPERTS_PER_GROUP, PAIRS_PER_GROUP)
    class_ea = jnp.asarray(group_base + np.tile(PAIR_SLOT_A, N_EXPERT_GROUPS), jnp.int32)
    class_eb = jnp.asarray(group_base + np.tile(PAIR_SLOT_B, N_EXPERT_GROUPS), jnp.int32)
    return pos, start, class_ea[item_cls], class_eb[item_cls], lo, hi, n_used


def _row_copy(src_ref, src_row, dst_ref, dst_row, sem):
    return pltpu.make_async_copy(src_ref.at[pl.ds(src_row, 1), :], dst_ref.at[pl.ds(dst_row, 1), :], sem)


def _invert_permutation_kernel(pos_ref, inv_ref):
    def body(t, carry):
        inv_ref[pos_ref[t]] = t
        return carry

    lax.fori_loop(0, pos_ref.shape[0], body, 0, unroll=8)


def _invert_permutation(pos):
    return pl.pallas_call(
        _invert_permutation_kernel,
        in_specs=[pl.BlockSpec(memory_space=pltpu.SMEM)],
        out_specs=pl.BlockSpec(memory_space=pltpu.SMEM),
        out_shape=jax.ShapeDtypeStruct(pos.shape, jnp.int32),
        name="invert_permutation",
    )(pos)


def _expert_pair_kernel(start_ref, ea_ref, eb_ref, lo_ref, hi_ref, nused_ref, tok_ref,
                        rows_hbm_ref, gffn_ref, gfin_ref,
                        wga_ref, wua_ref, wda_ref, wgb_ref, wub_ref, wdb_ref, out_hbm_ref,
                        buf_ref, acc_ref, gather_sem, scatter_sem):
    del ea_ref, eb_ref
    tm = EXPERT_TILE
    j = pl.program_id(0)
    last = nused_ref[0] - 1
    slot = j % 2
    other = 1 - slot

    def start_gather(item, dst_slot, rows=range(tm)):
        base = start_ref[item]
        for r in rows:
            _row_copy(rows_hbm_ref, tok_ref[base + r], buf_ref.at[dst_slot], r, gather_sem.at[dst_slot]).start()

    def wait_gather(dst_slot):
        pltpu.make_async_copy(rows_hbm_ref.at[pl.ds(0, tm), :], buf_ref.at[dst_slot],
                              gather_sem.at[dst_slot]).wait()

    def start_scatter(item, src_slot, rows=range(tm)):
        base = start_ref[item]
        for r in rows:
            _row_copy(acc_ref.at[src_slot], r, out_hbm_ref, tok_ref[base + r], scatter_sem).start()

    def wait_scatter():
        pltpu.make_async_copy(acc_ref.at[0], out_hbm_ref.at[pl.ds(0, tm), :], scatter_sem).wait()

    @pl.when(j == 0)
    def _():
        acc_ref[...] = jnp.zeros_like(acc_ref)
        start_gather(0, 0)
        start_scatter(0, 1)

    @pl.when(j <= last)
    def _():
        wait_scatter()
        wait_gather(slot)
        prev_item = jnp.maximum(j - 1, 0)
        next_item = jnp.minimum(j + 1, last)

        def issue_copies(part):
            rows = range(part * tm // DMA_PARTS, (part + 1) * tm // DMA_PARTS)
            start_scatter(prev_item, other, rows)
            start_gather(next_item, other, rows)

        lo = lo_ref[j]
        hi = hi_ref[j]
        x1 = buf_ref[slot, :, :D_MODEL]
        h = _rmsnorm(x1, gffn_ref[...]).astype(BF16)

        part = 0
        ff_chunk = EXPERT_FF * 4 // DMA_PARTS
        for slot_lane, (wg_ref, wu_ref, wd_ref) in enumerate(((wga_ref, wua_ref, wda_ref),
                                                              (wgb_ref, wub_ref, wdb_ref))):
            for c0 in range(0, EXPERT_FF, ff_chunk):
                issue_copies(part)
                wts = buf_ref[slot, :, D_MODEL:]
                lane = lax.broadcasted_iota(jnp.int32, wts.shape, 1)
                w = jnp.sum(jnp.where(lane == slot_lane, wts, 0.0), axis=-1, keepdims=True)
                a = _dot(h, wg_ref[0, :, c0:c0 + ff_chunk])
                v = _dot(h, wu_ref[0, :, c0:c0 + ff_chunk])
                act = (a * (1.0 / (1.0 + jnp.exp(-a))) * v * w).astype(BF16)
                for half, d0 in enumerate(range(0, D_MODEL, D_MODEL // 2)):
                    cols = slice(d0, d0 + D_MODEL // 2)
                    if half == 1:
                        issue_copies(part + 1)
                    y_part = _dot(act, wd_ref[0, c0:c0 + ff_chunk, cols])
                    if part == 0:
                        acc_ref[slot, :, cols] = y_part
                    else:
                        acc_ref[slot, :, cols] += y_part
                part += 2
        res = _rmsnorm(buf_ref[slot, :, :D_MODEL] + acc_ref[slot], gfin_ref[...])

        row = lax.broadcasted_iota(jnp.int32, (tm, 1), 0)
        mine = (row >= lo) & (row < hi)
        acc_ref[slot] = jnp.where(mine, res, jnp.where(lo > 0, acc_ref[other], 0.0))

    @pl.when(j == last)
    def _():
        wait_scatter()
        start_scatter(j, slot)
        wait_scatter()
        wait_gather(other)


def _expert_pairs(item_start, item_ea, item_eb, item_lo, item_hi, n_used, slot_token, rows,
                  g_ffn, g_final, wg, wu, wd):
    n, w = rows.shape
    d, f = D_MODEL, EXPERT_FF
    tm = EXPERT_TILE
    gate_a = pl.BlockSpec((1, d, f), lambda j, st, ea, eb, lo, hi, nu, tok: (ea[j], 0, 0))
    gate_b = pl.BlockSpec((1, d, f), lambda j, st, ea, eb, lo, hi, nu, tok: (eb[j], 0, 0))
    down_a = pl.BlockSpec((1, f, d), lambda j, st, ea, eb, lo, hi, nu, tok: (ea[j], 0, 0))
    down_b = pl.BlockSpec((1, f, d), lambda j, st, ea, eb, lo, hi, nu, tok: (eb[j], 0, 0))
    gain = pl.BlockSpec((1, d), lambda j, st, ea, eb, lo, hi, nu, tok: (0, 0))
    grid_spec = pltpu.PrefetchScalarGridSpec(
        num_scalar_prefetch=7,
        grid=(item_start.shape[0],),
        in_specs=[pl.BlockSpec(memory_space=pl.ANY), gain, gain,
                  gate_a, gate_a, down_a, gate_b, gate_b, down_b],
        out_specs=pl.BlockSpec(memory_space=pl.ANY),
        scratch_shapes=[pltpu.VMEM((2, tm, w), F32), pltpu.VMEM((2, tm, d), F32),
                        pltpu.SemaphoreType.DMA((2,)), pltpu.SemaphoreType.DMA(())],
    )
    vmem = 2 * 6 * d * f * 2 + 2 * tm * w * 4 + 2 * tm * d * 4 + 8 * tm * d * 4
    return pl.pallas_call(
        _expert_pair_kernel,
        grid_spec=grid_spec,
        out_shape=jax.ShapeDtypeStruct((n, d), F32),
        compiler_params=pltpu.CompilerParams(
            dimension_semantics=("arbitrary",), vmem_limit_bytes=vmem + 4 * MIB),
        name="expert_pairs",
    )(item_start, item_ea, item_eb, item_lo, item_hi, n_used, slot_token, rows, g_ffn.reshape(1, d),
      g_final.reshape(1, d), wg, wu, wd, wg, wu, wd)


def kernel(x, g_mix, w_in, w_pool, pool_scale, w_fourier, w_out, g_ffn, w_group_router,
           b_group_router, w_expert_router, b_expert_router, w_gate, w_up, w_down, g_final):
    b, s, d = x.shape
    assert d == D_MODEL and s % (2 * SEQ_TILE) == 0 and s % (2 * TOKEN_TILE) == 0
    assert (b * s) % max(TOKEN_TILE, EXPERT_TILE) == 0
    n = b * s
    x2 = x.reshape(n, d)

    cw, sw, w_top_bf16 = _mixer_weights(w_fourier, w_pool, pool_scale, w_out, s)
    u, (w_out_bf16, w_gate_bf16) = _norm_proj(x2, g_mix, w_in.astype(BF16), [w_out, w_gate])
    mixed_lo, mixed_hi, w_up_bf16 = _mix(u.reshape(b, s, d), cw, sw, w_up)

    wr = jnp.concatenate([w_group_router, w_expert_router], axis=1)
    wr = jnp.pad(wr, ((0, 0), (0, ROUTER_LANES - wr.shape[1]))).astype(BF16)
    br = jnp.concatenate([b_group_router, b_expert_router])
    br = jnp.pad(br, (0, ROUTER_LANES - br.shape[0])).reshape(1, ROUTER_LANES)

    rows, route, counts, w_down_bf16 = _out_proj_route(
        x2, mixed_lo.reshape(n // 2, d), mixed_hi.reshape(n // 2, d), s, w_top_bf16, w_out_bf16, g_ffn, wr, br,
        w_down)
    pos, item_start, item_ea, item_eb, item_lo, item_hi, n_used = _routing_plan(route, counts, n)
    slot_token = _invert_permutation(pos)
    out = _expert_pairs(item_start, item_ea, item_eb, item_lo, item_hi, n_used, slot_token, rows,
                        g_ffn, g_final, w_gate_bf16, w_up_bf16, w_down_bf16)
    return out.reshape(b, s, d)
```

```python
import functools

import numpy as np
import jax
import jax.numpy as jnp
from jax import lax
from jax.experimental import pallas as pl
from jax.experimental.pallas import tpu as pltpu

D_MODEL = 2048
POOL_WINDOWS = (2, 4, 8, 16)
N_POOL_GROUPS = len(POOL_WINDOWS)
POOL_WIDTH = D_MODEL // 2
POOL_GROUP_DIM = POOL_WIDTH // N_POOL_GROUPS
FOURIER_WIDTH = D_MODEL - POOL_WIDTH
N_FOURIER_HEADS = 4
FOURIER_HEAD_DIM = FOURIER_WIDTH // N_FOURIER_HEADS
N_EXPERT_GROUPS = 4
EXPERTS_PER_GROUP = 4
N_EXPERTS = N_EXPERT_GROUPS * EXPERTS_PER_GROUP
EXPERT_FF = D_MODEL // 4
RMS_EPS = 1e-6

LANES = 128
SUBLANES = 8
BF16_SUBLANES = 16
ROUTER_LANES = LANES
MIB = 1024 * 1024

TOKEN_TILE = 512
SEQ_TILE = 256
POOL_HALO = BF16_SUBLANES
DFT_ROWS = SEQ_TILE + BF16_SUBLANES
TWIDDLE_ROWS = 16

PAIR_SLOT_A = (0, 0, 0, 1, 1, 3)
PAIR_SLOT_B = (1, 2, 3, 3, 2, 2)
PAIRS_PER_GROUP = len(PAIR_SLOT_A)
N_CLASSES = N_EXPERT_GROUPS * PAIRS_PER_GROUP
ROW_WORDS = D_MODEL + LANES
RANK_RADIX = 128
EXPERT_TILE = 256
DMA_PARTS = 8
GATHER_SLOTS = 3

BF16 = jnp.bfloat16
F32 = jnp.float32


def _rmsnorm(x, g):
    ms = jnp.mean(x * x, axis=-1, keepdims=True)
    return x * lax.rsqrt(ms + RMS_EPS) * g


def _dot(a, b):
    return jnp.dot(a, b, preferred_element_type=F32)


def _twiddle(rows, cols, period):
    m = (np.asarray(rows, np.int64)[:, None] * np.asarray(cols, np.int64)[None, :]) % period
    ang = (2.0 * np.pi / period) * m.astype(np.float64)
    return np.cos(ang).astype(np.float32), np.sin(ang).astype(np.float32)


def _pool_band(seq_len, tile, halo):
    n_tiles = seq_len // tile
    out = np.zeros((3, N_POOL_GROUPS, tile, tile + 2 * halo), np.float64)
    for v, m in enumerate((0, 1, n_tiles - 1)):
        t0 = m * tile
        for g, k in enumerate(POOL_WINDOWS):
            for r in range(tile):
                t = t0 + r
                lo = max(t - (k - 1) // 2, 0)
                hi = min(t + k // 2 + 1, seq_len)
                out[v, g, r, lo - t0 + halo:hi - t0 + halo] = 1.0 / (hi - lo)
                out[v, g, r, r + halo] -= 1.0
    return out.astype(np.float32)


def _mixer_weight_kernel(cd_ref, sd_ref, wf_ref, wp_ref, ps_ref, wo_ref, cw_ref, sw_ref, wtop_ref, *, scale):
    wf = wf_ref[0]
    cw = jnp.dot(cd_ref[...], wf, preferred_element_type=F32, precision=lax.Precision.HIGHEST)
    sw = jnp.dot(sd_ref[...], wf, preferred_element_type=F32, precision=lax.Precision.HIGHEST)
    cw_ref[0] = (cw * scale).astype(BF16)
    sw_ref[0] = (sw * (-scale)).astype(BF16)
    wtop_ref[...] = _dot((wp_ref[0] * ps_ref[0]).astype(BF16), wo_ref[...].astype(BF16)).astype(BF16)


def _mixer_weights(w_fourier, w_pool, pool_scale, w_out, seq_len):
    dh = FOURIER_HEAD_DIM
    assert N_FOURIER_HEADS == N_POOL_GROUPS and dh == POOL_GROUP_DIM
    d = w_out.shape[1]
    cd, sd = _twiddle(np.arange(dh), np.arange(dh), dh)
    scale = 1.0 / np.sqrt(float(seq_len * dh))
    mat = pl.BlockSpec((dh, dh), lambda h: (0, 0))
    per_head = pl.BlockSpec((1, dh, dh), lambda h: (h, 0, 0))
    out_rows = pl.BlockSpec((dh, d), lambda h: (h, 0))
    return pl.pallas_call(
        functools.partial(_mixer_weight_kernel, scale=scale),
        grid=(N_FOURIER_HEADS,),
        in_specs=[mat, mat, per_head, per_head, pl.BlockSpec((1, 1, dh), lambda h: (h, 0, 0)), out_rows],
        out_specs=[per_head, per_head, out_rows],
        out_shape=[jax.ShapeDtypeStruct((N_FOURIER_HEADS, dh, dh), BF16)] * 2
        + [jax.ShapeDtypeStruct((POOL_WIDTH, d), BF16)],
        name="mixer_weights",
    )(jnp.asarray(cd), jnp.asarray(sd), w_fourier, w_pool, pool_scale.reshape(N_POOL_GROUPS, 1, dh), w_out)


class _CastAlong:
    def __init__(self, w, n_chunks, chunk_of):
        cols = w.shape[-1]
        rows = w.size // cols // n_chunks
        self.shape = w.shape
        self.src = w.reshape(n_chunks, rows, cols)
        self.spec = pl.BlockSpec((1, rows, cols), lambda *idx: (chunk_of(*idx), 0, 0))
        self.out_shape = jax.ShapeDtypeStruct((n_chunks, rows, cols), BF16)
        self.vmem_bytes = 2 * rows * cols * (4 + 2)


def _cast_chunks(src_refs, dst_refs):
    for src_ref, dst_ref in zip(src_refs, dst_refs):
        dst_ref[...] = src_ref[...].astype(BF16)


def _norm_proj_kernel(x_ref, g_ref, w_ref, *rest, n_cast):
    cast_src, (u_ref, *cast_dst) = rest[:n_cast], rest[n_cast:]
    h = _rmsnorm(x_ref[...], g_ref[...])
    u_ref[...] = _dot(h.astype(BF16), w_ref[...]).astype(BF16)
    _cast_chunks(cast_src, cast_dst)


def _norm_proj(x2, g_mix, w_in_bf16, cast_weights):
    n, d = x2.shape
    tm = TOKEN_TILE
    casts = [_CastAlong(w, n // tm, lambda i: i) for w in cast_weights]
    vmem = (2 * tm * d * 4 + d * d * 2 + 2 * tm * d * 2 + 3 * tm * d * 4
            + sum(c.vmem_bytes for c in casts))
    u, *cast_out = pl.pallas_call(
        functools.partial(_norm_proj_kernel, n_cast=len(casts)),
        grid=(n // tm,),
        in_specs=[
            pl.BlockSpec((tm, d), lambda i: (i, 0)),
            pl.BlockSpec((1, d), lambda i: (0, 0)),
            pl.BlockSpec((d, d), lambda i: (0, 0), pipeline_mode=pl.Buffered(1)),
        ] + [c.spec for c in casts],
        out_specs=[pl.BlockSpec((tm, d), lambda i: (i, 0))] + [c.spec for c in casts],
        out_shape=[jax.ShapeDtypeStruct((n, d), BF16)] + [c.out_shape for c in casts],
        compiler_params=pltpu.CompilerParams(
            dimension_semantics=("arbitrary",), vmem_limit_bytes=vmem + 4 * MIB),
        name="norm_proj",
    )(x2, g_mix.reshape(1, d), w_in_bf16, *[c.src for c in casts])
    return u, [o.reshape(c.shape) for o, c in zip(cast_out, casts)]


def _mix_kernel(uf_ref, up_lo_ref, prev_lo_ref, next_lo_ref, up_hi_ref, prev_hi_ref, next_hi_ref,
                c0_ref, s0_ref, cph_ref, sph_ref, band_lo_ref, band_hi_ref, rev_ref,
                cw_ref, sw_ref, cast_src_ref, lo_ref, hi_ref, cast_dst_ref, lhs_ref):
    t = SEQ_TILE
    tp = DFT_ROWS
    gd = POOL_GROUP_DIM
    hd = FOURIER_HEAD_DIM

    @pl.when(pl.program_id(1) == 0)
    def _():
        cph = cph_ref[0]
        sph = sph_ref[0]
        for r0 in range(0, tp, TWIDDLE_ROWS):
            c0 = c0_ref[r0:r0 + TWIDDLE_ROWS, :]
            s0 = s0_ref[r0:r0 + TWIDDLE_ROWS, :]
            lhs_ref[r0:r0 + TWIDDLE_ROWS, :] = (cph * c0 - sph * s0).astype(BF16)
            lhs_ref[tp + r0:tp + r0 + TWIDDLE_ROWS, :] = (sph * c0 + cph * s0).astype(BF16)

    _cast_chunks([cast_src_ref], [cast_dst_ref])
    pq = _dot(lhs_ref[...], uf_ref[...])
    for h in range(N_FOURIER_HEADS):
        cols = slice(h * hd, (h + 1) * hd)
        out_cols = slice(POOL_WIDTH + h * hd, POOL_WIDTH + (h + 1) * hd)
        pc = _dot(pq[:tp, cols].astype(BF16), cw_ref[h])
        qs = _dot(pq[tp:, cols].astype(BF16), sw_ref[h])
        lo_ref[:, out_cols] = (pc + qs)[:t].astype(BF16)
        hi_ref[:, out_cols] = _dot(rev_ref[...], (pc - qs).astype(BF16)).astype(BF16)

    def pool(up_ref, prev_ref, next_ref, band_ref, out_ref):
        win = jnp.concatenate([prev_ref[...], up_ref[...], next_ref[...]], axis=0)
        for g in range(N_POOL_GROUPS):
            cols = slice(g * gd, (g + 1) * gd)
            out_ref[:, cols] = _dot(band_ref[0, g], win[:, cols]).astype(BF16)

    pool(up_lo_ref, prev_lo_ref, next_lo_ref, band_lo_ref, lo_ref)
    pool(up_hi_ref, prev_hi_ref, next_hi_ref, band_hi_ref, hi_ref)


def _mix(u3, cw, sw, cast_weight):
    b, s, d = u3.shape
    t = SEQ_TILE
    tp = DFT_ROWS
    halo = POOL_HALO
    n_tiles = s // t
    n_steps = n_tiles // 2
    halo_blocks_per_tile = t // halo
    last_halo_block = s // halo - 1

    c0, s0 = _twiddle(np.arange(tp), np.arange(s), s)
    cph, sph = _twiddle(np.arange(0, s // 2, t), np.arange(s), s)
    band = jnp.asarray(_pool_band(s, t, halo)).astype(BF16)
    rev = np.zeros((t, tp), np.float32)
    rev[np.arange(t), t - np.arange(t)] = 1.0

    hi_tile = lambda m: n_tiles - 1 - m
    tile_spec = lambda tile_of: pl.BlockSpec((None, t, POOL_WIDTH), lambda m, bi: (bi, tile_of(m), 0))
    prev_spec = lambda tile_of: pl.BlockSpec(
        (None, halo, POOL_WIDTH),
        lambda m, bi: (bi, jnp.maximum(tile_of(m) * halo_blocks_per_tile - 1, 0), 0))
    next_spec = lambda tile_of: pl.BlockSpec(
        (None, halo, POOL_WIDTH),
        lambda m, bi: (bi, jnp.minimum((tile_of(m) + 1) * halo_blocks_per_tile, last_halo_block), 0))
    band_shape = (1, N_POOL_GROUPS, t, t + 2 * halo)
    const2 = lambda shape: pl.BlockSpec(shape, lambda m, bi: (0, 0))
    const3 = lambda shape: pl.BlockSpec(shape, lambda m, bi: (0, 0, 0))
    lo_tile = lambda m: m

    in_specs = [
        pl.BlockSpec((None, s, FOURIER_WIDTH), lambda m, bi: (bi, 0, 1)),
        tile_spec(lo_tile), prev_spec(lo_tile), next_spec(lo_tile),
        tile_spec(hi_tile), prev_spec(hi_tile), next_spec(hi_tile),
        pl.BlockSpec((tp, s), lambda m, bi: (0, 0), pipeline_mode=pl.Buffered(1)),
        pl.BlockSpec((tp, s), lambda m, bi: (0, 0), pipeline_mode=pl.Buffered(1)),
        pl.BlockSpec((1, 1, s), lambda m, bi: (m, 0, 0)),
        pl.BlockSpec((1, 1, s), lambda m, bi: (m, 0, 0)),
        pl.BlockSpec(band_shape, lambda m, bi: (jnp.where(m == 0, 0, 1), 0, 0, 0)),
        pl.BlockSpec(band_shape, lambda m, bi: (jnp.where(m == 0, 2, 1), 0, 0, 0)),
        const2((t, tp)),
        const3((N_FOURIER_HEADS, FOURIER_HEAD_DIM, FOURIER_HEAD_DIM)),
        const3((N_FOURIER_HEADS, FOURIER_HEAD_DIM, FOURIER_HEAD_DIM)),
    ]
    half_out = pl.BlockSpec((None, t, d), lambda m, bi: (bi, m, 0))
    hi_out = pl.BlockSpec((None, t, d), lambda m, bi: (bi, n_steps - 1 - m, 0))
    cast = _CastAlong(cast_weight, n_steps * b, lambda m, bi: m * b + bi)
    vmem = (2 * s * FOURIER_WIDTH * 2
            + 2 * tp * s * 4
            + 2 * tp * s * 2
            + 2 * 2 * t * d * 2 * 2
            + 4 * 2 * tp * FOURIER_WIDTH * 4
            + cast.vmem_bytes)
    lo, hi, cast_out = pl.pallas_call(
        _mix_kernel,
        grid=(n_steps, b),
        in_specs=in_specs + [cast.spec],
        out_specs=[half_out, hi_out, cast.spec],
        out_shape=[jax.ShapeDtypeStruct((b, s // 2, d), BF16)] * 2 + [cast.out_shape],
        scratch_shapes=[pltpu.VMEM((2 * tp, s), BF16)],
        compiler_params=pltpu.CompilerParams(
            dimension_semantics=("arbitrary", "arbitrary"), vmem_limit_bytes=vmem + 4 * MIB),
        name="seq_mix",
    )(u3, u3, u3, u3, u3, u3, u3, jnp.asarray(c0), jnp.asarray(s0),
      jnp.asarray(cph).reshape(n_steps, 1, s), jnp.asarray(sph).reshape(n_steps, 1, s),
      band, band, jnp.asarray(rev).astype(BF16),
      cw, sw, cast.src)
    return lo, hi, cast_out.reshape(cast.shape)


def _out_proj_route_kernel(x_ref, mixed_lo_ref, mixed_hi_ref, wtop_ref, wbot_ref, g_ref, wr_ref, br_ref,
                           tri_ref, pick_ref, cast_src_ref, rows_ref, route_ref, counts_ref, cast_dst_ref,
                           carry_ref, x1_ref, *, tiles_per_seq, n_tiles):
    step = pl.program_id(0)

    @pl.when(step == 0)
    def _():
        carry_ref[...] = jnp.zeros_like(carry_ref)
        x1_ref[...] = jnp.zeros_like(x1_ref)

    _cast_chunks([cast_src_ref], [cast_dst_ref])

    x1 = x1_ref[...]
    rows_ref[:, :D_MODEL] = x1
    h2 = _rmsnorm(x1, g_ref[...])
    logits = _dot(h2.astype(BF16), wr_ref[...]) + br_ref[...]

    tile = jnp.minimum(step, n_tiles - 1)
    in_lo_half = (tile % tiles_per_seq) < tiles_per_seq // 2
    mixed = jnp.where(in_lo_half, mixed_lo_ref[...], mixed_hi_ref[...])
    x1_ref[...] = (x_ref[...] + _dot(mixed[:, :POOL_WIDTH], wtop_ref[...])
                   + _dot(mixed[:, POOL_WIDTH:], wbot_ref[...]))

    lane = lax.broadcasted_iota(jnp.int32, logits.shape, 1)
    neg = jnp.float32(-jnp.inf)
    big = jnp.int32(ROUTER_LANES)

    is_group = lane < N_EXPERT_GROUPS
    gl = jnp.where(is_group, logits, neg)
    gmax = jnp.max(gl, axis=-1, keepdims=True)
    gidx = jnp.min(jnp.where(gl == gmax, lane, big), axis=-1, keepdims=True)
    p_g = 1.0 / jnp.sum(jnp.exp(gl - gmax), axis=-1, keepdims=True)

    e_lane = lane - N_EXPERT_GROUPS
    in_group = (e_lane >= gidx * EXPERTS_PER_GROUP) & (e_lane < (gidx + 1) * EXPERTS_PER_GROUP)
    el = jnp.where(in_group, logits, neg)
    v1 = jnp.max(el, axis=-1, keepdims=True)
    i1 = jnp.min(jnp.where(el == v1, lane, big), axis=-1, keepdims=True)
    el2 = jnp.where(lane == i1, neg, el)
    v2 = jnp.max(el2, axis=-1, keepdims=True)
    i2 = jnp.min(jnp.where(el2 == v2, lane, big), axis=-1, keepdims=True)
    r = jnp.exp(v2 - v1)
    w1 = p_g / (1.0 + r)
    w2 = p_g * r / (1.0 + r)

    first_is_low = i1 < i2
    first_lane = N_EXPERT_GROUPS + gidx * EXPERTS_PER_GROUP
    la = jnp.where(first_is_low, i1, i2) - first_lane
    lb = jnp.where(first_is_low, i2, i1) - first_lane
    w_low = jnp.where(first_is_low, w1, w2)
    w_high = jnp.where(first_is_low, w2, w1)
    pair = jnp.where(la == 0, lb - 1, jnp.where(la == 1, 6 - lb, 5))
    slot_a_is_high = la == 2
    w_a = jnp.where(slot_a_is_high, w_high, w_low)
    w_b = jnp.where(slot_a_is_high, w_low, w_high)
    cls = gidx * PAIRS_PER_GROUP + pair
    rows_ref[:, D_MODEL:] = jnp.where(lane == 0, w_a, jnp.where(lane == 1, w_b, 0.0))

    onehot = jnp.where(lane == cls, 1.0, 0.0)
    before = _dot(tri_ref[...], onehot.astype(BF16)) + carry_ref[...]
    rank = jnp.sum(jnp.where(lane == cls, before, 0.0), axis=-1, keepdims=True)
    carry_ref[...] += jnp.sum(onehot, axis=0, keepdims=True) * jnp.where(step > 0, 1.0, 0.0)
    counts_ref[...] = carry_ref[...]

    rank_hi = jnp.floor(rank * (1.0 / RANK_RADIX))
    rank_lo = rank - rank_hi * RANK_RADIX
    digits = jnp.where(lane == 0, cls.astype(F32),
                       jnp.where(lane == 1, rank_hi, jnp.where(lane == 2, rank_lo, 0.0)))
    route_ref[...] = lax.dot_general(pick_ref[...], digits.astype(BF16), (((1,), (1,)), ((), ())),
                                     preferred_element_type=F32)


def _out_proj_route(x2, mixed_lo, mixed_hi, seq_len, w_top_bf16, w_out_bf16, g_ffn, wr, br, cast_weight):
    n, d = x2.shape
    tm = TOKEN_TILE
    n_tiles = n // tm
    tiles_per_seq = seq_len // tm
    half_tiles = tiles_per_seq // 2
    const = lambda shape: pl.BlockSpec(shape, lambda i: (0, 0))
    in_tile = lambda i: jnp.minimum(i, n_tiles - 1)
    out_tile = lambda i: jnp.maximum(i - 1, 0)
    lo_spec = pl.BlockSpec((tm, d), lambda i: (
        (in_tile(i) // tiles_per_seq) * half_tiles + jnp.minimum(in_tile(i) % tiles_per_seq, half_tiles - 1), 0))
    hi_spec = pl.BlockSpec((tm, d), lambda i: (
        (in_tile(i) // tiles_per_seq) * half_tiles + jnp.maximum(in_tile(i) % tiles_per_seq - half_tiles, 0), 0))
    tri = np.tril(np.ones((tm, tm), np.float32), -1)
    pick = np.eye(SUBLANES, ROUTER_LANES, dtype=np.float32)
    vmem = (2 * tm * d * 4 + 4 * tm * d * 2 + 2 * tm * ROW_WORDS * 4 + d * d * 2 + tm * d * 4
            + 2 * d * ROUTER_LANES * 2 + 2 * tm * tm * 2 + 3 * tm * d * 4)
    cast = _CastAlong(cast_weight, n_tiles, in_tile)
    vmem += cast.vmem_bytes
    rows, route, counts, cast_out = pl.pallas_call(
        functools.partial(_out_proj_route_kernel, tiles_per_seq=tiles_per_seq, n_tiles=n_tiles),
        grid=(n_tiles + 1,),
        in_specs=[pl.BlockSpec((tm, d), lambda i: (in_tile(i), 0)), lo_spec, hi_spec,
                  pl.BlockSpec((POOL_WIDTH, d), lambda i: (0, 0), pipeline_mode=pl.Buffered(1)),
                  pl.BlockSpec((FOURIER_WIDTH, d), lambda i: (1, 0), pipeline_mode=pl.Buffered(1)),
                  const((1, d)), const((d, ROUTER_LANES)), const((1, ROUTER_LANES)),
                  const((tm, tm)), const((SUBLANES, ROUTER_LANES)), cast.spec],
        out_specs=[pl.BlockSpec((tm, ROW_WORDS), lambda i: (out_tile(i), 0)),
                   pl.BlockSpec((SUBLANES, tm), lambda i: (0, out_tile(i))),
                   const((1, ROUTER_LANES)), cast.spec],
        out_shape=[jax.ShapeDtypeStruct((n, ROW_WORDS), F32),
                   jax.ShapeDtypeStruct((SUBLANES, n), F32),
                   jax.ShapeDtypeStruct((1, ROUTER_LANES), F32), cast.out_shape],
        scratch_shapes=[pltpu.VMEM((1, ROUTER_LANES), F32), pltpu.VMEM((tm, d), F32)],
        compiler_params=pltpu.CompilerParams(
            dimension_semantics=("arbitrary",), vmem_limit_bytes=vmem + 4 * MIB),
        name="out_proj_route",
    )(x2, mixed_lo, mixed_hi, w_top_bf16, w_out_bf16, g_ffn.reshape(1, d), wr, br,
      jnp.asarray(tri).astype(BF16), jnp.asarray(pick).astype(BF16), cast.src)
    return rows, route, counts, cast_out.reshape(cast.shape)


def _routing_plan(route, counts, n):
    tm = EXPERT_TILE
    cnt = counts[0, :N_CLASSES].astype(jnp.int32)
    class_end = jnp.cumsum(cnt)
    class_start = class_end - cnt
    cls = route[0].astype(jnp.int32)
    rank = (route[1] * RANK_RADIX + route[2]).astype(jnp.int32)
    pos = class_start[cls] + rank

    n_items = n // tm + N_CLASSES
    item = jnp.arange(n_items, dtype=jnp.int32)

    def expand(items_per_class):
        item_end = jnp.cumsum(items_per_class)
        item_c = jnp.minimum(item, item_end[-1] - 1)
        item_cls = jnp.sum((item_c[:, None] >= item_end[None, :]).astype(jnp.int32), axis=1)
        return item_cls, item_c - (item_end - items_per_class)[item_cls], item_end[-1]

    in_cls, in_k, in_used = expand((cnt + tm - 1) // tm)
    in_start = jnp.where(in_k < cnt[in_cls] // tm, class_start[in_cls] + in_k * tm, class_end[in_cls] - tm)
    first_tile = class_start // tm
    al_cls, al_k, al_used = expand(jnp.where(cnt > 0, (class_end - 1) // tm - first_tile + 1, 0))
    al_start = (first_tile[al_cls] + al_k) * tm
    al_lo = jnp.clip(class_start[al_cls] - al_start, 0, tm)
    al_hi = jnp.clip(class_end[al_cls] - al_start, 0, tm)

    inside = jnp.all((cnt == 0) | (cnt >= tm))
    item_cls = jnp.where(inside, in_cls, al_cls)
    start = jnp.where(inside, in_start, al_start)
    lo = jnp.where(inside, 0, al_lo)
    hi = jnp.where(inside, tm, al_hi)
    n_used = jnp.where(inside, in_used, al_used).reshape(1)
    group_base = np.repeat(np.arange(N_EXPERT_GROUPS) * EXPERTS_PER_GROUP, PAIRS_PER_GROUP)
    class_ea = jnp.asarray(group_base + np.tile(PAIR_SLOT_A, N_EXPERT_GROUPS), jnp.int32)
    class_eb = jnp.asarray(group_base + np.tile(PAIR_SLOT_B, N_EXPERT_GROUPS), jnp.int32)
    return pos, start, class_ea[item_cls], class_eb[item_cls], lo, hi, n_used


def _row_copy(src_ref, src_row, dst_ref, dst_row, sem):
    return pltpu.make_async_copy(src_ref.at[pl.ds(src_row, 1), :], dst_ref.at[pl.ds(dst_row, 1), :], sem)


def _invert_permutation_kernel(pos_ref, inv_ref):
    def body(t, carry):
        inv_ref[pos_ref[t]] = t
        return carry

    lax.fori_loop(0, pos_ref.shape[0], body, 0, unroll=8)


def _invert_permutation(pos):
    return pl.pallas_call(
        _invert_permutation_kernel,
        in_specs=[pl.BlockSpec(memory_space=pltpu.SMEM)],
        out_specs=pl.BlockSpec(memory_space=pltpu.SMEM),
        out_shape=jax.ShapeDtypeStruct(pos.shape, jnp.int32),
        name="invert_permutation",
    )(pos)


def _expert_pair_kernel(start_ref, ea_ref, eb_ref, lo_ref, hi_ref, nused_ref, tok_ref,
                        rows_hbm_ref, gffn_ref, gfin_ref,
                        wga_ref, wua_ref, wda_ref, wgb_ref, wub_ref, wdb_ref, out_hbm_ref,
                        buf_ref, acc_ref, gather_sem, scatter_sem):
    del ea_ref, eb_ref
    tm = EXPERT_TILE
    j = pl.program_id(0)
    last = nused_ref[0] - 1
    slot = j % 2
    other = 1 - slot

    def start_gather(item, dst_slot, rows=range(tm)):
        base = start_ref[item]
        for r in rows:
            _row_copy(rows_hbm_ref, tok_ref[base + r], buf_ref.at[dst_slot], r, gather_sem.at[dst_slot]).start()

    def wait_gather(dst_slot):
        pltpu.make_async_copy(rows_hbm_ref.at[pl.ds(0, tm), :], buf_ref.at[dst_slot],
                              gather_sem.at[dst_slot]).wait()

    def start_scatter(item, src_slot, rows=range(tm)):
        base = start_ref[item]
        for r in rows:
            _row_copy(acc_ref.at[src_slot], r, out_hbm_ref, tok_ref[base + r], scatter_sem).start()

    def wait_scatter():
        pltpu.make_async_copy(acc_ref.at[0], out_hbm_ref.at[pl.ds(0, tm), :], scatter_sem).wait()

    gslot = j % GATHER_SLOTS

    @pl.when(j == 0)
    def _():
        acc_ref[...] = jnp.zeros_like(acc_ref)
        start_gather(0, 0)
        start_gather(jnp.minimum(1, last), 1)
        start_scatter(0, 1)

    @pl.when(j <= last)
    def _():
        wait_scatter()
        wait_gather(gslot)
        prev_item = jnp.maximum(j - 1, 0)
        ahead_item = jnp.minimum(j + 2, last)

        def issue_copies(part):
            half_parts = DMA_PARTS // 2
            rows = range((part % half_parts) * tm // half_parts, (part % half_parts + 1) * tm // half_parts)
            if part < half_parts:
                start_scatter(prev_item, other, rows)
            else:
                start_gather(ahead_item, (j + 2) % GATHER_SLOTS, rows)

        lo = lo_ref[j]
        hi = hi_ref[j]
        x1 = buf_ref[gslot, :, :D_MODEL]
        h = _rmsnorm(x1, gffn_ref[...]).astype(BF16)

        part = 0
        ff_chunk = EXPERT_FF * 4 // DMA_PARTS
        for slot_lane, (wg_ref, wu_ref, wd_ref) in enumerate(((wga_ref, wua_ref, wda_ref),
                                                              (wgb_ref, wub_ref, wdb_ref))):
            for c0 in range(0, EXPERT_FF, ff_chunk):
                issue_copies(part)
                wts = buf_ref[gslot, :, D_MODEL:]
                lane = lax.broadcasted_iota(jnp.int32, wts.shape, 1)
                w = jnp.sum(jnp.where(lane == slot_lane, wts, 0.0), axis=-1, keepdims=True)
                a = _dot(h, wg_ref[0, :, c0:c0 + ff_chunk])
                v = _dot(h, wu_ref[0, :, c0:c0 + ff_chunk])
                act = (a * (1.0 / (1.0 + jnp.exp(-a))) * v * w).astype(BF16)
                for half, d0 in enumerate(range(0, D_MODEL, D_MODEL // 2)):
                    cols = slice(d0, d0 + D_MODEL // 2)
                    if half == 1:
                        issue_copies(part + 1)
                    y_part = _dot(act, wd_ref[0, c0:c0 + ff_chunk, cols])
                    if part == 0:
                        acc_ref[slot, :, cols] = y_part
                    else:
                        acc_ref[slot, :, cols] += y_part
                part += 2
        res = _rmsnorm(buf_ref[gslot, :, :D_MODEL] + acc_ref[slot], gfin_ref[...])

        row = lax.broadcasted_iota(jnp.int32, (tm, 1), 0)
        mine = (row >= lo) & (row < hi)
        acc_ref[slot] = jnp.where(mine, res, jnp.where(lo > 0, acc_ref[other], 0.0))

    @pl.when(j == last)
    def _():
        wait_scatter()
        start_scatter(j, slot)
        wait_scatter()
        wait_gather((j + 1) % GATHER_SLOTS)
        wait_gather((j + 2) % GATHER_SLOTS)


def _expert_pairs(item_start, item_ea, item_eb, item_lo, item_hi, n_used, slot_token, rows,
                  g_ffn, g_final, wg, wu, wd):
    n, w = rows.shape
    d, f = D_MODEL, EXPERT_FF
    tm = EXPERT_TILE
    gate_a = pl.BlockSpec((1, d, f), lambda j, st, ea, eb, lo, hi, nu, tok: (ea[j], 0, 0))
    gate_b = pl.BlockSpec((1, d, f), lambda j, st, ea, eb, lo, hi, nu, tok: (eb[j], 0, 0))
    down_a = pl.BlockSpec((1, f, d), lambda j, st, ea, eb, lo, hi, nu, tok: (ea[j], 0, 0))
    down_b = pl.BlockSpec((1, f, d), lambda j, st, ea, eb, lo, hi, nu, tok: (eb[j], 0, 0))
    gain = pl.BlockSpec((1, d), lambda j, st, ea, eb, lo, hi, nu, tok: (0, 0))
    grid_spec = pltpu.PrefetchScalarGridSpec(
        num_scalar_prefetch=7,
        grid=(item_start.shape[0],),
        in_specs=[pl.BlockSpec(memory_space=pl.ANY), gain, gain,
                  gate_a, gate_a, down_a, gate_b, gate_b, down_b],
        out_specs=pl.BlockSpec(memory_space=pl.ANY),
        scratch_shapes=[pltpu.VMEM((GATHER_SLOTS, tm, w), F32), pltpu.VMEM((2, tm, d), F32),
                        pltpu.SemaphoreType.DMA((GATHER_SLOTS,)), pltpu.SemaphoreType.DMA(())],
    )
    vmem = 2 * 6 * d * f * 2 + GATHER_SLOTS * tm * w * 4 + 2 * tm * d * 4 + 8 * tm * d * 4
    return pl.pallas_call(
        _expert_pair_kernel,
        grid_spec=grid_spec,
        out_shape=jax.ShapeDtypeStruct((n, d), F32),
        compiler_params=pltpu.CompilerParams(
            dimension_semantics=("arbitrary",), vmem_limit_bytes=vmem + 4 * MIB),
        name="expert_pairs",
    )(item_start, item_ea, item_eb, item_lo, item_hi, n_used, slot_token, rows, g_ffn.reshape(1, d),
      g_final.reshape(1, d), wg, wu, wd, wg, wu, wd)


def kernel(x, g_mix, w_in, w_pool, pool_scale, w_fourier, w_out, g_ffn, w_group_router,
           b_group_router, w_expert_router, b_expert_router, w_gate, w_up, w_down, g_final):
    b, s, d = x.shape
    assert d == D_MODEL and s % (2 * SEQ_TILE) == 0 and s % (2 * TOKEN_TILE) == 0
    assert (b * s) % max(TOKEN_TILE, EXPERT_TILE) == 0
    n = b * s
    x2 = x.reshape(n, d)

    cw, sw, w_top_bf16 = _mixer_weights(w_fourier, w_pool, pool_scale, w_out, s)
    u, (w_out_bf16, w_gate_bf16) = _norm_proj(x2, g_mix, w_in.astype(BF16), [w_out, w_gate])
    mixed_lo, mixed_hi, w_up_bf16 = _mix(u.reshape(b, s, d), cw, sw, w_up)

    wr = jnp.concatenate([w_group_router, w_expert_router], axis=1)
    wr = jnp.pad(wr, ((0, 0), (0, ROUTER_LANES - wr.shape[1]))).astype(BF16)
    br = jnp.concatenate([b_group_router, b_expert_router])
    br = jnp.pad(br, (0, ROUTER_LANES - br.shape[0])).reshape(1, ROUTER_LANES)

    rows, route, counts, w_down_bf16 = _out_proj_route(
        x2, mixed_lo.reshape(n // 2, d), mixed_hi.reshape(n // 2, d), s, w_top_bf16, w_out_bf16, g_ffn, wr, br,
        w_down)
    pos, item_start, item_ea, item_eb, item_lo, item_hi, n_used = _routing_plan(route, counts, n)
    slot_token = _invert_permutation(pos)
    out = _expert_pairs(item_start, item_ea, item_eb, item_lo, item_hi, n_used, slot_token, rows,
                        g_ffn, g_final, w_gate_bf16, w_up_bf16, w_down_bf16)
    return out.reshape(b, s, d)
```

```python
import functools

import numpy as np
import jax
import jax.numpy as jnp
from jax import lax
from jax.experimental import pallas as pl
from jax.experimental.pallas import tpu as pltpu

D_MODEL = 2048
POOL_WINDOWS = (2, 4, 8, 16)
N_POOL_GROUPS = len(POOL_WINDOWS)
POOL_WIDTH = D_MODEL // 2
POOL_GROUP_DIM = POOL_WIDTH // N_POOL_GROUPS
FOURIER_WIDTH = D_MODEL - POOL_WIDTH
N_FOURIER_HEADS = 4
FOURIER_HEAD_DIM = FOURIER_WIDTH // N_FOURIER_HEADS
N_EXPERT_GROUPS = 4
EXPERTS_PER_GROUP = 4
N_EXPERTS = N_EXPERT_GROUPS * EXPERTS_PER_GROUP
EXPERT_FF = D_MODEL // 4
RMS_EPS = 1e-6

LANES = 128
SUBLANES = 8
BF16_SUBLANES = 16
ROUTER_LANES = LANES
MIB = 1024 * 1024

TOKEN_TILE = 512
SEQ_TILE = 256
POOL_HALO = BF16_SUBLANES
DFT_ROWS = SEQ_TILE + BF16_SUBLANES
TWIDDLE_ROWS = 16

PAIR_SLOT_A = (0, 0, 0, 1, 1, 3)
PAIR_SLOT_B = (1, 2, 3, 3, 2, 2)
PAIRS_PER_GROUP = len(PAIR_SLOT_A)
N_CLASSES = N_EXPERT_GROUPS * PAIRS_PER_GROUP
ROW_WORDS = D_MODEL + LANES
RANK_RADIX = 128
EXPERT_TILE = 256
DMA_PARTS = 8
GATHER_SLOTS = 3

BF16 = jnp.bfloat16
F32 = jnp.float32


def _rmsnorm(x, g):
    ms = jnp.mean(x * x, axis=-1, keepdims=True)
    return x * lax.rsqrt(ms + RMS_EPS) * g


def _dot(a, b):
    return jnp.dot(a, b, preferred_element_type=F32)


def _twiddle(rows, cols, period):
    m = (np.asarray(rows, np.int64)[:, None] * np.asarray(cols, np.int64)[None, :]) % period
    ang = (2.0 * np.pi / period) * m.astype(np.float64)
    return np.cos(ang).astype(np.float32), np.sin(ang).astype(np.float32)


def _pool_band(seq_len, tile, halo):
    n_tiles = seq_len // tile
    out = np.zeros((3, N_POOL_GROUPS, tile, tile + 2 * halo), np.float64)
    for v, m in enumerate((0, 1, n_tiles - 1)):
        t0 = m * tile
        for g, k in enumerate(POOL_WINDOWS):
            for r in range(tile):
                t = t0 + r
                lo = max(t - (k - 1) // 2, 0)
                hi = min(t + k // 2 + 1, seq_len)
                out[v, g, r, lo - t0 + halo:hi - t0 + halo] = 1.0 / (hi - lo)
                out[v, g, r, r + halo] -= 1.0
    return out.astype(np.float32)


def _mixer_weight_kernel(cd_ref, sd_ref, wf_ref, wp_ref, ps_ref, wo_ref, cast_src_ref,
                         cw_ref, sw_ref, wtop_ref, cast_dst_ref, *, scale):
    _cast_chunks([cast_src_ref], [cast_dst_ref])
    wf = wf_ref[0]
    cw = jnp.dot(cd_ref[...], wf, preferred_element_type=F32, precision=lax.Precision.HIGHEST)
    sw = jnp.dot(sd_ref[...], wf, preferred_element_type=F32, precision=lax.Precision.HIGHEST)
    cw_ref[0] = (cw * scale).astype(BF16)
    sw_ref[0] = (sw * (-scale)).astype(BF16)
    wtop_ref[...] = _dot((wp_ref[0] * ps_ref[0]).astype(BF16), wo_ref[...].astype(BF16)).astype(BF16)


def _mixer_weights(w_fourier, w_pool, pool_scale, w_out, seq_len, cast_weight):
    dh = FOURIER_HEAD_DIM
    assert N_FOURIER_HEADS == N_POOL_GROUPS and dh == POOL_GROUP_DIM
    d = w_out.shape[1]
    cd, sd = _twiddle(np.arange(dh), np.arange(dh), dh)
    scale = 1.0 / np.sqrt(float(seq_len * dh))
    mat = pl.BlockSpec((dh, dh), lambda h: (0, 0))
    per_head = pl.BlockSpec((1, dh, dh), lambda h: (h, 0, 0))
    out_rows = pl.BlockSpec((dh, d), lambda h: (h, 0))
    cast = _CastAlong(cast_weight, N_FOURIER_HEADS, lambda h: h)
    cw, sw, wtop, cast_out = pl.pallas_call(
        functools.partial(_mixer_weight_kernel, scale=scale),
        grid=(N_FOURIER_HEADS,),
        in_specs=[mat, mat, per_head, per_head, pl.BlockSpec((1, 1, dh), lambda h: (h, 0, 0)), out_rows,
                  cast.spec],
        out_specs=[per_head, per_head, out_rows, cast.spec],
        out_shape=[jax.ShapeDtypeStruct((N_FOURIER_HEADS, dh, dh), BF16)] * 2
        + [jax.ShapeDtypeStruct((POOL_WIDTH, d), BF16), cast.out_shape],
        compiler_params=pltpu.CompilerParams(
            dimension_semantics=("arbitrary",), vmem_limit_bytes=cast.vmem_bytes + 4 * dh * d * 6 + 8 * MIB),
        name="mixer_weights",
    )(jnp.asarray(cd), jnp.asarray(sd), w_fourier, w_pool, pool_scale.reshape(N_POOL_GROUPS, 1, dh), w_out,
      cast.src)
    return cw, sw, wtop, cast_out.reshape(cast.shape)


class _CastAlong:
    def __init__(self, w, n_chunks, chunk_of):
        cols = w.shape[-1]
        rows = w.size // cols // n_chunks
        self.shape = w.shape
        self.src = w.reshape(n_chunks, rows, cols)
        self.spec = pl.BlockSpec((1, rows, cols), lambda *idx: (chunk_of(*idx), 0, 0))
        self.out_shape = jax.ShapeDtypeStruct((n_chunks, rows, cols), BF16)
        self.vmem_bytes = 2 * rows * cols * (4 + 2)


def _cast_chunks(src_refs, dst_refs):
    for src_ref, dst_ref in zip(src_refs, dst_refs):
        dst_ref[...] = src_ref[...].astype(BF16)


def _norm_proj_kernel(x_ref, g_ref, w_ref, *rest, n_cast):
    cast_src, (u_ref, *cast_dst) = rest[:n_cast], rest[n_cast:]
    h = _rmsnorm(x_ref[...], g_ref[...])
    u_ref[...] = _dot(h.astype(BF16), w_ref[...]).astype(BF16)
    _cast_chunks(cast_src, cast_dst)


def _norm_proj(x2, g_mix, w_in_bf16, cast_weights):
    n, d = x2.shape
    tm = TOKEN_TILE
    casts = [_CastAlong(w, n // tm, lambda i: i) for w in cast_weights]
    vmem = (2 * tm * d * 4 + d * d * 2 + 2 * tm * d * 2 + 3 * tm * d * 4
            + sum(c.vmem_bytes for c in casts))
    u, *cast_out = pl.pallas_call(
        functools.partial(_norm_proj_kernel, n_cast=len(casts)),
        grid=(n // tm,),
        in_specs=[
            pl.BlockSpec((tm, d), lambda i: (i, 0)),
            pl.BlockSpec((1, d), lambda i: (0, 0)),
            pl.BlockSpec((d, d), lambda i: (0, 0), pipeline_mode=pl.Buffered(1)),
        ] + [c.spec for c in casts],
        out_specs=[pl.BlockSpec((tm, d), lambda i: (i, 0))] + [c.spec for c in casts],
        out_shape=[jax.ShapeDtypeStruct((n, d), BF16)] + [c.out_shape for c in casts],
        compiler_params=pltpu.CompilerParams(
            dimension_semantics=("arbitrary",), vmem_limit_bytes=vmem + 4 * MIB),
        name="norm_proj",
    )(x2, g_mix.reshape(1, d), w_in_bf16, *[c.src for c in casts])
    return u, [o.reshape(c.shape) for o, c in zip(cast_out, casts)]


def _mix_kernel(uf_ref, up_lo_ref, prev_lo_ref, next_lo_ref, up_hi_ref, prev_hi_ref, next_hi_ref,
                c0_ref, s0_ref, cph_ref, sph_ref, band_lo_ref, band_hi_ref, rev_ref,
                cw_ref, sw_ref, cast_src_ref, lo_ref, hi_ref, cast_dst_ref, lhs_ref):
    t = SEQ_TILE
    tp = DFT_ROWS
    gd = POOL_GROUP_DIM
    hd = FOURIER_HEAD_DIM

    @pl.when(pl.program_id(1) == 0)
    def _():
        cph = cph_ref[0]
        sph = sph_ref[0]
        for r0 in range(0, tp, TWIDDLE_ROWS):
            c0 = c0_ref[r0:r0 + TWIDDLE_ROWS, :]
            s0 = s0_ref[r0:r0 + TWIDDLE_ROWS, :]
            lhs_ref[r0:r0 + TWIDDLE_ROWS, :] = (cph * c0 - sph * s0).astype(BF16)
            lhs_ref[tp + r0:tp + r0 + TWIDDLE_ROWS, :] = (sph * c0 + cph * s0).astype(BF16)

    _cast_chunks([cast_src_ref], [cast_dst_ref])
    pq = _dot(lhs_ref[...], uf_ref[...])
    for h in range(N_FOURIER_HEADS):
        cols = slice(h * hd, (h + 1) * hd)
        out_cols = slice(POOL_WIDTH + h * hd, POOL_WIDTH + (h + 1) * hd)
        pc = _dot(pq[:tp, cols].astype(BF16), cw_ref[h])
        qs = _dot(pq[tp:, cols].astype(BF16), sw_ref[h])
        lo_ref[:, out_cols] = (pc + qs)[:t].astype(BF16)
        hi_ref[:, out_cols] = _dot(rev_ref[...], (pc - qs).astype(BF16)).astype(BF16)

    def pool(up_ref, prev_ref, next_ref, band_ref, out_ref):
        win = jnp.concatenate([prev_ref[...], up_ref[...], next_ref[...]], axis=0)
        for g in range(N_POOL_GROUPS):
            cols = slice(g * gd, (g + 1) * gd)
            out_ref[:, cols] = _dot(band_ref[0, g], win[:, cols]).astype(BF16)

    pool(up_lo_ref, prev_lo_ref, next_lo_ref, band_lo_ref, lo_ref)
    pool(up_hi_ref, prev_hi_ref, next_hi_ref, band_hi_ref, hi_ref)


def _mix(u3, cw, sw, cast_weight):
    b, s, d = u3.shape
    t = SEQ_TILE
    tp = DFT_ROWS
    halo = POOL_HALO
    n_tiles = s // t
    n_steps = n_tiles // 2
    halo_blocks_per_tile = t // halo
    last_halo_block = s // halo - 1

    c0, s0 = _twiddle(np.arange(tp), np.arange(s), s)
    cph, sph = _twiddle(np.arange(0, s // 2, t), np.arange(s), s)
    band = jnp.asarray(_pool_band(s, t, halo)).astype(BF16)
    rev = np.zeros((t, tp), np.float32)
    rev[np.arange(t), t - np.arange(t)] = 1.0

    hi_tile = lambda m: n_tiles - 1 - m
    tile_spec = lambda tile_of: pl.BlockSpec((None, t, POOL_WIDTH), lambda m, bi: (bi, tile_of(m), 0))
    prev_spec = lambda tile_of: pl.BlockSpec(
        (None, halo, POOL_WIDTH),
        lambda m, bi: (bi, jnp.maximum(tile_of(m) * halo_blocks_per_tile - 1, 0), 0))
    next_spec = lambda tile_of: pl.BlockSpec(
        (None, halo, POOL_WIDTH),
        lambda m, bi: (bi, jnp.minimum((tile_of(m) + 1) * halo_blocks_per_tile, last_halo_block), 0))
    band_shape = (1, N_POOL_GROUPS, t, t + 2 * halo)
    const2 = lambda shape: pl.BlockSpec(shape, lambda m, bi: (0, 0))
    const3 = lambda shape: pl.BlockSpec(shape, lambda m, bi: (0, 0, 0))
    lo_tile = lambda m: m

    in_specs = [
        pl.BlockSpec((None, s, FOURIER_WIDTH), lambda m, bi: (bi, 0, 1)),
        tile_spec(lo_tile), prev_spec(lo_tile), next_spec(lo_tile),
        tile_spec(hi_tile), prev_spec(hi_tile), next_spec(hi_tile),
        pl.BlockSpec((tp, s), lambda m, bi: (0, 0), pipeline_mode=pl.Buffered(1)),
        pl.BlockSpec((tp, s), lambda m, bi: (0, 0), pipeline_mode=pl.Buffered(1)),
        pl.BlockSpec((1, 1, s), lambda m, bi: (m, 0, 0)),
        pl.BlockSpec((1, 1, s), lambda m, bi: (m, 0, 0)),
        pl.BlockSpec(band_shape, lambda m, bi: (jnp.where(m == 0, 0, 1), 0, 0, 0)),
        pl.BlockSpec(band_shape, lambda m, bi: (jnp.where(m == 0, 2, 1), 0, 0, 0)),
        const2((t, tp)),
        const3((N_FOURIER_HEADS, FOURIER_HEAD_DIM, FOURIER_HEAD_DIM)),
        const3((N_FOURIER_HEADS, FOURIER_HEAD_DIM, FOURIER_HEAD_DIM)),
    ]
    half_out = pl.BlockSpec((None, t, d), lambda m, bi: (bi, m, 0))
    hi_out = pl.BlockSpec((None, t, d), lambda m, bi: (bi, n_steps - 1 - m, 0))
    cast = _CastAlong(cast_weight, n_steps * b, lambda m, bi: m * b + bi)
    vmem = (2 * s * FOURIER_WIDTH * 2
            + 2 * tp * s * 4
            + 2 * tp * s * 2
            + 2 * 2 * t * d * 2 * 2
            + 4 * 2 * tp * FOURIER_WIDTH * 4
            + cast.vmem_bytes)
    lo, hi, cast_out = pl.pallas_call(
        _mix_kernel,
        grid=(n_steps, b),
        in_specs=in_specs + [cast.spec],
        out_specs=[half_out, hi_out, cast.spec],
        out_shape=[jax.ShapeDtypeStruct((b, s // 2, d), BF16)] * 2 + [cast.out_shape],
        scratch_shapes=[pltpu.VMEM((2 * tp, s), BF16)],
        compiler_params=pltpu.CompilerParams(
            dimension_semantics=("arbitrary", "arbitrary"), vmem_limit_bytes=vmem + 4 * MIB),
        name="seq_mix",
    )(u3, u3, u3, u3, u3, u3, u3, jnp.asarray(c0), jnp.asarray(s0),
      jnp.asarray(cph).reshape(n_steps, 1, s), jnp.asarray(sph).reshape(n_steps, 1, s),
      band, band, jnp.asarray(rev).astype(BF16),
      cw, sw, cast.src)
    return lo, hi, cast_out.reshape(cast.shape)


def _out_proj_route_kernel(x_ref, mixed_lo_ref, mixed_hi_ref, wtop_ref, wbot_ref, g_ref, wr_ref, br_ref,
                           tri_ref, pick_ref, cast_src_ref, rows_ref, route_ref, counts_ref, cast_dst_ref,
                           carry_ref, x1_ref, *, tiles_per_seq, n_tiles):
    step = pl.program_id(0)

    @pl.when(step == 0)
    def _():
        carry_ref[...] = jnp.zeros_like(carry_ref)
        x1_ref[...] = jnp.zeros_like(x1_ref)

    _cast_chunks([cast_src_ref], [cast_dst_ref])

    x1 = x1_ref[...]
    rows_ref[:, :D_MODEL] = x1
    h2 = _rmsnorm(x1, g_ref[...])
    logits = _dot(h2.astype(BF16), wr_ref[...]) + br_ref[...]

    tile = jnp.minimum(step, n_tiles - 1)
    in_lo_half = (tile % tiles_per_seq) < tiles_per_seq // 2
    mixed = jnp.where(in_lo_half, mixed_lo_ref[...], mixed_hi_ref[...])
    x1_ref[...] = (x_ref[...] + _dot(mixed[:, :POOL_WIDTH], wtop_ref[...])
                   + _dot(mixed[:, POOL_WIDTH:], wbot_ref[...]))

    lane = lax.broadcasted_iota(jnp.int32, logits.shape, 1)
    neg = jnp.float32(-jnp.inf)
    big = jnp.int32(ROUTER_LANES)

    is_group = lane < N_EXPERT_GROUPS
    gl = jnp.where(is_group, logits, neg)
    gmax = jnp.max(gl, axis=-1, keepdims=True)
    gidx = jnp.min(jnp.where(gl == gmax, lane, big), axis=-1, keepdims=True)
    p_g = 1.0 / jnp.sum(jnp.exp(gl - gmax), axis=-1, keepdims=True)

    e_lane = lane - N_EXPERT_GROUPS
    in_group = (e_lane >= gidx * EXPERTS_PER_GROUP) & (e_lane < (gidx + 1) * EXPERTS_PER_GROUP)
    el = jnp.where(in_group, logits, neg)
    v1 = jnp.max(el, axis=-1, keepdims=True)
    i1 = jnp.min(jnp.where(el == v1, lane, big), axis=-1, keepdims=True)
    el2 = jnp.where(lane == i1, neg, el)
    v2 = jnp.max(el2, axis=-1, keepdims=True)
    i2 = jnp.min(jnp.where(el2 == v2, lane, big), axis=-1, keepdims=True)
    r = jnp.exp(v2 - v1)
    w1 = p_g / (1.0 + r)
    w2 = p_g * r / (1.0 + r)

    first_is_low = i1 < i2
    first_lane = N_EXPERT_GROUPS + gidx * EXPERTS_PER_GROUP
    la = jnp.where(first_is_low, i1, i2) - first_lane
    lb = jnp.where(first_is_low, i2, i1) - first_lane
    w_low = jnp.where(first_is_low, w1, w2)
    w_high = jnp.where(first_is_low, w2, w1)
    pair = jnp.where(la == 0, lb - 1, jnp.where(la == 1, 6 - lb, 5))
    slot_a_is_high = la == 2
    w_a = jnp.where(slot_a_is_high, w_high, w_low)
    w_b = jnp.where(slot_a_is_high, w_low, w_high)
    cls = gidx * PAIRS_PER_GROUP + pair
    rows_ref[:, D_MODEL:] = jnp.where(lane == 0, w_a, jnp.where(lane == 1, w_b, 0.0))

    onehot = jnp.where(lane == cls, 1.0, 0.0)
    before = _dot(tri_ref[...], onehot.astype(BF16)) + carry_ref[...]
    rank = jnp.sum(jnp.where(lane == cls, before, 0.0), axis=-1, keepdims=True)
    carry_ref[...] += jnp.sum(onehot, axis=0, keepdims=True) * jnp.where(step > 0, 1.0, 0.0)
    counts_ref[...] = carry_ref[...]

    rank_hi = jnp.floor(rank * (1.0 / RANK_RADIX))
    rank_lo = rank - rank_hi * RANK_RADIX
    digits = jnp.where(lane == 0, cls.astype(F32),
                       jnp.where(lane == 1, rank_hi, jnp.where(lane == 2, rank_lo, 0.0)))
    route_ref[...] = lax.dot_general(pick_ref[...], digits.astype(BF16), (((1,), (1,)), ((), ())),
                                     preferred_element_type=F32)


def _out_proj_route(x2, mixed_lo, mixed_hi, seq_len, w_top_bf16, w_out_bf16, g_ffn, wr, br, cast_weight):
    n, d = x2.shape
    tm = TOKEN_TILE
    n_tiles = n // tm
    tiles_per_seq = seq_len // tm
    half_tiles = tiles_per_seq // 2
    const = lambda shape: pl.BlockSpec(shape, lambda i: (0, 0))
    in_tile = lambda i: jnp.minimum(i, n_tiles - 1)
    out_tile = lambda i: jnp.maximum(i - 1, 0)
    lo_spec = pl.BlockSpec((tm, d), lambda i: (
        (in_tile(i) // tiles_per_seq) * half_tiles + jnp.minimum(in_tile(i) % tiles_per_seq, half_tiles - 1), 0))
    hi_spec = pl.BlockSpec((tm, d), lambda i: (
        (in_tile(i) // tiles_per_seq) * half_tiles + jnp.maximum(in_tile(i) % tiles_per_seq - half_tiles, 0), 0))
    tri = np.tril(np.ones((tm, tm), np.float32), -1)
    pick = np.eye(SUBLANES, ROUTER_LANES, dtype=np.float32)
    vmem = (2 * tm * d * 4 + 4 * tm * d * 2 + 2 * tm * ROW_WORDS * 4 + d * d * 2 + tm * d * 4
            + 2 * d * ROUTER_LANES * 2 + 2 * tm * tm * 2 + 3 * tm * d * 4)
    cast = _CastAlong(cast_weight, n_tiles, in_tile)
    vmem += cast.vmem_bytes
    rows, route, counts, cast_out = pl.pallas_call(
        functools.partial(_out_proj_route_kernel, tiles_per_seq=tiles_per_seq, n_tiles=n_tiles),
        grid=(n_tiles + 1,),
        in_specs=[pl.BlockSpec((tm, d), lambda i: (in_tile(i), 0)), lo_spec, hi_spec,
                  pl.BlockSpec((POOL_WIDTH, d), lambda i: (0, 0), pipeline_mode=pl.Buffered(1)),
                  pl.BlockSpec((FOURIER_WIDTH, d), lambda i: (1, 0), pipeline_mode=pl.Buffered(1)),
                  const((1, d)), const((d, ROUTER_LANES)), const((1, ROUTER_LANES)),
                  const((tm, tm)), const((SUBLANES, ROUTER_LANES)), cast.spec],
        out_specs=[pl.BlockSpec((tm, ROW_WORDS), lambda i: (out_tile(i), 0)),
                   pl.BlockSpec((SUBLANES, tm), lambda i: (0, out_tile(i))),
                   const((1, ROUTER_LANES)), cast.spec],
        out_shape=[jax.ShapeDtypeStruct((n, ROW_WORDS), F32),
                   jax.ShapeDtypeStruct((SUBLANES, n), F32),
                   jax.ShapeDtypeStruct((1, ROUTER_LANES), F32), cast.out_shape],
        scratch_shapes=[pltpu.VMEM((1, ROUTER_LANES), F32), pltpu.VMEM((tm, d), F32)],
        compiler_params=pltpu.CompilerParams(
            dimension_semantics=("arbitrary",), vmem_limit_bytes=vmem + 4 * MIB),
        name="out_proj_route",
    )(x2, mixed_lo, mixed_hi, w_top_bf16, w_out_bf16, g_ffn.reshape(1, d), wr, br,
      jnp.asarray(tri).astype(BF16), jnp.asarray(pick).astype(BF16), cast.src)
    return rows, route, counts, cast_out.reshape(cast.shape)


def _routing_plan(route, counts, n):
    tm = EXPERT_TILE
    cnt = counts[0, :N_CLASSES].astype(jnp.int32)
    class_end = jnp.cumsum(cnt)
    class_start = class_end - cnt
    cls = route[0].astype(jnp.int32)
    rank = (route[1] * RANK_RADIX + route[2]).astype(jnp.int32)
    pos = class_start[cls] + rank

    n_items = n // tm + N_CLASSES
    item = jnp.arange(n_items, dtype=jnp.int32)

    def expand(items_per_class):
        item_end = jnp.cumsum(items_per_class)
        item_c = jnp.minimum(item, item_end[-1] - 1)
        item_cls = jnp.sum((item_c[:, None] >= item_end[None, :]).astype(jnp.int32), axis=1)
        return item_cls, item_c - (item_end - items_per_class)[item_cls], item_end[-1]

    in_cls, in_k, in_used = expand((cnt + tm - 1) // tm)
    in_start = jnp.where(in_k < cnt[in_cls] // tm, class_start[in_cls] + in_k * tm, class_end[in_cls] - tm)
    first_tile = class_start // tm
    al_cls, al_k, al_used = expand(jnp.where(cnt > 0, (class_end - 1) // tm - first_tile + 1, 0))
    al_start = (first_tile[al_cls] + al_k) * tm
    al_lo = jnp.clip(class_start[al_cls] - al_start, 0, tm)
    al_hi = jnp.clip(class_end[al_cls] - al_start, 0, tm)

    inside = jnp.all((cnt == 0) | (cnt >= tm))
    item_cls = jnp.where(inside, in_cls, al_cls)
    start = jnp.where(inside, in_start, al_start)
    lo = jnp.where(inside, 0, al_lo)
    hi = jnp.where(inside, tm, al_hi)
    n_used = jnp.where(inside, in_used, al_used).reshape(1)
    group_base = np.repeat(np.arange(N_EXPERT_GROUPS) * EXPERTS_PER_GROUP, PAIRS_PER_GROUP)
    class_ea = jnp.asarray(group_base + np.tile(PAIR_SLOT_A, N_EXPERT_GROUPS), jnp.int32)
    class_eb = jnp.asarray(group_base + np.tile(PAIR_SLOT_B, N_EXPERT_GROUPS), jnp.int32)
    return pos, start, class_ea[item_cls], class_eb[item_cls], lo, hi, n_used


def _row_copy(src_ref, src_row, dst_ref, dst_row, sem):
    return pltpu.make_async_copy(src_ref.at[pl.ds(src_row, 1), :], dst_ref.at[pl.ds(dst_row, 1), :], sem)


def _invert_permutation_kernel(pos_ref, inv_ref):
    def body(t, carry):
        inv_ref[pos_ref[t]] = t
        return carry

    lax.fori_loop(0, pos_ref.shape[0], body, 0, unroll=8)


def _invert_permutation(pos):
    return pl.pallas_call(
        _invert_permutation_kernel,
        in_specs=[pl.BlockSpec(memory_space=pltpu.SMEM)],
        out_specs=pl.BlockSpec(memory_space=pltpu.SMEM),
        out_shape=jax.ShapeDtypeStruct(pos.shape, jnp.int32),
        name="invert_permutation",
    )(pos)


def _expert_pair_kernel(start_ref, ea_ref, eb_ref, lo_ref, hi_ref, nused_ref, tok_ref,
                        rows_hbm_ref, gffn_ref, gfin_ref,
                        wga_ref, wua_ref, wda_ref, wgb_ref, wub_ref, wdb_ref, out_hbm_ref,
                        buf_ref, acc_ref, gather_sem, scatter_sem):
    del ea_ref, eb_ref
    tm = EXPERT_TILE
    j = pl.program_id(0)
    last = nused_ref[0] - 1
    slot = j % 2
    other = 1 - slot

    def start_gather(item, dst_slot, rows=range(tm)):
        base = start_ref[item]
        for r in rows:
            _row_copy(rows_hbm_ref, tok_ref[base + r], buf_ref.at[dst_slot], r, gather_sem.at[dst_slot]).start()

    def wait_gather(dst_slot):
        pltpu.make_async_copy(rows_hbm_ref.at[pl.ds(0, tm), :], buf_ref.at[dst_slot],
                              gather_sem.at[dst_slot]).wait()

    def start_scatter(item, src_slot, rows=range(tm)):
        base = start_ref[item]
        for r in rows:
            _row_copy(acc_ref.at[src_slot], r, out_hbm_ref, tok_ref[base + r], scatter_sem).start()

    def wait_scatter():
        pltpu.make_async_copy(acc_ref.at[0], out_hbm_ref.at[pl.ds(0, tm), :], scatter_sem).wait()

    gslot = j % GATHER_SLOTS

    @pl.when(j == 0)
    def _():
        acc_ref[...] = jnp.zeros_like(acc_ref)
        start_gather(0, 0)
        start_gather(jnp.minimum(1, last), 1)
        start_scatter(0, 1)

    @pl.when(j <= last)
    def _():
        wait_scatter()
        wait_gather(gslot)
        prev_item = jnp.maximum(j - 1, 0)
        ahead_item = jnp.minimum(j + 2, last)

        def issue_copies(part):
            half_parts = DMA_PARTS // 2
            rows = range((part % half_parts) * tm // half_parts, (part % half_parts + 1) * tm // half_parts)
            if part < half_parts:
                start_scatter(prev_item, other, rows)
            else:
                start_gather(ahead_item, (j + 2) % GATHER_SLOTS, rows)

        lo = lo_ref[j]
        hi = hi_ref[j]
        x1 = buf_ref[gslot, :, :D_MODEL]
        h = _rmsnorm(x1, gffn_ref[...]).astype(BF16)

        part = 0
        ff_chunk = EXPERT_FF * 4 // DMA_PARTS
        for slot_lane, (wg_ref, wu_ref, wd_ref) in enumerate(((wga_ref, wua_ref, wda_ref),
                                                              (wgb_ref, wub_ref, wdb_ref))):
            for c0 in range(0, EXPERT_FF, ff_chunk):
                issue_copies(part)
                wts = buf_ref[gslot, :, D_MODEL:]
                lane = lax.broadcasted_iota(jnp.int32, wts.shape, 1)
                w = jnp.sum(jnp.where(lane == slot_lane, wts, 0.0), axis=-1, keepdims=True)
                a = _dot(h, wg_ref[0, :, c0:c0 + ff_chunk])
                v = _dot(h, wu_ref[0, :, c0:c0 + ff_chunk])
                act = (a * (1.0 / (1.0 + jnp.exp(-a))) * v * w).astype(BF16)
                for half, d0 in enumerate(range(0, D_MODEL, D_MODEL // 2)):
                    cols = slice(d0, d0 + D_MODEL // 2)
                    if half == 1:
                        issue_copies(part + 1)
                    y_part = _dot(act, wd_ref[0, c0:c0 + ff_chunk, cols])
                    if part == 0:
                        acc_ref[slot, :, cols] = y_part
                    else:
                        acc_ref[slot, :, cols] += y_part
                part += 2
        res = _rmsnorm(buf_ref[gslot, :, :D_MODEL] + acc_ref[slot], gfin_ref[...])

        row = lax.broadcasted_iota(jnp.int32, (tm, 1), 0)
        mine = (row >= lo) & (row < hi)
        acc_ref[slot] = jnp.where(mine, res, jnp.where(lo > 0, acc_ref[other], 0.0))

    @pl.when(j == last)
    def _():
        wait_scatter()
        start_scatter(j, slot)
        wait_scatter()
        wait_gather((j + 1) % GATHER_SLOTS)
        wait_gather((j + 2) % GATHER_SLOTS)


def _expert_pairs(item_start, item_ea, item_eb, item_lo, item_hi, n_used, slot_token, rows,
                  g_ffn, g_final, wg, wu, wd):
    n, w = rows.shape
    d, f = D_MODEL, EXPERT_FF
    tm = EXPERT_TILE
    gate_a = pl.BlockSpec((1, d, f), lambda j, st, ea, eb, lo, hi, nu, tok: (ea[j], 0, 0))
    gate_b = pl.BlockSpec((1, d, f), lambda j, st, ea, eb, lo, hi, nu, tok: (eb[j], 0, 0))
    down_a = pl.BlockSpec((1, f, d), lambda j, st, ea, eb, lo, hi, nu, tok: (ea[j], 0, 0))
    down_b = pl.BlockSpec((1, f, d), lambda j, st, ea, eb, lo, hi, nu, tok: (eb[j], 0, 0))
    gain = pl.BlockSpec((1, d), lambda j, st, ea, eb, lo, hi, nu, tok: (0, 0))
    grid_spec = pltpu.PrefetchScalarGridSpec(
        num_scalar_prefetch=7,
        grid=(item_start.shape[0],),
        in_specs=[pl.BlockSpec(memory_space=pl.ANY), gain, gain,
                  gate_a, gate_a, down_a, gate_b, gate_b, down_b],
        out_specs=pl.BlockSpec(memory_space=pl.ANY),
        scratch_shapes=[pltpu.VMEM((GATHER_SLOTS, tm, w), F32), pltpu.VMEM((2, tm, d), F32),
                        pltpu.SemaphoreType.DMA((GATHER_SLOTS,)), pltpu.SemaphoreType.DMA(())],
    )
    vmem = 2 * 6 * d * f * 2 + GATHER_SLOTS * tm * w * 4 + 2 * tm * d * 4 + 8 * tm * d * 4
    return pl.pallas_call(
        _expert_pair_kernel,
        grid_spec=grid_spec,
        out_shape=jax.ShapeDtypeStruct((n, d), F32),
        compiler_params=pltpu.CompilerParams(
            dimension_semantics=("arbitrary",), vmem_limit_bytes=vmem + 4 * MIB),
        name="expert_pairs",
    )(item_start, item_ea, item_eb, item_lo, item_hi, n_used, slot_token, rows, g_ffn.reshape(1, d),
      g_final.reshape(1, d), wg, wu, wd, wg, wu, wd)


def kernel(x, g_mix, w_in, w_pool, pool_scale, w_fourier, w_out, g_ffn, w_group_router,
           b_group_router, w_expert_router, b_expert_router, w_gate, w_up, w_down, g_final):
    b, s, d = x.shape
    assert d == D_MODEL and s % (2 * SEQ_TILE) == 0 and s % (2 * TOKEN_TILE) == 0
    assert (b * s) % max(TOKEN_TILE, EXPERT_TILE) == 0
    n = b * s
    x2 = x.reshape(n, d)

    cw, sw, w_top_bf16, w_in_bf16 = _mixer_weights(w_fourier, w_pool, pool_scale, w_out, s, w_in)
    u, (w_out_bf16, w_gate_bf16) = _norm_proj(x2, g_mix, w_in_bf16, [w_out, w_gate])
    mixed_lo, mixed_hi, w_up_bf16 = _mix(u.reshape(b, s, d), cw, sw, w_up)

    wr = jnp.concatenate([w_group_router, w_expert_router], axis=1)
    wr = jnp.pad(wr, ((0, 0), (0, ROUTER_LANES - wr.shape[1]))).astype(BF16)
    br = jnp.concatenate([b_group_router, b_expert_router])
    br = jnp.pad(br, (0, ROUTER_LANES - br.shape[0])).reshape(1, ROUTER_LANES)

    rows, route, counts, w_down_bf16 = _out_proj_route(
        x2, mixed_lo.reshape(n // 2, d), mixed_hi.reshape(n // 2, d), s, w_top_bf16, w_out_bf16, g_ffn, wr, br,
        w_down)
    pos, item_start, item_ea, item_eb, item_lo, item_hi, n_used = _routing_plan(route, counts, n)
    slot_token = _invert_permutation(pos)
    out = _expert_pairs(item_start, item_ea, item_eb, item_lo, item_hi, n_used, slot_token, rows,
                        g_ffn, g_final, w_gate_bf16, w_up_bf16, w_down_bf16)
    return out.reshape(b, s, d)
```

```python
import functools

import numpy as np
import jax
import jax.numpy as jnp
from jax import lax
from jax.experimental import pallas as pl
from jax.experimental.pallas import tpu as pltpu

D_MODEL = 2048
POOL_WINDOWS = (2, 4, 8, 16)
N_POOL_GROUPS = len(POOL_WINDOWS)
POOL_WIDTH = D_MODEL // 2
POOL_GROUP_DIM = POOL_WIDTH // N_POOL_GROUPS
FOURIER_WIDTH = D_MODEL - POOL_WIDTH
N_FOURIER_HEADS = 4
FOURIER_HEAD_DIM = FOURIER_WIDTH // N_FOURIER_HEADS
N_EXPERT_GROUPS = 4
EXPERTS_PER_GROUP = 4
N_EXPERTS = N_EXPERT_GROUPS * EXPERTS_PER_GROUP
EXPERT_FF = D_MODEL // 4
RMS_EPS = 1e-6

LANES = 128
SUBLANES = 8
BF16_SUBLANES = 16
ROUTER_LANES = LANES
MIB = 1024 * 1024

TOKEN_TILE = 512
SEQ_TILE = 256
POOL_HALO = BF16_SUBLANES
DFT_ROWS = SEQ_TILE + BF16_SUBLANES
TWIDDLE_ROWS = 16

PAIR_SLOT_A = (0, 0, 0, 1, 1, 3)
PAIR_SLOT_B = (1, 2, 3, 3, 2, 2)
PAIRS_PER_GROUP = len(PAIR_SLOT_A)
N_CLASSES = N_EXPERT_GROUPS * PAIRS_PER_GROUP
ROW_WORDS = D_MODEL + LANES
RANK_RADIX = 128
EXPERT_TILE = 256
DMA_PARTS = 8
GATHER_SLOTS = 3

BF16 = jnp.bfloat16
F32 = jnp.float32


def _rmsnorm(x, g):
    ms = jnp.mean(x * x, axis=-1, keepdims=True)
    return x * lax.rsqrt(ms + RMS_EPS) * g


def _dot(a, b):
    return jnp.dot(a, b, preferred_element_type=F32)


def _twiddle(rows, cols, period):
    m = (np.asarray(rows, np.int64)[:, None] * np.asarray(cols, np.int64)[None, :]) % period
    ang = (2.0 * np.pi / period) * m.astype(np.float64)
    return np.cos(ang).astype(np.float32), np.sin(ang).astype(np.float32)


def _pool_band(seq_len, tile, halo):
    n_tiles = seq_len // tile
    out = np.zeros((3, N_POOL_GROUPS, tile, tile + 2 * halo), np.float64)
    for v, m in enumerate((0, 1, n_tiles - 1)):
        t0 = m * tile
        for g, k in enumerate(POOL_WINDOWS):
            for r in range(tile):
                t = t0 + r
                lo = max(t - (k - 1) // 2, 0)
                hi = min(t + k // 2 + 1, seq_len)
                out[v, g, r, lo - t0 + halo:hi - t0 + halo] = 1.0 / (hi - lo)
                out[v, g, r, r + halo] -= 1.0
    return out.astype(np.float32)


def _mixer_weight_kernel(cd_ref, sd_ref, wf_ref, wp_ref, ps_ref, wo_ref, cw_ref, sw_ref, wtop_ref, *, scale):
    wf = wf_ref[0]
    cw = jnp.dot(cd_ref[...], wf, preferred_element_type=F32, precision=lax.Precision.HIGHEST)
    sw = jnp.dot(sd_ref[...], wf, preferred_element_type=F32, precision=lax.Precision.HIGHEST)
    cw_ref[0] = (cw * scale).astype(BF16)
    sw_ref[0] = (sw * (-scale)).astype(BF16)
    wtop_ref[...] = _dot((wp_ref[0] * ps_ref[0]).astype(BF16), wo_ref[...].astype(BF16)).astype(BF16)


def _mixer_weights(w_fourier, w_pool, pool_scale, w_out, seq_len):
    dh = FOURIER_HEAD_DIM
    assert N_FOURIER_HEADS == N_POOL_GROUPS and dh == POOL_GROUP_DIM
    d = w_out.shape[1]
    cd, sd = _twiddle(np.arange(dh), np.arange(dh), dh)
    scale = 1.0 / np.sqrt(float(seq_len * dh))
    mat = pl.BlockSpec((dh, dh), lambda h: (0, 0))
    per_head = pl.BlockSpec((1, dh, dh), lambda h: (h, 0, 0))
    out_rows = pl.BlockSpec((dh, d), lambda h: (h, 0))
    return pl.pallas_call(
        functools.partial(_mixer_weight_kernel, scale=scale),
        grid=(N_FOURIER_HEADS,),
        in_specs=[mat, mat, per_head, per_head, pl.BlockSpec((1, 1, dh), lambda h: (h, 0, 0)), out_rows],
        out_specs=[per_head, per_head, out_rows],
        out_shape=[jax.ShapeDtypeStruct((N_FOURIER_HEADS, dh, dh), BF16)] * 2
        + [jax.ShapeDtypeStruct((POOL_WIDTH, d), BF16)],
        name="mixer_weights",
    )(jnp.asarray(cd), jnp.asarray(sd), w_fourier, w_pool, pool_scale.reshape(N_POOL_GROUPS, 1, dh), w_out)


class _CastAlong:
    def __init__(self, w, n_chunks, chunk_of):
        cols = w.shape[-1]
        rows = w.size // cols // n_chunks
        self.shape = w.shape
        self.src = w.reshape(n_chunks, rows, cols)
        self.spec = pl.BlockSpec((1, rows, cols), lambda *idx: (chunk_of(*idx), 0, 0))
        self.out_shape = jax.ShapeDtypeStruct((n_chunks, rows, cols), BF16)
        self.vmem_bytes = 2 * rows * cols * (4 + 2)


def _cast_chunks(src_refs, dst_refs):
    for src_ref, dst_ref in zip(src_refs, dst_refs):
        dst_ref[...] = src_ref[...].astype(BF16)


def _norm_proj_kernel(x_ref, g_ref, w_ref, *rest, n_cast):
    cast_src, (u_ref, *cast_dst) = rest[:n_cast], rest[n_cast:]
    h = _rmsnorm(x_ref[...], g_ref[...])
    u_ref[...] = _dot(h.astype(BF16), w_ref[...]).astype(BF16)
    _cast_chunks(cast_src, cast_dst)


def _norm_proj(x2, g_mix, w_in_bf16, cast_weights):
    n, d = x2.shape
    tm = TOKEN_TILE
    casts = [_CastAlong(w, n // tm, lambda i: i) for w in cast_weights]
    vmem = (2 * tm * d * 4 + d * d * 2 + 2 * tm * d * 2 + 3 * tm * d * 4
            + sum(c.vmem_bytes for c in casts))
    u, *cast_out = pl.pallas_call(
        functools.partial(_norm_proj_kernel, n_cast=len(casts)),
        grid=(n // tm,),
        in_specs=[
            pl.BlockSpec((tm, d), lambda i: (i, 0)),
            pl.BlockSpec((1, d), lambda i: (0, 0)),
            pl.BlockSpec((d, d), lambda i: (0, 0), pipeline_mode=pl.Buffered(1)),
        ] + [c.spec for c in casts],
        out_specs=[pl.BlockSpec((tm, d), lambda i: (i, 0))] + [c.spec for c in casts],
        out_shape=[jax.ShapeDtypeStruct((n, d), BF16)] + [c.out_shape for c in casts],
        compiler_params=pltpu.CompilerParams(
            dimension_semantics=("arbitrary",), vmem_limit_bytes=vmem + 4 * MIB),
        name="norm_proj",
    )(x2, g_mix.reshape(1, d), w_in_bf16, *[c.src for c in casts])
    return u, [o.reshape(c.shape) for o, c in zip(cast_out, casts)]


def _mix_kernel(uf_ref, up_lo_ref, prev_lo_ref, next_lo_ref, up_hi_ref, prev_hi_ref, next_hi_ref,
                c0_ref, s0_ref, cph_ref, sph_ref, band_lo_ref, band_hi_ref, rev_ref,
                cw_ref, sw_ref, cast_src_ref, lo_ref, hi_ref, cast_dst_ref, lhs_ref):
    t = SEQ_TILE
    tp = DFT_ROWS
    gd = POOL_GROUP_DIM
    hd = FOURIER_HEAD_DIM

    @pl.when(pl.program_id(1) == 0)
    def _():
        cph = cph_ref[0]
        sph = sph_ref[0]
        for r0 in range(0, tp, TWIDDLE_ROWS):
            c0 = c0_ref[r0:r0 + TWIDDLE_ROWS, :]
            s0 = s0_ref[r0:r0 + TWIDDLE_ROWS, :]
            lhs_ref[r0:r0 + TWIDDLE_ROWS, :] = (cph * c0 - sph * s0).astype(BF16)
            lhs_ref[tp + r0:tp + r0 + TWIDDLE_ROWS, :] = (sph * c0 + cph * s0).astype(BF16)

    _cast_chunks([cast_src_ref], [cast_dst_ref])
    pq = _dot(lhs_ref[...], uf_ref[...])
    for h in range(N_FOURIER_HEADS):
        cols = slice(h * hd, (h + 1) * hd)
        out_cols = slice(POOL_WIDTH + h * hd, POOL_WIDTH + (h + 1) * hd)
        pc = _dot(pq[:tp, cols].astype(BF16), cw_ref[h])
        qs = _dot(pq[tp:, cols].astype(BF16), sw_ref[h])
        lo_ref[:, out_cols] = (pc + qs)[:t].astype(BF16)
        hi_ref[:, out_cols] = _dot(rev_ref[...], (pc - qs).astype(BF16)).astype(BF16)

    def pool(up_ref, prev_ref, next_ref, band_ref, out_ref):
        win = jnp.concatenate([prev_ref[...], up_ref[...], next_ref[...]], axis=0)
        for g in range(N_POOL_GROUPS):
            cols = slice(g * gd, (g + 1) * gd)
            out_ref[:, cols] = _dot(band_ref[0, g], win[:, cols]).astype(BF16)

    pool(up_lo_ref, prev_lo_ref, next_lo_ref, band_lo_ref, lo_ref)
    pool(up_hi_ref, prev_hi_ref, next_hi_ref, band_hi_ref, hi_ref)


def _mix(u3, cw, sw, cast_weight):
    b, s, d = u3.shape
    t = SEQ_TILE
    tp = DFT_ROWS
    halo = POOL_HALO
    n_tiles = s // t
    n_steps = n_tiles // 2
    halo_blocks_per_tile = t // halo
    last_halo_block = s // halo - 1

    c0, s0 = _twiddle(np.arange(tp), np.arange(s), s)
    cph, sph = _twiddle(np.arange(0, s // 2, t), np.arange(s), s)
    band = jnp.asarray(_pool_band(s, t, halo)).astype(BF16)
    rev = np.zeros((t, tp), np.float32)
    rev[np.arange(t), t - np.arange(t)] = 1.0

    hi_tile = lambda m: n_tiles - 1 - m
    tile_spec = lambda tile_of: pl.BlockSpec((None, t, POOL_WIDTH), lambda m, bi: (bi, tile_of(m), 0))
    prev_spec = lambda tile_of: pl.BlockSpec(
        (None, halo, POOL_WIDTH),
        lambda m, bi: (bi, jnp.maximum(tile_of(m) * halo_blocks_per_tile - 1, 0), 0))
    next_spec = lambda tile_of: pl.BlockSpec(
        (None, halo, POOL_WIDTH),
        lambda m, bi: (bi, jnp.minimum((tile_of(m) + 1) * halo_blocks_per_tile, last_halo_block), 0))
    band_shape = (1, N_POOL_GROUPS, t, t + 2 * halo)
    const2 = lambda shape: pl.BlockSpec(shape, lambda m, bi: (0, 0))
    const3 = lambda shape: pl.BlockSpec(shape, lambda m, bi: (0, 0, 0))
    lo_tile = lambda m: m

    in_specs = [
        pl.BlockSpec((None, s, FOURIER_WIDTH), lambda m, bi: (bi, 0, 1)),
        tile_spec(lo_tile), prev_spec(lo_tile), next_spec(lo_tile),
        tile_spec(hi_tile), prev_spec(hi_tile), next_spec(hi_tile),
        pl.BlockSpec((tp, s), lambda m, bi: (0, 0), pipeline_mode=pl.Buffered(1)),
        pl.BlockSpec((tp, s), lambda m, bi: (0, 0), pipeline_mode=pl.Buffered(1)),
        pl.BlockSpec((1, 1, s), lambda m, bi: (m, 0, 0)),
        pl.BlockSpec((1, 1, s), lambda m, bi: (m, 0, 0)),
        pl.BlockSpec(band_shape, lambda m, bi: (jnp.where(m == 0, 0, 1), 0, 0, 0)),
        pl.BlockSpec(band_shape, lambda m, bi: (jnp.where(m == 0, 2, 1), 0, 0, 0)),
        const2((t, tp)),
        const3((N_FOURIER_HEADS, FOURIER_HEAD_DIM, FOURIER_HEAD_DIM)),
        const3((N_FOURIER_HEADS, FOURIER_HEAD_DIM, FOURIER_HEAD_DIM)),
    ]
    half_out = pl.BlockSpec((None, t, d), lambda m, bi: (bi, m, 0))
    hi_out = pl.BlockSpec((None, t, d), lambda m, bi: (bi, n_steps - 1 - m, 0))
    cast = _CastAlong(cast_weight, n_steps * b, lambda m, bi: m * b + bi)
    vmem = (2 * s * FOURIER_WIDTH * 2
            + 2 * tp * s * 4
            + 2 * tp * s * 2
            + 2 * 2 * t * d * 2 * 2
            + 4 * 2 * tp * FOURIER_WIDTH * 4
            + cast.vmem_bytes)
    lo, hi, cast_out = pl.pallas_call(
        _mix_kernel,
        grid=(n_steps, b),
        in_specs=in_specs + [cast.spec],
        out_specs=[half_out, hi_out, cast.spec],
        out_shape=[jax.ShapeDtypeStruct((b, s // 2, d), BF16)] * 2 + [cast.out_shape],
        scratch_shapes=[pltpu.VMEM((2 * tp, s), BF16)],
        compiler_params=pltpu.CompilerParams(
            dimension_semantics=("arbitrary", "arbitrary"), vmem_limit_bytes=vmem + 4 * MIB),
        name="seq_mix",
    )(u3, u3, u3, u3, u3, u3, u3, jnp.asarray(c0), jnp.asarray(s0),
      jnp.asarray(cph).reshape(n_steps, 1, s), jnp.asarray(sph).reshape(n_steps, 1, s),
      band, band, jnp.asarray(rev).astype(BF16),
      cw, sw, cast.src)
    return lo, hi, cast_out.reshape(cast.shape)


def _out_proj_route_kernel(x_ref, mixed_lo_ref, mixed_hi_ref, wtop_ref, wbot_ref, g_ref, wr_ref, br_ref,
                           tri_ref, pick_ref, cast_src_ref, rows_ref, route_ref, counts_ref, cast_dst_ref,
                           carry_ref, x1_ref, *, tiles_per_seq, n_tiles):
    step = pl.program_id(0)

    @pl.when(step == 0)
    def _():
        carry_ref[...] = jnp.zeros_like(carry_ref)
        x1_ref[...] = jnp.zeros_like(x1_ref)

    _cast_chunks([cast_src_ref], [cast_dst_ref])

    x1 = x1_ref[...]
    rows_ref[:, :D_MODEL] = x1
    h2 = _rmsnorm(x1, g_ref[...])
    logits = _dot(h2.astype(BF16), wr_ref[...]) + br_ref[...]

    tile = jnp.minimum(step, n_tiles - 1)
    in_lo_half = (tile % tiles_per_seq) < tiles_per_seq // 2
    mixed = jnp.where(in_lo_half, mixed_lo_ref[...], mixed_hi_ref[...])
    x1_ref[...] = (x_ref[...] + _dot(mixed[:, :POOL_WIDTH], wtop_ref[...])
                   + _dot(mixed[:, POOL_WIDTH:], wbot_ref[...]))

    lane = lax.broadcasted_iota(jnp.int32, logits.shape, 1)
    neg = jnp.float32(-jnp.inf)
    big = jnp.int32(ROUTER_LANES)

    is_group = lane < N_EXPERT_GROUPS
    gl = jnp.where(is_group, logits, neg)
    gmax = jnp.max(gl, axis=-1, keepdims=True)
    gidx = jnp.min(jnp.where(gl == gmax, lane, big), axis=-1, keepdims=True)
    p_g = 1.0 / jnp.sum(jnp.exp(gl - gmax), axis=-1, keepdims=True)

    e_lane = lane - N_EXPERT_GROUPS
    in_group = (e_lane >= gidx * EXPERTS_PER_GROUP) & (e_lane < (gidx + 1) * EXPERTS_PER_GROUP)
    el = jnp.where(in_group, logits, neg)
    v1 = jnp.max(el, axis=-1, keepdims=True)
    i1 = jnp.min(jnp.where(el == v1, lane, big), axis=-1, keepdims=True)
    el2 = jnp.where(lane == i1, neg, el)
    v2 = jnp.max(el2, axis=-1, keepdims=True)
    i2 = jnp.min(jnp.where(el2 == v2, lane, big), axis=-1, keepdims=True)
    r = jnp.exp(v2 - v1)
    w1 = p_g / (1.0 + r)
    w2 = p_g * r / (1.0 + r)

    first_is_low = i1 < i2
    first_lane = N_EXPERT_GROUPS + gidx * EXPERTS_PER_GROUP
    la = jnp.where(first_is_low, i1, i2) - first_lane
    lb = jnp.where(first_is_low, i2, i1) - first_lane
    w_low = jnp.where(first_is_low, w1, w2)
    w_high = jnp.where(first_is_low, w2, w1)
    pair = jnp.where(la == 0, lb - 1, jnp.where(la == 1, 6 - lb, 5))
    slot_a_is_high = la == 2
    w_a = jnp.where(slot_a_is_high, w_high, w_low)
    w_b = jnp.where(slot_a_is_high, w_low, w_high)
    cls = gidx * PAIRS_PER_GROUP + pair
    rows_ref[:, D_MODEL:] = jnp.where(lane == 0, w_a, jnp.where(lane == 1, w_b, 0.0))

    onehot = jnp.where(lane == cls, 1.0, 0.0)
    before = _dot(tri_ref[...], onehot.astype(BF16)) + carry_ref[...]
    rank = jnp.sum(jnp.where(lane == cls, before, 0.0), axis=-1, keepdims=True)
    carry_ref[...] += jnp.sum(onehot, axis=0, keepdims=True) * jnp.where(step > 0, 1.0, 0.0)
    counts_ref[...] = carry_ref[...]

    rank_hi = jnp.floor(rank * (1.0 / RANK_RADIX))
    rank_lo = rank - rank_hi * RANK_RADIX
    digits = jnp.where(lane == 0, cls.astype(F32),
                       jnp.where(lane == 1, rank_hi, jnp.where(lane == 2, rank_lo, 0.0)))
    route_ref[...] = lax.dot_general(pick_ref[...], digits.astype(BF16), (((1,), (1,)), ((), ())),
                                     preferred_element_type=F32)


def _out_proj_route(x2, mixed_lo, mixed_hi, seq_len, w_top_bf16, w_out_bf16, g_ffn, wr, br, cast_weight):
    n, d = x2.shape
    tm = TOKEN_TILE
    n_tiles = n // tm
    tiles_per_seq = seq_len // tm
    half_tiles = tiles_per_seq // 2
    const = lambda shape: pl.BlockSpec(shape, lambda i: (0, 0))
    in_tile = lambda i: jnp.minimum(i, n_tiles - 1)
    out_tile = lambda i: jnp.maximum(i - 1, 0)
    lo_spec = pl.BlockSpec((tm, d), lambda i: (
        (in_tile(i) // tiles_per_seq) * half_tiles + jnp.minimum(in_tile(i) % tiles_per_seq, half_tiles - 1), 0))
    hi_spec = pl.BlockSpec((tm, d), lambda i: (
        (in_tile(i) // tiles_per_seq) * half_tiles + jnp.maximum(in_tile(i) % tiles_per_seq - half_tiles, 0), 0))
    tri = np.tril(np.ones((tm, tm), np.float32), -1)
    pick = np.eye(SUBLANES, ROUTER_LANES, dtype=np.float32)
    vmem = (2 * tm * d * 4 + 4 * tm * d * 2 + 2 * tm * ROW_WORDS * 4 + d * d * 2 + tm * d * 4
            + 2 * d * ROUTER_LANES * 2 + 2 * tm * tm * 2 + 3 * tm * d * 4)
    cast = _CastAlong(cast_weight, n_tiles, in_tile)
    vmem += cast.vmem_bytes
    rows, route, counts, cast_out = pl.pallas_call(
        functools.partial(_out_proj_route_kernel, tiles_per_seq=tiles_per_seq, n_tiles=n_tiles),
        grid=(n_tiles + 1,),
        in_specs=[pl.BlockSpec((tm, d), lambda i: (in_tile(i), 0)), lo_spec, hi_spec,
                  pl.BlockSpec((POOL_WIDTH, d), lambda i: (0, 0), pipeline_mode=pl.Buffered(1)),
                  pl.BlockSpec((FOURIER_WIDTH, d), lambda i: (1, 0), pipeline_mode=pl.Buffered(1)),
                  const((1, d)), const((d, ROUTER_LANES)), const((1, ROUTER_LANES)),
                  const((tm, tm)), const((SUBLANES, ROUTER_LANES)), cast.spec],
        out_specs=[pl.BlockSpec((tm, ROW_WORDS), lambda i: (out_tile(i), 0)),
                   pl.BlockSpec((SUBLANES, tm), lambda i: (0, out_tile(i))),
                   const((1, ROUTER_LANES)), cast.spec],
        out_shape=[jax.ShapeDtypeStruct((n, ROW_WORDS), F32),
                   jax.ShapeDtypeStruct((SUBLANES, n), F32),
                   jax.ShapeDtypeStruct((1, ROUTER_LANES), F32), cast.out_shape],
        scratch_shapes=[pltpu.VMEM((1, ROUTER_LANES), F32), pltpu.VMEM((tm, d), F32)],
        compiler_params=pltpu.CompilerParams(
            dimension_semantics=("arbitrary",), vmem_limit_bytes=vmem + 4 * MIB),
        name="out_proj_route",
    )(x2, mixed_lo, mixed_hi, w_top_bf16, w_out_bf16, g_ffn.reshape(1, d), wr, br,
      jnp.asarray(tri).astype(BF16), jnp.asarray(pick).astype(BF16), cast.src)
    return rows, route, counts, cast_out.reshape(cast.shape)


def _routing_plan(route, counts, n):
    tm = EXPERT_TILE
    cnt = counts[0, :N_CLASSES].astype(jnp.int32)
    class_end = jnp.cumsum(cnt)
    class_start = class_end - cnt
    cls = route[0].astype(jnp.int32)
    rank = (route[1] * RANK_RADIX + route[2]).astype(jnp.int32)
    pos = class_start[cls] + rank

    n_items = n // tm + N_CLASSES
    item = jnp.arange(n_items, dtype=jnp.int32)

    def expand(items_per_class):
        item_end = jnp.cumsum(items_per_class)
        item_c = jnp.minimum(item, item_end[-1] - 1)
        item_cls = jnp.sum((item_c[:, None] >= item_end[None, :]).astype(jnp.int32), axis=1)
        return item_cls, item_c - (item_end - items_per_class)[item_cls], item_end[-1]

    in_cls, in_k, in_used = expand((cnt + tm - 1) // tm)
    in_start = jnp.where(in_k < cnt[in_cls] // tm, class_start[in_cls] + in_k * tm, class_end[in_cls] - tm)
    first_tile = class_start // tm
    al_cls, al_k, al_used = expand(jnp.where(cnt > 0, (class_end - 1) // tm - first_tile + 1, 0))
    al_start = (first_tile[al_cls] + al_k) * tm
    al_lo = jnp.clip(class_start[al_cls] - al_start, 0, tm)
    al_hi = jnp.clip(class_end[al_cls] - al_start, 0, tm)

    inside = jnp.all((cnt == 0) | (cnt >= tm))
    item_cls = jnp.where(inside, in_cls, al_cls)
    start = jnp.where(inside, in_start, al_start)
    lo = jnp.where(inside, 0, al_lo)
    hi = jnp.where(inside, tm, al_hi)
    n_used = jnp.where(inside, in_used, al_used).reshape(1)
    group_base = np.repeat(np.arange(N_EXPERT_GROUPS) * EXPERTS_PER_GROUP, PAIRS_PER_GROUP)
    class_ea = jnp.asarray(group_base + np.tile(PAIR_SLOT_A, N_EXPERT_GROUPS), jnp.int32)
    class_eb = jnp.asarray(group_base + np.tile(PAIR_SLOT_B, N_EXPERT_GROUPS), jnp.int32)
    return pos, start, class_ea[item_cls], class_eb[item_cls], lo, hi, n_used


def _row_copy(src_ref, src_row, dst_ref, dst_row, sem):
    return pltpu.make_async_copy(src_ref.at[pl.ds(src_row, 1), :], dst_ref.at[pl.ds(dst_row, 1), :], sem)


def _invert_permutation_kernel(pos_ref, inv_ref):
    def body(t, carry):
        inv_ref[pos_ref[t]] = t
        return carry

    lax.fori_loop(0, pos_ref.shape[0], body, 0, unroll=8)


def _invert_permutation(pos):
    return pl.pallas_call(
        _invert_permutation_kernel,
        in_specs=[pl.BlockSpec(memory_space=pltpu.SMEM)],
        out_specs=pl.BlockSpec(memory_space=pltpu.SMEM),
        out_shape=jax.ShapeDtypeStruct(pos.shape, jnp.int32),
        name="invert_permutation",
    )(pos)


def _plan_kernel(cnt_ref, route_ref, tok_ref, start_ref, ea_ref, eb_ref, lo_ref, hi_ref, nused_ref,
                 cstart_ref, pos_vmem_ref, pos_smem_ref, sem):
    tm = EXPERT_TILE
    shift = tm.bit_length() - 1
    n = tok_ref.shape[0]
    n_items = start_ref.shape[0]

    run = jnp.int32(0)
    inside = jnp.bool_(True)
    for c in range(N_CLASSES):
        cnt = cnt_ref[c]
        cstart_ref[c] = run
        inside = inside & ((cnt == 0) | (cnt >= tm))
        run = run + cnt

    cls_row = route_ref[0:1, :]
    slot_row = route_ref[1:2, :] * RANK_RADIX + route_ref[2:3, :]
    for c in range(N_CLASSES):
        slot_row = slot_row + jnp.where(cls_row == c, cstart_ref[c].astype(F32), 0.0)
    pos_vmem_ref[...] = slot_row.astype(jnp.int32)
    to_smem = pltpu.make_async_copy(pos_vmem_ref, pos_smem_ref, sem)
    to_smem.start()

    k = jnp.int32(0)
    for c in range(N_CLASSES):
        cnt = cnt_ref[c]
        cs = cstart_ref[c]
        ce = cs + cnt
        group, pair = divmod(c, PAIRS_PER_GROUP)
        ea = group * EXPERTS_PER_GROUP + PAIR_SLOT_A[pair]
        eb = group * EXPERTS_PER_GROUP + PAIR_SLOT_B[pair]
        first_window = cs >> shift
        n_inside = (cnt + tm - 1) >> shift
        n_aligned = jnp.where(cnt > 0, ((ce - 1) >> shift) - first_window + 1, 0)
        full_windows = cnt >> shift

        def emit(i, carry, k=k, cs=cs, ce=ce, ea=ea, eb=eb, first_window=first_window,
                 full_windows=full_windows):
            start_inside = jnp.where(i < full_windows, cs + i * tm, ce - tm)
            start_aligned = (first_window + i) * tm
            start_ref[k + i] = jnp.where(inside, start_inside, start_aligned)
            ea_ref[k + i] = ea
            eb_ref[k + i] = eb
            lo_ref[k + i] = jnp.where(inside, 0, jnp.clip(cs - start_aligned, 0, tm))
            hi_ref[k + i] = jnp.where(inside, tm, jnp.clip(ce - start_aligned, 0, tm))
            return carry

        n_class_items = jnp.where(inside, n_inside, n_aligned)
        lax.fori_loop(0, n_class_items, emit, 0)
        k = k + n_class_items
    nused_ref[0] = k

    def repeat_last(i, carry):
        for ref in (start_ref, ea_ref, eb_ref, lo_ref, hi_ref):
            ref[i] = ref[k - 1]
        return carry

    lax.fori_loop(k, n_items, repeat_last, 0)

    to_smem.wait()

    def invert(t, carry):
        tok_ref[pos_smem_ref[0, t]] = t
        return carry

    lax.fori_loop(0, n, invert, 0, unroll=8)


def _plan_routing(route, counts, n):
    n_items = n // EXPERT_TILE + N_CLASSES
    smem = lambda size: jax.ShapeDtypeStruct((size,), jnp.int32)
    smem_spec = pl.BlockSpec(memory_space=pltpu.SMEM)
    return pl.pallas_call(
        _plan_kernel,
        in_specs=[smem_spec, pl.BlockSpec(memory_space=pltpu.VMEM)],
        out_specs=[smem_spec] * 7,
        out_shape=[smem(n)] + [smem(n_items)] * 5 + [smem(1)],
        scratch_shapes=[pltpu.SMEM((N_CLASSES,), jnp.int32), pltpu.VMEM((1, n), jnp.int32),
                        pltpu.SMEM((1, n), jnp.int32), pltpu.SemaphoreType.DMA(())],
        name="routing_plan",
    )(counts[0, :N_CLASSES].astype(jnp.int32), route)


def _expert_pair_kernel(start_ref, ea_ref, eb_ref, lo_ref, hi_ref, nused_ref, tok_ref,
                        rows_hbm_ref, gffn_ref, gfin_ref,
                        wga_ref, wua_ref, wda_ref, wgb_ref, wub_ref, wdb_ref, out_hbm_ref,
                        buf_ref, acc_ref, gather_sem, scatter_sem):
    del ea_ref, eb_ref
    tm = EXPERT_TILE
    j = pl.program_id(0)
    last = nused_ref[0] - 1
    slot = j % 2
    other = 1 - slot

    def start_gather(item, dst_slot, rows=range(tm)):
        base = start_ref[item]
        for r in rows:
            _row_copy(rows_hbm_ref, tok_ref[base + r], buf_ref.at[dst_slot], r, gather_sem.at[dst_slot]).start()

    def wait_gather(dst_slot):
        pltpu.make_async_copy(rows_hbm_ref.at[pl.ds(0, tm), :], buf_ref.at[dst_slot],
                              gather_sem.at[dst_slot]).wait()

    def start_scatter(item, src_slot, rows=range(tm)):
        base = start_ref[item]
        for r in rows:
            _row_copy(acc_ref.at[src_slot], r, out_hbm_ref, tok_ref[base + r], scatter_sem).start()

    def wait_scatter():
        pltpu.make_async_copy(acc_ref.at[0], out_hbm_ref.at[pl.ds(0, tm), :], scatter_sem).wait()

    gslot = j % GATHER_SLOTS

    @pl.when(j == 0)
    def _():
        acc_ref[...] = jnp.zeros_like(acc_ref)
        start_gather(0, 0)
        start_gather(jnp.minimum(1, last), 1)
        start_scatter(0, 1)

    @pl.when(j <= last)
    def _():
        wait_scatter()
        wait_gather(gslot)
        prev_item = jnp.maximum(j - 1, 0)
        ahead_item = jnp.minimum(j + 2, last)

        def issue_copies(part):
            half_parts = DMA_PARTS // 2
            rows = range((part % half_parts) * tm // half_parts, (part % half_parts + 1) * tm // half_parts)
            if part < half_parts:
                start_scatter(prev_item, other, rows)
            else:
                start_gather(ahead_item, (j + 2) % GATHER_SLOTS, rows)

        lo = lo_ref[j]
        hi = hi_ref[j]
        x1 = buf_ref[gslot, :, :D_MODEL]
        h = _rmsnorm(x1, gffn_ref[...]).astype(BF16)

        part = 0
        ff_chunk = EXPERT_FF * 4 // DMA_PARTS
        for slot_lane, (wg_ref, wu_ref, wd_ref) in enumerate(((wga_ref, wua_ref, wda_ref),
                                                              (wgb_ref, wub_ref, wdb_ref))):
            for c0 in range(0, EXPERT_FF, ff_chunk):
                issue_copies(part)
                wts = buf_ref[gslot, :, D_MODEL:]
                lane = lax.broadcasted_iota(jnp.int32, wts.shape, 1)
                w = jnp.sum(jnp.where(lane == slot_lane, wts, 0.0), axis=-1, keepdims=True)
                a = _dot(h, wg_ref[0, :, c0:c0 + ff_chunk])
                v = _dot(h, wu_ref[0, :, c0:c0 + ff_chunk])
                act = (a * (1.0 / (1.0 + jnp.exp(-a))) * v * w).astype(BF16)
                for half, d0 in enumerate(range(0, D_MODEL, D_MODEL // 2)):
                    cols = slice(d0, d0 + D_MODEL // 2)
                    if half == 1:
                        issue_copies(part + 1)
                    y_part = _dot(act, wd_ref[0, c0:c0 + ff_chunk, cols])
                    if part == 0:
                        acc_ref[slot, :, cols] = y_part
                    else:
                        acc_ref[slot, :, cols] += y_part
                part += 2
        res = _rmsnorm(buf_ref[gslot, :, :D_MODEL] + acc_ref[slot], gfin_ref[...])

        row = lax.broadcasted_iota(jnp.int32, (tm, 1), 0)
        mine = (row >= lo) & (row < hi)
        acc_ref[slot] = jnp.where(mine, res, jnp.where(lo > 0, acc_ref[other], 0.0))

    @pl.when(j == last)
    def _():
        wait_scatter()
        start_scatter(j, slot)
        wait_scatter()
        wait_gather((j + 1) % GATHER_SLOTS)
        wait_gather((j + 2) % GATHER_SLOTS)


def _expert_pairs(item_start, item_ea, item_eb, item_lo, item_hi, n_used, slot_token, rows,
                  g_ffn, g_final, wg, wu, wd):
    n, w = rows.shape
    d, f = D_MODEL, EXPERT_FF
    tm = EXPERT_TILE
    gate_a = pl.BlockSpec((1, d, f), lambda j, st, ea, eb, lo, hi, nu, tok: (ea[j], 0, 0))
    gate_b = pl.BlockSpec((1, d, f), lambda j, st, ea, eb, lo, hi, nu, tok: (eb[j], 0, 0))
    down_a = pl.BlockSpec((1, f, d), lambda j, st, ea, eb, lo, hi, nu, tok: (ea[j], 0, 0))
    down_b = pl.BlockSpec((1, f, d), lambda j, st, ea, eb, lo, hi, nu, tok: (eb[j], 0, 0))
    gain = pl.BlockSpec((1, d), lambda j, st, ea, eb, lo, hi, nu, tok: (0, 0))
    grid_spec = pltpu.PrefetchScalarGridSpec(
        num_scalar_prefetch=7,
        grid=(item_start.shape[0],),
        in_specs=[pl.BlockSpec(memory_space=pl.ANY), gain, gain,
                  gate_a, gate_a, down_a, gate_b, gate_b, down_b],
        out_specs=pl.BlockSpec(memory_space=pl.ANY),
        scratch_shapes=[pltpu.VMEM((GATHER_SLOTS, tm, w), F32), pltpu.VMEM((2, tm, d), F32),
                        pltpu.SemaphoreType.DMA((GATHER_SLOTS,)), pltpu.SemaphoreType.DMA(())],
    )
    vmem = 2 * 6 * d * f * 2 + GATHER_SLOTS * tm * w * 4 + 2 * tm * d * 4 + 8 * tm * d * 4
    return pl.pallas_call(
        _expert_pair_kernel,
        grid_spec=grid_spec,
        out_shape=jax.ShapeDtypeStruct((n, d), F32),
        compiler_params=pltpu.CompilerParams(
            dimension_semantics=("arbitrary",), vmem_limit_bytes=vmem + 4 * MIB),
        name="expert_pairs",
    )(item_start, item_ea, item_eb, item_lo, item_hi, n_used, slot_token, rows, g_ffn.reshape(1, d),
      g_final.reshape(1, d), wg, wu, wd, wg, wu, wd)


def kernel(x, g_mix, w_in, w_pool, pool_scale, w_fourier, w_out, g_ffn, w_group_router,
           b_group_router, w_expert_router, b_expert_router, w_gate, w_up, w_down, g_final):
    b, s, d = x.shape
    assert d == D_MODEL and s % (2 * SEQ_TILE) == 0 and s % (2 * TOKEN_TILE) == 0
    assert (b * s) % max(TOKEN_TILE, EXPERT_TILE) == 0
    n = b * s
    x2 = x.reshape(n, d)

    cw, sw, w_top_bf16 = _mixer_weights(w_fourier, w_pool, pool_scale, w_out, s)
    u, (w_out_bf16, w_gate_bf16) = _norm_proj(x2, g_mix, w_in.astype(BF16), [w_out, w_gate])
    mixed_lo, mixed_hi, w_up_bf16 = _mix(u.reshape(b, s, d), cw, sw, w_up)

    wr = jnp.concatenate([w_group_router, w_expert_router], axis=1)
    wr = jnp.pad(wr, ((0, 0), (0, ROUTER_LANES - wr.shape[1]))).astype(BF16)
    br = jnp.concatenate([b_group_router, b_expert_router])
    br = jnp.pad(br, (0, ROUTER_LANES - br.shape[0])).reshape(1, ROUTER_LANES)

    rows, route, counts, w_down_bf16 = _out_proj_route(
        x2, mixed_lo.reshape(n // 2, d), mixed_hi.reshape(n // 2, d), s, w_top_bf16, w_out_bf16, g_ffn, wr, br,
        w_down)
    slot_token, item_start, item_ea, item_eb, item_lo, item_hi, n_used = _plan_routing(route, counts, n)
    out = _expert_pairs(item_start, item_ea, item_eb, item_lo, item_hi, n_used, slot_token, rows,
                        g_ffn, g_final, w_gate_bf16, w_up_bf16, w_down_bf16)
    return out.reshape(b, s, d)
```

```python
import functools

import numpy as np
import jax
import jax.numpy as jnp
from jax import lax
from jax.experimental import pallas as pl
from jax.experimental.pallas import tpu as pltpu

D_MODEL = 2048
POOL_WINDOWS = (2, 4, 8, 16)
N_POOL_GROUPS = len(POOL_WINDOWS)
POOL_WIDTH = D_MODEL // 2
POOL_GROUP_DIM = POOL_WIDTH // N_POOL_GROUPS
FOURIER_WIDTH = D_MODEL - POOL_WIDTH
N_FOURIER_HEADS = 4
FOURIER_HEAD_DIM = FOURIER_WIDTH // N_FOURIER_HEADS
N_EXPERT_GROUPS = 4
EXPERTS_PER_GROUP = 4
N_EXPERTS = N_EXPERT_GROUPS * EXPERTS_PER_GROUP
EXPERT_FF = D_MODEL // 4
RMS_EPS = 1e-6

LANES = 128
SUBLANES = 8
BF16_SUBLANES = 16
ROUTER_LANES = LANES
MIB = 1024 * 1024

TOKEN_TILE = 512
SEQ_TILE = 256
POOL_HALO = BF16_SUBLANES
DFT_ROWS = SEQ_TILE + BF16_SUBLANES
TWIDDLE_ROWS = 16

PAIR_SLOT_A = (0, 0, 0, 1, 1, 3)
PAIR_SLOT_B = (1, 2, 3, 3, 2, 2)
PAIRS_PER_GROUP = len(PAIR_SLOT_A)
N_CLASSES = N_EXPERT_GROUPS * PAIRS_PER_GROUP
ROW_WORDS = D_MODEL + LANES
RANK_RADIX = 128
EXPERT_TILE = 256
DMA_PARTS = 8
GATHER_SLOTS = 3
INVERT_UNROLL = 32

BF16 = jnp.bfloat16
F32 = jnp.float32


def _rmsnorm(x, g):
    ms = jnp.mean(x * x, axis=-1, keepdims=True)
    return x * lax.rsqrt(ms + RMS_EPS) * g


def _dot(a, b):
    return jnp.dot(a, b, preferred_element_type=F32)


def _twiddle(rows, cols, period):
    m = (np.asarray(rows, np.int64)[:, None] * np.asarray(cols, np.int64)[None, :]) % period
    ang = (2.0 * np.pi / period) * m.astype(np.float64)
    return np.cos(ang).astype(np.float32), np.sin(ang).astype(np.float32)


def _pool_band(seq_len, tile, halo):
    n_tiles = seq_len // tile
    out = np.zeros((3, N_POOL_GROUPS, tile, tile + 2 * halo), np.float64)
    for v, m in enumerate((0, 1, n_tiles - 1)):
        t0 = m * tile
        for g, k in enumerate(POOL_WINDOWS):
            for r in range(tile):
                t = t0 + r
                lo = max(t - (k - 1) // 2, 0)
                hi = min(t + k // 2 + 1, seq_len)
                out[v, g, r, lo - t0 + halo:hi - t0 + halo] = 1.0 / (hi - lo)
                out[v, g, r, r + halo] -= 1.0
    return out.astype(np.float32)


def _mixer_weight_kernel(cd_ref, sd_ref, wf_ref, wp_ref, ps_ref, wo_ref, cw_ref, sw_ref, wtop_ref, *, scale):
    wf = wf_ref[0]
    cw = jnp.dot(cd_ref[...], wf, preferred_element_type=F32, precision=lax.Precision.HIGHEST)
    sw = jnp.dot(sd_ref[...], wf, preferred_element_type=F32, precision=lax.Precision.HIGHEST)
    cw_ref[0] = (cw * scale).astype(BF16)
    sw_ref[0] = (sw * (-scale)).astype(BF16)
    wtop_ref[...] = _dot((wp_ref[0] * ps_ref[0]).astype(BF16), wo_ref[...].astype(BF16)).astype(BF16)


def _mixer_weights(w_fourier, w_pool, pool_scale, w_out, seq_len):
    dh = FOURIER_HEAD_DIM
    assert N_FOURIER_HEADS == N_POOL_GROUPS and dh == POOL_GROUP_DIM
    d = w_out.shape[1]
    cd, sd = _twiddle(np.arange(dh), np.arange(dh), dh)
    scale = 1.0 / np.sqrt(float(seq_len * dh))
    mat = pl.BlockSpec((dh, dh), lambda h: (0, 0))
    per_head = pl.BlockSpec((1, dh, dh), lambda h: (h, 0, 0))
    out_rows = pl.BlockSpec((dh, d), lambda h: (h, 0))
    return pl.pallas_call(
        functools.partial(_mixer_weight_kernel, scale=scale),
        grid=(N_FOURIER_HEADS,),
        in_specs=[mat, mat, per_head, per_head, pl.BlockSpec((1, 1, dh), lambda h: (h, 0, 0)), out_rows],
        out_specs=[per_head, per_head, out_rows],
        out_shape=[jax.ShapeDtypeStruct((N_FOURIER_HEADS, dh, dh), BF16)] * 2
        + [jax.ShapeDtypeStruct((POOL_WIDTH, d), BF16)],
        name="mixer_weights",
    )(jnp.asarray(cd), jnp.asarray(sd), w_fourier, w_pool, pool_scale.reshape(N_POOL_GROUPS, 1, dh), w_out)


class _CastAlong:
    def __init__(self, w, n_chunks, chunk_of):
        cols = w.shape[-1]
        rows = w.size // cols // n_chunks
        self.shape = w.shape
        self.src = w.reshape(n_chunks, rows, cols)
        self.spec = pl.BlockSpec((1, rows, cols), lambda *idx: (chunk_of(*idx), 0, 0))
        self.out_shape = jax.ShapeDtypeStruct((n_chunks, rows, cols), BF16)
        self.vmem_bytes = 2 * rows * cols * (4 + 2)


def _cast_chunks(src_refs, dst_refs):
    for src_ref, dst_ref in zip(src_refs, dst_refs):
        dst_ref[...] = src_ref[...].astype(BF16)


def _norm_proj_kernel(x_ref, g_ref, w_ref, *rest, n_cast):
    cast_src, (u_ref, *cast_dst) = rest[:n_cast], rest[n_cast:]
    h = _rmsnorm(x_ref[...], g_ref[...])
    u_ref[...] = _dot(h.astype(BF16), w_ref[...]).astype(BF16)
    _cast_chunks(cast_src, cast_dst)


def _norm_proj(x2, g_mix, w_in_bf16, cast_weights):
    n, d = x2.shape
    tm = TOKEN_TILE
    casts = [_CastAlong(w, n // tm, lambda i: i) for w in cast_weights]
    vmem = (2 * tm * d * 4 + d * d * 2 + 2 * tm * d * 2 + 3 * tm * d * 4
            + sum(c.vmem_bytes for c in casts))
    u, *cast_out = pl.pallas_call(
        functools.partial(_norm_proj_kernel, n_cast=len(casts)),
        grid=(n // tm,),
        in_specs=[
            pl.BlockSpec((tm, d), lambda i: (i, 0)),
            pl.BlockSpec((1, d), lambda i: (0, 0)),
            pl.BlockSpec((d, d), lambda i: (0, 0), pipeline_mode=pl.Buffered(1)),
        ] + [c.spec for c in casts],
        out_specs=[pl.BlockSpec((tm, d), lambda i: (i, 0))] + [c.spec for c in casts],
        out_shape=[jax.ShapeDtypeStruct((n, d), BF16)] + [c.out_shape for c in casts],
        compiler_params=pltpu.CompilerParams(
            dimension_semantics=("arbitrary",), vmem_limit_bytes=vmem + 4 * MIB),
        name="norm_proj",
    )(x2, g_mix.reshape(1, d), w_in_bf16, *[c.src for c in casts])
    return u, [o.reshape(c.shape) for o, c in zip(cast_out, casts)]


def _mix_kernel(uf_ref, up_lo_ref, prev_lo_ref, next_lo_ref, up_hi_ref, prev_hi_ref, next_hi_ref,
                c0_ref, s0_ref, cph_ref, sph_ref, band_lo_ref, band_hi_ref, rev_ref,
                cw_ref, sw_ref, cast_src_ref, lo_ref, hi_ref, cast_dst_ref, lhs_ref):
    t = SEQ_TILE
    tp = DFT_ROWS
    gd = POOL_GROUP_DIM
    hd = FOURIER_HEAD_DIM

    @pl.when(pl.program_id(1) == 0)
    def _():
        cph = cph_ref[0]
        sph = sph_ref[0]
        for r0 in range(0, tp, TWIDDLE_ROWS):
            c0 = c0_ref[r0:r0 + TWIDDLE_ROWS, :]
            s0 = s0_ref[r0:r0 + TWIDDLE_ROWS, :]
            lhs_ref[r0:r0 + TWIDDLE_ROWS, :] = (cph * c0 - sph * s0).astype(BF16)
            lhs_ref[tp + r0:tp + r0 + TWIDDLE_ROWS, :] = (sph * c0 + cph * s0).astype(BF16)

    _cast_chunks([cast_src_ref], [cast_dst_ref])
    pq = _dot(lhs_ref[...], uf_ref[...])
    for h in range(N_FOURIER_HEADS):
        cols = slice(h * hd, (h + 1) * hd)
        out_cols = slice(POOL_WIDTH + h * hd, POOL_WIDTH + (h + 1) * hd)
        pc = _dot(pq[:tp, cols].astype(BF16), cw_ref[h])
        qs = _dot(pq[tp:, cols].astype(BF16), sw_ref[h])
        lo_ref[:, out_cols] = (pc + qs)[:t].astype(BF16)
        hi_ref[:, out_cols] = _dot(rev_ref[...], (pc - qs).astype(BF16)).astype(BF16)

    def pool(up_ref, prev_ref, next_ref, band_ref, out_ref):
        win = jnp.concatenate([prev_ref[...], up_ref[...], next_ref[...]], axis=0)
        for g in range(N_POOL_GROUPS):
            cols = slice(g * gd, (g + 1) * gd)
            out_ref[:, cols] = _dot(band_ref[0, g], win[:, cols]).astype(BF16)

    pool(up_lo_ref, prev_lo_ref, next_lo_ref, band_lo_ref, lo_ref)
    pool(up_hi_ref, prev_hi_ref, next_hi_ref, band_hi_ref, hi_ref)


def _mix(u3, cw, sw, cast_weight):
    b, s, d = u3.shape
    t = SEQ_TILE
    tp = DFT_ROWS
    halo = POOL_HALO
    n_tiles = s // t
    n_steps = n_tiles // 2
    halo_blocks_per_tile = t // halo
    last_halo_block = s // halo - 1

    c0, s0 = _twiddle(np.arange(tp), np.arange(s), s)
    cph, sph = _twiddle(np.arange(0, s // 2, t), np.arange(s), s)
    band = jnp.asarray(_pool_band(s, t, halo)).astype(BF16)
    rev = np.zeros((t, tp), np.float32)
    rev[np.arange(t), t - np.arange(t)] = 1.0

    hi_tile = lambda m: n_tiles - 1 - m
    tile_spec = lambda tile_of: pl.BlockSpec((None, t, POOL_WIDTH), lambda m, bi: (bi, tile_of(m), 0))
    prev_spec = lambda tile_of: pl.BlockSpec(
        (None, halo, POOL_WIDTH),
        lambda m, bi: (bi, jnp.maximum(tile_of(m) * halo_blocks_per_tile - 1, 0), 0))
    next_spec = lambda tile_of: pl.BlockSpec(
        (None, halo, POOL_WIDTH),
        lambda m, bi: (bi, jnp.minimum((tile_of(m) + 1) * halo_blocks_per_tile, last_halo_block), 0))
    band_shape = (1, N_POOL_GROUPS, t, t + 2 * halo)
    const2 = lambda shape: pl.BlockSpec(shape, lambda m, bi: (0, 0))
    const3 = lambda shape: pl.BlockSpec(shape, lambda m, bi: (0, 0, 0))
    lo_tile = lambda m: m

    in_specs = [
        pl.BlockSpec((None, s, FOURIER_WIDTH), lambda m, bi: (bi, 0, 1)),
        tile_spec(lo_tile), prev_spec(lo_tile), next_spec(lo_tile),
        tile_spec(hi_tile), prev_spec(hi_tile), next_spec(hi_tile),
        pl.BlockSpec((tp, s), lambda m, bi: (0, 0), pipeline_mode=pl.Buffered(1)),
        pl.BlockSpec((tp, s), lambda m, bi: (0, 0), pipeline_mode=pl.Buffered(1)),
        pl.BlockSpec((1, 1, s), lambda m, bi: (m, 0, 0)),
        pl.BlockSpec((1, 1, s), lambda m, bi: (m, 0, 0)),
        pl.BlockSpec(band_shape, lambda m, bi: (jnp.where(m == 0, 0, 1), 0, 0, 0)),
        pl.BlockSpec(band_shape, lambda m, bi: (jnp.where(m == 0, 2, 1), 0, 0, 0)),
        const2((t, tp)),
        const3((N_FOURIER_HEADS, FOURIER_HEAD_DIM, FOURIER_HEAD_DIM)),
        const3((N_FOURIER_HEADS, FOURIER_HEAD_DIM, FOURIER_HEAD_DIM)),
    ]
    half_out = pl.BlockSpec((None, t, d), lambda m, bi: (bi, m, 0))
    hi_out = pl.BlockSpec((None, t, d), lambda m, bi: (bi, n_steps - 1 - m, 0))
    cast = _CastAlong(cast_weight, n_steps * b, lambda m, bi: m * b + bi)
    vmem = (2 * s * FOURIER_WIDTH * 2
            + 2 * tp * s * 4
            + 2 * tp * s * 2
            + 2 * 2 * t * d * 2 * 2
            + 4 * 2 * tp * FOURIER_WIDTH * 4
            + cast.vmem_bytes)
    lo, hi, cast_out = pl.pallas_call(
        _mix_kernel,
        grid=(n_steps, b),
        in_specs=in_specs + [cast.spec],
        out_specs=[half_out, hi_out, cast.spec],
        out_shape=[jax.ShapeDtypeStruct((b, s // 2, d), BF16)] * 2 + [cast.out_shape],
        scratch_shapes=[pltpu.VMEM((2 * tp, s), BF16)],
        compiler_params=pltpu.CompilerParams(
            dimension_semantics=("arbitrary", "arbitrary"), vmem_limit_bytes=vmem + 4 * MIB),
        name="seq_mix",
    )(u3, u3, u3, u3, u3, u3, u3, jnp.asarray(c0), jnp.asarray(s0),
      jnp.asarray(cph).reshape(n_steps, 1, s), jnp.asarray(sph).reshape(n_steps, 1, s),
      band, band, jnp.asarray(rev).astype(BF16),
      cw, sw, cast.src)
    return lo, hi, cast_out.reshape(cast.shape)


def _out_proj_route_kernel(x_ref, mixed_lo_ref, mixed_hi_ref, wtop_ref, wbot_ref, g_ref, wr_ref, br_ref,
                           tri_ref, pick_ref, cast_src_ref, rows_ref, route_ref, counts_ref, cast_dst_ref,
                           carry_ref, x1_ref, *, tiles_per_seq, n_tiles):
    step = pl.program_id(0)

    @pl.when(step == 0)
    def _():
        carry_ref[...] = jnp.zeros_like(carry_ref)
        x1_ref[...] = jnp.zeros_like(x1_ref)

    _cast_chunks([cast_src_ref], [cast_dst_ref])

    x1 = x1_ref[...]
    rows_ref[:, :D_MODEL] = x1
    h2 = _rmsnorm(x1, g_ref[...])
    logits = _dot(h2.astype(BF16), wr_ref[...]) + br_ref[...]

    tile = jnp.minimum(step, n_tiles - 1)
    in_lo_half = (tile % tiles_per_seq) < tiles_per_seq // 2
    mixed = jnp.where(in_lo_half, mixed_lo_ref[...], mixed_hi_ref[...])
    x1_ref[...] = (x_ref[...] + _dot(mixed[:, :POOL_WIDTH], wtop_ref[...])
                   + _dot(mixed[:, POOL_WIDTH:], wbot_ref[...]))

    lane = lax.broadcasted_iota(jnp.int32, logits.shape, 1)
    neg = jnp.float32(-jnp.inf)
    big = jnp.int32(ROUTER_LANES)

    is_group = lane < N_EXPERT_GROUPS
    gl = jnp.where(is_group, logits, neg)
    gmax = jnp.max(gl, axis=-1, keepdims=True)
    gidx = jnp.min(jnp.where(gl == gmax, lane, big), axis=-1, keepdims=True)
    p_g = 1.0 / jnp.sum(jnp.exp(gl - gmax), axis=-1, keepdims=True)

    e_lane = lane - N_EXPERT_GROUPS
    in_group = (e_lane >= gidx * EXPERTS_PER_GROUP) & (e_lane < (gidx + 1) * EXPERTS_PER_GROUP)
    el = jnp.where(in_group, logits, neg)
    v1 = jnp.max(el, axis=-1, keepdims=True)
    i1 = jnp.min(jnp.where(el == v1, lane, big), axis=-1, keepdims=True)
    el2 = jnp.where(lane == i1, neg, el)
    v2 = jnp.max(el2, axis=-1, keepdims=True)
    i2 = jnp.min(jnp.where(el2 == v2, lane, big), axis=-1, keepdims=True)
    r = jnp.exp(v2 - v1)
    w1 = p_g / (1.0 + r)
    w2 = p_g * r / (1.0 + r)

    first_is_low = i1 < i2
    first_lane = N_EXPERT_GROUPS + gidx * EXPERTS_PER_GROUP
    la = jnp.where(first_is_low, i1, i2) - first_lane
    lb = jnp.where(first_is_low, i2, i1) - first_lane
    w_low = jnp.where(first_is_low, w1, w2)
    w_high = jnp.where(first_is_low, w2, w1)
    pair = jnp.where(la == 0, lb - 1, jnp.where(la == 1, 6 - lb, 5))
    slot_a_is_high = la == 2
    w_a = jnp.where(slot_a_is_high, w_high, w_low)
    w_b = jnp.where(slot_a_is_high, w_low, w_high)
    cls = gidx * PAIRS_PER_GROUP + pair
    rows_ref[:, D_MODEL:] = jnp.where(lane == 0, w_a, jnp.where(lane == 1, w_b, 0.0))

    onehot = jnp.where(lane == cls, 1.0, 0.0)
    before = _dot(tri_ref[...], onehot.astype(BF16)) + carry_ref[...]
    rank = jnp.sum(jnp.where(lane == cls, before, 0.0), axis=-1, keepdims=True)
    carry_ref[...] += jnp.sum(onehot, axis=0, keepdims=True) * jnp.where(step > 0, 1.0, 0.0)
    counts_ref[...] = carry_ref[...]

    rank_hi = jnp.floor(rank * (1.0 / RANK_RADIX))
    rank_lo = rank - rank_hi * RANK_RADIX
    digits = jnp.where(lane == 0, cls.astype(F32),
                       jnp.where(lane == 1, rank_hi, jnp.where(lane == 2, rank_lo, 0.0)))
    route_ref[...] = lax.dot_general(pick_ref[...], digits.astype(BF16), (((1,), (1,)), ((), ())),
                                     preferred_element_type=F32)


def _out_proj_route(x2, mixed_lo, mixed_hi, seq_len, w_top_bf16, w_out_bf16, g_ffn, wr, br, cast_weight):
    n, d = x2.shape
    tm = TOKEN_TILE
    n_tiles = n // tm
    tiles_per_seq = seq_len // tm
    half_tiles = tiles_per_seq // 2
    const = lambda shape: pl.BlockSpec(shape, lambda i: (0, 0))
    in_tile = lambda i: jnp.minimum(i, n_tiles - 1)
    out_tile = lambda i: jnp.maximum(i - 1, 0)
    lo_spec = pl.BlockSpec((tm, d), lambda i: (
        (in_tile(i) // tiles_per_seq) * half_tiles + jnp.minimum(in_tile(i) % tiles_per_seq, half_tiles - 1), 0))
    hi_spec = pl.BlockSpec((tm, d), lambda i: (
        (in_tile(i) // tiles_per_seq) * half_tiles + jnp.maximum(in_tile(i) % tiles_per_seq - half_tiles, 0), 0))
    tri = np.tril(np.ones((tm, tm), np.float32), -1)
    pick = np.eye(SUBLANES, ROUTER_LANES, dtype=np.float32)
    vmem = (2 * tm * d * 4 + 4 * tm * d * 2 + 2 * tm * ROW_WORDS * 4 + d * d * 2 + tm * d * 4
            + 2 * d * ROUTER_LANES * 2 + 2 * tm * tm * 2 + 3 * tm * d * 4)
    cast = _CastAlong(cast_weight, n_tiles, in_tile)
    vmem += cast.vmem_bytes
    rows, route, counts, cast_out = pl.pallas_call(
        functools.partial(_out_proj_route_kernel, tiles_per_seq=tiles_per_seq, n_tiles=n_tiles),
        grid=(n_tiles + 1,),
        in_specs=[pl.BlockSpec((tm, d), lambda i: (in_tile(i), 0)), lo_spec, hi_spec,
                  pl.BlockSpec((POOL_WIDTH, d), lambda i: (0, 0), pipeline_mode=pl.Buffered(1)),
                  pl.BlockSpec((FOURIER_WIDTH, d), lambda i: (1, 0), pipeline_mode=pl.Buffered(1)),
                  const((1, d)), const((d, ROUTER_LANES)), const((1, ROUTER_LANES)),
                  const((tm, tm)), const((SUBLANES, ROUTER_LANES)), cast.spec],
        out_specs=[pl.BlockSpec((tm, ROW_WORDS), lambda i: (out_tile(i), 0)),
                   pl.BlockSpec((SUBLANES, tm), lambda i: (0, out_tile(i))),
                   const((1, ROUTER_LANES)), cast.spec],
        out_shape=[jax.ShapeDtypeStruct((n, ROW_WORDS), F32),
                   jax.ShapeDtypeStruct((SUBLANES, n), F32),
                   jax.ShapeDtypeStruct((1, ROUTER_LANES), F32), cast.out_shape],
        scratch_shapes=[pltpu.VMEM((1, ROUTER_LANES), F32), pltpu.VMEM((tm, d), F32)],
        compiler_params=pltpu.CompilerParams(
            dimension_semantics=("arbitrary",), vmem_limit_bytes=vmem + 4 * MIB),
        name="out_proj_route",
    )(x2, mixed_lo, mixed_hi, w_top_bf16, w_out_bf16, g_ffn.reshape(1, d), wr, br,
      jnp.asarray(tri).astype(BF16), jnp.asarray(pick).astype(BF16), cast.src)
    return rows, route, counts, cast_out.reshape(cast.shape)


def _row_copy(src_ref, src_row, dst_ref, dst_row, sem):
    return pltpu.make_async_copy(src_ref.at[pl.ds(src_row, 1), :], dst_ref.at[pl.ds(dst_row, 1), :], sem)


def _plan_kernel(cnt_ref, route_ref, tok_ref, start_ref, ea_ref, eb_ref, lo_ref, hi_ref, nused_ref,
                 cstart_ref, pos_vmem_ref, pos_smem_ref, sem):
    tm = EXPERT_TILE
    shift = tm.bit_length() - 1
    n = tok_ref.shape[0]
    n_items = start_ref.shape[0]

    run = jnp.int32(0)
    inside = jnp.bool_(True)
    for c in range(N_CLASSES):
        cnt = cnt_ref[c]
        cstart_ref[c] = run
        inside = inside & ((cnt == 0) | (cnt >= tm))
        run = run + cnt

    cls_row = route_ref[0:1, :]
    slot_row = route_ref[1:2, :] * RANK_RADIX + route_ref[2:3, :]
    for c in range(N_CLASSES):
        slot_row = slot_row + jnp.where(cls_row == c, cstart_ref[c].astype(F32), 0.0)
    pos_vmem_ref[...] = slot_row.astype(jnp.int32)
    to_smem = pltpu.make_async_copy(pos_vmem_ref, pos_smem_ref, sem)
    to_smem.start()

    k = jnp.int32(0)
    for c in range(N_CLASSES):
        cnt = cnt_ref[c]
        cs = cstart_ref[c]
        ce = cs + cnt
        group, pair = divmod(c, PAIRS_PER_GROUP)
        ea = group * EXPERTS_PER_GROUP + PAIR_SLOT_A[pair]
        eb = group * EXPERTS_PER_GROUP + PAIR_SLOT_B[pair]
        first_window = cs >> shift
        n_inside = (cnt + tm - 1) >> shift
        n_aligned = jnp.where(cnt > 0, ((ce - 1) >> shift) - first_window + 1, 0)
        full_windows = cnt >> shift

        def emit(i, carry, k=k, cs=cs, ce=ce, ea=ea, eb=eb, first_window=first_window,
                 full_windows=full_windows):
            start_inside = jnp.where(i < full_windows, cs + i * tm, ce - tm)
            start_aligned = (first_window + i) * tm
            start_ref[k + i] = jnp.where(inside, start_inside, start_aligned)
            ea_ref[k + i] = ea
            eb_ref[k + i] = eb
            lo_ref[k + i] = jnp.where(inside, 0, jnp.clip(cs - start_aligned, 0, tm))
            hi_ref[k + i] = jnp.where(inside, tm, jnp.clip(ce - start_aligned, 0, tm))
            return carry

        n_class_items = jnp.where(inside, n_inside, n_aligned)
        lax.fori_loop(0, n_class_items, emit, 0)
        k = k + n_class_items
    nused_ref[0] = k

    def repeat_last(i, carry):
        for ref in (start_ref, ea_ref, eb_ref, lo_ref, hi_ref):
            ref[i] = ref[k - 1]
        return carry

    lax.fori_loop(k, n_items, repeat_last, 0)

    to_smem.wait()

    def invert(t, carry):
        tok_ref[pos_smem_ref[0, t]] = t
        return carry

    lax.fori_loop(0, n, invert, 0, unroll=INVERT_UNROLL)


def _plan_routing(route, counts, n):
    n_items = n // EXPERT_TILE + N_CLASSES
    smem = lambda size: jax.ShapeDtypeStruct((size,), jnp.int32)
    smem_spec = pl.BlockSpec(memory_space=pltpu.SMEM)
    return pl.pallas_call(
        _plan_kernel,
        in_specs=[smem_spec, pl.BlockSpec(memory_space=pltpu.VMEM)],
        out_specs=[smem_spec] * 7,
        out_shape=[smem(n)] + [smem(n_items)] * 5 + [smem(1)],
        scratch_shapes=[pltpu.SMEM((N_CLASSES,), jnp.int32), pltpu.VMEM((1, n), jnp.int32),
                        pltpu.SMEM((1, n), jnp.int32), pltpu.SemaphoreType.DMA(())],
        name="routing_plan",
    )(counts[0, :N_CLASSES].astype(jnp.int32), route)


def _expert_pair_kernel(start_ref, ea_ref, eb_ref, lo_ref, hi_ref, nused_ref, tok_ref,
                        rows_hbm_ref, gffn_ref, gfin_ref,
                        wga_ref, wua_ref, wda_ref, wgb_ref, wub_ref, wdb_ref, out_hbm_ref,
                        buf_ref, acc_ref, gather_sem, scatter_sem):
    del ea_ref, eb_ref
    tm = EXPERT_TILE
    j = pl.program_id(0)
    last = nused_ref[0] - 1
    slot = j % 2
    other = 1 - slot

    def start_gather(item, dst_slot, rows=range(tm)):
        base = start_ref[item]
        for r in rows:
            _row_copy(rows_hbm_ref, tok_ref[base + r], buf_ref.at[dst_slot], r, gather_sem.at[dst_slot]).start()

    def wait_gather(dst_slot):
        pltpu.make_async_copy(rows_hbm_ref.at[pl.ds(0, tm), :], buf_ref.at[dst_slot],
                              gather_sem.at[dst_slot]).wait()

    def start_scatter(item, src_slot, rows=range(tm)):
        base = start_ref[item]
        for r in rows:
            _row_copy(acc_ref.at[src_slot], r, out_hbm_ref, tok_ref[base + r], scatter_sem).start()

    def wait_scatter():
        pltpu.make_async_copy(acc_ref.at[0], out_hbm_ref.at[pl.ds(0, tm), :], scatter_sem).wait()

    gslot = j % GATHER_SLOTS

    @pl.when(j == 0)
    def _():
        acc_ref[...] = jnp.zeros_like(acc_ref)
        start_gather(0, 0)
        start_gather(jnp.minimum(1, last), 1)
        start_scatter(0, 1)

    @pl.when(j <= last)
    def _():
        wait_scatter()
        wait_gather(gslot)
        prev_item = jnp.maximum(j - 1, 0)
        ahead_item = jnp.minimum(j + 2, last)

        def issue_copies(part):
            half_parts = DMA_PARTS // 2
            rows = range((part % half_parts) * tm // half_parts, (part % half_parts + 1) * tm // half_parts)
            if part < half_parts:
                start_scatter(prev_item, other, rows)
            else:
                start_gather(ahead_item, (j + 2) % GATHER_SLOTS, rows)

        lo = lo_ref[j]
        hi = hi_ref[j]
        x1 = buf_ref[gslot, :, :D_MODEL]
        h = _rmsnorm(x1, gffn_ref[...]).astype(BF16)

        part = 0
        ff_chunk = EXPERT_FF * 4 // DMA_PARTS
        for slot_lane, (wg_ref, wu_ref, wd_ref) in enumerate(((wga_ref, wua_ref, wda_ref),
                                                              (wgb_ref, wub_ref, wdb_ref))):
            for c0 in range(0, EXPERT_FF, ff_chunk):
                issue_copies(part)
                wts = buf_ref[gslot, :, D_MODEL:]
                lane = lax.broadcasted_iota(jnp.int32, wts.shape, 1)
                w = jnp.sum(jnp.where(lane == slot_lane, wts, 0.0), axis=-1, keepdims=True)
                a = _dot(h, wg_ref[0, :, c0:c0 + ff_chunk])
                v = _dot(h, wu_ref[0, :, c0:c0 + ff_chunk])
                act = (a * (1.0 / (1.0 + jnp.exp(-a))) * v * w).astype(BF16)
                for half, d0 in enumerate(range(0, D_MODEL, D_MODEL // 2)):
                    cols = slice(d0, d0 + D_MODEL // 2)
                    if half == 1:
                        issue_copies(part + 1)
                    y_part = _dot(act, wd_ref[0, c0:c0 + ff_chunk, cols])
                    if part == 0:
                        acc_ref[slot, :, cols] = y_part
                    else:
                        acc_ref[slot, :, cols] += y_part
                part += 2
        res = _rmsnorm(buf_ref[gslot, :, :D_MODEL] + acc_ref[slot], gfin_ref[...])

        row = lax.broadcasted_iota(jnp.int32, (tm, 1), 0)
        mine = (row >= lo) & (row < hi)
        acc_ref[slot] = jnp.where(mine, res, jnp.where(lo > 0, acc_ref[other], 0.0))

    @pl.when(j == last)
    def _():
        wait_scatter()
        start_scatter(j, slot)
        wait_scatter()
        wait_gather((j + 1) % GATHER_SLOTS)
        wait_gather((j + 2) % GATHER_SLOTS)


def _expert_pairs(item_start, item_ea, item_eb, item_lo, item_hi, n_used, slot_token, rows,
                  g_ffn, g_final, wg, wu, wd):
    n, w = rows.shape
    d, f = D_MODEL, EXPERT_FF
    tm = EXPERT_TILE
    gate_a = pl.BlockSpec((1, d, f), lambda j, st, ea, eb, lo, hi, nu, tok: (ea[j], 0, 0))
    gate_b = pl.BlockSpec((1, d, f), lambda j, st, ea, eb, lo, hi, nu, tok: (eb[j], 0, 0))
    down_a = pl.BlockSpec((1, f, d), lambda j, st, ea, eb, lo, hi, nu, tok: (ea[j], 0, 0))
    down_b = pl.BlockSpec((1, f, d), lambda j, st, ea, eb, lo, hi, nu, tok: (eb[j], 0, 0))
    gain = pl.BlockSpec((1, d), lambda j, st, ea, eb, lo, hi, nu, tok: (0, 0))
    grid_spec = pltpu.PrefetchScalarGridSpec(
        num_scalar_prefetch=7,
        grid=(item_start.shape[0],),
        in_specs=[pl.BlockSpec(memory_space=pl.ANY), gain, gain,
                  gate_a, gate_a, down_a, gate_b, gate_b, down_b],
        out_specs=pl.BlockSpec(memory_space=pl.ANY),
        scratch_shapes=[pltpu.VMEM((GATHER_SLOTS, tm, w), F32), pltpu.VMEM((2, tm, d), F32),
                        pltpu.SemaphoreType.DMA((GATHER_SLOTS,)), pltpu.SemaphoreType.DMA(())],
    )
    vmem = 2 * 6 * d * f * 2 + GATHER_SLOTS * tm * w * 4 + 2 * tm * d * 4 + 8 * tm * d * 4
    return pl.pallas_call(
        _expert_pair_kernel,
        grid_spec=grid_spec,
        out_shape=jax.ShapeDtypeStruct((n, d), F32),
        compiler_params=pltpu.CompilerParams(
            dimension_semantics=("arbitrary",), vmem_limit_bytes=vmem + 4 * MIB),
        name="expert_pairs",
    )(item_start, item_ea, item_eb, item_lo, item_hi, n_used, slot_token, rows, g_ffn.reshape(1, d),
      g_final.reshape(1, d), wg, wu, wd, wg, wu, wd)


def kernel(x, g_mix, w_in, w_pool, pool_scale, w_fourier, w_out, g_ffn, w_group_router,
           b_group_router, w_expert_router, b_expert_router, w_gate, w_up, w_down, g_final):
    b, s, d = x.shape
    assert d == D_MODEL and s % (2 * SEQ_TILE) == 0 and s % (2 * TOKEN_TILE) == 0
    assert (b * s) % max(TOKEN_TILE, EXPERT_TILE) == 0
    n = b * s
    x2 = x.reshape(n, d)

    cw, sw, w_top_bf16 = _mixer_weights(w_fourier, w_pool, pool_scale, w_out, s)
    u, (w_out_bf16, w_gate_bf16) = _norm_proj(x2, g_mix, w_in.astype(BF16), [w_out, w_gate])
    mixed_lo, mixed_hi, w_up_bf16 = _mix(u.reshape(b, s, d), cw, sw, w_up)

    wr = jnp.concatenate([w_group_router, w_expert_router], axis=1)
    wr = jnp.pad(wr, ((0, 0), (0, ROUTER_LANES - wr.shape[1]))).astype(BF16)
    br = jnp.concatenate([b_group_router, b_expert_router])
    br = jnp.pad(br, (0, ROUTER_LANES - br.shape[0])).reshape(1, ROUTER_LANES)

    rows, route, counts, w_down_bf16 = _out_proj_route(
        x2, mixed_lo.reshape(n // 2, d), mixed_hi.reshape(n // 2, d), s, w_top_bf16, w_out_bf16, g_ffn, wr, br,
        w_down)
    slot_token, item_start, item_ea, item_eb, item_lo, item_hi, n_used = _plan_routing(route, counts, n)
    out = _expert_pairs(item_start, item_ea, item_eb, item_lo, item_hi, n_used, slot_token, rows,
                        g_ffn, g_final, w_gate_bf16, w_up_bf16, w_down_bf16)
    return out.reshape(b, s, d)
```

```python
import functools

import numpy as np
import jax
import jax.numpy as jnp
from jax import lax
from jax.experimental import pallas as pl
from jax.experimental.pallas import tpu as pltpu

D_MODEL = 2048
POOL_WINDOWS = (2, 4, 8, 16)
N_POOL_GROUPS = len(POOL_WINDOWS)
POOL_WIDTH = D_MODEL // 2
POOL_GROUP_DIM = POOL_WIDTH // N_POOL_GROUPS
FOURIER_WIDTH = D_MODEL - POOL_WIDTH
N_FOURIER_HEADS = 4
FOURIER_HEAD_DIM = FOURIER_WIDTH // N_FOURIER_HEADS
N_EXPERT_GROUPS = 4
EXPERTS_PER_GROUP = 4
N_EXPERTS = N_EXPERT_GROUPS * EXPERTS_PER_GROUP
EXPERT_FF = D_MODEL // 4
RMS_EPS = 1e-6

LANES = 128
SUBLANES = 8
BF16_SUBLANES = 16
ROUTER_LANES = LANES
MIB = 1024 * 1024

TOKEN_TILE = 512
SEQ_TILE = 256
POOL_HALO = BF16_SUBLANES
DFT_ROWS = SEQ_TILE + BF16_SUBLANES
TWIDDLE_ROWS = 16

PAIR_SLOT_A = (0, 0, 0, 1, 1, 3)
PAIR_SLOT_B = (1, 2, 3, 3, 2, 2)
PAIRS_PER_GROUP = len(PAIR_SLOT_A)
N_CLASSES = N_EXPERT_GROUPS * PAIRS_PER_GROUP
ROW_WORDS = D_MODEL + LANES
RANK_RADIX = 128
EXPERT_TILE = 256
DMA_PARTS = 8
GATHER_SLOTS = 3
INVERT_UNROLL = 32

BF16 = jnp.bfloat16
F32 = jnp.float32


def _rmsnorm(x, g):
    ms = jnp.mean(x * x, axis=-1, keepdims=True)
    return x * lax.rsqrt(ms + RMS_EPS) * g


def _dot(a, b):
    return jnp.dot(a, b, preferred_element_type=F32)


def _twiddle(rows, cols, period):
    m = (np.asarray(rows, np.int64)[:, None] * np.asarray(cols, np.int64)[None, :]) % period
    ang = (2.0 * np.pi / period) * m.astype(np.float64)
    return np.cos(ang).astype(np.float32), np.sin(ang).astype(np.float32)


def _pool_band(seq_len, tile, halo):
    n_tiles = seq_len // tile
    out = np.zeros((3, N_POOL_GROUPS, tile, tile + 2 * halo), np.float64)
    for v, m in enumerate((0, 1, n_tiles - 1)):
        t0 = m * tile
        for g, k in enumerate(POOL_WINDOWS):
            for r in range(tile):
                t = t0 + r
                lo = max(t - (k - 1) // 2, 0)
                hi = min(t + k // 2 + 1, seq_len)
                out[v, g, r, lo - t0 + halo:hi - t0 + halo] = 1.0 / (hi - lo)
                out[v, g, r, r + halo] -= 1.0
    return out.astype(np.float32)


def _mixer_weight_kernel(cd_ref, sd_ref, wf_ref, wp_ref, ps_ref, wo_ref, cw_ref, sw_ref, wtop_ref, *, scale):
    wf = wf_ref[0]
    cw = jnp.dot(cd_ref[...], wf, preferred_element_type=F32, precision=lax.Precision.HIGHEST)
    sw = jnp.dot(sd_ref[...], wf, preferred_element_type=F32, precision=lax.Precision.HIGHEST)
    cw_ref[0] = (cw * scale).astype(BF16)
    sw_ref[0] = (sw * (-scale)).astype(BF16)
    wtop_ref[...] = _dot((wp_ref[0] * ps_ref[0]).astype(BF16), wo_ref[...].astype(BF16)).astype(BF16)


def _mixer_weights(w_fourier, w_pool, pool_scale, w_out, seq_len):
    dh = FOURIER_HEAD_DIM
    assert N_FOURIER_HEADS == N_POOL_GROUPS and dh == POOL_GROUP_DIM
    d = w_out.shape[1]
    cd, sd = _twiddle(np.arange(dh), np.arange(dh), dh)
    scale = 1.0 / np.sqrt(float(seq_len * dh))
    mat = pl.BlockSpec((dh, dh), lambda h: (0, 0))
    per_head = pl.BlockSpec((1, dh, dh), lambda h: (h, 0, 0))
    out_rows = pl.BlockSpec((dh, d), lambda h: (h, 0))
    return pl.pallas_call(
        functools.partial(_mixer_weight_kernel, scale=scale),
        grid=(N_FOURIER_HEADS,),
        in_specs=[mat, mat, per_head, per_head, pl.BlockSpec((1, 1, dh), lambda h: (h, 0, 0)), out_rows],
        out_specs=[per_head, per_head, out_rows],
        out_shape=[jax.ShapeDtypeStruct((N_FOURIER_HEADS, dh, dh), BF16)] * 2
        + [jax.ShapeDtypeStruct((POOL_WIDTH, d), BF16)],
        name="mixer_weights",
    )(jnp.asarray(cd), jnp.asarray(sd), w_fourier, w_pool, pool_scale.reshape(N_POOL_GROUPS, 1, dh), w_out)


class _CastAlong:
    def __init__(self, w, n_chunks, chunk_of):
        cols = w.shape[-1]
        rows = w.size // cols // n_chunks
        self.shape = w.shape
        self.src = w.reshape(n_chunks, rows, cols)
        self.spec = pl.BlockSpec((1, rows, cols), lambda *idx: (chunk_of(*idx), 0, 0))
        self.out_shape = jax.ShapeDtypeStruct((n_chunks, rows, cols), BF16)
        self.vmem_bytes = 2 * rows * cols * (4 + 2)


def _cast_chunks(src_refs, dst_refs):
    for src_ref, dst_ref in zip(src_refs, dst_refs):
        dst_ref[...] = src_ref[...].astype(BF16)


def _norm_proj_kernel(x_ref, g_ref, w_ref, *rest, n_cast):
    cast_src, (u_ref, *cast_dst) = rest[:n_cast], rest[n_cast:]
    x = x_ref[...]
    inv_rms = lax.rsqrt(jnp.mean(x * x, axis=-1, keepdims=True) + RMS_EPS)
    u_ref[...] = (_dot((x * g_ref[...]).astype(BF16), w_ref[...]) * inv_rms).astype(BF16)
    _cast_chunks(cast_src, cast_dst)


def _norm_proj(x2, g_mix, w_in_bf16, cast_weights):
    n, d = x2.shape
    tm = TOKEN_TILE
    casts = [_CastAlong(w, n // tm, lambda i: i) for w in cast_weights]
    vmem = (2 * tm * d * 4 + d * d * 2 + 2 * tm * d * 2 + 3 * tm * d * 4
            + sum(c.vmem_bytes for c in casts))
    u, *cast_out = pl.pallas_call(
        functools.partial(_norm_proj_kernel, n_cast=len(casts)),
        grid=(n // tm,),
        in_specs=[
            pl.BlockSpec((tm, d), lambda i: (i, 0)),
            pl.BlockSpec((1, d), lambda i: (0, 0)),
            pl.BlockSpec((d, d), lambda i: (0, 0), pipeline_mode=pl.Buffered(1)),
        ] + [c.spec for c in casts],
        out_specs=[pl.BlockSpec((tm, d), lambda i: (i, 0))] + [c.spec for c in casts],
        out_shape=[jax.ShapeDtypeStruct((n, d), BF16)] + [c.out_shape for c in casts],
        compiler_params=pltpu.CompilerParams(
            dimension_semantics=("arbitrary",), vmem_limit_bytes=vmem + 4 * MIB),
        name="norm_proj",
    )(x2, g_mix.reshape(1, d), w_in_bf16, *[c.src for c in casts])
    return u, [o.reshape(c.shape) for o, c in zip(cast_out, casts)]


def _mix_kernel(uf_ref, up_lo_ref, prev_lo_ref, next_lo_ref, up_hi_ref, prev_hi_ref, next_hi_ref,
                c0_ref, s0_ref, cph_ref, sph_ref, band_lo_ref, band_hi_ref, rev_ref,
                cw_ref, sw_ref, cast_src_ref, lo_ref, hi_ref, cast_dst_ref, lhs_ref):
    t = SEQ_TILE
    tp = DFT_ROWS
    gd = POOL_GROUP_DIM
    hd = FOURIER_HEAD_DIM

    @pl.when(pl.program_id(1) == 0)
    def _():
        cph = cph_ref[0]
        sph = sph_ref[0]
        for r0 in range(0, tp, TWIDDLE_ROWS):
            c0 = c0_ref[r0:r0 + TWIDDLE_ROWS, :]
            s0 = s0_ref[r0:r0 + TWIDDLE_ROWS, :]
            lhs_ref[r0:r0 + TWIDDLE_ROWS, :] = (cph * c0 - sph * s0).astype(BF16)
            lhs_ref[tp + r0:tp + r0 + TWIDDLE_ROWS, :] = (sph * c0 + cph * s0).astype(BF16)

    _cast_chunks([cast_src_ref], [cast_dst_ref])
    pq = _dot(lhs_ref[...], uf_ref[...])
    for h in range(N_FOURIER_HEADS):
        cols = slice(h * hd, (h + 1) * hd)
        out_cols = slice(POOL_WIDTH + h * hd, POOL_WIDTH + (h + 1) * hd)
        pc = _dot(pq[:tp, cols].astype(BF16), cw_ref[h])
        qs = _dot(pq[tp:, cols].astype(BF16), sw_ref[h])
        lo_ref[:, out_cols] = (pc + qs)[:t].astype(BF16)
        hi_ref[:, out_cols] = _dot(rev_ref[...], (pc - qs).astype(BF16)).astype(BF16)

    def pool(up_ref, prev_ref, next_ref, band_ref, out_ref):
        win = jnp.concatenate([prev_ref[...], up_ref[...], next_ref[...]], axis=0)
        for g in range(N_POOL_GROUPS):
            cols = slice(g * gd, (g + 1) * gd)
            out_ref[:, cols] = _dot(band_ref[0, g], win[:, cols]).astype(BF16)

    pool(up_lo_ref, prev_lo_ref, next_lo_ref, band_lo_ref, lo_ref)
    pool(up_hi_ref, prev_hi_ref, next_hi_ref, band_hi_ref, hi_ref)


def _mix(u3, cw, sw, cast_weight):
    b, s, d = u3.shape
    t = SEQ_TILE
    tp = DFT_ROWS
    halo = POOL_HALO
    n_tiles = s // t
    n_steps = n_tiles // 2
    halo_blocks_per_tile = t // halo
    last_halo_block = s // halo - 1

    c0, s0 = _twiddle(np.arange(tp), np.arange(s), s)
    cph, sph = _twiddle(np.arange(0, s // 2, t), np.arange(s), s)
    band = jnp.asarray(_pool_band(s, t, halo)).astype(BF16)
    rev = np.zeros((t, tp), np.float32)
    rev[np.arange(t), t - np.arange(t)] = 1.0

    hi_tile = lambda m: n_tiles - 1 - m
    tile_spec = lambda tile_of: pl.BlockSpec((None, t, POOL_WIDTH), lambda m, bi: (bi, tile_of(m), 0))
    prev_spec = lambda tile_of: pl.BlockSpec(
        (None, halo, POOL_WIDTH),
        lambda m, bi: (bi, jnp.maximum(tile_of(m) * halo_blocks_per_tile - 1, 0), 0))
    next_spec = lambda tile_of: pl.BlockSpec(
        (None, halo, POOL_WIDTH),
        lambda m, bi: (bi, jnp.minimum((tile_of(m) + 1) * halo_blocks_per_tile, last_halo_block), 0))
    band_shape = (1, N_POOL_GROUPS, t, t + 2 * halo)
    const2 = lambda shape: pl.BlockSpec(shape, lambda m, bi: (0, 0))
    const3 = lambda shape: pl.BlockSpec(shape, lambda m, bi: (0, 0, 0))
    lo_tile = lambda m: m

    in_specs = [
        pl.BlockSpec((None, s, FOURIER_WIDTH), lambda m, bi: (bi, 0, 1)),
        tile_spec(lo_tile), prev_spec(lo_tile), next_spec(lo_tile),
        tile_spec(hi_tile), prev_spec(hi_tile), next_spec(hi_tile),
        pl.BlockSpec((tp, s), lambda m, bi: (0, 0), pipeline_mode=pl.Buffered(1)),
        pl.BlockSpec((tp, s), lambda m, bi: (0, 0), pipeline_mode=pl.Buffered(1)),
        pl.BlockSpec((1, 1, s), lambda m, bi: (m, 0, 0)),
        pl.BlockSpec((1, 1, s), lambda m, bi: (m, 0, 0)),
        pl.BlockSpec(band_shape, lambda m, bi: (jnp.where(m == 0, 0, 1), 0, 0, 0)),
        pl.BlockSpec(band_shape, lambda m, bi: (jnp.where(m == 0, 2, 1), 0, 0, 0)),
        const2((t, tp)),
        const3((N_FOURIER_HEADS, FOURIER_HEAD_DIM, FOURIER_HEAD_DIM)),
        const3((N_FOURIER_HEADS, FOURIER_HEAD_DIM, FOURIER_HEAD_DIM)),
    ]
    half_out = pl.BlockSpec((None, t, d), lambda m, bi: (bi, m, 0))
    hi_out = pl.BlockSpec((None, t, d), lambda m, bi: (bi, n_steps - 1 - m, 0))
    cast = _CastAlong(cast_weight, n_steps * b, lambda m, bi: m * b + bi)
    vmem = (2 * s * FOURIER_WIDTH * 2
            + 2 * tp * s * 4
            + 2 * tp * s * 2
            + 2 * 2 * t * d * 2 * 2
            + 4 * 2 * tp * FOURIER_WIDTH * 4
            + cast.vmem_bytes)
    lo, hi, cast_out = pl.pallas_call(
        _mix_kernel,
        grid=(n_steps, b),
        in_specs=in_specs + [cast.spec],
        out_specs=[half_out, hi_out, cast.spec],
        out_shape=[jax.ShapeDtypeStruct((b, s // 2, d), BF16)] * 2 + [cast.out_shape],
        scratch_shapes=[pltpu.VMEM((2 * tp, s), BF16)],
        compiler_params=pltpu.CompilerParams(
            dimension_semantics=("arbitrary", "arbitrary"), vmem_limit_bytes=vmem + 4 * MIB),
        name="seq_mix",
    )(u3, u3, u3, u3, u3, u3, u3, jnp.asarray(c0), jnp.asarray(s0),
      jnp.asarray(cph).reshape(n_steps, 1, s), jnp.asarray(sph).reshape(n_steps, 1, s),
      band, band, jnp.asarray(rev).astype(BF16),
      cw, sw, cast.src)
    return lo, hi, cast_out.reshape(cast.shape)


def _out_proj_route_kernel(x_ref, mixed_lo_ref, mixed_hi_ref, wtop_ref, wbot_ref, g_ref, wr_ref, br_ref,
                           tri_ref, pick_ref, cast_src_ref, rows_ref, route_ref, counts_ref, cast_dst_ref,
                           carry_ref, x1_ref, *, tiles_per_seq, n_tiles):
    step = pl.program_id(0)

    @pl.when(step == 0)
    def _():
        carry_ref[...] = jnp.zeros_like(carry_ref)
        x1_ref[...] = jnp.zeros_like(x1_ref)

    _cast_chunks([cast_src_ref], [cast_dst_ref])

    x1 = x1_ref[...]
    rows_ref[:, :D_MODEL] = x1
    h2 = _rmsnorm(x1, g_ref[...])
    logits = _dot(h2.astype(BF16), wr_ref[...]) + br_ref[...]

    tile = jnp.minimum(step, n_tiles - 1)
    in_lo_half = (tile % tiles_per_seq) < tiles_per_seq // 2
    mixed = jnp.where(in_lo_half, mixed_lo_ref[...], mixed_hi_ref[...])
    x1_ref[...] = (x_ref[...] + _dot(mixed[:, :POOL_WIDTH], wtop_ref[...])
                   + _dot(mixed[:, POOL_WIDTH:], wbot_ref[...]))

    lane = lax.broadcasted_iota(jnp.int32, logits.shape, 1)
    neg = jnp.float32(-jnp.inf)
    big = jnp.int32(ROUTER_LANES)

    is_group = lane < N_EXPERT_GROUPS
    gl = jnp.where(is_group, logits, neg)
    gmax = jnp.max(gl, axis=-1, keepdims=True)
    gidx = jnp.min(jnp.where(gl == gmax, lane, big), axis=-1, keepdims=True)
    p_g = 1.0 / jnp.sum(jnp.exp(gl - gmax), axis=-1, keepdims=True)

    e_lane = lane - N_EXPERT_GROUPS
    in_group = (e_lane >= gidx * EXPERTS_PER_GROUP) & (e_lane < (gidx + 1) * EXPERTS_PER_GROUP)
    el = jnp.where(in_group, logits, neg)
    v1 = jnp.max(el, axis=-1, keepdims=True)
    i1 = jnp.min(jnp.where(el == v1, lane, big), axis=-1, keepdims=True)
    el2 = jnp.where(lane == i1, neg, el)
    v2 = jnp.max(el2, axis=-1, keepdims=True)
    i2 = jnp.min(jnp.where(el2 == v2, lane, big), axis=-1, keepdims=True)
    r = jnp.exp(v2 - v1)
    w1 = p_g / (1.0 + r)
    w2 = p_g * r / (1.0 + r)

    first_is_low = i1 < i2
    first_lane = N_EXPERT_GROUPS + gidx * EXPERTS_PER_GROUP
    la = jnp.where(first_is_low, i1, i2) - first_lane
    lb = jnp.where(first_is_low, i2, i1) - first_lane
    w_low = jnp.where(first_is_low, w1, w2)
    w_high = jnp.where(first_is_low, w2, w1)
    pair = jnp.where(la == 0, lb - 1, jnp.where(la == 1, 6 - lb, 5))
    slot_a_is_high = la == 2
    w_a = jnp.where(slot_a_is_high, w_high, w_low)
    w_b = jnp.where(slot_a_is_high, w_low, w_high)
    cls = gidx * PAIRS_PER_GROUP + pair
    rows_ref[:, D_MODEL:] = jnp.where(lane == 0, w_a, jnp.where(lane == 1, w_b, 0.0))

    onehot = jnp.where(lane == cls, 1.0, 0.0)
    before = _dot(tri_ref[...], onehot.astype(BF16)) + carry_ref[...]
    rank = jnp.sum(jnp.where(lane == cls, before, 0.0), axis=-1, keepdims=True)
    carry_ref[...] += jnp.sum(onehot, axis=0, keepdims=True) * jnp.where(step > 0, 1.0, 0.0)
    counts_ref[...] = carry_ref[...]

    rank_hi = jnp.floor(rank * (1.0 / RANK_RADIX))
    rank_lo = rank - rank_hi * RANK_RADIX
    digits = jnp.where(lane == 0, cls.astype(F32),
                       jnp.where(lane == 1, rank_hi, jnp.where(lane == 2, rank_lo, 0.0)))
    route_ref[...] = lax.dot_general(pick_ref[...], digits.astype(BF16), (((1,), (1,)), ((), ())),
                                     preferred_element_type=F32)


def _out_proj_route(x2, mixed_lo, mixed_hi, seq_len, w_top_bf16, w_out_bf16, g_ffn, wr, br, cast_weight):
    n, d = x2.shape
    tm = TOKEN_TILE
    n_tiles = n // tm
    tiles_per_seq = seq_len // tm
    half_tiles = tiles_per_seq // 2
    const = lambda shape: pl.BlockSpec(shape, lambda i: (0, 0))
    in_tile = lambda i: jnp.minimum(i, n_tiles - 1)
    out_tile = lambda i: jnp.maximum(i - 1, 0)
    lo_spec = pl.BlockSpec((tm, d), lambda i: (
        (in_tile(i) // tiles_per_seq) * half_tiles + jnp.minimum(in_tile(i) % tiles_per_seq, half_tiles - 1), 0))
    hi_spec = pl.BlockSpec((tm, d), lambda i: (
        (in_tile(i) // tiles_per_seq) * half_tiles + jnp.maximum(in_tile(i) % tiles_per_seq - half_tiles, 0), 0))
    tri = np.tril(np.ones((tm, tm), np.float32), -1)
    pick = np.eye(SUBLANES, ROUTER_LANES, dtype=np.float32)
    vmem = (2 * tm * d * 4 + 4 * tm * d * 2 + 2 * tm * ROW_WORDS * 4 + d * d * 2 + tm * d * 4
            + 2 * d * ROUTER_LANES * 2 + 2 * tm * tm * 2 + 3 * tm * d * 4)
    cast = _CastAlong(cast_weight, n_tiles, in_tile)
    vmem += cast.vmem_bytes
    rows, route, counts, cast_out = pl.pallas_call(
        functools.partial(_out_proj_route_kernel, tiles_per_seq=tiles_per_seq, n_tiles=n_tiles),
        grid=(n_tiles + 1,),
        in_specs=[pl.BlockSpec((tm, d), lambda i: (in_tile(i), 0)), lo_spec, hi_spec,
                  pl.BlockSpec((POOL_WIDTH, d), lambda i: (0, 0), pipeline_mode=pl.Buffered(1)),
                  pl.BlockSpec((FOURIER_WIDTH, d), lambda i: (1, 0), pipeline_mode=pl.Buffered(1)),
                  const((1, d)), const((d, ROUTER_LANES)), const((1, ROUTER_LANES)),
                  const((tm, tm)), const((SUBLANES, ROUTER_LANES)), cast.spec],
        out_specs=[pl.BlockSpec((tm, ROW_WORDS), lambda i: (out_tile(i), 0)),
                   pl.BlockSpec((SUBLANES, tm), lambda i: (0, out_tile(i))),
                   const((1, ROUTER_LANES)), cast.spec],
        out_shape=[jax.ShapeDtypeStruct((n, ROW_WORDS), F32),
                   jax.ShapeDtypeStruct((SUBLANES, n), F32),
                   jax.ShapeDtypeStruct((1, ROUTER_LANES), F32), cast.out_shape],
        scratch_shapes=[pltpu.VMEM((1, ROUTER_LANES), F32), pltpu.VMEM((tm, d), F32)],
        compiler_params=pltpu.CompilerParams(
            dimension_semantics=("arbitrary",), vmem_limit_bytes=vmem + 4 * MIB),
        name="out_proj_route",
    )(x2, mixed_lo, mixed_hi, w_top_bf16, w_out_bf16, g_ffn.reshape(1, d), wr, br,
      jnp.asarray(tri).astype(BF16), jnp.asarray(pick).astype(BF16), cast.src)
    return rows, route, counts, cast_out.reshape(cast.shape)


def _row_copy(src_ref, src_row, dst_ref, dst_row, sem):
    return pltpu.make_async_copy(src_ref.at[pl.ds(src_row, 1), :], dst_ref.at[pl.ds(dst_row, 1), :], sem)


def _plan_kernel(cnt_ref, route_ref, tok_ref, start_ref, ea_ref, eb_ref, lo_ref, hi_ref, nused_ref,
                 cstart_ref, pos_vmem_ref, pos_smem_ref, sem):
    tm = EXPERT_TILE
    shift = tm.bit_length() - 1
    n = tok_ref.shape[0]
    n_items = start_ref.shape[0]

    run = jnp.int32(0)
    inside = jnp.bool_(True)
    for c in range(N_CLASSES):
        cnt = cnt_ref[c]
        cstart_ref[c] = run
        inside = inside & ((cnt == 0) | (cnt >= tm))
        run = run + cnt

    cls_row = route_ref[0:1, :]
    slot_row = route_ref[1:2, :] * RANK_RADIX + route_ref[2:3, :]
    for c in range(N_CLASSES):
        slot_row = slot_row + jnp.where(cls_row == c, cstart_ref[c].astype(F32), 0.0)
    pos_vmem_ref[...] = slot_row.astype(jnp.int32)
    to_smem = pltpu.make_async_copy(pos_vmem_ref, pos_smem_ref, sem)
    to_smem.start()

    k = jnp.int32(0)
    for c in range(N_CLASSES):
        cnt = cnt_ref[c]
        cs = cstart_ref[c]
        ce = cs + cnt
        group, pair = divmod(c, PAIRS_PER_GROUP)
        ea = group * EXPERTS_PER_GROUP + PAIR_SLOT_A[pair]
        eb = group * EXPERTS_PER_GROUP + PAIR_SLOT_B[pair]
        first_window = cs >> shift
        n_inside = (cnt + tm - 1) >> shift
        n_aligned = jnp.where(cnt > 0, ((ce - 1) >> shift) - first_window + 1, 0)
        full_windows = cnt >> shift

        def emit(i, carry, k=k, cs=cs, ce=ce, ea=ea, eb=eb, first_window=first_window,
                 full_windows=full_windows):
            start_inside = jnp.where(i < full_windows, cs + i * tm, ce - tm)
            start_aligned = (first_window + i) * tm
            start_ref[k + i] = jnp.where(inside, start_inside, start_aligned)
            ea_ref[k + i] = ea
            eb_ref[k + i] = eb
            lo_ref[k + i] = jnp.where(inside, 0, jnp.clip(cs - start_aligned, 0, tm))
            hi_ref[k + i] = jnp.where(inside, tm, jnp.clip(ce - start_aligned, 0, tm))
            return carry

        n_class_items = jnp.where(inside, n_inside, n_aligned)
        lax.fori_loop(0, n_class_items, emit, 0)
        k = k + n_class_items
    nused_ref[0] = k

    def repeat_last(i, carry):
        for ref in (start_ref, ea_ref, eb_ref, lo_ref, hi_ref):
            ref[i] = ref[k - 1]
        return carry

    lax.fori_loop(k, n_items, repeat_last, 0)

    to_smem.wait()

    def invert(t, carry):
        tok_ref[pos_smem_ref[0, t]] = t
        return carry

    lax.fori_loop(0, n, invert, 0, unroll=INVERT_UNROLL)


def _plan_routing(route, counts, n):
    n_items = n // EXPERT_TILE + N_CLASSES
    smem = lambda size: jax.ShapeDtypeStruct((size,), jnp.int32)
    smem_spec = pl.BlockSpec(memory_space=pltpu.SMEM)
    return pl.pallas_call(
        _plan_kernel,
        in_specs=[smem_spec, pl.BlockSpec(memory_space=pltpu.VMEM)],
        out_specs=[smem_spec] * 7,
        out_shape=[smem(n)] + [smem(n_items)] * 5 + [smem(1)],
        scratch_shapes=[pltpu.SMEM((N_CLASSES,), jnp.int32), pltpu.VMEM((1, n), jnp.int32),
                        pltpu.SMEM((1, n), jnp.int32), pltpu.SemaphoreType.DMA(())],
        name="routing_plan",
    )(counts[0, :N_CLASSES].astype(jnp.int32), route)


def _expert_pair_kernel(start_ref, ea_ref, eb_ref, lo_ref, hi_ref, nused_ref, tok_ref,
                        rows_hbm_ref, gffn_ref, gfin_ref,
                        wga_ref, wua_ref, wda_ref, wgb_ref, wub_ref, wdb_ref, out_hbm_ref,
                        buf_ref, acc_ref, gather_sem, scatter_sem):
    del ea_ref, eb_ref
    tm = EXPERT_TILE
    j = pl.program_id(0)
    last = nused_ref[0] - 1
    slot = j % 2
    other = 1 - slot

    def start_gather(item, dst_slot, rows=range(tm)):
        base = start_ref[item]
        for r in rows:
            _row_copy(rows_hbm_ref, tok_ref[base + r], buf_ref.at[dst_slot], r, gather_sem.at[dst_slot]).start()

    def wait_gather(dst_slot):
        pltpu.make_async_copy(rows_hbm_ref.at[pl.ds(0, tm), :], buf_ref.at[dst_slot],
                              gather_sem.at[dst_slot]).wait()

    def start_scatter(item, src_slot, rows=range(tm)):
        base = start_ref[item]
        for r in rows:
            _row_copy(acc_ref.at[src_slot], r, out_hbm_ref, tok_ref[base + r], scatter_sem).start()

    def wait_scatter():
        pltpu.make_async_copy(acc_ref.at[0], out_hbm_ref.at[pl.ds(0, tm), :], scatter_sem).wait()

    gslot = j % GATHER_SLOTS

    @pl.when(j == 0)
    def _():
        acc_ref[...] = jnp.zeros_like(acc_ref)
        start_gather(0, 0)
        start_gather(jnp.minimum(1, last), 1)
        start_scatter(0, 1)

    @pl.when(j <= last)
    def _():
        wait_scatter()
        wait_gather(gslot)
        prev_item = jnp.maximum(j - 1, 0)
        ahead_item = jnp.minimum(j + 2, last)

        def issue_copies(part):
            half_parts = DMA_PARTS // 2
            rows = range((part % half_parts) * tm // half_parts, (part % half_parts + 1) * tm // half_parts)
            if part < half_parts:
                start_scatter(prev_item, other, rows)
            else:
                start_gather(ahead_item, (j + 2) % GATHER_SLOTS, rows)

        lo = lo_ref[j]
        hi = hi_ref[j]
        x1 = buf_ref[gslot, :, :D_MODEL]
        xg = (x1 * gffn_ref[...]).astype(BF16)
        inv_rms = lax.rsqrt(jnp.mean(x1 * x1, axis=-1, keepdims=True) + RMS_EPS)

        part = 0
        ff_chunk = EXPERT_FF * 4 // DMA_PARTS
        for slot_lane, (wg_ref, wu_ref, wd_ref) in enumerate(((wga_ref, wua_ref, wda_ref),
                                                              (wgb_ref, wub_ref, wdb_ref))):
            for c0 in range(0, EXPERT_FF, ff_chunk):
                issue_copies(part)
                wts = buf_ref[gslot, :, D_MODEL:]
                lane = lax.broadcasted_iota(jnp.int32, wts.shape, 1)
                w = jnp.sum(jnp.where(lane == slot_lane, wts, 0.0), axis=-1, keepdims=True)
                a = _dot(xg, wg_ref[0, :, c0:c0 + ff_chunk]) * inv_rms
                v = _dot(xg, wu_ref[0, :, c0:c0 + ff_chunk]) * (inv_rms * w)
                act = (a * (1.0 / (1.0 + jnp.exp(-a))) * v).astype(BF16)
                for half, d0 in enumerate(range(0, D_MODEL, D_MODEL // 2)):
                    cols = slice(d0, d0 + D_MODEL // 2)
                    if half == 1:
                        issue_copies(part + 1)
                    y_part = _dot(act, wd_ref[0, c0:c0 + ff_chunk, cols])
                    if part == 0:
                        acc_ref[slot, :, cols] = y_part
                    else:
                        acc_ref[slot, :, cols] += y_part
                part += 2
        res = _rmsnorm(buf_ref[gslot, :, :D_MODEL] + acc_ref[slot], gfin_ref[...])

        row = lax.broadcasted_iota(jnp.int32, (tm, 1), 0)
        mine = (row >= lo) & (row < hi)
        acc_ref[slot] = jnp.where(mine, res, jnp.where(lo > 0, acc_ref[other], 0.0))

    @pl.when(j == last)
    def _():
        wait_scatter()
        start_scatter(j, slot)
        wait_scatter()
        wait_gather((j + 1) % GATHER_SLOTS)
        wait_gather((j + 2) % GATHER_SLOTS)


def _expert_pairs(item_start, item_ea, item_eb, item_lo, item_hi, n_used, slot_token, rows,
                  g_ffn, g_final, wg, wu, wd):
    n, w = rows.shape
    d, f = D_MODEL, EXPERT_FF
    tm = EXPERT_TILE
    gate_a = pl.BlockSpec((1, d, f), lambda j, st, ea, eb, lo, hi, nu, tok: (ea[j], 0, 0))
    gate_b = pl.BlockSpec((1, d, f), lambda j, st, ea, eb, lo, hi, nu, tok: (eb[j], 0, 0))
    down_a = pl.BlockSpec((1, f, d), lambda j, st, ea, eb, lo, hi, nu, tok: (ea[j], 0, 0))
    down_b = pl.BlockSpec((1, f, d), lambda j, st, ea, eb, lo, hi, nu, tok: (eb[j], 0, 0))
    gain = pl.BlockSpec((1, d), lambda j, st, ea, eb, lo, hi, nu, tok: (0, 0))
    grid_spec = pltpu.PrefetchScalarGridSpec(
        num_scalar_prefetch=7,
        grid=(item_start.shape[0],),
        in_specs=[pl.BlockSpec(memory_space=pl.ANY), gain, gain,
                  gate_a, gate_a, down_a, gate_b, gate_b, down_b],
        out_specs=pl.BlockSpec(memory_space=pl.ANY),
        scratch_shapes=[pltpu.VMEM((GATHER_SLOTS, tm, w), F32), pltpu.VMEM((2, tm, d), F32),
                        pltpu.SemaphoreType.DMA((GATHER_SLOTS,)), pltpu.SemaphoreType.DMA(())],
    )
    vmem = 2 * 6 * d * f * 2 + GATHER_SLOTS * tm * w * 4 + 2 * tm * d * 4 + 8 * tm * d * 4
    return pl.pallas_call(
        _expert_pair_kernel,
        grid_spec=grid_spec,
        out_shape=jax.ShapeDtypeStruct((n, d), F32),
        compiler_params=pltpu.CompilerParams(
            dimension_semantics=("arbitrary",), vmem_limit_bytes=vmem + 4 * MIB),
        name="expert_pairs",
    )(item_start, item_ea, item_eb, item_lo, item_hi, n_used, slot_token, rows, g_ffn.reshape(1, d),
      g_final.reshape(1, d), wg, wu, wd, wg, wu, wd)


def kernel(x, g_mix, w_in, w_pool, pool_scale, w_fourier, w_out, g_ffn, w_group_router,
           b_group_router, w_expert_router, b_expert_router, w_gate, w_up, w_down, g_final):
    b, s, d = x.shape
    assert d == D_MODEL and s % (2 * SEQ_TILE) == 0 and s % (2 * TOKEN_TILE) == 0
    assert (b * s) % max(TOKEN_TILE, EXPERT_TILE) == 0
    n = b * s
    x2 = x.reshape(n, d)

    cw, sw, w_top_bf16 = _mixer_weights(w_fourier, w_pool, pool_scale, w_out, s)
    u, (w_out_bf16, w_gate_bf16) = _norm_proj(x2, g_mix, w_in.astype(BF16), [w_out, w_gate])
    mixed_lo, mixed_hi, w_up_bf16 = _mix(u.reshape(b, s, d), cw, sw, w_up)

    wr = jnp.concatenate([w_group_router, w_expert_router], axis=1)
    wr = jnp.pad(wr, ((0, 0), (0, ROUTER_LANES - wr.shape[1]))).astype(BF16)
    br = jnp.concatenate([b_group_router, b_expert_router])
    br = jnp.pad(br, (0, ROUTER_LANES - br.shape[0])).reshape(1, ROUTER_LANES)

    rows, route, counts, w_down_bf16 = _out_proj_route(
        x2, mixed_lo.reshape(n // 2, d), mixed_hi.reshape(n // 2, d), s, w_top_bf16, w_out_bf16, g_ffn, wr, br,
        w_down)
    slot_token, item_start, item_ea, item_eb, item_lo, item_hi, n_used = _plan_routing(route, counts, n)
    out = _expert_pairs(item_start, item_ea, item_eb, item_lo, item_hi, n_used, slot_token, rows,
                        g_ffn, g_final, w_gate_bf16, w_up_bf16, w_down_bf16)
    return out.reshape(b, s, d)
```

```python
import functools

import numpy as np
import jax
import jax.numpy as jnp
from jax import lax
from jax.experimental import pallas as pl
from jax.experimental.pallas import tpu as pltpu

D_MODEL = 2048
POOL_WINDOWS = (2, 4, 8, 16)
N_POOL_GROUPS = len(POOL_WINDOWS)
POOL_WIDTH = D_MODEL // 2
POOL_GROUP_DIM = POOL_WIDTH // N_POOL_GROUPS
FOURIER_WIDTH = D_MODEL - POOL_WIDTH
N_FOURIER_HEADS = 4
FOURIER_HEAD_DIM = FOURIER_WIDTH // N_FOURIER_HEADS
N_EXPERT_GROUPS = 4
EXPERTS_PER_GROUP = 4
N_EXPERTS = N_EXPERT_GROUPS * EXPERTS_PER_GROUP
EXPERT_FF = D_MODEL // 4
RMS_EPS = 1e-6

LANES = 128
SUBLANES = 8
BF16_SUBLANES = 16
ROUTER_LANES = LANES
MIB = 1024 * 1024

TOKEN_TILE = 512
SEQ_TILE = 256
POOL_HALO = BF16_SUBLANES
DFT_ROWS = SEQ_TILE + BF16_SUBLANES
TWIDDLE_ROWS = 16

PAIR_SLOT_A = (0, 0, 0, 1, 1, 3)
PAIR_SLOT_B = (1, 2, 3, 3, 2, 2)
PAIRS_PER_GROUP = len(PAIR_SLOT_A)
N_CLASSES = N_EXPERT_GROUPS * PAIRS_PER_GROUP
ROW_WORDS = D_MODEL + LANES
RANK_RADIX = 128
EXPERT_TILE = 256
DMA_PARTS = 8
COPY_PLAN = {0: ("scatter", 0, 128), 2: ("scatter", 128, 64), 3: ("scatter", 192, 64),
             4: ("gather", 0, 64), 5: ("gather", 64, 64), "tail": ("gather", 128, 128)}
GATHER_SLOTS = 3
INVERT_UNROLL = 32

BF16 = jnp.bfloat16
F32 = jnp.float32


def _rmsnorm(x, g):
    ms = jnp.mean(x * x, axis=-1, keepdims=True)
    return x * lax.rsqrt(ms + RMS_EPS) * g


def _dot(a, b):
    return jnp.dot(a, b, preferred_element_type=F32)


def _twiddle(rows, cols, period):
    m = (np.asarray(rows, np.int64)[:, None] * np.asarray(cols, np.int64)[None, :]) % period
    ang = (2.0 * np.pi / period) * m.astype(np.float64)
    return np.cos(ang).astype(np.float32), np.sin(ang).astype(np.float32)


def _pool_band(seq_len, tile, halo):
    n_tiles = seq_len // tile
    out = np.zeros((3, N_POOL_GROUPS, tile, tile + 2 * halo), np.float64)
    for v, m in enumerate((0, 1, n_tiles - 1)):
        t0 = m * tile
        for g, k in enumerate(POOL_WINDOWS):
            for r in range(tile):
                t = t0 + r
                lo = max(t - (k - 1) // 2, 0)
                hi = min(t + k // 2 + 1, seq_len)
                out[v, g, r, lo - t0 + halo:hi - t0 + halo] = 1.0 / (hi - lo)
                out[v, g, r, r + halo] -= 1.0
    return out.astype(np.float32)


def _mixer_weight_kernel(cd_ref, sd_ref, wf_ref, wp_ref, ps_ref, wo_ref, cw_ref, sw_ref, wtop_ref, *, scale):
    wf = wf_ref[0]
    cw = jnp.dot(cd_ref[...], wf, preferred_element_type=F32, precision=lax.Precision.HIGHEST)
    sw = jnp.dot(sd_ref[...], wf, preferred_element_type=F32, precision=lax.Precision.HIGHEST)
    cw_ref[0] = (cw * scale).astype(BF16)
    sw_ref[0] = (sw * (-scale)).astype(BF16)
    wtop_ref[...] = _dot((wp_ref[0] * ps_ref[0]).astype(BF16), wo_ref[...].astype(BF16)).astype(BF16)


def _mixer_weights(w_fourier, w_pool, pool_scale, w_out, seq_len):
    dh = FOURIER_HEAD_DIM
    assert N_FOURIER_HEADS == N_POOL_GROUPS and dh == POOL_GROUP_DIM
    d = w_out.shape[1]
    cd, sd = _twiddle(np.arange(dh), np.arange(dh), dh)
    scale = 1.0 / np.sqrt(float(seq_len * dh))
    mat = pl.BlockSpec((dh, dh), lambda h: (0, 0))
    per_head = pl.BlockSpec((1, dh, dh), lambda h: (h, 0, 0))
    out_rows = pl.BlockSpec((dh, d), lambda h: (h, 0))
    return pl.pallas_call(
        functools.partial(_mixer_weight_kernel, scale=scale),
        grid=(N_FOURIER_HEADS,),
        in_specs=[mat, mat, per_head, per_head, pl.BlockSpec((1, 1, dh), lambda h: (h, 0, 0)), out_rows],
        out_specs=[per_head, per_head, out_rows],
        out_shape=[jax.ShapeDtypeStruct((N_FOURIER_HEADS, dh, dh), BF16)] * 2
        + [jax.ShapeDtypeStruct((POOL_WIDTH, d), BF16)],
        name="mixer_weights",
    )(jnp.asarray(cd), jnp.asarray(sd), w_fourier, w_pool, pool_scale.reshape(N_POOL_GROUPS, 1, dh), w_out)


class _CastAlong:
    def __init__(self, w, n_chunks, chunk_of):
        cols = w.shape[-1]
        rows = w.size // cols // n_chunks
        self.shape = w.shape
        self.src = w.reshape(n_chunks, rows, cols)
        self.spec = pl.BlockSpec((1, rows, cols), lambda *idx: (chunk_of(*idx), 0, 0))
        self.out_shape = jax.ShapeDtypeStruct((n_chunks, rows, cols), BF16)
        self.vmem_bytes = 2 * rows * cols * (4 + 2)


def _cast_chunks(src_refs, dst_refs):
    for src_ref, dst_ref in zip(src_refs, dst_refs):
        dst_ref[...] = src_ref[...].astype(BF16)


def _norm_proj_kernel(x_ref, g_ref, w_ref, *rest, n_cast):
    cast_src, (u_ref, *cast_dst) = rest[:n_cast], rest[n_cast:]
    h = _rmsnorm(x_ref[...], g_ref[...])
    u_ref[...] = _dot(h.astype(BF16), w_ref[...]).astype(BF16)
    _cast_chunks(cast_src, cast_dst)


def _norm_proj(x2, g_mix, w_in_bf16, cast_weights):
    n, d = x2.shape
    tm = TOKEN_TILE
    casts = [_CastAlong(w, n // tm, lambda i: i) for w in cast_weights]
    vmem = (2 * tm * d * 4 + d * d * 2 + 2 * tm * d * 2 + 3 * tm * d * 4
            + sum(c.vmem_bytes for c in casts))
    u, *cast_out = pl.pallas_call(
        functools.partial(_norm_proj_kernel, n_cast=len(casts)),
        grid=(n // tm,),
        in_specs=[
            pl.BlockSpec((tm, d), lambda i: (i, 0)),
            pl.BlockSpec((1, d), lambda i: (0, 0)),
            pl.BlockSpec((d, d), lambda i: (0, 0), pipeline_mode=pl.Buffered(1)),
        ] + [c.spec for c in casts],
        out_specs=[pl.BlockSpec((tm, d), lambda i: (i, 0))] + [c.spec for c in casts],
        out_shape=[jax.ShapeDtypeStruct((n, d), BF16)] + [c.out_shape for c in casts],
        compiler_params=pltpu.CompilerParams(
            dimension_semantics=("arbitrary",), vmem_limit_bytes=vmem + 4 * MIB),
        name="norm_proj",
    )(x2, g_mix.reshape(1, d), w_in_bf16, *[c.src for c in casts])
    return u, [o.reshape(c.shape) for o, c in zip(cast_out, casts)]


def _mix_kernel(uf_ref, up_lo_ref, prev_lo_ref, next_lo_ref, up_hi_ref, prev_hi_ref, next_hi_ref,
                c0_ref, s0_ref, cph_ref, sph_ref, band_lo_ref, band_hi_ref, rev_ref,
                cw_ref, sw_ref, cast_src_ref, lo_ref, hi_ref, cast_dst_ref, lhs_ref):
    t = SEQ_TILE
    tp = DFT_ROWS
    gd = POOL_GROUP_DIM
    hd = FOURIER_HEAD_DIM

    @pl.when(pl.program_id(1) == 0)
    def _():
        cph = cph_ref[0]
        sph = sph_ref[0]
        for r0 in range(0, tp, TWIDDLE_ROWS):
            c0 = c0_ref[r0:r0 + TWIDDLE_ROWS, :]
            s0 = s0_ref[r0:r0 + TWIDDLE_ROWS, :]
            lhs_ref[r0:r0 + TWIDDLE_ROWS, :] = (cph * c0 - sph * s0).astype(BF16)
            lhs_ref[tp + r0:tp + r0 + TWIDDLE_ROWS, :] = (sph * c0 + cph * s0).astype(BF16)

    _cast_chunks([cast_src_ref], [cast_dst_ref])
    pq = _dot(lhs_ref[...], uf_ref[...])
    for h in range(N_FOURIER_HEADS):
        cols = slice(h * hd, (h + 1) * hd)
        out_cols = slice(POOL_WIDTH + h * hd, POOL_WIDTH + (h + 1) * hd)
        pc = _dot(pq[:tp, cols].astype(BF16), cw_ref[h])
        qs = _dot(pq[tp:, cols].astype(BF16), sw_ref[h])
        lo_ref[:, out_cols] = (pc + qs)[:t].astype(BF16)
        hi_ref[:, out_cols] = _dot(rev_ref[...], (pc - qs).astype(BF16)).astype(BF16)

    def pool(up_ref, prev_ref, next_ref, band_ref, out_ref):
        win = jnp.concatenate([prev_ref[...], up_ref[...], next_ref[...]], axis=0)
        for g in range(N_POOL_GROUPS):
            cols = slice(g * gd, (g + 1) * gd)
            out_ref[:, cols] = _dot(band_ref[0, g], win[:, cols]).astype(BF16)

    pool(up_lo_ref, prev_lo_ref, next_lo_ref, band_lo_ref, lo_ref)
    pool(up_hi_ref, prev_hi_ref, next_hi_ref, band_hi_ref, hi_ref)


def _mix(u3, cw, sw, cast_weight):
    b, s, d = u3.shape
    t = SEQ_TILE
    tp = DFT_ROWS
    halo = POOL_HALO
    n_tiles = s // t
    n_steps = n_tiles // 2
    halo_blocks_per_tile = t // halo
    last_halo_block = s // halo - 1

    c0, s0 = _twiddle(np.arange(tp), np.arange(s), s)
    cph, sph = _twiddle(np.arange(0, s // 2, t), np.arange(s), s)
    band = jnp.asarray(_pool_band(s, t, halo)).astype(BF16)
    rev = np.zeros((t, tp), np.float32)
    rev[np.arange(t), t - np.arange(t)] = 1.0

    hi_tile = lambda m: n_tiles - 1 - m
    tile_spec = lambda tile_of: pl.BlockSpec((None, t, POOL_WIDTH), lambda m, bi: (bi, tile_of(m), 0))
    prev_spec = lambda tile_of: pl.BlockSpec(
        (None, halo, POOL_WIDTH),
        lambda m, bi: (bi, jnp.maximum(tile_of(m) * halo_blocks_per_tile - 1, 0), 0))
    next_spec = lambda tile_of: pl.BlockSpec(
        (None, halo, POOL_WIDTH),
        lambda m, bi: (bi, jnp.minimum((tile_of(m) + 1) * halo_blocks_per_tile, last_halo_block), 0))
    band_shape = (1, N_POOL_GROUPS, t, t + 2 * halo)
    const2 = lambda shape: pl.BlockSpec(shape, lambda m, bi: (0, 0))
    const3 = lambda shape: pl.BlockSpec(shape, lambda m, bi: (0, 0, 0))
    lo_tile = lambda m: m

    in_specs = [
        pl.BlockSpec((None, s, FOURIER_WIDTH), lambda m, bi: (bi, 0, 1)),
        tile_spec(lo_tile), prev_spec(lo_tile), next_spec(lo_tile),
        tile_spec(hi_tile), prev_spec(hi_tile), next_spec(hi_tile),
        pl.BlockSpec((tp, s), lambda m, bi: (0, 0), pipeline_mode=pl.Buffered(1)),
        pl.BlockSpec((tp, s), lambda m, bi: (0, 0), pipeline_mode=pl.Buffered(1)),
        pl.BlockSpec((1, 1, s), lambda m, bi: (m, 0, 0)),
        pl.BlockSpec((1, 1, s), lambda m, bi: (m, 0, 0)),
        pl.BlockSpec(band_shape, lambda m, bi: (jnp.where(m == 0, 0, 1), 0, 0, 0)),
        pl.BlockSpec(band_shape, lambda m, bi: (jnp.where(m == 0, 2, 1), 0, 0, 0)),
        const2((t, tp)),
        const3((N_FOURIER_HEADS, FOURIER_HEAD_DIM, FOURIER_HEAD_DIM)),
        const3((N_FOURIER_HEADS, FOURIER_HEAD_DIM, FOURIER_HEAD_DIM)),
    ]
    half_out = pl.BlockSpec((None, t, d), lambda m, bi: (bi, m, 0))
    hi_out = pl.BlockSpec((None, t, d), lambda m, bi: (bi, n_steps - 1 - m, 0))
    cast = _CastAlong(cast_weight, n_steps * b, lambda m, bi: m * b + bi)
    vmem = (2 * s * FOURIER_WIDTH * 2
            + 2 * tp * s * 4
            + 2 * tp * s * 2
            + 2 * 2 * t * d * 2 * 2
            + 4 * 2 * tp * FOURIER_WIDTH * 4
            + cast.vmem_bytes)
    lo, hi, cast_out = pl.pallas_call(
        _mix_kernel,
        grid=(n_steps, b),
        in_specs=in_specs + [cast.spec],
        out_specs=[half_out, hi_out, cast.spec],
        out_shape=[jax.ShapeDtypeStruct((b, s // 2, d), BF16)] * 2 + [cast.out_shape],
        scratch_shapes=[pltpu.VMEM((2 * tp, s), BF16)],
        compiler_params=pltpu.CompilerParams(
            dimension_semantics=("arbitrary", "arbitrary"), vmem_limit_bytes=vmem + 4 * MIB),
        name="seq_mix",
    )(u3, u3, u3, u3, u3, u3, u3, jnp.asarray(c0), jnp.asarray(s0),
      jnp.asarray(cph).reshape(n_steps, 1, s), jnp.asarray(sph).reshape(n_steps, 1, s),
      band, band, jnp.asarray(rev).astype(BF16),
      cw, sw, cast.src)
    return lo, hi, cast_out.reshape(cast.shape)


def _out_proj_route_kernel(x_ref, mixed_lo_ref, mixed_hi_ref, wtop_ref, wbot_ref, g_ref, wr_ref, br_ref,
                           tri_ref, pick_ref, cast_src_ref, rows_ref, route_ref, counts_ref, cast_dst_ref,
                           carry_ref, x1_ref, *, tiles_per_seq, n_tiles):
    step = pl.program_id(0)

    @pl.when(step == 0)
    def _():
        carry_ref[...] = jnp.zeros_like(carry_ref)
        x1_ref[...] = jnp.zeros_like(x1_ref)

    _cast_chunks([cast_src_ref], [cast_dst_ref])

    x1 = x1_ref[...]
    rows_ref[:, :D_MODEL] = x1
    h2 = _rmsnorm(x1, g_ref[...])
    logits = _dot(h2.astype(BF16), wr_ref[...]) + br_ref[...]

    tile = jnp.minimum(step, n_tiles - 1)
    in_lo_half = (tile % tiles_per_seq) < tiles_per_seq // 2
    mixed = jnp.where(in_lo_half, mixed_lo_ref[...], mixed_hi_ref[...])
    x1_ref[...] = (x_ref[...] + _dot(mixed[:, :POOL_WIDTH], wtop_ref[...])
                   + _dot(mixed[:, POOL_WIDTH:], wbot_ref[...]))

    lane = lax.broadcasted_iota(jnp.int32, logits.shape, 1)
    neg = jnp.float32(-jnp.inf)
    big = jnp.int32(ROUTER_LANES)

    is_group = lane < N_EXPERT_GROUPS
    gl = jnp.where(is_group, logits, neg)
    gmax = jnp.max(gl, axis=-1, keepdims=True)
    gidx = jnp.min(jnp.where(gl == gmax, lane, big), axis=-1, keepdims=True)
    p_g = 1.0 / jnp.sum(jnp.exp(gl - gmax), axis=-1, keepdims=True)

    e_lane = lane - N_EXPERT_GROUPS
    in_group = (e_lane >= gidx * EXPERTS_PER_GROUP) & (e_lane < (gidx + 1) * EXPERTS_PER_GROUP)
    el = jnp.where(in_group, logits, neg)
    v1 = jnp.max(el, axis=-1, keepdims=True)
    i1 = jnp.min(jnp.where(el == v1, lane, big), axis=-1, keepdims=True)
    el2 = jnp.where(lane == i1, neg, el)
    v2 = jnp.max(el2, axis=-1, keepdims=True)
    i2 = jnp.min(jnp.where(el2 == v2, lane, big), axis=-1, keepdims=True)
    r = jnp.exp(v2 - v1)
    w1 = p_g / (1.0 + r)
    w2 = p_g * r / (1.0 + r)

    first_is_low = i1 < i2
    first_lane = N_EXPERT_GROUPS + gidx * EXPERTS_PER_GROUP
    la = jnp.where(first_is_low, i1, i2) - first_lane
    lb = jnp.where(first_is_low, i2, i1) - first_lane
    w_low = jnp.where(first_is_low, w1, w2)
    w_high = jnp.where(first_is_low, w2, w1)
    pair = jnp.where(la == 0, lb - 1, jnp.where(la == 1, 6 - lb, 5))
    slot_a_is_high = la == 2
    w_a = jnp.where(slot_a_is_high, w_high, w_low)
    w_b = jnp.where(slot_a_is_high, w_low, w_high)
    cls = gidx * PAIRS_PER_GROUP + pair
    rows_ref[:, D_MODEL:] = jnp.where(lane == 0, w_a, jnp.where(lane == 1, w_b, 0.0))

    onehot = jnp.where(lane == cls, 1.0, 0.0)
    before = _dot(tri_ref[...], onehot.astype(BF16)) + carry_ref[...]
    rank = jnp.sum(jnp.where(lane == cls, before, 0.0), axis=-1, keepdims=True)
    carry_ref[...] += jnp.sum(onehot, axis=0, keepdims=True) * jnp.where(step > 0, 1.0, 0.0)
    counts_ref[...] = carry_ref[...]

    rank_hi = jnp.floor(rank * (1.0 / RANK_RADIX))
    rank_lo = rank - rank_hi * RANK_RADIX
    digits = jnp.where(lane == 0, cls.astype(F32),
                       jnp.where(lane == 1, rank_hi, jnp.where(lane == 2, rank_lo, 0.0)))
    route_ref[...] = lax.dot_general(pick_ref[...], digits.astype(BF16), (((1,), (1,)), ((), ())),
                                     preferred_element_type=F32)


def _out_proj_route(x2, mixed_lo, mixed_hi, seq_len, w_top_bf16, w_out_bf16, g_ffn, wr, br, cast_weight):
    n, d = x2.shape
    tm = TOKEN_TILE
    n_tiles = n // tm
    tiles_per_seq = seq_len // tm
    half_tiles = tiles_per_seq // 2
    const = lambda shape: pl.BlockSpec(shape, lambda i: (0, 0))
    in_tile = lambda i: jnp.minimum(i, n_tiles - 1)
    out_tile = lambda i: jnp.maximum(i - 1, 0)
    lo_spec = pl.BlockSpec((tm, d), lambda i: (
        (in_tile(i) // tiles_per_seq) * half_tiles + jnp.minimum(in_tile(i) % tiles_per_seq, half_tiles - 1), 0))
    hi_spec = pl.BlockSpec((tm, d), lambda i: (
        (in_tile(i) // tiles_per_seq) * half_tiles + jnp.maximum(in_tile(i) % tiles_per_seq - half_tiles, 0), 0))
    tri = np.tril(np.ones((tm, tm), np.float32), -1)
    pick = np.eye(SUBLANES, ROUTER_LANES, dtype=np.float32)
    vmem = (2 * tm * d * 4 + 4 * tm * d * 2 + 2 * tm * ROW_WORDS * 4 + d * d * 2 + tm * d * 4
            + 2 * d * ROUTER_LANES * 2 + 2 * tm * tm * 2 + 3 * tm * d * 4)
    cast = _CastAlong(cast_weight, n_tiles, in_tile)
    vmem += cast.vmem_bytes
    rows, route, counts, cast_out = pl.pallas_call(
        functools.partial(_out_proj_route_kernel, tiles_per_seq=tiles_per_seq, n_tiles=n_tiles),
        grid=(n_tiles + 1,),
        in_specs=[pl.BlockSpec((tm, d), lambda i: (in_tile(i), 0)), lo_spec, hi_spec,
                  pl.BlockSpec((POOL_WIDTH, d), lambda i: (0, 0), pipeline_mode=pl.Buffered(1)),
                  pl.BlockSpec((FOURIER_WIDTH, d), lambda i: (1, 0), pipeline_mode=pl.Buffered(1)),
                  const((1, d)), const((d, ROUTER_LANES)), const((1, ROUTER_LANES)),
                  const((tm, tm)), const((SUBLANES, ROUTER_LANES)), cast.spec],
        out_specs=[pl.BlockSpec((tm, ROW_WORDS), lambda i: (out_tile(i), 0)),
                   pl.BlockSpec((SUBLANES, tm), lambda i: (0, out_tile(i))),
                   const((1, ROUTER_LANES)), cast.spec],
        out_shape=[jax.ShapeDtypeStruct((n, ROW_WORDS), F32),
                   jax.ShapeDtypeStruct((SUBLANES, n), F32),
                   jax.ShapeDtypeStruct((1, ROUTER_LANES), F32), cast.out_shape],
        scratch_shapes=[pltpu.VMEM((1, ROUTER_LANES), F32), pltpu.VMEM((tm, d), F32)],
        compiler_params=pltpu.CompilerParams(
            dimension_semantics=("arbitrary",), vmem_limit_bytes=vmem + 4 * MIB),
        name="out_proj_route",
    )(x2, mixed_lo, mixed_hi, w_top_bf16, w_out_bf16, g_ffn.reshape(1, d), wr, br,
      jnp.asarray(tri).astype(BF16), jnp.asarray(pick).astype(BF16), cast.src)
    return rows, route, counts, cast_out.reshape(cast.shape)


def _row_copy(src_ref, src_row, dst_ref, dst_row, sem):
    return pltpu.make_async_copy(src_ref.at[pl.ds(src_row, 1), :], dst_ref.at[pl.ds(dst_row, 1), :], sem)


def _plan_kernel(cnt_ref, route_ref, tok_ref, start_ref, ea_ref, eb_ref, lo_ref, hi_ref, nused_ref,
                 cstart_ref, pos_vmem_ref, pos_smem_ref, sem):
    tm = EXPERT_TILE
    shift = tm.bit_length() - 1
    n = tok_ref.shape[0]
    n_items = start_ref.shape[0]

    run = jnp.int32(0)
    inside = jnp.bool_(True)
    for c in range(N_CLASSES):
        cnt = cnt_ref[c]
        cstart_ref[c] = run
        inside = inside & ((cnt == 0) | (cnt >= tm))
        run = run + cnt

    cls_row = route_ref[0:1, :]
    slot_row = route_ref[1:2, :] * RANK_RADIX + route_ref[2:3, :]
    for c in range(N_CLASSES):
        slot_row = slot_row + jnp.where(cls_row == c, cstart_ref[c].astype(F32), 0.0)
    pos_vmem_ref[...] = slot_row.astype(jnp.int32)
    to_smem = pltpu.make_async_copy(pos_vmem_ref, pos_smem_ref, sem)
    to_smem.start()

    k = jnp.int32(0)
    for c in range(N_CLASSES):
        cnt = cnt_ref[c]
        cs = cstart_ref[c]
        ce = cs + cnt
        group, pair = divmod(c, PAIRS_PER_GROUP)
        ea = group * EXPERTS_PER_GROUP + PAIR_SLOT_A[pair]
        eb = group * EXPERTS_PER_GROUP + PAIR_SLOT_B[pair]
        first_window = cs >> shift
        n_inside = (cnt + tm - 1) >> shift
        n_aligned = jnp.where(cnt > 0, ((ce - 1) >> shift) - first_window + 1, 0)
        full_windows = cnt >> shift

        def emit(i, carry, k=k, cs=cs, ce=ce, ea=ea, eb=eb, first_window=first_window,
                 full_windows=full_windows):
            start_inside = jnp.where(i < full_windows, cs + i * tm, ce - tm)
            start_aligned = (first_window + i) * tm
            start_ref[k + i] = jnp.where(inside, start_inside, start_aligned)
            ea_ref[k + i] = ea
            eb_ref[k + i] = eb
            lo_ref[k + i] = jnp.where(inside, 0, jnp.clip(cs - start_aligned, 0, tm))
            hi_ref[k + i] = jnp.where(inside, tm, jnp.clip(ce - start_aligned, 0, tm))
            return carry

        n_class_items = jnp.where(inside, n_inside, n_aligned)
        lax.fori_loop(0, n_class_items, emit, 0)
        k = k + n_class_items
    nused_ref[0] = k

    def repeat_last(i, carry):
        for ref in (start_ref, ea_ref, eb_ref, lo_ref, hi_ref):
            ref[i] = ref[k - 1]
        return carry

    lax.fori_loop(k, n_items, repeat_last, 0)

    to_smem.wait()

    def invert(t, carry):
        tok_ref[pos_smem_ref[0, t]] = t
        return carry

    lax.fori_loop(0, n, invert, 0, unroll=INVERT_UNROLL)


def _plan_routing(route, counts, n):
    n_items = n // EXPERT_TILE + N_CLASSES
    smem = lambda size: jax.ShapeDtypeStruct((size,), jnp.int32)
    smem_spec = pl.BlockSpec(memory_space=pltpu.SMEM)
    return pl.pallas_call(
        _plan_kernel,
        in_specs=[smem_spec, pl.BlockSpec(memory_space=pltpu.VMEM)],
        out_specs=[smem_spec] * 7,
        out_shape=[smem(n)] + [smem(n_items)] * 5 + [smem(1)],
        scratch_shapes=[pltpu.SMEM((N_CLASSES,), jnp.int32), pltpu.VMEM((1, n), jnp.int32),
                        pltpu.SMEM((1, n), jnp.int32), pltpu.SemaphoreType.DMA(())],
        name="routing_plan",
    )(counts[0, :N_CLASSES].astype(jnp.int32), route)


def _expert_pair_kernel(start_ref, ea_ref, eb_ref, lo_ref, hi_ref, nused_ref, tok_ref,
                        rows_hbm_ref, gffn_ref, gfin_ref,
                        wga_ref, wua_ref, wda_ref, wgb_ref, wub_ref, wdb_ref, out_hbm_ref,
                        buf_ref, acc_ref, gather_sem, scatter_sem):
    del ea_ref, eb_ref
    tm = EXPERT_TILE
    j = pl.program_id(0)
    last = nused_ref[0] - 1
    slot = j % 2
    other = 1 - slot

    def start_gather(item, dst_slot, rows=range(tm)):
        base = start_ref[item]
        for r in rows:
            _row_copy(rows_hbm_ref, tok_ref[base + r], buf_ref.at[dst_slot], r, gather_sem.at[dst_slot]).start()

    def wait_gather(dst_slot):
        pltpu.make_async_copy(rows_hbm_ref.at[pl.ds(0, tm), :], buf_ref.at[dst_slot],
                              gather_sem.at[dst_slot]).wait()

    def start_scatter(item, src_slot, rows=range(tm)):
        base = start_ref[item]
        for r in rows:
            _row_copy(acc_ref.at[src_slot], r, out_hbm_ref, tok_ref[base + r], scatter_sem).start()

    def wait_scatter():
        pltpu.make_async_copy(acc_ref.at[0], out_hbm_ref.at[pl.ds(0, tm), :], scatter_sem).wait()

    gslot = j % GATHER_SLOTS

    @pl.when(j == 0)
    def _():
        acc_ref[...] = jnp.zeros_like(acc_ref)
        start_gather(0, 0)
        start_gather(jnp.minimum(1, last), 1)
        start_scatter(0, 1)

    @pl.when(j <= last)
    def _():
        wait_scatter()
        wait_gather(gslot)
        prev_item = jnp.maximum(j - 1, 0)
        ahead_item = jnp.minimum(j + 2, last)

        def issue_copies(point):
            if point not in COPY_PLAN:
                return
            kind, first_row, n_rows = COPY_PLAN[point]
            rows = range(first_row, first_row + n_rows)
            if kind == "scatter":
                start_scatter(prev_item, other, rows)
            else:
                start_gather(ahead_item, (j + 2) % GATHER_SLOTS, rows)

        lo = lo_ref[j]
        hi = hi_ref[j]
        x1 = buf_ref[gslot, :, :D_MODEL]
        h = _rmsnorm(x1, gffn_ref[...]).astype(BF16)

        part = 0
        ff_chunk = EXPERT_FF * 4 // DMA_PARTS
        for slot_lane, (wg_ref, wu_ref, wd_ref) in enumerate(((wga_ref, wua_ref, wda_ref),
                                                              (wgb_ref, wub_ref, wdb_ref))):
            for c0 in range(0, EXPERT_FF, ff_chunk):
                issue_copies(part)
                wts = buf_ref[gslot, :, D_MODEL:]
                lane = lax.broadcasted_iota(jnp.int32, wts.shape, 1)
                w = jnp.sum(jnp.where(lane == slot_lane, wts, 0.0), axis=-1, keepdims=True)
                a = _dot(h, wg_ref[0, :, c0:c0 + ff_chunk])
                v = _dot(h, wu_ref[0, :, c0:c0 + ff_chunk])
                act = (a * (1.0 / (1.0 + jnp.exp(-a))) * v * w).astype(BF16)
                for half, d0 in enumerate(range(0, D_MODEL, D_MODEL // 2)):
                    cols = slice(d0, d0 + D_MODEL // 2)
                    if half == 1:
                        issue_copies(part + 1)
                    y_part = _dot(act, wd_ref[0, c0:c0 + ff_chunk, cols])
                    if part == 0:
                        acc_ref[slot, :, cols] = y_part
                    else:
                        acc_ref[slot, :, cols] += y_part
                part += 2
        res = _rmsnorm(buf_ref[gslot, :, :D_MODEL] + acc_ref[slot], gfin_ref[...])
        issue_copies("tail")

        row = lax.broadcasted_iota(jnp.int32, (tm, 1), 0)
        mine = (row >= lo) & (row < hi)
        acc_ref[slot] = jnp.where(mine, res, jnp.where(lo > 0, acc_ref[other], 0.0))

    @pl.when(j == last)
    def _():
        wait_scatter()
        start_scatter(j, slot)
        wait_scatter()
        wait_gather((j + 1) % GATHER_SLOTS)
        wait_gather((j + 2) % GATHER_SLOTS)


def _expert_pairs(item_start, item_ea, item_eb, item_lo, item_hi, n_used, slot_token, rows,
                  g_ffn, g_final, wg, wu, wd):
    n, w = rows.shape
    d, f = D_MODEL, EXPERT_FF
    tm = EXPERT_TILE
    gate_a = pl.BlockSpec((1, d, f), lambda j, st, ea, eb, lo, hi, nu, tok: (ea[j], 0, 0))
    gate_b = pl.BlockSpec((1, d, f), lambda j, st, ea, eb, lo, hi, nu, tok: (eb[j], 0, 0))
    down_a = pl.BlockSpec((1, f, d), lambda j, st, ea, eb, lo, hi, nu, tok: (ea[j], 0, 0))
    down_b = pl.BlockSpec((1, f, d), lambda j, st, ea, eb, lo, hi, nu, tok: (eb[j], 0, 0))
    gain = pl.BlockSpec((1, d), lambda j, st, ea, eb, lo, hi, nu, tok: (0, 0))
    grid_spec = pltpu.PrefetchScalarGridSpec(
        num_scalar_prefetch=7,
        grid=(item_start.shape[0],),
        in_specs=[pl.BlockSpec(memory_space=pl.ANY), gain, gain,
                  gate_a, gate_a, down_a, gate_b, gate_b, down_b],
        out_specs=pl.BlockSpec(memory_space=pl.ANY),
        scratch_shapes=[pltpu.VMEM((GATHER_SLOTS, tm, w), F32), pltpu.VMEM((2, tm, d), F32),
                        pltpu.SemaphoreType.DMA((GATHER_SLOTS,)), pltpu.SemaphoreType.DMA(())],
    )
    vmem = 2 * 6 * d * f * 2 + GATHER_SLOTS * tm * w * 4 + 2 * tm * d * 4 + 8 * tm * d * 4
    return pl.pallas_call(
        _expert_pair_kernel,
        grid_spec=grid_spec,
        out_shape=jax.ShapeDtypeStruct((n, d), F32),
        compiler_params=pltpu.CompilerParams(
            dimension_semantics=("arbitrary",), vmem_limit_bytes=vmem + 4 * MIB),
        name="expert_pairs",
    )(item_start, item_ea, item_eb, item_lo, item_hi, n_used, slot_token, rows, g_ffn.reshape(1, d),
      g_final.reshape(1, d), wg, wu, wd, wg, wu, wd)


def kernel(x, g_mix, w_in, w_pool, pool_scale, w_fourier, w_out, g_ffn, w_group_router,
           b_group_router, w_expert_router, b_expert_router, w_gate, w_up, w_down, g_final):
    b, s, d = x.shape
    assert d == D_MODEL and s % (2 * SEQ_TILE) == 0 and s % (2 * TOKEN_TILE) == 0
    assert (b * s) % max(TOKEN_TILE, EXPERT_TILE) == 0
    n = b * s
    x2 = x.reshape(n, d)

    cw, sw, w_top_bf16 = _mixer_weights(w_fourier, w_pool, pool_scale, w_out, s)
    u, (w_out_bf16, w_gate_bf16) = _norm_proj(x2, g_mix, w_in.astype(BF16), [w_out, w_gate])
    mixed_lo, mixed_hi, w_up_bf16 = _mix(u.reshape(b, s, d), cw, sw, w_up)

    wr = jnp.concatenate([w_group_router, w_expert_router], axis=1)
    wr = jnp.pad(wr, ((0, 0), (0, ROUTER_LANES - wr.shape[1]))).astype(BF16)
    br = jnp.concatenate([b_group_router, b_expert_router])
    br = jnp.pad(br, (0, ROUTER_LANES - br.shape[0])).reshape(1, ROUTER_LANES)

    rows, route, counts, w_down_bf16 = _out_proj_route(
        x2, mixed_lo.reshape(n // 2, d), mixed_hi.reshape(n // 2, d), s, w_top_bf16, w_out_bf16, g_ffn, wr, br,
        w_down)
    slot_token, item_start, item_ea, item_eb, item_lo, item_hi, n_used = _plan_routing(route, counts, n)
    out = _expert_pairs(item_start, item_ea, item_eb, item_lo, item_hi, n_used, slot_token, rows,
                        g_ffn, g_final, w_gate_bf16, w_up_bf16, w_down_bf16)
    return out.reshape(b, s, d)
```

```python
import functools

import numpy as np
import jax
import jax.numpy as jnp
from jax import lax
from jax.experimental import pallas as pl
from jax.experimental.pallas import tpu as pltpu

D_MODEL = 2048
POOL_WINDOWS = (2, 4, 8, 16)
N_POOL_GROUPS = len(POOL_WINDOWS)
POOL_WIDTH = D_MODEL // 2
POOL_GROUP_DIM = POOL_WIDTH // N_POOL_GROUPS
FOURIER_WIDTH = D_MODEL - POOL_WIDTH
N_FOURIER_HEADS = 4
FOURIER_HEAD_DIM = FOURIER_WIDTH // N_FOURIER_HEADS
N_EXPERT_GROUPS = 4
EXPERTS_PER_GROUP = 4
EXPERT_FF = D_MODEL // 4
RMS_EPS = 1e-6

LANES = 128
SUBLANES = 8
BF16_SUBLANES = 16
ROUTER_LANES = LANES
MIB = 1024 * 1024

TOKEN_TILE = 512
SEQ_TILE = 256
POOL_HALO = BF16_SUBLANES
DFT_ROWS = SEQ_TILE + BF16_SUBLANES
TWIDDLE_ROWS = 16

PAIR_SLOT_A = (0, 0, 0, 1, 1, 3)
PAIR_SLOT_B = (1, 2, 3, 3, 2, 2)
PAIRS_PER_GROUP = len(PAIR_SLOT_A)
N_CLASSES = N_EXPERT_GROUPS * PAIRS_PER_GROUP
ROW_WORDS = D_MODEL + LANES
RANK_RADIX = 128
EXPERT_TILE = 256
DMA_PARTS = 8
GATHER_SLOTS = 3
INVERT_UNROLL = 32
WEIGHT_CAST_ROWS = 256

BF16 = jnp.bfloat16
F32 = jnp.float32


def _rmsnorm(x, g):
    ms = jnp.mean(x * x, axis=-1, keepdims=True)
    return x * lax.rsqrt(ms + RMS_EPS) * g


def _dot(a, b):
    return jnp.dot(a, b, preferred_element_type=F32)


def _twiddle(rows, cols, period):
    m = (np.asarray(rows, np.int64)[:, None] * np.asarray(cols, np.int64)[None, :]) % period
    ang = (2.0 * np.pi / period) * m.astype(np.float64)
    return np.cos(ang).astype(np.float32), np.sin(ang).astype(np.float32)


def _pool_band(seq_len, tile, halo):
    n_tiles = seq_len // tile
    out = np.zeros((3, N_POOL_GROUPS, tile, tile + 2 * halo), np.float64)
    for v, m in enumerate((0, 1, n_tiles - 1)):
        t0 = m * tile
        for g, k in enumerate(POOL_WINDOWS):
            for r in range(tile):
                t = t0 + r
                lo = max(t - (k - 1) // 2, 0)
                hi = min(t + k // 2 + 1, seq_len)
                out[v, g, r, lo - t0 + halo:hi - t0 + halo] = 1.0 / (hi - lo)
                out[v, g, r, r + halo] -= 1.0
    return out.astype(np.float32)


def _mixer_weight_kernel(cd_ref, sd_ref, wf_ref, wp_ref, ps_ref, wo_ref, cw_ref, sw_ref, wtop_ref, *, scale):
    wf = wf_ref[0]
    cw = jnp.dot(cd_ref[...], wf, preferred_element_type=F32, precision=lax.Precision.HIGHEST)
    sw = jnp.dot(sd_ref[...], wf, preferred_element_type=F32, precision=lax.Precision.HIGHEST)
    cw_ref[0] = (cw * scale).astype(BF16)
    sw_ref[0] = (sw * (-scale)).astype(BF16)
    wtop_ref[...] = _dot((wp_ref[0] * ps_ref[0]).astype(BF16), wo_ref[...].astype(BF16)).astype(BF16)


def _mixer_weights(w_fourier, w_pool, pool_scale, w_out, seq_len):
    dh = FOURIER_HEAD_DIM
    assert N_FOURIER_HEADS == N_POOL_GROUPS and dh == POOL_GROUP_DIM
    d = w_out.shape[1]
    cd, sd = _twiddle(np.arange(dh), np.arange(dh), dh)
    scale = 1.0 / np.sqrt(float(seq_len * dh))
    mat = pl.BlockSpec((dh, dh), lambda h: (0, 0))
    per_head = pl.BlockSpec((1, dh, dh), lambda h: (h, 0, 0))
    out_rows = pl.BlockSpec((dh, d), lambda h: (h, 0))
    return pl.pallas_call(
        functools.partial(_mixer_weight_kernel, scale=scale),
        grid=(N_FOURIER_HEADS,),
        in_specs=[mat, mat, per_head, per_head, pl.BlockSpec((1, 1, dh), lambda h: (h, 0, 0)), out_rows],
        out_specs=[per_head, per_head, out_rows],
        out_shape=[jax.ShapeDtypeStruct((N_FOURIER_HEADS, dh, dh), BF16)] * 2
        + [jax.ShapeDtypeStruct((POOL_WIDTH, d), BF16)],
        name="mixer_weights",
    )(jnp.asarray(cd), jnp.asarray(sd), w_fourier, w_pool, pool_scale.reshape(N_POOL_GROUPS, 1, dh), w_out)


class _CastAlong:
    def __init__(self, w, n_chunks, chunk_of):
        cols = w.shape[-1]
        rows = w.size // cols // n_chunks
        self.shape = w.shape
        self.src = w.reshape(n_chunks, rows, cols)
        self.spec = pl.BlockSpec((1, rows, cols), lambda *idx: (chunk_of(*idx), 0, 0))
        self.out_shape = jax.ShapeDtypeStruct((n_chunks, rows, cols), BF16)
        self.vmem_bytes = 2 * rows * cols * (4 + 2)


def _cast_chunks(src_refs, dst_refs):
    for src_ref, dst_ref in zip(src_refs, dst_refs):
        dst_ref[...] = src_ref[...].astype(BF16)


def _norm_proj_kernel(x_ref, g_ref, w_ref, *rest, n_cast):
    cast_src, (u_ref, *cast_dst), w_bf16_ref = rest[:n_cast], rest[n_cast:-1], rest[-1]

    @pl.when(pl.program_id(0) == 0)
    def _():
        for r0 in range(0, w_ref.shape[0], WEIGHT_CAST_ROWS):
            w_bf16_ref[r0:r0 + WEIGHT_CAST_ROWS, :] = w_ref[r0:r0 + WEIGHT_CAST_ROWS, :].astype(BF16)

    h = _rmsnorm(x_ref[...], g_ref[...])
    u_ref[...] = _dot(h.astype(BF16), w_bf16_ref[...]).astype(BF16)
    _cast_chunks(cast_src, cast_dst)


def _norm_proj(x2, g_mix, w_in, cast_weights):
    n, d = x2.shape
    tm = TOKEN_TILE
    casts = [_CastAlong(w, n // tm, lambda i: i) for w in cast_weights]
    vmem = (2 * tm * d * 4 + d * d * (4 + 2) + 2 * tm * d * 2 + 3 * tm * d * 4
            + sum(c.vmem_bytes for c in casts))
    u, *cast_out = pl.pallas_call(
        functools.partial(_norm_proj_kernel, n_cast=len(casts)),
        grid=(n // tm,),
        in_specs=[
            pl.BlockSpec((tm, d), lambda i: (i, 0)),
            pl.BlockSpec((1, d), lambda i: (0, 0)),
            pl.BlockSpec((d, d), lambda i: (0, 0), pipeline_mode=pl.Buffered(1)),
        ] + [c.spec for c in casts],
        out_specs=[pl.BlockSpec((tm, d), lambda i: (i, 0))] + [c.spec for c in casts],
        out_shape=[jax.ShapeDtypeStruct((n, d), BF16)] + [c.out_shape for c in casts],
        scratch_shapes=[pltpu.VMEM((d, d), BF16)],
        compiler_params=pltpu.CompilerParams(
            dimension_semantics=("arbitrary",), vmem_limit_bytes=vmem + 4 * MIB),
        name="norm_proj",
    )(x2, g_mix.reshape(1, d), w_in, *[c.src for c in casts])
    return u, [o.reshape(c.shape) for o, c in zip(cast_out, casts)]


def _fold_kernel(blk_ref, partner_ref, after_ref, rev_ref, even_ref, odd_ref):
    after = jnp.where(pl.program_id(1) > 0, after_ref[...], jnp.zeros_like(after_ref))
    window = jnp.concatenate([partner_ref[...], after], axis=0)
    mirrored = _dot(rev_ref[...], window)
    blk = blk_ref[...].astype(F32)
    even_ref[...] = (blk + mirrored).astype(BF16)
    odd_ref[...] = (blk - mirrored).astype(BF16)


def _fold_sequence(u3, rev):
    b, s, _ = u3.shape
    t = SEQ_TILE
    halo = POOL_HALO
    n_tiles = s // t
    half_tiles = n_tiles // 2
    tile = lambda tile_of: pl.BlockSpec((None, t, FOURIER_WIDTH), lambda bi, j: (bi, tile_of(j), 1))
    after = pl.BlockSpec((None, halo, FOURIER_WIDTH),
                         lambda bi, j: (bi, jnp.minimum((n_tiles - j) * (t // halo), s // halo - 1), 1))
    out = pl.BlockSpec((None, t, FOURIER_WIDTH), lambda bi, j: (bi, j, 0))
    return pl.pallas_call(
        _fold_kernel,
        grid=(b, half_tiles),
        in_specs=[tile(lambda j: j), tile(lambda j: n_tiles - 1 - j), after,
                  pl.BlockSpec(rev.shape, lambda bi, j: (0, 0))],
        out_specs=[out, out],
        out_shape=[jax.ShapeDtypeStruct((b, s // 2, FOURIER_WIDTH), BF16)] * 2,
        name="fold_sequence",
    )(u3, u3, u3, rev)


def _mix_kernel(even_ref, odd_ref, umid_ref, up_lo_ref, prev_lo_ref, next_lo_ref, up_hi_ref, prev_hi_ref, next_hi_ref,
                c0_ref, s0_ref, cph_ref, sph_ref, band_lo_ref, band_hi_ref, rev_ref,
                cw_ref, sw_ref, cast_src_ref, lo_ref, hi_ref, cast_dst_ref, lhs_ref):
    t = SEQ_TILE
    tp = DFT_ROWS
    gd = POOL_GROUP_DIM
    hd = FOURIER_HEAD_DIM

    @pl.when(pl.program_id(1) == 0)
    def _():
        cph = cph_ref[0]
        sph = sph_ref[0]
        for r0 in range(0, tp, TWIDDLE_ROWS):
            c0 = c0_ref[r0:r0 + TWIDDLE_ROWS, :]
            s0 = s0_ref[r0:r0 + TWIDDLE_ROWS, :]
            lhs_ref[r0:r0 + TWIDDLE_ROWS, :] = (cph * c0 - sph * s0).astype(BF16)
            lhs_ref[tp + r0:tp + r0 + TWIDDLE_ROWS, :] = (sph * c0 + cph * s0).astype(BF16)

    _cast_chunks([cast_src_ref], [cast_dst_ref])
    row = lax.broadcasted_iota(jnp.int32, (tp, 1), 0)
    sign = jnp.where(row % 2 == 0, 1.0, -1.0)
    p_all = _dot(lhs_ref[:tp, :], even_ref[...]) + sign * umid_ref[0:1, :].astype(F32)
    q_all = _dot(lhs_ref[tp:, :], odd_ref[...])
    for h in range(N_FOURIER_HEADS):
        cols = slice(h * hd, (h + 1) * hd)
        out_cols = slice(POOL_WIDTH + h * hd, POOL_WIDTH + (h + 1) * hd)
        pc = _dot(p_all[:, cols].astype(BF16), cw_ref[h])
        qs = _dot(q_all[:, cols].astype(BF16), sw_ref[h])
        lo_ref[:, out_cols] = (pc + qs)[:t].astype(BF16)
        hi_ref[:, out_cols] = _dot(rev_ref[...], (pc - qs).astype(BF16)).astype(BF16)

    def pool(up_ref, prev_ref, next_ref, band_ref, out_ref):
        win = jnp.concatenate([prev_ref[...], up_ref[...], next_ref[...]], axis=0)
        for g in range(N_POOL_GROUPS):
            cols = slice(g * gd, (g + 1) * gd)
            out_ref[:, cols] = _dot(band_ref[0, g], win[:, cols]).astype(BF16)

    pool(up_lo_ref, prev_lo_ref, next_lo_ref, band_lo_ref, lo_ref)
    pool(up_hi_ref, prev_hi_ref, next_hi_ref, band_hi_ref, hi_ref)


def _mix(u3, cw, sw, cast_weight):
    b, s, d = u3.shape
    t = SEQ_TILE
    tp = DFT_ROWS
    halo = POOL_HALO
    n_tiles = s // t
    n_steps = n_tiles // 2
    halo_blocks_per_tile = t // halo
    last_halo_block = s // halo - 1

    sh = s // 2
    c0, s0 = _twiddle(np.arange(tp), np.arange(sh), s)
    cph, sph = _twiddle(np.arange(0, sh, t), np.arange(sh), s)
    band = jnp.asarray(_pool_band(s, t, halo)).astype(BF16)
    rev = np.zeros((t, tp), np.float32)
    rev[np.arange(t), t - np.arange(t)] = 1.0
    rev = jnp.asarray(rev).astype(BF16)
    even, odd = _fold_sequence(u3, rev)

    hi_tile = lambda m: n_tiles - 1 - m
    tile_spec = lambda tile_of: pl.BlockSpec((None, t, POOL_WIDTH), lambda m, bi: (bi, tile_of(m), 0))
    prev_spec = lambda tile_of: pl.BlockSpec(
        (None, halo, POOL_WIDTH),
        lambda m, bi: (bi, jnp.maximum(tile_of(m) * halo_blocks_per_tile - 1, 0), 0))
    next_spec = lambda tile_of: pl.BlockSpec(
        (None, halo, POOL_WIDTH),
        lambda m, bi: (bi, jnp.minimum((tile_of(m) + 1) * halo_blocks_per_tile, last_halo_block), 0))
    band_shape = (1, N_POOL_GROUPS, t, t + 2 * halo)
    const2 = lambda shape: pl.BlockSpec(shape, lambda m, bi: (0, 0))
    const3 = lambda shape: pl.BlockSpec(shape, lambda m, bi: (0, 0, 0))
    lo_tile = lambda m: m

    in_specs = [
        pl.BlockSpec((None, sh, FOURIER_WIDTH), lambda m, bi: (bi, 0, 0)),
        pl.BlockSpec((None, sh, FOURIER_WIDTH), lambda m, bi: (bi, 0, 0)),
        pl.BlockSpec((None, halo, FOURIER_WIDTH), lambda m, bi: (bi, sh // halo, 1)),
        tile_spec(lo_tile), prev_spec(lo_tile), next_spec(lo_tile),
        tile_spec(hi_tile), prev_spec(hi_tile), next_spec(hi_tile),
        pl.BlockSpec((tp, sh), lambda m, bi: (0, 0), pipeline_mode=pl.Buffered(1)),
        pl.BlockSpec((tp, sh), lambda m, bi: (0, 0), pipeline_mode=pl.Buffered(1)),
        pl.BlockSpec((1, 1, sh), lambda m, bi: (m, 0, 0)),
        pl.BlockSpec((1, 1, sh), lambda m, bi: (m, 0, 0)),
        pl.BlockSpec(band_shape, lambda m, bi: (jnp.where(m == 0, 0, 1), 0, 0, 0)),
        pl.BlockSpec(band_shape, lambda m, bi: (jnp.where(m == 0, 2, 1), 0, 0, 0)),
        const2((t, tp)),
        const3((N_FOURIER_HEADS, FOURIER_HEAD_DIM, FOURIER_HEAD_DIM)),
        const3((N_FOURIER_HEADS, FOURIER_HEAD_DIM, FOURIER_HEAD_DIM)),
    ]
    half_out = pl.BlockSpec((None, t, d), lambda m, bi: (bi, m, 0))
    hi_out = pl.BlockSpec((None, t, d), lambda m, bi: (bi, n_steps - 1 - m, 0))
    cast = _CastAlong(cast_weight, n_steps * b, lambda m, bi: m * b + bi)
    vmem = (2 * 2 * sh * FOURIER_WIDTH * 2
            + 2 * tp * sh * 4
            + 2 * tp * sh * 2
            + 2 * 2 * t * d * 2 * 2
            + 4 * 2 * tp * FOURIER_WIDTH * 4
            + cast.vmem_bytes)
    lo, hi, cast_out = pl.pallas_call(
        _mix_kernel,
        grid=(n_steps, b),
        in_specs=in_specs + [cast.spec],
        out_specs=[half_out, hi_out, cast.spec],
        out_shape=[jax.ShapeDtypeStruct((b, s // 2, d), BF16)] * 2 + [cast.out_shape],
        scratch_shapes=[pltpu.VMEM((2 * tp, sh), BF16)],
        compiler_params=pltpu.CompilerParams(
            dimension_semantics=("arbitrary", "arbitrary"), vmem_limit_bytes=vmem + 4 * MIB),
        name="seq_mix",
    )(even, odd, u3, u3, u3, u3, u3, u3, u3, jnp.asarray(c0), jnp.asarray(s0),
      jnp.asarray(cph).reshape(n_steps, 1, sh), jnp.asarray(sph).reshape(n_steps, 1, sh),
      band, band, rev,
      cw, sw, cast.src)
    return lo, hi, cast_out.reshape(cast.shape)


def _out_proj_route_kernel(x_ref, mixed_lo_ref, mixed_hi_ref, wtop_ref, wbot_ref, g_ref, wr_ref, br_ref,
                           tri_ref, pick_ref, cast_src_ref, rows_ref, route_ref, counts_ref, cast_dst_ref,
                           carry_ref, x1_ref, *, tiles_per_seq, n_tiles):
    step = pl.program_id(0)

    @pl.when(step == 0)
    def _():
        carry_ref[...] = jnp.zeros_like(carry_ref)
        x1_ref[...] = jnp.zeros_like(x1_ref)

    _cast_chunks([cast_src_ref], [cast_dst_ref])

    x1 = x1_ref[...]
    rows_ref[:, :D_MODEL] = x1
    h2 = _rmsnorm(x1, g_ref[...])
    logits = _dot(h2.astype(BF16), wr_ref[...]) + br_ref[...]

    tile = jnp.minimum(step, n_tiles - 1)
    in_lo_half = (tile % tiles_per_seq) < tiles_per_seq // 2
    mixed = jnp.where(in_lo_half, mixed_lo_ref[...], mixed_hi_ref[...])
    x1_ref[...] = (x_ref[...] + _dot(mixed[:, :POOL_WIDTH], wtop_ref[...])
                   + _dot(mixed[:, POOL_WIDTH:], wbot_ref[...]))

    lane = lax.broadcasted_iota(jnp.int32, logits.shape, 1)
    neg = jnp.float32(-jnp.inf)
    big = jnp.int32(ROUTER_LANES)

    is_group = lane < N_EXPERT_GROUPS
    gl = jnp.where(is_group, logits, neg)
    gmax = jnp.max(gl, axis=-1, keepdims=True)
    gidx = jnp.min(jnp.where(gl == gmax, lane, big), axis=-1, keepdims=True)
    p_g = 1.0 / jnp.sum(jnp.exp(gl - gmax), axis=-1, keepdims=True)

    e_lane = lane - N_EXPERT_GROUPS
    in_group = (e_lane >= gidx * EXPERTS_PER_GROUP) & (e_lane < (gidx + 1) * EXPERTS_PER_GROUP)
    el = jnp.where(in_group, logits, neg)
    v1 = jnp.max(el, axis=-1, keepdims=True)
    i1 = jnp.min(jnp.where(el == v1, lane, big), axis=-1, keepdims=True)
    el2 = jnp.where(lane == i1, neg, el)
    v2 = jnp.max(el2, axis=-1, keepdims=True)
    i2 = jnp.min(jnp.where(el2 == v2, lane, big), axis=-1, keepdims=True)
    r = jnp.exp(v2 - v1)
    w1 = p_g / (1.0 + r)
    w2 = p_g * r / (1.0 + r)

    first_is_low = i1 < i2
    first_lane = N_EXPERT_GROUPS + gidx * EXPERTS_PER_GROUP
    la = jnp.where(first_is_low, i1, i2) - first_lane
    lb = jnp.where(first_is_low, i2, i1) - first_lane
    w_low = jnp.where(first_is_low, w1, w2)
    w_high = jnp.where(first_is_low, w2, w1)
    assert (PAIR_SLOT_A, PAIR_SLOT_B) == ((0, 0, 0, 1, 1, 3), (1, 2, 3, 3, 2, 2))
    pair = jnp.where(la == 0, lb - 1, jnp.where(la == 1, 6 - lb, 5))
    slot_a_is_high = la == 2
    w_a = jnp.where(slot_a_is_high, w_high, w_low)
    w_b = jnp.where(slot_a_is_high, w_low, w_high)
    cls = gidx * PAIRS_PER_GROUP + pair
    rows_ref[:, D_MODEL:] = jnp.where(lane == 0, w_a, jnp.where(lane == 1, w_b, 0.0))

    onehot = jnp.where(lane == cls, 1.0, 0.0)
    before = _dot(tri_ref[...], onehot.astype(BF16)) + carry_ref[...]
    rank = jnp.sum(jnp.where(lane == cls, before, 0.0), axis=-1, keepdims=True)
    carry_ref[...] += jnp.sum(onehot, axis=0, keepdims=True) * jnp.where(step > 0, 1.0, 0.0)
    counts_ref[...] = carry_ref[...]

    rank_hi = jnp.floor(rank * (1.0 / RANK_RADIX))
    rank_lo = rank - rank_hi * RANK_RADIX
    digits = jnp.where(lane == 0, cls.astype(F32),
                       jnp.where(lane == 1, rank_hi, jnp.where(lane == 2, rank_lo, 0.0)))
    route_ref[...] = lax.dot_general(pick_ref[...], digits.astype(BF16), (((1,), (1,)), ((), ())),
                                     preferred_element_type=F32)


def _out_proj_route(x2, mixed_lo, mixed_hi, seq_len, w_top_bf16, w_out_bf16, g_ffn, wr, br, cast_weight):
    n, d = x2.shape
    tm = TOKEN_TILE
    n_tiles = n // tm
    tiles_per_seq = seq_len // tm
    half_tiles = tiles_per_seq // 2
    const = lambda shape: pl.BlockSpec(shape, lambda i: (0, 0))
    in_tile = lambda i: jnp.minimum(i, n_tiles - 1)
    out_tile = lambda i: jnp.maximum(i - 1, 0)
    lo_spec = pl.BlockSpec((tm, d), lambda i: (
        (in_tile(i) // tiles_per_seq) * half_tiles + jnp.minimum(in_tile(i) % tiles_per_seq, half_tiles - 1), 0))
    hi_spec = pl.BlockSpec((tm, d), lambda i: (
        (in_tile(i) // tiles_per_seq) * half_tiles + jnp.maximum(in_tile(i) % tiles_per_seq - half_tiles, 0), 0))
    tri = np.tril(np.ones((tm, tm), np.float32), -1)
    pick = np.eye(SUBLANES, ROUTER_LANES, dtype=np.float32)
    vmem = (2 * tm * d * 4 + 4 * tm * d * 2 + 2 * tm * ROW_WORDS * 4 + d * d * 2 + tm * d * 4
            + 2 * d * ROUTER_LANES * 2 + 2 * tm * tm * 2 + 3 * tm * d * 4)
    cast = _CastAlong(cast_weight, n_tiles, in_tile)
    vmem += cast.vmem_bytes
    rows, route, counts, cast_out = pl.pallas_call(
        functools.partial(_out_proj_route_kernel, tiles_per_seq=tiles_per_seq, n_tiles=n_tiles),
        grid=(n_tiles + 1,),
        in_specs=[pl.BlockSpec((tm, d), lambda i: (in_tile(i), 0)), lo_spec, hi_spec,
                  pl.BlockSpec((POOL_WIDTH, d), lambda i: (0, 0), pipeline_mode=pl.Buffered(1)),
                  pl.BlockSpec((FOURIER_WIDTH, d), lambda i: (1, 0), pipeline_mode=pl.Buffered(1)),
                  const((1, d)), const((d, ROUTER_LANES)), const((1, ROUTER_LANES)),
                  const((tm, tm)), const((SUBLANES, ROUTER_LANES)), cast.spec],
        out_specs=[pl.BlockSpec((tm, ROW_WORDS), lambda i: (out_tile(i), 0)),
                   pl.BlockSpec((SUBLANES, tm), lambda i: (0, out_tile(i))),
                   const((1, ROUTER_LANES)), cast.spec],
        out_shape=[jax.ShapeDtypeStruct((n, ROW_WORDS), F32),
                   jax.ShapeDtypeStruct((SUBLANES, n), F32),
                   jax.ShapeDtypeStruct((1, ROUTER_LANES), F32), cast.out_shape],
        scratch_shapes=[pltpu.VMEM((1, ROUTER_LANES), F32), pltpu.VMEM((tm, d), F32)],
        compiler_params=pltpu.CompilerParams(
            dimension_semantics=("arbitrary",), vmem_limit_bytes=vmem + 4 * MIB),
        name="out_proj_route",
    )(x2, mixed_lo, mixed_hi, w_top_bf16, w_out_bf16, g_ffn.reshape(1, d), wr, br,
      jnp.asarray(tri).astype(BF16), jnp.asarray(pick).astype(BF16), cast.src)
    return rows, route, counts, cast_out.reshape(cast.shape)


def _row_copy(src_ref, src_row, dst_ref, dst_row, sem):
    return pltpu.make_async_copy(src_ref.at[pl.ds(src_row, 1), :], dst_ref.at[pl.ds(dst_row, 1), :], sem)


def _plan_kernel(cnt_ref, route_ref, tok_ref, start_ref, ea_ref, eb_ref, lo_ref, hi_ref, nused_ref,
                 cstart_ref, pos_vmem_ref, pos_smem_ref, sem):
    tm = EXPERT_TILE
    shift = tm.bit_length() - 1
    n = tok_ref.shape[0]
    n_items = start_ref.shape[0]

    run = jnp.int32(0)
    inside = jnp.bool_(True)
    for c in range(N_CLASSES):
        cnt = cnt_ref[c]
        cstart_ref[c] = run
        inside = inside & ((cnt == 0) | (cnt >= tm))
        run = run + cnt

    cls_row = route_ref[0:1, :]
    slot_row = route_ref[1:2, :] * RANK_RADIX + route_ref[2:3, :]
    for c in range(N_CLASSES):
        slot_row = slot_row + jnp.where(cls_row == c, cstart_ref[c].astype(F32), 0.0)
    pos_vmem_ref[...] = slot_row.astype(jnp.int32)
    to_smem = pltpu.make_async_copy(pos_vmem_ref, pos_smem_ref, sem)
    to_smem.start()

    k = jnp.int32(0)
    for c in range(N_CLASSES):
        cnt = cnt_ref[c]
        cs = cstart_ref[c]
        ce = cs + cnt
        group, pair = divmod(c, PAIRS_PER_GROUP)
        ea = group * EXPERTS_PER_GROUP + PAIR_SLOT_A[pair]
        eb = group * EXPERTS_PER_GROUP + PAIR_SLOT_B[pair]
        first_window = cs >> shift
        n_inside = (cnt + tm - 1) >> shift
        n_aligned = jnp.where(cnt > 0, ((ce - 1) >> shift) - first_window + 1, 0)
        full_windows = cnt >> shift

        def emit(i, carry, k=k, cs=cs, ce=ce, ea=ea, eb=eb, first_window=first_window,
                 full_windows=full_windows):
            start_inside = jnp.where(i < full_windows, cs + i * tm, ce - tm)
            start_aligned = (first_window + i) * tm
            start_ref[k + i] = jnp.where(inside, start_inside, start_aligned)
            ea_ref[k + i] = ea
            eb_ref[k + i] = eb
            lo_ref[k + i] = jnp.where(inside, 0, jnp.clip(cs - start_aligned, 0, tm))
            hi_ref[k + i] = jnp.where(inside, tm, jnp.clip(ce - start_aligned, 0, tm))
            return carry

        n_class_items = jnp.where(inside, n_inside, n_aligned)
        lax.fori_loop(0, n_class_items, emit, 0)
        k = k + n_class_items
    nused_ref[0] = k

    def repeat_last(i, carry):
        for ref in (start_ref, ea_ref, eb_ref, lo_ref, hi_ref):
            ref[i] = ref[k - 1]
        return carry

    lax.fori_loop(k, n_items, repeat_last, 0)

    to_smem.wait()

    def invert(t, carry):
        tok_ref[pos_smem_ref[0, t]] = t
        return carry

    lax.fori_loop(0, n, invert, 0, unroll=INVERT_UNROLL)


def _plan_routing(route, counts, n):
    n_items = n // EXPERT_TILE + N_CLASSES
    smem = lambda size: jax.ShapeDtypeStruct((size,), jnp.int32)
    smem_spec = pl.BlockSpec(memory_space=pltpu.SMEM)
    return pl.pallas_call(
        _plan_kernel,
        in_specs=[smem_spec, pl.BlockSpec(memory_space=pltpu.VMEM)],
        out_specs=[smem_spec] * 7,
        out_shape=[smem(n)] + [smem(n_items)] * 5 + [smem(1)],
        scratch_shapes=[pltpu.SMEM((N_CLASSES,), jnp.int32), pltpu.VMEM((1, n), jnp.int32),
                        pltpu.SMEM((1, n), jnp.int32), pltpu.SemaphoreType.DMA(())],
        name="routing_plan",
    )(counts[0, :N_CLASSES].astype(jnp.int32), route)


def _expert_pair_kernel(start_ref, ea_ref, eb_ref, lo_ref, hi_ref, nused_ref, tok_ref,
                        rows_hbm_ref, gffn_ref, gfin_ref,
                        wga_ref, wua_ref, wda_ref, wgb_ref, wub_ref, wdb_ref, out_hbm_ref,
                        buf_ref, acc_ref, gather_sem, scatter_sem):
    del ea_ref, eb_ref
    tm = EXPERT_TILE
    j = pl.program_id(0)
    last = nused_ref[0] - 1
    slot = j % 2
    other = 1 - slot

    def start_gather(item, dst_slot, rows=range(tm)):
        base = start_ref[item]
        for r in rows:
            _row_copy(rows_hbm_ref, tok_ref[base + r], buf_ref.at[dst_slot], r, gather_sem.at[dst_slot]).start()

    def wait_gather(dst_slot):
        pltpu.make_async_copy(rows_hbm_ref.at[pl.ds(0, tm), :], buf_ref.at[dst_slot],
                              gather_sem.at[dst_slot]).wait()

    def start_scatter(item, src_slot, rows=range(tm)):
        base = start_ref[item]
        for r in rows:
            _row_copy(acc_ref.at[src_slot], r, out_hbm_ref, tok_ref[base + r], scatter_sem).start()

    def wait_scatter():
        pltpu.make_async_copy(acc_ref.at[0], out_hbm_ref.at[pl.ds(0, tm), :], scatter_sem).wait()

    gslot = j % GATHER_SLOTS

    @pl.when(j == 0)
    def _():
        acc_ref[...] = jnp.zeros_like(acc_ref)
        start_gather(0, 0)
        start_gather(jnp.minimum(1, last), 1)
        start_scatter(0, 1)

    @pl.when(j <= last)
    def _():
        wait_scatter()
        wait_gather(gslot)
        prev_item = jnp.maximum(j - 1, 0)
        ahead_item = jnp.minimum(j + 2, last)

        def issue_copies(part):
            half_parts = DMA_PARTS // 2
            rows = range((part % half_parts) * tm // half_parts, (part % half_parts + 1) * tm // half_parts)
            if part < half_parts:
                start_scatter(prev_item, other, rows)
            else:
                start_gather(ahead_item, (j + 2) % GATHER_SLOTS, rows)

        lo = lo_ref[j]
        hi = hi_ref[j]
        x1 = buf_ref[gslot, :, :D_MODEL]
        h = _rmsnorm(x1, gffn_ref[...]).astype(BF16)

        part = 0
        ff_chunk = EXPERT_FF * 4 // DMA_PARTS
        for slot_lane, (wg_ref, wu_ref, wd_ref) in enumerate(((wga_ref, wua_ref, wda_ref),
                                                              (wgb_ref, wub_ref, wdb_ref))):
            for c0 in range(0, EXPERT_FF, ff_chunk):
                issue_copies(part)
                wts = buf_ref[gslot, :, D_MODEL:]
                lane = lax.broadcasted_iota(jnp.int32, wts.shape, 1)
                w = jnp.sum(jnp.where(lane == slot_lane, wts, 0.0), axis=-1, keepdims=True)
                a = _dot(h, wg_ref[0, :, c0:c0 + ff_chunk])
                v = _dot(h, wu_ref[0, :, c0:c0 + ff_chunk])
                act = (a * (1.0 / (1.0 + jnp.exp(-a))) * v * w).astype(BF16)
                for half, d0 in enumerate(range(0, D_MODEL, D_MODEL // 2)):
                    cols = slice(d0, d0 + D_MODEL // 2)
                    if half == 1:
                        issue_copies(part + 1)
                    y_part = _dot(act, wd_ref[0, c0:c0 + ff_chunk, cols])
                    if part == 0:
                        acc_ref[slot, :, cols] = y_part
                    else:
                        acc_ref[slot, :, cols] += y_part
                part += 2
        res = _rmsnorm(buf_ref[gslot, :, :D_MODEL] + acc_ref[slot], gfin_ref[...])

        row = lax.broadcasted_iota(jnp.int32, (tm, 1), 0)
        mine = (row >= lo) & (row < hi)
        acc_ref[slot] = jnp.where(mine, res, jnp.where(lo > 0, acc_ref[other], 0.0))

    @pl.when(j == last)
    def _():
        wait_scatter()
        start_scatter(j, slot)
        wait_scatter()
        wait_gather((j + 1) % GATHER_SLOTS)
        wait_gather((j + 2) % GATHER_SLOTS)


def _expert_pairs(item_start, item_ea, item_eb, item_lo, item_hi, n_used, slot_token, rows,
                  g_ffn, g_final, wg, wu, wd):
    n, w = rows.shape
    d, f = D_MODEL, EXPERT_FF
    tm = EXPERT_TILE
    gate_a = pl.BlockSpec((1, d, f), lambda j, st, ea, eb, lo, hi, nu, tok: (ea[j], 0, 0))
    gate_b = pl.BlockSpec((1, d, f), lambda j, st, ea, eb, lo, hi, nu, tok: (eb[j], 0, 0))
    down_a = pl.BlockSpec((1, f, d), lambda j, st, ea, eb, lo, hi, nu, tok: (ea[j], 0, 0))
    down_b = pl.BlockSpec((1, f, d), lambda j, st, ea, eb, lo, hi, nu, tok: (eb[j], 0, 0))
    gain = pl.BlockSpec((1, d), lambda j, st, ea, eb, lo, hi, nu, tok: (0, 0))
    grid_spec = pltpu.PrefetchScalarGridSpec(
        num_scalar_prefetch=7,
        grid=(item_start.shape[0],),
        in_specs=[pl.BlockSpec(memory_space=pl.ANY), gain, gain,
                  gate_a, gate_a, down_a, gate_b, gate_b, down_b],
        out_specs=pl.BlockSpec(memory_space=pl.ANY),
        scratch_shapes=[pltpu.VMEM((GATHER_SLOTS, tm, w), F32), pltpu.VMEM((2, tm, d), F32),
                        pltpu.SemaphoreType.DMA((GATHER_SLOTS,)), pltpu.SemaphoreType.DMA(())],
    )
    vmem = 2 * 6 * d * f * 2 + GATHER_SLOTS * tm * w * 4 + 2 * tm * d * 4 + 8 * tm * d * 4
    return pl.pallas_call(
        _expert_pair_kernel,
        grid_spec=grid_spec,
        out_shape=jax.ShapeDtypeStruct((n, d), F32),
        compiler_params=pltpu.CompilerParams(
            dimension_semantics=("arbitrary",), vmem_limit_bytes=vmem + 4 * MIB),
        name="expert_pairs",
    )(item_start, item_ea, item_eb, item_lo, item_hi, n_used, slot_token, rows, g_ffn.reshape(1, d),
      g_final.reshape(1, d), wg, wu, wd, wg, wu, wd)


def kernel(x, g_mix, w_in, w_pool, pool_scale, w_fourier, w_out, g_ffn, w_group_router,
           b_group_router, w_expert_router, b_expert_router, w_gate, w_up, w_down, g_final):
    b, s, d = x.shape
    assert d == D_MODEL and s % (2 * SEQ_TILE) == 0 and s % (2 * TOKEN_TILE) == 0
    assert (b * s) % max(TOKEN_TILE, EXPERT_TILE) == 0
    n = b * s
    x2 = x.reshape(n, d)

    cw, sw, w_top_bf16 = _mixer_weights(w_fourier, w_pool, pool_scale, w_out, s)
    u, (w_out_bf16, w_gate_bf16) = _norm_proj(x2, g_mix, w_in, [w_out, w_gate])
    mixed_lo, mixed_hi, w_up_bf16 = _mix(u.reshape(b, s, d), cw, sw, w_up)

    wr = jnp.concatenate([w_group_router, w_expert_router], axis=1)
    wr = jnp.pad(wr, ((0, 0), (0, ROUTER_LANES - wr.shape[1]))).astype(BF16)
    br = jnp.concatenate([b_group_router, b_expert_router])
    br = jnp.pad(br, (0, ROUTER_LANES - br.shape[0])).reshape(1, ROUTER_LANES)

    rows, route, counts, w_down_bf16 = _out_proj_route(
        x2, mixed_lo.reshape(n // 2, d), mixed_hi.reshape(n // 2, d), s, w_top_bf16, w_out_bf16, g_ffn, wr, br,
        w_down)
    slot_token, item_start, item_ea, item_eb, item_lo, item_hi, n_used = _plan_routing(route, counts, n)
    out = _expert_pairs(item_start, item_ea, item_eb, item_lo, item_hi, n_used, slot_token, rows,
                        g_ffn, g_final, w_gate_bf16, w_up_bf16, w_down_bf16)
    return out.reshape(b, s, d)
```

```python
import functools

import numpy as np
import jax
import jax.numpy as jnp
from jax import lax
from jax.experimental import pallas as pl
from jax.experimental.pallas import tpu as pltpu

D_MODEL = 2048
POOL_WINDOWS = (2, 4, 8, 16)
N_POOL_GROUPS = len(POOL_WINDOWS)
POOL_WIDTH = D_MODEL // 2
POOL_GROUP_DIM = POOL_WIDTH // N_POOL_GROUPS
FOURIER_WIDTH = D_MODEL - POOL_WIDTH
N_FOURIER_HEADS = 4
FOURIER_HEAD_DIM = FOURIER_WIDTH // N_FOURIER_HEADS
N_EXPERT_GROUPS = 4
EXPERTS_PER_GROUP = 4
EXPERT_FF = D_MODEL // 4
RMS_EPS = 1e-6

LANES = 128
SUBLANES = 8
BF16_SUBLANES = 16
ROUTER_LANES = LANES
MIB = 1024 * 1024

TOKEN_TILE = 512
SEQ_TILE = 256
POOL_HALO = BF16_SUBLANES
DFT_ROWS = SEQ_TILE + BF16_SUBLANES
TWIDDLE_ROWS = 16
FOLD_TILES = 4

PAIR_SLOT_A = (0, 0, 0, 1, 1, 3)
PAIR_SLOT_B = (1, 2, 3, 3, 2, 2)
PAIRS_PER_GROUP = len(PAIR_SLOT_A)
N_CLASSES = N_EXPERT_GROUPS * PAIRS_PER_GROUP
ROW_WORDS = D_MODEL + LANES
RANK_RADIX = 128
EXPERT_TILE = 256
DMA_PARTS = 8
GATHER_SLOTS = 3
INVERT_UNROLL = 32
WEIGHT_CAST_ROWS = 256

BF16 = jnp.bfloat16
F32 = jnp.float32


def _rmsnorm(x, g):
    ms = jnp.mean(x * x, axis=-1, keepdims=True)
    return x * lax.rsqrt(ms + RMS_EPS) * g


def _dot(a, b):
    return jnp.dot(a, b, preferred_element_type=F32)


def _twiddle(rows, cols, period):
    m = (np.asarray(rows, np.int64)[:, None] * np.asarray(cols, np.int64)[None, :]) % period
    ang = (2.0 * np.pi / period) * m.astype(np.float64)
    return np.cos(ang).astype(np.float32), np.sin(ang).astype(np.float32)


def _pool_band(seq_len, tile, halo):
    n_tiles = seq_len // tile
    out = np.zeros((3, N_POOL_GROUPS, tile, tile + 2 * halo), np.float64)
    for v, m in enumerate((0, 1, n_tiles - 1)):
        t0 = m * tile
        for g, k in enumerate(POOL_WINDOWS):
            for r in range(tile):
                t = t0 + r
                lo = max(t - (k - 1) // 2, 0)
                hi = min(t + k // 2 + 1, seq_len)
                out[v, g, r, lo - t0 + halo:hi - t0 + halo] = 1.0 / (hi - lo)
                out[v, g, r, r + halo] -= 1.0
    return out.astype(np.float32)


def _mixer_weight_kernel(cd_ref, sd_ref, wf_ref, wp_ref, ps_ref, wo_ref, cw_ref, sw_ref, wtop_ref, *, scale):
    wf = wf_ref[0]
    cw = jnp.dot(cd_ref[...], wf, preferred_element_type=F32, precision=lax.Precision.HIGHEST)
    sw = jnp.dot(sd_ref[...], wf, preferred_element_type=F32, precision=lax.Precision.HIGHEST)
    cw_ref[0] = (cw * scale).astype(BF16)
    sw_ref[0] = (sw * (-scale)).astype(BF16)
    wtop_ref[...] = _dot((wp_ref[0] * ps_ref[0]).astype(BF16), wo_ref[...].astype(BF16)).astype(BF16)


def _mixer_weights(w_fourier, w_pool, pool_scale, w_out, seq_len):
    dh = FOURIER_HEAD_DIM
    assert N_FOURIER_HEADS == N_POOL_GROUPS and dh == POOL_GROUP_DIM
    d = w_out.shape[1]
    cd, sd = _twiddle(np.arange(dh), np.arange(dh), dh)
    scale = 1.0 / np.sqrt(float(seq_len * dh))
    mat = pl.BlockSpec((dh, dh), lambda h: (0, 0))
    per_head = pl.BlockSpec((1, dh, dh), lambda h: (h, 0, 0))
    out_rows = pl.BlockSpec((dh, d), lambda h: (h, 0))
    return pl.pallas_call(
        functools.partial(_mixer_weight_kernel, scale=scale),
        grid=(N_FOURIER_HEADS,),
        in_specs=[mat, mat, per_head, per_head, pl.BlockSpec((1, 1, dh), lambda h: (h, 0, 0)), out_rows],
        out_specs=[per_head, per_head, out_rows],
        out_shape=[jax.ShapeDtypeStruct((N_FOURIER_HEADS, dh, dh), BF16)] * 2
        + [jax.ShapeDtypeStruct((POOL_WIDTH, d), BF16)],
        name="mixer_weights",
    )(jnp.asarray(cd), jnp.asarray(sd), w_fourier, w_pool, pool_scale.reshape(N_POOL_GROUPS, 1, dh), w_out)


class _CastAlong:
    def __init__(self, w, n_chunks, chunk_of):
        cols = w.shape[-1]
        rows = w.size // cols // n_chunks
        self.shape = w.shape
        self.src = w.reshape(n_chunks, rows, cols)
        self.spec = pl.BlockSpec((1, rows, cols), lambda *idx: (chunk_of(*idx), 0, 0))
        self.out_shape = jax.ShapeDtypeStruct((n_chunks, rows, cols), BF16)
        self.vmem_bytes = 2 * rows * cols * (4 + 2)


def _cast_chunks(src_refs, dst_refs):
    for src_ref, dst_ref in zip(src_refs, dst_refs):
        dst_ref[...] = src_ref[...].astype(BF16)


def _norm_proj_kernel(x_ref, g_ref, w_ref, *rest, n_cast):
    cast_src, (u_ref, *cast_dst), w_bf16_ref = rest[:n_cast], rest[n_cast:-1], rest[-1]

    @pl.when(pl.program_id(0) == 0)
    def _():
        for r0 in range(0, w_ref.shape[0], WEIGHT_CAST_ROWS):
            w_bf16_ref[r0:r0 + WEIGHT_CAST_ROWS, :] = w_ref[r0:r0 + WEIGHT_CAST_ROWS, :].astype(BF16)

    h = _rmsnorm(x_ref[...], g_ref[...])
    u_ref[...] = _dot(h.astype(BF16), w_bf16_ref[...]).astype(BF16)
    _cast_chunks(cast_src, cast_dst)


def _norm_proj(x2, g_mix, w_in, cast_weights):
    n, d = x2.shape
    tm = TOKEN_TILE
    casts = [_CastAlong(w, n // tm, lambda i: i) for w in cast_weights]
    vmem = (2 * tm * d * 4 + d * d * (4 + 2) + 2 * tm * d * 2 + 3 * tm * d * 4
            + sum(c.vmem_bytes for c in casts))
    u, *cast_out = pl.pallas_call(
        functools.partial(_norm_proj_kernel, n_cast=len(casts)),
        grid=(n // tm,),
        in_specs=[
            pl.BlockSpec((tm, d), lambda i: (i, 0)),
            pl.BlockSpec((1, d), lambda i: (0, 0)),
            pl.BlockSpec((d, d), lambda i: (0, 0), pipeline_mode=pl.Buffered(1)),
        ] + [c.spec for c in casts],
        out_specs=[pl.BlockSpec((tm, d), lambda i: (i, 0))] + [c.spec for c in casts],
        out_shape=[jax.ShapeDtypeStruct((n, d), BF16)] + [c.out_shape for c in casts],
        scratch_shapes=[pltpu.VMEM((d, d), BF16)],
        compiler_params=pltpu.CompilerParams(
            dimension_semantics=("arbitrary",), vmem_limit_bytes=vmem + 4 * MIB),
        name="norm_proj",
    )(x2, g_mix.reshape(1, d), w_in, *[c.src for c in casts])
    return u, [o.reshape(c.shape) for o, c in zip(cast_out, casts)]


def _fold_kernel(blk_ref, mirror_ref, after_ref, rev_ref, even_ref, odd_ref):
    t = SEQ_TILE
    halo = POOL_HALO
    for i in range(FOLD_TILES):
        top = (FOLD_TILES - 1 - i) * t
        if i == 0:
            after = jnp.where(pl.program_id(1) > 0, after_ref[...], jnp.zeros_like(after_ref))
        else:
            after = mirror_ref[top + t:top + t + halo, :]
        window = jnp.concatenate([mirror_ref[top:top + t, :], after], axis=0)
        mirrored = _dot(rev_ref[...], window)
        blk = blk_ref[i * t:(i + 1) * t, :].astype(F32)
        even_ref[i * t:(i + 1) * t, :] = (blk + mirrored).astype(BF16)
        odd_ref[i * t:(i + 1) * t, :] = (blk - mirrored).astype(BF16)


def _fold_sequence(u3, rev):
    b, s, _ = u3.shape
    halo = POOL_HALO
    rows = FOLD_TILES * SEQ_TILE
    n_blocks = s // rows
    block = lambda block_of: pl.BlockSpec((None, rows, FOURIER_WIDTH), lambda bi, j: (bi, block_of(j), 1))
    after = pl.BlockSpec((None, halo, FOURIER_WIDTH),
                         lambda bi, j: (bi, jnp.minimum((n_blocks - j) * (rows // halo), s // halo - 1), 1))
    out = pl.BlockSpec((None, rows, FOURIER_WIDTH), lambda bi, j: (bi, j, 0))
    return pl.pallas_call(
        _fold_kernel,
        grid=(b, n_blocks // 2),
        in_specs=[block(lambda j: j), block(lambda j: n_blocks - 1 - j), after,
                  pl.BlockSpec(rev.shape, lambda bi, j: (0, 0))],
        out_specs=[out, out],
        out_shape=[jax.ShapeDtypeStruct((b, s // 2, FOURIER_WIDTH), BF16)] * 2,
        compiler_params=pltpu.CompilerParams(
            dimension_semantics=("arbitrary", "arbitrary"),
            vmem_limit_bytes=2 * 4 * rows * FOURIER_WIDTH * 2 + 8 * MIB),
        name="fold_sequence",
    )(u3, u3, u3, rev)


def _mix_kernel(even_ref, odd_ref, umid_ref, up_lo_ref, prev_lo_ref, next_lo_ref, up_hi_ref, prev_hi_ref, next_hi_ref,
                c0_ref, s0_ref, cph_ref, sph_ref, band_lo_ref, band_hi_ref, rev_ref,
                cw_ref, sw_ref, cast_src_ref, lo_ref, hi_ref, cast_dst_ref, lhs_ref):
    t = SEQ_TILE
    tp = DFT_ROWS
    gd = POOL_GROUP_DIM
    hd = FOURIER_HEAD_DIM

    @pl.when(pl.program_id(1) == 0)
    def _():
        cph = cph_ref[0]
        sph = sph_ref[0]
        for r0 in range(0, tp, TWIDDLE_ROWS):
            c0 = c0_ref[r0:r0 + TWIDDLE_ROWS, :]
            s0 = s0_ref[r0:r0 + TWIDDLE_ROWS, :]
            lhs_ref[r0:r0 + TWIDDLE_ROWS, :] = (cph * c0 - sph * s0).astype(BF16)
            lhs_ref[tp + r0:tp + r0 + TWIDDLE_ROWS, :] = (sph * c0 + cph * s0).astype(BF16)

    _cast_chunks([cast_src_ref], [cast_dst_ref])
    row = lax.broadcasted_iota(jnp.int32, (tp, 1), 0)
    sign = jnp.where(row % 2 == 0, 1.0, -1.0)
    p_all = _dot(lhs_ref[:tp, :], even_ref[...]) + sign * umid_ref[0:1, :].astype(F32)
    q_all = _dot(lhs_ref[tp:, :], odd_ref[...])
    for h in range(N_FOURIER_HEADS):
        cols = slice(h * hd, (h + 1) * hd)
        out_cols = slice(POOL_WIDTH + h * hd, POOL_WIDTH + (h + 1) * hd)
        pc = _dot(p_all[:, cols].astype(BF16), cw_ref[h])
        qs = _dot(q_all[:, cols].astype(BF16), sw_ref[h])
        lo_ref[:, out_cols] = (pc + qs)[:t].astype(BF16)
        hi_ref[:, out_cols] = _dot(rev_ref[...], (pc - qs).astype(BF16)).astype(BF16)

    def pool(up_ref, prev_ref, next_ref, band_ref, out_ref):
        win = jnp.concatenate([prev_ref[...], up_ref[...], next_ref[...]], axis=0)
        for g in range(N_POOL_GROUPS):
            cols = slice(g * gd, (g + 1) * gd)
            out_ref[:, cols] = _dot(band_ref[0, g], win[:, cols]).astype(BF16)

    pool(up_lo_ref, prev_lo_ref, next_lo_ref, band_lo_ref, lo_ref)
    pool(up_hi_ref, prev_hi_ref, next_hi_ref, band_hi_ref, hi_ref)


def _mix(u3, cw, sw, cast_weight):
    b, s, d = u3.shape
    t = SEQ_TILE
    tp = DFT_ROWS
    halo = POOL_HALO
    n_tiles = s // t
    n_steps = n_tiles // 2
    halo_blocks_per_tile = t // halo
    last_halo_block = s // halo - 1

    sh = s // 2
    c0, s0 = _twiddle(np.arange(tp), np.arange(sh), s)
    cph, sph = _twiddle(np.arange(0, sh, t), np.arange(sh), s)
    band = jnp.asarray(_pool_band(s, t, halo)).astype(BF16)
    rev = np.zeros((t, tp), np.float32)
    rev[np.arange(t), t - np.arange(t)] = 1.0
    rev = jnp.asarray(rev).astype(BF16)
    even, odd = _fold_sequence(u3, rev)

    hi_tile = lambda m: n_tiles - 1 - m
    tile_spec = lambda tile_of: pl.BlockSpec((None, t, POOL_WIDTH), lambda m, bi: (bi, tile_of(m), 0))
    prev_spec = lambda tile_of: pl.BlockSpec(
        (None, halo, POOL_WIDTH),
        lambda m, bi: (bi, jnp.maximum(tile_of(m) * halo_blocks_per_tile - 1, 0), 0))
    next_spec = lambda tile_of: pl.BlockSpec(
        (None, halo, POOL_WIDTH),
        lambda m, bi: (bi, jnp.minimum((tile_of(m) + 1) * halo_blocks_per_tile, last_halo_block), 0))
    band_shape = (1, N_POOL_GROUPS, t, t + 2 * halo)
    const2 = lambda shape: pl.BlockSpec(shape, lambda m, bi: (0, 0))
    const3 = lambda shape: pl.BlockSpec(shape, lambda m, bi: (0, 0, 0))
    lo_tile = lambda m: m

    in_specs = [
        pl.BlockSpec((None, sh, FOURIER_WIDTH), lambda m, bi: (bi, 0, 0)),
        pl.BlockSpec((None, sh, FOURIER_WIDTH), lambda m, bi: (bi, 0, 0)),
        pl.BlockSpec((None, halo, FOURIER_WIDTH), lambda m, bi: (bi, sh // halo, 1)),
        tile_spec(lo_tile), prev_spec(lo_tile), next_spec(lo_tile),
        tile_spec(hi_tile), prev_spec(hi_tile), next_spec(hi_tile),
        pl.BlockSpec((tp, sh), lambda m, bi: (0, 0), pipeline_mode=pl.Buffered(1)),
        pl.BlockSpec((tp, sh), lambda m, bi: (0, 0), pipeline_mode=pl.Buffered(1)),
        pl.BlockSpec((1, 1, sh), lambda m, bi: (m, 0, 0)),
        pl.BlockSpec((1, 1, sh), lambda m, bi: (m, 0, 0)),
        pl.BlockSpec(band_shape, lambda m, bi: (jnp.where(m == 0, 0, 1), 0, 0, 0)),
        pl.BlockSpec(band_shape, lambda m, bi: (jnp.where(m == 0, 2, 1), 0, 0, 0)),
        const2((t, tp)),
        const3((N_FOURIER_HEADS, FOURIER_HEAD_DIM, FOURIER_HEAD_DIM)),
        const3((N_FOURIER_HEADS, FOURIER_HEAD_DIM, FOURIER_HEAD_DIM)),
    ]
    half_out = pl.BlockSpec((None, t, d), lambda m, bi: (bi, m, 0))
    hi_out = pl.BlockSpec((None, t, d), lambda m, bi: (bi, n_steps - 1 - m, 0))
    cast = _CastAlong(cast_weight, n_steps * b, lambda m, bi: m * b + bi)
    vmem = (2 * 2 * sh * FOURIER_WIDTH * 2
            + 2 * tp * sh * 4
            + 2 * tp * sh * 2
            + 2 * 2 * t * d * 2 * 2
            + 4 * 2 * tp * FOURIER_WIDTH * 4
            + cast.vmem_bytes)
    lo, hi, cast_out = pl.pallas_call(
        _mix_kernel,
        grid=(n_steps, b),
        in_specs=in_specs + [cast.spec],
        out_specs=[half_out, hi_out, cast.spec],
        out_shape=[jax.ShapeDtypeStruct((b, s // 2, d), BF16)] * 2 + [cast.out_shape],
        scratch_shapes=[pltpu.VMEM((2 * tp, sh), BF16)],
        compiler_params=pltpu.CompilerParams(
            dimension_semantics=("arbitrary", "arbitrary"), vmem_limit_bytes=vmem + 4 * MIB),
        name="seq_mix",
    )(even, odd, u3, u3, u3, u3, u3, u3, u3, jnp.asarray(c0), jnp.asarray(s0),
      jnp.asarray(cph).reshape(n_steps, 1, sh), jnp.asarray(sph).reshape(n_steps, 1, sh),
      band, band, rev,
      cw, sw, cast.src)
    return lo, hi, cast_out.reshape(cast.shape)


def _out_proj_route_kernel(x_ref, mixed_lo_ref, mixed_hi_ref, wtop_ref, wbot_ref, g_ref, wr_ref, br_ref,
                           tri_ref, pick_ref, cast_src_ref, rows_ref, route_ref, counts_ref, cast_dst_ref,
                           carry_ref, x1_ref, *, tiles_per_seq, n_tiles):
    step = pl.program_id(0)

    @pl.when(step == 0)
    def _():
        carry_ref[...] = jnp.zeros_like(carry_ref)
        x1_ref[...] = jnp.zeros_like(x1_ref)

    _cast_chunks([cast_src_ref], [cast_dst_ref])

    x1 = x1_ref[...]
    rows_ref[:, :D_MODEL] = x1
    h2 = _rmsnorm(x1, g_ref[...])
    logits = _dot(h2.astype(BF16), wr_ref[...]) + br_ref[...]

    tile = jnp.minimum(step, n_tiles - 1)
    in_lo_half = (tile % tiles_per_seq) < tiles_per_seq // 2
    mixed = jnp.where(in_lo_half, mixed_lo_ref[...], mixed_hi_ref[...])
    x1_ref[...] = (x_ref[...] + _dot(mixed[:, :POOL_WIDTH], wtop_ref[...])
                   + _dot(mixed[:, POOL_WIDTH:], wbot_ref[...]))

    lane = lax.broadcasted_iota(jnp.int32, logits.shape, 1)
    neg = jnp.float32(-jnp.inf)
    big = jnp.int32(ROUTER_LANES)

    is_group = lane < N_EXPERT_GROUPS
    gl = jnp.where(is_group, logits, neg)
    gmax = jnp.max(gl, axis=-1, keepdims=True)
    gidx = jnp.min(jnp.where(gl == gmax, lane, big), axis=-1, keepdims=True)
    p_g = 1.0 / jnp.sum(jnp.exp(gl - gmax), axis=-1, keepdims=True)

    e_lane = lane - N_EXPERT_GROUPS
    in_group = (e_lane >= gidx * EXPERTS_PER_GROUP) & (e_lane < (gidx + 1) * EXPERTS_PER_GROUP)
    el = jnp.where(in_group, logits, neg)
    v1 = jnp.max(el, axis=-1, keepdims=True)
    i1 = jnp.min(jnp.where(el == v1, lane, big), axis=-1, keepdims=True)
    el2 = jnp.where(lane == i1, neg, el)
    v2 = jnp.max(el2, axis=-1, keepdims=True)
    i2 = jnp.min(jnp.where(el2 == v2, lane, big), axis=-1, keepdims=True)
    r = jnp.exp(v2 - v1)
    w1 = p_g / (1.0 + r)
    w2 = p_g * r / (1.0 + r)

    first_is_low = i1 < i2
    first_lane = N_EXPERT_GROUPS + gidx * EXPERTS_PER_GROUP
    la = jnp.where(first_is_low, i1, i2) - first_lane
    lb = jnp.where(first_is_low, i2, i1) - first_lane
    w_low = jnp.where(first_is_low, w1, w2)
    w_high = jnp.where(first_is_low, w2, w1)
    assert (PAIR_SLOT_A, PAIR_SLOT_B) == ((0, 0, 0, 1, 1, 3), (1, 2, 3, 3, 2, 2))
    pair = jnp.where(la == 0, lb - 1, jnp.where(la == 1, 6 - lb, 5))
    slot_a_is_high = la == 2
    w_a = jnp.where(slot_a_is_high, w_high, w_low)
    w_b = jnp.where(slot_a_is_high, w_low, w_high)
    cls = gidx * PAIRS_PER_GROUP + pair
    rows_ref[:, D_MODEL:] = jnp.where(lane == 0, w_a, jnp.where(lane == 1, w_b, 0.0))

    onehot = jnp.where(lane == cls, 1.0, 0.0)
    before = _dot(tri_ref[...], onehot.astype(BF16)) + carry_ref[...]
    rank = jnp.sum(jnp.where(lane == cls, before, 0.0), axis=-1, keepdims=True)
    carry_ref[...] += jnp.sum(onehot, axis=0, keepdims=True) * jnp.where(step > 0, 1.0, 0.0)
    counts_ref[...] = carry_ref[...]

    rank_hi = jnp.floor(rank * (1.0 / RANK_RADIX))
    rank_lo = rank - rank_hi * RANK_RADIX
    digits = jnp.where(lane == 0, cls.astype(F32),
                       jnp.where(lane == 1, rank_hi, jnp.where(lane == 2, rank_lo, 0.0)))
    route_ref[...] = lax.dot_general(pick_ref[...], digits.astype(BF16), (((1,), (1,)), ((), ())),
                                     preferred_element_type=F32)


def _out_proj_route(x2, mixed_lo, mixed_hi, seq_len, w_top_bf16, w_out_bf16, g_ffn, wr, br, cast_weight):
    n, d = x2.shape
    tm = TOKEN_TILE
    n_tiles = n // tm
    tiles_per_seq = seq_len // tm
    half_tiles = tiles_per_seq // 2
    const = lambda shape: pl.BlockSpec(shape, lambda i: (0, 0))
    in_tile = lambda i: jnp.minimum(i, n_tiles - 1)
    out_tile = lambda i: jnp.maximum(i - 1, 0)
    lo_spec = pl.BlockSpec((tm, d), lambda i: (
        (in_tile(i) // tiles_per_seq) * half_tiles + jnp.minimum(in_tile(i) % tiles_per_seq, half_tiles - 1), 0))
    hi_spec = pl.BlockSpec((tm, d), lambda i: (
        (in_tile(i) // tiles_per_seq) * half_tiles + jnp.maximum(in_tile(i) % tiles_per_seq - half_tiles, 0), 0))
    tri = np.tril(np.ones((tm, tm), np.float32), -1)
    pick = np.eye(SUBLANES, ROUTER_LANES, dtype=np.float32)
    vmem = (2 * tm * d * 4 + 4 * tm * d * 2 + 2 * tm * ROW_WORDS * 4 + d * d * 2 + tm * d * 4
            + 2 * d * ROUTER_LANES * 2 + 2 * tm * tm * 2 + 3 * tm * d * 4)
    cast = _CastAlong(cast_weight, n_tiles, in_tile)
    vmem += cast.vmem_bytes
    rows, route, counts, cast_out = pl.pallas_call(
        functools.partial(_out_proj_route_kernel, tiles_per_seq=tiles_per_seq, n_tiles=n_tiles),
        grid=(n_tiles + 1,),
        in_specs=[pl.BlockSpec((tm, d), lambda i: (in_tile(i), 0)), lo_spec, hi_spec,
                  pl.BlockSpec((POOL_WIDTH, d), lambda i: (0, 0), pipeline_mode=pl.Buffered(1)),
                  pl.BlockSpec((FOURIER_WIDTH, d), lambda i: (1, 0), pipeline_mode=pl.Buffered(1)),
                  const((1, d)), const((d, ROUTER_LANES)), const((1, ROUTER_LANES)),
                  const((tm, tm)), const((SUBLANES, ROUTER_LANES)), cast.spec],
        out_specs=[pl.BlockSpec((tm, ROW_WORDS), lambda i: (out_tile(i), 0)),
                   pl.BlockSpec((SUBLANES, tm), lambda i: (0, out_tile(i))),
                   const((1, ROUTER_LANES)), cast.spec],
        out_shape=[jax.ShapeDtypeStruct((n, ROW_WORDS), F32),
                   jax.ShapeDtypeStruct((SUBLANES, n), F32),
                   jax.ShapeDtypeStruct((1, ROUTER_LANES), F32), cast.out_shape],
        scratch_shapes=[pltpu.VMEM((1, ROUTER_LANES), F32), pltpu.VMEM((tm, d), F32)],
        compiler_params=pltpu.CompilerParams(
            dimension_semantics=("arbitrary",), vmem_limit_bytes=vmem + 4 * MIB),
        name="out_proj_route",
    )(x2, mixed_lo, mixed_hi, w_top_bf16, w_out_bf16, g_ffn.reshape(1, d), wr, br,
      jnp.asarray(tri).astype(BF16), jnp.asarray(pick).astype(BF16), cast.src)
    return rows, route, counts, cast_out.reshape(cast.shape)


def _row_copy(src_ref, src_row, dst_ref, dst_row, sem):
    return pltpu.make_async_copy(src_ref.at[pl.ds(src_row, 1), :], dst_ref.at[pl.ds(dst_row, 1), :], sem)


def _plan_kernel(cnt_ref, route_ref, tok_ref, start_ref, ea_ref, eb_ref, lo_ref, hi_ref, nused_ref,
                 cstart_ref, pos_vmem_ref, pos_smem_ref, sem):
    tm = EXPERT_TILE
    shift = tm.bit_length() - 1
    n = tok_ref.shape[0]
    n_items = start_ref.shape[0]

    run = jnp.int32(0)
    inside = jnp.bool_(True)
    for c in range(N_CLASSES):
        cnt = cnt_ref[c]
        cstart_ref[c] = run
        inside = inside & ((cnt == 0) | (cnt >= tm))
        run = run + cnt

    cls_row = route_ref[0:1, :]
    slot_row = route_ref[1:2, :] * RANK_RADIX + route_ref[2:3, :]
    for c in range(N_CLASSES):
        slot_row = slot_row + jnp.where(cls_row == c, cstart_ref[c].astype(F32), 0.0)
    pos_vmem_ref[...] = slot_row.astype(jnp.int32)
    to_smem = pltpu.make_async_copy(pos_vmem_ref, pos_smem_ref, sem)
    to_smem.start()

    k = jnp.int32(0)
    for c in range(N_CLASSES):
        cnt = cnt_ref[c]
        cs = cstart_ref[c]
        ce = cs + cnt
        group, pair = divmod(c, PAIRS_PER_GROUP)
        ea = group * EXPERTS_PER_GROUP + PAIR_SLOT_A[pair]
        eb = group * EXPERTS_PER_GROUP + PAIR_SLOT_B[pair]
        first_window = cs >> shift
        n_inside = (cnt + tm - 1) >> shift
        n_aligned = jnp.where(cnt > 0, ((ce - 1) >> shift) - first_window + 1, 0)
        full_windows = cnt >> shift

        def emit(i, carry, k=k, cs=cs, ce=ce, ea=ea, eb=eb, first_window=first_window,
                 full_windows=full_windows):
            start_inside = jnp.where(i < full_windows, cs + i * tm, ce - tm)
            start_aligned = (first_window + i) * tm
            start_ref[k + i] = jnp.where(inside, start_inside, start_aligned)
            ea_ref[k + i] = ea
            eb_ref[k + i] = eb
            lo_ref[k + i] = jnp.where(inside, 0, jnp.clip(cs - start_aligned, 0, tm))
            hi_ref[k + i] = jnp.where(inside, tm, jnp.clip(ce - start_aligned, 0, tm))
            return carry

        n_class_items = jnp.where(inside, n_inside, n_aligned)
        lax.fori_loop(0, n_class_items, emit, 0)
        k = k + n_class_items
    nused_ref[0] = k

    def repeat_last(i, carry):
        for ref in (start_ref, ea_ref, eb_ref, lo_ref, hi_ref):
            ref[i] = ref[k - 1]
        return carry

    lax.fori_loop(k, n_items, repeat_last, 0)

    to_smem.wait()

    def invert(t, carry):
        tok_ref[pos_smem_ref[0, t]] = t
        return carry

    lax.fori_loop(0, n, invert, 0, unroll=INVERT_UNROLL)


def _plan_routing(route, counts, n):
    n_items = n // EXPERT_TILE + N_CLASSES
    smem = lambda size: jax.ShapeDtypeStruct((size,), jnp.int32)
    smem_spec = pl.BlockSpec(memory_space=pltpu.SMEM)
    return pl.pallas_call(
        _plan_kernel,
        in_specs=[smem_spec, pl.BlockSpec(memory_space=pltpu.VMEM)],
        out_specs=[smem_spec] * 7,
        out_shape=[smem(n)] + [smem(n_items)] * 5 + [smem(1)],
        scratch_shapes=[pltpu.SMEM((N_CLASSES,), jnp.int32), pltpu.VMEM((1, n), jnp.int32),
                        pltpu.SMEM((1, n), jnp.int32), pltpu.SemaphoreType.DMA(())],
        name="routing_plan",
    )(counts[0, :N_CLASSES].astype(jnp.int32), route)


def _expert_pair_kernel(start_ref, ea_ref, eb_ref, lo_ref, hi_ref, nused_ref, tok_ref,
                        rows_hbm_ref, gffn_ref, gfin_ref,
                        wga_ref, wua_ref, wda_ref, wgb_ref, wub_ref, wdb_ref, out_hbm_ref,
                        buf_ref, acc_ref, gather_sem, scatter_sem):
    del ea_ref, eb_ref
    tm = EXPERT_TILE
    j = pl.program_id(0)
    last = nused_ref[0] - 1
    slot = j % 2
    other = 1 - slot

    def start_gather(item, dst_slot, rows=range(tm)):
        base = start_ref[item]
        for r in rows:
            _row_copy(rows_hbm_ref, tok_ref[base + r], buf_ref.at[dst_slot], r, gather_sem.at[dst_slot]).start()

    def wait_gather(dst_slot):
        pltpu.make_async_copy(rows_hbm_ref.at[pl.ds(0, tm), :], buf_ref.at[dst_slot],
                              gather_sem.at[dst_slot]).wait()

    def start_scatter(item, src_slot, rows=range(tm)):
        base = start_ref[item]
        for r in rows:
            _row_copy(acc_ref.at[src_slot], r, out_hbm_ref, tok_ref[base + r], scatter_sem).start()

    def wait_scatter():
        pltpu.make_async_copy(acc_ref.at[0], out_hbm_ref.at[pl.ds(0, tm), :], scatter_sem).wait()

    gslot = j % GATHER_SLOTS

    @pl.when(j == 0)
    def _():
        acc_ref[...] = jnp.zeros_like(acc_ref)
        start_gather(0, 0)
        start_gather(jnp.minimum(1, last), 1)
        start_scatter(0, 1)

    @pl.when(j <= last)
    def _():
        wait_scatter()
        wait_gather(gslot)
        prev_item = jnp.maximum(j - 1, 0)
        ahead_item = jnp.minimum(j + 2, last)

        def issue_copies(part):
            half_parts = DMA_PARTS // 2
            rows = range((part % half_parts) * tm // half_parts, (part % half_parts + 1) * tm // half_parts)
            if part < half_parts:
                start_scatter(prev_item, other, rows)
            else:
                start_gather(ahead_item, (j + 2) % GATHER_SLOTS, rows)

        lo = lo_ref[j]
        hi = hi_ref[j]
        x1 = buf_ref[gslot, :, :D_MODEL]
        h = _rmsnorm(x1, gffn_ref[...]).astype(BF16)

        part = 0
        ff_chunk = EXPERT_FF * 4 // DMA_PARTS
        for slot_lane, (wg_ref, wu_ref, wd_ref) in enumerate(((wga_ref, wua_ref, wda_ref),
                                                              (wgb_ref, wub_ref, wdb_ref))):
            for c0 in range(0, EXPERT_FF, ff_chunk):
                issue_copies(part)
                wts = buf_ref[gslot, :, D_MODEL:]
                lane = lax.broadcasted_iota(jnp.int32, wts.shape, 1)
                w = jnp.sum(jnp.where(lane == slot_lane, wts, 0.0), axis=-1, keepdims=True)
                a = _dot(h, wg_ref[0, :, c0:c0 + ff_chunk])
                v = _dot(h, wu_ref[0, :, c0:c0 + ff_chunk])
                act = (a * (1.0 / (1.0 + jnp.exp(-a))) * v * w).astype(BF16)
                for half, d0 in enumerate(range(0, D_MODEL, D_MODEL // 2)):
                    cols = slice(d0, d0 + D_MODEL // 2)
                    if half == 1:
                        issue_copies(part + 1)
                    y_part = _dot(act, wd_ref[0, c0:c0 + ff_chunk, cols])
                    if part == 0:
                        acc_ref[slot, :, cols] = y_part
                    else:
                        acc_ref[slot, :, cols] += y_part
                part += 2
        res = _rmsnorm(buf_ref[gslot, :, :D_MODEL] + acc_ref[slot], gfin_ref[...])

        row = lax.broadcasted_iota(jnp.int32, (tm, 1), 0)
        mine = (row >= lo) & (row < hi)
        acc_ref[slot] = jnp.where(mine, res, jnp.where(lo > 0, acc_ref[other], 0.0))

    @pl.when(j == last)
    def _():
        wait_scatter()
        start_scatter(j, slot)
        wait_scatter()
        wait_gather((j + 1) % GATHER_SLOTS)
        wait_gather((j + 2) % GATHER_SLOTS)


def _expert_pairs(item_start, item_ea, item_eb, item_lo, item_hi, n_used, slot_token, rows,
                  g_ffn, g_final, wg, wu, wd):
    n, w = rows.shape
    d, f = D_MODEL, EXPERT_FF
    tm = EXPERT_TILE
    gate_a = pl.BlockSpec((1, d, f), lambda j, st, ea, eb, lo, hi, nu, tok: (ea[j], 0, 0))
    gate_b = pl.BlockSpec((1, d, f), lambda j, st, ea, eb, lo, hi, nu, tok: (eb[j], 0, 0))
    down_a = pl.BlockSpec((1, f, d), lambda j, st, ea, eb, lo, hi, nu, tok: (ea[j], 0, 0))
    down_b = pl.BlockSpec((1, f, d), lambda j, st, ea, eb, lo, hi, nu, tok: (eb[j], 0, 0))
    gain = pl.BlockSpec((1, d), lambda j, st, ea, eb, lo, hi, nu, tok: (0, 0))
    grid_spec = pltpu.PrefetchScalarGridSpec(
        num_scalar_prefetch=7,
        grid=(item_start.shape[0],),
        in_specs=[pl.BlockSpec(memory_space=pl.ANY), gain, gain,
                  gate_a, gate_a, down_a, gate_b, gate_b, down_b],
        out_specs=pl.BlockSpec(memory_space=pl.ANY),
        scratch_shapes=[pltpu.VMEM((GATHER_SLOTS, tm, w), F32), pltpu.VMEM((2, tm, d), F32),
                        pltpu.SemaphoreType.DMA((GATHER_SLOTS,)), pltpu.SemaphoreType.DMA(())],
    )
    vmem = 2 * 6 * d * f * 2 + GATHER_SLOTS * tm * w * 4 + 2 * tm * d * 4 + 8 * tm * d * 4
    return pl.pallas_call(
        _expert_pair_kernel,
        grid_spec=grid_spec,
        out_shape=jax.ShapeDtypeStruct((n, d), F32),
        compiler_params=pltpu.CompilerParams(
            dimension_semantics=("arbitrary",), vmem_limit_bytes=vmem + 4 * MIB),
        name="expert_pairs",
    )(item_start, item_ea, item_eb, item_lo, item_hi, n_used, slot_token, rows, g_ffn.reshape(1, d),
      g_final.reshape(1, d), wg, wu, wd, wg, wu, wd)


def kernel(x, g_mix, w_in, w_pool, pool_scale, w_fourier, w_out, g_ffn, w_group_router,
           b_group_router, w_expert_router, b_expert_router, w_gate, w_up, w_down, g_final):
    b, s, d = x.shape
    assert d == D_MODEL and s % (2 * SEQ_TILE) == 0 and s % (2 * TOKEN_TILE) == 0
    assert (b * s) % max(TOKEN_TILE, EXPERT_TILE) == 0
    n = b * s
    x2 = x.reshape(n, d)

    cw, sw, w_top_bf16 = _mixer_weights(w_fourier, w_pool, pool_scale, w_out, s)
    u, (w_out_bf16, w_gate_bf16) = _norm_proj(x2, g_mix, w_in, [w_out, w_gate])
    mixed_lo, mixed_hi, w_up_bf16 = _mix(u.reshape(b, s, d), cw, sw, w_up)

    wr = jnp.concatenate([w_group_router, w_expert_router], axis=1)
    wr = jnp.pad(wr, ((0, 0), (0, ROUTER_LANES - wr.shape[1]))).astype(BF16)
    br = jnp.concatenate([b_group_router, b_expert_router])
    br = jnp.pad(br, (0, ROUTER_LANES - br.shape[0])).reshape(1, ROUTER_LANES)

    rows, route, counts, w_down_bf16 = _out_proj_route(
        x2, mixed_lo.reshape(n // 2, d), mixed_hi.reshape(n // 2, d), s, w_top_bf16, w_out_bf16, g_ffn, wr, br,
        w_down)
    slot_token, item_start, item_ea, item_eb, item_lo, item_hi, n_used = _plan_routing(route, counts, n)
    out = _expert_pairs(item_start, item_ea, item_eb, item_lo, item_hi, n_used, slot_token, rows,
                        g_ffn, g_final, w_gate_bf16, w_up_bf16, w_down_bf16)
    return out.reshape(b, s, d)
```

```python
import functools

import numpy as np
import jax
import jax.numpy as jnp
from jax import lax
from jax.experimental import pallas as pl
from jax.experimental.pallas import tpu as pltpu

D_MODEL = 2048
POOL_WINDOWS = (2, 4, 8, 16)
N_POOL_GROUPS = len(POOL_WINDOWS)
POOL_WIDTH = D_MODEL // 2
POOL_GROUP_DIM = POOL_WIDTH // N_POOL_GROUPS
FOURIER_WIDTH = D_MODEL - POOL_WIDTH
N_FOURIER_HEADS = 4
FOURIER_HEAD_DIM = FOURIER_WIDTH // N_FOURIER_HEADS
N_EXPERT_GROUPS = 4
EXPERTS_PER_GROUP = 4
EXPERT_FF = D_MODEL // 4
RMS_EPS = 1e-6

LANES = 128
SUBLANES = 8
BF16_SUBLANES = 16
ROUTER_LANES = LANES
MIB = 1024 * 1024

TOKEN_TILE = 512
SEQ_TILE = 256
POOL_HALO = BF16_SUBLANES
DFT_ROWS = SEQ_TILE + BF16_SUBLANES
TWIDDLE_ROWS = 16
FOLD_TILES = 4

PAIR_SLOT_A = (0, 0, 0, 1, 1, 3)
PAIR_SLOT_B = (1, 2, 3, 3, 2, 2)
PAIRS_PER_GROUP = len(PAIR_SLOT_A)
N_CLASSES = N_EXPERT_GROUPS * PAIRS_PER_GROUP
ROW_WORDS = D_MODEL + LANES
RANK_RADIX = 128
EXPERT_TILE = 256
DMA_PARTS = 8
GATHER_SLOTS = 3
INVERT_UNROLL = 32
WEIGHT_CAST_ROWS = 256

BF16 = jnp.bfloat16
F32 = jnp.float32


def _rmsnorm(x, g):
    ms = jnp.mean(x * x, axis=-1, keepdims=True)
    return x * lax.rsqrt(ms + RMS_EPS) * g


def _dot(a, b):
    return jnp.dot(a, b, preferred_element_type=F32)


def _twiddle(rows, cols, period):
    m = (np.asarray(rows, np.int64)[:, None] * np.asarray(cols, np.int64)[None, :]) % period
    ang = (2.0 * np.pi / period) * m.astype(np.float64)
    return np.cos(ang).astype(np.float32), np.sin(ang).astype(np.float32)


def _pool_band(seq_len, tile, halo):
    n_tiles = seq_len // tile
    out = np.zeros((3, N_POOL_GROUPS, tile, tile + 2 * halo), np.float64)
    for v, m in enumerate((0, 1, n_tiles - 1)):
        t0 = m * tile
        for g, k in enumerate(POOL_WINDOWS):
            for r in range(tile):
                t = t0 + r
                lo = max(t - (k - 1) // 2, 0)
                hi = min(t + k // 2 + 1, seq_len)
                out[v, g, r, lo - t0 + halo:hi - t0 + halo] = 1.0 / (hi - lo)
                out[v, g, r, r + halo] -= 1.0
    return out.astype(np.float32)


def _mixer_weight_kernel(cd_ref, sd_ref, wf_ref, wp_ref, ps_ref, wo_ref, cw_ref, sw_ref, wtop_ref, *, scale):
    wf = wf_ref[0]
    cw = jnp.dot(cd_ref[...], wf, preferred_element_type=F32, precision=lax.Precision.HIGHEST)
    sw = jnp.dot(sd_ref[...], wf, preferred_element_type=F32, precision=lax.Precision.HIGHEST)
    cw_ref[0] = (cw * scale).astype(BF16)
    sw_ref[0] = (sw * (-scale)).astype(BF16)
    wtop_ref[...] = _dot((wp_ref[0] * ps_ref[0]).astype(BF16), wo_ref[...].astype(BF16)).astype(BF16)


def _mixer_weights(w_fourier, w_pool, pool_scale, w_out, seq_len):
    dh = FOURIER_HEAD_DIM
    assert N_FOURIER_HEADS == N_POOL_GROUPS and dh == POOL_GROUP_DIM
    d = w_out.shape[1]
    cd, sd = _twiddle(np.arange(dh), np.arange(dh), dh)
    scale = 1.0 / np.sqrt(float(seq_len * dh))
    mat = pl.BlockSpec((dh, dh), lambda h: (0, 0))
    per_head = pl.BlockSpec((1, dh, dh), lambda h: (h, 0, 0))
    out_rows = pl.BlockSpec((dh, d), lambda h: (h, 0))
    return pl.pallas_call(
        functools.partial(_mixer_weight_kernel, scale=scale),
        grid=(N_FOURIER_HEADS,),
        in_specs=[mat, mat, per_head, per_head, pl.BlockSpec((1, 1, dh), lambda h: (h, 0, 0)), out_rows],
        out_specs=[per_head, per_head, out_rows],
        out_shape=[jax.ShapeDtypeStruct((N_FOURIER_HEADS, dh, dh), BF16)] * 2
        + [jax.ShapeDtypeStruct((POOL_WIDTH, d), BF16)],
        name="mixer_weights",
    )(jnp.asarray(cd), jnp.asarray(sd), w_fourier, w_pool, pool_scale.reshape(N_POOL_GROUPS, 1, dh), w_out)


class _CastAlong:
    def __init__(self, w, n_chunks, chunk_of):
        cols = w.shape[-1]
        rows = w.size // cols // n_chunks
        self.shape = w.shape
        self.src = w.reshape(n_chunks, rows, cols)
        self.spec = pl.BlockSpec((1, rows, cols), lambda *idx: (chunk_of(*idx), 0, 0))
        self.out_shape = jax.ShapeDtypeStruct((n_chunks, rows, cols), BF16)
        self.vmem_bytes = 2 * rows * cols * (4 + 2)


def _cast_chunks(src_refs, dst_refs):
    for src_ref, dst_ref in zip(src_refs, dst_refs):
        dst_ref[...] = src_ref[...].astype(BF16)


def _norm_proj_kernel(x_ref, g_ref, w_ref, *rest, n_cast):
    cast_src, (u_ref, *cast_dst), w_bf16_ref = rest[:n_cast], rest[n_cast:-1], rest[-1]

    @pl.when(pl.program_id(0) == 0)
    def _():
        for r0 in range(0, w_ref.shape[0], WEIGHT_CAST_ROWS):
            w_bf16_ref[r0:r0 + WEIGHT_CAST_ROWS, :] = w_ref[r0:r0 + WEIGHT_CAST_ROWS, :].astype(BF16)

    h = _rmsnorm(x_ref[...], g_ref[...])
    u_ref[...] = _dot(h.astype(BF16), w_bf16_ref[...]).astype(BF16)
    _cast_chunks(cast_src, cast_dst)


def _norm_proj(x2, g_mix, w_in, cast_weights):
    n, d = x2.shape
    tm = TOKEN_TILE
    casts = [_CastAlong(w, n // tm, lambda i: i) for w in cast_weights]
    vmem = (2 * tm * d * 4 + d * d * (4 + 2) + 2 * tm * d * 2 + 3 * tm * d * 4
            + sum(c.vmem_bytes for c in casts))
    u, *cast_out = pl.pallas_call(
        functools.partial(_norm_proj_kernel, n_cast=len(casts)),
        grid=(n // tm,),
        in_specs=[
            pl.BlockSpec((tm, d), lambda i: (i, 0)),
            pl.BlockSpec((1, d), lambda i: (0, 0)),
            pl.BlockSpec((d, d), lambda i: (0, 0), pipeline_mode=pl.Buffered(1)),
        ] + [c.spec for c in casts],
        out_specs=[pl.BlockSpec((tm, d), lambda i: (i, 0))] + [c.spec for c in casts],
        out_shape=[jax.ShapeDtypeStruct((n, d), BF16)] + [c.out_shape for c in casts],
        scratch_shapes=[pltpu.VMEM((d, d), BF16)],
        compiler_params=pltpu.CompilerParams(
            dimension_semantics=("arbitrary",), vmem_limit_bytes=vmem + 4 * MIB),
        name="norm_proj",
    )(x2, g_mix.reshape(1, d), w_in, *[c.src for c in casts])
    return u, [o.reshape(c.shape) for o, c in zip(cast_out, casts)]


def _fold_kernel(blk_ref, mirror_ref, after_ref, rev_ref, cw_ref, sw_ref, even_ref, odd_ref):
    t = SEQ_TILE
    halo = POOL_HALO
    hd = FOURIER_HEAD_DIM
    for i in range(FOLD_TILES):
        top = (FOLD_TILES - 1 - i) * t
        if i == 0:
            after = jnp.where(pl.program_id(1) > 0, after_ref[...], jnp.zeros_like(after_ref))
        else:
            after = mirror_ref[top + t:top + t + halo, :]
        window = jnp.concatenate([mirror_ref[top:top + t, :], after], axis=0)
        mirrored = _dot(rev_ref[...], window)
        blk = blk_ref[i * t:(i + 1) * t, :].astype(F32)
        even_ref[i * t:(i + 1) * t, :] = (blk + mirrored).astype(BF16)
        odd_ref[i * t:(i + 1) * t, :] = (blk - mirrored).astype(BF16)
    for h in range(N_FOURIER_HEADS):
        cols = slice(h * hd, (h + 1) * hd)
        even_ref[:, cols] = _dot(even_ref[:, cols], cw_ref[h]).astype(BF16)
        odd_ref[:, cols] = _dot(odd_ref[:, cols], sw_ref[h]).astype(BF16)


def _fold_sequence(u3, rev, cw, sw):
    b, s, _ = u3.shape
    halo = POOL_HALO
    rows = FOLD_TILES * SEQ_TILE
    n_blocks = s // rows
    block = lambda block_of: pl.BlockSpec((None, rows, FOURIER_WIDTH), lambda bi, j: (bi, block_of(j), 1))
    after = pl.BlockSpec((None, halo, FOURIER_WIDTH),
                         lambda bi, j: (bi, jnp.minimum((n_blocks - j) * (rows // halo), s // halo - 1), 1))
    out = pl.BlockSpec((None, rows, FOURIER_WIDTH), lambda bi, j: (bi, j, 0))
    return pl.pallas_call(
        _fold_kernel,
        grid=(b, n_blocks // 2),
        in_specs=[block(lambda j: j), block(lambda j: n_blocks - 1 - j), after,
                  pl.BlockSpec(rev.shape, lambda bi, j: (0, 0)),
                  pl.BlockSpec(cw.shape, lambda bi, j: (0, 0, 0)), pl.BlockSpec(sw.shape, lambda bi, j: (0, 0, 0))],
        out_specs=[out, out],
        out_shape=[jax.ShapeDtypeStruct((b, s // 2, FOURIER_WIDTH), BF16)] * 2,
        compiler_params=pltpu.CompilerParams(
            dimension_semantics=("arbitrary", "arbitrary"),
            vmem_limit_bytes=2 * 4 * rows * FOURIER_WIDTH * 2 + 8 * MIB),
        name="fold_sequence",
    )(u3, u3, u3, rev, cw, sw)


def _mix_kernel(even_ref, odd_ref, umid_ref, up_lo_ref, prev_lo_ref, next_lo_ref, up_hi_ref, prev_hi_ref, next_hi_ref,
                c0_ref, s0_ref, cph_ref, sph_ref, band_lo_ref, band_hi_ref, rev_ref,
                cw_ref, cast_src_ref, lo_ref, hi_ref, cast_dst_ref, lhs_ref):
    t = SEQ_TILE
    tp = DFT_ROWS
    gd = POOL_GROUP_DIM
    hd = FOURIER_HEAD_DIM

    @pl.when(pl.program_id(1) == 0)
    def _():
        cph = cph_ref[0]
        sph = sph_ref[0]
        for r0 in range(0, tp, TWIDDLE_ROWS):
            c0 = c0_ref[r0:r0 + TWIDDLE_ROWS, :]
            s0 = s0_ref[r0:r0 + TWIDDLE_ROWS, :]
            lhs_ref[r0:r0 + TWIDDLE_ROWS, :] = (cph * c0 - sph * s0).astype(BF16)
            lhs_ref[tp + r0:tp + r0 + TWIDDLE_ROWS, :] = (sph * c0 + cph * s0).astype(BF16)

    _cast_chunks([cast_src_ref], [cast_dst_ref])
    row = lax.broadcasted_iota(jnp.int32, (tp, 1), 0)
    sign = jnp.where(row % 2 == 0, 1.0, -1.0)
    mid = jnp.concatenate([_dot(umid_ref[:, h * hd:(h + 1) * hd], cw_ref[h]) for h in range(N_FOURIER_HEADS)],
                          axis=1)
    pc = _dot(lhs_ref[:tp, :], even_ref[...]) + sign * mid[0:1, :]
    qs = _dot(lhs_ref[tp:, :], odd_ref[...])
    lo_ref[:, POOL_WIDTH:] = (pc + qs)[:t].astype(BF16)
    hi_ref[:, POOL_WIDTH:] = _dot(rev_ref[...], (pc - qs).astype(BF16)).astype(BF16)

    def pool(up_ref, prev_ref, next_ref, band_ref, out_ref):
        win = jnp.concatenate([prev_ref[...], up_ref[...], next_ref[...]], axis=0)
        for g in range(N_POOL_GROUPS):
            cols = slice(g * gd, (g + 1) * gd)
            out_ref[:, cols] = _dot(band_ref[0, g], win[:, cols]).astype(BF16)

    pool(up_lo_ref, prev_lo_ref, next_lo_ref, band_lo_ref, lo_ref)
    pool(up_hi_ref, prev_hi_ref, next_hi_ref, band_hi_ref, hi_ref)


def _mix(u3, cw, sw, cast_weight):
    b, s, d = u3.shape
    t = SEQ_TILE
    tp = DFT_ROWS
    halo = POOL_HALO
    n_tiles = s // t
    n_steps = n_tiles // 2
    halo_blocks_per_tile = t // halo
    last_halo_block = s // halo - 1

    sh = s // 2
    c0, s0 = _twiddle(np.arange(tp), np.arange(sh), s)
    cph, sph = _twiddle(np.arange(0, sh, t), np.arange(sh), s)
    band = jnp.asarray(_pool_band(s, t, halo)).astype(BF16)
    rev = np.zeros((t, tp), np.float32)
    rev[np.arange(t), t - np.arange(t)] = 1.0
    rev = jnp.asarray(rev).astype(BF16)
    even, odd = _fold_sequence(u3, rev, cw, sw)

    hi_tile = lambda m: n_tiles - 1 - m
    tile_spec = lambda tile_of: pl.BlockSpec((None, t, POOL_WIDTH), lambda m, bi: (bi, tile_of(m), 0))
    prev_spec = lambda tile_of: pl.BlockSpec(
        (None, halo, POOL_WIDTH),
        lambda m, bi: (bi, jnp.maximum(tile_of(m) * halo_blocks_per_tile - 1, 0), 0))
    next_spec = lambda tile_of: pl.BlockSpec(
        (None, halo, POOL_WIDTH),
        lambda m, bi: (bi, jnp.minimum((tile_of(m) + 1) * halo_blocks_per_tile, last_halo_block), 0))
    band_shape = (1, N_POOL_GROUPS, t, t + 2 * halo)
    const2 = lambda shape: pl.BlockSpec(shape, lambda m, bi: (0, 0))
    const3 = lambda shape: pl.BlockSpec(shape, lambda m, bi: (0, 0, 0))
    lo_tile = lambda m: m

    in_specs = [
        pl.BlockSpec((None, sh, FOURIER_WIDTH), lambda m, bi: (bi, 0, 0)),
        pl.BlockSpec((None, sh, FOURIER_WIDTH), lambda m, bi: (bi, 0, 0)),
        pl.BlockSpec((None, halo, FOURIER_WIDTH), lambda m, bi: (bi, sh // halo, 1)),
        tile_spec(lo_tile), prev_spec(lo_tile), next_spec(lo_tile),
        tile_spec(hi_tile), prev_spec(hi_tile), next_spec(hi_tile),
        pl.BlockSpec((tp, sh), lambda m, bi: (0, 0), pipeline_mode=pl.Buffered(1)),
        pl.BlockSpec((tp, sh), lambda m, bi: (0, 0), pipeline_mode=pl.Buffered(1)),
        pl.BlockSpec((1, 1, sh), lambda m, bi: (m, 0, 0)),
        pl.BlockSpec((1, 1, sh), lambda m, bi: (m, 0, 0)),
        pl.BlockSpec(band_shape, lambda m, bi: (jnp.where(m == 0, 0, 1), 0, 0, 0)),
        pl.BlockSpec(band_shape, lambda m, bi: (jnp.where(m == 0, 2, 1), 0, 0, 0)),
        const2((t, tp)),
        const3((N_FOURIER_HEADS, FOURIER_HEAD_DIM, FOURIER_HEAD_DIM)),
    ]
    half_out = pl.BlockSpec((None, t, d), lambda m, bi: (bi, m, 0))
    hi_out = pl.BlockSpec((None, t, d), lambda m, bi: (bi, n_steps - 1 - m, 0))
    cast = _CastAlong(cast_weight, n_steps * b, lambda m, bi: m * b + bi)
    vmem = (2 * 2 * sh * FOURIER_WIDTH * 2
            + 2 * tp * sh * 4
            + 2 * tp * sh * 2
            + 2 * 2 * t * d * 2 * 2
            + 4 * 2 * tp * FOURIER_WIDTH * 4
            + cast.vmem_bytes)
    lo, hi, cast_out = pl.pallas_call(
        _mix_kernel,
        grid=(n_steps, b),
        in_specs=in_specs + [cast.spec],
        out_specs=[half_out, hi_out, cast.spec],
        out_shape=[jax.ShapeDtypeStruct((b, s // 2, d), BF16)] * 2 + [cast.out_shape],
        scratch_shapes=[pltpu.VMEM((2 * tp, sh), BF16)],
        compiler_params=pltpu.CompilerParams(
            dimension_semantics=("arbitrary", "arbitrary"), vmem_limit_bytes=vmem + 4 * MIB),
        name="seq_mix",
    )(even, odd, u3, u3, u3, u3, u3, u3, u3, jnp.asarray(c0), jnp.asarray(s0),
      jnp.asarray(cph).reshape(n_steps, 1, sh), jnp.asarray(sph).reshape(n_steps, 1, sh),
      band, band, rev,
      cw, cast.src)
    return lo, hi, cast_out.reshape(cast.shape)


def _out_proj_route_kernel(x_ref, mixed_lo_ref, mixed_hi_ref, wtop_ref, wbot_ref, g_ref, wr_ref, br_ref,
                           tri_ref, pick_ref, cast_src_ref, rows_ref, route_ref, counts_ref, cast_dst_ref,
                           carry_ref, x1_ref, *, tiles_per_seq, n_tiles):
    step = pl.program_id(0)

    @pl.when(step == 0)
    def _():
        carry_ref[...] = jnp.zeros_like(carry_ref)
        x1_ref[...] = jnp.zeros_like(x1_ref)

    _cast_chunks([cast_src_ref], [cast_dst_ref])

    x1 = x1_ref[...]
    rows_ref[:, :D_MODEL] = x1
    h2 = _rmsnorm(x1, g_ref[...])
    logits = _dot(h2.astype(BF16), wr_ref[...]) + br_ref[...]

    tile = jnp.minimum(step, n_tiles - 1)
    in_lo_half = (tile % tiles_per_seq) < tiles_per_seq // 2
    mixed = jnp.where(in_lo_half, mixed_lo_ref[...], mixed_hi_ref[...])
    x1_ref[...] = (x_ref[...] + _dot(mixed[:, :POOL_WIDTH], wtop_ref[...])
                   + _dot(mixed[:, POOL_WIDTH:], wbot_ref[...]))

    lane = lax.broadcasted_iota(jnp.int32, logits.shape, 1)
    neg = jnp.float32(-jnp.inf)
    big = jnp.int32(ROUTER_LANES)

    is_group = lane < N_EXPERT_GROUPS
    gl = jnp.where(is_group, logits, neg)
    gmax = jnp.max(gl, axis=-1, keepdims=True)
    gidx = jnp.min(jnp.where(gl == gmax, lane, big), axis=-1, keepdims=True)
    p_g = 1.0 / jnp.sum(jnp.exp(gl - gmax), axis=-1, keepdims=True)

    e_lane = lane - N_EXPERT_GROUPS
    in_group = (e_lane >= gidx * EXPERTS_PER_GROUP) & (e_lane < (gidx + 1) * EXPERTS_PER_GROUP)
    el = jnp.where(in_group, logits, neg)
    v1 = jnp.max(el, axis=-1, keepdims=True)
    i1 = jnp.min(jnp.where(el == v1, lane, big), axis=-1, keepdims=True)
    el2 = jnp.where(lane == i1, neg, el)
    v2 = jnp.max(el2, axis=-1, keepdims=True)
    i2 = jnp.min(jnp.where(el2 == v2, lane, big), axis=-1, keepdims=True)
    r = jnp.exp(v2 - v1)
    w1 = p_g / (1.0 + r)
    w2 = p_g * r / (1.0 + r)

    first_is_low = i1 < i2
    first_lane = N_EXPERT_GROUPS + gidx * EXPERTS_PER_GROUP
    la = jnp.where(first_is_low, i1, i2) - first_lane
    lb = jnp.where(first_is_low, i2, i1) - first_lane
    w_low = jnp.where(first_is_low, w1, w2)
    w_high = jnp.where(first_is_low, w2, w1)
    assert (PAIR_SLOT_A, PAIR_SLOT_B) == ((0, 0, 0, 1, 1, 3), (1, 2, 3, 3, 2, 2))
    pair = jnp.where(la == 0, lb - 1, jnp.where(la == 1, 6 - lb, 5))
    slot_a_is_high = la == 2
    w_a = jnp.where(slot_a_is_high, w_high, w_low)
    w_b = jnp.where(slot_a_is_high, w_low, w_high)
    cls = gidx * PAIRS_PER_GROUP + pair
    rows_ref[:, D_MODEL:] = jnp.where(lane == 0, w_a, jnp.where(lane == 1, w_b, 0.0))

    onehot = jnp.where(lane == cls, 1.0, 0.0)
    before = _dot(tri_ref[...], onehot.astype(BF16)) + carry_ref[...]
    rank = jnp.sum(jnp.where(lane == cls, before, 0.0), axis=-1, keepdims=True)
    carry_ref[...] += jnp.sum(onehot, axis=0, keepdims=True) * jnp.where(step > 0, 1.0, 0.0)
    counts_ref[...] = carry_ref[...]

    rank_hi = jnp.floor(rank * (1.0 / RANK_RADIX))
    rank_lo = rank - rank_hi * RANK_RADIX
    digits = jnp.where(lane == 0, cls.astype(F32),
                       jnp.where(lane == 1, rank_hi, jnp.where(lane == 2, rank_lo, 0.0)))
    route_ref[...] = lax.dot_general(pick_ref[...], digits.astype(BF16), (((1,), (1,)), ((), ())),
                                     preferred_element_type=F32)


def _out_proj_route(x2, mixed_lo, mixed_hi, seq_len, w_top_bf16, w_out_bf16, g_ffn, wr, br, cast_weight):
    n, d = x2.shape
    tm = TOKEN_TILE
    n_tiles = n // tm
    tiles_per_seq = seq_len // tm
    half_tiles = tiles_per_seq // 2
    const = lambda shape: pl.BlockSpec(shape, lambda i: (0, 0))
    in_tile = lambda i: jnp.minimum(i, n_tiles - 1)
    out_tile = lambda i: jnp.maximum(i - 1, 0)
    lo_spec = pl.BlockSpec((tm, d), lambda i: (
        (in_tile(i) // tiles_per_seq) * half_tiles + jnp.minimum(in_tile(i) % tiles_per_seq, half_tiles - 1), 0))
    hi_spec = pl.BlockSpec((tm, d), lambda i: (
        (in_tile(i) // tiles_per_seq) * half_tiles + jnp.maximum(in_tile(i) % tiles_per_seq - half_tiles, 0), 0))
    tri = np.tril(np.ones((tm, tm), np.float32), -1)
    pick = np.eye(SUBLANES, ROUTER_LANES, dtype=np.float32)
    vmem = (2 * tm * d * 4 + 4 * tm * d * 2 + 2 * tm * ROW_WORDS * 4 + d * d * 2 + tm * d * 4
            + 2 * d * ROUTER_LANES * 2 + 2 * tm * tm * 2 + 3 * tm * d * 4)
    cast = _CastAlong(cast_weight, n_tiles, in_tile)
    vmem += cast.vmem_bytes
    rows, route, counts, cast_out = pl.pallas_call(
        functools.partial(_out_proj_route_kernel, tiles_per_seq=tiles_per_seq, n_tiles=n_tiles),
        grid=(n_tiles + 1,),
        in_specs=[pl.BlockSpec((tm, d), lambda i: (in_tile(i), 0)), lo_spec, hi_spec,
                  pl.BlockSpec((POOL_WIDTH, d), lambda i: (0, 0), pipeline_mode=pl.Buffered(1)),
                  pl.BlockSpec((FOURIER_WIDTH, d), lambda i: (1, 0), pipeline_mode=pl.Buffered(1)),
                  const((1, d)), const((d, ROUTER_LANES)), const((1, ROUTER_LANES)),
                  const((tm, tm)), const((SUBLANES, ROUTER_LANES)), cast.spec],
        out_specs=[pl.BlockSpec((tm, ROW_WORDS), lambda i: (out_tile(i), 0)),
                   pl.BlockSpec((SUBLANES, tm), lambda i: (0, out_tile(i))),
                   const((1, ROUTER_LANES)), cast.spec],
        out_shape=[jax.ShapeDtypeStruct((n, ROW_WORDS), F32),
                   jax.ShapeDtypeStruct((SUBLANES, n), F32),
                   jax.ShapeDtypeStruct((1, ROUTER_LANES), F32), cast.out_shape],
        scratch_shapes=[pltpu.VMEM((1, ROUTER_LANES), F32), pltpu.VMEM((tm, d), F32)],
        compiler_params=pltpu.CompilerParams(
            dimension_semantics=("arbitrary",), vmem_limit_bytes=vmem + 4 * MIB),
        name="out_proj_route",
    )(x2, mixed_lo, mixed_hi, w_top_bf16, w_out_bf16, g_ffn.reshape(1, d), wr, br,
      jnp.asarray(tri).astype(BF16), jnp.asarray(pick).astype(BF16), cast.src)
    return rows, route, counts, cast_out.reshape(cast.shape)


def _row_copy(src_ref, src_row, dst_ref, dst_row, sem):
    return pltpu.make_async_copy(src_ref.at[pl.ds(src_row, 1), :], dst_ref.at[pl.ds(dst_row, 1), :], sem)


def _plan_kernel(cnt_ref, route_ref, tok_ref, start_ref, ea_ref, eb_ref, lo_ref, hi_ref, nused_ref,
                 cstart_ref, pos_vmem_ref, pos_smem_ref, sem):
    tm = EXPERT_TILE
    shift = tm.bit_length() - 1
    n = tok_ref.shape[0]
    n_items = start_ref.shape[0]

    run = jnp.int32(0)
    inside = jnp.bool_(True)
    for c in range(N_CLASSES):
        cnt = cnt_ref[c]
        cstart_ref[c] = run
        inside = inside & ((cnt == 0) | (cnt >= tm))
        run = run + cnt

    cls_row = route_ref[0:1, :]
    slot_row = route_ref[1:2, :] * RANK_RADIX + route_ref[2:3, :]
    for c in range(N_CLASSES):
        slot_row = slot_row + jnp.where(cls_row == c, cstart_ref[c].astype(F32), 0.0)
    pos_vmem_ref[...] = slot_row.astype(jnp.int32)
    to_smem = pltpu.make_async_copy(pos_vmem_ref, pos_smem_ref, sem)
    to_smem.start()

    k = jnp.int32(0)
    for c in range(N_CLASSES):
        cnt = cnt_ref[c]
        cs = cstart_ref[c]
        ce = cs + cnt
        group, pair = divmod(c, PAIRS_PER_GROUP)
        ea = group * EXPERTS_PER_GROUP + PAIR_SLOT_A[pair]
        eb = group * EXPERTS_PER_GROUP + PAIR_SLOT_B[pair]
        first_window = cs >> shift
        n_inside = (cnt + tm - 1) >> shift
        n_aligned = jnp.where(cnt > 0, ((ce - 1) >> shift) - first_window + 1, 0)
        full_windows = cnt >> shift

        def emit(i, carry, k=k, cs=cs, ce=ce, ea=ea, eb=eb, first_window=first_window,
                 full_windows=full_windows):
            start_inside = jnp.where(i < full_windows, cs + i * tm, ce - tm)
            start_aligned = (first_window + i) * tm
            start_ref[k + i] = jnp.where(inside, start_inside, start_aligned)
            ea_ref[k + i] = ea
            eb_ref[k + i] = eb
            lo_ref[k + i] = jnp.where(inside, 0, jnp.clip(cs - start_aligned, 0, tm))
            hi_ref[k + i] = jnp.where(inside, tm, jnp.clip(ce - start_aligned, 0, tm))
            return carry

        n_class_items = jnp.where(inside, n_inside, n_aligned)
        lax.fori_loop(0, n_class_items, emit, 0)
        k = k + n_class_items
    nused_ref[0] = k

    def repeat_last(i, carry):
        for ref in (start_ref, ea_ref, eb_ref, lo_ref, hi_ref):
            ref[i] = ref[k - 1]
        return carry

    lax.fori_loop(k, n_items, repeat_last, 0)

    to_smem.wait()

    def invert(t, carry):
        tok_ref[pos_smem_ref[0, t]] = t
        return carry

    lax.fori_loop(0, n, invert, 0, unroll=INVERT_UNROLL)


def _plan_routing(route, counts, n):
    n_items = n // EXPERT_TILE + N_CLASSES
    smem = lambda size: jax.ShapeDtypeStruct((size,), jnp.int32)
    smem_spec = pl.BlockSpec(memory_space=pltpu.SMEM)
    return pl.pallas_call(
        _plan_kernel,
        in_specs=[smem_spec, pl.BlockSpec(memory_space=pltpu.VMEM)],
        out_specs=[smem_spec] * 7,
        out_shape=[smem(n)] + [smem(n_items)] * 5 + [smem(1)],
        scratch_shapes=[pltpu.SMEM((N_CLASSES,), jnp.int32), pltpu.VMEM((1, n), jnp.int32),
                        pltpu.SMEM((1, n), jnp.int32), pltpu.SemaphoreType.DMA(())],
        name="routing_plan",
    )(counts[0, :N_CLASSES].astype(jnp.int32), route)


def _expert_pair_kernel(start_ref, ea_ref, eb_ref, lo_ref, hi_ref, nused_ref, tok_ref,
                        rows_hbm_ref, gffn_ref, gfin_ref,
                        wga_ref, wua_ref, wda_ref, wgb_ref, wub_ref, wdb_ref, out_hbm_ref,
                        buf_ref, acc_ref, gather_sem, scatter_sem):
    del ea_ref, eb_ref
    tm = EXPERT_TILE
    j = pl.program_id(0)
    last = nused_ref[0] - 1
    slot = j % 2
    other = 1 - slot

    def start_gather(item, dst_slot, rows=range(tm)):
        base = start_ref[item]
        for r in rows:
            _row_copy(rows_hbm_ref, tok_ref[base + r], buf_ref.at[dst_slot], r, gather_sem.at[dst_slot]).start()

    def wait_gather(dst_slot):
        pltpu.make_async_copy(rows_hbm_ref.at[pl.ds(0, tm), :], buf_ref.at[dst_slot],
                              gather_sem.at[dst_slot]).wait()

    def start_scatter(item, src_slot, rows=range(tm)):
        base = start_ref[item]
        for r in rows:
            _row_copy(acc_ref.at[src_slot], r, out_hbm_ref, tok_ref[base + r], scatter_sem).start()

    def wait_scatter():
        pltpu.make_async_copy(acc_ref.at[0], out_hbm_ref.at[pl.ds(0, tm), :], scatter_sem).wait()

    gslot = j % GATHER_SLOTS

    @pl.when(j == 0)
    def _():
        acc_ref[...] = jnp.zeros_like(acc_ref)
        start_gather(0, 0)
        start_gather(jnp.minimum(1, last), 1)
        start_scatter(0, 1)

    @pl.when(j <= last)
    def _():
        wait_scatter()
        wait_gather(gslot)
        prev_item = jnp.maximum(j - 1, 0)
        ahead_item = jnp.minimum(j + 2, last)

        def issue_copies(part):
            half_parts = DMA_PARTS // 2
            rows = range((part % half_parts) * tm // half_parts, (part % half_parts + 1) * tm // half_parts)
            if part < half_parts:
                start_scatter(prev_item, other, rows)
            else:
                start_gather(ahead_item, (j + 2) % GATHER_SLOTS, rows)

        lo = lo_ref[j]
        hi = hi_ref[j]
        x1 = buf_ref[gslot, :, :D_MODEL]
        h = _rmsnorm(x1, gffn_ref[...]).astype(BF16)

        part = 0
        ff_chunk = EXPERT_FF * 4 // DMA_PARTS
        for slot_lane, (wg_ref, wu_ref, wd_ref) in enumerate(((wga_ref, wua_ref, wda_ref),
                                                              (wgb_ref, wub_ref, wdb_ref))):
            for c0 in range(0, EXPERT_FF, ff_chunk):
                issue_copies(part)
                wts = buf_ref[gslot, :, D_MODEL:]
                lane = lax.broadcasted_iota(jnp.int32, wts.shape, 1)
                w = jnp.sum(jnp.where(lane == slot_lane, wts, 0.0), axis=-1, keepdims=True)
                a = _dot(h, wg_ref[0, :, c0:c0 + ff_chunk])
                v = _dot(h, wu_ref[0, :, c0:c0 + ff_chunk])
                act = (a * (1.0 / (1.0 + jnp.exp(-a))) * v * w).astype(BF16)
                for half, d0 in enumerate(range(0, D_MODEL, D_MODEL // 2)):
                    cols = slice(d0, d0 + D_MODEL // 2)
                    if half == 1:
                        issue_copies(part + 1)
                    y_part = _dot(act, wd_ref[0, c0:c0 + ff_chunk, cols])
                    if part == 0:
                        acc_ref[slot, :, cols] = y_part
                    else:
                        acc_ref[slot, :, cols] += y_part
                part += 2
        res = _rmsnorm(buf_ref[gslot, :, :D_MODEL] + acc_ref[slot], gfin_ref[...])

        row = lax.broadcasted_iota(jnp.int32, (tm, 1), 0)
        mine = (row >= lo) & (row < hi)
        acc_ref[slot] = jnp.where(mine, res, jnp.where(lo > 0, acc_ref[other], 0.0))

    @pl.when(j == last)
    def _():
        wait_scatter()
        start_scatter(j, slot)
        wait_scatter()
        wait_gather((j + 1) % GATHER_SLOTS)
        wait_gather((j + 2) % GATHER_SLOTS)


def _expert_pairs(item_start, item_ea, item_eb, item_lo, item_hi, n_used, slot_token, rows,
                  g_ffn, g_final, wg, wu, wd):
    n, w = rows.shape
    d, f = D_MODEL, EXPERT_FF
    tm = EXPERT_TILE
    gate_a = pl.BlockSpec((1, d, f), lambda j, st, ea, eb, lo, hi, nu, tok: (ea[j], 0, 0))
    gate_b = pl.BlockSpec((1, d, f), lambda j, st, ea, eb, lo, hi, nu, tok: (eb[j], 0, 0))
    down_a = pl.BlockSpec((1, f, d), lambda j, st, ea, eb, lo, hi, nu, tok: (ea[j], 0, 0))
    down_b = pl.BlockSpec((1, f, d), lambda j, st, ea, eb, lo, hi, nu, tok: (eb[j], 0, 0))
    gain = pl.BlockSpec((1, d), lambda j, st, ea, eb, lo, hi, nu, tok: (0, 0))
    grid_spec = pltpu.PrefetchScalarGridSpec(
        num_scalar_prefetch=7,
        grid=(item_start.shape[0],),
        in_specs=[pl.BlockSpec(memory_space=pl.ANY), gain, gain,
                  gate_a, gate_a, down_a, gate_b, gate_b, down_b],
        out_specs=pl.BlockSpec(memory_space=pl.ANY),
        scratch_shapes=[pltpu.VMEM((GATHER_SLOTS, tm, w), F32), pltpu.VMEM((2, tm, d), F32),
                        pltpu.SemaphoreType.DMA((GATHER_SLOTS,)), pltpu.SemaphoreType.DMA(())],
    )
    vmem = 2 * 6 * d * f * 2 + GATHER_SLOTS * tm * w * 4 + 2 * tm * d * 4 + 8 * tm * d * 4
    return pl.pallas_call(
        _expert_pair_kernel,
        grid_spec=grid_spec,
        out_shape=jax.ShapeDtypeStruct((n, d), F32),
        compiler_params=pltpu.CompilerParams(
            dimension_semantics=("arbitrary",), vmem_limit_bytes=vmem + 4 * MIB),
        name="expert_pairs",
    )(item_start, item_ea, item_eb, item_lo, item_hi, n_used, slot_token, rows, g_ffn.reshape(1, d),
      g_final.reshape(1, d), wg, wu, wd, wg, wu, wd)


def kernel(x, g_mix, w_in, w_pool, pool_scale, w_fourier, w_out, g_ffn, w_group_router,
           b_group_router, w_expert_router, b_expert_router, w_gate, w_up, w_down, g_final):
    b, s, d = x.shape
    assert d == D_MODEL and s % (2 * SEQ_TILE) == 0 and s % (2 * TOKEN_TILE) == 0
    assert (b * s) % max(TOKEN_TILE, EXPERT_TILE) == 0
    n = b * s
    x2 = x.reshape(n, d)

    cw, sw, w_top_bf16 = _mixer_weights(w_fourier, w_pool, pool_scale, w_out, s)
    u, (w_out_bf16, w_gate_bf16) = _norm_proj(x2, g_mix, w_in, [w_out, w_gate])
    mixed_lo, mixed_hi, w_up_bf16 = _mix(u.reshape(b, s, d), cw, sw, w_up)

    wr = jnp.concatenate([w_group_router, w_expert_router], axis=1)
    wr = jnp.pad(wr, ((0, 0), (0, ROUTER_LANES - wr.shape[1]))).astype(BF16)
    br = jnp.concatenate([b_group_router, b_expert_router])
    br = jnp.pad(br, (0, ROUTER_LANES - br.shape[0])).reshape(1, ROUTER_LANES)

    rows, route, counts, w_down_bf16 = _out_proj_route(
        x2, mixed_lo.reshape(n // 2, d), mixed_hi.reshape(n // 2, d), s, w_top_bf16, w_out_bf16, g_ffn, wr, br,
        w_down)
    slot_token, item_start, item_ea, item_eb, item_lo, item_hi, n_used = _plan_routing(route, counts, n)
    out = _expert_pairs(item_start, item_ea, item_eb, item_lo, item_hi, n_used, slot_token, rows,
                        g_ffn, g_final, w_gate_bf16, w_up_bf16, w_down_bf16)
    return out.reshape(b, s, d)
```

```python
import functools

import numpy as np
import jax
import jax.numpy as jnp
from jax import lax
from jax.experimental import pallas as pl
from jax.experimental.pallas import tpu as pltpu

D_MODEL = 2048
POOL_WINDOWS = (2, 4, 8, 16)
N_POOL_GROUPS = len(POOL_WINDOWS)
POOL_WIDTH = D_MODEL // 2
POOL_GROUP_DIM = POOL_WIDTH // N_POOL_GROUPS
FOURIER_WIDTH = D_MODEL - POOL_WIDTH
N_FOURIER_HEADS = 4
FOURIER_HEAD_DIM = FOURIER_WIDTH // N_FOURIER_HEADS
N_EXPERT_GROUPS = 4
EXPERTS_PER_GROUP = 4
EXPERT_FF = D_MODEL // 4
RMS_EPS = 1e-6

LANES = 128
SUBLANES = 8
BF16_SUBLANES = 16
ROUTER_LANES = LANES
MIB = 1024 * 1024

TOKEN_TILE = 512
SEQ_TILE = 256
POOL_HALO = BF16_SUBLANES
DFT_ROWS = SEQ_TILE + BF16_SUBLANES
TWIDDLE_ROWS = 16
FOLD_TILES = 4

PAIR_SLOT_A = (0, 0, 0, 1, 1, 3)
PAIR_SLOT_B = (1, 2, 3, 3, 2, 2)
PAIRS_PER_GROUP = len(PAIR_SLOT_A)
N_CLASSES = N_EXPERT_GROUPS * PAIRS_PER_GROUP
ROW_WORDS = D_MODEL + LANES
RANK_RADIX = 128
EXPERT_TILE = 256
DMA_PARTS = 8
GATHER_SLOTS = 3
INVERT_UNROLL = 32
WEIGHT_CAST_ROWS = 256

BF16 = jnp.bfloat16
F32 = jnp.float32


def _rmsnorm(x, g):
    ms = jnp.mean(x * x, axis=-1, keepdims=True)
    return x * lax.rsqrt(ms + RMS_EPS) * g


def _dot(a, b):
    return jnp.dot(a, b, preferred_element_type=F32)


def _twiddle(rows, cols, period):
    m = (np.asarray(rows, np.int64)[:, None] * np.asarray(cols, np.int64)[None, :]) % period
    ang = (2.0 * np.pi / period) * m.astype(np.float64)
    return np.cos(ang).astype(np.float32), np.sin(ang).astype(np.float32)


def _pool_band(seq_len, tile, halo):
    n_tiles = seq_len // tile
    out = np.zeros((3, N_POOL_GROUPS, tile, tile + 2 * halo), np.float64)
    for v, m in enumerate((0, 1, n_tiles - 1)):
        t0 = m * tile
        for g, k in enumerate(POOL_WINDOWS):
            for r in range(tile):
                t = t0 + r
                lo = max(t - (k - 1) // 2, 0)
                hi = min(t + k // 2 + 1, seq_len)
                out[v, g, r, lo - t0 + halo:hi - t0 + halo] = 1.0 / (hi - lo)
                out[v, g, r, r + halo] -= 1.0
    return out.astype(np.float32)


def _mixer_weight_kernel(cd_ref, sd_ref, wf_ref, wp_ref, ps_ref, wo_ref, cw_ref, sw_ref, wtop_ref, *, scale):
    wf = wf_ref[0]
    cw = jnp.dot(cd_ref[...], wf, preferred_element_type=F32, precision=lax.Precision.HIGHEST)
    sw = jnp.dot(sd_ref[...], wf, preferred_element_type=F32, precision=lax.Precision.HIGHEST)
    cw_ref[0] = (cw * scale).astype(BF16)
    sw_ref[0] = (sw * (-scale)).astype(BF16)
    wtop_ref[...] = _dot((wp_ref[0] * ps_ref[0]).astype(BF16), wo_ref[...].astype(BF16)).astype(BF16)


def _mixer_weights(w_fourier, w_pool, pool_scale, w_out, seq_len):
    dh = FOURIER_HEAD_DIM
    assert N_FOURIER_HEADS == N_POOL_GROUPS and dh == POOL_GROUP_DIM
    d = w_out.shape[1]
    cd, sd = _twiddle(np.arange(dh), np.arange(dh), dh)
    scale = 1.0 / np.sqrt(float(seq_len * dh))
    mat = pl.BlockSpec((dh, dh), lambda h: (0, 0))
    per_head = pl.BlockSpec((1, dh, dh), lambda h: (h, 0, 0))
    out_rows = pl.BlockSpec((dh, d), lambda h: (h, 0))
    return pl.pallas_call(
        functools.partial(_mixer_weight_kernel, scale=scale),
        grid=(N_FOURIER_HEADS,),
        in_specs=[mat, mat, per_head, per_head, pl.BlockSpec((1, 1, dh), lambda h: (h, 0, 0)), out_rows],
        out_specs=[per_head, per_head, out_rows],
        out_shape=[jax.ShapeDtypeStruct((N_FOURIER_HEADS, dh, dh), BF16)] * 2
        + [jax.ShapeDtypeStruct((POOL_WIDTH, d), BF16)],
        name="mixer_weights",
    )(jnp.asarray(cd), jnp.asarray(sd), w_fourier, w_pool, pool_scale.reshape(N_POOL_GROUPS, 1, dh), w_out)


class _CastAlong:
    def __init__(self, w, n_chunks, chunk_of):
        cols = w.shape[-1]
        rows = w.size // cols // n_chunks
        self.shape = w.shape
        self.src = w.reshape(n_chunks, rows, cols)
        self.spec = pl.BlockSpec((1, rows, cols), lambda *idx: (chunk_of(*idx), 0, 0))
        self.out_shape = jax.ShapeDtypeStruct((n_chunks, rows, cols), BF16)
        self.vmem_bytes = 2 * rows * cols * (4 + 2)


def _cast_chunks(src_refs, dst_refs):
    for src_ref, dst_ref in zip(src_refs, dst_refs):
        dst_ref[...] = src_ref[...].astype(BF16)


def _norm_proj_kernel(x_ref, g_ref, w_ref, *rest, n_cast):
    cast_src, (u_ref, *cast_dst), w_bf16_ref = rest[:n_cast], rest[n_cast:-1], rest[-1]

    @pl.when(pl.program_id(0) == 0)
    def _():
        for r0 in range(0, w_ref.shape[0], WEIGHT_CAST_ROWS):
            w_bf16_ref[r0:r0 + WEIGHT_CAST_ROWS, :] = w_ref[r0:r0 + WEIGHT_CAST_ROWS, :].astype(BF16)

    h = _rmsnorm(x_ref[...], g_ref[...])
    u_ref[...] = _dot(h.astype(BF16), w_bf16_ref[...]).astype(BF16)
    _cast_chunks(cast_src, cast_dst)


def _norm_proj(x2, g_mix, w_in, cast_weights):
    n, d = x2.shape
    tm = TOKEN_TILE
    casts = [_CastAlong(w, n // tm, lambda i: i) for w in cast_weights]
    vmem = (2 * tm * d * 4 + d * d * (4 + 2) + 2 * tm * d * 2 + 3 * tm * d * 4
            + sum(c.vmem_bytes for c in casts))
    u, *cast_out = pl.pallas_call(
        functools.partial(_norm_proj_kernel, n_cast=len(casts)),
        grid=(n // tm,),
        in_specs=[
            pl.BlockSpec((tm, d), lambda i: (i, 0)),
            pl.BlockSpec((1, d), lambda i: (0, 0)),
            pl.BlockSpec((d, d), lambda i: (0, 0), pipeline_mode=pl.Buffered(1)),
        ] + [c.spec for c in casts],
        out_specs=[pl.BlockSpec((tm, d), lambda i: (i, 0))] + [c.spec for c in casts],
        out_shape=[jax.ShapeDtypeStruct((n, d), BF16)] + [c.out_shape for c in casts],
        scratch_shapes=[pltpu.VMEM((d, d), BF16)],
        compiler_params=pltpu.CompilerParams(
            dimension_semantics=("arbitrary",), vmem_limit_bytes=vmem + 4 * MIB),
        name="norm_proj",
    )(x2, g_mix.reshape(1, d), w_in, *[c.src for c in casts])
    return u, [o.reshape(c.shape) for o, c in zip(cast_out, casts)]


def _fold_kernel(blk_ref, mirror_ref, after_ref, rev_ref, cw_ref, sw_ref, even_ref, odd_ref):
    t = SEQ_TILE
    halo = POOL_HALO
    hd = FOURIER_HEAD_DIM
    for i in range(FOLD_TILES):
        top = (FOLD_TILES - 1 - i) * t
        if i == 0:
            after = jnp.where(pl.program_id(1) > 0, after_ref[...], jnp.zeros_like(after_ref))
        else:
            after = mirror_ref[top + t:top + t + halo, :]
        window = jnp.concatenate([mirror_ref[top:top + t, :], after], axis=0)
        mirrored = _dot(rev_ref[...], window)
        blk = blk_ref[i * t:(i + 1) * t, :].astype(F32)
        even_ref[i * t:(i + 1) * t, :] = (blk + mirrored).astype(BF16)
        odd_ref[i * t:(i + 1) * t, :] = (blk - mirrored).astype(BF16)
    for h in range(N_FOURIER_HEADS):
        cols = slice(h * hd, (h + 1) * hd)
        even_ref[:, cols] = _dot(even_ref[:, cols], cw_ref[h]).astype(BF16)
        odd_ref[:, cols] = _dot(odd_ref[:, cols], sw_ref[h]).astype(BF16)


def _fold_sequence(u3, rev, cw, sw):
    b, s, _ = u3.shape
    halo = POOL_HALO
    rows = FOLD_TILES * SEQ_TILE
    n_blocks = s // rows
    block = lambda block_of: pl.BlockSpec((None, rows, FOURIER_WIDTH), lambda bi, j: (bi, block_of(j), 1))
    after = pl.BlockSpec((None, halo, FOURIER_WIDTH),
                         lambda bi, j: (bi, jnp.minimum((n_blocks - j) * (rows // halo), s // halo - 1), 1))
    out = pl.BlockSpec((None, rows, FOURIER_WIDTH), lambda bi, j: (bi, j, 0))
    return pl.pallas_call(
        _fold_kernel,
        grid=(b, n_blocks // 2),
        in_specs=[block(lambda j: j), block(lambda j: n_blocks - 1 - j), after,
                  pl.BlockSpec(rev.shape, lambda bi, j: (0, 0)),
                  pl.BlockSpec(cw.shape, lambda bi, j: (0, 0, 0)), pl.BlockSpec(sw.shape, lambda bi, j: (0, 0, 0))],
        out_specs=[out, out],
        out_shape=[jax.ShapeDtypeStruct((b, s // 2, FOURIER_WIDTH), BF16)] * 2,
        compiler_params=pltpu.CompilerParams(
            dimension_semantics=("arbitrary", "arbitrary"),
            vmem_limit_bytes=2 * 4 * rows * FOURIER_WIDTH * 2 + 8 * MIB),
        name="fold_sequence",
    )(u3, u3, u3, rev, cw, sw)


def _mix_kernel(even_ref, odd_ref, umid_ref, up_lo_ref, prev_lo_ref, next_lo_ref, up_hi_ref, prev_hi_ref, next_hi_ref,
                c0_ref, s0_ref, cph_ref, sph_ref, band_lo_ref, band_hi_ref, rev_ref,
                cw_ref, cast_src_ref, lo_ref, hi_ref, cast_dst_ref, lhs_ref):
    t = SEQ_TILE
    tp = DFT_ROWS
    gd = POOL_GROUP_DIM
    hd = FOURIER_HEAD_DIM

    m = pl.program_id(1)

    @pl.when(pl.program_id(0) == 0)
    def _():
        cph = cph_ref[0]
        sph = sph_ref[0]
        for r0 in range(0, tp, TWIDDLE_ROWS):
            c0 = c0_ref[r0:r0 + TWIDDLE_ROWS, :]
            s0 = s0_ref[r0:r0 + TWIDDLE_ROWS, :]
            lhs_ref[m, r0:r0 + TWIDDLE_ROWS, :] = (cph * c0 - sph * s0).astype(BF16)
            lhs_ref[m, tp + r0:tp + r0 + TWIDDLE_ROWS, :] = (sph * c0 + cph * s0).astype(BF16)

    _cast_chunks([cast_src_ref], [cast_dst_ref])
    row = lax.broadcasted_iota(jnp.int32, (tp, 1), 0)
    sign = jnp.where(row % 2 == 0, 1.0, -1.0)
    mid = jnp.concatenate([_dot(umid_ref[:, h * hd:(h + 1) * hd], cw_ref[h]) for h in range(N_FOURIER_HEADS)],
                          axis=1)
    pc = _dot(lhs_ref[m, :tp, :], even_ref[...]) + sign * mid[0:1, :]
    qs = _dot(lhs_ref[m, tp:, :], odd_ref[...])
    lo_ref[:, POOL_WIDTH:] = (pc + qs)[:t].astype(BF16)
    hi_ref[:, POOL_WIDTH:] = _dot(rev_ref[...], (pc - qs).astype(BF16)).astype(BF16)

    def pool(up_ref, prev_ref, next_ref, band_ref, out_ref):
        win = jnp.concatenate([prev_ref[...], up_ref[...], next_ref[...]], axis=0)
        for g in range(N_POOL_GROUPS):
            cols = slice(g * gd, (g + 1) * gd)
            out_ref[:, cols] = _dot(band_ref[0, g], win[:, cols]).astype(BF16)

    pool(up_lo_ref, prev_lo_ref, next_lo_ref, band_lo_ref, lo_ref)
    pool(up_hi_ref, prev_hi_ref, next_hi_ref, band_hi_ref, hi_ref)


def _mix(u3, cw, sw, cast_weight):
    b, s, d = u3.shape
    t = SEQ_TILE
    tp = DFT_ROWS
    halo = POOL_HALO
    n_tiles = s // t
    n_steps = n_tiles // 2
    halo_blocks_per_tile = t // halo
    last_halo_block = s // halo - 1

    sh = s // 2
    c0, s0 = _twiddle(np.arange(tp), np.arange(sh), s)
    cph, sph = _twiddle(np.arange(0, sh, t), np.arange(sh), s)
    band = jnp.asarray(_pool_band(s, t, halo)).astype(BF16)
    rev = np.zeros((t, tp), np.float32)
    rev[np.arange(t), t - np.arange(t)] = 1.0
    rev = jnp.asarray(rev).astype(BF16)
    even, odd = _fold_sequence(u3, rev, cw, sw)

    hi_tile = lambda m: n_tiles - 1 - m
    tile_spec = lambda tile_of: pl.BlockSpec((None, t, POOL_WIDTH), lambda bi, m: (bi, tile_of(m), 0))
    prev_spec = lambda tile_of: pl.BlockSpec(
        (None, halo, POOL_WIDTH),
        lambda bi, m: (bi, jnp.maximum(tile_of(m) * halo_blocks_per_tile - 1, 0), 0))
    next_spec = lambda tile_of: pl.BlockSpec(
        (None, halo, POOL_WIDTH),
        lambda bi, m: (bi, jnp.minimum((tile_of(m) + 1) * halo_blocks_per_tile, last_halo_block), 0))
    band_shape = (1, N_POOL_GROUPS, t, t + 2 * halo)
    const2 = lambda shape: pl.BlockSpec(shape, lambda bi, m: (0, 0))
    const3 = lambda shape: pl.BlockSpec(shape, lambda bi, m: (0, 0, 0))
    lo_tile = lambda m: m

    in_specs = [
        pl.BlockSpec((None, sh, FOURIER_WIDTH), lambda bi, m: (bi, 0, 0)),
        pl.BlockSpec((None, sh, FOURIER_WIDTH), lambda bi, m: (bi, 0, 0)),
        pl.BlockSpec((None, halo, FOURIER_WIDTH), lambda bi, m: (bi, sh // halo, 1)),
        tile_spec(lo_tile), prev_spec(lo_tile), next_spec(lo_tile),
        tile_spec(hi_tile), prev_spec(hi_tile), next_spec(hi_tile),
        pl.BlockSpec((tp, sh), lambda bi, m: (0, 0), pipeline_mode=pl.Buffered(1)),
        pl.BlockSpec((tp, sh), lambda bi, m: (0, 0), pipeline_mode=pl.Buffered(1)),
        pl.BlockSpec((1, 1, sh), lambda bi, m: (m, 0, 0)),
        pl.BlockSpec((1, 1, sh), lambda bi, m: (m, 0, 0)),
        pl.BlockSpec(band_shape, lambda bi, m: (jnp.where(m == 0, 0, 1), 0, 0, 0)),
        pl.BlockSpec(band_shape, lambda bi, m: (jnp.where(m == 0, 2, 1), 0, 0, 0)),
        const2((t, tp)),
        const3((N_FOURIER_HEADS, FOURIER_HEAD_DIM, FOURIER_HEAD_DIM)),
    ]
    half_out = pl.BlockSpec((None, t, d), lambda bi, m: (bi, m, 0))
    hi_out = pl.BlockSpec((None, t, d), lambda bi, m: (bi, n_steps - 1 - m, 0))
    cast = _CastAlong(cast_weight, n_steps * b, lambda bi, m: m * b + bi)
    vmem = (2 * 2 * sh * FOURIER_WIDTH * 2
            + 2 * tp * sh * 4
            + n_steps * 2 * tp * sh * 2
            + 2 * 2 * t * d * 2 * 2
            + 4 * 2 * tp * FOURIER_WIDTH * 4
            + cast.vmem_bytes)
    lo, hi, cast_out = pl.pallas_call(
        _mix_kernel,
        grid=(b, n_steps),
        in_specs=in_specs + [cast.spec],
        out_specs=[half_out, hi_out, cast.spec],
        out_shape=[jax.ShapeDtypeStruct((b, s // 2, d), BF16)] * 2 + [cast.out_shape],
        scratch_shapes=[pltpu.VMEM((n_steps, 2 * tp, sh), BF16)],
        compiler_params=pltpu.CompilerParams(
            dimension_semantics=("arbitrary", "arbitrary"), vmem_limit_bytes=vmem + 4 * MIB),
        name="seq_mix",
    )(even, odd, u3, u3, u3, u3, u3, u3, u3, jnp.asarray(c0), jnp.asarray(s0),
      jnp.asarray(cph).reshape(n_steps, 1, sh), jnp.asarray(sph).reshape(n_steps, 1, sh),
      band, band, rev,
      cw, cast.src)
    return lo, hi, cast_out.reshape(cast.shape)


def _out_proj_route_kernel(x_ref, mixed_lo_ref, mixed_hi_ref, wtop_ref, wbot_ref, g_ref, wr_ref, br_ref,
                           tri_ref, pick_ref, cast_src_ref, rows_ref, route_ref, counts_ref, cast_dst_ref,
                           carry_ref, x1_ref, *, tiles_per_seq, n_tiles):
    step = pl.program_id(0)

    @pl.when(step == 0)
    def _():
        carry_ref[...] = jnp.zeros_like(carry_ref)
        x1_ref[...] = jnp.zeros_like(x1_ref)

    _cast_chunks([cast_src_ref], [cast_dst_ref])

    x1 = x1_ref[...]
    rows_ref[:, :D_MODEL] = x1
    h2 = _rmsnorm(x1, g_ref[...])
    logits = _dot(h2.astype(BF16), wr_ref[...]) + br_ref[...]

    tile = jnp.minimum(step, n_tiles - 1)
    in_lo_half = (tile % tiles_per_seq) < tiles_per_seq // 2
    mixed = jnp.where(in_lo_half, mixed_lo_ref[...], mixed_hi_ref[...])
    x1_ref[...] = (x_ref[...] + _dot(mixed[:, :POOL_WIDTH], wtop_ref[...])
                   + _dot(mixed[:, POOL_WIDTH:], wbot_ref[...]))

    lane = lax.broadcasted_iota(jnp.int32, logits.shape, 1)
    neg = jnp.float32(-jnp.inf)
    big = jnp.int32(ROUTER_LANES)

    is_group = lane < N_EXPERT_GROUPS
    gl = jnp.where(is_group, logits, neg)
    gmax = jnp.max(gl, axis=-1, keepdims=True)
    gidx = jnp.min(jnp.where(gl == gmax, lane, big), axis=-1, keepdims=True)
    p_g = 1.0 / jnp.sum(jnp.exp(gl - gmax), axis=-1, keepdims=True)

    e_lane = lane - N_EXPERT_GROUPS
    in_group = (e_lane >= gidx * EXPERTS_PER_GROUP) & (e_lane < (gidx + 1) * EXPERTS_PER_GROUP)
    el = jnp.where(in_group, logits, neg)
    v1 = jnp.max(el, axis=-1, keepdims=True)
    i1 = jnp.min(jnp.where(el == v1, lane, big), axis=-1, keepdims=True)
    el2 = jnp.where(lane == i1, neg, el)
    v2 = jnp.max(el2, axis=-1, keepdims=True)
    i2 = jnp.min(jnp.where(el2 == v2, lane, big), axis=-1, keepdims=True)
    r = jnp.exp(v2 - v1)
    w1 = p_g / (1.0 + r)
    w2 = p_g * r / (1.0 + r)

    first_is_low = i1 < i2
    first_lane = N_EXPERT_GROUPS + gidx * EXPERTS_PER_GROUP
    la = jnp.where(first_is_low, i1, i2) - first_lane
    lb = jnp.where(first_is_low, i2, i1) - first_lane
    w_low = jnp.where(first_is_low, w1, w2)
    w_high = jnp.where(first_is_low, w2, w1)
    assert (PAIR_SLOT_A, PAIR_SLOT_B) == ((0, 0, 0, 1, 1, 3), (1, 2, 3, 3, 2, 2))
    pair = jnp.where(la == 0, lb - 1, jnp.where(la == 1, 6 - lb, 5))
    slot_a_is_high = la == 2
    w_a = jnp.where(slot_a_is_high, w_high, w_low)
    w_b = jnp.where(slot_a_is_high, w_low, w_high)
    cls = gidx * PAIRS_PER_GROUP + pair
    rows_ref[:, D_MODEL:] = jnp.where(lane == 0, w_a, jnp.where(lane == 1, w_b, 0.0))

    onehot = jnp.where(lane == cls, 1.0, 0.0)
    before = _dot(tri_ref[...], onehot.astype(BF16)) + carry_ref[...]
    rank = jnp.sum(jnp.where(lane == cls, before, 0.0), axis=-1, keepdims=True)
    carry_ref[...] += jnp.sum(onehot, axis=0, keepdims=True) * jnp.where(step > 0, 1.0, 0.0)
    counts_ref[...] = carry_ref[...]

    rank_hi = jnp.floor(rank * (1.0 / RANK_RADIX))
    rank_lo = rank - rank_hi * RANK_RADIX
    digits = jnp.where(lane == 0, cls.astype(F32),
                       jnp.where(lane == 1, rank_hi, jnp.where(lane == 2, rank_lo, 0.0)))
    route_ref[...] = lax.dot_general(pick_ref[...], digits.astype(BF16), (((1,), (1,)), ((), ())),
                                     preferred_element_type=F32)


def _out_proj_route(x2, mixed_lo, mixed_hi, seq_len, w_top_bf16, w_out_bf16, g_ffn, wr, br, cast_weight):
    n, d = x2.shape
    tm = TOKEN_TILE
    n_tiles = n // tm
    tiles_per_seq = seq_len // tm
    half_tiles = tiles_per_seq // 2
    const = lambda shape: pl.BlockSpec(shape, lambda i: (0, 0))
    in_tile = lambda i: jnp.minimum(i, n_tiles - 1)
    out_tile = lambda i: jnp.maximum(i - 1, 0)
    lo_spec = pl.BlockSpec((tm, d), lambda i: (
        (in_tile(i) // tiles_per_seq) * half_tiles + jnp.minimum(in_tile(i) % tiles_per_seq, half_tiles - 1), 0))
    hi_spec = pl.BlockSpec((tm, d), lambda i: (
        (in_tile(i) // tiles_per_seq) * half_tiles + jnp.maximum(in_tile(i) % tiles_per_seq - half_tiles, 0), 0))
    tri = np.tril(np.ones((tm, tm), np.float32), -1)
    pick = np.eye(SUBLANES, ROUTER_LANES, dtype=np.float32)
    vmem = (2 * tm * d * 4 + 4 * tm * d * 2 + 2 * tm * ROW_WORDS * 4 + d * d * 2 + tm * d * 4
            + 2 * d * ROUTER_LANES * 2 + 2 * tm * tm * 2 + 3 * tm * d * 4)
    cast = _CastAlong(cast_weight, n_tiles, in_tile)
    vmem += cast.vmem_bytes
    rows, route, counts, cast_out = pl.pallas_call(
        functools.partial(_out_proj_route_kernel, tiles_per_seq=tiles_per_seq, n_tiles=n_tiles),
        grid=(n_tiles + 1,),
        in_specs=[pl.BlockSpec((tm, d), lambda i: (in_tile(i), 0)), lo_spec, hi_spec,
                  pl.BlockSpec((POOL_WIDTH, d), lambda i: (0, 0), pipeline_mode=pl.Buffered(1)),
                  pl.BlockSpec((FOURIER_WIDTH, d), lambda i: (1, 0), pipeline_mode=pl.Buffered(1)),
                  const((1, d)), const((d, ROUTER_LANES)), const((1, ROUTER_LANES)),
                  const((tm, tm)), const((SUBLANES, ROUTER_LANES)), cast.spec],
        out_specs=[pl.BlockSpec((tm, ROW_WORDS), lambda i: (out_tile(i), 0)),
                   pl.BlockSpec((SUBLANES, tm), lambda i: (0, out_tile(i))),
                   const((1, ROUTER_LANES)), cast.spec],
        out_shape=[jax.ShapeDtypeStruct((n, ROW_WORDS), F32),
                   jax.ShapeDtypeStruct((SUBLANES, n), F32),
                   jax.ShapeDtypeStruct((1, ROUTER_LANES), F32), cast.out_shape],
        scratch_shapes=[pltpu.VMEM((1, ROUTER_LANES), F32), pltpu.VMEM((tm, d), F32)],
        compiler_params=pltpu.CompilerParams(
            dimension_semantics=("arbitrary",), vmem_limit_bytes=vmem + 4 * MIB),
        name="out_proj_route",
    )(x2, mixed_lo, mixed_hi, w_top_bf16, w_out_bf16, g_ffn.reshape(1, d), wr, br,
      jnp.asarray(tri).astype(BF16), jnp.asarray(pick).astype(BF16), cast.src)
    return rows, route, counts, cast_out.reshape(cast.shape)


def _row_copy(src_ref, src_row, dst_ref, dst_row, sem):
    return pltpu.make_async_copy(src_ref.at[pl.ds(src_row, 1), :], dst_ref.at[pl.ds(dst_row, 1), :], sem)


def _plan_kernel(cnt_ref, route_ref, tok_ref, start_ref, ea_ref, eb_ref, lo_ref, hi_ref, nused_ref,
                 cstart_ref, pos_vmem_ref, pos_smem_ref, sem):
    tm = EXPERT_TILE
    shift = tm.bit_length() - 1
    n = tok_ref.shape[0]
    n_items = start_ref.shape[0]

    run = jnp.int32(0)
    inside = jnp.bool_(True)
    for c in range(N_CLASSES):
        cnt = cnt_ref[c]
        cstart_ref[c] = run
        inside = inside & ((cnt == 0) | (cnt >= tm))
        run = run + cnt

    cls_row = route_ref[0:1, :]
    slot_row = route_ref[1:2, :] * RANK_RADIX + route_ref[2:3, :]
    for c in range(N_CLASSES):
        slot_row = slot_row + jnp.where(cls_row == c, cstart_ref[c].astype(F32), 0.0)
    pos_vmem_ref[...] = slot_row.astype(jnp.int32)
    to_smem = pltpu.make_async_copy(pos_vmem_ref, pos_smem_ref, sem)
    to_smem.start()

    k = jnp.int32(0)
    for c in range(N_CLASSES):
        cnt = cnt_ref[c]
        cs = cstart_ref[c]
        ce = cs + cnt
        group, pair = divmod(c, PAIRS_PER_GROUP)
        ea = group * EXPERTS_PER_GROUP + PAIR_SLOT_A[pair]
        eb = group * EXPERTS_PER_GROUP + PAIR_SLOT_B[pair]
        first_window = cs >> shift
        n_inside = (cnt + tm - 1) >> shift
        n_aligned = jnp.where(cnt > 0, ((ce - 1) >> shift) - first_window + 1, 0)
        full_windows = cnt >> shift

        def emit(i, carry, k=k, cs=cs, ce=ce, ea=ea, eb=eb, first_window=first_window,
                 full_windows=full_windows):
            start_inside = jnp.where(i < full_windows, cs + i * tm, ce - tm)
            start_aligned = (first_window + i) * tm
            start_ref[k + i] = jnp.where(inside, start_inside, start_aligned)
            ea_ref[k + i] = ea
            eb_ref[k + i] = eb
            lo_ref[k + i] = jnp.where(inside, 0, jnp.clip(cs - start_aligned, 0, tm))
            hi_ref[k + i] = jnp.where(inside, tm, jnp.clip(ce - start_aligned, 0, tm))
            return carry

        n_class_items = jnp.where(inside, n_inside, n_aligned)
        lax.fori_loop(0, n_class_items, emit, 0)
        k = k + n_class_items
    nused_ref[0] = k

    def repeat_last(i, carry):
        for ref in (start_ref, ea_ref, eb_ref, lo_ref, hi_ref):
            ref[i] = ref[k - 1]
        return carry

    lax.fori_loop(k, n_items, repeat_last, 0)

    to_smem.wait()

    def invert(t, carry):
        tok_ref[pos_smem_ref[0, t]] = t
        return carry

    lax.fori_loop(0, n, invert, 0, unroll=INVERT_UNROLL)


def _plan_routing(route, counts, n):
    n_items = n // EXPERT_TILE + N_CLASSES
    smem = lambda size: jax.ShapeDtypeStruct((size,), jnp.int32)
    smem_spec = pl.BlockSpec(memory_space=pltpu.SMEM)
    return pl.pallas_call(
        _plan_kernel,
        in_specs=[smem_spec, pl.BlockSpec(memory_space=pltpu.VMEM)],
        out_specs=[smem_spec] * 7,
        out_shape=[smem(n)] + [smem(n_items)] * 5 + [smem(1)],
        scratch_shapes=[pltpu.SMEM((N_CLASSES,), jnp.int32), pltpu.VMEM((1, n), jnp.int32),
                        pltpu.SMEM((1, n), jnp.int32), pltpu.SemaphoreType.DMA(())],
        name="routing_plan",
    )(counts[0, :N_CLASSES].astype(jnp.int32), route)


def _expert_pair_kernel(start_ref, ea_ref, eb_ref, lo_ref, hi_ref, nused_ref, tok_ref,
                        rows_hbm_ref, gffn_ref, gfin_ref,
                        wga_ref, wua_ref, wda_ref, wgb_ref, wub_ref, wdb_ref, out_hbm_ref,
                        buf_ref, acc_ref, gather_sem, scatter_sem):
    del ea_ref, eb_ref
    tm = EXPERT_TILE
    j = pl.program_id(0)
    last = nused_ref[0] - 1
    slot = j % 2
    other = 1 - slot

    def start_gather(item, dst_slot, rows=range(tm)):
        base = start_ref[item]
        for r in rows:
            _row_copy(rows_hbm_ref, tok_ref[base + r], buf_ref.at[dst_slot], r, gather_sem.at[dst_slot]).start()

    def wait_gather(dst_slot):
        pltpu.make_async_copy(rows_hbm_ref.at[pl.ds(0, tm), :], buf_ref.at[dst_slot],
                              gather_sem.at[dst_slot]).wait()

    def start_scatter(item, src_slot, rows=range(tm)):
        base = start_ref[item]
        for r in rows:
            _row_copy(acc_ref.at[src_slot], r, out_hbm_ref, tok_ref[base + r], scatter_sem).start()

    def wait_scatter():
        pltpu.make_async_copy(acc_ref.at[0], out_hbm_ref.at[pl.ds(0, tm), :], scatter_sem).wait()

    gslot = j % GATHER_SLOTS

    @pl.when(j == 0)
    def _():
        acc_ref[...] = jnp.zeros_like(acc_ref)
        start_gather(0, 0)
        start_gather(jnp.minimum(1, last), 1)
        start_scatter(0, 1)

    @pl.when(j <= last)
    def _():
        wait_scatter()
        wait_gather(gslot)
        prev_item = jnp.maximum(j - 1, 0)
        ahead_item = jnp.minimum(j + 2, last)

        def issue_copies(part):
            half_parts = DMA_PARTS // 2
            rows = range((part % half_parts) * tm // half_parts, (part % half_parts + 1) * tm // half_parts)
            if part < half_parts:
                start_scatter(prev_item, other, rows)
            else:
                start_gather(ahead_item, (j + 2) % GATHER_SLOTS, rows)

        lo = lo_ref[j]
        hi = hi_ref[j]
        x1 = buf_ref[gslot, :, :D_MODEL]
        h = _rmsnorm(x1, gffn_ref[...]).astype(BF16)

        part = 0
        ff_chunk = EXPERT_FF * 4 // DMA_PARTS
        for slot_lane, (wg_ref, wu_ref, wd_ref) in enumerate(((wga_ref, wua_ref, wda_ref),
                                                              (wgb_ref, wub_ref, wdb_ref))):
            for c0 in range(0, EXPERT_FF, ff_chunk):
                issue_copies(part)
                wts = buf_ref[gslot, :, D_MODEL:]
                lane = lax.broadcasted_iota(jnp.int32, wts.shape, 1)
                w = jnp.sum(jnp.where(lane == slot_lane, wts, 0.0), axis=-1, keepdims=True)
                a = _dot(h, wg_ref[0, :, c0:c0 + ff_chunk])
                v = _dot(h, wu_ref[0, :, c0:c0 + ff_chunk])
                act = (a * (1.0 / (1.0 + jnp.exp(-a))) * v * w).astype(BF16)
                for half, d0 in enumerate(range(0, D_MODEL, D_MODEL // 2)):
                    cols = slice(d0, d0 + D_MODEL // 2)
                    if half == 1:
                        issue_copies(part + 1)
                    y_part = _dot(act, wd_ref[0, c0:c0 + ff_chunk, cols])
                    if part == 0:
                        acc_ref[slot, :, cols] = y_part
                    else:
                        acc_ref[slot, :, cols] += y_part
                part += 2
        res = _rmsnorm(buf_ref[gslot, :, :D_MODEL] + acc_ref[slot], gfin_ref[...])

        row = lax.broadcasted_iota(jnp.int32, (tm, 1), 0)
        mine = (row >= lo) & (row < hi)
        acc_ref[slot] = jnp.where(mine, res, jnp.where(lo > 0, acc_ref[other], 0.0))

    @pl.when(j == last)
    def _():
        wait_scatter()
        start_scatter(j, slot)
        wait_scatter()
        wait_gather((j + 1) % GATHER_SLOTS)
        wait_gather((j + 2) % GATHER_SLOTS)


def _expert_pairs(item_start, item_ea, item_eb, item_lo, item_hi, n_used, slot_token, rows,
                  g_ffn, g_final, wg, wu, wd):
    n, w = rows.shape
    d, f = D_MODEL, EXPERT_FF
    tm = EXPERT_TILE
    gate_a = pl.BlockSpec((1, d, f), lambda j, st, ea, eb, lo, hi, nu, tok: (ea[j], 0, 0))
    gate_b = pl.BlockSpec((1, d, f), lambda j, st, ea, eb, lo, hi, nu, tok: (eb[j], 0, 0))
    down_a = pl.BlockSpec((1, f, d), lambda j, st, ea, eb, lo, hi, nu, tok: (ea[j], 0, 0))
    down_b = pl.BlockSpec((1, f, d), lambda j, st, ea, eb, lo, hi, nu, tok: (eb[j], 0, 0))
    gain = pl.BlockSpec((1, d), lambda j, st, ea, eb, lo, hi, nu, tok: (0, 0))
    grid_spec = pltpu.PrefetchScalarGridSpec(
        num_scalar_prefetch=7,
        grid=(item_start.shape[0],),
        in_specs=[pl.BlockSpec(memory_space=pl.ANY), gain, gain,
                  gate_a, gate_a, down_a, gate_b, gate_b, down_b],
        out_specs=pl.BlockSpec(memory_space=pl.ANY),
        scratch_shapes=[pltpu.VMEM((GATHER_SLOTS, tm, w), F32), pltpu.VMEM((2, tm, d), F32),
                        pltpu.SemaphoreType.DMA((GATHER_SLOTS,)), pltpu.SemaphoreType.DMA(())],
    )
    vmem = 2 * 6 * d * f * 2 + GATHER_SLOTS * tm * w * 4 + 2 * tm * d * 4 + 8 * tm * d * 4
    return pl.pallas_call(
        _expert_pair_kernel,
        grid_spec=grid_spec,
        out_shape=jax.ShapeDtypeStruct((n, d), F32),
        compiler_params=pltpu.CompilerParams(
            dimension_semantics=("arbitrary",), vmem_limit_bytes=vmem + 4 * MIB),
        name="expert_pairs",
    )(item_start, item_ea, item_eb, item_lo, item_hi, n_used, slot_token, rows, g_ffn.reshape(1, d),
      g_final.reshape(1, d), wg, wu, wd, wg, wu, wd)


def kernel(x, g_mix, w_in, w_pool, pool_scale, w_fourier, w_out, g_ffn, w_group_router,
           b_group_router, w_expert_router, b_expert_router, w_gate, w_up, w_down, g_final):
    b, s, d = x.shape
    assert d == D_MODEL and s % (2 * SEQ_TILE) == 0 and s % (2 * TOKEN_TILE) == 0
    assert (b * s) % max(TOKEN_TILE, EXPERT_TILE) == 0
    n = b * s
    x2 = x.reshape(n, d)

    cw, sw, w_top_bf16 = _mixer_weights(w_fourier, w_pool, pool_scale, w_out, s)
    u, (w_out_bf16, w_gate_bf16) = _norm_proj(x2, g_mix, w_in, [w_out, w_gate])
    mixed_lo, mixed_hi, w_up_bf16 = _mix(u.reshape(b, s, d), cw, sw, w_up)

    wr = jnp.concatenate([w_group_router, w_expert_router], axis=1)
    wr = jnp.pad(wr, ((0, 0), (0, ROUTER_LANES - wr.shape[1]))).astype(BF16)
    br = jnp.concatenate([b_group_router, b_expert_router])
    br = jnp.pad(br, (0, ROUTER_LANES - br.shape[0])).reshape(1, ROUTER_LANES)

    rows, route, counts, w_down_bf16 = _out_proj_route(
        x2, mixed_lo.reshape(n // 2, d), mixed_hi.reshape(n // 2, d), s, w_top_bf16, w_out_bf16, g_ffn, wr, br,
        w_down)
    slot_token, item_start, item_ea, item_eb, item_lo, item_hi, n_used = _plan_routing(route, counts, n)
    out = _expert_pairs(item_start, item_ea, item_eb, item_lo, item_hi, n_used, slot_token, rows,
                        g_ffn, g_final, w_gate_bf16, w_up_bf16, w_down_bf16)
    return out.reshape(b, s, d)
```

```python
import functools

import numpy as np
import jax
import jax.numpy as jnp
from jax import lax
from jax.experimental import pallas as pl
from jax.experimental.pallas import tpu as pltpu

D_MODEL = 2048
POOL_WINDOWS = (2, 4, 8, 16)
N_POOL_GROUPS = len(POOL_WINDOWS)
POOL_WIDTH = D_MODEL // 2
POOL_GROUP_DIM = POOL_WIDTH // N_POOL_GROUPS
FOURIER_WIDTH = D_MODEL - POOL_WIDTH
N_FOURIER_HEADS = 4
FOURIER_HEAD_DIM = FOURIER_WIDTH // N_FOURIER_HEADS
N_EXPERT_GROUPS = 4
EXPERTS_PER_GROUP = 4
EXPERT_FF = D_MODEL // 4
RMS_EPS = 1e-6

LANES = 128
SUBLANES = 8
BF16_SUBLANES = 16
ROUTER_LANES = LANES
MIB = 1024 * 1024

TOKEN_TILE = 512
SEQ_TILE = 256
POOL_HALO = BF16_SUBLANES
DFT_ROWS = SEQ_TILE + BF16_SUBLANES
TWIDDLE_ROWS = 16
FOLD_TILES = 4

PAIR_SLOT_A = (0, 0, 0, 1, 1, 3)
PAIR_SLOT_B = (1, 2, 3, 3, 2, 2)
PAIRS_PER_GROUP = len(PAIR_SLOT_A)
N_CLASSES = N_EXPERT_GROUPS * PAIRS_PER_GROUP
ROW_WORDS = D_MODEL + LANES
RANK_RADIX = 128
EXPERT_TILE = 256
DMA_PARTS = 8
GATHER_SLOTS = 3
INVERT_UNROLL = 32
WEIGHT_CAST_ROWS = 256

BF16 = jnp.bfloat16
F32 = jnp.float32


def _rmsnorm(x, g):
    ms = jnp.mean(x * x, axis=-1, keepdims=True)
    return x * lax.rsqrt(ms + RMS_EPS) * g


def _dot(a, b):
    return jnp.dot(a, b, preferred_element_type=F32)


def _twiddle(rows, cols, period):
    m = (np.asarray(rows, np.int64)[:, None] * np.asarray(cols, np.int64)[None, :]) % period
    ang = (2.0 * np.pi / period) * m.astype(np.float64)
    return np.cos(ang).astype(np.float32), np.sin(ang).astype(np.float32)


def _pool_band(seq_len, tile, halo):
    n_tiles = seq_len // tile
    out = np.zeros((3, N_POOL_GROUPS, tile, tile + 2 * halo), np.float64)
    for v, m in enumerate((0, 1, n_tiles - 1)):
        t0 = m * tile
        for g, k in enumerate(POOL_WINDOWS):
            for r in range(tile):
                t = t0 + r
                lo = max(t - (k - 1) // 2, 0)
                hi = min(t + k // 2 + 1, seq_len)
                out[v, g, r, lo - t0 + halo:hi - t0 + halo] = 1.0 / (hi - lo)
                out[v, g, r, r + halo] -= 1.0
    return out.astype(np.float32)


def _mixer_weight_kernel(cd_ref, sd_ref, wf_ref, wp_ref, ps_ref, wo_ref, cw_ref, sw_ref, wtop_ref, *, scale):
    wf = wf_ref[0]
    cw = jnp.dot(cd_ref[...], wf, preferred_element_type=F32, precision=lax.Precision.HIGHEST)
    sw = jnp.dot(sd_ref[...], wf, preferred_element_type=F32, precision=lax.Precision.HIGHEST)
    cw_ref[0] = (cw * scale).astype(BF16)
    sw_ref[0] = (sw * (-scale)).astype(BF16)
    wtop_ref[...] = _dot((wp_ref[0] * ps_ref[0]).astype(BF16), wo_ref[...].astype(BF16)).astype(BF16)


def _mixer_weights(w_fourier, w_pool, pool_scale, w_out, seq_len):
    dh = FOURIER_HEAD_DIM
    assert N_FOURIER_HEADS == N_POOL_GROUPS and dh == POOL_GROUP_DIM
    d = w_out.shape[1]
    cd, sd = _twiddle(np.arange(dh), np.arange(dh), dh)
    scale = 1.0 / np.sqrt(float(seq_len * dh))
    mat = pl.BlockSpec((dh, dh), lambda h: (0, 0))
    per_head = pl.BlockSpec((1, dh, dh), lambda h: (h, 0, 0))
    out_rows = pl.BlockSpec((dh, d), lambda h: (h, 0))
    return pl.pallas_call(
        functools.partial(_mixer_weight_kernel, scale=scale),
        grid=(N_FOURIER_HEADS,),
        in_specs=[mat, mat, per_head, per_head, pl.BlockSpec((1, 1, dh), lambda h: (h, 0, 0)), out_rows],
        out_specs=[per_head, per_head, out_rows],
        out_shape=[jax.ShapeDtypeStruct((N_FOURIER_HEADS, dh, dh), BF16)] * 2
        + [jax.ShapeDtypeStruct((POOL_WIDTH, d), BF16)],
        name="mixer_weights",
    )(jnp.asarray(cd), jnp.asarray(sd), w_fourier, w_pool, pool_scale.reshape(N_POOL_GROUPS, 1, dh), w_out)


class _CastAlong:
    def __init__(self, w, n_chunks, chunk_of):
        cols = w.shape[-1]
        rows = w.size // cols // n_chunks
        self.shape = w.shape
        self.src = w.reshape(n_chunks, rows, cols)
        self.spec = pl.BlockSpec((1, rows, cols), lambda *idx: (chunk_of(*idx), 0, 0))
        self.out_shape = jax.ShapeDtypeStruct((n_chunks, rows, cols), BF16)
        self.vmem_bytes = 2 * rows * cols * (4 + 2)


def _cast_chunks(src_refs, dst_refs):
    for src_ref, dst_ref in zip(src_refs, dst_refs):
        dst_ref[...] = src_ref[...].astype(BF16)


def _norm_proj_kernel(x_ref, g_ref, w_ref, *rest, n_cast):
    cast_src, (u_ref, *cast_dst), w_bf16_ref = rest[:n_cast], rest[n_cast:-1], rest[-1]

    @pl.when(pl.program_id(0) == 0)
    def _():
        for r0 in range(0, w_ref.shape[0], WEIGHT_CAST_ROWS):
            w_bf16_ref[r0:r0 + WEIGHT_CAST_ROWS, :] = w_ref[r0:r0 + WEIGHT_CAST_ROWS, :].astype(BF16)

    h = _rmsnorm(x_ref[...], g_ref[...])
    u_ref[...] = _dot(h.astype(BF16), w_bf16_ref[...]).astype(BF16)
    _cast_chunks(cast_src, cast_dst)


def _norm_proj(x2, g_mix, w_in, cast_weights):
    n, d = x2.shape
    tm = TOKEN_TILE
    casts = [_CastAlong(w, n // tm, lambda i: i) for w in cast_weights]
    vmem = (2 * tm * d * 4 + d * d * (4 + 2) + 2 * tm * d * 2 + 3 * tm * d * 4
            + sum(c.vmem_bytes for c in casts))
    u, *cast_out = pl.pallas_call(
        functools.partial(_norm_proj_kernel, n_cast=len(casts)),
        grid=(n // tm,),
        in_specs=[
            pl.BlockSpec((tm, d), lambda i: (i, 0)),
            pl.BlockSpec((1, d), lambda i: (0, 0)),
            pl.BlockSpec((d, d), lambda i: (0, 0), pipeline_mode=pl.Buffered(1)),
        ] + [c.spec for c in casts],
        out_specs=[pl.BlockSpec((tm, d), lambda i: (i, 0))] + [c.spec for c in casts],
        out_shape=[jax.ShapeDtypeStruct((n, d), BF16)] + [c.out_shape for c in casts],
        scratch_shapes=[pltpu.VMEM((d, d), BF16)],
        compiler_params=pltpu.CompilerParams(
            dimension_semantics=("arbitrary",), vmem_limit_bytes=vmem + 4 * MIB),
        name="norm_proj",
    )(x2, g_mix.reshape(1, d), w_in, *[c.src for c in casts])
    return u, [o.reshape(c.shape) for o, c in zip(cast_out, casts)]


def _fold_kernel(blk_ref, mirror_ref, after_ref, rev_ref, cw_ref, sw_ref, even_ref, odd_ref):
    t = SEQ_TILE
    halo = POOL_HALO
    hd = FOURIER_HEAD_DIM
    for i in range(FOLD_TILES):
        top = (FOLD_TILES - 1 - i) * t
        if i == 0:
            after = jnp.where(pl.program_id(1) > 0, after_ref[...], jnp.zeros_like(after_ref))
        else:
            after = mirror_ref[top + t:top + t + halo, :]
        window = jnp.concatenate([mirror_ref[top:top + t, :], after], axis=0)
        mirrored = _dot(rev_ref[...], window)
        blk = blk_ref[i * t:(i + 1) * t, :].astype(F32)
        even_ref[i * t:(i + 1) * t, :] = (blk + mirrored).astype(BF16)
        odd_ref[i * t:(i + 1) * t, :] = (blk - mirrored).astype(BF16)
    for h in range(N_FOURIER_HEADS):
        cols = slice(h * hd, (h + 1) * hd)
        even_ref[:, cols] = _dot(even_ref[:, cols], cw_ref[h]).astype(BF16)
        odd_ref[:, cols] = _dot(odd_ref[:, cols], sw_ref[h]).astype(BF16)


def _fold_sequence(u3, rev, cw, sw):
    b, s, _ = u3.shape
    halo = POOL_HALO
    rows = FOLD_TILES * SEQ_TILE
    n_blocks = s // rows
    block = lambda block_of: pl.BlockSpec((None, rows, FOURIER_WIDTH), lambda bi, j: (bi, block_of(j), 1))
    after = pl.BlockSpec((None, halo, FOURIER_WIDTH),
                         lambda bi, j: (bi, jnp.minimum((n_blocks - j) * (rows // halo), s // halo - 1), 1))
    out = pl.BlockSpec((None, rows, FOURIER_WIDTH), lambda bi, j: (bi, j, 0))
    return pl.pallas_call(
        _fold_kernel,
        grid=(b, n_blocks // 2),
        in_specs=[block(lambda j: j), block(lambda j: n_blocks - 1 - j), after,
                  pl.BlockSpec(rev.shape, lambda bi, j: (0, 0)),
                  pl.BlockSpec(cw.shape, lambda bi, j: (0, 0, 0)), pl.BlockSpec(sw.shape, lambda bi, j: (0, 0, 0))],
        out_specs=[out, out],
        out_shape=[jax.ShapeDtypeStruct((b, s // 2, FOURIER_WIDTH), BF16)] * 2,
        compiler_params=pltpu.CompilerParams(
            dimension_semantics=("arbitrary", "arbitrary"),
            vmem_limit_bytes=2 * 4 * rows * FOURIER_WIDTH * 2 + 8 * MIB),
        name="fold_sequence",
    )(u3, u3, u3, rev, cw, sw)


def _mix_kernel(even_ref, odd_ref, umid_ref, up_lo_ref, prev_lo_ref, next_lo_ref, up_hi_ref, prev_hi_ref, next_hi_ref,
                c0_ref, s0_ref, cph_ref, sph_ref, band_lo_ref, band_hi_ref, rev_ref,
                cw_ref, cast_src_ref, lo_ref, hi_ref, cast_dst_ref, lhs_ref):
    t = SEQ_TILE
    tp = DFT_ROWS
    gd = POOL_GROUP_DIM
    hd = FOURIER_HEAD_DIM

    m = pl.program_id(1)

    @pl.when(pl.program_id(0) == 0)
    def _():
        cph = cph_ref[0]
        sph = sph_ref[0]
        for r0 in range(0, tp, TWIDDLE_ROWS):
            c0 = c0_ref[r0:r0 + TWIDDLE_ROWS, :]
            s0 = s0_ref[r0:r0 + TWIDDLE_ROWS, :]
            lhs_ref[m, r0:r0 + TWIDDLE_ROWS, :] = (cph * c0 - sph * s0).astype(BF16)
            lhs_ref[m, tp + r0:tp + r0 + TWIDDLE_ROWS, :] = (sph * c0 + cph * s0).astype(BF16)

    _cast_chunks([cast_src_ref], [cast_dst_ref])
    row = lax.broadcasted_iota(jnp.int32, (tp, 1), 0)
    sign = jnp.where(row % 2 == 0, 1.0, -1.0)
    mid = jnp.concatenate([_dot(umid_ref[:, h * hd:(h + 1) * hd], cw_ref[h]) for h in range(N_FOURIER_HEADS)],
                          axis=1)
    pc = _dot(lhs_ref[m, :tp, :], even_ref[...]) + sign * mid[0:1, :]
    qs = _dot(lhs_ref[m, tp:, :], odd_ref[...])
    lo_ref[:, POOL_WIDTH:] = (pc + qs)[:t].astype(BF16)
    hi_ref[:, POOL_WIDTH:] = _dot(rev_ref[...], (pc - qs).astype(BF16)).astype(BF16)

    def pool(up_ref, prev_ref, next_ref, band_ref, out_ref):
        win = jnp.concatenate([prev_ref[...], up_ref[...], next_ref[...]], axis=0)
        for g in range(N_POOL_GROUPS):
            cols = slice(g * gd, (g + 1) * gd)
            out_ref[:, cols] = _dot(band_ref[0, g], win[:, cols]).astype(BF16)

    pool(up_lo_ref, prev_lo_ref, next_lo_ref, band_lo_ref, lo_ref)
    pool(up_hi_ref, prev_hi_ref, next_hi_ref, band_hi_ref, hi_ref)


def _mix(u3, cw, sw, cast_weight):
    b, s, d = u3.shape
    t = SEQ_TILE
    tp = DFT_ROWS
    halo = POOL_HALO
    n_tiles = s // t
    n_steps = n_tiles // 2
    halo_blocks_per_tile = t // halo
    last_halo_block = s // halo - 1

    sh = s // 2
    c0, s0 = _twiddle(np.arange(tp), np.arange(sh), s)
    cph, sph = _twiddle(np.arange(0, sh, t), np.arange(sh), s)
    band = jnp.asarray(_pool_band(s, t, halo)).astype(BF16)
    rev = np.zeros((t, tp), np.float32)
    rev[np.arange(t), t - np.arange(t)] = 1.0
    rev = jnp.asarray(rev).astype(BF16)
    even, odd = _fold_sequence(u3, rev, cw, sw)

    hi_tile = lambda m: n_tiles - 1 - m
    tile_spec = lambda tile_of: pl.BlockSpec((None, t, POOL_WIDTH), lambda bi, m: (bi, tile_of(m), 0))
    prev_spec = lambda tile_of: pl.BlockSpec(
        (None, halo, POOL_WIDTH),
        lambda bi, m: (bi, jnp.maximum(tile_of(m) * halo_blocks_per_tile - 1, 0), 0))
    next_spec = lambda tile_of: pl.BlockSpec(
        (None, halo, POOL_WIDTH),
        lambda bi, m: (bi, jnp.minimum((tile_of(m) + 1) * halo_blocks_per_tile, last_halo_block), 0))
    band_shape = (1, N_POOL_GROUPS, t, t + 2 * halo)
    const2 = lambda shape: pl.BlockSpec(shape, lambda bi, m: (0, 0))
    const3 = lambda shape: pl.BlockSpec(shape, lambda bi, m: (0, 0, 0))
    lo_tile = lambda m: m

    in_specs = [
        pl.BlockSpec((None, sh, FOURIER_WIDTH), lambda bi, m: (bi, 0, 0)),
        pl.BlockSpec((None, sh, FOURIER_WIDTH), lambda bi, m: (bi, 0, 0)),
        pl.BlockSpec((None, halo, FOURIER_WIDTH), lambda bi, m: (bi, sh // halo, 1)),
        tile_spec(lo_tile), prev_spec(lo_tile), next_spec(lo_tile),
        tile_spec(hi_tile), prev_spec(hi_tile), next_spec(hi_tile),
        pl.BlockSpec((tp, sh), lambda bi, m: (0, 0), pipeline_mode=pl.Buffered(1)),
        pl.BlockSpec((tp, sh), lambda bi, m: (0, 0), pipeline_mode=pl.Buffered(1)),
        pl.BlockSpec((1, 1, sh), lambda bi, m: (m, 0, 0)),
        pl.BlockSpec((1, 1, sh), lambda bi, m: (m, 0, 0)),
        pl.BlockSpec(band_shape, lambda bi, m: (jnp.where(m == 0, 0, 1), 0, 0, 0)),
        pl.BlockSpec(band_shape, lambda bi, m: (jnp.where(m == 0, 2, 1), 0, 0, 0)),
        const2((t, tp)),
        const3((N_FOURIER_HEADS, FOURIER_HEAD_DIM, FOURIER_HEAD_DIM)),
    ]
    half_out = pl.BlockSpec((None, t, d), lambda bi, m: (bi, m, 0))
    hi_out = pl.BlockSpec((None, t, d), lambda bi, m: (bi, n_steps - 1 - m, 0))
    cast = _CastAlong(cast_weight, n_steps * b, lambda bi, m: m * b + bi)
    vmem = (2 * 2 * sh * FOURIER_WIDTH * 2
            + 2 * tp * sh * 4
            + n_steps * 2 * tp * sh * 2
            + 2 * 2 * t * d * 2 * 2
            + 4 * 2 * tp * FOURIER_WIDTH * 4
            + cast.vmem_bytes)
    lo, hi, cast_out = pl.pallas_call(
        _mix_kernel,
        grid=(b, n_steps),
        in_specs=in_specs + [cast.spec],
        out_specs=[half_out, hi_out, cast.spec],
        out_shape=[jax.ShapeDtypeStruct((b, s // 2, d), BF16)] * 2 + [cast.out_shape],
        scratch_shapes=[pltpu.VMEM((n_steps, 2 * tp, sh), BF16)],
        compiler_params=pltpu.CompilerParams(
            dimension_semantics=("arbitrary", "arbitrary"), vmem_limit_bytes=vmem + 4 * MIB),
        name="seq_mix",
    )(even, odd, u3, u3, u3, u3, u3, u3, u3, jnp.asarray(c0), jnp.asarray(s0),
      jnp.asarray(cph).reshape(n_steps, 1, sh), jnp.asarray(sph).reshape(n_steps, 1, sh),
      band, band, rev,
      cw, cast.src)
    return lo, hi, cast_out.reshape(cast.shape)


def _out_proj_route_kernel(x_ref, mixed_lo_ref, mixed_hi_ref, wtop_ref, wbot_ref, g_ref, wr_ref, br_ref,
                           tri_ref, pick_ref, cast_src_ref, rows_ref, route_ref, counts_ref, cast_dst_ref,
                           carry_ref, x1_ref, *, tiles_per_seq, n_tiles):
    step = pl.program_id(0)

    @pl.when(step == 0)
    def _():
        carry_ref[...] = jnp.zeros_like(carry_ref)
        x1_ref[...] = jnp.zeros_like(x1_ref)

    _cast_chunks([cast_src_ref], [cast_dst_ref])

    x1 = x1_ref[...]
    rows_ref[:, :D_MODEL] = x1
    h2 = _rmsnorm(x1, g_ref[...])
    logits = _dot(h2.astype(BF16), wr_ref[...]) + br_ref[...]

    tile = jnp.minimum(step, n_tiles - 1)
    in_lo_half = (tile % tiles_per_seq) < tiles_per_seq // 2
    mixed = jnp.where(in_lo_half, mixed_lo_ref[...], mixed_hi_ref[...])
    x1_ref[...] = (x_ref[...] + _dot(mixed[:, :POOL_WIDTH], wtop_ref[...])
                   + _dot(mixed[:, POOL_WIDTH:], wbot_ref[...]))

    lane = lax.broadcasted_iota(jnp.int32, logits.shape, 1)
    neg = jnp.float32(-jnp.inf)
    big = jnp.int32(ROUTER_LANES)

    is_group = lane < N_EXPERT_GROUPS
    gl = jnp.where(is_group, logits, neg)
    gmax = jnp.max(gl, axis=-1, keepdims=True)
    gidx = jnp.min(jnp.where(gl == gmax, lane, big), axis=-1, keepdims=True)
    p_g = 1.0 / jnp.sum(jnp.exp(gl - gmax), axis=-1, keepdims=True)

    e_lane = lane - N_EXPERT_GROUPS
    in_group = (e_lane >= gidx * EXPERTS_PER_GROUP) & (e_lane < (gidx + 1) * EXPERTS_PER_GROUP)
    el = jnp.where(in_group, logits, neg)
    v1 = jnp.max(el, axis=-1, keepdims=True)
    i1 = jnp.min(jnp.where(el == v1, lane, big), axis=-1, keepdims=True)
    el2 = jnp.where(lane == i1, neg, el)
    v2 = jnp.max(el2, axis=-1, keepdims=True)
    i2 = jnp.min(jnp.where(el2 == v2, lane, big), axis=-1, keepdims=True)
    r = jnp.exp(v2 - v1)
    w1 = p_g / (1.0 + r)
    w2 = p_g * r / (1.0 + r)

    first_is_low = i1 < i2
    first_lane = N_EXPERT_GROUPS + gidx * EXPERTS_PER_GROUP
    la = jnp.where(first_is_low, i1, i2) - first_lane
    lb = jnp.where(first_is_low, i2, i1) - first_lane
    w_low = jnp.where(first_is_low, w1, w2)
    w_high = jnp.where(first_is_low, w2, w1)
    assert (PAIR_SLOT_A, PAIR_SLOT_B) == ((0, 0, 0, 1, 1, 3), (1, 2, 3, 3, 2, 2))
    pair = jnp.where(la == 0, lb - 1, jnp.where(la == 1, 6 - lb, 5))
    slot_a_is_high = la == 2
    w_a = jnp.where(slot_a_is_high, w_high, w_low)
    w_b = jnp.where(slot_a_is_high, w_low, w_high)
    cls = gidx * PAIRS_PER_GROUP + pair
    rows_ref[:, D_MODEL:] = jnp.where(lane == 0, w_a, jnp.where(lane == 1, w_b, 0.0))

    onehot = jnp.where(lane == cls, 1.0, 0.0)
    before = _dot(tri_ref[...], onehot.astype(BF16)) + carry_ref[...]
    rank = jnp.sum(jnp.where(lane == cls, before, 0.0), axis=-1, keepdims=True)
    carry_ref[...] += jnp.sum(onehot, axis=0, keepdims=True) * jnp.where(step > 0, 1.0, 0.0)
    counts_ref[...] = carry_ref[...]

    rank_hi = jnp.floor(rank * (1.0 / RANK_RADIX))
    rank_lo = rank - rank_hi * RANK_RADIX
    digits = jnp.where(lane == 0, cls.astype(F32),
                       jnp.where(lane == 1, rank_hi, jnp.where(lane == 2, rank_lo, 0.0)))
    route_ref[...] = lax.dot_general(pick_ref[...], digits.astype(BF16), (((1,), (1,)), ((), ())),
                                     preferred_element_type=F32)


def _out_proj_route(x2, mixed_lo, mixed_hi, seq_len, w_top_bf16, w_out_bf16, g_ffn, wr, br, cast_weight):
    n, d = x2.shape
    tm = TOKEN_TILE
    n_tiles = n // tm
    tiles_per_seq = seq_len // tm
    half_tiles = tiles_per_seq // 2
    const = lambda shape: pl.BlockSpec(shape, lambda i: (0, 0))
    in_tile = lambda i: jnp.minimum(i, n_tiles - 1)
    out_tile = lambda i: jnp.maximum(i - 1, 0)
    lo_spec = pl.BlockSpec((tm, d), lambda i: (
        (in_tile(i) // tiles_per_seq) * half_tiles + jnp.minimum(in_tile(i) % tiles_per_seq, half_tiles - 1), 0))
    hi_spec = pl.BlockSpec((tm, d), lambda i: (
        (in_tile(i) // tiles_per_seq) * half_tiles + jnp.maximum(in_tile(i) % tiles_per_seq - half_tiles, 0), 0))
    tri = np.tril(np.ones((tm, tm), np.float32), -1)
    pick = np.eye(SUBLANES, ROUTER_LANES, dtype=np.float32)
    vmem = (2 * tm * d * 4 + 4 * tm * d * 2 + 2 * tm * ROW_WORDS * 4 + d * d * 2 + tm * d * 4
            + 2 * d * ROUTER_LANES * 2 + 2 * tm * tm * 2 + 3 * tm * d * 4)
    cast = _CastAlong(cast_weight, n_tiles, in_tile)
    vmem += cast.vmem_bytes
    rows, route, counts, cast_out = pl.pallas_call(
        functools.partial(_out_proj_route_kernel, tiles_per_seq=tiles_per_seq, n_tiles=n_tiles),
        grid=(n_tiles + 1,),
        in_specs=[pl.BlockSpec((tm, d), lambda i: (in_tile(i), 0)), lo_spec, hi_spec,
                  pl.BlockSpec((POOL_WIDTH, d), lambda i: (0, 0), pipeline_mode=pl.Buffered(1)),
                  pl.BlockSpec((FOURIER_WIDTH, d), lambda i: (1, 0), pipeline_mode=pl.Buffered(1)),
                  const((1, d)), const((d, ROUTER_LANES)), const((1, ROUTER_LANES)),
                  const((tm, tm)), const((SUBLANES, ROUTER_LANES)), cast.spec],
        out_specs=[pl.BlockSpec((tm, ROW_WORDS), lambda i: (out_tile(i), 0)),
                   pl.BlockSpec((SUBLANES, tm), lambda i: (0, out_tile(i))),
                   const((1, ROUTER_LANES)), cast.spec],
        out_shape=[jax.ShapeDtypeStruct((n, ROW_WORDS), F32),
                   jax.ShapeDtypeStruct((SUBLANES, n), F32),
                   jax.ShapeDtypeStruct((1, ROUTER_LANES), F32), cast.out_shape],
        scratch_shapes=[pltpu.VMEM((1, ROUTER_LANES), F32), pltpu.VMEM((tm, d), F32)],
        compiler_params=pltpu.CompilerParams(
            dimension_semantics=("arbitrary",), vmem_limit_bytes=vmem + 4 * MIB),
        name="out_proj_route",
    )(x2, mixed_lo, mixed_hi, w_top_bf16, w_out_bf16, g_ffn.reshape(1, d), wr, br,
      jnp.asarray(tri).astype(BF16), jnp.asarray(pick).astype(BF16), cast.src)
    return rows, route, counts, cast_out.reshape(cast.shape)


def _row_copy(src_ref, src_row, dst_ref, dst_row, sem):
    return pltpu.make_async_copy(src_ref.at[pl.ds(src_row, 1), :], dst_ref.at[pl.ds(dst_row, 1), :], sem)


def _plan_kernel(cnt_ref, route_ref, tok_ref, start_ref, ea_ref, eb_ref, lo_ref, hi_ref, nused_ref,
                 cstart_ref, pos_vmem_ref, pos_smem_ref, sem):
    tm = EXPERT_TILE
    shift = tm.bit_length() - 1
    n = tok_ref.shape[0]
    n_items = start_ref.shape[0]

    run = jnp.int32(0)
    inside = jnp.bool_(True)
    for c in range(N_CLASSES):
        cnt = cnt_ref[c]
        cstart_ref[c] = run
        inside = inside & ((cnt == 0) | (cnt >= tm))
        run = run + cnt

    cls_row = route_ref[0:1, :]
    slot_row = route_ref[1:2, :] * RANK_RADIX + route_ref[2:3, :]
    for c in range(N_CLASSES):
        slot_row = slot_row + jnp.where(cls_row == c, cstart_ref[c].astype(F32), 0.0)
    pos_vmem_ref[...] = slot_row.astype(jnp.int32)
    to_smem = pltpu.make_async_copy(pos_vmem_ref, pos_smem_ref, sem)
    to_smem.start()

    k = jnp.int32(0)
    for c in range(N_CLASSES):
        cnt = cnt_ref[c]
        cs = cstart_ref[c]
        ce = cs + cnt
        group, pair = divmod(c, PAIRS_PER_GROUP)
        ea = group * EXPERTS_PER_GROUP + PAIR_SLOT_A[pair]
        eb = group * EXPERTS_PER_GROUP + PAIR_SLOT_B[pair]
        first_window = cs >> shift
        n_inside = (cnt + tm - 1) >> shift
        n_aligned = jnp.where(cnt > 0, ((ce - 1) >> shift) - first_window + 1, 0)
        full_windows = cnt >> shift

        def emit(i, carry, k=k, cs=cs, ce=ce, ea=ea, eb=eb, first_window=first_window,
                 full_windows=full_windows):
            start_inside = jnp.where(i < full_windows, cs + i * tm, ce - tm)
            start_aligned = (first_window + i) * tm
            start_ref[k + i] = jnp.where(inside, start_inside, start_aligned)
            ea_ref[k + i] = ea
            eb_ref[k + i] = eb
            lo_ref[k + i] = jnp.where(inside, 0, jnp.clip(cs - start_aligned, 0, tm))
            hi_ref[k + i] = jnp.where(inside, tm, jnp.clip(ce - start_aligned, 0, tm))
            return carry

        n_class_items = jnp.where(inside, n_inside, n_aligned)
        lax.fori_loop(0, n_class_items, emit, 0)
        k = k + n_class_items
    nused_ref[0] = k

    def repeat_last(i, carry):
        for ref in (start_ref, ea_ref, eb_ref, lo_ref, hi_ref):
            ref[i] = ref[k - 1]
        return carry

    lax.fori_loop(k, n_items, repeat_last, 0)

    to_smem.wait()

    def invert(t, carry):
        tok_ref[pos_smem_ref[0, t]] = t
        return carry

    lax.fori_loop(0, n, invert, 0, unroll=INVERT_UNROLL)


def _plan_routing(route, counts, n):
    n_items = n // EXPERT_TILE + N_CLASSES
    smem = lambda size: jax.ShapeDtypeStruct((size,), jnp.int32)
    smem_spec = pl.BlockSpec(memory_space=pltpu.SMEM)
    return pl.pallas_call(
        _plan_kernel,
        in_specs=[smem_spec, pl.BlockSpec(memory_space=pltpu.VMEM)],
        out_specs=[smem_spec] * 7,
        out_shape=[smem(n)] + [smem(n_items)] * 5 + [smem(1)],
        scratch_shapes=[pltpu.SMEM((N_CLASSES,), jnp.int32), pltpu.VMEM((1, n), jnp.int32),
                        pltpu.SMEM((1, n), jnp.int32), pltpu.SemaphoreType.DMA(())],
        name="routing_plan",
    )(counts[0, :N_CLASSES].astype(jnp.int32), route)


def _expert_pair_kernel(start_ref, ea_ref, eb_ref, lo_ref, hi_ref, nused_ref, tok_ref,
                        rows_hbm_ref, gffn_ref, gfin_ref,
                        wga_ref, wua_ref, wda_ref, wgb_ref, wub_ref, wdb_ref, out_hbm_ref,
                        buf_ref, acc_ref, gather_sem, scatter_sem):
    del ea_ref, eb_ref
    tm = EXPERT_TILE
    j = pl.program_id(0)
    last = nused_ref[0] - 1
    slot = j % 2
    other = 1 - slot

    def start_gather(item, dst_slot, rows=range(tm)):
        base = start_ref[item]
        for r in rows:
            _row_copy(rows_hbm_ref, tok_ref[base + r], buf_ref.at[dst_slot], r, gather_sem.at[dst_slot]).start()

    def wait_gather(dst_slot):
        pltpu.make_async_copy(rows_hbm_ref.at[pl.ds(0, tm), :], buf_ref.at[dst_slot],
                              gather_sem.at[dst_slot]).wait()

    def start_scatter(item, src_slot, rows=range(tm)):
        base = start_ref[item]
        for r in rows:
            _row_copy(acc_ref.at[src_slot], r, out_hbm_ref, tok_ref[base + r], scatter_sem).start()

    def wait_scatter():
        pltpu.make_async_copy(acc_ref.at[0], out_hbm_ref.at[pl.ds(0, tm), :], scatter_sem).wait()

    gslot = j % GATHER_SLOTS

    @pl.when(j == 0)
    def _():
        acc_ref[...] = jnp.zeros_like(acc_ref)
        start_gather(0, 0)
        start_gather(jnp.minimum(1, last), 1)
        start_scatter(0, 1)

    @pl.when(j <= last)
    def _():
        wait_scatter()
        wait_gather(gslot)
        prev_item = jnp.maximum(j - 1, 0)
        ahead_item = jnp.minimum(j + 2, last)

        def issue_copies(part):
            half_parts = DMA_PARTS // 2
            rows = range((part % half_parts) * tm // half_parts, (part % half_parts + 1) * tm // half_parts)
            if part < half_parts:
                start_scatter(prev_item, other, rows)
            else:
                start_gather(ahead_item, (j + 2) % GATHER_SLOTS, rows)

        lo = lo_ref[j]
        hi = hi_ref[j]
        x1 = buf_ref[gslot, :, :D_MODEL]
        h = _rmsnorm(x1, gffn_ref[...]).astype(BF16)

        part = 0
        ff_chunk = EXPERT_FF * 4 // DMA_PARTS
        for slot_lane, (wg_ref, wu_ref, wd_ref) in enumerate(((wga_ref, wua_ref, wda_ref),
                                                              (wgb_ref, wub_ref, wdb_ref))):
            for c0 in range(0, EXPERT_FF, ff_chunk):
                issue_copies(part)
                wts = buf_ref[gslot, :, D_MODEL:]
                lane = lax.broadcasted_iota(jnp.int32, wts.shape, 1)
                w = jnp.sum(jnp.where(lane == slot_lane, wts, 0.0), axis=-1, keepdims=True)
                a = _dot(h, wg_ref[0, :, c0:c0 + ff_chunk])
                v = _dot(h, wu_ref[0, :, c0:c0 + ff_chunk])
                act = (a * (1.0 / (1.0 + jnp.exp(-a))) * v * w).astype(BF16)
                for half, d0 in enumerate(range(0, D_MODEL, D_MODEL // 2)):
                    cols = slice(d0, d0 + D_MODEL // 2)
                    if half == 1:
                        issue_copies(part + 1)
                    y_part = _dot(act, wd_ref[0, c0:c0 + ff_chunk, cols])
                    if part == 0:
                        acc_ref[slot, :, cols] = y_part
                    else:
                        acc_ref[slot, :, cols] += y_part
                part += 2
        res = _rmsnorm(buf_ref[gslot, :, :D_MODEL] + acc_ref[slot], gfin_ref[...])

        row = lax.broadcasted_iota(jnp.int32, (tm, 1), 0)
        mine = (row >= lo) & (row < hi)
        acc_ref[slot] = jnp.where(mine, res, 0.0)

        @pl.when(lo > 0)
        def _():
            acc_ref[slot] = jnp.where(row < lo, acc_ref[other], acc_ref[slot])

    @pl.when(j == last)
    def _():
        wait_scatter()
        start_scatter(j, slot)
        wait_scatter()
        wait_gather((j + 1) % GATHER_SLOTS)
        wait_gather((j + 2) % GATHER_SLOTS)


def _expert_pairs(item_start, item_ea, item_eb, item_lo, item_hi, n_used, slot_token, rows,
                  g_ffn, g_final, wg, wu, wd):
    n, w = rows.shape
    d, f = D_MODEL, EXPERT_FF
    tm = EXPERT_TILE
    gate_a = pl.BlockSpec((1, d, f), lambda j, st, ea, eb, lo, hi, nu, tok: (ea[j], 0, 0))
    gate_b = pl.BlockSpec((1, d, f), lambda j, st, ea, eb, lo, hi, nu, tok: (eb[j], 0, 0))
    down_a = pl.BlockSpec((1, f, d), lambda j, st, ea, eb, lo, hi, nu, tok: (ea[j], 0, 0))
    down_b = pl.BlockSpec((1, f, d), lambda j, st, ea, eb, lo, hi, nu, tok: (eb[j], 0, 0))
    gain = pl.BlockSpec((1, d), lambda j, st, ea, eb, lo, hi, nu, tok: (0, 0))
    grid_spec = pltpu.PrefetchScalarGridSpec(
        num_scalar_prefetch=7,
        grid=(item_start.shape[0],),
        in_specs=[pl.BlockSpec(memory_space=pl.ANY), gain, gain,
                  gate_a, gate_a, down_a, gate_b, gate_b, down_b],
        out_specs=pl.BlockSpec(memory_space=pl.ANY),
        scratch_shapes=[pltpu.VMEM((GATHER_SLOTS, tm, w), F32), pltpu.VMEM((2, tm, d), F32),
                        pltpu.SemaphoreType.DMA((GATHER_SLOTS,)), pltpu.SemaphoreType.DMA(())],
    )
    vmem = 2 * 6 * d * f * 2 + GATHER_SLOTS * tm * w * 4 + 2 * tm * d * 4 + 8 * tm * d * 4
    return pl.pallas_call(
        _expert_pair_kernel,
        grid_spec=grid_spec,
        out_shape=jax.ShapeDtypeStruct((n, d), F32),
        compiler_params=pltpu.CompilerParams(
            dimension_semantics=("arbitrary",), vmem_limit_bytes=vmem + 4 * MIB),
        name="expert_pairs",
    )(item_start, item_ea, item_eb, item_lo, item_hi, n_used, slot_token, rows, g_ffn.reshape(1, d),
      g_final.reshape(1, d), wg, wu, wd, wg, wu, wd)


def kernel(x, g_mix, w_in, w_pool, pool_scale, w_fourier, w_out, g_ffn, w_group_router,
           b_group_router, w_expert_router, b_expert_router, w_gate, w_up, w_down, g_final):
    b, s, d = x.shape
    assert d == D_MODEL and s % (2 * SEQ_TILE) == 0 and s % (2 * TOKEN_TILE) == 0
    assert (b * s) % max(TOKEN_TILE, EXPERT_TILE) == 0
    n = b * s
    x2 = x.reshape(n, d)

    cw, sw, w_top_bf16 = _mixer_weights(w_fourier, w_pool, pool_scale, w_out, s)
    u, (w_out_bf16, w_gate_bf16) = _norm_proj(x2, g_mix, w_in, [w_out, w_gate])
    mixed_lo, mixed_hi, w_up_bf16 = _mix(u.reshape(b, s, d), cw, sw, w_up)

    wr = jnp.concatenate([w_group_router, w_expert_router], axis=1)
    wr = jnp.pad(wr, ((0, 0), (0, ROUTER_LANES - wr.shape[1]))).astype(BF16)
    br = jnp.concatenate([b_group_router, b_expert_router])
    br = jnp.pad(br, (0, ROUTER_LANES - br.shape[0])).reshape(1, ROUTER_LANES)

    rows, route, counts, w_down_bf16 = _out_proj_route(
        x2, mixed_lo.reshape(n // 2, d), mixed_hi.reshape(n // 2, d), s, w_top_bf16, w_out_bf16, g_ffn, wr, br,
        w_down)
    slot_token, item_start, item_ea, item_eb, item_lo, item_hi, n_used = _plan_routing(route, counts, n)
    out = _expert_pairs(item_start, item_ea, item_eb, item_lo, item_hi, n_used, slot_token, rows,
                        g_ffn, g_final, w_gate_bf16, w_up_bf16, w_down_bf16)
    return out.reshape(b, s, d)
```

```python
import functools

import numpy as np
import jax
import jax.numpy as jnp
from jax import lax
from jax.experimental import pallas as pl
from jax.experimental.pallas import tpu as pltpu

D_MODEL = 2048
POOL_WINDOWS = (2, 4, 8, 16)
N_POOL_GROUPS = len(POOL_WINDOWS)
POOL_WIDTH = D_MODEL // 2
POOL_GROUP_DIM = POOL_WIDTH // N_POOL_GROUPS
FOURIER_WIDTH = D_MODEL - POOL_WIDTH
N_FOURIER_HEADS = 4
FOURIER_HEAD_DIM = FOURIER_WIDTH // N_FOURIER_HEADS
N_EXPERT_GROUPS = 4
EXPERTS_PER_GROUP = 4
EXPERT_FF = D_MODEL // 4
RMS_EPS = 1e-6

LANES = 128
SUBLANES = 8
BF16_SUBLANES = 16
ROUTER_LANES = LANES
MIB = 1024 * 1024

TOKEN_TILE = 512
SEQ_TILE = 256
POOL_HALO = BF16_SUBLANES
DFT_ROWS = SEQ_TILE + BF16_SUBLANES
TWIDDLE_ROWS = 16
FOLD_TILES = 4

PAIR_SLOT_A = (0, 0, 0, 1, 1, 3)
PAIR_SLOT_B = (1, 2, 3, 3, 2, 2)
PAIRS_PER_GROUP = len(PAIR_SLOT_A)
N_CLASSES = N_EXPERT_GROUPS * PAIRS_PER_GROUP
ROW_WORDS = D_MODEL + LANES
RANK_RADIX = 128
EXPERT_TILE = 256
DMA_PARTS = 8
GATHER_SLOTS = 3
INVERT_UNROLL = 32
WEIGHT_CAST_ROWS = 256

BF16 = jnp.bfloat16
F32 = jnp.float32


def _rmsnorm(x, g):
    ms = jnp.mean(x * x, axis=-1, keepdims=True)
    return x * lax.rsqrt(ms + RMS_EPS) * g


def _dot(a, b):
    return jnp.dot(a, b, preferred_element_type=F32)


def _twiddle(rows, cols, period):
    m = (np.asarray(rows, np.int64)[:, None] * np.asarray(cols, np.int64)[None, :]) % period
    ang = (2.0 * np.pi / period) * m.astype(np.float64)
    return np.cos(ang).astype(np.float32), np.sin(ang).astype(np.float32)


def _pool_band(seq_len, tile, halo):
    n_tiles = seq_len // tile
    out = np.zeros((3, N_POOL_GROUPS, tile, tile + 2 * halo), np.float64)
    for v, m in enumerate((0, 1, n_tiles - 1)):
        t0 = m * tile
        for g, k in enumerate(POOL_WINDOWS):
            for r in range(tile):
                t = t0 + r
                lo = max(t - (k - 1) // 2, 0)
                hi = min(t + k // 2 + 1, seq_len)
                out[v, g, r, lo - t0 + halo:hi - t0 + halo] = 1.0 / (hi - lo)
                out[v, g, r, r + halo] -= 1.0
    return out.astype(np.float32)


def _mixer_weight_kernel(cd_ref, sd_ref, wf_ref, wp_ref, ps_ref, wo_ref, cw_ref, sw_ref, wtop_ref, *, scale):
    wf = wf_ref[0]
    cw = jnp.dot(cd_ref[...], wf, preferred_element_type=F32, precision=lax.Precision.HIGHEST)
    sw = jnp.dot(sd_ref[...], wf, preferred_element_type=F32, precision=lax.Precision.HIGHEST)
    cw_ref[0] = (cw * scale).astype(BF16)
    sw_ref[0] = (sw * (-scale)).astype(BF16)
    wtop_ref[...] = _dot((wp_ref[0] * ps_ref[0]).astype(BF16), wo_ref[...].astype(BF16)).astype(BF16)


def _mixer_weights(w_fourier, w_pool, pool_scale, w_out, seq_len):
    dh = FOURIER_HEAD_DIM
    assert N_FOURIER_HEADS == N_POOL_GROUPS and dh == POOL_GROUP_DIM
    d = w_out.shape[1]
    cd, sd = _twiddle(np.arange(dh), np.arange(dh), dh)
    scale = 1.0 / np.sqrt(float(seq_len * dh))
    mat = pl.BlockSpec((dh, dh), lambda h: (0, 0))
    per_head = pl.BlockSpec((1, dh, dh), lambda h: (h, 0, 0))
    out_rows = pl.BlockSpec((dh, d), lambda h: (h, 0))
    return pl.pallas_call(
        functools.partial(_mixer_weight_kernel, scale=scale),
        grid=(N_FOURIER_HEADS,),
        in_specs=[mat, mat, per_head, per_head, pl.BlockSpec((1, 1, dh), lambda h: (h, 0, 0)), out_rows],
        out_specs=[per_head, per_head, out_rows],
        out_shape=[jax.ShapeDtypeStruct((N_FOURIER_HEADS, dh, dh), BF16)] * 2
        + [jax.ShapeDtypeStruct((POOL_WIDTH, d), BF16)],
        name="mixer_weights",
    )(jnp.asarray(cd), jnp.asarray(sd), w_fourier, w_pool, pool_scale.reshape(N_POOL_GROUPS, 1, dh), w_out)


class _CastAlong:
    def __init__(self, w, n_chunks, chunk_of):
        cols = w.shape[-1]
        rows = w.size // cols // n_chunks
        self.shape = w.shape
        self.src = w.reshape(n_chunks, rows, cols)
        self.spec = pl.BlockSpec((1, rows, cols), lambda *idx: (chunk_of(*idx), 0, 0))
        self.out_shape = jax.ShapeDtypeStruct((n_chunks, rows, cols), BF16)
        self.vmem_bytes = 2 * rows * cols * (4 + 2)


def _cast_chunks(src_refs, dst_refs):
    for src_ref, dst_ref in zip(src_refs, dst_refs):
        dst_ref[...] = src_ref[...].astype(BF16)


def _norm_proj_kernel(x_ref, g_ref, w_ref, *rest, n_cast):
    cast_src, (u_ref, *cast_dst), w_bf16_ref = rest[:n_cast], rest[n_cast:-1], rest[-1]

    @pl.when(pl.program_id(0) == 0)
    def _():
        for r0 in range(0, w_ref.shape[0], WEIGHT_CAST_ROWS):
            w_bf16_ref[r0:r0 + WEIGHT_CAST_ROWS, :] = w_ref[r0:r0 + WEIGHT_CAST_ROWS, :].astype(BF16)

    h = _rmsnorm(x_ref[...], g_ref[...])
    u_ref[...] = _dot(h.astype(BF16), w_bf16_ref[...]).astype(BF16)
    _cast_chunks(cast_src, cast_dst)


def _norm_proj(x2, g_mix, w_in, cast_weights):
    n, d = x2.shape
    tm = TOKEN_TILE
    casts = [_CastAlong(w, n // tm, lambda i: i) for w in cast_weights]
    vmem = (2 * tm * d * 4 + d * d * (4 + 2) + 2 * tm * d * 2 + 3 * tm * d * 4
            + sum(c.vmem_bytes for c in casts))
    u, *cast_out = pl.pallas_call(
        functools.partial(_norm_proj_kernel, n_cast=len(casts)),
        grid=(n // tm,),
        in_specs=[
            pl.BlockSpec((tm, d), lambda i: (i, 0)),
            pl.BlockSpec((1, d), lambda i: (0, 0)),
            pl.BlockSpec((d, d), lambda i: (0, 0), pipeline_mode=pl.Buffered(1)),
        ] + [c.spec for c in casts],
        out_specs=[pl.BlockSpec((tm, d), lambda i: (i, 0))] + [c.spec for c in casts],
        out_shape=[jax.ShapeDtypeStruct((n, d), BF16)] + [c.out_shape for c in casts],
        scratch_shapes=[pltpu.VMEM((d, d), BF16)],
        compiler_params=pltpu.CompilerParams(
            dimension_semantics=("arbitrary",), vmem_limit_bytes=vmem + 4 * MIB),
        name="norm_proj",
    )(x2, g_mix.reshape(1, d), w_in, *[c.src for c in casts])
    return u, [o.reshape(c.shape) for o, c in zip(cast_out, casts)]


def _fold_kernel(blk_ref, mirror_ref, after_ref, rev_ref, cw_ref, sw_ref, even_ref, odd_ref):
    t = SEQ_TILE
    halo = POOL_HALO
    hd = FOURIER_HEAD_DIM
    for i in range(FOLD_TILES):
        top = (FOLD_TILES - 1 - i) * t
        if i == 0:
            after = jnp.where(pl.program_id(1) > 0, after_ref[...], jnp.zeros_like(after_ref))
        else:
            after = mirror_ref[top + t:top + t + halo, :]
        window = jnp.concatenate([mirror_ref[top:top + t, :], after], axis=0)
        mirrored = _dot(rev_ref[...], window)
        blk = blk_ref[i * t:(i + 1) * t, :].astype(F32)
        even_ref[i * t:(i + 1) * t, :] = (blk + mirrored).astype(BF16)
        odd_ref[i * t:(i + 1) * t, :] = (blk - mirrored).astype(BF16)
    for h in range(N_FOURIER_HEADS):
        cols = slice(h * hd, (h + 1) * hd)
        even_ref[:, cols] = _dot(even_ref[:, cols], cw_ref[h]).astype(BF16)
        odd_ref[:, cols] = _dot(odd_ref[:, cols], sw_ref[h]).astype(BF16)


def _fold_sequence(u3, rev, cw, sw):
    b, s, _ = u3.shape
    halo = POOL_HALO
    rows = FOLD_TILES * SEQ_TILE
    n_blocks = s // rows
    block = lambda block_of: pl.BlockSpec((None, rows, FOURIER_WIDTH), lambda bi, j: (bi, block_of(j), 1))
    after = pl.BlockSpec((None, halo, FOURIER_WIDTH),
                         lambda bi, j: (bi, jnp.minimum((n_blocks - j) * (rows // halo), s // halo - 1), 1))
    out = pl.BlockSpec((None, rows, FOURIER_WIDTH), lambda bi, j: (bi, j, 0))
    return pl.pallas_call(
        _fold_kernel,
        grid=(b, n_blocks // 2),
        in_specs=[block(lambda j: j), block(lambda j: n_blocks - 1 - j), after,
                  pl.BlockSpec(rev.shape, lambda bi, j: (0, 0)),
                  pl.BlockSpec(cw.shape, lambda bi, j: (0, 0, 0)), pl.BlockSpec(sw.shape, lambda bi, j: (0, 0, 0))],
        out_specs=[out, out],
        out_shape=[jax.ShapeDtypeStruct((b, s // 2, FOURIER_WIDTH), BF16)] * 2,
        compiler_params=pltpu.CompilerParams(
            dimension_semantics=("arbitrary", "arbitrary"),
            vmem_limit_bytes=2 * 4 * rows * FOURIER_WIDTH * 2 + 8 * MIB),
        name="fold_sequence",
    )(u3, u3, u3, rev, cw, sw)


def _mix_kernel(even_ref, odd_ref, umid_ref, up_lo_ref, prev_lo_ref, next_lo_ref, up_hi_ref, prev_hi_ref, next_hi_ref,
                c0_ref, s0_ref, cph_ref, sph_ref, band_lo_ref, band_hi_ref, rev_ref,
                cw_ref, cast_src_ref, lo_ref, hi_ref, cast_dst_ref, lhs_ref):
    t = SEQ_TILE
    tp = DFT_ROWS
    gd = POOL_GROUP_DIM
    hd = FOURIER_HEAD_DIM

    m = pl.program_id(1)

    @pl.when(pl.program_id(0) == 0)
    def _():
        cph = cph_ref[0]
        sph = sph_ref[0]
        for r0 in range(0, tp, TWIDDLE_ROWS):
            c0 = c0_ref[r0:r0 + TWIDDLE_ROWS, :]
            s0 = s0_ref[r0:r0 + TWIDDLE_ROWS, :]
            lhs_ref[m, r0:r0 + TWIDDLE_ROWS, :] = (cph * c0 - sph * s0).astype(BF16)
            lhs_ref[m, tp + r0:tp + r0 + TWIDDLE_ROWS, :] = (sph * c0 + cph * s0).astype(BF16)

    _cast_chunks([cast_src_ref], [cast_dst_ref])
    row = lax.broadcasted_iota(jnp.int32, (tp, 1), 0)
    sign = jnp.where(row % 2 == 0, 1.0, -1.0)
    mid = jnp.concatenate([_dot(umid_ref[:, h * hd:(h + 1) * hd], cw_ref[h]) for h in range(N_FOURIER_HEADS)],
                          axis=1)
    pc = _dot(lhs_ref[m, :tp, :], even_ref[...]) + sign * mid[0:1, :]
    qs = _dot(lhs_ref[m, tp:, :], odd_ref[...])
    lo_ref[:, POOL_WIDTH:] = (pc + qs)[:t].astype(BF16)
    hi_ref[:, POOL_WIDTH:] = _dot(rev_ref[...], (pc - qs).astype(BF16)).astype(BF16)

    def pool(up_ref, prev_ref, next_ref, band_ref, out_ref):
        win = jnp.concatenate([prev_ref[...], up_ref[...], next_ref[...]], axis=0)
        for g in range(N_POOL_GROUPS):
            cols = slice(g * gd, (g + 1) * gd)
            out_ref[:, cols] = _dot(band_ref[0, g], win[:, cols]).astype(BF16)

    pool(up_lo_ref, prev_lo_ref, next_lo_ref, band_lo_ref, lo_ref)
    pool(up_hi_ref, prev_hi_ref, next_hi_ref, band_hi_ref, hi_ref)


def _mix(u3, cw, sw, cast_weight):
    b, s, d = u3.shape
    t = SEQ_TILE
    tp = DFT_ROWS
    halo = POOL_HALO
    n_tiles = s // t
    n_steps = n_tiles // 2
    halo_blocks_per_tile = t // halo
    last_halo_block = s // halo - 1

    sh = s // 2
    c0, s0 = _twiddle(np.arange(tp), np.arange(sh), s)
    cph, sph = _twiddle(np.arange(0, sh, t), np.arange(sh), s)
    band = jnp.asarray(_pool_band(s, t, halo)).astype(BF16)
    rev = np.zeros((t, tp), np.float32)
    rev[np.arange(t), t - np.arange(t)] = 1.0
    rev = jnp.asarray(rev).astype(BF16)
    even, odd = _fold_sequence(u3, rev, cw, sw)

    hi_tile = lambda m: n_tiles - 1 - m
    tile_spec = lambda tile_of: pl.BlockSpec((None, t, POOL_WIDTH), lambda bi, m: (bi, tile_of(m), 0))
    prev_spec = lambda tile_of: pl.BlockSpec(
        (None, halo, POOL_WIDTH),
        lambda bi, m: (bi, jnp.maximum(tile_of(m) * halo_blocks_per_tile - 1, 0), 0))
    next_spec = lambda tile_of: pl.BlockSpec(
        (None, halo, POOL_WIDTH),
        lambda bi, m: (bi, jnp.minimum((tile_of(m) + 1) * halo_blocks_per_tile, last_halo_block), 0))
    band_shape = (1, N_POOL_GROUPS, t, t + 2 * halo)
    const2 = lambda shape: pl.BlockSpec(shape, lambda bi, m: (0, 0))
    const3 = lambda shape: pl.BlockSpec(shape, lambda bi, m: (0, 0, 0))
    lo_tile = lambda m: m

    in_specs = [
        pl.BlockSpec((None, sh, FOURIER_WIDTH), lambda bi, m: (bi, 0, 0)),
        pl.BlockSpec((None, sh, FOURIER_WIDTH), lambda bi, m: (bi, 0, 0)),
        pl.BlockSpec((None, halo, FOURIER_WIDTH), lambda bi, m: (bi, sh // halo, 1)),
        tile_spec(lo_tile), prev_spec(lo_tile), next_spec(lo_tile),
        tile_spec(hi_tile), prev_spec(hi_tile), next_spec(hi_tile),
        pl.BlockSpec((tp, sh), lambda bi, m: (0, 0), pipeline_mode=pl.Buffered(1)),
        pl.BlockSpec((tp, sh), lambda bi, m: (0, 0), pipeline_mode=pl.Buffered(1)),
        pl.BlockSpec((1, 1, sh), lambda bi, m: (m, 0, 0)),
        pl.BlockSpec((1, 1, sh), lambda bi, m: (m, 0, 0)),
        pl.BlockSpec(band_shape, lambda bi, m: (jnp.where(m == 0, 0, 1), 0, 0, 0)),
        pl.BlockSpec(band_shape, lambda bi, m: (jnp.where(m == 0, 2, 1), 0, 0, 0)),
        const2((t, tp)),
        const3((N_FOURIER_HEADS, FOURIER_HEAD_DIM, FOURIER_HEAD_DIM)),
    ]
    half_out = pl.BlockSpec((None, t, d), lambda bi, m: (bi, m, 0))
    hi_out = pl.BlockSpec((None, t, d), lambda bi, m: (bi, n_steps - 1 - m, 0))
    cast = _CastAlong(cast_weight, n_steps * b, lambda bi, m: m * b + bi)
    vmem = (2 * 2 * sh * FOURIER_WIDTH * 2
            + 2 * tp * sh * 4
            + n_steps * 2 * tp * sh * 2
            + 2 * 2 * t * d * 2 * 2
            + 4 * 2 * tp * FOURIER_WIDTH * 4
            + cast.vmem_bytes)
    lo, hi, cast_out = pl.pallas_call(
        _mix_kernel,
        grid=(b, n_steps),
        in_specs=in_specs + [cast.spec],
        out_specs=[half_out, hi_out, cast.spec],
        out_shape=[jax.ShapeDtypeStruct((b, s // 2, d), BF16)] * 2 + [cast.out_shape],
        scratch_shapes=[pltpu.VMEM((n_steps, 2 * tp, sh), BF16)],
        compiler_params=pltpu.CompilerParams(
            dimension_semantics=("arbitrary", "arbitrary"), vmem_limit_bytes=vmem + 4 * MIB),
        name="seq_mix",
    )(even, odd, u3, u3, u3, u3, u3, u3, u3, jnp.asarray(c0), jnp.asarray(s0),
      jnp.asarray(cph).reshape(n_steps, 1, sh), jnp.asarray(sph).reshape(n_steps, 1, sh),
      band, band, rev,
      cw, cast.src)
    return lo, hi, cast_out.reshape(cast.shape)


def _out_proj_route_kernel(x_ref, mixed_lo_ref, mixed_hi_ref, wtop_ref, wbot_ref, g_ref, wr_ref, br_ref,
                           tri_ref, pick_ref, cast_src_ref, rows_ref, route_ref, counts_ref, cast_dst_ref,
                           carry_ref, x1_ref, *, tiles_per_seq, n_tiles):
    step = pl.program_id(0)

    @pl.when(step == 0)
    def _():
        carry_ref[...] = jnp.zeros_like(carry_ref)
        x1_ref[...] = jnp.zeros_like(x1_ref)

    _cast_chunks([cast_src_ref], [cast_dst_ref])

    x1 = x1_ref[...]
    rows_ref[:, :D_MODEL] = x1
    h2 = _rmsnorm(x1, g_ref[...])
    logits = _dot(h2.astype(BF16), wr_ref[...]) + br_ref[...]

    tile = jnp.minimum(step, n_tiles - 1)
    in_lo_half = (tile % tiles_per_seq) < tiles_per_seq // 2
    mixed = jnp.where(in_lo_half, mixed_lo_ref[...], mixed_hi_ref[...])
    x1_ref[...] = (x_ref[...] + _dot(mixed[:, :POOL_WIDTH], wtop_ref[...])
                   + _dot(mixed[:, POOL_WIDTH:], wbot_ref[...]))

    lane = lax.broadcasted_iota(jnp.int32, logits.shape, 1)
    neg = jnp.float32(-jnp.inf)
    big = jnp.int32(ROUTER_LANES)

    is_group = lane < N_EXPERT_GROUPS
    gl = jnp.where(is_group, logits, neg)
    gmax = jnp.max(gl, axis=-1, keepdims=True)
    gidx = jnp.min(jnp.where(gl == gmax, lane, big), axis=-1, keepdims=True)
    p_g = 1.0 / jnp.sum(jnp.exp(gl - gmax), axis=-1, keepdims=True)

    e_lane = lane - N_EXPERT_GROUPS
    in_group = (e_lane >= gidx * EXPERTS_PER_GROUP) & (e_lane < (gidx + 1) * EXPERTS_PER_GROUP)
    el = jnp.where(in_group, logits, neg)
    v1 = jnp.max(el, axis=-1, keepdims=True)
    i1 = jnp.min(jnp.where(el == v1, lane, big), axis=-1, keepdims=True)
    el2 = jnp.where(lane == i1, neg, el)
    v2 = jnp.max(el2, axis=-1, keepdims=True)
    i2 = jnp.min(jnp.where(el2 == v2, lane, big), axis=-1, keepdims=True)
    r = jnp.exp(v2 - v1)
    w1 = p_g / (1.0 + r)
    w2 = p_g * r / (1.0 + r)

    first_is_low = i1 < i2
    first_lane = N_EXPERT_GROUPS + gidx * EXPERTS_PER_GROUP
    la = jnp.where(first_is_low, i1, i2) - first_lane
    lb = jnp.where(first_is_low, i2, i1) - first_lane
    w_low = jnp.where(first_is_low, w1, w2)
    w_high = jnp.where(first_is_low, w2, w1)
    assert (PAIR_SLOT_A, PAIR_SLOT_B) == ((0, 0, 0, 1, 1, 3), (1, 2, 3, 3, 2, 2))
    pair = jnp.where(la == 0, lb - 1, jnp.where(la == 1, 6 - lb, 5))
    slot_a_is_high = la == 2
    w_a = jnp.where(slot_a_is_high, w_high, w_low)
    w_b = jnp.where(slot_a_is_high, w_low, w_high)
    cls = gidx * PAIRS_PER_GROUP + pair
    rows_ref[:, D_MODEL:] = jnp.where(lane == 0, w_a, jnp.where(lane == 1, w_b, 0.0))

    onehot = jnp.where(lane == cls, 1.0, 0.0)
    before = _dot(tri_ref[...], onehot.astype(BF16)) + carry_ref[...]
    rank = jnp.sum(jnp.where(lane == cls, before, 0.0), axis=-1, keepdims=True)
    carry_ref[...] += jnp.sum(onehot, axis=0, keepdims=True) * jnp.where(step > 0, 1.0, 0.0)
    counts_ref[...] = carry_ref[...]

    rank_hi = jnp.floor(rank * (1.0 / RANK_RADIX))
    rank_lo = rank - rank_hi * RANK_RADIX
    digits = jnp.where(lane == 0, cls.astype(F32),
                       jnp.where(lane == 1, rank_hi, jnp.where(lane == 2, rank_lo, 0.0)))
    route_ref[...] = lax.dot_general(pick_ref[...], digits.astype(BF16), (((1,), (1,)), ((), ())),
                                     preferred_element_type=F32)


def _out_proj_route(x2, mixed_lo, mixed_hi, seq_len, w_top_bf16, w_out_bf16, g_ffn, wr, br, cast_weight):
    n, d = x2.shape
    tm = TOKEN_TILE
    n_tiles = n // tm
    tiles_per_seq = seq_len // tm
    half_tiles = tiles_per_seq // 2
    const = lambda shape: pl.BlockSpec(shape, lambda i: (0, 0))
    in_tile = lambda i: jnp.minimum(i, n_tiles - 1)
    out_tile = lambda i: jnp.maximum(i - 1, 0)
    lo_spec = pl.BlockSpec((tm, d), lambda i: (
        (in_tile(i) // tiles_per_seq) * half_tiles + jnp.minimum(in_tile(i) % tiles_per_seq, half_tiles - 1), 0))
    hi_spec = pl.BlockSpec((tm, d), lambda i: (
        (in_tile(i) // tiles_per_seq) * half_tiles + jnp.maximum(in_tile(i) % tiles_per_seq - half_tiles, 0), 0))
    tri = np.tril(np.ones((tm, tm), np.float32), -1)
    pick = np.eye(SUBLANES, ROUTER_LANES, dtype=np.float32)
    vmem = (2 * tm * d * 4 + 4 * tm * d * 2 + 2 * tm * ROW_WORDS * 4 + d * d * 2 + tm * d * 4
            + 2 * d * ROUTER_LANES * 2 + 2 * tm * tm * 2 + 3 * tm * d * 4)
    cast = _CastAlong(cast_weight, n_tiles, in_tile)
    vmem += cast.vmem_bytes
    rows, route, counts, cast_out = pl.pallas_call(
        functools.partial(_out_proj_route_kernel, tiles_per_seq=tiles_per_seq, n_tiles=n_tiles),
        grid=(n_tiles + 1,),
        in_specs=[pl.BlockSpec((tm, d), lambda i: (in_tile(i), 0)), lo_spec, hi_spec,
                  pl.BlockSpec((POOL_WIDTH, d), lambda i: (0, 0), pipeline_mode=pl.Buffered(1)),
                  pl.BlockSpec((FOURIER_WIDTH, d), lambda i: (1, 0), pipeline_mode=pl.Buffered(1)),
                  const((1, d)), const((d, ROUTER_LANES)), const((1, ROUTER_LANES)),
                  const((tm, tm)), const((SUBLANES, ROUTER_LANES)), cast.spec],
        out_specs=[pl.BlockSpec((tm, ROW_WORDS), lambda i: (out_tile(i), 0)),
                   pl.BlockSpec((SUBLANES, tm), lambda i: (0, out_tile(i))),
                   const((1, ROUTER_LANES)), cast.spec],
        out_shape=[jax.ShapeDtypeStruct((n, ROW_WORDS), F32),
                   jax.ShapeDtypeStruct((SUBLANES, n), F32),
                   jax.ShapeDtypeStruct((1, ROUTER_LANES), F32), cast.out_shape],
        scratch_shapes=[pltpu.VMEM((1, ROUTER_LANES), F32), pltpu.VMEM((tm, d), F32)],
        compiler_params=pltpu.CompilerParams(
            dimension_semantics=("arbitrary",), vmem_limit_bytes=vmem + 4 * MIB),
        name="out_proj_route",
    )(x2, mixed_lo, mixed_hi, w_top_bf16, w_out_bf16, g_ffn.reshape(1, d), wr, br,
      jnp.asarray(tri).astype(BF16), jnp.asarray(pick).astype(BF16), cast.src)
    return rows, route, counts, cast_out.reshape(cast.shape)


def _row_copy(src_ref, src_row, dst_ref, dst_row, sem):
    return pltpu.make_async_copy(src_ref.at[pl.ds(src_row, 1), :], dst_ref.at[pl.ds(dst_row, 1), :], sem)


def _plan_kernel(cnt_ref, route_ref, tok_ref, start_ref, ea_ref, eb_ref, lo_ref, hi_ref, nused_ref,
                 cstart_ref, pos_vmem_ref, pos_smem_ref, sem):
    tm = EXPERT_TILE
    shift = tm.bit_length() - 1
    n = tok_ref.shape[0]
    n_items = start_ref.shape[0]

    run = jnp.int32(0)
    inside = jnp.bool_(True)
    for c in range(N_CLASSES):
        cnt = cnt_ref[c]
        cstart_ref[c] = run
        inside = inside & ((cnt == 0) | (cnt >= tm))
        run = run + cnt

    cls_row = route_ref[0:1, :]
    slot_row = route_ref[1:2, :] * RANK_RADIX + route_ref[2:3, :]
    for c in range(N_CLASSES):
        slot_row = slot_row + jnp.where(cls_row == c, cstart_ref[c].astype(F32), 0.0)
    pos_vmem_ref[...] = slot_row.astype(jnp.int32)
    to_smem = pltpu.make_async_copy(pos_vmem_ref, pos_smem_ref, sem)
    to_smem.start()

    k = jnp.int32(0)
    for c in range(N_CLASSES):
        cnt = cnt_ref[c]
        cs = cstart_ref[c]
        ce = cs + cnt
        group, pair = divmod(c, PAIRS_PER_GROUP)
        ea = group * EXPERTS_PER_GROUP + PAIR_SLOT_A[pair]
        eb = group * EXPERTS_PER_GROUP + PAIR_SLOT_B[pair]
        first_window = cs >> shift
        n_inside = (cnt + tm - 1) >> shift
        n_aligned = jnp.where(cnt > 0, ((ce - 1) >> shift) - first_window + 1, 0)
        full_windows = cnt >> shift

        def emit(i, carry, k=k, cs=cs, ce=ce, ea=ea, eb=eb, first_window=first_window,
                 full_windows=full_windows):
            start_inside = jnp.where(i < full_windows, cs + i * tm, ce - tm)
            start_aligned = (first_window + i) * tm
            start_ref[k + i] = jnp.where(inside, start_inside, start_aligned)
            ea_ref[k + i] = ea
            eb_ref[k + i] = eb
            lo_ref[k + i] = jnp.where(inside, 0, jnp.clip(cs - start_aligned, 0, tm))
            hi_ref[k + i] = jnp.where(inside, tm, jnp.clip(ce - start_aligned, 0, tm))
            return carry

        n_class_items = jnp.where(inside, n_inside, n_aligned)
        lax.fori_loop(0, n_class_items, emit, 0)
        k = k + n_class_items
    nused_ref[0] = k

    def repeat_last(i, carry):
        for ref in (start_ref, ea_ref, eb_ref, lo_ref, hi_ref):
            ref[i] = ref[k - 1]
        return carry

    lax.fori_loop(k, n_items, repeat_last, 0)

    to_smem.wait()

    def invert(t, carry):
        tok_ref[pos_smem_ref[0, t]] = t
        return carry

    lax.fori_loop(0, n, invert, 0, unroll=INVERT_UNROLL)


def _plan_routing(route, counts, n):
    n_items = n // EXPERT_TILE + N_CLASSES
    smem = lambda size: jax.ShapeDtypeStruct((size,), jnp.int32)
    smem_spec = pl.BlockSpec(memory_space=pltpu.SMEM)
    return pl.pallas_call(
        _plan_kernel,
        in_specs=[smem_spec, pl.BlockSpec(memory_space=pltpu.VMEM)],
        out_specs=[smem_spec] * 7,
        out_shape=[smem(n)] + [smem(n_items)] * 5 + [smem(1)],
        scratch_shapes=[pltpu.SMEM((N_CLASSES,), jnp.int32), pltpu.VMEM((1, n), jnp.int32),
                        pltpu.SMEM((1, n), jnp.int32), pltpu.SemaphoreType.DMA(())],
        name="routing_plan",
    )(counts[0, :N_CLASSES].astype(jnp.int32), route)


def _expert_pair_kernel(start_ref, ea_ref, eb_ref, lo_ref, hi_ref, nused_ref, tok_ref,
                        rows_hbm_ref, gffn_ref, gfin_ref,
                        wga_ref, wua_ref, wda_ref, wgb_ref, wub_ref, wdb_ref, out_hbm_ref,
                        buf_ref, acc_ref, gather_sem, scatter_sem):
    del ea_ref, eb_ref, hi_ref
    tm = EXPERT_TILE
    j = pl.program_id(0)
    last = nused_ref[0] - 1
    slot = j % 2
    other = 1 - slot

    def start_gather(item, dst_slot, rows=range(tm)):
        base = start_ref[item]
        for r in rows:
            _row_copy(rows_hbm_ref, tok_ref[base + r], buf_ref.at[dst_slot], r, gather_sem.at[dst_slot]).start()

    def wait_gather(dst_slot):
        pltpu.make_async_copy(rows_hbm_ref.at[pl.ds(0, tm), :], buf_ref.at[dst_slot],
                              gather_sem.at[dst_slot]).wait()

    def start_scatter(item, src_slot, rows=range(tm)):
        base = start_ref[item]
        for r in rows:
            _row_copy(acc_ref.at[src_slot], r, out_hbm_ref, tok_ref[base + r], scatter_sem).start()

    def wait_scatter():
        pltpu.make_async_copy(acc_ref.at[0], out_hbm_ref.at[pl.ds(0, tm), :], scatter_sem).wait()

    gslot = j % GATHER_SLOTS

    @pl.when(j == 0)
    def _():
        acc_ref[...] = jnp.zeros_like(acc_ref)
        start_gather(0, 0)
        start_gather(jnp.minimum(1, last), 1)
        start_scatter(0, 1)

    @pl.when(j <= last)
    def _():
        wait_scatter()
        wait_gather(gslot)
        prev_item = jnp.maximum(j - 1, 0)
        ahead_item = jnp.minimum(j + 2, last)

        def issue_copies(part):
            half_parts = DMA_PARTS // 2
            rows = range((part % half_parts) * tm // half_parts, (part % half_parts + 1) * tm // half_parts)
            if part < half_parts:
                start_scatter(prev_item, other, rows)
            else:
                start_gather(ahead_item, (j + 2) % GATHER_SLOTS, rows)

        lo = lo_ref[j]
        x1 = buf_ref[gslot, :, :D_MODEL]
        h = _rmsnorm(x1, gffn_ref[...]).astype(BF16)

        part = 0
        ff_chunk = EXPERT_FF * 4 // DMA_PARTS
        for slot_lane, (wg_ref, wu_ref, wd_ref) in enumerate(((wga_ref, wua_ref, wda_ref),
                                                              (wgb_ref, wub_ref, wdb_ref))):
            for c0 in range(0, EXPERT_FF, ff_chunk):
                issue_copies(part)
                wts = buf_ref[gslot, :, D_MODEL:]
                lane = lax.broadcasted_iota(jnp.int32, wts.shape, 1)
                w = jnp.sum(jnp.where(lane == slot_lane, wts, 0.0), axis=-1, keepdims=True)
                a = _dot(h, wg_ref[0, :, c0:c0 + ff_chunk])
                v = _dot(h, wu_ref[0, :, c0:c0 + ff_chunk])
                act = (a * (1.0 / (1.0 + jnp.exp(-a))) * v * w).astype(BF16)
                for half, d0 in enumerate(range(0, D_MODEL, D_MODEL // 2)):
                    cols = slice(d0, d0 + D_MODEL // 2)
                    if half == 1:
                        issue_copies(part + 1)
                    y_part = _dot(act, wd_ref[0, c0:c0 + ff_chunk, cols])
                    if part == 0:
                        acc_ref[slot, :, cols] = y_part
                    else:
                        acc_ref[slot, :, cols] += y_part
                part += 2
        res = _rmsnorm(buf_ref[gslot, :, :D_MODEL] + acc_ref[slot], gfin_ref[...])

        acc_ref[slot] = res

        @pl.when(lo > 0)
        def _():
            row = lax.broadcasted_iota(jnp.int32, (tm, 1), 0)
            acc_ref[slot] = jnp.where(row < lo, acc_ref[other], acc_ref[slot])

    @pl.when(j == last)
    def _():
        wait_scatter()
        start_scatter(j, slot)
        wait_scatter()
        wait_gather((j + 1) % GATHER_SLOTS)
        wait_gather((j + 2) % GATHER_SLOTS)


def _expert_pairs(item_start, item_ea, item_eb, item_lo, item_hi, n_used, slot_token, rows,
                  g_ffn, g_final, wg, wu, wd):
    n, w = rows.shape
    d, f = D_MODEL, EXPERT_FF
    tm = EXPERT_TILE
    gate_a = pl.BlockSpec((1, d, f), lambda j, st, ea, eb, lo, hi, nu, tok: (ea[j], 0, 0))
    gate_b = pl.BlockSpec((1, d, f), lambda j, st, ea, eb, lo, hi, nu, tok: (eb[j], 0, 0))
    down_a = pl.BlockSpec((1, f, d), lambda j, st, ea, eb, lo, hi, nu, tok: (ea[j], 0, 0))
    down_b = pl.BlockSpec((1, f, d), lambda j, st, ea, eb, lo, hi, nu, tok: (eb[j], 0, 0))
    gain = pl.BlockSpec((1, d), lambda j, st, ea, eb, lo, hi, nu, tok: (0, 0))
    grid_spec = pltpu.PrefetchScalarGridSpec(
        num_scalar_prefetch=7,
        grid=(item_start.shape[0],),
        in_specs=[pl.BlockSpec(memory_space=pl.ANY), gain, gain,
                  gate_a, gate_a, down_a, gate_b, gate_b, down_b],
        out_specs=pl.BlockSpec(memory_space=pl.ANY),
        scratch_shapes=[pltpu.VMEM((GATHER_SLOTS, tm, w), F32), pltpu.VMEM((2, tm, d), F32),
                        pltpu.SemaphoreType.DMA((GATHER_SLOTS,)), pltpu.SemaphoreType.DMA(())],
    )
    vmem = 2 * 6 * d * f * 2 + GATHER_SLOTS * tm * w * 4 + 2 * tm * d * 4 + 8 * tm * d * 4
    return pl.pallas_call(
        _expert_pair_kernel,
        grid_spec=grid_spec,
        out_shape=jax.ShapeDtypeStruct((n, d), F32),
        compiler_params=pltpu.CompilerParams(
            dimension_semantics=("arbitrary",), vmem_limit_bytes=vmem + 4 * MIB),
        name="expert_pairs",
    )(item_start, item_ea, item_eb, item_lo, item_hi, n_used, slot_token, rows, g_ffn.reshape(1, d),
      g_final.reshape(1, d), wg, wu, wd, wg, wu, wd)


def kernel(x, g_mix, w_in, w_pool, pool_scale, w_fourier, w_out, g_ffn, w_group_router,
           b_group_router, w_expert_router, b_expert_router, w_gate, w_up, w_down, g_final):
    b, s, d = x.shape
    assert d == D_MODEL and s % (2 * SEQ_TILE) == 0 and s % (2 * TOKEN_TILE) == 0
    assert (b * s) % max(TOKEN_TILE, EXPERT_TILE) == 0
    n = b * s
    x2 = x.reshape(n, d)

    cw, sw, w_top_bf16 = _mixer_weights(w_fourier, w_pool, pool_scale, w_out, s)
    u, (w_out_bf16, w_gate_bf16) = _norm_proj(x2, g_mix, w_in, [w_out, w_gate])
    mixed_lo, mixed_hi, w_up_bf16 = _mix(u.reshape(b, s, d), cw, sw, w_up)

    wr = jnp.concatenate([w_group_router, w_expert_router], axis=1)
    wr = jnp.pad(wr, ((0, 0), (0, ROUTER_LANES - wr.shape[1]))).astype(BF16)
    br = jnp.concatenate([b_group_router, b_expert_router])
    br = jnp.pad(br, (0, ROUTER_LANES - br.shape[0])).reshape(1, ROUTER_LANES)

    rows, route, counts, w_down_bf16 = _out_proj_route(
        x2, mixed_lo.reshape(n // 2, d), mixed_hi.reshape(n // 2, d), s, w_top_bf16, w_out_bf16, g_ffn, wr, br,
        w_down)
    slot_token, item_start, item_ea, item_eb, item_lo, item_hi, n_used = _plan_routing(route, counts, n)
    out = _expert_pairs(item_start, item_ea, item_eb, item_lo, item_hi, n_used, slot_token, rows,
                        g_ffn, g_final, w_gate_bf16, w_up_bf16, w_down_bf16)
    return out.reshape(b, s, d)
```

```python
import functools

import numpy as np
import jax
import jax.numpy as jnp
from jax import lax
from jax.experimental import pallas as pl
from jax.experimental.pallas import tpu as pltpu

D_MODEL = 2048
POOL_WINDOWS = (2, 4, 8, 16)
N_POOL_GROUPS = len(POOL_WINDOWS)
POOL_WIDTH = D_MODEL // 2
POOL_GROUP_DIM = POOL_WIDTH // N_POOL_GROUPS
FOURIER_WIDTH = D_MODEL - POOL_WIDTH
N_FOURIER_HEADS = 4
FOURIER_HEAD_DIM = FOURIER_WIDTH // N_FOURIER_HEADS
N_EXPERT_GROUPS = 4
EXPERTS_PER_GROUP = 4
EXPERT_FF = D_MODEL // 4
RMS_EPS = 1e-6

LANES = 128
SUBLANES = 8
BF16_SUBLANES = 16
ROUTER_LANES = LANES
MIB = 1024 * 1024

TOKEN_TILE = 512
SEQ_TILE = 256
POOL_HALO = BF16_SUBLANES
DFT_ROWS = SEQ_TILE + BF16_SUBLANES
TWIDDLE_ROWS = 16
FOLD_TILES = 8

PAIR_SLOT_A = (0, 0, 0, 1, 1, 3)
PAIR_SLOT_B = (1, 2, 3, 3, 2, 2)
PAIRS_PER_GROUP = len(PAIR_SLOT_A)
N_CLASSES = N_EXPERT_GROUPS * PAIRS_PER_GROUP
ROW_WORDS = D_MODEL + LANES
RANK_RADIX = 128
EXPERT_TILE = 256
DMA_PARTS = 8
GATHER_SLOTS = 3
INVERT_UNROLL = 32
WEIGHT_CAST_ROWS = 256

BF16 = jnp.bfloat16
F32 = jnp.float32


def _rmsnorm(x, g):
    ms = jnp.mean(x * x, axis=-1, keepdims=True)
    return x * lax.rsqrt(ms + RMS_EPS) * g


def _dot(a, b):
    return jnp.dot(a, b, preferred_element_type=F32)


def _twiddle(rows, cols, period):
    m = (np.asarray(rows, np.int64)[:, None] * np.asarray(cols, np.int64)[None, :]) % period
    ang = (2.0 * np.pi / period) * m.astype(np.float64)
    return np.cos(ang).astype(np.float32), np.sin(ang).astype(np.float32)


def _pool_band(seq_len, tile, halo):
    n_tiles = seq_len // tile
    out = np.zeros((3, N_POOL_GROUPS, tile, tile + 2 * halo), np.float64)
    for v, m in enumerate((0, 1, n_tiles - 1)):
        t0 = m * tile
        for g, k in enumerate(POOL_WINDOWS):
            for r in range(tile):
                t = t0 + r
                lo = max(t - (k - 1) // 2, 0)
                hi = min(t + k // 2 + 1, seq_len)
                out[v, g, r, lo - t0 + halo:hi - t0 + halo] = 1.0 / (hi - lo)
                out[v, g, r, r + halo] -= 1.0
    return out.astype(np.float32)


def _mixer_weight_kernel(cd_ref, sd_ref, wf_ref, wp_ref, ps_ref, wo_ref, cw_ref, sw_ref, wtop_ref, *, scale):
    wf = wf_ref[0]
    cw = jnp.dot(cd_ref[...], wf, preferred_element_type=F32, precision=lax.Precision.HIGHEST)
    sw = jnp.dot(sd_ref[...], wf, preferred_element_type=F32, precision=lax.Precision.HIGHEST)
    cw_ref[0] = (cw * scale).astype(BF16)
    sw_ref[0] = (sw * (-scale)).astype(BF16)
    wtop_ref[...] = _dot((wp_ref[0] * ps_ref[0]).astype(BF16), wo_ref[...].astype(BF16)).astype(BF16)


def _mixer_weights(w_fourier, w_pool, pool_scale, w_out, seq_len):
    dh = FOURIER_HEAD_DIM
    assert N_FOURIER_HEADS == N_POOL_GROUPS and dh == POOL_GROUP_DIM
    d = w_out.shape[1]
    cd, sd = _twiddle(np.arange(dh), np.arange(dh), dh)
    scale = 1.0 / np.sqrt(float(seq_len * dh))
    mat = pl.BlockSpec((dh, dh), lambda h: (0, 0))
    per_head = pl.BlockSpec((1, dh, dh), lambda h: (h, 0, 0))
    out_rows = pl.BlockSpec((dh, d), lambda h: (h, 0))
    return pl.pallas_call(
        functools.partial(_mixer_weight_kernel, scale=scale),
        grid=(N_FOURIER_HEADS,),
        in_specs=[mat, mat, per_head, per_head, pl.BlockSpec((1, 1, dh), lambda h: (h, 0, 0)), out_rows],
        out_specs=[per_head, per_head, out_rows],
        out_shape=[jax.ShapeDtypeStruct((N_FOURIER_HEADS, dh, dh), BF16)] * 2
        + [jax.ShapeDtypeStruct((POOL_WIDTH, d), BF16)],
        name="mixer_weights",
    )(jnp.asarray(cd), jnp.asarray(sd), w_fourier, w_pool, pool_scale.reshape(N_POOL_GROUPS, 1, dh), w_out)


class _CastAlong:
    def __init__(self, w, n_chunks, chunk_of):
        cols = w.shape[-1]
        rows = w.size // cols // n_chunks
        self.shape = w.shape
        self.src = w.reshape(n_chunks, rows, cols)
        self.spec = pl.BlockSpec((1, rows, cols), lambda *idx: (chunk_of(*idx), 0, 0))
        self.out_shape = jax.ShapeDtypeStruct((n_chunks, rows, cols), BF16)
        self.vmem_bytes = 2 * rows * cols * (4 + 2)


def _cast_chunks(src_refs, dst_refs):
    for src_ref, dst_ref in zip(src_refs, dst_refs):
        dst_ref[...] = src_ref[...].astype(BF16)


def _norm_proj_kernel(x_ref, g_ref, w_ref, *rest, n_cast):
    cast_src, (u_ref, *cast_dst), w_bf16_ref = rest[:n_cast], rest[n_cast:-1], rest[-1]

    @pl.when(pl.program_id(0) == 0)
    def _():
        for r0 in range(0, w_ref.shape[0], WEIGHT_CAST_ROWS):
            w_bf16_ref[r0:r0 + WEIGHT_CAST_ROWS, :] = w_ref[r0:r0 + WEIGHT_CAST_ROWS, :].astype(BF16)

    x = x_ref[...]
    inv_rms = lax.rsqrt(jnp.mean(x * x, axis=-1, keepdims=True) + RMS_EPS)
    u_ref[...] = (_dot((x * g_ref[...]).astype(BF16), w_bf16_ref[...]) * inv_rms).astype(BF16)
    _cast_chunks(cast_src, cast_dst)


def _norm_proj(x2, g_mix, w_in, cast_weights):
    n, d = x2.shape
    tm = TOKEN_TILE
    casts = [_CastAlong(w, n // tm, lambda i: i) for w in cast_weights]
    vmem = (2 * tm * d * 4 + d * d * (4 + 2) + 2 * tm * d * 2 + 3 * tm * d * 4
            + sum(c.vmem_bytes for c in casts))
    u, *cast_out = pl.pallas_call(
        functools.partial(_norm_proj_kernel, n_cast=len(casts)),
        grid=(n // tm,),
        in_specs=[
            pl.BlockSpec((tm, d), lambda i: (i, 0)),
            pl.BlockSpec((1, d), lambda i: (0, 0)),
            pl.BlockSpec((d, d), lambda i: (0, 0), pipeline_mode=pl.Buffered(1)),
        ] + [c.spec for c in casts],
        out_specs=[pl.BlockSpec((tm, d), lambda i: (i, 0))] + [c.spec for c in casts],
        out_shape=[jax.ShapeDtypeStruct((n, d), BF16)] + [c.out_shape for c in casts],
        scratch_shapes=[pltpu.VMEM((d, d), BF16)],
        compiler_params=pltpu.CompilerParams(
            dimension_semantics=("arbitrary",), vmem_limit_bytes=vmem + 4 * MIB),
        name="norm_proj",
    )(x2, g_mix.reshape(1, d), w_in, *[c.src for c in casts])
    return u, [o.reshape(c.shape) for o, c in zip(cast_out, casts)]


def _fold_kernel(blk_ref, mirror_ref, after_ref, rev_ref, cw_ref, sw_ref, even_ref, odd_ref):
    t = SEQ_TILE
    halo = POOL_HALO
    hd = FOURIER_HEAD_DIM
    for i in range(FOLD_TILES):
        top = (FOLD_TILES - 1 - i) * t
        if i == 0:
            after = jnp.where(pl.program_id(1) > 0, after_ref[...], jnp.zeros_like(after_ref))
        else:
            after = mirror_ref[top + t:top + t + halo, :]
        window = jnp.concatenate([mirror_ref[top:top + t, :], after], axis=0)
        mirrored = _dot(rev_ref[...], window)
        blk = blk_ref[i * t:(i + 1) * t, :].astype(F32)
        even_ref[i * t:(i + 1) * t, :] = (blk + mirrored).astype(BF16)
        odd_ref[i * t:(i + 1) * t, :] = (blk - mirrored).astype(BF16)
    for h in range(N_FOURIER_HEADS):
        cols = slice(h * hd, (h + 1) * hd)
        even_ref[:, cols] = _dot(even_ref[:, cols], cw_ref[h]).astype(BF16)
        odd_ref[:, cols] = _dot(odd_ref[:, cols], sw_ref[h]).astype(BF16)


def _fold_sequence(u3, rev, cw, sw):
    b, s, _ = u3.shape
    halo = POOL_HALO
    rows = FOLD_TILES * SEQ_TILE
    n_blocks = s // rows
    block = lambda block_of: pl.BlockSpec((None, rows, FOURIER_WIDTH), lambda bi, j: (bi, block_of(j), 1))
    after = pl.BlockSpec((None, halo, FOURIER_WIDTH),
                         lambda bi, j: (bi, jnp.minimum((n_blocks - j) * (rows // halo), s // halo - 1), 1))
    out = pl.BlockSpec((None, rows, FOURIER_WIDTH), lambda bi, j: (bi, j, 0))
    return pl.pallas_call(
        _fold_kernel,
        grid=(b, n_blocks // 2),
        in_specs=[block(lambda j: j), block(lambda j: n_blocks - 1 - j), after,
                  pl.BlockSpec(rev.shape, lambda bi, j: (0, 0)),
                  pl.BlockSpec(cw.shape, lambda bi, j: (0, 0, 0)), pl.BlockSpec(sw.shape, lambda bi, j: (0, 0, 0))],
        out_specs=[out, out],
        out_shape=[jax.ShapeDtypeStruct((b, s // 2, FOURIER_WIDTH), BF16)] * 2,
        compiler_params=pltpu.CompilerParams(
            dimension_semantics=("arbitrary", "arbitrary"),
            vmem_limit_bytes=2 * 4 * rows * FOURIER_WIDTH * 2 + 8 * MIB),
        name="fold_sequence",
    )(u3, u3, u3, rev, cw, sw)


def _mix_kernel(even_ref, odd_ref, umid_ref, up_lo_ref, prev_lo_ref, next_lo_ref, up_hi_ref, prev_hi_ref, next_hi_ref,
                c0_ref, s0_ref, cph_ref, sph_ref, band_lo_ref, band_hi_ref, rev_ref,
                cw_ref, cast_src_ref, lo_ref, hi_ref, cast_dst_ref, lhs_ref):
    t = SEQ_TILE
    tp = DFT_ROWS
    gd = POOL_GROUP_DIM
    hd = FOURIER_HEAD_DIM

    m = pl.program_id(1)

    @pl.when(pl.program_id(0) == 0)
    def _():
        cph = cph_ref[0]
        sph = sph_ref[0]
        for r0 in range(0, tp, TWIDDLE_ROWS):
            c0 = c0_ref[r0:r0 + TWIDDLE_ROWS, :]
            s0 = s0_ref[r0:r0 + TWIDDLE_ROWS, :]
            lhs_ref[m, r0:r0 + TWIDDLE_ROWS, :] = (cph * c0 - sph * s0).astype(BF16)
            lhs_ref[m, tp + r0:tp + r0 + TWIDDLE_ROWS, :] = (sph * c0 + cph * s0).astype(BF16)

    _cast_chunks([cast_src_ref], [cast_dst_ref])
    row = lax.broadcasted_iota(jnp.int32, (tp, 1), 0)
    sign = jnp.where(row % 2 == 0, 1.0, -1.0)
    mid = jnp.concatenate([_dot(umid_ref[:, h * hd:(h + 1) * hd], cw_ref[h]) for h in range(N_FOURIER_HEADS)],
                          axis=1)
    pc = _dot(lhs_ref[m, :tp, :], even_ref[...]) + sign * mid[0:1, :]
    qs = _dot(lhs_ref[m, tp:, :], odd_ref[...])
    lo_ref[:, POOL_WIDTH:] = (pc + qs)[:t].astype(BF16)
    hi_ref[:, POOL_WIDTH:] = _dot(rev_ref[...], (pc - qs).astype(BF16)).astype(BF16)

    def pool(up_ref, prev_ref, next_ref, band_ref, out_ref):
        win = jnp.concatenate([prev_ref[...], up_ref[...], next_ref[...]], axis=0)
        for g in range(N_POOL_GROUPS):
            cols = slice(g * gd, (g + 1) * gd)
            out_ref[:, cols] = _dot(band_ref[0, g], win[:, cols]).astype(BF16)

    pool(up_lo_ref, prev_lo_ref, next_lo_ref, band_lo_ref, lo_ref)
    pool(up_hi_ref, prev_hi_ref, next_hi_ref, band_hi_ref, hi_ref)


def _mix(u3, cw, sw, cast_weight):
    b, s, d = u3.shape
    t = SEQ_TILE
    tp = DFT_ROWS
    halo = POOL_HALO
    n_tiles = s // t
    n_steps = n_tiles // 2
    halo_blocks_per_tile = t // halo
    last_halo_block = s // halo - 1

    sh = s // 2
    c0, s0 = _twiddle(np.arange(tp), np.arange(sh), s)
    cph, sph = _twiddle(np.arange(0, sh, t), np.arange(sh), s)
    band = jnp.asarray(_pool_band(s, t, halo)).astype(BF16)
    rev = np.zeros((t, tp), np.float32)
    rev[np.arange(t), t - np.arange(t)] = 1.0
    rev = jnp.asarray(rev).astype(BF16)
    even, odd = _fold_sequence(u3, rev, cw, sw)

    hi_tile = lambda m: n_tiles - 1 - m
    tile_spec = lambda tile_of: pl.BlockSpec((None, t, POOL_WIDTH), lambda bi, m: (bi, tile_of(m), 0))
    prev_spec = lambda tile_of: pl.BlockSpec(
        (None, halo, POOL_WIDTH),
        lambda bi, m: (bi, jnp.maximum(tile_of(m) * halo_blocks_per_tile - 1, 0), 0))
    next_spec = lambda tile_of: pl.BlockSpec(
        (None, halo, POOL_WIDTH),
        lambda bi, m: (bi, jnp.minimum((tile_of(m) + 1) * halo_blocks_per_tile, last_halo_block), 0))
    band_shape = (1, N_POOL_GROUPS, t, t + 2 * halo)
    const2 = lambda shape: pl.BlockSpec(shape, lambda bi, m: (0, 0))
    const3 = lambda shape: pl.BlockSpec(shape, lambda bi, m: (0, 0, 0))
    lo_tile = lambda m: m

    in_specs = [
        pl.BlockSpec((None, sh, FOURIER_WIDTH), lambda bi, m: (bi, 0, 0)),
        pl.BlockSpec((None, sh, FOURIER_WIDTH), lambda bi, m: (bi, 0, 0)),
        pl.BlockSpec((None, halo, FOURIER_WIDTH), lambda bi, m: (bi, sh // halo, 1)),
        tile_spec(lo_tile), prev_spec(lo_tile), next_spec(lo_tile),
        tile_spec(hi_tile), prev_spec(hi_tile), next_spec(hi_tile),
        pl.BlockSpec((tp, sh), lambda bi, m: (0, 0), pipeline_mode=pl.Buffered(1)),
        pl.BlockSpec((tp, sh), lambda bi, m: (0, 0), pipeline_mode=pl.Buffered(1)),
        pl.BlockSpec((1, 1, sh), lambda bi, m: (m, 0, 0)),
        pl.BlockSpec((1, 1, sh), lambda bi, m: (m, 0, 0)),
        pl.BlockSpec(band_shape, lambda bi, m: (jnp.where(m == 0, 0, 1), 0, 0, 0)),
        pl.BlockSpec(band_shape, lambda bi, m: (jnp.where(m == 0, 2, 1), 0, 0, 0)),
        const2((t, tp)),
        const3((N_FOURIER_HEADS, FOURIER_HEAD_DIM, FOURIER_HEAD_DIM)),
    ]
    half_out = pl.BlockSpec((None, t, d), lambda bi, m: (bi, m, 0))
    hi_out = pl.BlockSpec((None, t, d), lambda bi, m: (bi, n_steps - 1 - m, 0))
    cast = _CastAlong(cast_weight, n_steps * b, lambda bi, m: m * b + bi)
    vmem = (2 * 2 * sh * FOURIER_WIDTH * 2
            + 2 * tp * sh * 4
            + n_steps * 2 * tp * sh * 2
            + 2 * 2 * t * d * 2 * 2
            + 4 * 2 * tp * FOURIER_WIDTH * 4
            + cast.vmem_bytes)
    lo, hi, cast_out = pl.pallas_call(
        _mix_kernel,
        grid=(b, n_steps),
        in_specs=in_specs + [cast.spec],
        out_specs=[half_out, hi_out, cast.spec],
        out_shape=[jax.ShapeDtypeStruct((b, s // 2, d), BF16)] * 2 + [cast.out_shape],
        scratch_shapes=[pltpu.VMEM((n_steps, 2 * tp, sh), BF16)],
        compiler_params=pltpu.CompilerParams(
            dimension_semantics=("arbitrary", "arbitrary"), vmem_limit_bytes=vmem + 4 * MIB),
        name="seq_mix",
    )(even, odd, u3, u3, u3, u3, u3, u3, u3, jnp.asarray(c0), jnp.asarray(s0),
      jnp.asarray(cph).reshape(n_steps, 1, sh), jnp.asarray(sph).reshape(n_steps, 1, sh),
      band, band, rev,
      cw, cast.src)
    return lo, hi, cast_out.reshape(cast.shape)


def _out_proj_route_kernel(x_ref, mixed_lo_ref, mixed_hi_ref, wtop_ref, wbot_ref, g_ref, wr_ref, br_ref,
                           tri_ref, pick_ref, cast_src_ref, rows_ref, route_ref, counts_ref, cast_dst_ref,
                           carry_ref, x1_ref, *, tiles_per_seq, n_tiles):
    step = pl.program_id(0)

    @pl.when(step == 0)
    def _():
        carry_ref[...] = jnp.zeros_like(carry_ref)
        x1_ref[...] = jnp.zeros_like(x1_ref)

    _cast_chunks([cast_src_ref], [cast_dst_ref])

    x1 = x1_ref[...]
    rows_ref[:, :D_MODEL] = x1
    h2 = _rmsnorm(x1, g_ref[...])
    logits = _dot(h2.astype(BF16), wr_ref[...]) + br_ref[...]

    tile = jnp.minimum(step, n_tiles - 1)
    in_lo_half = (tile % tiles_per_seq) < tiles_per_seq // 2
    mixed = jnp.where(in_lo_half, mixed_lo_ref[...], mixed_hi_ref[...])
    x1_ref[...] = (x_ref[...] + _dot(mixed[:, :POOL_WIDTH], wtop_ref[...])
                   + _dot(mixed[:, POOL_WIDTH:], wbot_ref[...]))

    lane = lax.broadcasted_iota(jnp.int32, logits.shape, 1)
    neg = jnp.float32(-jnp.inf)
    big = jnp.int32(ROUTER_LANES)

    is_group = lane < N_EXPERT_GROUPS
    gl = jnp.where(is_group, logits, neg)
    gmax = jnp.max(gl, axis=-1, keepdims=True)
    gidx = jnp.min(jnp.where(gl == gmax, lane, big), axis=-1, keepdims=True)
    p_g = 1.0 / jnp.sum(jnp.exp(gl - gmax), axis=-1, keepdims=True)

    e_lane = lane - N_EXPERT_GROUPS
    in_group = (e_lane >= gidx * EXPERTS_PER_GROUP) & (e_lane < (gidx + 1) * EXPERTS_PER_GROUP)
    el = jnp.where(in_group, logits, neg)
    v1 = jnp.max(el, axis=-1, keepdims=True)
    i1 = jnp.min(jnp.where(el == v1, lane, big), axis=-1, keepdims=True)
    el2 = jnp.where(lane == i1, neg, el)
    v2 = jnp.max(el2, axis=-1, keepdims=True)
    i2 = jnp.min(jnp.where(el2 == v2, lane, big), axis=-1, keepdims=True)
    r = jnp.exp(v2 - v1)
    w1 = p_g / (1.0 + r)
    w2 = p_g * r / (1.0 + r)

    first_is_low = i1 < i2
    first_lane = N_EXPERT_GROUPS + gidx * EXPERTS_PER_GROUP
    la = jnp.where(first_is_low, i1, i2) - first_lane
    lb = jnp.where(first_is_low, i2, i1) - first_lane
    w_low = jnp.where(first_is_low, w1, w2)
    w_high = jnp.where(first_is_low, w2, w1)
    assert (PAIR_SLOT_A, PAIR_SLOT_B) == ((0, 0, 0, 1, 1, 3), (1, 2, 3, 3, 2, 2))
    pair = jnp.where(la == 0, lb - 1, jnp.where(la == 1, 6 - lb, 5))
    slot_a_is_high = la == 2
    w_a = jnp.where(slot_a_is_high, w_high, w_low)
    w_b = jnp.where(slot_a_is_high, w_low, w_high)
    cls = gidx * PAIRS_PER_GROUP + pair
    rows_ref[:, D_MODEL:] = jnp.where(lane == 0, w_a, jnp.where(lane == 1, w_b, 0.0))

    onehot = jnp.where(lane == cls, 1.0, 0.0)
    before = _dot(tri_ref[...], onehot.astype(BF16)) + carry_ref[...]
    rank = jnp.sum(jnp.where(lane == cls, before, 0.0), axis=-1, keepdims=True)
    carry_ref[...] += jnp.sum(onehot, axis=0, keepdims=True) * jnp.where(step > 0, 1.0, 0.0)
    counts_ref[...] = carry_ref[...]

    rank_hi = jnp.floor(rank * (1.0 / RANK_RADIX))
    rank_lo = rank - rank_hi * RANK_RADIX
    digits = jnp.where(lane == 0, cls.astype(F32),
                       jnp.where(lane == 1, rank_hi, jnp.where(lane == 2, rank_lo, 0.0)))
    route_ref[...] = lax.dot_general(pick_ref[...], digits.astype(BF16), (((1,), (1,)), ((), ())),
                                     preferred_element_type=F32)


def _out_proj_route(x2, mixed_lo, mixed_hi, seq_len, w_top_bf16, w_out_bf16, g_ffn, wr, br, cast_weight):
    n, d = x2.shape
    tm = TOKEN_TILE
    n_tiles = n // tm
    tiles_per_seq = seq_len // tm
    half_tiles = tiles_per_seq // 2
    const = lambda shape: pl.BlockSpec(shape, lambda i: (0, 0))
    in_tile = lambda i: jnp.minimum(i, n_tiles - 1)
    out_tile = lambda i: jnp.maximum(i - 1, 0)
    lo_spec = pl.BlockSpec((tm, d), lambda i: (
        (in_tile(i) // tiles_per_seq) * half_tiles + jnp.minimum(in_tile(i) % tiles_per_seq, half_tiles - 1), 0))
    hi_spec = pl.BlockSpec((tm, d), lambda i: (
        (in_tile(i) // tiles_per_seq) * half_tiles + jnp.maximum(in_tile(i) % tiles_per_seq - half_tiles, 0), 0))
    tri = np.tril(np.ones((tm, tm), np.float32), -1)
    pick = np.eye(SUBLANES, ROUTER_LANES, dtype=np.float32)
    vmem = (2 * tm * d * 4 + 4 * tm * d * 2 + 2 * tm * ROW_WORDS * 4 + d * d * 2 + tm * d * 4
            + 2 * d * ROUTER_LANES * 2 + 2 * tm * tm * 2 + 3 * tm * d * 4)
    cast = _CastAlong(cast_weight, n_tiles, in_tile)
    vmem += cast.vmem_bytes
    rows, route, counts, cast_out = pl.pallas_call(
        functools.partial(_out_proj_route_kernel, tiles_per_seq=tiles_per_seq, n_tiles=n_tiles),
        grid=(n_tiles + 1,),
        in_specs=[pl.BlockSpec((tm, d), lambda i: (in_tile(i), 0)), lo_spec, hi_spec,
                  pl.BlockSpec((POOL_WIDTH, d), lambda i: (0, 0), pipeline_mode=pl.Buffered(1)),
                  pl.BlockSpec((FOURIER_WIDTH, d), lambda i: (1, 0), pipeline_mode=pl.Buffered(1)),
                  const((1, d)), const((d, ROUTER_LANES)), const((1, ROUTER_LANES)),
                  const((tm, tm)), const((SUBLANES, ROUTER_LANES)), cast.spec],
        out_specs=[pl.BlockSpec((tm, ROW_WORDS), lambda i: (out_tile(i), 0)),
                   pl.BlockSpec((SUBLANES, tm), lambda i: (0, out_tile(i))),
                   const((1, ROUTER_LANES)), cast.spec],
        out_shape=[jax.ShapeDtypeStruct((n, ROW_WORDS), F32),
                   jax.ShapeDtypeStruct((SUBLANES, n), F32),
                   jax.ShapeDtypeStruct((1, ROUTER_LANES), F32), cast.out_shape],
        scratch_shapes=[pltpu.VMEM((1, ROUTER_LANES), F32), pltpu.VMEM((tm, d), F32)],
        compiler_params=pltpu.CompilerParams(
            dimension_semantics=("arbitrary",), vmem_limit_bytes=vmem + 4 * MIB),
        name="out_proj_route",
    )(x2, mixed_lo, mixed_hi, w_top_bf16, w_out_bf16, g_ffn.reshape(1, d), wr, br,
      jnp.asarray(tri).astype(BF16), jnp.asarray(pick).astype(BF16), cast.src)
    return rows, route, counts, cast_out.reshape(cast.shape)


def _row_copy(src_ref, src_row, dst_ref, dst_row, sem):
    return pltpu.make_async_copy(src_ref.at[pl.ds(src_row, 1), :], dst_ref.at[pl.ds(dst_row, 1), :], sem)


def _plan_kernel(cnt_ref, route_ref, tok_ref, start_ref, ea_ref, eb_ref, lo_ref, hi_ref, nused_ref,
                 cstart_ref, pos_vmem_ref, pos_smem_ref, sem):
    tm = EXPERT_TILE
    shift = tm.bit_length() - 1
    n = tok_ref.shape[0]
    n_items = start_ref.shape[0]

    run = jnp.int32(0)
    inside = jnp.bool_(True)
    for c in range(N_CLASSES):
        cnt = cnt_ref[c]
        cstart_ref[c] = run
        inside = inside & ((cnt == 0) | (cnt >= tm))
        run = run + cnt

    cls_row = route_ref[0:1, :]
    slot_row = route_ref[1:2, :] * RANK_RADIX + route_ref[2:3, :]
    for c in range(N_CLASSES):
        slot_row = slot_row + jnp.where(cls_row == c, cstart_ref[c].astype(F32), 0.0)
    pos_vmem_ref[...] = slot_row.astype(jnp.int32)
    to_smem = pltpu.make_async_copy(pos_vmem_ref, pos_smem_ref, sem)
    to_smem.start()

    k = jnp.int32(0)
    for c in range(N_CLASSES):
        cnt = cnt_ref[c]
        cs = cstart_ref[c]
        ce = cs + cnt
        group, pair = divmod(c, PAIRS_PER_GROUP)
        ea = group * EXPERTS_PER_GROUP + PAIR_SLOT_A[pair]
        eb = group * EXPERTS_PER_GROUP + PAIR_SLOT_B[pair]
        first_window = cs >> shift
        n_inside = (cnt + tm - 1) >> shift
        n_aligned = jnp.where(cnt > 0, ((ce - 1) >> shift) - first_window + 1, 0)
        full_windows = cnt >> shift

        def emit(i, carry, k=k, cs=cs, ce=ce, ea=ea, eb=eb, first_window=first_window,
                 full_windows=full_windows):
            start_inside = jnp.where(i < full_windows, cs + i * tm, ce - tm)
            start_aligned = (first_window + i) * tm
            start_ref[k + i] = jnp.where(inside, start_inside, start_aligned)
            ea_ref[k + i] = ea
            eb_ref[k + i] = eb
            lo_ref[k + i] = jnp.where(inside, 0, jnp.clip(cs - start_aligned, 0, tm))
            hi_ref[k + i] = jnp.where(inside, tm, jnp.clip(ce - start_aligned, 0, tm))
            return carry

        n_class_items = jnp.where(inside, n_inside, n_aligned)
        lax.fori_loop(0, n_class_items, emit, 0)
        k = k + n_class_items
    nused_ref[0] = k

    def repeat_last(i, carry):
        for ref in (start_ref, ea_ref, eb_ref, lo_ref, hi_ref):
            ref[i] = ref[k - 1]
        return carry

    lax.fori_loop(k, n_items, repeat_last, 0)

    to_smem.wait()

    def invert(t, carry):
        tok_ref[pos_smem_ref[0, t]] = t
        return carry

    lax.fori_loop(0, n, invert, 0, unroll=INVERT_UNROLL)


def _plan_routing(route, counts, n):
    n_items = n // EXPERT_TILE + N_CLASSES
    smem = lambda size: jax.ShapeDtypeStruct((size,), jnp.int32)
    smem_spec = pl.BlockSpec(memory_space=pltpu.SMEM)
    return pl.pallas_call(
        _plan_kernel,
        in_specs=[smem_spec, pl.BlockSpec(memory_space=pltpu.VMEM)],
        out_specs=[smem_spec] * 7,
        out_shape=[smem(n)] + [smem(n_items)] * 5 + [smem(1)],
        scratch_shapes=[pltpu.SMEM((N_CLASSES,), jnp.int32), pltpu.VMEM((1, n), jnp.int32),
                        pltpu.SMEM((1, n), jnp.int32), pltpu.SemaphoreType.DMA(())],
        name="routing_plan",
    )(counts[0, :N_CLASSES].astype(jnp.int32), route)


def _expert_pair_kernel(start_ref, ea_ref, eb_ref, lo_ref, hi_ref, nused_ref, tok_ref,
                        rows_hbm_ref, gffn_ref, gfin_ref,
                        wga_ref, wua_ref, wda_ref, wgb_ref, wub_ref, wdb_ref, out_hbm_ref,
                        buf_ref, acc_ref, gather_sem, scatter_sem):
    del ea_ref, eb_ref
    tm = EXPERT_TILE
    j = pl.program_id(0)
    last = nused_ref[0] - 1
    slot = j % 2
    other = 1 - slot

    def start_gather(item, dst_slot, rows=range(tm)):
        base = start_ref[item]
        for r in rows:
            _row_copy(rows_hbm_ref, tok_ref[base + r], buf_ref.at[dst_slot], r, gather_sem.at[dst_slot]).start()

    def wait_gather(dst_slot):
        pltpu.make_async_copy(rows_hbm_ref.at[pl.ds(0, tm), :], buf_ref.at[dst_slot],
                              gather_sem.at[dst_slot]).wait()

    def start_scatter(item, src_slot, rows=range(tm)):
        base = start_ref[item]
        for r in rows:
            _row_copy(acc_ref.at[src_slot], r, out_hbm_ref, tok_ref[base + r], scatter_sem).start()

    def wait_scatter():
        pltpu.make_async_copy(acc_ref.at[0], out_hbm_ref.at[pl.ds(0, tm), :], scatter_sem).wait()

    gslot = j % GATHER_SLOTS

    @pl.when(j == 0)
    def _():
        acc_ref[...] = jnp.zeros_like(acc_ref)
        start_gather(0, 0)
        start_gather(jnp.minimum(1, last), 1)
        start_scatter(0, 1)

    @pl.when(j <= last)
    def _():
        wait_scatter()
        wait_gather(gslot)
        prev_item = jnp.maximum(j - 1, 0)
        ahead_item = jnp.minimum(j + 2, last)

        def issue_copies(part):
            half_parts = DMA_PARTS // 2
            rows = range((part % half_parts) * tm // half_parts, (part % half_parts + 1) * tm // half_parts)
            if part < half_parts:
                start_scatter(prev_item, other, rows)
            else:
                start_gather(ahead_item, (j + 2) % GATHER_SLOTS, rows)

        lo = lo_ref[j]
        hi = hi_ref[j]
        x1 = buf_ref[gslot, :, :D_MODEL]
        h = _rmsnorm(x1, gffn_ref[...]).astype(BF16)

        part = 0
        ff_chunk = EXPERT_FF * 4 // DMA_PARTS
        for slot_lane, (wg_ref, wu_ref, wd_ref) in enumerate(((wga_ref, wua_ref, wda_ref),
                                                              (wgb_ref, wub_ref, wdb_ref))):
            for c0 in range(0, EXPERT_FF, ff_chunk):
                issue_copies(part)
                wts = buf_ref[gslot, :, D_MODEL:]
                lane = lax.broadcasted_iota(jnp.int32, wts.shape, 1)
                w = jnp.sum(jnp.where(lane == slot_lane, wts, 0.0), axis=-1, keepdims=True)
                a = _dot(h, wg_ref[0, :, c0:c0 + ff_chunk])
                v = _dot(h, wu_ref[0, :, c0:c0 + ff_chunk])
                act = (a * (1.0 / (1.0 + jnp.exp(-a))) * v * w).astype(BF16)
                for half, d0 in enumerate(range(0, D_MODEL, D_MODEL // 2)):
                    cols = slice(d0, d0 + D_MODEL // 2)
                    if half == 1:
                        issue_copies(part + 1)
                    y_part = _dot(act, wd_ref[0, c0:c0 + ff_chunk, cols])
                    if part == 0:
                        acc_ref[slot, :, cols] = y_part
                    else:
                        acc_ref[slot, :, cols] += y_part
                part += 2
        res = _rmsnorm(buf_ref[gslot, :, :D_MODEL] + acc_ref[slot], gfin_ref[...])

        row = lax.broadcasted_iota(jnp.int32, (tm, 1), 0)
        mine = (row >= lo) & (row < hi)
        acc_ref[slot] = jnp.where(mine, res, 0.0)

        @pl.when(lo > 0)
        def _():
            acc_ref[slot] = jnp.where(row < lo, acc_ref[other], acc_ref[slot])

    @pl.when(j == last)
    def _():
        wait_scatter()
        start_scatter(j, slot)
        wait_scatter()
        wait_gather((j + 1) % GATHER_SLOTS)
        wait_gather((j + 2) % GATHER_SLOTS)


def _expert_pairs(item_start, item_ea, item_eb, item_lo, item_hi, n_used, slot_token, rows,
                  g_ffn, g_final, wg, wu, wd):
    n, w = rows.shape
    d, f = D_MODEL, EXPERT_FF
    tm = EXPERT_TILE
    gate_a = pl.BlockSpec((1, d, f), lambda j, st, ea, eb, lo, hi, nu, tok: (ea[j], 0, 0))
    gate_b = pl.BlockSpec((1, d, f), lambda j, st, ea, eb, lo, hi, nu, tok: (eb[j], 0, 0))
    down_a = pl.BlockSpec((1, f, d), lambda j, st, ea, eb, lo, hi, nu, tok: (ea[j], 0, 0))
    down_b = pl.BlockSpec((1, f, d), lambda j, st, ea, eb, lo, hi, nu, tok: (eb[j], 0, 0))
    gain = pl.BlockSpec((1, d), lambda j, st, ea, eb, lo, hi, nu, tok: (0, 0))
    grid_spec = pltpu.PrefetchScalarGridSpec(
        num_scalar_prefetch=7,
        grid=(item_start.shape[0],),
        in_specs=[pl.BlockSpec(memory_space=pl.ANY), gain, gain,
                  gate_a, gate_a, down_a, gate_b, gate_b, down_b],
        out_specs=pl.BlockSpec(memory_space=pl.ANY),
        scratch_shapes=[pltpu.VMEM((GATHER_SLOTS, tm, w), F32), pltpu.VMEM((2, tm, d), F32),
                        pltpu.SemaphoreType.DMA((GATHER_SLOTS,)), pltpu.SemaphoreType.DMA(())],
    )
    vmem = 2 * 6 * d * f * 2 + GATHER_SLOTS * tm * w * 4 + 2 * tm * d * 4 + 8 * tm * d * 4
    return pl.pallas_call(
        _expert_pair_kernel,
        grid_spec=grid_spec,
        out_shape=jax.ShapeDtypeStruct((n, d), F32),
        compiler_params=pltpu.CompilerParams(
            dimension_semantics=("arbitrary",), vmem_limit_bytes=vmem + 4 * MIB),
        name="expert_pairs",
    )(item_start, item_ea, item_eb, item_lo, item_hi, n_used, slot_token, rows, g_ffn.reshape(1, d),
      g_final.reshape(1, d), wg, wu, wd, wg, wu, wd)


def kernel(x, g_mix, w_in, w_pool, pool_scale, w_fourier, w_out, g_ffn, w_group_router,
           b_group_router, w_expert_router, b_expert_router, w_gate, w_up, w_down, g_final):
    b, s, d = x.shape
    assert d == D_MODEL and s % (2 * FOLD_TILES * SEQ_TILE) == 0 and s % (2 * TOKEN_TILE) == 0
    assert (b * s) % max(TOKEN_TILE, EXPERT_TILE) == 0
    n = b * s
    x2 = x.reshape(n, d)

    cw, sw, w_top_bf16 = _mixer_weights(w_fourier, w_pool, pool_scale, w_out, s)
    u, (w_out_bf16, w_gate_bf16) = _norm_proj(x2, g_mix, w_in, [w_out, w_gate])
    mixed_lo, mixed_hi, w_up_bf16 = _mix(u.reshape(b, s, d), cw, sw, w_up)

    wr = jnp.concatenate([w_group_router, w_expert_router], axis=1)
    wr = jnp.pad(wr, ((0, 0), (0, ROUTER_LANES - wr.shape[1]))).astype(BF16)
    br = jnp.concatenate([b_group_router, b_expert_router])
    br = jnp.pad(br, (0, ROUTER_LANES - br.shape[0])).reshape(1, ROUTER_LANES)

    rows, route, counts, w_down_bf16 = _out_proj_route(
        x2, mixed_lo.reshape(n // 2, d), mixed_hi.reshape(n // 2, d), s, w_top_bf16, w_out_bf16, g_ffn, wr, br,
        w_down)
    slot_token, item_start, item_ea, item_eb, item_lo, item_hi, n_used = _plan_routing(route, counts, n)
    out = _expert_pairs(item_start, item_ea, item_eb, item_lo, item_hi, n_used, slot_token, rows,
                        g_ffn, g_final, w_gate_bf16, w_up_bf16, w_down_bf16)
    return out.reshape(b, s, d)
```

```python
import functools

import numpy as np
import jax
import jax.numpy as jnp
from jax import lax
from jax.experimental import pallas as pl
from jax.experimental.pallas import tpu as pltpu

D_MODEL = 2048
POOL_WINDOWS = (2, 4, 8, 16)
N_POOL_GROUPS = len(POOL_WINDOWS)
POOL_WIDTH = D_MODEL // 2
POOL_GROUP_DIM = POOL_WIDTH // N_POOL_GROUPS
FOURIER_WIDTH = D_MODEL - POOL_WIDTH
N_FOURIER_HEADS = 4
FOURIER_HEAD_DIM = FOURIER_WIDTH // N_FOURIER_HEADS
N_EXPERT_GROUPS = 4
EXPERTS_PER_GROUP = 4
EXPERT_FF = D_MODEL // 4
RMS_EPS = 1e-6

LANES = 128
SUBLANES = 8
BF16_SUBLANES = 16
ROUTER_LANES = LANES
MIB = 1024 * 1024

TOKEN_TILE = 512
SEQ_TILE = 256
POOL_HALO = BF16_SUBLANES
DFT_ROWS = SEQ_TILE + BF16_SUBLANES
TWIDDLE_ROWS = 16
FOLD_TILES = 4

PAIR_SLOT_A = (0, 0, 0, 1, 1, 3)
PAIR_SLOT_B = (1, 2, 3, 3, 2, 2)
PAIRS_PER_GROUP = len(PAIR_SLOT_A)
N_CLASSES = N_EXPERT_GROUPS * PAIRS_PER_GROUP
ROW_WORDS = D_MODEL + LANES
RANK_RADIX = 128
EXPERT_TILE = 256
DMA_PARTS = 8
GATHER_SLOTS = 3
INVERT_UNROLL = 32
WEIGHT_CAST_ROWS = 256

BF16 = jnp.bfloat16
F32 = jnp.float32


def _rmsnorm(x, g):
    ms = jnp.mean(x * x, axis=-1, keepdims=True)
    return x * lax.rsqrt(ms + RMS_EPS) * g


def _dot(a, b):
    return jnp.dot(a, b, preferred_element_type=F32)


def _twiddle(rows, cols, period):
    m = (np.asarray(rows, np.int64)[:, None] * np.asarray(cols, np.int64)[None, :]) % period
    ang = (2.0 * np.pi / period) * m.astype(np.float64)
    return np.cos(ang).astype(np.float32), np.sin(ang).astype(np.float32)


def _pool_band(seq_len, tile, halo):
    n_tiles = seq_len // tile
    out = np.zeros((3, N_POOL_GROUPS, tile, tile + 2 * halo), np.float64)
    for v, m in enumerate((0, 1, n_tiles - 1)):
        t0 = m * tile
        for g, k in enumerate(POOL_WINDOWS):
            for r in range(tile):
                t = t0 + r
                lo = max(t - (k - 1) // 2, 0)
                hi = min(t + k // 2 + 1, seq_len)
                out[v, g, r, lo - t0 + halo:hi - t0 + halo] = 1.0 / (hi - lo)
                out[v, g, r, r + halo] -= 1.0
    return out.astype(np.float32)


def _mixer_weight_kernel(cd_ref, sd_ref, wf_ref, wp_ref, ps_ref, wo_ref, cw_ref, sw_ref, wtop_ref, *, scale):
    wf = wf_ref[0]
    cw = jnp.dot(cd_ref[...], wf, preferred_element_type=F32, precision=lax.Precision.HIGHEST)
    sw = jnp.dot(sd_ref[...], wf, preferred_element_type=F32, precision=lax.Precision.HIGHEST)
    cw_ref[0] = (cw * scale).astype(BF16)
    sw_ref[0] = (sw * (-scale)).astype(BF16)
    wtop_ref[...] = _dot((wp_ref[0] * ps_ref[0]).astype(BF16), wo_ref[...].astype(BF16)).astype(BF16)


def _mixer_weights(w_fourier, w_pool, pool_scale, w_out, seq_len):
    dh = FOURIER_HEAD_DIM
    assert N_FOURIER_HEADS == N_POOL_GROUPS and dh == POOL_GROUP_DIM
    d = w_out.shape[1]
    cd, sd = _twiddle(np.arange(dh), np.arange(dh), dh)
    scale = 1.0 / np.sqrt(float(seq_len * dh))
    mat = pl.BlockSpec((dh, dh), lambda h: (0, 0))
    per_head = pl.BlockSpec((1, dh, dh), lambda h: (h, 0, 0))
    out_rows = pl.BlockSpec((dh, d), lambda h: (h, 0))
    return pl.pallas_call(
        functools.partial(_mixer_weight_kernel, scale=scale),
        grid=(N_FOURIER_HEADS,),
        in_specs=[mat, mat, per_head, per_head, pl.BlockSpec((1, 1, dh), lambda h: (h, 0, 0)), out_rows],
        out_specs=[per_head, per_head, out_rows],
        out_shape=[jax.ShapeDtypeStruct((N_FOURIER_HEADS, dh, dh), BF16)] * 2
        + [jax.ShapeDtypeStruct((POOL_WIDTH, d), BF16)],
        name="mixer_weights",
    )(jnp.asarray(cd), jnp.asarray(sd), w_fourier, w_pool, pool_scale.reshape(N_POOL_GROUPS, 1, dh), w_out)


class _CastAlong:
    def __init__(self, w, n_chunks, chunk_of):
        cols = w.shape[-1]
        rows = w.size // cols // n_chunks
        self.shape = w.shape
        self.src = w.reshape(n_chunks, rows, cols)
        self.spec = pl.BlockSpec((1, rows, cols), lambda *idx: (chunk_of(*idx), 0, 0))
        self.out_shape = jax.ShapeDtypeStruct((n_chunks, rows, cols), BF16)
        self.vmem_bytes = 2 * rows * cols * (4 + 2)


def _cast_chunks(src_refs, dst_refs):
    for src_ref, dst_ref in zip(src_refs, dst_refs):
        dst_ref[...] = src_ref[...].astype(BF16)


def _norm_proj_kernel(x_ref, g_ref, w_ref, *rest, n_cast):
    cast_src, (u_ref, *cast_dst) = rest[:n_cast], rest[n_cast:-3]
    w_bf16_ref, stage_ref, stage_sem = rest[-3:]

    @pl.when(pl.program_id(0) == 0)
    def _():
        rows = WEIGHT_CAST_ROWS
        n_chunks = w_ref.shape[0] // rows

        def chunk_copy(k):
            return pltpu.make_async_copy(w_ref.at[pl.ds(k * rows, rows), :], stage_ref.at[k % 2],
                                         stage_sem.at[k % 2])

        chunk_copy(0).start()
        for k in range(n_chunks):
            if k + 1 < n_chunks:
                chunk_copy(k + 1).start()
            chunk_copy(k).wait()
            w_bf16_ref[k * rows:(k + 1) * rows, :] = stage_ref[k % 2].astype(BF16)

    x = x_ref[...]
    inv_rms = lax.rsqrt(jnp.mean(x * x, axis=-1, keepdims=True) + RMS_EPS)
    u_ref[...] = (_dot((x * g_ref[...]).astype(BF16), w_bf16_ref[...]) * inv_rms).astype(BF16)
    _cast_chunks(cast_src, cast_dst)


def _norm_proj(x2, g_mix, w_in, cast_weights):
    n, d = x2.shape
    tm = TOKEN_TILE
    casts = [_CastAlong(w, n // tm, lambda i: i) for w in cast_weights]
    assert d % WEIGHT_CAST_ROWS == 0
    vmem = (2 * tm * d * 4 + d * d * 2 + 2 * WEIGHT_CAST_ROWS * d * 4 + 2 * tm * d * 2 + 3 * tm * d * 4
            + sum(c.vmem_bytes for c in casts))
    u, *cast_out = pl.pallas_call(
        functools.partial(_norm_proj_kernel, n_cast=len(casts)),
        grid=(n // tm,),
        in_specs=[
            pl.BlockSpec((tm, d), lambda i: (i, 0)),
            pl.BlockSpec((1, d), lambda i: (0, 0)),
            pl.BlockSpec(memory_space=pl.ANY),
        ] + [c.spec for c in casts],
        out_specs=[pl.BlockSpec((tm, d), lambda i: (i, 0))] + [c.spec for c in casts],
        out_shape=[jax.ShapeDtypeStruct((n, d), BF16)] + [c.out_shape for c in casts],
        scratch_shapes=[pltpu.VMEM((d, d), BF16), pltpu.VMEM((2, WEIGHT_CAST_ROWS, d), F32),
                        pltpu.SemaphoreType.DMA((2,))],
        compiler_params=pltpu.CompilerParams(
            dimension_semantics=("arbitrary",), vmem_limit_bytes=vmem + 4 * MIB),
        name="norm_proj",
    )(x2, g_mix.reshape(1, d), w_in, *[c.src for c in casts])
    return u, [o.reshape(c.shape) for o, c in zip(cast_out, casts)]


def _fold_kernel(blk_ref, mirror_ref, after_ref, rev_ref, cw_ref, sw_ref, even_ref, odd_ref):
    t = SEQ_TILE
    halo = POOL_HALO
    hd = FOURIER_HEAD_DIM
    for i in range(FOLD_TILES):
        top = (FOLD_TILES - 1 - i) * t
        if i == 0:
            after = jnp.where(pl.program_id(1) > 0, after_ref[...], jnp.zeros_like(after_ref))
        else:
            after = mirror_ref[top + t:top + t + halo, :]
        window = jnp.concatenate([mirror_ref[top:top + t, :], after], axis=0)
        mirrored = _dot(rev_ref[...], window)
        blk = blk_ref[i * t:(i + 1) * t, :].astype(F32)
        even_ref[i * t:(i + 1) * t, :] = (blk + mirrored).astype(BF16)
        odd_ref[i * t:(i + 1) * t, :] = (blk - mirrored).astype(BF16)
    for h in range(N_FOURIER_HEADS):
        cols = slice(h * hd, (h + 1) * hd)
        even_ref[:, cols] = _dot(even_ref[:, cols], cw_ref[h]).astype(BF16)
        odd_ref[:, cols] = _dot(odd_ref[:, cols], sw_ref[h]).astype(BF16)


def _fold_sequence(u3, rev, cw, sw):
    b, s, _ = u3.shape
    halo = POOL_HALO
    rows = FOLD_TILES * SEQ_TILE
    n_blocks = s // rows
    block = lambda block_of: pl.BlockSpec((None, rows, FOURIER_WIDTH), lambda bi, j: (bi, block_of(j), 1))
    after = pl.BlockSpec((None, halo, FOURIER_WIDTH),
                         lambda bi, j: (bi, jnp.minimum((n_blocks - j) * (rows // halo), s // halo - 1), 1))
    out = pl.BlockSpec((None, rows, FOURIER_WIDTH), lambda bi, j: (bi, j, 0))
    return pl.pallas_call(
        _fold_kernel,
        grid=(b, n_blocks // 2),
        in_specs=[block(lambda j: j), block(lambda j: n_blocks - 1 - j), after,
                  pl.BlockSpec(rev.shape, lambda bi, j: (0, 0)),
                  pl.BlockSpec(cw.shape, lambda bi, j: (0, 0, 0)), pl.BlockSpec(sw.shape, lambda bi, j: (0, 0, 0))],
        out_specs=[out, out],
        out_shape=[jax.ShapeDtypeStruct((b, s // 2, FOURIER_WIDTH), BF16)] * 2,
        compiler_params=pltpu.CompilerParams(
            dimension_semantics=("arbitrary", "arbitrary"),
            vmem_limit_bytes=2 * 4 * rows * FOURIER_WIDTH * 2 + 8 * MIB),
        name="fold_sequence",
    )(u3, u3, u3, rev, cw, sw)


def _mix_kernel(even_ref, odd_ref, umid_ref, up_lo_ref, prev_lo_ref, next_lo_ref, up_hi_ref, prev_hi_ref, next_hi_ref,
                c0_ref, s0_ref, cph_ref, sph_ref, band_lo_ref, band_hi_ref, rev_ref,
                cw_ref, cast_src_ref, lo_ref, hi_ref, cast_dst_ref, lhs_ref):
    t = SEQ_TILE
    tp = DFT_ROWS
    gd = POOL_GROUP_DIM
    hd = FOURIER_HEAD_DIM

    m = pl.program_id(1)

    @pl.when(pl.program_id(0) == 0)
    def _():
        cph = cph_ref[0]
        sph = sph_ref[0]
        for r0 in range(0, tp, TWIDDLE_ROWS):
            c0 = c0_ref[r0:r0 + TWIDDLE_ROWS, :]
            s0 = s0_ref[r0:r0 + TWIDDLE_ROWS, :]
            lhs_ref[m, r0:r0 + TWIDDLE_ROWS, :] = (cph * c0 - sph * s0).astype(BF16)
            lhs_ref[m, tp + r0:tp + r0 + TWIDDLE_ROWS, :] = (sph * c0 + cph * s0).astype(BF16)

    _cast_chunks([cast_src_ref], [cast_dst_ref])
    row = lax.broadcasted_iota(jnp.int32, (tp, 1), 0)
    sign = jnp.where(row % 2 == 0, 1.0, -1.0)
    mid = jnp.concatenate([_dot(umid_ref[:, h * hd:(h + 1) * hd], cw_ref[h]) for h in range(N_FOURIER_HEADS)],
                          axis=1)
    pc = _dot(lhs_ref[m, :tp, :], even_ref[...]) + sign * mid[0:1, :]
    qs = _dot(lhs_ref[m, tp:, :], odd_ref[...])
    lo_ref[:, POOL_WIDTH:] = (pc + qs)[:t].astype(BF16)
    hi_ref[:, POOL_WIDTH:] = _dot(rev_ref[...], (pc - qs).astype(BF16)).astype(BF16)

    def pool(up_ref, prev_ref, next_ref, band_ref, out_ref):
        win = jnp.concatenate([prev_ref[...], up_ref[...], next_ref[...]], axis=0)
        for g in range(N_POOL_GROUPS):
            cols = slice(g * gd, (g + 1) * gd)
            out_ref[:, cols] = _dot(band_ref[0, g], win[:, cols]).astype(BF16)

    pool(up_lo_ref, prev_lo_ref, next_lo_ref, band_lo_ref, lo_ref)
    pool(up_hi_ref, prev_hi_ref, next_hi_ref, band_hi_ref, hi_ref)


def _mix(u3, cw, sw, cast_weight):
    b, s, d = u3.shape
    t = SEQ_TILE
    tp = DFT_ROWS
    halo = POOL_HALO
    n_tiles = s // t
    n_steps = n_tiles // 2
    halo_blocks_per_tile = t // halo
    last_halo_block = s // halo - 1

    sh = s // 2
    c0, s0 = _twiddle(np.arange(tp), np.arange(sh), s)
    cph, sph = _twiddle(np.arange(0, sh, t), np.arange(sh), s)
    band = jnp.asarray(_pool_band(s, t, halo)).astype(BF16)
    rev = np.zeros((t, tp), np.float32)
    rev[np.arange(t), t - np.arange(t)] = 1.0
    rev = jnp.asarray(rev).astype(BF16)
    even, odd = _fold_sequence(u3, rev, cw, sw)

    hi_tile = lambda m: n_tiles - 1 - m
    tile_spec = lambda tile_of: pl.BlockSpec((None, t, POOL_WIDTH), lambda bi, m: (bi, tile_of(m), 0))
    prev_spec = lambda tile_of: pl.BlockSpec(
        (None, halo, POOL_WIDTH),
        lambda bi, m: (bi, jnp.maximum(tile_of(m) * halo_blocks_per_tile - 1, 0), 0))
    next_spec = lambda tile_of: pl.BlockSpec(
        (None, halo, POOL_WIDTH),
        lambda bi, m: (bi, jnp.minimum((tile_of(m) + 1) * halo_blocks_per_tile, last_halo_block), 0))
    band_shape = (1, N_POOL_GROUPS, t, t + 2 * halo)
    const2 = lambda shape: pl.BlockSpec(shape, lambda bi, m: (0, 0))
    const3 = lambda shape: pl.BlockSpec(shape, lambda bi, m: (0, 0, 0))
    lo_tile = lambda m: m

    in_specs = [
        pl.BlockSpec((None, sh, FOURIER_WIDTH), lambda bi, m: (bi, 0, 0)),
        pl.BlockSpec((None, sh, FOURIER_WIDTH), lambda bi, m: (bi, 0, 0)),
        pl.BlockSpec((None, halo, FOURIER_WIDTH), lambda bi, m: (bi, sh // halo, 1)),
        tile_spec(lo_tile), prev_spec(lo_tile), next_spec(lo_tile),
        tile_spec(hi_tile), prev_spec(hi_tile), next_spec(hi_tile),
        pl.BlockSpec((tp, sh), lambda bi, m: (0, 0), pipeline_mode=pl.Buffered(1)),
        pl.BlockSpec((tp, sh), lambda bi, m: (0, 0), pipeline_mode=pl.Buffered(1)),
        pl.BlockSpec((1, 1, sh), lambda bi, m: (m, 0, 0)),
        pl.BlockSpec((1, 1, sh), lambda bi, m: (m, 0, 0)),
        pl.BlockSpec(band_shape, lambda bi, m: (jnp.where(m == 0, 0, 1), 0, 0, 0)),
        pl.BlockSpec(band_shape, lambda bi, m: (jnp.where(m == 0, 2, 1), 0, 0, 0)),
        const2((t, tp)),
        const3((N_FOURIER_HEADS, FOURIER_HEAD_DIM, FOURIER_HEAD_DIM)),
    ]
    half_out = pl.BlockSpec((None, t, d), lambda bi, m: (bi, m, 0))
    hi_out = pl.BlockSpec((None, t, d), lambda bi, m: (bi, n_steps - 1 - m, 0))
    cast = _CastAlong(cast_weight, n_steps * b, lambda bi, m: m * b + bi)
    vmem = (2 * 2 * sh * FOURIER_WIDTH * 2
            + 2 * tp * sh * 4
            + n_steps * 2 * tp * sh * 2
            + 2 * 2 * t * d * 2 * 2
            + 4 * 2 * tp * FOURIER_WIDTH * 4
            + cast.vmem_bytes)
    lo, hi, cast_out = pl.pallas_call(
        _mix_kernel,
        grid=(b, n_steps),
        in_specs=in_specs + [cast.spec],
        out_specs=[half_out, hi_out, cast.spec],
        out_shape=[jax.ShapeDtypeStruct((b, s // 2, d), BF16)] * 2 + [cast.out_shape],
        scratch_shapes=[pltpu.VMEM((n_steps, 2 * tp, sh), BF16)],
        compiler_params=pltpu.CompilerParams(
            dimension_semantics=("arbitrary", "arbitrary"), vmem_limit_bytes=vmem + 4 * MIB),
        name="seq_mix",
    )(even, odd, u3, u3, u3, u3, u3, u3, u3, jnp.asarray(c0), jnp.asarray(s0),
      jnp.asarray(cph).reshape(n_steps, 1, sh), jnp.asarray(sph).reshape(n_steps, 1, sh),
      band, band, rev,
      cw, cast.src)
    return lo, hi, cast_out.reshape(cast.shape)


def _out_proj_route_kernel(x_ref, mixed_lo_ref, mixed_hi_ref, wtop_ref, wbot_ref, g_ref, wr_ref, br_ref,
                           tri_ref, pick_ref, cast_src_ref, rows_ref, route_ref, counts_ref, cast_dst_ref,
                           carry_ref, x1_ref, *, tiles_per_seq, n_tiles):
    step = pl.program_id(0)

    @pl.when(step == 0)
    def _():
        carry_ref[...] = jnp.zeros_like(carry_ref)
        x1_ref[...] = jnp.zeros_like(x1_ref)

    _cast_chunks([cast_src_ref], [cast_dst_ref])

    x1 = x1_ref[...]
    rows_ref[:, :D_MODEL] = x1
    h2 = _rmsnorm(x1, g_ref[...])
    logits = _dot(h2.astype(BF16), wr_ref[...]) + br_ref[...]

    tile = jnp.minimum(step, n_tiles - 1)
    in_lo_half = (tile % tiles_per_seq) < tiles_per_seq // 2
    mixed = jnp.where(in_lo_half, mixed_lo_ref[...], mixed_hi_ref[...])
    x1_ref[...] = (x_ref[...] + _dot(mixed[:, :POOL_WIDTH], wtop_ref[...])
                   + _dot(mixed[:, POOL_WIDTH:], wbot_ref[...]))

    lane = lax.broadcasted_iota(jnp.int32, logits.shape, 1)
    neg = jnp.float32(-jnp.inf)
    big = jnp.int32(ROUTER_LANES)

    is_group = lane < N_EXPERT_GROUPS
    gl = jnp.where(is_group, logits, neg)
    gmax = jnp.max(gl, axis=-1, keepdims=True)
    gidx = jnp.min(jnp.where(gl == gmax, lane, big), axis=-1, keepdims=True)
    p_g = 1.0 / jnp.sum(jnp.exp(gl - gmax), axis=-1, keepdims=True)

    e_lane = lane - N_EXPERT_GROUPS
    in_group = (e_lane >= gidx * EXPERTS_PER_GROUP) & (e_lane < (gidx + 1) * EXPERTS_PER_GROUP)
    el = jnp.where(in_group, logits, neg)
    v1 = jnp.max(el, axis=-1, keepdims=True)
    i1 = jnp.min(jnp.where(el == v1, lane, big), axis=-1, keepdims=True)
    el2 = jnp.where(lane == i1, neg, el)
    v2 = jnp.max(el2, axis=-1, keepdims=True)
    i2 = jnp.min(jnp.where(el2 == v2, lane, big), axis=-1, keepdims=True)
    r = jnp.exp(v2 - v1)
    w1 = p_g / (1.0 + r)
    w2 = p_g * r / (1.0 + r)

    first_is_low = i1 < i2
    first_lane = N_EXPERT_GROUPS + gidx * EXPERTS_PER_GROUP
    la = jnp.where(first_is_low, i1, i2) - first_lane
    lb = jnp.where(first_is_low, i2, i1) - first_lane
    w_low = jnp.where(first_is_low, w1, w2)
    w_high = jnp.where(first_is_low, w2, w1)
    assert (PAIR_SLOT_A, PAIR_SLOT_B) == ((0, 0, 0, 1, 1, 3), (1, 2, 3, 3, 2, 2))
    pair = jnp.where(la == 0, lb - 1, jnp.where(la == 1, 6 - lb, 5))
    slot_a_is_high = la == 2
    w_a = jnp.where(slot_a_is_high, w_high, w_low)
    w_b = jnp.where(slot_a_is_high, w_low, w_high)
    cls = gidx * PAIRS_PER_GROUP + pair
    rows_ref[:, D_MODEL:] = jnp.where(lane == 0, w_a, jnp.where(lane == 1, w_b, 0.0))

    onehot = jnp.where(lane == cls, 1.0, 0.0)
    before = _dot(tri_ref[...], onehot.astype(BF16)) + carry_ref[...]
    rank = jnp.sum(jnp.where(lane == cls, before, 0.0), axis=-1, keepdims=True)
    carry_ref[...] += jnp.sum(onehot, axis=0, keepdims=True) * jnp.where(step > 0, 1.0, 0.0)
    counts_ref[...] = carry_ref[...]

    rank_hi = jnp.floor(rank * (1.0 / RANK_RADIX))
    rank_lo = rank - rank_hi * RANK_RADIX
    digits = jnp.where(lane == 0, cls.astype(F32),
                       jnp.where(lane == 1, rank_hi, jnp.where(lane == 2, rank_lo, 0.0)))
    route_ref[...] = lax.dot_general(pick_ref[...], digits.astype(BF16), (((1,), (1,)), ((), ())),
                                     preferred_element_type=F32)


def _out_proj_route(x2, mixed_lo, mixed_hi, seq_len, w_top_bf16, w_out_bf16, g_ffn, wr, br, cast_weight):
    n, d = x2.shape
    tm = TOKEN_TILE
    n_tiles = n // tm
    tiles_per_seq = seq_len // tm
    half_tiles = tiles_per_seq // 2
    const = lambda shape: pl.BlockSpec(shape, lambda i: (0, 0))
    in_tile = lambda i: jnp.minimum(i, n_tiles - 1)
    out_tile = lambda i: jnp.maximum(i - 1, 0)
    lo_spec = pl.BlockSpec((tm, d), lambda i: (
        (in_tile(i) // tiles_per_seq) * half_tiles + jnp.minimum(in_tile(i) % tiles_per_seq, half_tiles - 1), 0))
    hi_spec = pl.BlockSpec((tm, d), lambda i: (
        (in_tile(i) // tiles_per_seq) * half_tiles + jnp.maximum(in_tile(i) % tiles_per_seq - half_tiles, 0), 0))
    tri = np.tril(np.ones((tm, tm), np.float32), -1)
    pick = np.eye(SUBLANES, ROUTER_LANES, dtype=np.float32)
    vmem = (2 * tm * d * 4 + 4 * tm * d * 2 + 2 * tm * ROW_WORDS * 4 + d * d * 2 + tm * d * 4
            + 2 * d * ROUTER_LANES * 2 + 2 * tm * tm * 2 + 3 * tm * d * 4)
    cast = _CastAlong(cast_weight, n_tiles, in_tile)
    vmem += cast.vmem_bytes
    rows, route, counts, cast_out = pl.pallas_call(
        functools.partial(_out_proj_route_kernel, tiles_per_seq=tiles_per_seq, n_tiles=n_tiles),
        grid=(n_tiles + 1,),
        in_specs=[pl.BlockSpec((tm, d), lambda i: (in_tile(i), 0)), lo_spec, hi_spec,
                  pl.BlockSpec((POOL_WIDTH, d), lambda i: (0, 0), pipeline_mode=pl.Buffered(1)),
                  pl.BlockSpec((FOURIER_WIDTH, d), lambda i: (1, 0), pipeline_mode=pl.Buffered(1)),
                  const((1, d)), const((d, ROUTER_LANES)), const((1, ROUTER_LANES)),
                  const((tm, tm)), const((SUBLANES, ROUTER_LANES)), cast.spec],
        out_specs=[pl.BlockSpec((tm, ROW_WORDS), lambda i: (out_tile(i), 0)),
                   pl.BlockSpec((SUBLANES, tm), lambda i: (0, out_tile(i))),
                   const((1, ROUTER_LANES)), cast.spec],
        out_shape=[jax.ShapeDtypeStruct((n, ROW_WORDS), F32),
                   jax.ShapeDtypeStruct((SUBLANES, n), F32),
                   jax.ShapeDtypeStruct((1, ROUTER_LANES), F32), cast.out_shape],
        scratch_shapes=[pltpu.VMEM((1, ROUTER_LANES), F32), pltpu.VMEM((tm, d), F32)],
        compiler_params=pltpu.CompilerParams(
            dimension_semantics=("arbitrary",), vmem_limit_bytes=vmem + 4 * MIB),
        name="out_proj_route",
    )(x2, mixed_lo, mixed_hi, w_top_bf16, w_out_bf16, g_ffn.reshape(1, d), wr, br,
      jnp.asarray(tri).astype(BF16), jnp.asarray(pick).astype(BF16), cast.src)
    return rows, route, counts, cast_out.reshape(cast.shape)


def _row_copy(src_ref, src_row, dst_ref, dst_row, sem):
    return pltpu.make_async_copy(src_ref.at[pl.ds(src_row, 1), :], dst_ref.at[pl.ds(dst_row, 1), :], sem)


def _plan_kernel(cnt_ref, route_ref, tok_ref, start_ref, ea_ref, eb_ref, lo_ref, hi_ref, nused_ref,
                 cstart_ref, pos_vmem_ref, pos_smem_ref, sem):
    tm = EXPERT_TILE
    shift = tm.bit_length() - 1
    n = tok_ref.shape[0]
    n_items = start_ref.shape[0]

    run = jnp.int32(0)
    inside = jnp.bool_(True)
    for c in range(N_CLASSES):
        cnt = cnt_ref[c]
        cstart_ref[c] = run
        inside = inside & ((cnt == 0) | (cnt >= tm))
        run = run + cnt

    cls_row = route_ref[0:1, :]
    slot_row = route_ref[1:2, :] * RANK_RADIX + route_ref[2:3, :]
    for c in range(N_CLASSES):
        slot_row = slot_row + jnp.where(cls_row == c, cstart_ref[c].astype(F32), 0.0)
    pos_vmem_ref[...] = slot_row.astype(jnp.int32)
    to_smem = pltpu.make_async_copy(pos_vmem_ref, pos_smem_ref, sem)
    to_smem.start()

    k = jnp.int32(0)
    for c in range(N_CLASSES):
        cnt = cnt_ref[c]
        cs = cstart_ref[c]
        ce = cs + cnt
        group, pair = divmod(c, PAIRS_PER_GROUP)
        ea = group * EXPERTS_PER_GROUP + PAIR_SLOT_A[pair]
        eb = group * EXPERTS_PER_GROUP + PAIR_SLOT_B[pair]
        first_window = cs >> shift
        n_inside = (cnt + tm - 1) >> shift
        n_aligned = jnp.where(cnt > 0, ((ce - 1) >> shift) - first_window + 1, 0)
        full_windows = cnt >> shift

        def emit(i, carry, k=k, cs=cs, ce=ce, ea=ea, eb=eb, first_window=first_window,
                 full_windows=full_windows):
            start_inside = jnp.where(i < full_windows, cs + i * tm, ce - tm)
            start_aligned = (first_window + i) * tm
            start_ref[k + i] = jnp.where(inside, start_inside, start_aligned)
            ea_ref[k + i] = ea
            eb_ref[k + i] = eb
            lo_ref[k + i] = jnp.where(inside, 0, jnp.clip(cs - start_aligned, 0, tm))
            hi_ref[k + i] = jnp.where(inside, tm, jnp.clip(ce - start_aligned, 0, tm))
            return carry

        n_class_items = jnp.where(inside, n_inside, n_aligned)
        lax.fori_loop(0, n_class_items, emit, 0)
        k = k + n_class_items
    nused_ref[0] = k

    def repeat_last(i, carry):
        for ref in (start_ref, ea_ref, eb_ref, lo_ref, hi_ref):
            ref[i] = ref[k - 1]
        return carry

    lax.fori_loop(k, n_items, repeat_last, 0)

    to_smem.wait()

    def invert(t, carry):
        tok_ref[pos_smem_ref[0, t]] = t
        return carry

    lax.fori_loop(0, n, invert, 0, unroll=INVERT_UNROLL)


def _plan_routing(route, counts, n):
    n_items = n // EXPERT_TILE + N_CLASSES
    smem = lambda size: jax.ShapeDtypeStruct((size,), jnp.int32)
    smem_spec = pl.BlockSpec(memory_space=pltpu.SMEM)
    return pl.pallas_call(
        _plan_kernel,
        in_specs=[smem_spec, pl.BlockSpec(memory_space=pltpu.VMEM)],
        out_specs=[smem_spec] * 7,
        out_shape=[smem(n)] + [smem(n_items)] * 5 + [smem(1)],
        scratch_shapes=[pltpu.SMEM((N_CLASSES,), jnp.int32), pltpu.VMEM((1, n), jnp.int32),
                        pltpu.SMEM((1, n), jnp.int32), pltpu.SemaphoreType.DMA(())],
        name="routing_plan",
    )(counts[0, :N_CLASSES].astype(jnp.int32), route)


def _expert_pair_kernel(start_ref, ea_ref, eb_ref, lo_ref, hi_ref, nused_ref, tok_ref,
                        rows_hbm_ref, gffn_ref, gfin_ref,
                        wga_ref, wua_ref, wda_ref, wgb_ref, wub_ref, wdb_ref, out_hbm_ref,
                        buf_ref, acc_ref, gather_sem, scatter_sem):
    del ea_ref, eb_ref
    tm = EXPERT_TILE
    j = pl.program_id(0)
    last = nused_ref[0] - 1
    slot = j % 2
    other = 1 - slot

    def start_gather(item, dst_slot, rows=range(tm)):
        base = start_ref[item]
        for r in rows:
            _row_copy(rows_hbm_ref, tok_ref[base + r], buf_ref.at[dst_slot], r, gather_sem.at[dst_slot]).start()

    def wait_gather(dst_slot):
        pltpu.make_async_copy(rows_hbm_ref.at[pl.ds(0, tm), :], buf_ref.at[dst_slot],
                              gather_sem.at[dst_slot]).wait()

    def start_scatter(item, src_slot, rows=range(tm)):
        base = start_ref[item]
        for r in rows:
            _row_copy(acc_ref.at[src_slot], r, out_hbm_ref, tok_ref[base + r], scatter_sem).start()

    def wait_scatter():
        pltpu.make_async_copy(acc_ref.at[0], out_hbm_ref.at[pl.ds(0, tm), :], scatter_sem).wait()

    gslot = j % GATHER_SLOTS

    @pl.when(j == 0)
    def _():
        acc_ref[...] = jnp.zeros_like(acc_ref)
        start_gather(0, 0)
        start_gather(jnp.minimum(1, last), 1)
        start_scatter(0, 1)

    @pl.when(j <= last)
    def _():
        wait_scatter()
        wait_gather(gslot)
        prev_item = jnp.maximum(j - 1, 0)
        ahead_item = jnp.minimum(j + 2, last)

        def issue_copies(part):
            half_parts = DMA_PARTS // 2
            rows = range((part % half_parts) * tm // half_parts, (part % half_parts + 1) * tm // half_parts)
            if part < half_parts:
                start_scatter(prev_item, other, rows)
            else:
                start_gather(ahead_item, (j + 2) % GATHER_SLOTS, rows)

        lo = lo_ref[j]
        hi = hi_ref[j]
        x1 = buf_ref[gslot, :, :D_MODEL]
        h = _rmsnorm(x1, gffn_ref[...]).astype(BF16)

        part = 0
        ff_chunk = EXPERT_FF * 4 // DMA_PARTS
        for slot_lane, (wg_ref, wu_ref, wd_ref) in enumerate(((wga_ref, wua_ref, wda_ref),
                                                              (wgb_ref, wub_ref, wdb_ref))):
            for c0 in range(0, EXPERT_FF, ff_chunk):
                issue_copies(part)
                wts = buf_ref[gslot, :, D_MODEL:]
                lane = lax.broadcasted_iota(jnp.int32, wts.shape, 1)
                w = jnp.sum(jnp.where(lane == slot_lane, wts, 0.0), axis=-1, keepdims=True)
                a = _dot(h, wg_ref[0, :, c0:c0 + ff_chunk])
                v = _dot(h, wu_ref[0, :, c0:c0 + ff_chunk])
                act = (a * (1.0 / (1.0 + jnp.exp(-a))) * v * w).astype(BF16)
                for half, d0 in enumerate(range(0, D_MODEL, D_MODEL // 2)):
                    cols = slice(d0, d0 + D_MODEL // 2)
                    if half == 1:
                        issue_copies(part + 1)
                    y_part = _dot(act, wd_ref[0, c0:c0 + ff_chunk, cols])
                    if part == 0:
                        acc_ref[slot, :, cols] = y_part
                    else:
                        acc_ref[slot, :, cols] += y_part
                part += 2
        res = _rmsnorm(buf_ref[gslot, :, :D_MODEL] + acc_ref[slot], gfin_ref[...])

        row = lax.broadcasted_iota(jnp.int32, (tm, 1), 0)
        mine = (row >= lo) & (row < hi)
        acc_ref[slot] = jnp.where(mine, res, 0.0)

        @pl.when(lo > 0)
        def _():
            acc_ref[slot] = jnp.where(row < lo, acc_ref[other], acc_ref[slot])

    @pl.when(j == last)
    def _():
        wait_scatter()
        start_scatter(j, slot)
        wait_scatter()
        wait_gather((j + 1) % GATHER_SLOTS)
        wait_gather((j + 2) % GATHER_SLOTS)


def _expert_pairs(item_start, item_ea, item_eb, item_lo, item_hi, n_used, slot_token, rows,
                  g_ffn, g_final, wg, wu, wd):
    n, w = rows.shape
    d, f = D_MODEL, EXPERT_FF
    tm = EXPERT_TILE
    gate_a = pl.BlockSpec((1, d, f), lambda j, st, ea, eb, lo, hi, nu, tok: (ea[j], 0, 0))
    gate_b = pl.BlockSpec((1, d, f), lambda j, st, ea, eb, lo, hi, nu, tok: (eb[j], 0, 0))
    down_a = pl.BlockSpec((1, f, d), lambda j, st, ea, eb, lo, hi, nu, tok: (ea[j], 0, 0))
    down_b = pl.BlockSpec((1, f, d), lambda j, st, ea, eb, lo, hi, nu, tok: (eb[j], 0, 0))
    gain = pl.BlockSpec((1, d), lambda j, st, ea, eb, lo, hi, nu, tok: (0, 0))
    grid_spec = pltpu.PrefetchScalarGridSpec(
        num_scalar_prefetch=7,
        grid=(item_start.shape[0],),
        in_specs=[pl.BlockSpec(memory_space=pl.ANY), gain, gain,
                  gate_a, gate_a, down_a, gate_b, gate_b, down_b],
        out_specs=pl.BlockSpec(memory_space=pl.ANY),
        scratch_shapes=[pltpu.VMEM((GATHER_SLOTS, tm, w), F32), pltpu.VMEM((2, tm, d), F32),
                        pltpu.SemaphoreType.DMA((GATHER_SLOTS,)), pltpu.SemaphoreType.DMA(())],
    )
    vmem = 2 * 6 * d * f * 2 + GATHER_SLOTS * tm * w * 4 + 2 * tm * d * 4 + 8 * tm * d * 4
    return pl.pallas_call(
        _expert_pair_kernel,
        grid_spec=grid_spec,
        out_shape=jax.ShapeDtypeStruct((n, d), F32),
        compiler_params=pltpu.CompilerParams(
            dimension_semantics=("arbitrary",), vmem_limit_bytes=vmem + 4 * MIB),
        name="expert_pairs",
    )(item_start, item_ea, item_eb, item_lo, item_hi, n_used, slot_token, rows, g_ffn.reshape(1, d),
      g_final.reshape(1, d), wg, wu, wd, wg, wu, wd)


def kernel(x, g_mix, w_in, w_pool, pool_scale, w_fourier, w_out, g_ffn, w_group_router,
           b_group_router, w_expert_router, b_expert_router, w_gate, w_up, w_down, g_final):
    b, s, d = x.shape
    assert d == D_MODEL and s % (2 * FOLD_TILES * SEQ_TILE) == 0 and s % (2 * TOKEN_TILE) == 0
    assert (b * s) % max(TOKEN_TILE, EXPERT_TILE) == 0
    n = b * s
    x2 = x.reshape(n, d)

    cw, sw, w_top_bf16 = _mixer_weights(w_fourier, w_pool, pool_scale, w_out, s)
    u, (w_out_bf16, w_gate_bf16) = _norm_proj(x2, g_mix, w_in, [w_out, w_gate])
    mixed_lo, mixed_hi, w_up_bf16 = _mix(u.reshape(b, s, d), cw, sw, w_up)

    wr = jnp.concatenate([w_group_router, w_expert_router], axis=1)
    wr = jnp.pad(wr, ((0, 0), (0, ROUTER_LANES - wr.shape[1]))).astype(BF16)
    br = jnp.concatenate([b_group_router, b_expert_router])
    br = jnp.pad(br, (0, ROUTER_LANES - br.shape[0])).reshape(1, ROUTER_LANES)

    rows, route, counts, w_down_bf16 = _out_proj_route(
        x2, mixed_lo.reshape(n // 2, d), mixed_hi.reshape(n // 2, d), s, w_top_bf16, w_out_bf16, g_ffn, wr, br,
        w_down)
    slot_token, item_start, item_ea, item_eb, item_lo, item_hi, n_used = _plan_routing(route, counts, n)
    out = _expert_pairs(item_start, item_ea, item_eb, item_lo, item_hi, n_used, slot_token, rows,
                        g_ffn, g_final, w_gate_bf16, w_up_bf16, w_down_bf16)
    return out.reshape(b, s, d)
```

```python
import functools

import numpy as np
import jax
import jax.numpy as jnp
from jax import lax
from jax.experimental import pallas as pl
from jax.experimental.pallas import tpu as pltpu

D_MODEL = 2048
POOL_WINDOWS = (2, 4, 8, 16)
N_POOL_GROUPS = len(POOL_WINDOWS)
POOL_WIDTH = D_MODEL // 2
POOL_GROUP_DIM = POOL_WIDTH // N_POOL_GROUPS
FOURIER_WIDTH = D_MODEL - POOL_WIDTH
N_FOURIER_HEADS = 4
FOURIER_HEAD_DIM = FOURIER_WIDTH // N_FOURIER_HEADS
N_EXPERT_GROUPS = 4
EXPERTS_PER_GROUP = 4
EXPERT_FF = D_MODEL // 4
RMS_EPS = 1e-6

LANES = 128
SUBLANES = 8
BF16_SUBLANES = 16
ROUTER_LANES = LANES
MIB = 1024 * 1024

TOKEN_TILE = 512
SEQ_TILE = 256
POOL_HALO = BF16_SUBLANES
DFT_ROWS = SEQ_TILE + BF16_SUBLANES
TWIDDLE_ROWS = 16
FOLD_TILES = 4

PAIR_SLOT_A = (0, 0, 0, 1, 1, 3)
PAIR_SLOT_B = (1, 2, 3, 3, 2, 2)
PAIRS_PER_GROUP = len(PAIR_SLOT_A)
N_CLASSES = N_EXPERT_GROUPS * PAIRS_PER_GROUP
ROW_WORDS = D_MODEL + LANES
RANK_RADIX = 128
EXPERT_TILE = 256
DMA_PARTS = 8
GATHER_SLOTS = 3
INVERT_UNROLL = 32
WEIGHT_CAST_ROWS = 256

BF16 = jnp.bfloat16
F32 = jnp.float32


def _rmsnorm(x, g):
    ms = jnp.mean(x * x, axis=-1, keepdims=True)
    return x * lax.rsqrt(ms + RMS_EPS) * g


def _dot(a, b):
    return jnp.dot(a, b, preferred_element_type=F32)


def _twiddle(rows, cols, period):
    m = (np.asarray(rows, np.int64)[:, None] * np.asarray(cols, np.int64)[None, :]) % period
    ang = (2.0 * np.pi / period) * m.astype(np.float64)
    return np.cos(ang).astype(np.float32), np.sin(ang).astype(np.float32)


def _pool_band(seq_len, tile, halo):
    n_tiles = seq_len // tile
    out = np.zeros((3, N_POOL_GROUPS, tile, tile + 2 * halo), np.float64)
    for v, m in enumerate((0, 1, n_tiles - 1)):
        t0 = m * tile
        for g, k in enumerate(POOL_WINDOWS):
            for r in range(tile):
                t = t0 + r
                lo = max(t - (k - 1) // 2, 0)
                hi = min(t + k // 2 + 1, seq_len)
                out[v, g, r, lo - t0 + halo:hi - t0 + halo] = 1.0 / (hi - lo)
                out[v, g, r, r + halo] -= 1.0
    return out.astype(np.float32)


def _mixer_weight_kernel(cd_ref, sd_ref, wf_ref, wp_ref, ps_ref, wo_ref, cw_ref, sw_ref, wtop_ref, *, scale):
    wf = wf_ref[0]
    cw = jnp.dot(cd_ref[...], wf, preferred_element_type=F32, precision=lax.Precision.HIGHEST)
    sw = jnp.dot(sd_ref[...], wf, preferred_element_type=F32, precision=lax.Precision.HIGHEST)
    cw_ref[0] = (cw * scale).astype(BF16)
    sw_ref[0] = (sw * (-scale)).astype(BF16)
    wtop_ref[...] = _dot((wp_ref[0] * ps_ref[0]).astype(BF16), wo_ref[...].astype(BF16)).astype(BF16)


def _mixer_weights(w_fourier, w_pool, pool_scale, w_out, seq_len):
    dh = FOURIER_HEAD_DIM
    assert N_FOURIER_HEADS == N_POOL_GROUPS and dh == POOL_GROUP_DIM
    d = w_out.shape[1]
    cd, sd = _twiddle(np.arange(dh), np.arange(dh), dh)
    scale = 1.0 / np.sqrt(float(seq_len * dh))
    mat = pl.BlockSpec((dh, dh), lambda h: (0, 0))
    per_head = pl.BlockSpec((1, dh, dh), lambda h: (h, 0, 0))
    out_rows = pl.BlockSpec((dh, d), lambda h: (h, 0))
    return pl.pallas_call(
        functools.partial(_mixer_weight_kernel, scale=scale),
        grid=(N_FOURIER_HEADS,),
        in_specs=[mat, mat, per_head, per_head, pl.BlockSpec((1, 1, dh), lambda h: (h, 0, 0)), out_rows],
        out_specs=[per_head, per_head, out_rows],
        out_shape=[jax.ShapeDtypeStruct((N_FOURIER_HEADS, dh, dh), BF16)] * 2
        + [jax.ShapeDtypeStruct((POOL_WIDTH, d), BF16)],
        name="mixer_weights",
    )(jnp.asarray(cd), jnp.asarray(sd), w_fourier, w_pool, pool_scale.reshape(N_POOL_GROUPS, 1, dh), w_out)


class _CastAlong:
    def __init__(self, w, n_chunks, chunk_of):
        cols = w.shape[-1]
        rows = w.size // cols // n_chunks
        self.shape = w.shape
        self.src = w.reshape(n_chunks, rows, cols)
        self.spec = pl.BlockSpec((1, rows, cols), lambda *idx: (chunk_of(*idx), 0, 0))
        self.out_shape = jax.ShapeDtypeStruct((n_chunks, rows, cols), BF16)
        self.vmem_bytes = 2 * rows * cols * (4 + 2)


def _cast_chunks(src_refs, dst_refs):
    for src_ref, dst_ref in zip(src_refs, dst_refs):
        dst_ref[...] = src_ref[...].astype(BF16)


def _norm_proj_kernel(x_ref, g_ref, w_ref, *rest, n_cast):
    cast_src, (u_ref, *cast_dst), w_bf16_ref = rest[:n_cast], rest[n_cast:-1], rest[-1]

    @pl.when(pl.program_id(0) == 0)
    def _():
        for r0 in range(0, w_ref.shape[0], WEIGHT_CAST_ROWS):
            w_bf16_ref[r0:r0 + WEIGHT_CAST_ROWS, :] = w_ref[r0:r0 + WEIGHT_CAST_ROWS, :].astype(BF16)

    x = x_ref[...]
    inv_rms = lax.rsqrt(jnp.mean(x * x, axis=-1, keepdims=True) + RMS_EPS)
    u_ref[...] = (_dot((x * g_ref[...]).astype(BF16), w_bf16_ref[...]) * inv_rms).astype(BF16)
    _cast_chunks(cast_src, cast_dst)


def _norm_proj(x2, g_mix, w_in, cast_weights):
    n, d = x2.shape
    tm = TOKEN_TILE
    casts = [_CastAlong(w, n // tm, lambda i: i) for w in cast_weights]
    vmem = (2 * tm * d * 4 + d * d * (4 + 2) + 2 * tm * d * 2 + 3 * tm * d * 4
            + sum(c.vmem_bytes for c in casts))
    u, *cast_out = pl.pallas_call(
        functools.partial(_norm_proj_kernel, n_cast=len(casts)),
        grid=(n // tm,),
        in_specs=[
            pl.BlockSpec((tm, d), lambda i: (i, 0)),
            pl.BlockSpec((1, d), lambda i: (0, 0)),
            pl.BlockSpec((d, d), lambda i: (0, 0), pipeline_mode=pl.Buffered(1)),
        ] + [c.spec for c in casts],
        out_specs=[pl.BlockSpec((tm, d), lambda i: (i, 0))] + [c.spec for c in casts],
        out_shape=[jax.ShapeDtypeStruct((n, d), BF16)] + [c.out_shape for c in casts],
        scratch_shapes=[pltpu.VMEM((d, d), BF16)],
        compiler_params=pltpu.CompilerParams(
            dimension_semantics=("arbitrary",), vmem_limit_bytes=vmem + 4 * MIB),
        name="norm_proj",
    )(x2, g_mix.reshape(1, d), w_in, *[c.src for c in casts])
    return u, [o.reshape(c.shape) for o, c in zip(cast_out, casts)]


def _fold_kernel(blk_ref, mirror_ref, after_ref, rev_ref, cw_ref, sw_ref, even_ref, odd_ref):
    t = SEQ_TILE
    halo = POOL_HALO
    hd = FOURIER_HEAD_DIM
    for i in range(FOLD_TILES):
        top = (FOLD_TILES - 1 - i) * t
        if i == 0:
            after = jnp.where(pl.program_id(1) > 0, after_ref[...], jnp.zeros_like(after_ref))
        else:
            after = mirror_ref[top + t:top + t + halo, :]
        window = jnp.concatenate([mirror_ref[top:top + t, :], after], axis=0)
        mirrored = _dot(rev_ref[...], window)
        blk = blk_ref[i * t:(i + 1) * t, :].astype(F32)
        even_ref[i * t:(i + 1) * t, :] = (blk + mirrored).astype(BF16)
        odd_ref[i * t:(i + 1) * t, :] = (blk - mirrored).astype(BF16)
    for h in range(N_FOURIER_HEADS):
        cols = slice(h * hd, (h + 1) * hd)
        even_ref[:, cols] = _dot(even_ref[:, cols], cw_ref[h]).astype(BF16)
        odd_ref[:, cols] = _dot(odd_ref[:, cols], sw_ref[h]).astype(BF16)


def _fold_sequence(u3, rev, cw, sw):
    b, s, _ = u3.shape
    halo = POOL_HALO
    rows = FOLD_TILES * SEQ_TILE
    n_blocks = s // rows
    block = lambda block_of: pl.BlockSpec((None, rows, FOURIER_WIDTH), lambda bi, j: (bi, block_of(j), 1))
    after = pl.BlockSpec((None, halo, FOURIER_WIDTH),
                         lambda bi, j: (bi, jnp.minimum((n_blocks - j) * (rows // halo), s // halo - 1), 1))
    out = pl.BlockSpec((None, rows, FOURIER_WIDTH), lambda bi, j: (bi, j, 0))
    return pl.pallas_call(
        _fold_kernel,
        grid=(b, n_blocks // 2),
        in_specs=[block(lambda j: j), block(lambda j: n_blocks - 1 - j), after,
                  pl.BlockSpec(rev.shape, lambda bi, j: (0, 0)),
                  pl.BlockSpec(cw.shape, lambda bi, j: (0, 0, 0)), pl.BlockSpec(sw.shape, lambda bi, j: (0, 0, 0))],
        out_specs=[out, out],
        out_shape=[jax.ShapeDtypeStruct((b, s // 2, FOURIER_WIDTH), BF16)] * 2,
        compiler_params=pltpu.CompilerParams(
            dimension_semantics=("arbitrary", "arbitrary"),
            vmem_limit_bytes=2 * 4 * rows * FOURIER_WIDTH * 2 + 8 * MIB),
        name="fold_sequence",
    )(u3, u3, u3, rev, cw, sw)


def _mix_kernel(even_ref, odd_ref, umid_ref, up_lo_ref, prev_lo_ref, next_lo_ref, up_hi_ref, prev_hi_ref, next_hi_ref,
                c0_ref, s0_ref, cph_ref, sph_ref, band_lo_ref, band_hi_ref, rev_ref,
                cw_ref, cast_src_ref, lo_ref, hi_ref, cast_dst_ref, lhs_ref):
    t = SEQ_TILE
    tp = DFT_ROWS
    gd = POOL_GROUP_DIM
    hd = FOURIER_HEAD_DIM

    m = pl.program_id(1)

    @pl.when(pl.program_id(0) == 0)
    def _():
        cph = cph_ref[0]
        sph = sph_ref[0]
        for r0 in range(0, tp, TWIDDLE_ROWS):
            c0 = c0_ref[r0:r0 + TWIDDLE_ROWS, :]
            s0 = s0_ref[r0:r0 + TWIDDLE_ROWS, :]
            lhs_ref[m, r0:r0 + TWIDDLE_ROWS, :] = (cph * c0 - sph * s0).astype(BF16)
            lhs_ref[m, tp + r0:tp + r0 + TWIDDLE_ROWS, :] = (sph * c0 + cph * s0).astype(BF16)

    _cast_chunks([cast_src_ref], [cast_dst_ref])
    row = lax.broadcasted_iota(jnp.int32, (tp, 1), 0)
    sign = jnp.where(row % 2 == 0, 1.0, -1.0)
    mid = jnp.concatenate([_dot(umid_ref[:, h * hd:(h + 1) * hd], cw_ref[h]) for h in range(N_FOURIER_HEADS)],
                          axis=1)
    pc = _dot(lhs_ref[m, :tp, :], even_ref[...]) + sign * mid[0:1, :]
    qs = _dot(lhs_ref[m, tp:, :], odd_ref[...])
    lo_ref[:, POOL_WIDTH:] = (pc + qs)[:t].astype(BF16)
    hi_ref[:, POOL_WIDTH:] = _dot(rev_ref[...], (pc - qs).astype(BF16)).astype(BF16)

    def pool(up_ref, prev_ref, next_ref, band_ref, out_ref):
        win = jnp.concatenate([prev_ref[...], up_ref[...], next_ref[...]], axis=0)
        for g in range(N_POOL_GROUPS):
            cols = slice(g * gd, (g + 1) * gd)
            out_ref[:, cols] = _dot(band_ref[0, g], win[:, cols]).astype(BF16)

    pool(up_lo_ref, prev_lo_ref, next_lo_ref, band_lo_ref, lo_ref)
    pool(up_hi_ref, prev_hi_ref, next_hi_ref, band_hi_ref, hi_ref)


def _mix(u3, cw, sw, cast_weight):
    b, s, d = u3.shape
    t = SEQ_TILE
    tp = DFT_ROWS
    halo = POOL_HALO
    n_tiles = s // t
    n_steps = n_tiles // 2
    halo_blocks_per_tile = t // halo
    last_halo_block = s // halo - 1

    sh = s // 2
    c0, s0 = _twiddle(np.arange(tp), np.arange(sh), s)
    cph, sph = _twiddle(np.arange(0, sh, t), np.arange(sh), s)
    band = jnp.asarray(_pool_band(s, t, halo)).astype(BF16)
    rev = np.zeros((t, tp), np.float32)
    rev[np.arange(t), t - np.arange(t)] = 1.0
    rev = jnp.asarray(rev).astype(BF16)
    even, odd = _fold_sequence(u3, rev, cw, sw)

    hi_tile = lambda m: n_tiles - 1 - m
    tile_spec = lambda tile_of: pl.BlockSpec((None, t, POOL_WIDTH), lambda bi, m: (bi, tile_of(m), 0))
    prev_spec = lambda tile_of: pl.BlockSpec(
        (None, halo, POOL_WIDTH),
        lambda bi, m: (bi, jnp.maximum(tile_of(m) * halo_blocks_per_tile - 1, 0), 0))
    next_spec = lambda tile_of: pl.BlockSpec(
        (None, halo, POOL_WIDTH),
        lambda bi, m: (bi, jnp.minimum((tile_of(m) + 1) * halo_blocks_per_tile, last_halo_block), 0))
    band_shape = (1, N_POOL_GROUPS, t, t + 2 * halo)
    const2 = lambda shape: pl.BlockSpec(shape, lambda bi, m: (0, 0))
    const3 = lambda shape: pl.BlockSpec(shape, lambda bi, m: (0, 0, 0))
    lo_tile = lambda m: m

    in_specs = [
        pl.BlockSpec((None, sh, FOURIER_WIDTH), lambda bi, m: (bi, 0, 0)),
        pl.BlockSpec((None, sh, FOURIER_WIDTH), lambda bi, m: (bi, 0, 0)),
        pl.BlockSpec((None, halo, FOURIER_WIDTH), lambda bi, m: (bi, sh // halo, 1)),
        tile_spec(lo_tile), prev_spec(lo_tile), next_spec(lo_tile),
        tile_spec(hi_tile), prev_spec(hi_tile), next_spec(hi_tile),
        pl.BlockSpec((tp, sh), lambda bi, m: (0, 0), pipeline_mode=pl.Buffered(1)),
        pl.BlockSpec((tp, sh), lambda bi, m: (0, 0), pipeline_mode=pl.Buffered(1)),
        pl.BlockSpec((1, 1, sh), lambda bi, m: (m, 0, 0)),
        pl.BlockSpec((1, 1, sh), lambda bi, m: (m, 0, 0)),
        pl.BlockSpec(band_shape, lambda bi, m: (jnp.where(m == 0, 0, 1), 0, 0, 0)),
        pl.BlockSpec(band_shape, lambda bi, m: (jnp.where(m == 0, 2, 1), 0, 0, 0)),
        const2((t, tp)),
        const3((N_FOURIER_HEADS, FOURIER_HEAD_DIM, FOURIER_HEAD_DIM)),
    ]
    half_out = pl.BlockSpec((None, t, d), lambda bi, m: (bi, m, 0))
    hi_out = pl.BlockSpec((None, t, d), lambda bi, m: (bi, n_steps - 1 - m, 0))
    cast = _CastAlong(cast_weight, n_steps * b, lambda bi, m: m * b + bi)
    vmem = (2 * 2 * sh * FOURIER_WIDTH * 2
            + 2 * tp * sh * 4
            + n_steps * 2 * tp * sh * 2
            + 2 * 2 * t * d * 2 * 2
            + 4 * 2 * tp * FOURIER_WIDTH * 4
            + cast.vmem_bytes)
    lo, hi, cast_out = pl.pallas_call(
        _mix_kernel,
        grid=(b, n_steps),
        in_specs=in_specs + [cast.spec],
        out_specs=[half_out, hi_out, cast.spec],
        out_shape=[jax.ShapeDtypeStruct((b, s // 2, d), BF16)] * 2 + [cast.out_shape],
        scratch_shapes=[pltpu.VMEM((n_steps, 2 * tp, sh), BF16)],
        compiler_params=pltpu.CompilerParams(
            dimension_semantics=("arbitrary", "arbitrary"), vmem_limit_bytes=vmem + 4 * MIB),
        name="seq_mix",
    )(even, odd, u3, u3, u3, u3, u3, u3, u3, jnp.asarray(c0), jnp.asarray(s0),
      jnp.asarray(cph).reshape(n_steps, 1, sh), jnp.asarray(sph).reshape(n_steps, 1, sh),
      band, band, rev,
      cw, cast.src)
    return lo, hi, cast_out.reshape(cast.shape)


def _out_proj_route_kernel(x_ref, mixed_lo_ref, mixed_hi_ref, wtop_ref, wbot_ref, g_ref, wr_ref, br_ref,
                           tri_ref, pick_ref, cast_src_ref, rows_ref, route_ref, counts_ref, cast_dst_ref,
                           carry_ref, x1_ref, *, tiles_per_seq, n_tiles):
    step = pl.program_id(0)

    @pl.when(step == 0)
    def _():
        carry_ref[...] = jnp.zeros_like(carry_ref)
        x1_ref[...] = jnp.zeros_like(x1_ref)

    _cast_chunks([cast_src_ref], [cast_dst_ref])

    x1 = x1_ref[...]
    rows_ref[:, :D_MODEL] = x1
    h2 = _rmsnorm(x1, g_ref[...])
    logits = _dot(h2.astype(BF16), wr_ref[...]) + br_ref[...]

    tile = jnp.minimum(step, n_tiles - 1)
    in_lo_half = (tile % tiles_per_seq) < tiles_per_seq // 2
    mixed = jnp.where(in_lo_half, mixed_lo_ref[...], mixed_hi_ref[...])
    x1_ref[...] = (x_ref[...] + _dot(mixed[:, :POOL_WIDTH], wtop_ref[...])
                   + _dot(mixed[:, POOL_WIDTH:], wbot_ref[...]))

    lane = lax.broadcasted_iota(jnp.int32, logits.shape, 1)
    neg = jnp.float32(-jnp.inf)
    big = jnp.int32(ROUTER_LANES)

    is_group = lane < N_EXPERT_GROUPS
    gl = jnp.where(is_group, logits, neg)
    gmax = jnp.max(gl, axis=-1, keepdims=True)
    gidx = jnp.min(jnp.where(gl == gmax, lane, big), axis=-1, keepdims=True)
    p_g = 1.0 / jnp.sum(jnp.exp(gl - gmax), axis=-1, keepdims=True)

    e_lane = lane - N_EXPERT_GROUPS
    in_group = (e_lane >= gidx * EXPERTS_PER_GROUP) & (e_lane < (gidx + 1) * EXPERTS_PER_GROUP)
    el = jnp.where(in_group, logits, neg)
    v1 = jnp.max(el, axis=-1, keepdims=True)
    i1 = jnp.min(jnp.where(el == v1, lane, big), axis=-1, keepdims=True)
    el2 = jnp.where(lane == i1, neg, el)
    v2 = jnp.max(el2, axis=-1, keepdims=True)
    i2 = jnp.min(jnp.where(el2 == v2, lane, big), axis=-1, keepdims=True)
    r = jnp.exp(v2 - v1)
    w1 = p_g / (1.0 + r)
    w2 = p_g * r / (1.0 + r)

    first_is_low = i1 < i2
    first_lane = N_EXPERT_GROUPS + gidx * EXPERTS_PER_GROUP
    la = jnp.where(first_is_low, i1, i2) - first_lane
    lb = jnp.where(first_is_low, i2, i1) - first_lane
    w_low = jnp.where(first_is_low, w1, w2)
    w_high = jnp.where(first_is_low, w2, w1)
    assert (PAIR_SLOT_A, PAIR_SLOT_B) == ((0, 0, 0, 1, 1, 3), (1, 2, 3, 3, 2, 2))
    pair = jnp.where(la == 0, lb - 1, jnp.where(la == 1, 6 - lb, 5))
    slot_a_is_high = la == 2
    w_a = jnp.where(slot_a_is_high, w_high, w_low)
    w_b = jnp.where(slot_a_is_high, w_low, w_high)
    cls = gidx * PAIRS_PER_GROUP + pair
    rows_ref[:, D_MODEL:] = jnp.where(lane == 0, w_a, jnp.where(lane == 1, w_b, 0.0))

    onehot = jnp.where(lane == cls, 1.0, 0.0)
    before = _dot(tri_ref[...], onehot.astype(BF16)) + carry_ref[...]
    rank = jnp.sum(jnp.where(lane == cls, before, 0.0), axis=-1, keepdims=True)
    carry_ref[...] += jnp.sum(onehot, axis=0, keepdims=True) * jnp.where(step > 0, 1.0, 0.0)
    counts_ref[...] = carry_ref[...]

    rank_hi = jnp.floor(rank * (1.0 / RANK_RADIX))
    rank_lo = rank - rank_hi * RANK_RADIX
    digits = jnp.where(lane == 0, cls.astype(F32),
                       jnp.where(lane == 1, rank_hi, jnp.where(lane == 2, rank_lo, 0.0)))
    route_ref[...] = lax.dot_general(pick_ref[...], digits.astype(BF16), (((1,), (1,)), ((), ())),
                                     preferred_element_type=F32)


def _out_proj_route(x2, mixed_lo, mixed_hi, seq_len, w_top_bf16, w_out_bf16, g_ffn, wr, br, cast_weight):
    n, d = x2.shape
    tm = TOKEN_TILE
    n_tiles = n // tm
    tiles_per_seq = seq_len // tm
    half_tiles = tiles_per_seq // 2
    const = lambda shape: pl.BlockSpec(shape, lambda i: (0, 0))
    in_tile = lambda i: jnp.minimum(i, n_tiles - 1)
    out_tile = lambda i: jnp.maximum(i - 1, 0)
    lo_spec = pl.BlockSpec((tm, d), lambda i: (
        (in_tile(i) // tiles_per_seq) * half_tiles + jnp.minimum(in_tile(i) % tiles_per_seq, half_tiles - 1), 0))
    hi_spec = pl.BlockSpec((tm, d), lambda i: (
        (in_tile(i) // tiles_per_seq) * half_tiles + jnp.maximum(in_tile(i) % tiles_per_seq - half_tiles, 0), 0))
    tri = np.tril(np.ones((tm, tm), np.float32), -1)
    pick = np.eye(SUBLANES, ROUTER_LANES, dtype=np.float32)
    vmem = (2 * tm * d * 4 + 4 * tm * d * 2 + 2 * tm * ROW_WORDS * 4 + d * d * 2 + tm * d * 4
            + 2 * d * ROUTER_LANES * 2 + 2 * tm * tm * 2 + 3 * tm * d * 4)
    cast = _CastAlong(cast_weight, n_tiles, in_tile)
    vmem += cast.vmem_bytes
    rows, route, counts, cast_out = pl.pallas_call(
        functools.partial(_out_proj_route_kernel, tiles_per_seq=tiles_per_seq, n_tiles=n_tiles),
        grid=(n_tiles + 1,),
        in_specs=[pl.BlockSpec((tm, d), lambda i: (in_tile(i), 0)), lo_spec, hi_spec,
                  pl.BlockSpec((POOL_WIDTH, d), lambda i: (0, 0), pipeline_mode=pl.Buffered(1)),
                  pl.BlockSpec((FOURIER_WIDTH, d), lambda i: (1, 0), pipeline_mode=pl.Buffered(1)),
                  const((1, d)), const((d, ROUTER_LANES)), const((1, ROUTER_LANES)),
                  const((tm, tm)), const((SUBLANES, ROUTER_LANES)), cast.spec],
        out_specs=[pl.BlockSpec((tm, ROW_WORDS), lambda i: (out_tile(i), 0)),
                   pl.BlockSpec((SUBLANES, tm), lambda i: (0, out_tile(i))),
                   const((1, ROUTER_LANES)), cast.spec],
        out_shape=[jax.ShapeDtypeStruct((n, ROW_WORDS), F32),
                   jax.ShapeDtypeStruct((SUBLANES, n), F32),
                   jax.ShapeDtypeStruct((1, ROUTER_LANES), F32), cast.out_shape],
        scratch_shapes=[pltpu.VMEM((1, ROUTER_LANES), F32), pltpu.VMEM((tm, d), F32)],
        compiler_params=pltpu.CompilerParams(
            dimension_semantics=("arbitrary",), vmem_limit_bytes=vmem + 4 * MIB),
        name="out_proj_route",
    )(x2, mixed_lo, mixed_hi, w_top_bf16, w_out_bf16, g_ffn.reshape(1, d), wr, br,
      jnp.asarray(tri).astype(BF16), jnp.asarray(pick).astype(BF16), cast.src)
    return rows, route, counts, cast_out.reshape(cast.shape)


def _row_copy(src_ref, src_row, dst_ref, dst_row, sem):
    return pltpu.make_async_copy(src_ref.at[pl.ds(src_row, 1), :], dst_ref.at[pl.ds(dst_row, 1), :], sem)


def _plan_kernel(cnt_ref, route_ref, tok_ref, start_ref, ea_ref, eb_ref, lo_ref, hi_ref, nused_ref,
                 cstart_ref, pos_vmem_ref, pos_smem_ref, sem):
    tm = EXPERT_TILE
    shift = tm.bit_length() - 1
    n = tok_ref.shape[0]
    n_items = start_ref.shape[0]

    run = jnp.int32(0)
    inside = jnp.bool_(True)
    for c in range(N_CLASSES):
        cnt = cnt_ref[c]
        cstart_ref[c] = run
        inside = inside & ((cnt == 0) | (cnt >= tm))
        run = run + cnt

    cls_row = route_ref[0:1, :]
    slot_row = route_ref[1:2, :] * RANK_RADIX + route_ref[2:3, :]
    for c in range(N_CLASSES):
        slot_row = slot_row + jnp.where(cls_row == c, cstart_ref[c].astype(F32), 0.0)
    pos_vmem_ref[...] = slot_row.astype(jnp.int32)
    to_smem = pltpu.make_async_copy(pos_vmem_ref, pos_smem_ref, sem)
    to_smem.start()

    k = jnp.int32(0)
    for c in range(N_CLASSES):
        cnt = cnt_ref[c]
        cs = cstart_ref[c]
        ce = cs + cnt
        group, pair = divmod(c, PAIRS_PER_GROUP)
        ea = group * EXPERTS_PER_GROUP + PAIR_SLOT_A[pair]
        eb = group * EXPERTS_PER_GROUP + PAIR_SLOT_B[pair]
        first_window = cs >> shift
        n_inside = (cnt + tm - 1) >> shift
        n_aligned = jnp.where(cnt > 0, ((ce - 1) >> shift) - first_window + 1, 0)
        full_windows = cnt >> shift

        def emit(i, carry, k=k, cs=cs, ce=ce, ea=ea, eb=eb, first_window=first_window,
                 full_windows=full_windows):
            start_inside = jnp.where(i < full_windows, cs + i * tm, ce - tm)
            start_aligned = (first_window + i) * tm
            start_ref[k + i] = jnp.where(inside, start_inside, start_aligned)
            ea_ref[k + i] = ea
            eb_ref[k + i] = eb
            lo_ref[k + i] = jnp.where(inside, 0, jnp.clip(cs - start_aligned, 0, tm))
            hi_ref[k + i] = jnp.where(inside, tm, jnp.clip(ce - start_aligned, 0, tm))
            return carry

        n_class_items = jnp.where(inside, n_inside, n_aligned)
        lax.fori_loop(0, n_class_items, emit, 0)
        k = k + n_class_items
    nused_ref[0] = k

    def repeat_last(i, carry):
        for ref in (start_ref, ea_ref, eb_ref, lo_ref, hi_ref):
            ref[i] = ref[k - 1]
        return carry

    lax.fori_loop(k, n_items, repeat_last, 0)

    to_smem.wait()

    def invert(t, carry):
        tok_ref[pos_smem_ref[0, t]] = t
        return carry

    lax.fori_loop(0, n, invert, 0, unroll=INVERT_UNROLL)


def _plan_routing(route, counts, n):
    n_items = n // EXPERT_TILE + N_CLASSES
    smem = lambda size: jax.ShapeDtypeStruct((size,), jnp.int32)
    smem_spec = pl.BlockSpec(memory_space=pltpu.SMEM)
    return pl.pallas_call(
        _plan_kernel,
        in_specs=[smem_spec, pl.BlockSpec(memory_space=pltpu.VMEM)],
        out_specs=[smem_spec] * 7,
        out_shape=[smem(n)] + [smem(n_items)] * 5 + [smem(1)],
        scratch_shapes=[pltpu.SMEM((N_CLASSES,), jnp.int32), pltpu.VMEM((1, n), jnp.int32),
                        pltpu.SMEM((1, n), jnp.int32), pltpu.SemaphoreType.DMA(())],
        name="routing_plan",
    )(counts[0, :N_CLASSES].astype(jnp.int32), route)


def _expert_pair_kernel(start_ref, ea_ref, eb_ref, lo_ref, hi_ref, nused_ref, tok_ref,
                        rows_hbm_ref, gffn_ref, gfin_ref,
                        wga_ref, wua_ref, wda_ref, wgb_ref, wub_ref, wdb_ref, out_hbm_ref,
                        buf_ref, acc_ref, gather_sem, scatter_sem):
    del ea_ref, eb_ref
    tm = EXPERT_TILE
    j = pl.program_id(0)
    last = nused_ref[0] - 1
    slot = j % 2
    other = 1 - slot

    def start_gather(item, dst_slot, rows=range(tm)):
        base = start_ref[item]
        for r in rows:
            _row_copy(rows_hbm_ref, tok_ref[base + r], buf_ref.at[dst_slot], r, gather_sem.at[dst_slot]).start()

    def wait_gather(dst_slot):
        pltpu.make_async_copy(rows_hbm_ref.at[pl.ds(0, tm), :], buf_ref.at[dst_slot],
                              gather_sem.at[dst_slot]).wait()

    def start_scatter(item, src_slot, rows=range(tm)):
        base = start_ref[item]
        for r in rows:
            _row_copy(acc_ref.at[src_slot], r, out_hbm_ref, tok_ref[base + r], scatter_sem).start()

    def wait_scatter():
        pltpu.make_async_copy(acc_ref.at[0], out_hbm_ref.at[pl.ds(0, tm), :], scatter_sem).wait()

    gslot = j % GATHER_SLOTS

    @pl.when(j == 0)
    def _():
        acc_ref[...] = jnp.zeros_like(acc_ref)
        start_gather(0, 0)
        start_gather(jnp.minimum(1, last), 1)
        start_scatter(0, 1)

    @pl.when(j <= last)
    def _():
        wait_scatter()
        wait_gather(gslot)
        prev_item = jnp.maximum(j - 1, 0)
        ahead_item = jnp.minimum(j + 2, last)

        def issue_copies(part):
            half_parts = DMA_PARTS // 2
            rows = range((part % half_parts) * tm // half_parts, (part % half_parts + 1) * tm // half_parts)
            if part < half_parts:
                start_scatter(prev_item, other, rows)
            else:
                start_gather(ahead_item, (j + 2) % GATHER_SLOTS, rows)

        lo = lo_ref[j]
        hi = hi_ref[j]
        x1 = buf_ref[gslot, :, :D_MODEL]
        h = _rmsnorm(x1, gffn_ref[...]).astype(BF16)

        part = 0
        ff_chunk = EXPERT_FF * 4 // DMA_PARTS
        for slot_lane, (wg_ref, wu_ref, wd_ref) in enumerate(((wga_ref, wua_ref, wda_ref),
                                                              (wgb_ref, wub_ref, wdb_ref))):
            for c0 in range(0, EXPERT_FF, ff_chunk):
                issue_copies(part)
                wts = buf_ref[gslot, :, D_MODEL:]
                lane = lax.broadcasted_iota(jnp.int32, wts.shape, 1)
                w = jnp.sum(jnp.where(lane == slot_lane, wts, 0.0), axis=-1, keepdims=True)
                a = _dot(h, wg_ref[0, :, c0:c0 + ff_chunk])
                v = _dot(h, wu_ref[0, :, c0:c0 + ff_chunk])
                act = (a * (1.0 / (1.0 + jnp.exp(-a))) * v * w).astype(BF16)
                for half, d0 in enumerate(range(0, D_MODEL, D_MODEL // 2)):
                    cols = slice(d0, d0 + D_MODEL // 2)
                    if half == 1:
                        issue_copies(part + 1)
                    y_part = _dot(act, wd_ref[0, c0:c0 + ff_chunk, cols])
                    if part == 0:
                        acc_ref[slot, :, cols] = y_part
                    else:
                        acc_ref[slot, :, cols] += y_part
                part += 2
        res = _rmsnorm(buf_ref[gslot, :, :D_MODEL] + acc_ref[slot], gfin_ref[...])

        row = lax.broadcasted_iota(jnp.int32, (tm, 1), 0)
        mine = (row >= lo) & (row < hi)
        acc_ref[slot] = jnp.where(mine, res, 0.0)

        @pl.when(lo > 0)
        def _():
            acc_ref[slot] = jnp.where(row < lo, acc_ref[other], acc_ref[slot])

    @pl.when(j == last)
    def _():
        wait_scatter()
        start_scatter(j, slot)
        wait_scatter()
        wait_gather((j + 1) % GATHER_SLOTS)
        wait_gather((j + 2) % GATHER_SLOTS)


def _expert_pairs(item_start, item_ea, item_eb, item_lo, item_hi, n_used, slot_token, rows,
                  g_ffn, g_final, wg, wu, wd):
    n, w = rows.shape
    d, f = D_MODEL, EXPERT_FF
    tm = EXPERT_TILE
    gate_a = pl.BlockSpec((1, d, f), lambda j, st, ea, eb, lo, hi, nu, tok: (ea[j], 0, 0))
    gate_b = pl.BlockSpec((1, d, f), lambda j, st, ea, eb, lo, hi, nu, tok: (eb[j], 0, 0))
    down_a = pl.BlockSpec((1, f, d), lambda j, st, ea, eb, lo, hi, nu, tok: (ea[j], 0, 0))
    down_b = pl.BlockSpec((1, f, d), lambda j, st, ea, eb, lo, hi, nu, tok: (eb[j], 0, 0))
    gain = pl.BlockSpec((1, d), lambda j, st, ea, eb, lo, hi, nu, tok: (0, 0))
    grid_spec = pltpu.PrefetchScalarGridSpec(
        num_scalar_prefetch=7,
        grid=(item_start.shape[0],),
        in_specs=[pl.BlockSpec(memory_space=pl.ANY), gain, gain,
                  gate_a, gate_a, down_a, gate_b, gate_b, down_b],
        out_specs=pl.BlockSpec(memory_space=pl.ANY),
        scratch_shapes=[pltpu.VMEM((GATHER_SLOTS, tm, w), F32), pltpu.VMEM((2, tm, d), F32),
                        pltpu.SemaphoreType.DMA((GATHER_SLOTS,)), pltpu.SemaphoreType.DMA(())],
    )
    vmem = 2 * 6 * d * f * 2 + GATHER_SLOTS * tm * w * 4 + 2 * tm * d * 4 + 8 * tm * d * 4
    return pl.pallas_call(
        _expert_pair_kernel,
        grid_spec=grid_spec,
        out_shape=jax.ShapeDtypeStruct((n, d), F32),
        compiler_params=pltpu.CompilerParams(
            dimension_semantics=("arbitrary",), vmem_limit_bytes=vmem + 4 * MIB),
        name="expert_pairs",
    )(item_start, item_ea, item_eb, item_lo, item_hi, n_used, slot_token, rows, g_ffn.reshape(1, d),
      g_final.reshape(1, d), wg, wu, wd, wg, wu, wd)


def kernel(x, g_mix, w_in, w_pool, pool_scale, w_fourier, w_out, g_ffn, w_group_router,
           b_group_router, w_expert_router, b_expert_router, w_gate, w_up, w_down, g_final):
    b, s, d = x.shape
    assert d == D_MODEL and s % (2 * FOLD_TILES * SEQ_TILE) == 0 and s % (2 * TOKEN_TILE) == 0
    assert (b * s) % max(TOKEN_TILE, EXPERT_TILE) == 0
    n = b * s
    x2 = x.reshape(n, d)

    cw, sw, w_top_bf16 = _mixer_weights(w_fourier, w_pool, pool_scale, w_out, s)
    u, (w_out_bf16, w_gate_bf16) = _norm_proj(x2, g_mix, w_in, [w_out, w_gate])
    mixed_lo, mixed_hi, w_up_bf16 = _mix(u.reshape(b, s, d), cw, sw, w_up)

    wr = jnp.concatenate([w_group_router, w_expert_router], axis=1)
    wr = jnp.pad(wr, ((0, 0), (0, ROUTER_LANES - wr.shape[1]))).astype(BF16)
    br = jnp.concatenate([b_group_router, b_expert_router])
    br = jnp.pad(br, (0, ROUTER_LANES - br.shape[0])).reshape(1, ROUTER_LANES)

    rows, route, counts, w_down_bf16 = _out_proj_route(
        x2, mixed_lo.reshape(n // 2, d), mixed_hi.reshape(n // 2, d), s, w_top_bf16, w_out_bf16, g_ffn, wr, br,
        w_down)
    slot_token, item_start, item_ea, item_eb, item_lo, item_hi, n_used = _plan_routing(route, counts, n)
    out = _expert_pairs(item_start, item_ea, item_eb, item_lo, item_hi, n_used, slot_token, rows,
                        g_ffn, g_final, w_gate_bf16, w_up_bf16, w_down_bf16)
    return out.reshape(b, s, d)
```
